```python
import math
import jax
import jax.numpy as jnp
from jax import lax
import numpy as np


D_MODEL = 1024
BATCH = 8
SEQ = 8192
DEPTH = 4

PLE_DIM = 256
N_EVEN = (DEPTH + 1) // 2
N_ODD = DEPTH // 2
D_FF = 2816
NORM_EPS = 1e-6
LN_EPS = 1e-5

GM_CHUNK = 128
GM_HEADS = 8
GM_HEAD_DIM = 128
GM_WIDTH = GM_HEADS * GM_HEAD_DIM

SSD_HEAD_DIM = 64
SSD_HEADS = 16
SSD_INNER = SSD_HEADS * SSD_HEAD_DIM
SSD_GROUPS = 2
SSD_STATE = 128
SSD_CONV = 4
SSD_CHUNK = 128
SSD_CONV_CH = SSD_INNER + 2 * SSD_GROUPS * SSD_STATE

HYB_IN = 2 * GM_WIDTH + SSD_INNER + SSD_CONV_CH + SSD_HEADS
HYB_OUT = GM_WIDTH + SSD_INNER

MLA_HEADS = 16
MLA_NOPE = 128
MLA_ROPE = 64
MLA_V = 128
MLA_Q_LORA = 256
MLA_KV_LORA = 128
MLA_QK = MLA_NOPE + MLA_ROPE
MLA_IN = MLA_Q_LORA + MLA_KV_LORA + MLA_ROPE
ROPE_BASE = 10000.0
ATTN_BLOCK = 128

kernel_name = "hybrid_gmlp_ssd_mla_macaron"


def rmsnorm(x, g):
    xf = x.astype(jnp.float32)
    y = xf * lax.rsqrt(jnp.mean(xf * xf, axis=-1, keepdims=True) + NORM_EPS)
    return (y * g.astype(jnp.float32)).astype(x.dtype)


def layernorm(x, g, b):
    xf = x.astype(jnp.float32)
    mu = jnp.mean(xf, axis=-1, keepdims=True)
    xc = xf - mu
    var = jnp.mean(xc * xc, axis=-1, keepdims=True)
    y = xc * lax.rsqrt(var + LN_EPS) * g.astype(jnp.float32) + b.astype(jnp.float32)
    return y.astype(x.dtype)


def swiglu(x, w_in, w_down):
    gate, up = jnp.split(x @ w_in, 2, axis=-1)
    return (jax.nn.silu(gate) * up) @ w_down


def rope_tables(positions):
    inv = 1.0 / (ROPE_BASE ** (jnp.arange(0, MLA_ROPE, 2, dtype=jnp.float32) / MLA_ROPE))
    ang = positions.astype(jnp.float32)[..., None] * inv
    return jnp.cos(ang), jnp.sin(ang)


def apply_rope(x, cos, sin):
    shape = cos.shape[:2] + (1,) * (x.ndim - 3) + cos.shape[2:]
    c = cos.reshape(shape)
    s = sin.reshape(shape)
    x1, x2 = jnp.split(x.astype(jnp.float32), 2, axis=-1)
    return jnp.concatenate([x1 * c - x2 * s, x2 * c + x1 * s], axis=-1).astype(x.dtype)


def gmlp_spatial_gating(uv, ln_g, ln_b, w_s, b_s):
    bsz, seq, _ = uv.shape
    u, v = jnp.split(jax.nn.gelu(uv), 2, axis=-1)
    v = v.reshape(bsz, seq // GM_CHUNK, GM_CHUNK, GM_HEADS, GM_HEAD_DIM)
    v = layernorm(v, ln_g.reshape(GM_HEADS, GM_HEAD_DIM), ln_b.reshape(GM_HEADS, GM_HEAD_DIM))
    causal = jnp.tril(jnp.ones((GM_CHUNK, GM_CHUNK), dtype=bool))
    w = jnp.where(causal[None], w_s, 0).astype(v.dtype)
    mixed = jnp.einsum('hts,bcshd->bcthd', w, v) + b_s.T[None, None, :, :, None]
    return u * mixed.reshape(bsz, seq, GM_WIDTH)


def ssd_chunked_scan(x, dt, a, bmat, cmat):
    bsz, seq, nh, hp = x.shape
    f32 = jnp.float32
    r = nh // SSD_GROUPS
    nc = seq // SSD_CHUNK
    L = SSD_CHUNK
    xd = (x.astype(f32) * dt[..., None]).reshape(bsz, nc, L, SSD_GROUPS, r, hp)
    da = (dt * a).reshape(bsz, nc, L, SSD_GROUPS, r)
    bc = bmat.astype(f32).reshape(bsz, nc, L, SSD_GROUPS, SSD_STATE)
    cc = cmat.astype(f32).reshape(bsz, nc, L, SSD_GROUPS, SSD_STATE)
    a_cs = jnp.cumsum(da, axis=2)
    seg = a_cs[:, :, :, None] - a_cs[:, :, None, :]
    causal = jnp.tril(jnp.ones((L, L), dtype=bool))[:, :, None, None]
    decay = jnp.exp(jnp.where(causal, seg, -jnp.inf))
    cb = jnp.einsum('bclgn,bcsgn->bclsg', cc, bc)
    y_diag = jnp.einsum('bclsgr,bcsgrp->bclgrp', cb[..., None] * decay, xd)
    decay_to_end = jnp.exp(a_cs[:, :, -1:] - a_cs)
    states = jnp.einsum('bclgn,bclgrp->bcgrpn', bc, xd * decay_to_end[..., None])
    chunk_decay = jnp.exp(a_cs[:, :, -1])

    def step(h, inp):
        st, dec = inp
        return h * dec[..., None, None] + st, h

    h0 = jnp.zeros((bsz, SSD_GROUPS, r, hp, SSD_STATE), f32)
    _, prev = lax.scan(step, h0, (jnp.moveaxis(states, 1, 0), jnp.moveaxis(chunk_decay, 1, 0)))
    prev = jnp.moveaxis(prev, 0, 1)
    y_off = jnp.einsum('bclgn,bcgrpn->bclgrp', cc, prev) * jnp.exp(a_cs)[..., None]
    return (y_diag + y_off).reshape(bsz, seq, nh, hp).astype(x.dtype)


def ssd_mixer(zxbcdt, conv_w, conv_b, dt_bias, a_log, d_skip, norm_g):
    bsz, seq, _ = zxbcdt.shape
    z, xbc, dt = jnp.split(zxbcdt, [SSD_INNER, SSD_INNER + SSD_CONV_CH], axis=-1)
    xbc = lax.conv_general_dilated(
        xbc, conv_w[:, None, :], window_strides=(1,), padding=[(SSD_CONV - 1, 0)],
        dimension_numbers=('NWC', 'WIO', 'NWC'), feature_group_count=SSD_CONV_CH) + conv_b
    xbc = jax.nn.silu(xbc)
    xs, bmat, cmat = jnp.split(xbc, [SSD_INNER, SSD_INNER + SSD_GROUPS * SSD_STATE], axis=-1)
    dt = jax.nn.softplus(dt.astype(jnp.float32) + dt_bias.astype(jnp.float32))
    a = -jnp.exp(a_log.astype(jnp.float32))
    xs = xs.reshape(bsz, seq, SSD_HEADS, SSD_HEAD_DIM)
    y = ssd_chunked_scan(xs, dt, a,
                         bmat.reshape(bsz, seq, SSD_GROUPS, SSD_STATE),
                         cmat.reshape(bsz, seq, SSD_GROUPS, SSD_STATE))
    y = (y + d_skip[:, None] * xs).reshape(bsz, seq, SSD_INNER)
    yg = (y * jax.nn.silu(z)).reshape(bsz, seq, SSD_GROUPS, SSD_INNER // SSD_GROUPS)
    yg = rmsnorm(yg, norm_g.reshape(SSD_GROUPS, SSD_INNER // SSD_GROUPS))
    return yg.reshape(bsz, seq, SSD_INNER)


def causal_attention_blocked(q, k, v):
    bsz, seq, nh, dk = q.shape
    nb = seq // ATTN_BLOCK
    scale = dk ** -0.5
    qb = jnp.moveaxis(q.reshape(bsz, nb, ATTN_BLOCK, nh, dk), 1, 0)
    k_pos = jnp.arange(seq)

    def one_block(args):
        qi, blk = args
        s = jnp.einsum('bthd,bshd->bhts', qi, k).astype(jnp.float32) * scale
        q_pos = blk * ATTN_BLOCK + jnp.arange(ATTN_BLOCK)
        s = jnp.where(k_pos[None, :] <= q_pos[:, None], s, -jnp.inf)
        pr = jax.nn.softmax(s, axis=-1).astype(v.dtype)
        return jnp.einsum('bhts,bshd->bthd', pr, v)

    out = lax.map(one_block, (qb, jnp.arange(nb)))
    return jnp.moveaxis(out, 0, 1).reshape(bsz, seq, nh, v.shape[-1])


def mla_attention(h, w_in, q_norm_g, kv_norm_g, w_uq, w_ukv, w_out, cos, sin):
    bsz, seq, _ = h.shape
    c_q, c_kv, k_rope = jnp.split(h @ w_in, [MLA_Q_LORA, MLA_Q_LORA + MLA_KV_LORA], axis=-1)
    q = (rmsnorm(c_q, q_norm_g) @ w_uq).reshape(bsz, seq, MLA_HEADS, MLA_QK)
    q_nope, q_rope = jnp.split(q, [MLA_NOPE], axis=-1)
    q = jnp.concatenate([q_nope, apply_rope(q_rope, cos, sin)], axis=-1)
    kv = (rmsnorm(c_kv, kv_norm_g) @ w_ukv).reshape(bsz, seq, MLA_HEADS, MLA_NOPE + MLA_V)
    k_nope, v = jnp.split(kv, [MLA_NOPE], axis=-1)
    k_rope = apply_rope(k_rope, cos, sin)
    k = jnp.concatenate(
        [k_nope, jnp.broadcast_to(k_rope[:, :, None, :], (bsz, seq, MLA_HEADS, MLA_ROPE))], axis=-1)
    o = causal_attention_blocked(q, k, v)
    return o.reshape(bsz, seq, MLA_HEADS * MLA_V) @ w_out


def _fwd_setup_inputs(seed: int = 0) -> dict:
    key = jax.random.key(seed)
    ks = iter(jax.random.split(key, 48))
    f32 = jnp.float32

    def nrm(shape, scale):
        return jax.random.normal(next(ks), shape, f32) * scale

    def gain(shape):
        return 1.0 + nrm(shape, 0.02)

    x = nrm((BATCH, SEQ, D_MODEL), 1.0)
    p = nrm((DEPTH, BATCH, SEQ, PLE_DIM), 1.0)
    start = jax.random.randint(next(ks), (BATCH, 1), 0, 4096, dtype=jnp.int32)
    positions = start + jnp.arange(SEQ, dtype=jnp.int32)[None, :]

    ffn1_pre_g = gain((DEPTH, D_MODEL))
    ffn1_w_in = nrm((DEPTH, D_MODEL, 2 * D_FF), D_MODEL ** -0.5)
    ffn1_w_down = nrm((DEPTH, D_FF, D_MODEL), D_FF ** -0.5)
    ffn1_post_g = gain((DEPTH, D_MODEL))
    mix_pre_g = gain((DEPTH, D_MODEL))
    mix_post_g = gain((DEPTH, D_MODEL))
    ffn2_pre_g = gain((DEPTH, D_MODEL))
    ffn2_w_in = nrm((DEPTH, D_MODEL, 2 * D_FF), D_MODEL ** -0.5)
    ffn2_w_down = nrm((DEPTH, D_FF, D_MODEL), D_FF ** -0.5)
    ffn2_post_g = gain((DEPTH, D_MODEL))
    ple_pre_g = gain((DEPTH, D_MODEL))
    ple_w_gate = nrm((DEPTH, D_MODEL, D_MODEL), D_MODEL ** -0.5)
    ple_w_proj = nrm((DEPTH, PLE_DIM, D_MODEL), PLE_DIM ** -0.5)
    ple_post_g = gain((DEPTH, D_MODEL))

    hyb_w_in = nrm((N_EVEN, D_MODEL, HYB_IN), D_MODEL ** -0.5)
    gm_ln_g = gain((N_EVEN, GM_WIDTH))
    gm_ln_b = nrm((N_EVEN, GM_WIDTH), 0.02)
    gm_w_s = nrm((N_EVEN, GM_HEADS, GM_CHUNK, GM_CHUNK), GM_CHUNK ** -0.5)
    gm_b_s = gain((N_EVEN, GM_HEADS, GM_CHUNK))
    ssd_conv_w = nrm((N_EVEN, SSD_CONV, SSD_CONV_CH), SSD_CONV ** -0.5)
    ssd_conv_b = nrm((N_EVEN, SSD_CONV_CH), 0.02)
    dt0 = jnp.exp(jax.random.uniform(next(ks), (N_EVEN, SSD_HEADS), f32,
                                     math.log(1e-3), math.log(1e-1)))
    ssd_dt_bias = dt0 + jnp.log(-jnp.expm1(-dt0))
    ssd_a_log = jnp.log(jax.random.uniform(next(ks), (N_EVEN, SSD_HEADS), f32, 1.0, 16.0))
    ssd_d = gain((N_EVEN, SSD_HEADS))
    ssd_norm_g = gain((N_EVEN, SSD_INNER))
    hyb_w_out = nrm((N_EVEN, HYB_OUT, D_MODEL), HYB_OUT ** -0.5)

    mla_w_in = nrm((N_ODD, D_MODEL, MLA_IN), D_MODEL ** -0.5)
    mla_q_norm_g = gain((N_ODD, MLA_Q_LORA))
    mla_kv_norm_g = gain((N_ODD, MLA_KV_LORA))
    mla_w_uq = nrm((N_ODD, MLA_Q_LORA, MLA_HEADS * MLA_QK), MLA_Q_LORA ** -0.5)
    mla_w_ukv = nrm((N_ODD, MLA_KV_LORA, MLA_HEADS * (MLA_NOPE + MLA_V)), MLA_KV_LORA ** -0.5)
    mla_w_out = nrm((N_ODD, MLA_HEADS * MLA_V, D_MODEL), (MLA_HEADS * MLA_V) ** -0.5)

    return {
        "x": x, "p": p, "positions": positions,
        "ffn1_pre_g": ffn1_pre_g, "ffn1_w_in": ffn1_w_in, "ffn1_w_down": ffn1_w_down,
        "ffn1_post_g": ffn1_post_g, "mix_pre_g": mix_pre_g, "mix_post_g": mix_post_g,
        "ffn2_pre_g": ffn2_pre_g, "ffn2_w_in": ffn2_w_in, "ffn2_w_down": ffn2_w_down,
        "ffn2_post_g": ffn2_post_g, "ple_pre_g": ple_pre_g, "ple_w_gate": ple_w_gate,
        "ple_w_proj": ple_w_proj, "ple_post_g": ple_post_g,
        "hyb_w_in": hyb_w_in, "gm_ln_g": gm_ln_g, "gm_ln_b": gm_ln_b, "gm_w_s": gm_w_s,
        "gm_b_s": gm_b_s, "ssd_conv_w": ssd_conv_w, "ssd_conv_b": ssd_conv_b,
        "ssd_dt_bias": ssd_dt_bias, "ssd_a_log": ssd_a_log, "ssd_d": ssd_d,
        "ssd_norm_g": ssd_norm_g, "hyb_w_out": hyb_w_out,
        "mla_w_in": mla_w_in, "mla_q_norm_g": mla_q_norm_g, "mla_kv_norm_g": mla_kv_norm_g,
        "mla_w_uq": mla_w_uq, "mla_w_ukv": mla_w_ukv, "mla_w_out": mla_w_out,
    }


def _fwd_reference(x, p, positions,
              ffn1_pre_g, ffn1_w_in, ffn1_w_down, ffn1_post_g, mix_pre_g, mix_post_g,
              ffn2_pre_g, ffn2_w_in, ffn2_w_down, ffn2_post_g,
              ple_pre_g, ple_w_gate, ple_w_proj, ple_post_g,
              hyb_w_in, gm_ln_g, gm_ln_b, gm_w_s, gm_b_s,
              ssd_conv_w, ssd_conv_b, ssd_dt_bias, ssd_a_log, ssd_d, ssd_norm_g, hyb_w_out,
              mla_w_in, mla_q_norm_g, mla_kv_norm_g, mla_w_uq, mla_w_ukv, mla_w_out):
    cos, sin = rope_tables(positions)
    h = x
    for i in range(DEPTH):
        j = i // 2
        f = swiglu(rmsnorm(h, ffn1_pre_g[i]), ffn1_w_in[i], ffn1_w_down[i])
        h = h + 0.5 * rmsnorm(f, ffn1_post_g[i])
        hn = rmsnorm(h, mix_pre_g[i])
        if i % 2 == 0:
            uv, zxbcdt = jnp.split(hn @ hyb_w_in[j], [2 * GM_WIDTH], axis=-1)
            ya = gmlp_spatial_gating(uv, gm_ln_g[j], gm_ln_b[j], gm_w_s[j], gm_b_s[j])
            yb = ssd_mixer(zxbcdt, ssd_conv_w[j], ssd_conv_b[j], ssd_dt_bias[j],
                           ssd_a_log[j], ssd_d[j], ssd_norm_g[j])
            mixed = jnp.concatenate([ya, yb], axis=-1) @ hyb_w_out[j]
        else:
            mixed = mla_attention(hn, mla_w_in[j], mla_q_norm_g[j], mla_kv_norm_g[j],
                                  mla_w_uq[j], mla_w_ukv[j], mla_w_out[j], cos, sin)
        h = h + rmsnorm(mixed, mix_post_g[i])
        f = swiglu(rmsnorm(h, ffn2_pre_g[i]), ffn2_w_in[i], ffn2_w_down[i])
        h = h + 0.5 * rmsnorm(f, ffn2_post_g[i])
        gate = jax.nn.sigmoid(rmsnorm(h, ple_pre_g[i]) @ ple_w_gate[i])
        h = h + rmsnorm(gate * (p[i] @ ple_w_proj[i]), ple_post_g[i])
    return h


import jax as _jax
import jax.numpy as _jnp

TWIN_FORMAT = 'train_step'
FWD_PARAMS = ['x', 'p', 'positions', 'ffn1_pre_g', 'ffn1_w_in', 'ffn1_w_down', 'ffn1_post_g', 'mix_pre_g', 'mix_post_g', 'ffn2_pre_g', 'ffn2_w_in', 'ffn2_w_down', 'ffn2_post_g', 'ple_pre_g', 'ple_w_gate', 'ple_w_proj', 'ple_post_g', 'hyb_w_in', 'gm_ln_g', 'gm_ln_b', 'gm_w_s', 'gm_b_s', 'ssd_conv_w', 'ssd_conv_b', 'ssd_dt_bias', 'ssd_a_log', 'ssd_d', 'ssd_norm_g', 'hyb_w_out', 'mla_w_in', 'mla_q_norm_g', 'mla_kv_norm_g', 'mla_w_uq', 'mla_w_ukv', 'mla_w_out']
TWIN_WEIGHTS = ['ffn1_pre_g', 'ffn1_w_in', 'ffn1_w_down', 'ffn1_post_g', 'mix_pre_g', 'mix_post_g', 'ffn2_pre_g', 'ffn2_w_in', 'ffn2_w_down', 'ffn2_post_g', 'ple_pre_g', 'ple_w_gate', 'ple_w_proj', 'ple_post_g', 'hyb_w_in', 'gm_ln_g', 'gm_ln_b', 'gm_w_s', 'gm_b_s', 'ssd_conv_w', 'ssd_conv_b', 'ssd_dt_bias', 'ssd_a_log', 'ssd_d', 'ssd_norm_g', 'hyb_w_out', 'mla_w_in', 'mla_q_norm_g', 'mla_kv_norm_g', 'mla_w_uq', 'mla_w_ukv', 'mla_w_out']
TWIN_DIFF_INPUT = 'x'
TWIN_INPUTS = ['x', 'p', 'positions', 'ffn1_pre_g', 'ffn1_w_in', 'ffn1_w_down', 'ffn1_post_g', 'mix_pre_g', 'mix_post_g', 'ffn2_pre_g', 'ffn2_w_in', 'ffn2_w_down', 'ffn2_post_g', 'ple_pre_g', 'ple_w_gate', 'ple_w_proj', 'ple_post_g', 'hyb_w_in', 'gm_ln_g', 'gm_ln_b', 'gm_w_s', 'gm_b_s', 'ssd_conv_w', 'ssd_conv_b', 'ssd_dt_bias', 'ssd_a_log', 'ssd_d', 'ssd_norm_g', 'hyb_w_out', 'mla_w_in', 'mla_q_norm_g', 'mla_kv_norm_g', 'mla_w_uq', 'mla_w_ukv', 'mla_w_out', 'loss_target', 'm_ffn1_pre_g', 'm_ffn1_w_in', 'm_ffn1_w_down', 'm_ffn1_post_g', 'm_mix_pre_g', 'm_mix_post_g', 'm_ffn2_pre_g', 'm_ffn2_w_in', 'm_ffn2_w_down', 'm_ffn2_post_g', 'm_ple_pre_g', 'm_ple_w_gate', 'm_ple_w_proj', 'm_ple_post_g', 'm_hyb_w_in', 'm_gm_ln_g', 'm_gm_ln_b', 'm_gm_w_s', 'm_gm_b_s', 'm_ssd_conv_w', 'm_ssd_conv_b', 'm_ssd_dt_bias', 'm_ssd_a_log', 'm_ssd_d', 'm_ssd_norm_g', 'm_hyb_w_out', 'm_mla_w_in', 'm_mla_q_norm_g', 'm_mla_kv_norm_g', 'm_mla_w_uq', 'm_mla_w_ukv', 'm_mla_w_out', 'v_ffn1_pre_g', 'v_ffn1_w_in', 'v_ffn1_w_down', 'v_ffn1_post_g', 'v_mix_pre_g', 'v_mix_post_g', 'v_ffn2_pre_g', 'v_ffn2_w_in', 'v_ffn2_w_down', 'v_ffn2_post_g', 'v_ple_pre_g', 'v_ple_w_gate', 'v_ple_w_proj', 'v_ple_post_g', 'v_hyb_w_in', 'v_gm_ln_g', 'v_gm_ln_b', 'v_gm_w_s', 'v_gm_b_s', 'v_ssd_conv_w', 'v_ssd_conv_b', 'v_ssd_dt_bias', 'v_ssd_a_log', 'v_ssd_d', 'v_ssd_norm_g', 'v_hyb_w_out', 'v_mla_w_in', 'v_mla_q_norm_g', 'v_mla_kv_norm_g', 'v_mla_w_uq', 'v_mla_w_ukv', 'v_mla_w_out']
TWIN_OUTPUTS = ['loss', 'grad_x', 'grad_ffn1_pre_g', 'grad_ffn1_w_in', 'grad_ffn1_w_down', 'grad_ffn1_post_g', 'grad_mix_pre_g', 'grad_mix_post_g', 'grad_ffn2_pre_g', 'grad_ffn2_w_in', 'grad_ffn2_w_down', 'grad_ffn2_post_g', 'grad_ple_pre_g', 'grad_ple_w_gate', 'grad_ple_w_proj', 'grad_ple_post_g', 'grad_hyb_w_in', 'grad_gm_ln_g', 'grad_gm_ln_b', 'grad_gm_w_s', 'grad_gm_b_s', 'grad_ssd_conv_w', 'grad_ssd_conv_b', 'grad_ssd_dt_bias', 'grad_ssd_a_log', 'grad_ssd_d', 'grad_ssd_norm_g', 'grad_hyb_w_out', 'grad_mla_w_in', 'grad_mla_q_norm_g', 'grad_mla_kv_norm_g', 'grad_mla_w_uq', 'grad_mla_w_ukv', 'grad_mla_w_out', 'delta_ffn1_pre_g', 'delta_ffn1_w_in', 'delta_ffn1_w_down', 'delta_ffn1_post_g', 'delta_mix_pre_g', 'delta_mix_post_g', 'delta_ffn2_pre_g', 'delta_ffn2_w_in', 'delta_ffn2_w_down', 'delta_ffn2_post_g', 'delta_ple_pre_g', 'delta_ple_w_gate', 'delta_ple_w_proj', 'delta_ple_post_g', 'delta_hyb_w_in', 'delta_gm_ln_g', 'delta_gm_ln_b', 'delta_gm_w_s', 'delta_gm_b_s', 'delta_ssd_conv_w', 'delta_ssd_conv_b', 'delta_ssd_dt_bias', 'delta_ssd_a_log', 'delta_ssd_d', 'delta_ssd_norm_g', 'delta_hyb_w_out', 'delta_mla_w_in', 'delta_mla_q_norm_g', 'delta_mla_kv_norm_g', 'delta_mla_w_uq', 'delta_mla_w_ukv', 'delta_mla_w_out', 'new_m_ffn1_pre_g', 'new_m_ffn1_w_in', 'new_m_ffn1_w_down', 'new_m_ffn1_post_g', 'new_m_mix_pre_g', 'new_m_mix_post_g', 'new_m_ffn2_pre_g', 'new_m_ffn2_w_in', 'new_m_ffn2_w_down', 'new_m_ffn2_post_g', 'new_m_ple_pre_g', 'new_m_ple_w_gate', 'new_m_ple_w_proj', 'new_m_ple_post_g', 'new_m_hyb_w_in', 'new_m_gm_ln_g', 'new_m_gm_ln_b', 'new_m_gm_w_s', 'new_m_gm_b_s', 'new_m_ssd_conv_w', 'new_m_ssd_conv_b', 'new_m_ssd_dt_bias', 'new_m_ssd_a_log', 'new_m_ssd_d', 'new_m_ssd_norm_g', 'new_m_hyb_w_out', 'new_m_mla_w_in', 'new_m_mla_q_norm_g', 'new_m_mla_kv_norm_g', 'new_m_mla_w_uq', 'new_m_mla_w_ukv', 'new_m_mla_w_out', 'new_v_ffn1_pre_g', 'new_v_ffn1_w_in', 'new_v_ffn1_w_down', 'new_v_ffn1_post_g', 'new_v_mix_pre_g', 'new_v_mix_post_g', 'new_v_ffn2_pre_g', 'new_v_ffn2_w_in', 'new_v_ffn2_w_down', 'new_v_ffn2_post_g', 'new_v_ple_pre_g', 'new_v_ple_w_gate', 'new_v_ple_w_proj', 'new_v_ple_post_g', 'new_v_hyb_w_in', 'new_v_gm_ln_g', 'new_v_gm_ln_b', 'new_v_gm_w_s', 'new_v_gm_b_s', 'new_v_ssd_conv_w', 'new_v_ssd_conv_b', 'new_v_ssd_dt_bias', 'new_v_ssd_a_log', 'new_v_ssd_d', 'new_v_ssd_norm_g', 'new_v_hyb_w_out', 'new_v_mla_w_in', 'new_v_mla_q_norm_g', 'new_v_mla_kv_norm_g', 'new_v_mla_w_uq', 'new_v_mla_w_ukv', 'new_v_mla_w_out']
TWIN_LEAF_KINDS = {'loss': 'loss', 'grad_x': 'grad_x', 'grad_ffn1_pre_g': 'grad_w', 'grad_ffn1_w_in': 'grad_w', 'grad_ffn1_w_down': 'grad_w', 'grad_ffn1_post_g': 'grad_w', 'grad_mix_pre_g': 'grad_w', 'grad_mix_post_g': 'grad_w', 'grad_ffn2_pre_g': 'grad_w', 'grad_ffn2_w_in': 'grad_w', 'grad_ffn2_w_down': 'grad_w', 'grad_ffn2_post_g': 'grad_w', 'grad_ple_pre_g': 'grad_w', 'grad_ple_w_gate': 'grad_w', 'grad_ple_w_proj': 'grad_w', 'grad_ple_post_g': 'grad_w', 'grad_hyb_w_in': 'grad_w', 'grad_gm_ln_g': 'grad_w', 'grad_gm_ln_b': 'grad_w', 'grad_gm_w_s': 'grad_w', 'grad_gm_b_s': 'grad_w', 'grad_ssd_conv_w': 'grad_w', 'grad_ssd_conv_b': 'grad_w', 'grad_ssd_dt_bias': 'grad_w', 'grad_ssd_a_log': 'grad_w', 'grad_ssd_d': 'grad_w', 'grad_ssd_norm_g': 'grad_w', 'grad_hyb_w_out': 'grad_w', 'grad_mla_w_in': 'grad_w', 'grad_mla_q_norm_g': 'grad_w', 'grad_mla_kv_norm_g': 'grad_w', 'grad_mla_w_uq': 'grad_w', 'grad_mla_w_ukv': 'grad_w', 'grad_mla_w_out': 'grad_w', 'delta_ffn1_pre_g': 'delta_w', 'delta_ffn1_w_in': 'delta_w', 'delta_ffn1_w_down': 'delta_w', 'delta_ffn1_post_g': 'delta_w', 'delta_mix_pre_g': 'delta_w', 'delta_mix_post_g': 'delta_w', 'delta_ffn2_pre_g': 'delta_w', 'delta_ffn2_w_in': 'delta_w', 'delta_ffn2_w_down': 'delta_w', 'delta_ffn2_post_g': 'delta_w', 'delta_ple_pre_g': 'delta_w', 'delta_ple_w_gate': 'delta_w', 'delta_ple_w_proj': 'delta_w', 'delta_ple_post_g': 'delta_w', 'delta_hyb_w_in': 'delta_w', 'delta_gm_ln_g': 'delta_w', 'delta_gm_ln_b': 'delta_w', 'delta_gm_w_s': 'delta_w', 'delta_gm_b_s': 'delta_w', 'delta_ssd_conv_w': 'delta_w', 'delta_ssd_conv_b': 'delta_w', 'delta_ssd_dt_bias': 'delta_w', 'delta_ssd_a_log': 'delta_w', 'delta_ssd_d': 'delta_w', 'delta_ssd_norm_g': 'delta_w', 'delta_hyb_w_out': 'delta_w', 'delta_mla_w_in': 'delta_w', 'delta_mla_q_norm_g': 'delta_w', 'delta_mla_kv_norm_g': 'delta_w', 'delta_mla_w_uq': 'delta_w', 'delta_mla_w_ukv': 'delta_w', 'delta_mla_w_out': 'delta_w', 'new_m_ffn1_pre_g': 'new_m', 'new_m_ffn1_w_in': 'new_m', 'new_m_ffn1_w_down': 'new_m', 'new_m_ffn1_post_g': 'new_m', 'new_m_mix_pre_g': 'new_m', 'new_m_mix_post_g': 'new_m', 'new_m_ffn2_pre_g': 'new_m', 'new_m_ffn2_w_in': 'new_m', 'new_m_ffn2_w_down': 'new_m', 'new_m_ffn2_post_g': 'new_m', 'new_m_ple_pre_g': 'new_m', 'new_m_ple_w_gate': 'new_m', 'new_m_ple_w_proj': 'new_m', 'new_m_ple_post_g': 'new_m', 'new_m_hyb_w_in': 'new_m', 'new_m_gm_ln_g': 'new_m', 'new_m_gm_ln_b': 'new_m', 'new_m_gm_w_s': 'new_m', 'new_m_gm_b_s': 'new_m', 'new_m_ssd_conv_w': 'new_m', 'new_m_ssd_conv_b': 'new_m', 'new_m_ssd_dt_bias': 'new_m', 'new_m_ssd_a_log': 'new_m', 'new_m_ssd_d': 'new_m', 'new_m_ssd_norm_g': 'new_m', 'new_m_hyb_w_out': 'new_m', 'new_m_mla_w_in': 'new_m', 'new_m_mla_q_norm_g': 'new_m', 'new_m_mla_kv_norm_g': 'new_m', 'new_m_mla_w_uq': 'new_m', 'new_m_mla_w_ukv': 'new_m', 'new_m_mla_w_out': 'new_m', 'new_v_ffn1_pre_g': 'new_v', 'new_v_ffn1_w_in': 'new_v', 'new_v_ffn1_w_down': 'new_v', 'new_v_ffn1_post_g': 'new_v', 'new_v_mix_pre_g': 'new_v', 'new_v_mix_post_g': 'new_v', 'new_v_ffn2_pre_g': 'new_v', 'new_v_ffn2_w_in': 'new_v', 'new_v_ffn2_w_down': 'new_v', 'new_v_ffn2_post_g': 'new_v', 'new_v_ple_pre_g': 'new_v', 'new_v_ple_w_gate': 'new_v', 'new_v_ple_w_proj': 'new_v', 'new_v_ple_post_g': 'new_v', 'new_v_hyb_w_in': 'new_v', 'new_v_gm_ln_g': 'new_v', 'new_v_gm_ln_b': 'new_v', 'new_v_gm_w_s': 'new_v', 'new_v_gm_b_s': 'new_v', 'new_v_ssd_conv_w': 'new_v', 'new_v_ssd_conv_b': 'new_v', 'new_v_ssd_dt_bias': 'new_v', 'new_v_ssd_a_log': 'new_v', 'new_v_ssd_d': 'new_v', 'new_v_ssd_norm_g': 'new_v', 'new_v_hyb_w_out': 'new_v', 'new_v_mla_w_in': 'new_v', 'new_v_mla_q_norm_g': 'new_v', 'new_v_mla_kv_norm_g': 'new_v', 'new_v_mla_w_uq': 'new_v', 'new_v_mla_w_ukv': 'new_v', 'new_v_mla_w_out': 'new_v'}


def _forward(args):
    return _fwd_reference(*[args[k] for k in FWD_PARAMS])


def _output_shape():
    def fwd():
        inp = _fwd_setup_inputs(0)
        return _fwd_reference(*[inp[k] for k in FWD_PARAMS])
    out = _jax.eval_shape(fwd)
    return out.shape, out.dtype

N_MICROBATCH = 1
ADAM_LR = 0.001
ADAM_B1 = 0.9
ADAM_B2 = 0.999
ADAM_EPS = 1e-08
ADAM_WD = 0.01
ADAM_STEP = 10
PER_EXAMPLE_BATCH_AXIS = {'x': 0, 'p': 1, 'positions': 0, 'loss_target': 0}
SHARED_INPUTS = []
_WEIGHT_DTYPES = {'ffn1_pre_g': _jnp.float32, 'ffn1_w_in': _jnp.float32, 'ffn1_w_down': _jnp.float32, 'ffn1_post_g': _jnp.float32, 'mix_pre_g': _jnp.float32, 'mix_post_g': _jnp.float32, 'ffn2_pre_g': _jnp.float32, 'ffn2_w_in': _jnp.float32, 'ffn2_w_down': _jnp.float32, 'ffn2_post_g': _jnp.float32, 'ple_pre_g': _jnp.float32, 'ple_w_gate': _jnp.float32, 'ple_w_proj': _jnp.float32, 'ple_post_g': _jnp.float32, 'hyb_w_in': _jnp.float32, 'gm_ln_g': _jnp.float32, 'gm_ln_b': _jnp.float32, 'gm_w_s': _jnp.float32, 'gm_b_s': _jnp.float32, 'ssd_conv_w': _jnp.float32, 'ssd_conv_b': _jnp.float32, 'ssd_dt_bias': _jnp.float32, 'ssd_a_log': _jnp.float32, 'ssd_d': _jnp.float32, 'ssd_norm_g': _jnp.float32, 'hyb_w_out': _jnp.float32, 'mla_w_in': _jnp.float32, 'mla_q_norm_g': _jnp.float32, 'mla_kv_norm_g': _jnp.float32, 'mla_w_uq': _jnp.float32, 'mla_w_ukv': _jnp.float32, 'mla_w_out': _jnp.float32}
MOMENT_SCALE = {'ffn1_pre_g': 9.186609e+00, 'ffn1_w_in': 3.241996e+00, 'ffn1_w_down': 6.847504e+00, 'ffn1_post_g': 1.735271e+01, 'mix_pre_g': 2.966101e+01, 'mix_post_g': 7.891343e+01, 'ffn2_pre_g': 8.811156e+00, 'ffn2_w_in': 3.704382e+00, 'ffn2_w_down': 6.754385e+00, 'ffn2_post_g': 1.767714e+01, 'ple_pre_g': 6.934898e-01, 'ple_w_gate': 6.951990e-01, 'ple_w_proj': 2.234093e+00, 'ple_post_g': 6.581057e+01, 'hyb_w_in': 7.170439e+00, 'gm_ln_g': 9.720316e-01, 'gm_ln_b': 3.433004e+00, 'gm_w_s': 8.682667e-01, 'gm_b_s': 2.477689e+00, 'ssd_conv_w': 1.435671e+01, 'ssd_conv_b': 3.910410e+01, 'ssd_dt_bias': 2.492471e+01, 'ssd_a_log': 8.802828e+01, 'ssd_d': 1.020390e+02, 'ssd_norm_g': 2.150927e+01, 'hyb_w_out': 4.008459e+01, 'mla_w_in': 6.087477e+01, 'mla_q_norm_g': 9.337762e+00, 'mla_kv_norm_g': 1.195704e+02, 'mla_w_uq': 2.836663e+00, 'mla_w_ukv': 2.008224e+01, 'mla_w_out': 3.875825e+01}


def _to_microbatches(a, axis):
    t = _jnp.moveaxis(a, axis, 0)
    t = t.reshape((N_MICROBATCH, t.shape[0] // N_MICROBATCH) + t.shape[1:])
    return _jnp.moveaxis(t, 1, axis + 1)


def setup_inputs(seed: int = 0) -> dict:
    inp = _fwd_setup_inputs(seed)
    key = _jax.random.fold_in(_jax.random.key(seed), 7919)
    shape, _ = _output_shape()
    out = dict(inp)
    out["loss_target"] = _jax.random.normal(_jax.random.fold_in(key, 0), shape, _jnp.float32)
    for i, name in enumerate(TWIN_WEIGHTS):
        w = inp[name].astype(_jnp.float32)
        if MOMENT_SCALE is None:
            s = _jnp.sqrt(_jnp.mean(_jnp.square(w)) + 1e-30)
        else:
            s = MOMENT_SCALE[name]
        km, kv = _jax.random.split(_jax.random.fold_in(key, i + 1))
        out[name] = w
        out["m_" + name] = s * _jax.random.normal(km, w.shape, _jnp.float32)
        out["v_" + name] = (s * s) * _jax.random.uniform(kv, w.shape, _jnp.float32, 0.5, 1.5)
    if N_MICROBATCH > 1:
        for name, axis in PER_EXAMPLE_BATCH_AXIS.items():
            out[name] = _to_microbatches(out[name], axis)
    return {'x': out['x'], 'p': out['p'], 'positions': out['positions'], 'ffn1_pre_g': out['ffn1_pre_g'], 'ffn1_w_in': out['ffn1_w_in'], 'ffn1_w_down': out['ffn1_w_down'], 'ffn1_post_g': out['ffn1_post_g'], 'mix_pre_g': out['mix_pre_g'], 'mix_post_g': out['mix_post_g'], 'ffn2_pre_g': out['ffn2_pre_g'], 'ffn2_w_in': out['ffn2_w_in'], 'ffn2_w_down': out['ffn2_w_down'], 'ffn2_post_g': out['ffn2_post_g'], 'ple_pre_g': out['ple_pre_g'], 'ple_w_gate': out['ple_w_gate'], 'ple_w_proj': out['ple_w_proj'], 'ple_post_g': out['ple_post_g'], 'hyb_w_in': out['hyb_w_in'], 'gm_ln_g': out['gm_ln_g'], 'gm_ln_b': out['gm_ln_b'], 'gm_w_s': out['gm_w_s'], 'gm_b_s': out['gm_b_s'], 'ssd_conv_w': out['ssd_conv_w'], 'ssd_conv_b': out['ssd_conv_b'], 'ssd_dt_bias': out['ssd_dt_bias'], 'ssd_a_log': out['ssd_a_log'], 'ssd_d': out['ssd_d'], 'ssd_norm_g': out['ssd_norm_g'], 'hyb_w_out': out['hyb_w_out'], 'mla_w_in': out['mla_w_in'], 'mla_q_norm_g': out['mla_q_norm_g'], 'mla_kv_norm_g': out['mla_kv_norm_g'], 'mla_w_uq': out['mla_w_uq'], 'mla_w_ukv': out['mla_w_ukv'], 'mla_w_out': out['mla_w_out'], 'loss_target': out['loss_target'], 'm_ffn1_pre_g': out['m_ffn1_pre_g'], 'm_ffn1_w_in': out['m_ffn1_w_in'], 'm_ffn1_w_down': out['m_ffn1_w_down'], 'm_ffn1_post_g': out['m_ffn1_post_g'], 'm_mix_pre_g': out['m_mix_pre_g'], 'm_mix_post_g': out['m_mix_post_g'], 'm_ffn2_pre_g': out['m_ffn2_pre_g'], 'm_ffn2_w_in': out['m_ffn2_w_in'], 'm_ffn2_w_down': out['m_ffn2_w_down'], 'm_ffn2_post_g': out['m_ffn2_post_g'], 'm_ple_pre_g': out['m_ple_pre_g'], 'm_ple_w_gate': out['m_ple_w_gate'], 'm_ple_w_proj': out['m_ple_w_proj'], 'm_ple_post_g': out['m_ple_post_g'], 'm_hyb_w_in': out['m_hyb_w_in'], 'm_gm_ln_g': out['m_gm_ln_g'], 'm_gm_ln_b': out['m_gm_ln_b'], 'm_gm_w_s': out['m_gm_w_s'], 'm_gm_b_s': out['m_gm_b_s'], 'm_ssd_conv_w': out['m_ssd_conv_w'], 'm_ssd_conv_b': out['m_ssd_conv_b'], 'm_ssd_dt_bias': out['m_ssd_dt_bias'], 'm_ssd_a_log': out['m_ssd_a_log'], 'm_ssd_d': out['m_ssd_d'], 'm_ssd_norm_g': out['m_ssd_norm_g'], 'm_hyb_w_out': out['m_hyb_w_out'], 'm_mla_w_in': out['m_mla_w_in'], 'm_mla_q_norm_g': out['m_mla_q_norm_g'], 'm_mla_kv_norm_g': out['m_mla_kv_norm_g'], 'm_mla_w_uq': out['m_mla_w_uq'], 'm_mla_w_ukv': out['m_mla_w_ukv'], 'm_mla_w_out': out['m_mla_w_out'], 'v_ffn1_pre_g': out['v_ffn1_pre_g'], 'v_ffn1_w_in': out['v_ffn1_w_in'], 'v_ffn1_w_down': out['v_ffn1_w_down'], 'v_ffn1_post_g': out['v_ffn1_post_g'], 'v_mix_pre_g': out['v_mix_pre_g'], 'v_mix_post_g': out['v_mix_post_g'], 'v_ffn2_pre_g': out['v_ffn2_pre_g'], 'v_ffn2_w_in': out['v_ffn2_w_in'], 'v_ffn2_w_down': out['v_ffn2_w_down'], 'v_ffn2_post_g': out['v_ffn2_post_g'], 'v_ple_pre_g': out['v_ple_pre_g'], 'v_ple_w_gate': out['v_ple_w_gate'], 'v_ple_w_proj': out['v_ple_w_proj'], 'v_ple_post_g': out['v_ple_post_g'], 'v_hyb_w_in': out['v_hyb_w_in'], 'v_gm_ln_g': out['v_gm_ln_g'], 'v_gm_ln_b': out['v_gm_ln_b'], 'v_gm_w_s': out['v_gm_w_s'], 'v_gm_b_s': out['v_gm_b_s'], 'v_ssd_conv_w': out['v_ssd_conv_w'], 'v_ssd_conv_b': out['v_ssd_conv_b'], 'v_ssd_dt_bias': out['v_ssd_dt_bias'], 'v_ssd_a_log': out['v_ssd_a_log'], 'v_ssd_d': out['v_ssd_d'], 'v_ssd_norm_g': out['v_ssd_norm_g'], 'v_hyb_w_out': out['v_hyb_w_out'], 'v_mla_w_in': out['v_mla_w_in'], 'v_mla_q_norm_g': out['v_mla_q_norm_g'], 'v_mla_kv_norm_g': out['v_mla_kv_norm_g'], 'v_mla_w_uq': out['v_mla_w_uq'], 'v_mla_w_ukv': out['v_mla_w_ukv'], 'v_mla_w_out': out['v_mla_w_out']}


def _loss(weights, diff, rest, loss_target):
    with _jax.named_scope("forward"):
        args = {**rest, TWIN_DIFF_INPUT: diff, **{k: w.astype(_WEIGHT_DTYPES[k]) for k, w in weights.items()}}
        y = _forward(args)
    with _jax.named_scope("loss_head"):
        err = _jnp.square(y.astype(_jnp.float32) - loss_target)
        return 0.5 * _jnp.sum(_jnp.mean(err, axis=-1)) if err.ndim else 0.5 * err


def _adamw(w, g, m, v):
    m = ADAM_B1 * m + (1.0 - ADAM_B1) * g
    v = ADAM_B2 * v + (1.0 - ADAM_B2) * _jnp.square(g)
    m_hat = m / (1.0 - ADAM_B1 ** ADAM_STEP)
    v_hat = v / (1.0 - ADAM_B2 ** ADAM_STEP)
    delta = -ADAM_LR * (m_hat / (_jnp.sqrt(v_hat) + ADAM_EPS) + ADAM_WD * w)
    return delta, m, v


def reference(x, p, positions, ffn1_pre_g, ffn1_w_in, ffn1_w_down, ffn1_post_g, mix_pre_g, mix_post_g, ffn2_pre_g, ffn2_w_in, ffn2_w_down, ffn2_post_g, ple_pre_g, ple_w_gate, ple_w_proj, ple_post_g, hyb_w_in, gm_ln_g, gm_ln_b, gm_w_s, gm_b_s, ssd_conv_w, ssd_conv_b, ssd_dt_bias, ssd_a_log, ssd_d, ssd_norm_g, hyb_w_out, mla_w_in, mla_q_norm_g, mla_kv_norm_g, mla_w_uq, mla_w_ukv, mla_w_out, loss_target, m_ffn1_pre_g, m_ffn1_w_in, m_ffn1_w_down, m_ffn1_post_g, m_mix_pre_g, m_mix_post_g, m_ffn2_pre_g, m_ffn2_w_in, m_ffn2_w_down, m_ffn2_post_g, m_ple_pre_g, m_ple_w_gate, m_ple_w_proj, m_ple_post_g, m_hyb_w_in, m_gm_ln_g, m_gm_ln_b, m_gm_w_s, m_gm_b_s, m_ssd_conv_w, m_ssd_conv_b, m_ssd_dt_bias, m_ssd_a_log, m_ssd_d, m_ssd_norm_g, m_hyb_w_out, m_mla_w_in, m_mla_q_norm_g, m_mla_kv_norm_g, m_mla_w_uq, m_mla_w_ukv, m_mla_w_out, v_ffn1_pre_g, v_ffn1_w_in, v_ffn1_w_down, v_ffn1_post_g, v_mix_pre_g, v_mix_post_g, v_ffn2_pre_g, v_ffn2_w_in, v_ffn2_w_down, v_ffn2_post_g, v_ple_pre_g, v_ple_w_gate, v_ple_w_proj, v_ple_post_g, v_hyb_w_in, v_gm_ln_g, v_gm_ln_b, v_gm_w_s, v_gm_b_s, v_ssd_conv_w, v_ssd_conv_b, v_ssd_dt_bias, v_ssd_a_log, v_ssd_d, v_ssd_norm_g, v_hyb_w_out, v_mla_w_in, v_mla_q_norm_g, v_mla_kv_norm_g, v_mla_w_uq, v_mla_w_ukv, v_mla_w_out):
    given = dict(x=x, p=p, positions=positions, ffn1_pre_g=ffn1_pre_g, ffn1_w_in=ffn1_w_in, ffn1_w_down=ffn1_w_down, ffn1_post_g=ffn1_post_g, mix_pre_g=mix_pre_g, mix_post_g=mix_post_g, ffn2_pre_g=ffn2_pre_g, ffn2_w_in=ffn2_w_in, ffn2_w_down=ffn2_w_down, ffn2_post_g=ffn2_post_g, ple_pre_g=ple_pre_g, ple_w_gate=ple_w_gate, ple_w_proj=ple_w_proj, ple_post_g=ple_post_g, hyb_w_in=hyb_w_in, gm_ln_g=gm_ln_g, gm_ln_b=gm_ln_b, gm_w_s=gm_w_s, gm_b_s=gm_b_s, ssd_conv_w=ssd_conv_w, ssd_conv_b=ssd_conv_b, ssd_dt_bias=ssd_dt_bias, ssd_a_log=ssd_a_log, ssd_d=ssd_d, ssd_norm_g=ssd_norm_g, hyb_w_out=hyb_w_out, mla_w_in=mla_w_in, mla_q_norm_g=mla_q_norm_g, mla_kv_norm_g=mla_kv_norm_g, mla_w_uq=mla_w_uq, mla_w_ukv=mla_w_ukv, mla_w_out=mla_w_out, loss_target=loss_target, m_ffn1_pre_g=m_ffn1_pre_g, m_ffn1_w_in=m_ffn1_w_in, m_ffn1_w_down=m_ffn1_w_down, m_ffn1_post_g=m_ffn1_post_g, m_mix_pre_g=m_mix_pre_g, m_mix_post_g=m_mix_post_g, m_ffn2_pre_g=m_ffn2_pre_g, m_ffn2_w_in=m_ffn2_w_in, m_ffn2_w_down=m_ffn2_w_down, m_ffn2_post_g=m_ffn2_post_g, m_ple_pre_g=m_ple_pre_g, m_ple_w_gate=m_ple_w_gate, m_ple_w_proj=m_ple_w_proj, m_ple_post_g=m_ple_post_g, m_hyb_w_in=m_hyb_w_in, m_gm_ln_g=m_gm_ln_g, m_gm_ln_b=m_gm_ln_b, m_gm_w_s=m_gm_w_s, m_gm_b_s=m_gm_b_s, m_ssd_conv_w=m_ssd_conv_w, m_ssd_conv_b=m_ssd_conv_b, m_ssd_dt_bias=m_ssd_dt_bias, m_ssd_a_log=m_ssd_a_log, m_ssd_d=m_ssd_d, m_ssd_norm_g=m_ssd_norm_g, m_hyb_w_out=m_hyb_w_out, m_mla_w_in=m_mla_w_in, m_mla_q_norm_g=m_mla_q_norm_g, m_mla_kv_norm_g=m_mla_kv_norm_g, m_mla_w_uq=m_mla_w_uq, m_mla_w_ukv=m_mla_w_ukv, m_mla_w_out=m_mla_w_out, v_ffn1_pre_g=v_ffn1_pre_g, v_ffn1_w_in=v_ffn1_w_in, v_ffn1_w_down=v_ffn1_w_down, v_ffn1_post_g=v_ffn1_post_g, v_mix_pre_g=v_mix_pre_g, v_mix_post_g=v_mix_post_g, v_ffn2_pre_g=v_ffn2_pre_g, v_ffn2_w_in=v_ffn2_w_in, v_ffn2_w_down=v_ffn2_w_down, v_ffn2_post_g=v_ffn2_post_g, v_ple_pre_g=v_ple_pre_g, v_ple_w_gate=v_ple_w_gate, v_ple_w_proj=v_ple_w_proj, v_ple_post_g=v_ple_post_g, v_hyb_w_in=v_hyb_w_in, v_gm_ln_g=v_gm_ln_g, v_gm_ln_b=v_gm_ln_b, v_gm_w_s=v_gm_w_s, v_gm_b_s=v_gm_b_s, v_ssd_conv_w=v_ssd_conv_w, v_ssd_conv_b=v_ssd_conv_b, v_ssd_dt_bias=v_ssd_dt_bias, v_ssd_a_log=v_ssd_a_log, v_ssd_d=v_ssd_d, v_ssd_norm_g=v_ssd_norm_g, v_hyb_w_out=v_hyb_w_out, v_mla_w_in=v_mla_w_in, v_mla_q_norm_g=v_mla_q_norm_g, v_mla_kv_norm_g=v_mla_kv_norm_g, v_mla_w_uq=v_mla_w_uq, v_mla_w_ukv=v_mla_w_ukv, v_mla_w_out=v_mla_w_out)
    weights = {n: given[n] for n in TWIN_WEIGHTS}
    shared = {n: given[n] for n in SHARED_INPUTS}
    per_example = {n: given[n] for n in ['x', 'p', 'positions']}
    grad_fn = _jax.value_and_grad(_loss, argnums=(0, 1))

    def one_microbatch(ex, loss_target):
        ex = dict(ex)
        diff = ex.pop(TWIN_DIFF_INPUT)
        return grad_fn(weights, diff, {**shared, **ex}, loss_target)

    if N_MICROBATCH == 1:
        loss, (grad_w, grad_x) = one_microbatch(per_example, given["loss_target"])
    else:
        def body(carry, xs):
            loss_sum, grad_sum = carry
            l_k, (gw_k, gx_k) = one_microbatch(xs[0], xs[1])
            with _jax.named_scope("update"):
                return (loss_sum + l_k, _jax.tree.map(_jnp.add, grad_sum, gw_k)), gx_k

        init = (_jnp.zeros((), _jnp.float32), _jax.tree.map(_jnp.zeros_like, weights))
        (loss, grad_w), grad_x = _jax.lax.scan(body, init, (per_example, given["loss_target"]))
    with _jax.named_scope("update"):
        delta_w, new_m, new_v = {}, {}, {}
        for n in TWIN_WEIGHTS:
            delta_w[n], new_m[n], new_v[n] = _adamw(weights[n], grad_w[n], given["m_" + n], given["v_" + n])
    return (loss, grad_x, *[grad_w[n] for n in TWIN_WEIGHTS], *[delta_w[n] for n in TWIN_WEIGHTS],
            *[new_m[n] for n in TWIN_WEIGHTS], *[new_v[n] for n in TWIN_WEIGHTS])
```

```python
import functools
import math

import jax
import jax.numpy as jnp
import numpy as np
from jax import lax
from jax.experimental import pallas as pl
from jax.experimental.pallas import tpu as pltpu

F32 = jnp.float32
BF16 = jnp.bfloat16
HIGHEST = lax.Precision.HIGHEST

V7X_VMEM_LIMIT_BYTES = 52 * 1024 * 1024
LANES = 128

D_MODEL = 1024
DEPTH = 4
D_FF = 2816
PLE_DIM = 256
NORM_EPS = 1e-6
LN_EPS = 1e-5
CHUNK = 128
GM_HEADS = 8
SSD_HEADS = 16
SSD_HEAD_DIM = 64
SSD_INNER = 1024
SSD_STATE = 128
SSD_BC = 256
SSD_CONV_CH = 1536
HYB_MAIN = 4608
MLA_HEADS = 16
MLA_Q_LORA = 256
MLA_KV_LORA = 128
MLA_ROPE = 64
MLA_IN = 448
MLA_IN_PAD = 512
ATTN_SCALE = 192.0 ** -0.5
ROPE_BASE = 10000.0

ADAM_LR = 0.001
ADAM_B1 = 0.9
ADAM_B2 = 0.999
ADAM_EPS = 1e-08
ADAM_WD = 0.01
ADAM_STEP = 10

N_DEV = 8
PACK_W = 1024
PACK_TM = 256

WEIGHTS = ['ffn1_pre_g', 'ffn1_w_in', 'ffn1_w_down', 'ffn1_post_g', 'mix_pre_g', 'mix_post_g', 'ffn2_pre_g',
           'ffn2_w_in', 'ffn2_w_down', 'ffn2_post_g', 'ple_pre_g', 'ple_w_gate', 'ple_w_proj', 'ple_post_g',
           'hyb_w_in', 'gm_ln_g', 'gm_ln_b', 'gm_w_s', 'gm_b_s', 'ssd_conv_w', 'ssd_conv_b', 'ssd_dt_bias',
           'ssd_a_log', 'ssd_d', 'ssd_norm_g', 'hyb_w_out', 'mla_w_in', 'mla_q_norm_g', 'mla_kv_norm_g',
           'mla_w_uq', 'mla_w_ukv', 'mla_w_out']
SHARD_AXIS = {'ffn1_w_in': 2, 'ffn1_w_down': 1, 'ffn2_w_in': 2, 'ffn2_w_down': 1, 'ple_w_gate': 1, 'ple_w_proj': 2,
              'hyb_w_in': 2, 'ssd_conv_w': 2, 'hyb_w_out': 1, 'mla_w_in': 1, 'mla_q_norm_g': 1, 'mla_w_uq': 2,
              'mla_w_ukv': 2, 'mla_w_out': 1}
SHARDED = [n for n in WEIGHTS if n in SHARD_AXIS]
REPLICATED = [n for n in WEIGHTS if n not in SHARD_AXIS]
SHARDED_F32 = ['ssd_conv_w', 'mla_q_norm_g']
SHARDED_BF16 = [n for n in SHARDED if n not in SHARDED_F32]


def _params(*sem):
    return pltpu.CompilerParams(dimension_semantics=sem or None, vmem_limit_bytes=V7X_VMEM_LIMIT_BYTES)


def _pick(n, prefs):
    for t in prefs:
        if t <= n and n % t == 0:
            return t
    return n


def _mm(a, b, *, ta=False, tb=False, out_dtype=F32, tm=1024, tn=512, tk=512, name):
    m, k = (a.shape[1], a.shape[0]) if ta else a.shape
    n = b.shape[0] if tb else b.shape[1]
    assert k == (b.shape[1] if tb else b.shape[0]), (a.shape, b.shape, ta, tb)
    tm, tn, tk = _pick(m, (tm, 512, 256, 128)), _pick(n, (tn, 512, 256, 128)), _pick(k, (tk, 512, 256, 128))
    nk = k // tk
    dims = (((0 if ta else 1,), (1 if tb else 0,)), ((), ()))

    def body(a_ref, b_ref, o_ref, *acc):
        part = lax.dot_general(a_ref[...].astype(BF16), b_ref[...].astype(BF16), dims, preferred_element_type=F32)
        if nk == 1:
            o_ref[...] = part.astype(o_ref.dtype)
            return
        acc_ref, = acc
        kk = pl.program_id(2)

        @pl.when(kk == 0)
        def _():
            acc_ref[...] = part

        @pl.when(kk > 0)
        def _():
            acc_ref[...] += part

        @pl.when(kk == nk - 1)
        def _():
            o_ref[...] = acc_ref[...].astype(o_ref.dtype)

    a_spec = pl.BlockSpec((tk, tm), lambda i, j, kk: (kk, i)) if ta else pl.BlockSpec((tm, tk), lambda i, j, kk: (i, kk))
    b_spec = pl.BlockSpec((tn, tk), lambda i, j, kk: (j, kk)) if tb else pl.BlockSpec((tk, tn), lambda i, j, kk: (kk, j))
    return pl.pallas_call(
        body, name=name, grid=(m // tm, n // tn, nk), in_specs=[a_spec, b_spec],
        out_specs=pl.BlockSpec((tm, tn), lambda i, j, kk: (i, j)), out_shape=jax.ShapeDtypeStruct((m, n), out_dtype),
        scratch_shapes=[] if nk == 1 else [pltpu.VMEM((tm, tn), F32)],
        compiler_params=_params("parallel", "parallel", "arbitrary"),
    )(a, b)


def _rowwise(fn, rows, consts, outs, accs=(), *, tm=256, name):
    first = rows[0][0] if isinstance(rows[0], tuple) else rows[0]
    t = first.shape[-2]
    tm = _pick(t, (tm, 256, 128, 64, 32, 16, 8))
    n_r, n_c, n_o = len(rows), len(consts), len(outs)

    def body(*refs):
        vals = [r[...] for r in refs[:n_r + n_c]]
        res = fn(*vals)
        res = res if isinstance(res, tuple) else (res,)
        o_refs, a_refs = refs[n_r + n_c:n_r + n_c + n_o], refs[n_r + n_c + n_o:]
        for o_ref, v in zip(o_refs, res[:n_o]):
            if isinstance(v, (tuple, list)):
                off = 0
                for piece in v:
                    o_ref[:, off:off + piece.shape[1]] = piece.astype(o_ref.dtype)
                    off += piece.shape[1]
            else:
                o_ref[...] = v.astype(o_ref.dtype)
        if a_refs:
            terms = res[n_o:]
            is_first = pl.program_id(0) == 0

            @pl.when(is_first)
            def _():
                for a_ref, v in zip(a_refs, terms):
                    a_ref[...] = v

            @pl.when(jnp.logical_not(is_first))
            def _():
                for a_ref, v in zip(a_refs, terms):
                    a_ref[...] += v

    in_specs, args = [], []
    for r in rows:
        if isinstance(r, tuple):
            arr, slot = r
            in_specs.append(pl.BlockSpec((None, tm, arr.shape[2]), functools.partial(lambda i, s: (s, i, 0), s=slot)))
        else:
            arr = r
            in_specs.append(pl.BlockSpec((tm, arr.shape[1]), lambda i: (i, 0)))
        args.append(arr)
    for c in consts:
        in_specs.append(pl.BlockSpec(c.shape, lambda i: (0, 0)))
        args.append(c)
    out_specs = [pl.BlockSpec((tm, c), lambda i: (i, 0)) for c, _ in outs]
    out_shape = [jax.ShapeDtypeStruct((t, c), dt) for c, dt in outs]
    for shp in accs:
        out_specs.append(pl.BlockSpec(shp, lambda i: (0, 0)))
        out_shape.append(jax.ShapeDtypeStruct(shp, F32))
    res = pl.pallas_call(
        body, name=name, grid=(t // tm,), in_specs=in_specs, out_specs=out_specs, out_shape=out_shape,
        compiler_params=_params("arbitrary" if accs else "parallel"),
    )(*args)
    return res[0] if len(res) == 1 else tuple(res)


def _colsum(v):
    return jnp.sum(v, axis=0, keepdims=True)


def _rms(x, g, eps=NORM_EPS):
    r = lax.rsqrt(jnp.mean(x * x, axis=-1, keepdims=True) + eps)
    return x * r * g


def _rms_bwd(x, g, dy, eps=NORM_EPS):
    r = lax.rsqrt(jnp.mean(x * x, axis=-1, keepdims=True) + eps)
    xh = x * r
    dyg = dy * g
    dx = r * (dyg - xh * jnp.mean(dyg * xh, axis=-1, keepdims=True))
    return dx, dy * xh


def _silu(x):
    return x * jax.nn.sigmoid(x)


def _silu_grad(x):
    s = jax.nn.sigmoid(x)
    return s * (1.0 + x * (1.0 - s))


_GELU_K = math.sqrt(2.0 / math.pi)


def _gelu(x):
    return 0.5 * x * (1.0 + jnp.tanh(_GELU_K * (x + 0.044715 * x * x * x)))


def _gelu_grad(x):
    t = jnp.tanh(_GELU_K * (x + 0.044715 * x * x * x))
    return 0.5 * (1.0 + t) + 0.5 * x * (1.0 - t * t) * _GELU_K * (1.0 + 3.0 * 0.044715 * x * x)


def _prenorm(h, g, name):
    return _rowwise(lambda x, gg: _rms(x, gg), [h], [g], [(D_MODEL, BF16)], name=name)


def _postnorm_residual(h, f, g, scale, name):
    return _rowwise(lambda x, ff, gg: x + scale * _rms(ff, gg), [h, f], [g], [(D_MODEL, F32)], name=name)


def _postnorm_bwd(f, dh, g, scale, name):
    def fn(ff, d, gg):
        dx, dgt = _rms_bwd(ff, gg, scale * d)
        return dx, _colsum(dgt)
    return _rowwise(fn, [f, dh], [g], [(D_MODEL, BF16)], [(1, D_MODEL)], name=name)


def _prenorm_bwd(h, das, dh, g, name):
    n = len(das)

    def fn(x, *rest):
        da = rest[0]
        for extra in rest[1:n]:
            da = da + extra
        d, gg = rest[n], rest[n + 1]
        dx, dgt = _rms_bwd(x, gg, da)
        return d + dx, _colsum(dgt)
    return _rowwise(fn, [h, *das, dh], [g], [(D_MODEL, F32)], [(1, D_MODEL)], name=name)


def _ffn_fwd(h, w, pre_g, post_g, tag):
    a = _prenorm(h, pre_g, tag + "_prenorm")
    gu = _mm(a, w["in"], tm=1024, tn=512, tk=1024, name=tag + "_in")
    s = _rowwise(lambda x: _silu(x[:, :D_FF]) * x[:, D_FF:], [gu], [], [(D_FF, BF16)], name=tag + "_swiglu")
    f = _mm(s, w["down"], tm=1024, tn=1024, tk=D_FF, name=tag + "_down")
    out = _postnorm_residual(h, f, post_g, 0.5, tag + "_postnorm")
    return out, (h, a, gu, s, f)


def _ffn_bwd(dh, saved, w, pre_g, post_g, tag):
    h, a, gu, s, f = saved
    df, d_post = _postnorm_bwd(f, dh, post_g, 0.5, tag + "_postnorm_bwd")
    ds = _mm(df, w["down"], tb=True, tm=1024, tn=256, tk=1024, name=tag + "_down_dx")
    d_down = _mm(s, df, ta=True, tm=256, tn=1024, tk=512, name=tag + "_down_dw")

    def swiglu_bwd(x, d):
        gate, up = x[:, :D_FF], x[:, D_FF:]
        return ((d * up * _silu_grad(gate), d * _silu(gate)),)
    dgu = _rowwise(swiglu_bwd, [gu, ds], [], [(2 * D_FF, BF16)], name=tag + "_swiglu_bwd")
    da = _mm(dgu, w["in"], tb=True, tm=1024, tn=1024, tk=512, name=tag + "_in_dx")
    d_in = _mm(a, dgu, ta=True, tm=1024, tn=512, tk=512, name=tag + "_in_dw")
    dh_in, d_pre = _prenorm_bwd(h, [da], dh, pre_g, tag + "_prenorm_bwd")
    return dh_in, dict(w_in=d_in, w_down=d_down, pre_g=d_pre, post_g=d_post)


def _ple_fwd(h, p_i, w, pre_g, post_g):
    a = _prenorm(h, pre_g, "ple_prenorm")
    gl = _mm(a, w["gate"], tm=1024, tn=1024, tk=1024, name="ple_gate")
    e = _mm(p_i, w["proj"], tm=1024, tn=1024, tk=PLE_DIM, name="ple_proj")
    out = _rowwise(lambda x, g_, e_, gg: x + _rms(jax.nn.sigmoid(g_) * e_, gg), [h, gl, e], [post_g],
                   [(D_MODEL, F32)], name="ple_out")
    return out, (h, a, gl, e)


def _ple_bwd(dh, saved, p_i, w, pre_g, post_g):
    h, a, gl, e = saved

    def fn(g_, e_, d, gg):
        sg = jax.nn.sigmoid(g_)
        du, dgt = _rms_bwd(sg * e_, gg, d)
        return du * e_ * sg * (1.0 - sg), du * sg, _colsum(dgt)
    dgl, de, d_post = _rowwise(fn, [gl, e, dh], [post_g], [(D_MODEL, BF16), (D_MODEL, BF16)], [(1, D_MODEL)],
                               name="ple_out_bwd")
    da = _mm(dgl, w["gate"], tb=True, tm=1024, tn=1024, tk=1024, name="ple_gate_dx")
    d_gate = _mm(a, dgl, ta=True, tm=1024, tn=1024, tk=512, name="ple_gate_dw")
    d_proj = _mm(p_i, de, ta=True, tm=PLE_DIM, tn=1024, tk=512, name="ple_proj_dw")
    dh_in, d_pre = _prenorm_bwd(h, [da], dh, pre_g, "ple_prenorm_bwd")
    return dh_in, dict(w_gate=d_gate, w_proj=d_proj, pre_g=d_pre, post_g=d_post)


def _gm_layernorm(v, g, b):
    mu = jnp.mean(v, axis=-1, keepdims=True)
    xc = v - mu
    rstd = lax.rsqrt(jnp.mean(xc * xc, axis=-1, keepdims=True) + LN_EPS)
    vhat = xc * rstd
    return vhat, rstd, vhat * g + b


def _gmlp_fwd(proj, wm, bias_t, ln_g, ln_b, name):
    t = proj.shape[0]

    def body(uv_ref, wm_ref, bt_ref, g_ref, b_ref, o_ref):
        for hd in range(GM_HEADS):
            lo = hd * LANES
            u = _gelu(uv_ref[:, lo:lo + LANES])
            v = _gelu(uv_ref[:, 1024 + lo:1024 + lo + LANES])
            _, _, vln = _gm_layernorm(v, g_ref[:, lo:lo + LANES], b_ref[:, lo:lo + LANES])
            mixed = jnp.dot(wm_ref[hd], vln.astype(BF16), preferred_element_type=F32) + bt_ref[:, hd:hd + 1]
            o_ref[:, lo:lo + LANES] = (u * mixed).astype(o_ref.dtype)

    return pl.pallas_call(
        body, name=name, grid=(t // CHUNK,),
        in_specs=[pl.BlockSpec((CHUNK, 2048), lambda i: (i, 0)), pl.BlockSpec(wm.shape, lambda i: (0, 0, 0)),
                  pl.BlockSpec(bias_t.shape, lambda i: (0, 0)), pl.BlockSpec(ln_g.shape, lambda i: (0, 0)),
                  pl.BlockSpec(ln_b.shape, lambda i: (0, 0))],
        out_specs=pl.BlockSpec((CHUNK, 1024), lambda i: (i, 0)), out_shape=jax.ShapeDtypeStruct((t, 1024), BF16),
        compiler_params=_params("parallel"),
    )(proj, wm, bias_t, ln_g, ln_b)


def _gmlp_bwd(proj, dyab, wm, bias_t, ln_g, ln_b, name):
    t = proj.shape[0]
    nc = t // CHUNK

    def body(uv_ref, dy_ref, wm_ref, bt_ref, g_ref, b_ref, duv_ref, dw_ref, db_ref, dg_ref, dbeta_ref, dbacc):
        c = pl.program_id(0)

        @pl.when(c == 0)
        def _():
            dw_ref[...] = jnp.zeros_like(dw_ref)
            dbacc[...] = jnp.zeros_like(dbacc)
            dg_ref[...] = jnp.zeros_like(dg_ref)
            dbeta_ref[...] = jnp.zeros_like(dbeta_ref)

        for hd in range(GM_HEADS):
            lo = hd * LANES
            xu = uv_ref[:, lo:lo + LANES]
            xv = uv_ref[:, 1024 + lo:1024 + lo + LANES]
            u = _gelu(xu)
            g_h = g_ref[:, lo:lo + LANES]
            vhat, rstd, vln = _gm_layernorm(_gelu(xv), g_h, b_ref[:, lo:lo + LANES])
            vln16 = vln.astype(BF16)
            mixed = jnp.dot(wm_ref[hd], vln16, preferred_element_type=F32) + bt_ref[:, hd:hd + 1]
            dy = dy_ref[:, lo:lo + LANES]
            du = dy * mixed
            dmix = dy * u
            dmix16 = dmix.astype(BF16)
            dw_ref[hd] += lax.dot_general(dmix16, vln16, (((1,), (1,)), ((), ())), preferred_element_type=F32)
            dbacc[hd] += dmix
            dvln = lax.dot_general(wm_ref[hd], dmix16, (((0,), (0,)), ((), ())), preferred_element_type=F32)
            dg_ref[:, lo:lo + LANES] += _colsum(dvln * vhat)
            dbeta_ref[:, lo:lo + LANES] += _colsum(dvln)
            dvh = dvln * g_h
            dv = rstd * (dvh - jnp.mean(dvh, axis=-1, keepdims=True)
                         - vhat * jnp.mean(dvh * vhat, axis=-1, keepdims=True))
            duv_ref[:, lo:lo + LANES] = (du * _gelu_grad(xu)).astype(duv_ref.dtype)
            duv_ref[:, 1024 + lo:1024 + lo + LANES] = (dv * _gelu_grad(xv)).astype(duv_ref.dtype)

        @pl.when(c == nc - 1)
        def _():
            row = lax.broadcasted_iota(jnp.int32, (CHUNK, CHUNK), 0)
            col = lax.broadcasted_iota(jnp.int32, (CHUNK, CHUNK), 1)
            for hd in range(GM_HEADS):
                dw_ref[hd] = jnp.where(col <= row, dw_ref[hd], 0.0)
                db_ref[hd] = jnp.sum(dbacc[hd], axis=1, keepdims=True)

    return pl.pallas_call(
        body, name=name, grid=(nc,),
        in_specs=[pl.BlockSpec((CHUNK, 2048), lambda i: (i, 0)), pl.BlockSpec((CHUNK, 1024), lambda i: (i, 0)),
                  pl.BlockSpec(wm.shape, lambda i: (0, 0, 0)), pl.BlockSpec(bias_t.shape, lambda i: (0, 0)),
                  pl.BlockSpec(ln_g.shape, lambda i: (0, 0)), pl.BlockSpec(ln_b.shape, lambda i: (0, 0))],
        out_specs=[pl.BlockSpec((CHUNK, 2048), lambda i: (i, 0)), pl.BlockSpec((GM_HEADS, CHUNK, CHUNK), lambda i: (0, 0, 0)),
                   pl.BlockSpec((GM_HEADS, CHUNK, 1), lambda i: (0, 0, 0)), pl.BlockSpec((1, 1024), lambda i: (0, 0)),
                   pl.BlockSpec((1, 1024), lambda i: (0, 0))],
        out_shape=[jax.ShapeDtypeStruct((t, 2048), BF16), jax.ShapeDtypeStruct((GM_HEADS, CHUNK, CHUNK), F32),
                   jax.ShapeDtypeStruct((GM_HEADS, CHUNK, 1), F32), jax.ShapeDtypeStruct((1, 1024), F32),
                   jax.ShapeDtypeStruct((1, 1024), F32)],
        scratch_shapes=[pltpu.VMEM((GM_HEADS, CHUNK, CHUNK), F32)],
        compiler_params=_params("arbitrary"),
    )(proj, dyab, wm, bias_t, ln_g, ln_b)


def _ssd_chunk_terms(dt_pad, a_pad):
    row = lax.broadcasted_iota(jnp.int32, (CHUNK, CHUNK), 0)
    col = lax.broadcasted_iota(jnp.int32, (CHUNK, CHUNK), 1)
    tril = jnp.where(col <= row, 1.0, 0.0).astype(F32)
    a_cs = jnp.dot(tril, dt_pad * a_pad, precision=HIGHEST, preferred_element_type=F32)
    return a_cs, a_cs.T


def _pair_cols(mat, hd_a, lane_lt64):
    return jnp.where(lane_lt64, mat[:, hd_a:hd_a + 1], mat[:, hd_a + 1:hd_a + 2])


def _head_decay(a_cs, a_cs_t, hd, causal):
    seg = a_cs[:, hd:hd + 1] - a_cs_t[hd:hd + 1, :]
    return jnp.exp(jnp.where(causal, seg, -jnp.inf))


def _ssd_fwd(act, dt_pad, a_pad, d_pad, name):
    t = act.shape[0]
    nc = t // CHUNK

    def body(act_ref, dt_ref, a_ref, d_ref, y_ref, st_ref, h_sc):
        c = pl.program_id(0)

        @pl.when(c == 0)
        def _():
            h_sc[...] = jnp.zeros_like(h_sc)

        st_ref[...] = h_sc[...]
        row = lax.broadcasted_iota(jnp.int32, (CHUNK, CHUNK), 0)
        col = lax.broadcasted_iota(jnp.int32, (CHUNK, CHUNK), 1)
        causal = col <= row
        lane_lt64 = lax.broadcasted_iota(jnp.int32, (CHUNK, LANES), 1) < SSD_HEAD_DIM
        row_lt64 = lax.broadcasted_iota(jnp.int32, (LANES, 1), 0) < SSD_HEAD_DIM
        dt = dt_ref[...]
        a_cs, a_cs_t = _ssd_chunk_terms(dt, a_ref[...])
        last = a_cs[CHUNK - 1:CHUNK, :]
        for g in range(2):
            b16 = act_ref[:, SSD_INNER + g * SSD_STATE:SSD_INNER + (g + 1) * SSD_STATE].astype(BF16)
            c16 = act_ref[:, SSD_INNER + SSD_BC + g * SSD_STATE:SSD_INNER + SSD_BC + (g + 1) * SSD_STATE].astype(BF16)
            cb = lax.dot_general(c16, b16, (((1,), (1,)), ((), ())), preferred_element_type=F32)
            for pr in range(4):
                ha = g * 8 + pr * 2
                lo = ha * SSD_HEAD_DIM
                xs = act_ref[:, lo:lo + LANES]
                xd = xs * _pair_cols(dt, ha, lane_lt64)
                xd16 = xd.astype(BF16)
                ya = jnp.dot((cb * _head_decay(a_cs, a_cs_t, ha, causal)).astype(BF16), xd16, preferred_element_type=F32)
                yb = jnp.dot((cb * _head_decay(a_cs, a_cs_t, ha + 1, causal)).astype(BF16), xd16, preferred_element_type=F32)
                a_p = _pair_cols(a_cs, ha, lane_lt64)
                hp = h_sc[lo:lo + LANES, :]
                y_off = lax.dot_general(c16, hp.astype(BF16), (((1,), (1,)), ((), ())), preferred_element_type=F32)
                d_p = jnp.where(lane_lt64[:1], d_ref[:, ha:ha + 1], d_ref[:, ha + 1:ha + 2])
                y_ref[:, lo:lo + LANES] = jnp.where(lane_lt64, ya, yb) + y_off * jnp.exp(a_p) + d_p * xs
                last_p = jnp.where(lane_lt64[:1], last[:, ha:ha + 1], last[:, ha + 1:ha + 2])
                xw16 = (xd * jnp.exp(last_p - a_p)).astype(BF16)
                s_new = lax.dot_general(xw16, b16, (((0,), (0,)), ((), ())), preferred_element_type=F32)
                t_col = jnp.where(row_lt64, jnp.exp(last[:, ha:ha + 1]), jnp.exp(last[:, ha + 1:ha + 2]))
                h_sc[lo:lo + LANES, :] = t_col * hp + s_new

    return pl.pallas_call(
        body, name=name, grid=(nc,),
        in_specs=[pl.BlockSpec((CHUNK, SSD_CONV_CH), lambda i: (i, 0)), pl.BlockSpec((CHUNK, LANES), lambda i: (i, 0)),
                  pl.BlockSpec((1, LANES), lambda i: (0, 0)), pl.BlockSpec((1, LANES), lambda i: (0, 0))],
        out_specs=[pl.BlockSpec((CHUNK, SSD_INNER), lambda i: (i, 0)),
                   pl.BlockSpec((None, SSD_INNER, SSD_STATE), lambda i: (i, 0, 0))],
        out_shape=[jax.ShapeDtypeStruct((t, SSD_INNER), F32), jax.ShapeDtypeStruct((nc, SSD_INNER, SSD_STATE), F32)],
        scratch_shapes=[pltpu.VMEM((SSD_INNER, SSD_STATE), F32)],
        compiler_params=_params("arbitrary"),
    )(act, dt_pad, a_pad, d_pad)


def _ssd_bwd(act, dt_pad, a_pad, d_pad, states, dy, name):
    t = act.shape[0]
    nc = t // CHUNK

    def body(act_ref, dt_ref, a_ref, d_ref, st_ref, dy_ref, dact_ref, ddt_ref, da_ref, dd_ref, dh_sc):
        c = pl.program_id(0)

        @pl.when(c == 0)
        def _():
            dh_sc[...] = jnp.zeros_like(dh_sc)
            da_ref[...] = jnp.zeros_like(da_ref)
            dd_ref[...] = jnp.zeros_like(dd_ref)

        row = lax.broadcasted_iota(jnp.int32, (CHUNK, CHUNK), 0)
        col = lax.broadcasted_iota(jnp.int32, (CHUNK, CHUNK), 1)
        causal = col <= row
        lane = lax.broadcasted_iota(jnp.int32, (CHUNK, LANES), 1)
        lane_lt64 = lane < SSD_HEAD_DIM
        row_lt64 = lax.broadcasted_iota(jnp.int32, (LANES, 1), 0) < SSD_HEAD_DIM
        is_last = lax.broadcasted_iota(jnp.int32, (CHUNK, 1), 0) == CHUNK - 1
        dt = dt_ref[...]
        a_cs, a_cs_t = _ssd_chunk_terms(dt, a_ref[...])
        last = a_cs[CHUNK - 1:CHUNK, :]
        d_acs = jnp.zeros((CHUNK, LANES), F32)
        ddt_x = jnp.zeros((CHUNK, LANES), F32)
        dd_acc = jnp.zeros((1, LANES), F32)

        def head_sum(v, first):
            return jnp.sum(jnp.where(lane_lt64 if first else jnp.logical_not(lane_lt64), v, 0.0), axis=1, keepdims=True)

        for g in range(2):
            b_lo = SSD_INNER + g * SSD_STATE
            c_lo = SSD_INNER + SSD_BC + g * SSD_STATE
            b16 = act_ref[:, b_lo:b_lo + SSD_STATE].astype(BF16)
            c16 = act_ref[:, c_lo:c_lo + SSD_STATE].astype(BF16)
            cb = lax.dot_general(c16, b16, (((1,), (1,)), ((), ())), preferred_element_type=F32)
            dcb = jnp.zeros((CHUNK, CHUNK), F32)
            db_g = jnp.zeros((CHUNK, SSD_STATE), F32)
            dc_g = jnp.zeros((CHUNK, SSD_STATE), F32)
            for pr in range(4):
                ha = g * 8 + pr * 2
                lo = ha * SSD_HEAD_DIM
                xs = act_ref[:, lo:lo + LANES]
                dt_p = _pair_cols(dt, ha, lane_lt64)
                xd = xs * dt_p
                xd16 = xd.astype(BF16)
                a_p = _pair_cols(a_cs, ha, lane_lt64)
                exp_a = jnp.exp(a_p)
                last_p = jnp.where(lane_lt64[:1], last[:, ha:ha + 1], last[:, ha + 1:ha + 2])
                w_p = jnp.exp(last_p - a_p)
                hp = st_ref[lo:lo + LANES, :]
                hp16 = hp.astype(BF16)
                dhn = dh_sc[lo:lo + LANES, :]
                dhn16 = dhn.astype(BF16)
                dyp = dy_ref[:, lo:lo + LANES]
                d_p = jnp.where(lane_lt64[:1], d_ref[:, ha:ha + 1], d_ref[:, ha + 1:ha + 2])
                dd_acc = dd_acc + jnp.where(lane[:1] == ha, jnp.sum(head_sum(dyp * xs, True), axis=0, keepdims=True), 0.0) \
                    + jnp.where(lane[:1] == ha + 1, jnp.sum(head_sum(dyp * xs, False), axis=0, keepdims=True), 0.0)
                g_off = lax.dot_general(c16, hp16, (((1,), (1,)), ((), ())), preferred_element_type=F32)
                dg16 = (dyp * exp_a).astype(BF16)
                dc_g = dc_g + jnp.dot(dg16, hp16, preferred_element_type=F32)
                dh_prev = lax.dot_general(dg16, c16, (((0,), (0,)), ((), ())), preferred_element_type=F32)
                off_term = dyp * g_off * exp_a
                q = lax.dot_general(b16, dhn16, (((1,), (1,)), ((), ())), preferred_element_type=F32)
                xw16 = (xd * w_p).astype(BF16)
                db_g = db_g + jnp.dot(xw16, dhn16, preferred_element_type=F32)
                dw_term = xd * q * w_p
                dxd = w_p * q
                dt_all = dhn * hp
                dyp16 = dyp.astype(BF16)
                for k, first in ((0, True), (1, False)):
                    hd = ha + k
                    sel = lane_lt64 if first else jnp.logical_not(lane_lt64)
                    decay = _head_decay(a_cs, a_cs_t, hd, causal)
                    m = cb * decay
                    dy_h = jnp.where(sel, dyp16, jnp.zeros_like(dyp16))
                    dm = lax.dot_general(dy_h, xd16, (((1,), (1,)), ((), ())), preferred_element_type=F32)
                    dcb = dcb + dm * decay
                    dseg = dm * m
                    dxd = dxd + jnp.where(sel, lax.dot_general(m.astype(BF16), dyp16, (((0,), (0,)), ((), ())),
                                                               preferred_element_type=F32), 0.0)
                    d_col = jnp.sum(dseg, axis=1, keepdims=True) - jnp.sum(dseg.T, axis=1, keepdims=True)
                    dw_col = head_sum(dw_term, first)
                    d_col = d_col + head_sum(off_term, first) - dw_col
                    t_h = jnp.exp(last[:, hd:hd + 1])
                    dt_sum = jnp.sum(jnp.sum(jnp.where(row_lt64 if first else jnp.logical_not(row_lt64), dt_all, 0.0),
                                             axis=1, keepdims=True), axis=0, keepdims=True)
                    end_term = jnp.sum(dw_col, axis=0, keepdims=True) + dt_sum * t_h
                    d_col = d_col + jnp.where(is_last, end_term, 0.0)
                    d_acs = d_acs + jnp.where(lane == hd, d_col, 0.0)
                t_col = jnp.where(row_lt64, jnp.exp(last[:, ha:ha + 1]), jnp.exp(last[:, ha + 1:ha + 2]))
                dh_sc[lo:lo + LANES, :] = t_col * dhn + dh_prev
                dact_ref[:, lo:lo + LANES] = d_p * dyp + dxd * dt_p
                ddt_all = dxd * xs
                ddt_x = ddt_x + jnp.where(lane == ha, head_sum(ddt_all, True), 0.0) \
                    + jnp.where(lane == ha + 1, head_sum(ddt_all, False), 0.0)
            dcb16 = dcb.astype(BF16)
            dact_ref[:, b_lo:b_lo + SSD_STATE] = db_g + lax.dot_general(dcb16, c16, (((0,), (0,)), ((), ())),
                                                                          preferred_element_type=F32)
            dact_ref[:, c_lo:c_lo + SSD_STATE] = dc_g + jnp.dot(dcb16, b16, preferred_element_type=F32)
        triu = jnp.where(col >= row, 1.0, 0.0).astype(F32)
        dda = jnp.dot(triu, d_acs, precision=HIGHEST, preferred_element_type=F32)
        ddt_ref[...] = dda * a_ref[...] + ddt_x
        da_ref[...] += _colsum(dda * dt)
        dd_ref[...] += dd_acc

    rev = lambda i: (nc - 1 - i, 0)
    return pl.pallas_call(
        body, name=name, grid=(nc,),
        in_specs=[pl.BlockSpec((CHUNK, SSD_CONV_CH), rev), pl.BlockSpec((CHUNK, LANES), rev),
                  pl.BlockSpec((1, LANES), lambda i: (0, 0)), pl.BlockSpec((1, LANES), lambda i: (0, 0)),
                  pl.BlockSpec((None, SSD_INNER, SSD_STATE), lambda i: (nc - 1 - i, 0, 0)),
                  pl.BlockSpec((CHUNK, SSD_INNER), rev)],
        out_specs=[pl.BlockSpec((CHUNK, SSD_CONV_CH), rev), pl.BlockSpec((CHUNK, LANES), rev),
                   pl.BlockSpec((1, LANES), lambda i: (0, 0)), pl.BlockSpec((1, LANES), lambda i: (0, 0))],
        out_shape=[jax.ShapeDtypeStruct((t, SSD_CONV_CH), F32), jax.ShapeDtypeStruct((t, LANES), F32),
                   jax.ShapeDtypeStruct((1, LANES), F32), jax.ShapeDtypeStruct((1, LANES), F32)],
        scratch_shapes=[pltpu.VMEM((SSD_INNER, SSD_STATE), F32)],
        compiler_params=_params("arbitrary"),
    )(act, dt_pad, a_pad, d_pad, states, dy)


def _shift_down(x, k):
    return x if k == 0 else jnp.pad(x, ((k, 0), (0, 0)))[:x.shape[0]]


def _shift_up(x, k):
    return x if k == 0 else jnp.pad(x, ((0, k), (0, 0)))[k:]


def _conv_pre(x0, x1, x2, x3, w, b):
    return x0 * w[0:1] + x1 * w[1:2] + x2 * w[2:3] + x3 * w[3:4] + b


def _rope128(x, cpad, s_lo, s_hi):
    return x * cpad + pltpu.roll(x, 96, 1) * s_lo + pltpu.roll(x, 32, 1) * s_hi


def _causal_mask(i, j, tq, tk):
    qpos = i * tq + lax.broadcasted_iota(jnp.int32, (tq, tk), 0)
    kpos = j * tk + lax.broadcasted_iota(jnp.int32, (tq, tk), 1)
    return kpos <= qpos


def _attn_scores(qn, qr, kn, kr):
    nt = (((1,), (1,)), ((), ()))
    return (lax.dot_general(qn, kn, nt, preferred_element_type=F32)
            + lax.dot_general(qr, kr, nt, preferred_element_type=F32)) * ATTN_SCALE


def _attn_fwd(qf, kvf, kr, *, tq, name):
    t = qf.shape[0]
    nq = t // tq
    tk = tq

    def body(qn_ref, qr_ref, kn_ref, v_ref, kr_ref, o_ref, lse_ref, m_sc, l_sc, acc_sc):
        i, j = pl.program_id(1), pl.program_id(2)

        @pl.when(j == 0)
        def _():
            m_sc[...] = jnp.full_like(m_sc, -jnp.inf)
            l_sc[...] = jnp.zeros_like(l_sc)
            acc_sc[...] = jnp.zeros_like(acc_sc)

        @pl.when(j <= i)
        def _():
            s = _attn_scores(qn_ref[...], qr_ref[...], kn_ref[...], kr_ref[...])
            s = jnp.where(_causal_mask(i, j, tq, tk), s, -jnp.inf)
            m_prev = m_sc[...]
            m_new = jnp.maximum(m_prev, jnp.max(s, axis=1, keepdims=True))
            p = jnp.exp(s - m_new)
            alpha = jnp.exp(m_prev - m_new)
            l_sc[...] = alpha * l_sc[...] + jnp.sum(p, axis=1, keepdims=True)
            acc_sc[...] = alpha * acc_sc[...] + jnp.dot(p.astype(BF16), v_ref[...], preferred_element_type=F32)
            m_sc[...] = m_new

        @pl.when(j == nq - 1)
        def _():
            o_ref[...] = (acc_sc[...] / l_sc[...]).astype(o_ref.dtype)
            lse_ref[...] = m_sc[...] + jnp.log(l_sc[...])

    kj = lambda i, j: jnp.minimum(i, j)
    return pl.pallas_call(
        body, name=name, grid=(MLA_HEADS, nq, nq),
        in_specs=[pl.BlockSpec((tq, LANES), lambda h, i, j: (i, 2 * h)), pl.BlockSpec((tq, LANES), lambda h, i, j: (i, 2 * h + 1)),
                  pl.BlockSpec((tk, LANES), lambda h, i, j: (kj(i, j), 2 * h)),
                  pl.BlockSpec((tk, LANES), lambda h, i, j: (kj(i, j), 2 * h + 1)),
                  pl.BlockSpec((tk, LANES), lambda h, i, j: (kj(i, j), 0))],
        out_specs=[pl.BlockSpec((tq, LANES), lambda h, i, j: (i, h)), pl.BlockSpec((None, tq, 1), lambda h, i, j: (h, i, 0))],
        out_shape=[jax.ShapeDtypeStruct((t, MLA_HEADS * LANES), BF16), jax.ShapeDtypeStruct((MLA_HEADS, t, 1), F32)],
        scratch_shapes=[pltpu.VMEM((tq, 1), F32), pltpu.VMEM((tq, 1), F32), pltpu.VMEM((tq, LANES), F32)],
        compiler_params=_params("parallel", "parallel", "arbitrary"),
    )(qf, qf, kvf, kvf, kr)


def _attn_bwd_dq(qf, kvf, kr, o, do, lse, *, tq, name):
    t = qf.shape[0]
    nq = t // tq
    tk = tq

    def body(qn_ref, qr_ref, kn_ref, v_ref, kr_ref, o_ref, do_ref, lse_ref, dq_ref, acc_sc, delta_sc):
        i, j = pl.program_id(1), pl.program_id(2)

        @pl.when(j == 0)
        def _():
            acc_sc[...] = jnp.zeros_like(acc_sc)
            delta_sc[...] = jnp.sum(do_ref[...].astype(F32) * o_ref[...].astype(F32), axis=1, keepdims=True)

        @pl.when(j <= i)
        def _():
            s = _attn_scores(qn_ref[...], qr_ref[...], kn_ref[...], kr_ref[...])
            p = jnp.where(_causal_mask(i, j, tq, tk), jnp.exp(s - lse_ref[...]), 0.0)
            dp = lax.dot_general(do_ref[...], v_ref[...], (((1,), (1,)), ((), ())), preferred_element_type=F32)
            ds = (p * (dp - delta_sc[...]) * ATTN_SCALE).astype(BF16)
            acc_sc[:, :LANES] += jnp.dot(ds, kn_ref[...], preferred_element_type=F32)
            acc_sc[:, LANES:] += jnp.dot(ds, kr_ref[...], preferred_element_type=F32)

        @pl.when(j == nq - 1)
        def _():
            dq_ref[...] = acc_sc[...]

    kj = lambda i, j: jnp.minimum(i, j)
    return pl.pallas_call(
        body, name=name, grid=(MLA_HEADS, nq, nq),
        in_specs=[pl.BlockSpec((tq, LANES), lambda h, i, j: (i, 2 * h)), pl.BlockSpec((tq, LANES), lambda h, i, j: (i, 2 * h + 1)),
                  pl.BlockSpec((tk, LANES), lambda h, i, j: (kj(i, j), 2 * h)),
                  pl.BlockSpec((tk, LANES), lambda h, i, j: (kj(i, j), 2 * h + 1)),
                  pl.BlockSpec((tk, LANES), lambda h, i, j: (kj(i, j), 0)),
                  pl.BlockSpec((tq, LANES), lambda h, i, j: (i, h)), pl.BlockSpec((tq, LANES), lambda h, i, j: (i, h)),
                  pl.BlockSpec((None, tq, 1), lambda h, i, j: (h, i, 0))],
        out_specs=pl.BlockSpec((tq, 2 * LANES), lambda h, i, j: (i, h)),
        out_shape=jax.ShapeDtypeStruct((t, MLA_HEADS * 2 * LANES), F32),
        scratch_shapes=[pltpu.VMEM((tq, 2 * LANES), F32), pltpu.VMEM((tq, 1), F32)],
        compiler_params=_params("parallel", "parallel", "arbitrary"),
    )(qf, qf, kvf, kvf, kr, o, do, lse)


def _attn_bwd_dkv(qf, kvf, kr, o, do, lse, *, tq, name):
    t = qf.shape[0]
    nq = t // tq
    tk = tq

    def body(qn_ref, qr_ref, kn_ref, v_ref, kr_ref, o_ref, do_ref, lse_ref, dkv_ref, dkr_ref, dkn_sc, dv_sc, dkr_sc):
        j, i = pl.program_id(1), pl.program_id(2)

        @pl.when(i == 0)
        def _():
            dkn_sc[...] = jnp.zeros_like(dkn_sc)
            dv_sc[...] = jnp.zeros_like(dv_sc)
            dkr_sc[...] = jnp.zeros_like(dkr_sc)

        @pl.when(i >= j)
        def _():
            do_ = do_ref[...]
            delta = jnp.sum(do_.astype(F32) * o_ref[...].astype(F32), axis=1, keepdims=True)
            s = _attn_scores(qn_ref[...], qr_ref[...], kn_ref[...], kr_ref[...])
            p = jnp.where(_causal_mask(i, j, tq, tk), jnp.exp(s - lse_ref[...]), 0.0)
            dp = lax.dot_general(do_, v_ref[...], (((1,), (1,)), ((), ())), preferred_element_type=F32)
            ds = (p * (dp - delta) * ATTN_SCALE).astype(BF16)
            tn = (((0,), (0,)), ((), ()))
            dv_sc[...] += lax.dot_general(p.astype(BF16), do_, tn, preferred_element_type=F32)
            dkn_sc[...] += lax.dot_general(ds, qn_ref[...], tn, preferred_element_type=F32)
            dkr_sc[...] += lax.dot_general(ds, qr_ref[...], tn, preferred_element_type=F32)

        @pl.when(i == nq - 1)
        def _():
            dkv_ref[:, :LANES] = dkn_sc[...].astype(dkv_ref.dtype)
            dkv_ref[:, LANES:] = dv_sc[...].astype(dkv_ref.dtype)
            dkr_ref[...] = dkr_sc[...]

    qi = lambda j, i: jnp.maximum(i, j)
    return pl.pallas_call(
        body, name=name, grid=(MLA_HEADS, nq, nq),
        in_specs=[pl.BlockSpec((tq, LANES), lambda h, j, i: (qi(j, i), 2 * h)),
                  pl.BlockSpec((tq, LANES), lambda h, j, i: (qi(j, i), 2 * h + 1)),
                  pl.BlockSpec((tk, LANES), lambda h, j, i: (j, 2 * h)), pl.BlockSpec((tk, LANES), lambda h, j, i: (j, 2 * h + 1)),
                  pl.BlockSpec((tk, LANES), lambda h, j, i: (j, 0)),
                  pl.BlockSpec((tq, LANES), lambda h, j, i: (qi(j, i), h)), pl.BlockSpec((tq, LANES), lambda h, j, i: (qi(j, i), h)),
                  pl.BlockSpec((None, tq, 1), lambda h, j, i: (h, qi(j, i), 0))],
        out_specs=[pl.BlockSpec((tk, 2 * LANES), lambda h, j, i: (j, h)), pl.BlockSpec((tk, LANES), lambda h, j, i: (j, h))],
        out_shape=[jax.ShapeDtypeStruct((t, MLA_HEADS * 2 * LANES), BF16), jax.ShapeDtypeStruct((t, MLA_HEADS * LANES), F32)],
        scratch_shapes=[pltpu.VMEM((tk, LANES), F32), pltpu.VMEM((tk, LANES), F32), pltpu.VMEM((tk, LANES), F32)],
        compiler_params=_params("parallel", "parallel", "arbitrary"),
    )(qf, qf, kvf, kvf, kr, o, do, lse)


def _rope_tables(positions):
    t = positions.shape[0]
    inv = 1.0 / (ROPE_BASE ** (jnp.arange(0, MLA_ROPE, 2, dtype=F32) / MLA_ROPE))
    ang = positions.astype(F32)[:, None] * inv
    cos, sin = jnp.cos(ang), jnp.sin(ang)
    z32, z64 = jnp.zeros((t, 32), F32), jnp.zeros((t, 64), F32)
    cpad = jnp.concatenate([cos, cos, z64], axis=1)
    s_lo = jnp.concatenate([-sin, z32, z64], axis=1)
    s_hi = jnp.concatenate([z32, sin, z64], axis=1)
    return cpad, s_lo, s_hi


def _mla_fwd(h, w, pre_g, post_g, rope, tq):
    cpad, s_lo, s_hi = rope
    hn = _prenorm(h, pre_g, "mla_prenorm")
    cin = _mm(hn, w["in"], tm=1024, tn=512, tk=1024, name="mla_in")

    def lat(c, cp, sl, sh, qg, kvg):
        cq, ckv, kr = c[:, :MLA_Q_LORA], c[:, MLA_Q_LORA:MLA_Q_LORA + MLA_KV_LORA], c[:, MLA_Q_LORA + MLA_KV_LORA:]
        return _rms(cq, qg), _rms(ckv, kvg), _rope128(kr, cp, sl, sh)
    cqn, ckvn, kr = _rowwise(lat, [cin, cpad, s_lo, s_hi], [w["q_norm_g"], w["kv_norm_g"]],
                             [(MLA_Q_LORA, BF16), (MLA_KV_LORA, BF16), (LANES, BF16)], name="mla_latent")
    q_raw = _mm(cqn, w["uq"], tm=1024, tn=1024, tk=MLA_Q_LORA, name="mla_uq")

    def rope_q(q, cp, sl, sh):
        pieces = []
        for hd in range(MLA_HEADS):
            pieces.append(q[:, 256 * hd:256 * hd + LANES])
            pieces.append(_rope128(q[:, 256 * hd + LANES:256 * hd + 256], cp, sl, sh))
        return (tuple(pieces),)
    qf = _rowwise(rope_q, [q_raw, cpad, s_lo, s_hi], [], [(4096, BF16)], name="mla_rope_q")
    kvf = _mm(ckvn, w["ukv"], out_dtype=BF16, tm=1024, tn=1024, tk=MLA_KV_LORA, name="mla_ukv")
    o, lse = _attn_fwd(qf, kvf, kr, tq=tq, name="mla_attn")
    mixed = _mm(o, w["out"], tm=1024, tn=1024, tk=2048, name="mla_out")
    out = _postnorm_residual(h, mixed, post_g, 1.0, "mla_postnorm")
    return out, (h, hn, cin, cqn, ckvn, kr, qf, kvf, o, lse, mixed)


def _mla_bwd(dh, saved, w, pre_g, post_g, rope, tq):
    cpad, s_lo, s_hi = rope
    h, hn, cin, cqn, ckvn, kr, qf, kvf, o, lse, mixed = saved
    dmixed, d_post = _postnorm_bwd(mixed, dh, post_g, 1.0, "mla_postnorm_bwd")
    do = _mm(dmixed, w["out"], tb=True, out_dtype=BF16, tm=1024, tn=1024, tk=1024, name="mla_out_dx")
    d_out = _mm(o, dmixed, ta=True, tm=1024, tn=1024, tk=512, name="mla_out_dw")
    dq = _attn_bwd_dq(qf, kvf, kr, o, do, lse, tq=tq, name="mla_attn_dq")
    dkvf, dkr_heads = _attn_bwd_dkv(qf, kvf, kr, o, do, lse, tq=tq, name="mla_attn_dkv")

    def unrope_q(d, cp, sl, sh):
        pieces = []
        for hd in range(MLA_HEADS):
            pieces.append(d[:, 256 * hd:256 * hd + LANES])
            pieces.append(_rope128(d[:, 256 * hd + LANES:256 * hd + 256], cp, -sl, -sh))
        return (tuple(pieces),)
    dq_raw = _rowwise(unrope_q, [dq, cpad, s_lo, s_hi], [], [(4096, BF16)], name="mla_rope_q_bwd")
    dcqn = _mm(dq_raw, w["uq"], tb=True, tm=1024, tn=256, tk=1024, name="mla_uq_dx")
    d_uq = _mm(cqn, dq_raw, ta=True, tm=256, tn=1024, tk=512, name="mla_uq_dw")
    dckvn = _mm(dkvf, w["ukv"], tb=True, tm=1024, tn=128, tk=1024, name="mla_ukv_dx")
    d_ukv = _mm(ckvn, dkvf, ta=True, tm=128, tn=1024, tk=512, name="mla_ukv_dw")

    def lat_bwd(c, dq_, dkv_, dkrh, cp, sl, sh, qg, kvg):
        cq, ckv = c[:, :MLA_Q_LORA], c[:, MLA_Q_LORA:MLA_Q_LORA + MLA_KV_LORA]
        dcq, dqg = _rms_bwd(cq, qg, dq_)
        dckv, dkvg = _rms_bwd(ckv, kvg, dkv_)
        dkr = dkrh[:, :LANES]
        for hd in range(1, MLA_HEADS):
            dkr = dkr + dkrh[:, hd * LANES:(hd + 1) * LANES]
        return (dcq, dckv, _rope128(dkr, cp, -sl, -sh)), _colsum(dqg), _colsum(dkvg)
    dcin, d_qg, d_kvg = _rowwise(lat_bwd, [cin, dcqn, dckvn, dkr_heads, cpad, s_lo, s_hi], [w["q_norm_g"], w["kv_norm_g"]],
                                 [(MLA_IN_PAD, BF16)], [(1, MLA_Q_LORA), (1, MLA_KV_LORA)], name="mla_latent_bwd")
    dhn = _mm(dcin, w["in"], tb=True, tm=1024, tn=1024, tk=512, name="mla_in_dx")
    d_in = _mm(hn, dcin, ta=True, tm=1024, tn=512, tk=512, name="mla_in_dw")
    dh_in, d_pre = _prenorm_bwd(h, [dhn], dh, pre_g, "mla_prenorm_bwd")
    return dh_in, dict(w_in=d_in, q_norm_g=d_qg, kv_norm_g=d_kvg, w_uq=d_uq, w_ukv=d_ukv, w_out=d_out,
                       pre_g=d_pre, post_g=d_post)


def _hyb_fwd(h, w, pre_g, post_g):
    hn = _prenorm(h, pre_g, "hyb_prenorm")
    proj = _mm(hn, w["main"], tm=1024, tn=512, tk=1024, name="hyb_in")
    dtr = _mm(hn, w["dt"], tm=1024, tn=LANES, tk=1024, name="hyb_in_dt")
    ya = _gmlp_fwd(proj, w["gm_w"], w["gm_bt"], w["gm_ln_g"], w["gm_ln_b"], "gmlp")
    xbc = proj[:, 3072:]
    xsh = [_shift_down(xbc, 3 - k) for k in range(4)]
    act = _rowwise(lambda x0, x1, x2, x3, cw, cb: _silu(_conv_pre(x0, x1, x2, x3, cw, cb)), xsh,
                   [w["conv_w"], w["conv_b"]], [(SSD_CONV_CH, F32)], name="ssd_conv")
    dt_pad = _rowwise(lambda d, b: jax.nn.softplus(d + b), [dtr], [w["dt_bias"]], [(LANES, F32)], name="ssd_dt")
    y, states = _ssd_fwd(act, dt_pad, w["a"], w["d"], "ssd_scan")

    def gate_norm(y_, p_, ng):
        yg = y_ * _silu(p_[:, 2048:3072])
        return ((_rms(yg[:, :512], ng[:, :512]), _rms(yg[:, 512:], ng[:, 512:])),)
    yb = _rowwise(gate_norm, [y, proj], [w["norm_g"]], [(SSD_INNER, BF16)], name="ssd_gate_norm")
    yab = jnp.concatenate([ya, yb], axis=1)
    mixed = _mm(yab, w["out"], tm=1024, tn=1024, tk=2048, name="hyb_out")
    out = _postnorm_residual(h, mixed, post_g, 1.0, "hyb_postnorm")
    return out, (h, hn, proj, dtr, xsh, act, dt_pad, y, states, yab, mixed)


def _hyb_bwd(dh, saved, w, pre_g, post_g):
    h, hn, proj, dtr, xsh, act, dt_pad, y, states, yab, mixed = saved
    dmixed, d_post = _postnorm_bwd(mixed, dh, post_g, 1.0, "hyb_postnorm_bwd")
    dyab = _mm(dmixed, w["out"], tb=True, tm=1024, tn=1024, tk=1024, name="hyb_out_dx")
    d_out = _mm(yab, dmixed, ta=True, tm=1024, tn=1024, tk=512, name="hyb_out_dw")

    def gate_norm_bwd(y_, p_, d, ng):
        z = p_[:, 2048:3072]
        sz = _silu(z)
        yg = y_ * sz
        d_lo, g_lo = _rms_bwd(yg[:, :512], ng[:, :512], d[:, 1024:1536])
        d_hi, g_hi = _rms_bwd(yg[:, 512:], ng[:, 512:], d[:, 1536:])
        dyg = jnp.concatenate([d_lo, d_hi], axis=1)
        return dyg * sz, dyg * y_ * _silu_grad(z), _colsum(jnp.concatenate([g_lo, g_hi], axis=1))
    dy, dz, d_norm = _rowwise(gate_norm_bwd, [y, proj, dyab], [w["norm_g"]], [(SSD_INNER, F32), (SSD_INNER, BF16)],
                              [(1, SSD_INNER)], name="ssd_gate_norm_bwd")
    dact, ddt, da_sum, dd_sum = _ssd_bwd(act, dt_pad, w["a"], w["d"], states, dy, "ssd_scan_bwd")

    def conv_bwd(x0, x1, x2, x3, da_, cw, cb):
        dpre = da_ * _silu_grad(_conv_pre(x0, x1, x2, x3, cw, cb))
        dw = jnp.concatenate([_colsum(dpre * x0), _colsum(dpre * x1), _colsum(dpre * x2), _colsum(dpre * x3)], axis=0)
        return dpre, dw, _colsum(dpre)
    dconv, d_conv_w, d_conv_b = _rowwise(conv_bwd, [*xsh, dact], [w["conv_w"], w["conv_b"]], [(SSD_CONV_CH, F32)],
                                         [(4, SSD_CONV_CH), (1, SSD_CONV_CH)], name="ssd_conv_bwd")
    dsh = [_shift_up(dconv, 3 - k) for k in range(4)]
    dxbc = _rowwise(lambda d0, d1, d2, d3, cw: d0 * cw[0:1] + d1 * cw[1:2] + d2 * cw[2:3] + d3 * cw[3:4], dsh,
                    [w["conv_w"]], [(SSD_CONV_CH, BF16)], name="ssd_conv_dx")

    def dt_bwd(dd, d, b):
        g = dd * jax.nn.sigmoid(d + b)
        g = jnp.where(lax.broadcasted_iota(jnp.int32, g.shape, 1) < SSD_HEADS, g, 0.0)
        return g, _colsum(g)
    ddtr, d_dt_bias = _rowwise(dt_bwd, [ddt, dtr], [w["dt_bias"]], [(LANES, BF16)], [(1, LANES)], name="ssd_dt_bwd")
    duv, d_gm_w, d_gm_b, d_ln_g, d_ln_b = _gmlp_bwd(proj, dyab, w["gm_w"], w["gm_bt"], w["gm_ln_g"], w["gm_ln_b"],
                                                    "gmlp_bwd")
    dproj = jnp.concatenate([duv, dz, dxbc], axis=1)
    dhn_a = _mm(dproj, w["main"], tb=True, tm=1024, tn=1024, tk=512, name="hyb_in_dx")
    dhn_b = _mm(ddtr, w["dt"], tb=True, tm=1024, tn=1024, tk=LANES, name="hyb_in_dt_dx")
    d_main = _mm(hn, dproj, ta=True, tm=1024, tn=512, tk=512, name="hyb_in_dw")
    d_dt = _mm(hn, ddtr, ta=True, tm=1024, tn=LANES, tk=512, name="hyb_in_dt_dw")
    dh_in, d_pre = _prenorm_bwd(h, [dhn_a, dhn_b], dh, pre_g, "hyb_prenorm_bwd")
    grads = dict(w_in=jnp.concatenate([d_main, d_dt[:, :SSD_HEADS]], axis=1), gm_ln_g=d_ln_g, gm_ln_b=d_ln_b,
                 gm_w_s=d_gm_w, gm_b_s=d_gm_b[:, :, 0], conv_w=d_conv_w, conv_b=d_conv_b,
                 dt_bias=d_dt_bias[:, :SSD_HEADS], a_log=(da_sum * w["a"])[:, :SSD_HEADS], d=dd_sum[:, :SSD_HEADS],
                 norm_g=d_norm, w_out=d_out, pre_g=d_pre, post_g=d_post)
    return dh_in, grads


def _row(v):
    return v.reshape(1, -1).astype(F32)


def _pad_lanes(v, n=LANES):
    v = _row(v)
    return jnp.pad(v, ((0, 0), (0, n - v.shape[1])))


def _layer_weights(fw, sm, i):
    j = i // 2
    lw = dict(
        ffn1=dict({"in": fw["ffn1_w_in"][i], "down": fw["ffn1_w_down"][i]}),
        ffn2=dict({"in": fw["ffn2_w_in"][i], "down": fw["ffn2_w_down"][i]}),
        ple=dict(gate=fw["ple_w_gate"][i], proj=fw["ple_w_proj"][i]),
    )
    if i % 2 == 0:
        w_in = fw["hyb_w_in"][j]
        causal = jnp.tril(jnp.ones((CHUNK, CHUNK), dtype=bool))
        lw["mix"] = {
            "main": w_in[:, :HYB_MAIN], "dt": jnp.pad(w_in[:, HYB_MAIN:], ((0, 0), (0, LANES - SSD_HEADS))),
            "gm_w": jnp.where(causal[None], sm["gm_w_s"][j], 0.0).astype(BF16),
            "gm_bt": jnp.pad(sm["gm_b_s"][j].T, ((0, 0), (0, LANES - GM_HEADS))),
            "gm_ln_g": _row(sm["gm_ln_g"][j]), "gm_ln_b": _row(sm["gm_ln_b"][j]),
            "conv_w": fw["ssd_conv_w"][j], "conv_b": _row(sm["ssd_conv_b"][j]),
            "dt_bias": _pad_lanes(sm["ssd_dt_bias"][j]), "a": _pad_lanes(-jnp.exp(sm["ssd_a_log"][j])),
            "d": _pad_lanes(sm["ssd_d"][j]), "norm_g": _row(sm["ssd_norm_g"][j]), "out": fw["hyb_w_out"][j],
        }
    else:
        uq = fw["mla_w_uq"][j].reshape(MLA_Q_LORA, MLA_HEADS, 192)
        uq = jnp.pad(uq, ((0, 0), (0, 0), (0, 64))).reshape(MLA_Q_LORA, MLA_HEADS * 256)
        lw["mix"] = {
            "in": jnp.pad(fw["mla_w_in"][j], ((0, 0), (0, MLA_IN_PAD - MLA_IN))), "uq": uq, "ukv": fw["mla_w_ukv"][j],
            "out": fw["mla_w_out"][j], "q_norm_g": _row(fw["mla_q_norm_g"][j]), "kv_norm_g": _row(sm["mla_kv_norm_g"][j]),
        }
    return lw


def _device_step(x, p, positions, target, fw, sm):
    t = x.shape[0]
    tq = _pick(t, (512, 256, 128))
    rope = _rope_tables(positions)
    h = x
    saved, lws = [], []
    for i in range(DEPTH):
        lw = _layer_weights(fw, sm, i)
        lws.append(lw)
        h, s1 = _ffn_fwd(h, lw["ffn1"], _row(sm["ffn1_pre_g"][i]), _row(sm["ffn1_post_g"][i]), "ffn")
        if i % 2 == 0:
            h, s2 = _hyb_fwd(h, lw["mix"], _row(sm["mix_pre_g"][i]), _row(sm["mix_post_g"][i]))
        else:
            h, s2 = _mla_fwd(h, lw["mix"], _row(sm["mix_pre_g"][i]), _row(sm["mix_post_g"][i]), rope, tq)
        h, s3 = _ffn_fwd(h, lw["ffn2"], _row(sm["ffn2_pre_g"][i]), _row(sm["ffn2_post_g"][i]), "ffn")
        h, s4 = _ple_fwd(h, p[i], lw["ple"], _row(sm["ple_pre_g"][i]), _row(sm["ple_post_g"][i]))
        saved.append((s1, s2, s3, s4))

    def loss_fn(y, tg):
        err = y - tg
        return err * (1.0 / D_MODEL), jnp.sum(_colsum(err * err), axis=1, keepdims=True)
    dh, loss_sum = _rowwise(loss_fn, [h, target], [], [(D_MODEL, F32)], [(1, 1)], name="loss")
    loss = loss_sum[0, 0] * (0.5 / D_MODEL)

    per_layer = {n: [None] * DEPTH for n in WEIGHTS if n.startswith(("ffn", "mix", "ple"))}
    per_mixer = {n: [None] * (DEPTH // 2) for n in WEIGHTS if n.startswith(("hyb", "gm", "ssd", "mla"))}
    for i in reversed(range(DEPTH)):
        lw = lws[i]
        s1, s2, s3, s4 = saved[i]
        j = i // 2
        dh, g = _ple_bwd(dh, s4, p[i], lw["ple"], _row(sm["ple_pre_g"][i]), _row(sm["ple_post_g"][i]))
        for k, v in g.items():
            per_layer["ple_" + k][i] = v
        dh, g = _ffn_bwd(dh, s3, lw["ffn2"], _row(sm["ffn2_pre_g"][i]), _row(sm["ffn2_post_g"][i]), "ffn")
        for k, v in g.items():
            per_layer["ffn2_" + k][i] = v
        if i % 2 == 0:
            dh, g = _hyb_bwd(dh, s2, lw["mix"], _row(sm["mix_pre_g"][i]), _row(sm["mix_post_g"][i]))
            names = dict(w_in="hyb_w_in", gm_ln_g="gm_ln_g", gm_ln_b="gm_ln_b", gm_w_s="gm_w_s", gm_b_s="gm_b_s",
                         conv_w="ssd_conv_w", conv_b="ssd_conv_b", dt_bias="ssd_dt_bias", a_log="ssd_a_log", d="ssd_d",
                         norm_g="ssd_norm_g", w_out="hyb_w_out")
        else:
            dh, g = _mla_bwd(dh, s2, lw["mix"], _row(sm["mix_pre_g"][i]), _row(sm["mix_post_g"][i]), rope, tq)
            g["w_in"] = g["w_in"][:, :MLA_IN]
            g["w_uq"] = g["w_uq"].reshape(MLA_Q_LORA, MLA_HEADS, 256)[:, :, :192].reshape(MLA_Q_LORA, MLA_HEADS * 192)
            names = dict(w_in="mla_w_in", q_norm_g="mla_q_norm_g", kv_norm_g="mla_kv_norm_g", w_uq="mla_w_uq",
                         w_ukv="mla_w_ukv", w_out="mla_w_out")
        per_layer["mix_pre_g"][i] = g.pop("pre_g")
        per_layer["mix_post_g"][i] = g.pop("post_g")
        for k, v in g.items():
            per_mixer[names[k]][j] = v
        dh, g = _ffn_bwd(dh, s1, lw["ffn1"], _row(sm["ffn1_pre_g"][i]), _row(sm["ffn1_post_g"][i]), "ffn")
        for k, v in g.items():
            per_layer["ffn1_" + k][i] = v

    grads = {}
    for n, parts in {**per_layer, **per_mixer}.items():
        stacked = jnp.stack(parts, axis=0)
        if stacked.ndim == 3 and stacked.shape[1] == 1:
            stacked = stacked[:, 0]
        grads[n] = stacked
    return loss, dh, grads


MESH_AXES = ("x", "y", "c")


def _exchange(src, axes, mode, name):
    n = 2 ** len(axes)
    blk = src.shape[-2:]
    flips = [tuple(a for a, bit in zip(axes, np.binary_repr(f, len(axes))) if bit == "1") for f in range(1, n)]

    def body(src_ref, out_ref, send_sems, recv_sems, local_sem):
        pos = {a: lax.axis_index(a) for a in MESH_AXES}

        def index(where):
            idx = 0
            for a in axes:
                idx = idx * 2 + where[a]
            return idx

        me = index(pos)
        own = src_ref.at[me] if mode == "a2a" else src_ref
        local = pltpu.make_async_copy(own, out_ref.at[me], local_sem)
        local.start()
        copies = []
        for k, flip in enumerate(flips):
            peer = {a: (1 - pos[a]) if a in flip else pos[a] for a in MESH_AXES}
            payload = src_ref.at[index(peer)] if mode == "a2a" else src_ref
            cp = pltpu.make_async_remote_copy(
                src_ref=payload, dst_ref=out_ref.at[me], send_sem=send_sems.at[k], recv_sem=recv_sems.at[k],
                device_id=(peer["x"], peer["y"], peer["c"]), device_id_type=pl.DeviceIdType.MESH)
            cp.start()
            copies.append(cp)
        for cp in copies:
            cp.wait()
        local.wait()

    return pl.pallas_call(
        body, name=name, in_specs=[pl.BlockSpec(memory_space=pl.ANY)], out_specs=pl.BlockSpec(memory_space=pl.ANY),
        out_shape=jax.ShapeDtypeStruct((n, *blk), src.dtype),
        scratch_shapes=[pltpu.SemaphoreType.DMA((n - 1,)), pltpu.SemaphoreType.DMA((n - 1,)), pltpu.SemaphoreType.DMA],
    )(src)


def _pack_rows(n_elems):
    return -(-n_elems // PACK_W)


def _pack(parts, lead=()):
    nl = len(lead)
    rows = []
    for a in parts:
        flat = a.reshape(*lead, -1)
        r = _pack_rows(flat.shape[-1])
        flat = jnp.pad(flat, [(0, 0)] * nl + [(0, r * PACK_W - flat.shape[-1])])
        rows.append(flat.reshape(*lead, r, PACK_W))
    total = sum(r.shape[nl] for r in rows)
    pad = -total % PACK_TM
    if pad:
        rows.append(jnp.zeros((*lead, pad, PACK_W), rows[0].dtype))
    return jnp.concatenate(rows, axis=nl)


def _unpack(buf, shapes, lead=()):
    nl = len(lead)
    out, r0 = [], 0
    for shp in shapes:
        n = int(np.prod(shp))
        r = _pack_rows(n)
        piece = lax.slice_in_dim(buf, r0, r0 + r, axis=nl).reshape(*lead, r * PACK_W)
        out.append(lax.slice_in_dim(piece, 0, n, axis=nl).reshape(*lead, *shp))
        r0 += r
    return out


def _split_for_devices(g, axis):
    shp = g.shape
    g = g.reshape(*shp[:axis], N_DEV, shp[axis] // N_DEV, *shp[axis + 1:])
    return jnp.moveaxis(g, axis, 0)


def _join_from_devices(parts, axis):
    parts = jnp.moveaxis(parts, 0, axis)
    shp = parts.shape
    return parts.reshape(*shp[:axis], shp[axis] * shp[axis + 1], *shp[axis + 2:])


def _adamw_terms(w, g, m, v):
    m = ADAM_B1 * m + (1.0 - ADAM_B1) * g
    v = ADAM_B2 * v + (1.0 - ADAM_B2) * (g * g)
    m_hat = m / (1.0 - ADAM_B1 ** ADAM_STEP)
    v_hat = v / (1.0 - ADAM_B2 ** ADAM_STEP)
    delta = -ADAM_LR * (m_hat / (jnp.sqrt(v_hat) + ADAM_EPS) + ADAM_WD * w)
    return delta, m, v


def _adamw_packed(w, m, v, partials, n_partials, name):
    def fn(w_, m_, v_, *parts):
        g = parts[0]
        for part in parts[1:]:
            g = g + part
        return (g,) + _adamw_terms(w_, g, m_, v_)
    return _rowwise(fn, [w, m, v] + [(partials, s) for s in range(n_partials)], [], [(PACK_W, F32)] * 4,
                    tm=PACK_TM, name=name)


def kernel(x, p, positions, ffn1_pre_g, ffn1_w_in, ffn1_w_down, ffn1_post_g, mix_pre_g, mix_post_g, ffn2_pre_g, ffn2_w_in, ffn2_w_down, ffn2_post_g, ple_pre_g, ple_w_gate, ple_w_proj, ple_post_g, hyb_w_in, gm_ln_g, gm_ln_b, gm_w_s, gm_b_s, ssd_conv_w, ssd_conv_b, ssd_dt_bias, ssd_a_log, ssd_d, ssd_norm_g, hyb_w_out, mla_w_in, mla_q_norm_g, mla_kv_norm_g, mla_w_uq, mla_w_ukv, mla_w_out, loss_target, m_ffn1_pre_g, m_ffn1_w_in, m_ffn1_w_down, m_ffn1_post_g, m_mix_pre_g, m_mix_post_g, m_ffn2_pre_g, m_ffn2_w_in, m_ffn2_w_down, m_ffn2_post_g, m_ple_pre_g, m_ple_w_gate, m_ple_w_proj, m_ple_post_g, m_hyb_w_in, m_gm_ln_g, m_gm_ln_b, m_gm_w_s, m_gm_b_s, m_ssd_conv_w, m_ssd_conv_b, m_ssd_dt_bias, m_ssd_a_log, m_ssd_d, m_ssd_norm_g, m_hyb_w_out, m_mla_w_in, m_mla_q_norm_g, m_mla_kv_norm_g, m_mla_w_uq, m_mla_w_ukv, m_mla_w_out, v_ffn1_pre_g, v_ffn1_w_in, v_ffn1_w_down, v_ffn1_post_g, v_mix_pre_g, v_mix_post_g, v_ffn2_pre_g, v_ffn2_w_in, v_ffn2_w_down, v_ffn2_post_g, v_ple_pre_g, v_ple_w_gate, v_ple_w_proj, v_ple_post_g, v_hyb_w_in, v_gm_ln_g, v_gm_ln_b, v_gm_w_s, v_gm_b_s, v_ssd_conv_w, v_ssd_conv_b, v_ssd_dt_bias, v_ssd_a_log, v_ssd_d, v_ssd_norm_g, v_hyb_w_out, v_mla_w_in, v_mla_q_norm_g, v_mla_kv_norm_g, v_mla_w_uq, v_mla_w_ukv, v_mla_w_out):
    given = dict(locals())
    w = {n: given[n] for n in WEIGHTS}
    mom = {n: given["m_" + n] for n in WEIGHTS}
    var = {n: given["v_" + n] for n in WEIGHTS}
    shard_shapes = [w[n].shape for n in SHARDED]
    repl_shapes = [w[n].shape for n in REPLICATED]

    pack16 = _pack([w[n].astype(BF16) for n in SHARDED_BF16])
    by_chip = _exchange(pack16, ("x", "y"), "gather", "gather_weights_ici")
    by_core = _exchange(by_chip.reshape(-1, PACK_W), ("c",), "gather", "gather_weights_d2d")
    gathered = by_core.reshape(2, 4, -1, PACK_W).transpose(1, 0, 2, 3).reshape(N_DEV, -1, PACK_W)
    fw = {n: _join_from_devices(a, SHARD_AXIS[n])
          for n, a in zip(SHARDED_BF16, _unpack(gathered, [w[n].shape for n in SHARDED_BF16], (N_DEV,)))}
    small = _exchange(_pack([w[n] for n in SHARDED_F32]), MESH_AXES, "gather", "gather_weights_f32")
    fw.update({n: _join_from_devices(a, SHARD_AXIS[n])
               for n, a in zip(SHARDED_F32, _unpack(small, [w[n].shape for n in SHARDED_F32], (N_DEV,)))})

    loss_local, grad_x, grads = _device_step(x[0], p[:, 0], positions[0], loss_target[0], fw, w)
    loss = lax.psum(loss_local, MESH_AXES)

    per_dev = [_split_for_devices(grads[n], SHARD_AXIS[n]) for n in SHARDED]
    per_dev = [a.reshape(4, 2, *a.shape[1:]).swapaxes(0, 1) for a in per_dev]
    gpack = _pack(per_dev, (2, 4))
    rows = gpack.shape[2]
    pair = _exchange(gpack.reshape(2, 4 * rows, PACK_W), ("c",), "a2a", "reduce_grads_d2d")
    chip_sum = _rowwise(lambda a, b: a + b, [(pair, 0), (pair, 1)], [], [(PACK_W, F32)], tm=PACK_TM, name="reduce_grads_pair")
    quads = _exchange(chip_sum.reshape(4, rows, PACK_W), ("x", "y"), "a2a", "reduce_grads_ici")
    g_s, d_s, m_s, v_s = _adamw_packed(_pack([w[n] for n in SHARDED]), _pack([mom[n] for n in SHARDED]),
                                       _pack([var[n] for n in SHARDED]), quads, 4, "adamw_sharded")

    rpack = _pack([grads[n].reshape(w[n].shape) for n in REPLICATED])
    everyone = _exchange(rpack, MESH_AXES, "gather", "gather_small_grads")
    g_r, d_r, m_r, v_r = _adamw_packed(_pack([w[n] for n in REPLICATED]), _pack([mom[n] for n in REPLICATED]),
                                       _pack([var[n] for n in REPLICATED]), everyone, N_DEV, "adamw_replicated")

    outs = []
    for sharded_buf, repl_buf in ((g_s, g_r), (d_s, d_r), (m_s, m_r), (v_s, v_r)):
        vals = dict(zip(SHARDED, _unpack(sharded_buf, shard_shapes)))
        vals.update(zip(REPLICATED, _unpack(repl_buf, repl_shapes)))
        outs.extend(vals[n] for n in WEIGHTS)
    return (loss, grad_x[None], *outs)
```

```python
import functools
import math

import jax
import jax.numpy as jnp
import numpy as np
from jax import lax
from jax.experimental import pallas as pl
from jax.experimental.pallas import tpu as pltpu

F32 = jnp.float32
BF16 = jnp.bfloat16
HIGHEST = lax.Precision.HIGHEST

V7X_VMEM_LIMIT_BYTES = 52 * 1024 * 1024
LANES = 128

D_MODEL = 1024
DEPTH = 4
D_FF = 2816
PLE_DIM = 256
NORM_EPS = 1e-6
LN_EPS = 1e-5
CHUNK = 128
GM_HEADS = 8
SSD_HEADS = 16
SSD_HEAD_DIM = 64
SSD_INNER = 1024
SSD_STATE = 128
SSD_BC = 256
SSD_CONV_CH = 1536
HYB_MAIN = 4608
MLA_HEADS = 16
MLA_Q_LORA = 256
MLA_KV_LORA = 128
MLA_ROPE = 64
MLA_IN = 448
MLA_IN_PAD = 512
ATTN_SCALE = 192.0 ** -0.5
ROPE_BASE = 10000.0

ADAM_LR = 0.001
ADAM_B1 = 0.9
ADAM_B2 = 0.999
ADAM_EPS = 1e-08
ADAM_WD = 0.01
ADAM_STEP = 10

N_DEV = 8
PACK_W = 1024
PACK_TM = 256

WEIGHTS = ['ffn1_pre_g', 'ffn1_w_in', 'ffn1_w_down', 'ffn1_post_g', 'mix_pre_g', 'mix_post_g', 'ffn2_pre_g',
           'ffn2_w_in', 'ffn2_w_down', 'ffn2_post_g', 'ple_pre_g', 'ple_w_gate', 'ple_w_proj', 'ple_post_g',
           'hyb_w_in', 'gm_ln_g', 'gm_ln_b', 'gm_w_s', 'gm_b_s', 'ssd_conv_w', 'ssd_conv_b', 'ssd_dt_bias',
           'ssd_a_log', 'ssd_d', 'ssd_norm_g', 'hyb_w_out', 'mla_w_in', 'mla_q_norm_g', 'mla_kv_norm_g',
           'mla_w_uq', 'mla_w_ukv', 'mla_w_out']
SHARD_AXIS = {'ffn1_w_in': 2, 'ffn1_w_down': 1, 'ffn2_w_in': 2, 'ffn2_w_down': 1, 'ple_w_gate': 1, 'ple_w_proj': 2,
              'hyb_w_in': 2, 'ssd_conv_w': 2, 'hyb_w_out': 1, 'mla_w_in': 1, 'mla_q_norm_g': 1, 'mla_w_uq': 2,
              'mla_w_ukv': 2, 'mla_w_out': 1}
SHARDED = [n for n in WEIGHTS if n in SHARD_AXIS]
REPLICATED = [n for n in WEIGHTS if n not in SHARD_AXIS]
SHARDED_F32 = ['ssd_conv_w', 'mla_q_norm_g']
SHARDED_BF16 = [n for n in SHARDED if n not in SHARDED_F32]


def _params(*sem):
    return pltpu.CompilerParams(dimension_semantics=sem or None, vmem_limit_bytes=V7X_VMEM_LIMIT_BYTES)


def _pick(n, prefs):
    for t in prefs:
        if t <= n and n % t == 0:
            return t
    return n


def _mm(a, b, *, ta=False, tb=False, out_dtype=F32, tm=1024, tn=512, tk=512, name):
    m, k = (a.shape[1], a.shape[0]) if ta else a.shape
    n = b.shape[0] if tb else b.shape[1]
    assert k == (b.shape[1] if tb else b.shape[0]), (a.shape, b.shape, ta, tb)
    tm, tn, tk = _pick(m, (tm, 512, 256, 128)), _pick(n, (tn, 512, 256, 128)), _pick(k, (tk, 512, 256, 128))
    nk = k // tk
    dims = (((0 if ta else 1,), (1 if tb else 0,)), ((), ()))

    def body(a_ref, b_ref, o_ref, *acc):
        part = lax.dot_general(a_ref[...].astype(BF16), b_ref[...].astype(BF16), dims, preferred_element_type=F32)
        if nk == 1:
            o_ref[...] = part.astype(o_ref.dtype)
            return
        acc_ref, = acc
        kk = pl.program_id(2)

        @pl.when(kk == 0)
        def _():
            acc_ref[...] = part

        @pl.when(kk > 0)
        def _():
            acc_ref[...] += part

        @pl.when(kk == nk - 1)
        def _():
            o_ref[...] = acc_ref[...].astype(o_ref.dtype)

    a_spec = pl.BlockSpec((tk, tm), lambda i, j, kk: (kk, i)) if ta else pl.BlockSpec((tm, tk), lambda i, j, kk: (i, kk))
    b_spec = pl.BlockSpec((tn, tk), lambda i, j, kk: (j, kk)) if tb else pl.BlockSpec((tk, tn), lambda i, j, kk: (kk, j))
    return pl.pallas_call(
        body, name=name, grid=(m // tm, n // tn, nk), in_specs=[a_spec, b_spec],
        out_specs=pl.BlockSpec((tm, tn), lambda i, j, kk: (i, j)), out_shape=jax.ShapeDtypeStruct((m, n), out_dtype),
        scratch_shapes=[] if nk == 1 else [pltpu.VMEM((tm, tn), F32)],
        compiler_params=_params("parallel", "parallel", "arbitrary"),
    )(a, b)


def _rowwise(fn, rows, consts, outs, accs=(), *, tm=256, name):
    first = rows[0][0] if isinstance(rows[0], tuple) else rows[0]
    t = first.shape[-2]
    tm = _pick(t, (tm, 256, 128, 64, 32, 16, 8))
    n_r, n_c, n_o = len(rows), len(consts), len(outs)

    def body(*refs):
        vals = [r[...] for r in refs[:n_r + n_c]]
        res = fn(*vals)
        res = res if isinstance(res, tuple) else (res,)
        o_refs, a_refs = refs[n_r + n_c:n_r + n_c + n_o], refs[n_r + n_c + n_o:]
        for o_ref, v in zip(o_refs, res[:n_o]):
            if isinstance(v, (tuple, list)):
                off = 0
                for piece in v:
                    o_ref[:, off:off + piece.shape[1]] = piece.astype(o_ref.dtype)
                    off += piece.shape[1]
            else:
                o_ref[...] = v.astype(o_ref.dtype)
        if a_refs:
            terms = res[n_o:]
            is_first = pl.program_id(0) == 0

            @pl.when(is_first)
            def _():
                for a_ref, v in zip(a_refs, terms):
                    a_ref[...] = v

            @pl.when(jnp.logical_not(is_first))
            def _():
                for a_ref, v in zip(a_refs, terms):
                    a_ref[...] += v

    in_specs, args = [], []
    for r in rows:
        if isinstance(r, tuple):
            arr, slot = r
            in_specs.append(pl.BlockSpec((None, tm, arr.shape[2]), functools.partial(lambda i, s: (s, i, 0), s=slot)))
        else:
            arr = r
            in_specs.append(pl.BlockSpec((tm, arr.shape[1]), lambda i: (i, 0)))
        args.append(arr)
    for c in consts:
        in_specs.append(pl.BlockSpec(c.shape, lambda i: (0, 0)))
        args.append(c)
    out_specs = [pl.BlockSpec((tm, c), lambda i: (i, 0)) for c, _ in outs]
    out_shape = [jax.ShapeDtypeStruct((t, c), dt) for c, dt in outs]
    for shp in accs:
        out_specs.append(pl.BlockSpec(shp, lambda i: (0, 0)))
        out_shape.append(jax.ShapeDtypeStruct(shp, F32))
    res = pl.pallas_call(
        body, name=name, grid=(t // tm,), in_specs=in_specs, out_specs=out_specs, out_shape=out_shape,
        compiler_params=_params("arbitrary" if accs else "parallel"),
    )(*args)
    return res[0] if len(res) == 1 else tuple(res)


def _colsum(v):
    return jnp.sum(v, axis=0, keepdims=True)


def _rms(x, g, eps=NORM_EPS):
    r = lax.rsqrt(jnp.mean(x * x, axis=-1, keepdims=True) + eps)
    return x * r * g


def _rms_bwd(x, g, dy, eps=NORM_EPS):
    r = lax.rsqrt(jnp.mean(x * x, axis=-1, keepdims=True) + eps)
    xh = x * r
    dyg = dy * g
    dx = r * (dyg - xh * jnp.mean(dyg * xh, axis=-1, keepdims=True))
    return dx, dy * xh


def _silu(x):
    return x * jax.nn.sigmoid(x)


def _silu_grad(x):
    s = jax.nn.sigmoid(x)
    return s * (1.0 + x * (1.0 - s))


_GELU_K = math.sqrt(2.0 / math.pi)


def _gelu(x):
    return 0.5 * x * (1.0 + jnp.tanh(_GELU_K * (x + 0.044715 * x * x * x)))


def _gelu_grad(x):
    t = jnp.tanh(_GELU_K * (x + 0.044715 * x * x * x))
    return 0.5 * (1.0 + t) + 0.5 * x * (1.0 - t * t) * _GELU_K * (1.0 + 3.0 * 0.044715 * x * x)


def _prenorm(h, g, name):
    return _rowwise(lambda x, gg: _rms(x, gg), [h], [g], [(D_MODEL, BF16)], name=name)


def _postnorm_residual(h, f, g, scale, name):
    return _rowwise(lambda x, ff, gg: x + scale * _rms(ff, gg), [h, f], [g], [(D_MODEL, F32)], name=name)


def _postnorm_bwd(f, dh, g, scale, name):
    def fn(ff, d, gg):
        dx, dgt = _rms_bwd(ff, gg, scale * d)
        return dx, _colsum(dgt)
    return _rowwise(fn, [f, dh], [g], [(D_MODEL, BF16)], [(1, D_MODEL)], name=name)


def _prenorm_bwd(h, das, dh, g, name):
    n = len(das)

    def fn(x, *rest):
        da = rest[0]
        for extra in rest[1:n]:
            da = da + extra
        d, gg = rest[n], rest[n + 1]
        dx, dgt = _rms_bwd(x, gg, da)
        return d + dx, _colsum(dgt)
    return _rowwise(fn, [h, *das, dh], [g], [(D_MODEL, F32)], [(1, D_MODEL)], name=name)


def _ffn_fwd(h, w, pre_g, post_g, tag):
    a = _prenorm(h, pre_g, tag + "_prenorm")
    gu = _mm(a, w["in"], tm=1024, tn=512, tk=1024, name=tag + "_in")
    s = _rowwise(lambda x: _silu(x[:, :D_FF]) * x[:, D_FF:], [gu], [], [(D_FF, BF16)], name=tag + "_swiglu")
    f = _mm(s, w["down"], tm=1024, tn=1024, tk=D_FF, name=tag + "_down")
    out = _postnorm_residual(h, f, post_g, 0.5, tag + "_postnorm")
    return out, (h, a, gu, s, f)


def _ffn_bwd(dh, saved, w, pre_g, post_g, tag):
    h, a, gu, s, f = saved
    df, d_post = _postnorm_bwd(f, dh, post_g, 0.5, tag + "_postnorm_bwd")
    ds = _mm(df, w["down"], tb=True, tm=1024, tn=256, tk=1024, name=tag + "_down_dx")
    d_down = _mm(s, df, ta=True, tm=256, tn=1024, tk=512, name=tag + "_down_dw")

    def swiglu_bwd(x, d):
        gate, up = x[:, :D_FF], x[:, D_FF:]
        return ((d * up * _silu_grad(gate), d * _silu(gate)),)
    dgu = _rowwise(swiglu_bwd, [gu, ds], [], [(2 * D_FF, BF16)], name=tag + "_swiglu_bwd")
    da = _mm(dgu, w["in"], tb=True, tm=1024, tn=1024, tk=512, name=tag + "_in_dx")
    d_in = _mm(a, dgu, ta=True, tm=1024, tn=512, tk=512, name=tag + "_in_dw")
    dh_in, d_pre = _prenorm_bwd(h, [da], dh, pre_g, tag + "_prenorm_bwd")
    return dh_in, dict(w_in=d_in, w_down=d_down, pre_g=d_pre, post_g=d_post)


def _ple_fwd(h, p_i, w, pre_g, post_g):
    a = _prenorm(h, pre_g, "ple_prenorm")
    gl = _mm(a, w["gate"], tm=1024, tn=1024, tk=1024, name="ple_gate")
    e = _mm(p_i, w["proj"], tm=1024, tn=1024, tk=PLE_DIM, name="ple_proj")
    out = _rowwise(lambda x, g_, e_, gg: x + _rms(jax.nn.sigmoid(g_) * e_, gg), [h, gl, e], [post_g],
                   [(D_MODEL, F32)], name="ple_out")
    return out, (h, a, gl, e)


def _ple_bwd(dh, saved, p_i, w, pre_g, post_g):
    h, a, gl, e = saved

    def fn(g_, e_, d, gg):
        sg = jax.nn.sigmoid(g_)
        du, dgt = _rms_bwd(sg * e_, gg, d)
        return du * e_ * sg * (1.0 - sg), du * sg, _colsum(dgt)
    dgl, de, d_post = _rowwise(fn, [gl, e, dh], [post_g], [(D_MODEL, BF16), (D_MODEL, BF16)], [(1, D_MODEL)],
                               name="ple_out_bwd")
    da = _mm(dgl, w["gate"], tb=True, tm=1024, tn=1024, tk=1024, name="ple_gate_dx")
    d_gate = _mm(a, dgl, ta=True, tm=1024, tn=1024, tk=512, name="ple_gate_dw")
    d_proj = _mm(p_i, de, ta=True, tm=PLE_DIM, tn=1024, tk=512, name="ple_proj_dw")
    dh_in, d_pre = _prenorm_bwd(h, [da], dh, pre_g, "ple_prenorm_bwd")
    return dh_in, dict(w_gate=d_gate, w_proj=d_proj, pre_g=d_pre, post_g=d_post)


def _gm_layernorm(v, g, b):
    mu = jnp.mean(v, axis=-1, keepdims=True)
    xc = v - mu
    rstd = lax.rsqrt(jnp.mean(xc * xc, axis=-1, keepdims=True) + LN_EPS)
    vhat = xc * rstd
    return vhat, rstd, vhat * g + b


def _gmlp_fwd(proj, wm, bias_t, ln_g, ln_b, name):
    t = proj.shape[0]

    def body(uv_ref, wm_ref, bt_ref, g_ref, b_ref, o_ref):
        for hd in range(GM_HEADS):
            lo = hd * LANES
            u = _gelu(uv_ref[:, lo:lo + LANES])
            v = _gelu(uv_ref[:, 1024 + lo:1024 + lo + LANES])
            _, _, vln = _gm_layernorm(v, g_ref[:, lo:lo + LANES], b_ref[:, lo:lo + LANES])
            mixed = jnp.dot(wm_ref[hd], vln.astype(BF16), preferred_element_type=F32) + bt_ref[:, hd:hd + 1]
            o_ref[:, lo:lo + LANES] = (u * mixed).astype(o_ref.dtype)

    return pl.pallas_call(
        body, name=name, grid=(t // CHUNK,),
        in_specs=[pl.BlockSpec((CHUNK, 2048), lambda i: (i, 0)), pl.BlockSpec(wm.shape, lambda i: (0, 0, 0)),
                  pl.BlockSpec(bias_t.shape, lambda i: (0, 0)), pl.BlockSpec(ln_g.shape, lambda i: (0, 0)),
                  pl.BlockSpec(ln_b.shape, lambda i: (0, 0))],
        out_specs=pl.BlockSpec((CHUNK, 1024), lambda i: (i, 0)), out_shape=jax.ShapeDtypeStruct((t, 1024), BF16),
        compiler_params=_params("parallel"),
    )(proj, wm, bias_t, ln_g, ln_b)


def _gmlp_bwd(proj, dyab, wm, bias_t, ln_g, ln_b, name):
    t = proj.shape[0]
    nc = t // CHUNK

    def body(uv_ref, dy_ref, wm_ref, bt_ref, g_ref, b_ref, duv_ref, dw_ref, db_ref, dg_ref, dbeta_ref, dbacc):
        c = pl.program_id(0)

        @pl.when(c == 0)
        def _():
            dw_ref[...] = jnp.zeros_like(dw_ref)
            dbacc[...] = jnp.zeros_like(dbacc)
            dg_ref[...] = jnp.zeros_like(dg_ref)
            dbeta_ref[...] = jnp.zeros_like(dbeta_ref)

        for hd in range(GM_HEADS):
            lo = hd * LANES
            xu = uv_ref[:, lo:lo + LANES]
            xv = uv_ref[:, 1024 + lo:1024 + lo + LANES]
            u = _gelu(xu)
            g_h = g_ref[:, lo:lo + LANES]
            vhat, rstd, vln = _gm_layernorm(_gelu(xv), g_h, b_ref[:, lo:lo + LANES])
            vln16 = vln.astype(BF16)
            mixed = jnp.dot(wm_ref[hd], vln16, preferred_element_type=F32) + bt_ref[:, hd:hd + 1]
            dy = dy_ref[:, lo:lo + LANES]
            du = dy * mixed
            dmix = dy * u
            dmix16 = dmix.astype(BF16)
            dw_ref[hd] += lax.dot_general(dmix16, vln16, (((1,), (1,)), ((), ())), preferred_element_type=F32)
            dbacc[hd] += dmix
            dvln = lax.dot_general(wm_ref[hd], dmix16, (((0,), (0,)), ((), ())), preferred_element_type=F32)
            dg_ref[:, lo:lo + LANES] += _colsum(dvln * vhat)
            dbeta_ref[:, lo:lo + LANES] += _colsum(dvln)
            dvh = dvln * g_h
            dv = rstd * (dvh - jnp.mean(dvh, axis=-1, keepdims=True)
                         - vhat * jnp.mean(dvh * vhat, axis=-1, keepdims=True))
            duv_ref[:, lo:lo + LANES] = (du * _gelu_grad(xu)).astype(duv_ref.dtype)
            duv_ref[:, 1024 + lo:1024 + lo + LANES] = (dv * _gelu_grad(xv)).astype(duv_ref.dtype)

        @pl.when(c == nc - 1)
        def _():
            row = lax.broadcasted_iota(jnp.int32, (CHUNK, CHUNK), 0)
            col = lax.broadcasted_iota(jnp.int32, (CHUNK, CHUNK), 1)
            for hd in range(GM_HEADS):
                dw_ref[hd] = jnp.where(col <= row, dw_ref[hd], 0.0)
                db_ref[hd] = jnp.sum(dbacc[hd], axis=1, keepdims=True)

    return pl.pallas_call(
        body, name=name, grid=(nc,),
        in_specs=[pl.BlockSpec((CHUNK, 2048), lambda i: (i, 0)), pl.BlockSpec((CHUNK, 1024), lambda i: (i, 0)),
                  pl.BlockSpec(wm.shape, lambda i: (0, 0, 0)), pl.BlockSpec(bias_t.shape, lambda i: (0, 0)),
                  pl.BlockSpec(ln_g.shape, lambda i: (0, 0)), pl.BlockSpec(ln_b.shape, lambda i: (0, 0))],
        out_specs=[pl.BlockSpec((CHUNK, 2048), lambda i: (i, 0)), pl.BlockSpec((GM_HEADS, CHUNK, CHUNK), lambda i: (0, 0, 0)),
                   pl.BlockSpec((GM_HEADS, CHUNK, 1), lambda i: (0, 0, 0)), pl.BlockSpec((1, 1024), lambda i: (0, 0)),
                   pl.BlockSpec((1, 1024), lambda i: (0, 0))],
        out_shape=[jax.ShapeDtypeStruct((t, 2048), BF16), jax.ShapeDtypeStruct((GM_HEADS, CHUNK, CHUNK), F32),
                   jax.ShapeDtypeStruct((GM_HEADS, CHUNK, 1), F32), jax.ShapeDtypeStruct((1, 1024), F32),
                   jax.ShapeDtypeStruct((1, 1024), F32)],
        scratch_shapes=[pltpu.VMEM((GM_HEADS, CHUNK, CHUNK), F32)],
        compiler_params=_params("arbitrary"),
    )(proj, dyab, wm, bias_t, ln_g, ln_b)


def _ssd_chunk_terms(dt_pad, a_pad):
    row = lax.broadcasted_iota(jnp.int32, (CHUNK, CHUNK), 0)
    col = lax.broadcasted_iota(jnp.int32, (CHUNK, CHUNK), 1)
    tril = jnp.where(col <= row, 1.0, 0.0).astype(F32)
    a_cs = jnp.dot(tril, dt_pad * a_pad, precision=HIGHEST, preferred_element_type=F32)
    return a_cs, a_cs.T


def _pair_cols(mat, hd_a, lane_lt64):
    return jnp.where(lane_lt64, mat[:, hd_a:hd_a + 1], mat[:, hd_a + 1:hd_a + 2])


def _head_decay(a_cs, a_cs_t, hd, causal):
    seg = a_cs[:, hd:hd + 1] - a_cs_t[hd:hd + 1, :]
    return jnp.exp(jnp.where(causal, seg, -jnp.inf))


def _ssd_fwd(act, dt_pad, a_pad, d_pad, name):
    t = act.shape[0]
    nc = t // CHUNK

    def body(act_ref, dt_ref, a_ref, d_ref, y_ref, st_ref, h_sc):
        c = pl.program_id(0)

        @pl.when(c == 0)
        def _():
            h_sc[...] = jnp.zeros_like(h_sc)

        st_ref[...] = h_sc[...]
        row = lax.broadcasted_iota(jnp.int32, (CHUNK, CHUNK), 0)
        col = lax.broadcasted_iota(jnp.int32, (CHUNK, CHUNK), 1)
        causal = col <= row
        lane_lt64 = lax.broadcasted_iota(jnp.int32, (CHUNK, LANES), 1) < SSD_HEAD_DIM
        row_lt64 = lax.broadcasted_iota(jnp.int32, (LANES, 1), 0) < SSD_HEAD_DIM
        dt = dt_ref[...]
        a_cs, a_cs_t = _ssd_chunk_terms(dt, a_ref[...])
        last = a_cs[CHUNK - 1:CHUNK, :]
        for g in range(2):
            b16 = act_ref[:, SSD_INNER + g * SSD_STATE:SSD_INNER + (g + 1) * SSD_STATE].astype(BF16)
            c16 = act_ref[:, SSD_INNER + SSD_BC + g * SSD_STATE:SSD_INNER + SSD_BC + (g + 1) * SSD_STATE].astype(BF16)
            cb = lax.dot_general(c16, b16, (((1,), (1,)), ((), ())), preferred_element_type=F32)
            for pr in range(4):
                ha = g * 8 + pr * 2
                lo = ha * SSD_HEAD_DIM
                xs = act_ref[:, lo:lo + LANES]
                xd = xs * _pair_cols(dt, ha, lane_lt64)
                xd16 = xd.astype(BF16)
                ya = jnp.dot((cb * _head_decay(a_cs, a_cs_t, ha, causal)).astype(BF16), xd16, preferred_element_type=F32)
                yb = jnp.dot((cb * _head_decay(a_cs, a_cs_t, ha + 1, causal)).astype(BF16), xd16, preferred_element_type=F32)
                a_p = _pair_cols(a_cs, ha, lane_lt64)
                hp = h_sc[lo:lo + LANES, :]
                y_off = lax.dot_general(c16, hp.astype(BF16), (((1,), (1,)), ((), ())), preferred_element_type=F32)
                d_p = jnp.where(lane_lt64[:1], d_ref[:, ha:ha + 1], d_ref[:, ha + 1:ha + 2])
                y_ref[:, lo:lo + LANES] = jnp.where(lane_lt64, ya, yb) + y_off * jnp.exp(a_p) + d_p * xs
                last_p = jnp.where(lane_lt64[:1], last[:, ha:ha + 1], last[:, ha + 1:ha + 2])
                xw16 = (xd * jnp.exp(last_p - a_p)).astype(BF16)
                s_new = lax.dot_general(xw16, b16, (((0,), (0,)), ((), ())), preferred_element_type=F32)
                t_col = jnp.where(row_lt64, jnp.exp(last[:, ha:ha + 1]), jnp.exp(last[:, ha + 1:ha + 2]))
                h_sc[lo:lo + LANES, :] = t_col * hp + s_new

    return pl.pallas_call(
        body, name=name, grid=(nc,),
        in_specs=[pl.BlockSpec((CHUNK, SSD_CONV_CH), lambda i: (i, 0)), pl.BlockSpec((CHUNK, LANES), lambda i: (i, 0)),
                  pl.BlockSpec((1, LANES), lambda i: (0, 0)), pl.BlockSpec((1, LANES), lambda i: (0, 0))],
        out_specs=[pl.BlockSpec((CHUNK, SSD_INNER), lambda i: (i, 0)),
                   pl.BlockSpec((None, SSD_INNER, SSD_STATE), lambda i: (i, 0, 0))],
        out_shape=[jax.ShapeDtypeStruct((t, SSD_INNER), F32), jax.ShapeDtypeStruct((nc, SSD_INNER, SSD_STATE), F32)],
        scratch_shapes=[pltpu.VMEM((SSD_INNER, SSD_STATE), F32)],
        compiler_params=_params("arbitrary"),
    )(act, dt_pad, a_pad, d_pad)


def _ssd_bwd(act, dt_pad, a_pad, d_pad, states, dy, name):
    t = act.shape[0]
    nc = t // CHUNK

    def body(act_ref, dt_ref, a_ref, d_ref, st_ref, dy_ref, dact_ref, ddt_ref, da_ref, dd_ref, dh_sc):
        c = pl.program_id(0)

        @pl.when(c == 0)
        def _():
            dh_sc[...] = jnp.zeros_like(dh_sc)
            da_ref[...] = jnp.zeros_like(da_ref)
            dd_ref[...] = jnp.zeros_like(dd_ref)

        row = lax.broadcasted_iota(jnp.int32, (CHUNK, CHUNK), 0)
        col = lax.broadcasted_iota(jnp.int32, (CHUNK, CHUNK), 1)
        causal = col <= row
        lane = lax.broadcasted_iota(jnp.int32, (CHUNK, LANES), 1)
        lane_lt64 = lane < SSD_HEAD_DIM
        row_lt64 = lax.broadcasted_iota(jnp.int32, (LANES, 1), 0) < SSD_HEAD_DIM
        is_last = lax.broadcasted_iota(jnp.int32, (CHUNK, 1), 0) == CHUNK - 1
        dt = dt_ref[...]
        a_cs, a_cs_t = _ssd_chunk_terms(dt, a_ref[...])
        last = a_cs[CHUNK - 1:CHUNK, :]
        d_acs = jnp.zeros((CHUNK, LANES), F32)
        ddt_x = jnp.zeros((CHUNK, LANES), F32)
        dd_acc = jnp.zeros((1, LANES), F32)

        def head_sum(v, first):
            return jnp.sum(jnp.where(lane_lt64 if first else jnp.logical_not(lane_lt64), v, 0.0), axis=1, keepdims=True)

        for g in range(2):
            b_lo = SSD_INNER + g * SSD_STATE
            c_lo = SSD_INNER + SSD_BC + g * SSD_STATE
            b16 = act_ref[:, b_lo:b_lo + SSD_STATE].astype(BF16)
            c16 = act_ref[:, c_lo:c_lo + SSD_STATE].astype(BF16)
            cb = lax.dot_general(c16, b16, (((1,), (1,)), ((), ())), preferred_element_type=F32)
            dcb = jnp.zeros((CHUNK, CHUNK), F32)
            db_g = jnp.zeros((CHUNK, SSD_STATE), F32)
            dc_g = jnp.zeros((CHUNK, SSD_STATE), F32)
            for pr in range(4):
                ha = g * 8 + pr * 2
                lo = ha * SSD_HEAD_DIM
                xs = act_ref[:, lo:lo + LANES]
                dt_p = _pair_cols(dt, ha, lane_lt64)
                xd = xs * dt_p
                xd16 = xd.astype(BF16)
                a_p = _pair_cols(a_cs, ha, lane_lt64)
                exp_a = jnp.exp(a_p)
                last_p = jnp.where(lane_lt64[:1], last[:, ha:ha + 1], last[:, ha + 1:ha + 2])
                w_p = jnp.exp(last_p - a_p)
                hp = st_ref[lo:lo + LANES, :]
                hp16 = hp.astype(BF16)
                dhn = dh_sc[lo:lo + LANES, :]
                dhn16 = dhn.astype(BF16)
                dyp = dy_ref[:, lo:lo + LANES]
                d_p = jnp.where(lane_lt64[:1], d_ref[:, ha:ha + 1], d_ref[:, ha + 1:ha + 2])
                dd_acc = dd_acc + jnp.where(lane[:1] == ha, jnp.sum(head_sum(dyp * xs, True), axis=0, keepdims=True), 0.0) \
                    + jnp.where(lane[:1] == ha + 1, jnp.sum(head_sum(dyp * xs, False), axis=0, keepdims=True), 0.0)
                g_off = lax.dot_general(c16, hp16, (((1,), (1,)), ((), ())), preferred_element_type=F32)
                dg16 = (dyp * exp_a).astype(BF16)
                dc_g = dc_g + jnp.dot(dg16, hp16, preferred_element_type=F32)
                dh_prev = lax.dot_general(dg16, c16, (((0,), (0,)), ((), ())), preferred_element_type=F32)
                off_term = dyp * g_off * exp_a
                q = lax.dot_general(b16, dhn16, (((1,), (1,)), ((), ())), preferred_element_type=F32)
                xw16 = (xd * w_p).astype(BF16)
                db_g = db_g + jnp.dot(xw16, dhn16, preferred_element_type=F32)
                dw_term = xd * q * w_p
                dxd = w_p * q
                dt_all = dhn * hp
                dyp16 = dyp.astype(BF16)
                for k, first in ((0, True), (1, False)):
                    hd = ha + k
                    sel = lane_lt64 if first else jnp.logical_not(lane_lt64)
                    decay = _head_decay(a_cs, a_cs_t, hd, causal)
                    m = cb * decay
                    dy_h = jnp.where(sel, dyp16, jnp.zeros_like(dyp16))
                    dm = lax.dot_general(dy_h, xd16, (((1,), (1,)), ((), ())), preferred_element_type=F32)
                    dcb = dcb + dm * decay
                    dseg = dm * m
                    dxd = dxd + jnp.where(sel, lax.dot_general(m.astype(BF16), dyp16, (((0,), (0,)), ((), ())),
                                                               preferred_element_type=F32), 0.0)
                    d_col = jnp.sum(dseg, axis=1, keepdims=True) - jnp.sum(dseg.T, axis=1, keepdims=True)
                    dw_col = head_sum(dw_term, first)
                    d_col = d_col + head_sum(off_term, first) - dw_col
                    t_h = jnp.exp(last[:, hd:hd + 1])
                    dt_sum = jnp.sum(jnp.sum(jnp.where(row_lt64 if first else jnp.logical_not(row_lt64), dt_all, 0.0),
                                             axis=1, keepdims=True), axis=0, keepdims=True)
                    end_term = jnp.sum(dw_col, axis=0, keepdims=True) + dt_sum * t_h
                    d_col = d_col + jnp.where(is_last, end_term, 0.0)
                    d_acs = d_acs + jnp.where(lane == hd, d_col, 0.0)
                t_col = jnp.where(row_lt64, jnp.exp(last[:, ha:ha + 1]), jnp.exp(last[:, ha + 1:ha + 2]))
                dh_sc[lo:lo + LANES, :] = t_col * dhn + dh_prev
                dact_ref[:, lo:lo + LANES] = d_p * dyp + dxd * dt_p
                ddt_all = dxd * xs
                ddt_x = ddt_x + jnp.where(lane == ha, head_sum(ddt_all, True), 0.0) \
                    + jnp.where(lane == ha + 1, head_sum(ddt_all, False), 0.0)
            dcb16 = dcb.astype(BF16)
            dact_ref[:, b_lo:b_lo + SSD_STATE] = db_g + lax.dot_general(dcb16, c16, (((0,), (0,)), ((), ())),
                                                                          preferred_element_type=F32)
            dact_ref[:, c_lo:c_lo + SSD_STATE] = dc_g + jnp.dot(dcb16, b16, preferred_element_type=F32)
        triu = jnp.where(col >= row, 1.0, 0.0).astype(F32)
        dda = jnp.dot(triu, d_acs, precision=HIGHEST, preferred_element_type=F32)
        ddt_ref[...] = dda * a_ref[...] + ddt_x
        da_ref[...] += _colsum(dda * dt)
        dd_ref[...] += dd_acc

    rev = lambda i: (nc - 1 - i, 0)
    return pl.pallas_call(
        body, name=name, grid=(nc,),
        in_specs=[pl.BlockSpec((CHUNK, SSD_CONV_CH), rev), pl.BlockSpec((CHUNK, LANES), rev),
                  pl.BlockSpec((1, LANES), lambda i: (0, 0)), pl.BlockSpec((1, LANES), lambda i: (0, 0)),
                  pl.BlockSpec((None, SSD_INNER, SSD_STATE), lambda i: (nc - 1 - i, 0, 0)),
                  pl.BlockSpec((CHUNK, SSD_INNER), rev)],
        out_specs=[pl.BlockSpec((CHUNK, SSD_CONV_CH), rev), pl.BlockSpec((CHUNK, LANES), rev),
                   pl.BlockSpec((1, LANES), lambda i: (0, 0)), pl.BlockSpec((1, LANES), lambda i: (0, 0))],
        out_shape=[jax.ShapeDtypeStruct((t, SSD_CONV_CH), F32), jax.ShapeDtypeStruct((t, LANES), F32),
                   jax.ShapeDtypeStruct((1, LANES), F32), jax.ShapeDtypeStruct((1, LANES), F32)],
        scratch_shapes=[pltpu.VMEM((SSD_INNER, SSD_STATE), F32)],
        compiler_params=_params("arbitrary"),
    )(act, dt_pad, a_pad, d_pad, states, dy)


def _shift_down(x, k):
    return x if k == 0 else jnp.pad(x, ((k, 0), (0, 0)))[:x.shape[0]]


def _shift_up(x, k):
    return x if k == 0 else jnp.pad(x, ((0, k), (0, 0)))[k:]


def _conv_pre(x0, x1, x2, x3, w, b):
    return x0 * w[0:1] + x1 * w[1:2] + x2 * w[2:3] + x3 * w[3:4] + b


def _rope128(x, cpad, s_lo, s_hi):
    return x * cpad + pltpu.roll(x, 96, 1) * s_lo + pltpu.roll(x, 32, 1) * s_hi


ATTN_ROW_SPLIT = 2


def _diag_mask(rows, cols, row0):
    return lax.broadcasted_iota(jnp.int32, (rows, cols), 1) <= row0 + lax.broadcasted_iota(jnp.int32, (rows, cols), 0)


def _attn_scores(qn, qr, kn, kr):
    nt = (((1,), (1,)), ((), ()))
    return (lax.dot_general(qn, kn, nt, preferred_element_type=F32)
            + lax.dot_general(qr, kr, nt, preferred_element_type=F32)) * ATTN_SCALE


def _causal_pairs(nq, by_key):
    if by_key:
        pairs = [(i, j) for j in range(nq) for i in range(j, nq)]
    else:
        pairs = [(i, j) for i in range(nq) for j in range(i + 1)]
    return (jnp.asarray([pr[0] for pr in pairs], jnp.int32), jnp.asarray([pr[1] for pr in pairs], jnp.int32))


def _attn_fwd(qf, kvf, kr, *, tq, name):
    t = qf.shape[0]
    nq = t // tq
    tk = tq
    qi, kj = _causal_pairs(nq, by_key=False)
    rs = tq // ATTN_ROW_SPLIT

    def body(qi_ref, kj_ref, qn_ref, qr_ref, kn_ref, v_ref, kr_ref, o_ref, lse_ref, m_sc, l_sc, acc_sc):
        pp = pl.program_id(1)
        i, j = qi_ref[pp], kj_ref[pp]

        @pl.when(j == 0)
        def _():
            m_sc[...] = jnp.full_like(m_sc, -jnp.inf)
            l_sc[...] = jnp.zeros_like(l_sc)
            acc_sc[...] = jnp.zeros_like(acc_sc)

        def update(diag):
            for r in range(ATTN_ROW_SPLIT):
                rows = slice(r * rs, (r + 1) * rs)
                s = _attn_scores(qn_ref[rows, :], qr_ref[rows, :], kn_ref[...], kr_ref[...])
                if diag:
                    s = jnp.where(_diag_mask(rs, tk, r * rs), s, -jnp.inf)
                m_prev = m_sc[rows, :]
                m_new = jnp.maximum(m_prev, jnp.max(s, axis=1, keepdims=True))
                p = jnp.exp(s - m_new)
                alpha = jnp.exp(m_prev - m_new)
                l_new = alpha * l_sc[rows, :] + jnp.sum(p, axis=1, keepdims=True)
                acc = alpha * acc_sc[rows, :] + jnp.dot(p.astype(BF16), v_ref[...], preferred_element_type=F32)
                if diag:
                    o_ref[rows, :] = (acc / l_new).astype(o_ref.dtype)
                    lse_ref[rows, :] = m_new + jnp.log(l_new)
                else:
                    l_sc[rows, :] = l_new
                    acc_sc[rows, :] = acc
                    m_sc[rows, :] = m_new

        @pl.when(j < i)
        def _():
            update(False)

        @pl.when(j == i)
        def _():
            update(True)

    return pl.pallas_call(
        body, name=name,
        grid_spec=pltpu.PrefetchScalarGridSpec(
            num_scalar_prefetch=2, grid=(MLA_HEADS, int(qi.shape[0])),
            in_specs=[pl.BlockSpec((tq, LANES), lambda h, pp, qi_, kj_: (qi_[pp], 2 * h)),
                      pl.BlockSpec((tq, LANES), lambda h, pp, qi_, kj_: (qi_[pp], 2 * h + 1)),
                      pl.BlockSpec((tk, LANES), lambda h, pp, qi_, kj_: (kj_[pp], 2 * h)),
                      pl.BlockSpec((tk, LANES), lambda h, pp, qi_, kj_: (kj_[pp], 2 * h + 1)),
                      pl.BlockSpec((tk, LANES), lambda h, pp, qi_, kj_: (kj_[pp], 0))],
            out_specs=[pl.BlockSpec((tq, LANES), lambda h, pp, qi_, kj_: (qi_[pp], h)),
                       pl.BlockSpec((None, tq, 1), lambda h, pp, qi_, kj_: (h, qi_[pp], 0))],
            scratch_shapes=[pltpu.VMEM((tq, 1), F32), pltpu.VMEM((tq, 1), F32), pltpu.VMEM((tq, LANES), F32)]),
        out_shape=[jax.ShapeDtypeStruct((t, MLA_HEADS * LANES), BF16), jax.ShapeDtypeStruct((MLA_HEADS, t, 1), F32)],
        compiler_params=_params("parallel", "arbitrary"),
    )(qi, kj, qf, qf, kvf, kvf, kr)


def _attn_bwd_dq(qf, kvf, kr, o, do, lse, *, tq, name):
    t = qf.shape[0]
    nq = t // tq
    tk = tq
    qi, kj = _causal_pairs(nq, by_key=False)
    rs = tq // ATTN_ROW_SPLIT

    def body(qi_ref, kj_ref, qn_ref, qr_ref, kn_ref, v_ref, kr_ref, o_ref, do_ref, lse_ref, dq_ref, acc_sc, delta_sc):
        pp = pl.program_id(1)
        i, j = qi_ref[pp], kj_ref[pp]

        @pl.when(j == 0)
        def _():
            acc_sc[...] = jnp.zeros_like(acc_sc)
            delta_sc[...] = jnp.sum(do_ref[...].astype(F32) * o_ref[...].astype(F32), axis=1, keepdims=True)

        def update(diag):
            for r in range(ATTN_ROW_SPLIT):
                rows = slice(r * rs, (r + 1) * rs)
                s = _attn_scores(qn_ref[rows, :], qr_ref[rows, :], kn_ref[...], kr_ref[...])
                p = jnp.exp(s - lse_ref[rows, :])
                if diag:
                    p = jnp.where(_diag_mask(rs, tk, r * rs), p, 0.0)
                dp = lax.dot_general(do_ref[rows, :], v_ref[...], (((1,), (1,)), ((), ())), preferred_element_type=F32)
                ds = (p * (dp - delta_sc[rows, :]) * ATTN_SCALE).astype(BF16)
                dqn = acc_sc[rows, :LANES] + jnp.dot(ds, kn_ref[...], preferred_element_type=F32)
                dqr = acc_sc[rows, LANES:] + jnp.dot(ds, kr_ref[...], preferred_element_type=F32)
                if diag:
                    dq_ref[rows, :LANES] = dqn
                    dq_ref[rows, LANES:] = dqr
                else:
                    acc_sc[rows, :LANES] = dqn
                    acc_sc[rows, LANES:] = dqr

        @pl.when(j < i)
        def _():
            update(False)

        @pl.when(j == i)
        def _():
            update(True)

    qblk = lambda c: (lambda h, pp, qi_, kj_: (qi_[pp], c(h)))
    kblk = lambda c: (lambda h, pp, qi_, kj_: (kj_[pp], c(h)))
    return pl.pallas_call(
        body, name=name,
        grid_spec=pltpu.PrefetchScalarGridSpec(
            num_scalar_prefetch=2, grid=(MLA_HEADS, int(qi.shape[0])),
            in_specs=[pl.BlockSpec((tq, LANES), qblk(lambda h: 2 * h)), pl.BlockSpec((tq, LANES), qblk(lambda h: 2 * h + 1)),
                      pl.BlockSpec((tk, LANES), kblk(lambda h: 2 * h)), pl.BlockSpec((tk, LANES), kblk(lambda h: 2 * h + 1)),
                      pl.BlockSpec((tk, LANES), kblk(lambda h: 0)),
                      pl.BlockSpec((tq, LANES), qblk(lambda h: h)), pl.BlockSpec((tq, LANES), qblk(lambda h: h)),
                      pl.BlockSpec((None, tq, 1), lambda h, pp, qi_, kj_: (h, qi_[pp], 0))],
            out_specs=pl.BlockSpec((tq, 2 * LANES), qblk(lambda h: h)),
            scratch_shapes=[pltpu.VMEM((tq, 2 * LANES), F32), pltpu.VMEM((tq, 1), F32)]),
        out_shape=jax.ShapeDtypeStruct((t, MLA_HEADS * 2 * LANES), F32),
        compiler_params=_params("parallel", "arbitrary"),
    )(qi, kj, qf, qf, kvf, kvf, kr, o, do, lse)


def _attn_bwd_dkv(qf, kvf, kr, o, do, lse, *, tq, name):
    t = qf.shape[0]
    nq = t // tq
    tk = tq
    qi, kj = _causal_pairs(nq, by_key=True)
    rs = tq // ATTN_ROW_SPLIT

    def body(qi_ref, kj_ref, qn_ref, qr_ref, kn_ref, v_ref, kr_ref, o_ref, do_ref, lse_ref, dkv_ref, dkr_ref,
             dkn_sc, dv_sc, dkr_sc):
        pp = pl.program_id(1)
        i, j = qi_ref[pp], kj_ref[pp]
        tn = (((0,), (0,)), ((), ()))

        def update(diag):
            dv = dkn = dkr = None
            for r in range(ATTN_ROW_SPLIT):
                rows = slice(r * rs, (r + 1) * rs)
                do_ = do_ref[rows, :]
                delta = jnp.sum(do_.astype(F32) * o_ref[rows, :].astype(F32), axis=1, keepdims=True)
                s = _attn_scores(qn_ref[rows, :], qr_ref[rows, :], kn_ref[...], kr_ref[...])
                p = jnp.exp(s - lse_ref[rows, :])
                if diag:
                    p = jnp.where(_diag_mask(rs, tk, r * rs), p, 0.0)
                dp = lax.dot_general(do_, v_ref[...], (((1,), (1,)), ((), ())), preferred_element_type=F32)
                ds = (p * (dp - delta) * ATTN_SCALE).astype(BF16)
                parts = (lax.dot_general(p.astype(BF16), do_, tn, preferred_element_type=F32),
                         lax.dot_general(ds, qn_ref[rows, :], tn, preferred_element_type=F32),
                         lax.dot_general(ds, qr_ref[rows, :], tn, preferred_element_type=F32))
                dv, dkn, dkr = parts if dv is None else (dv + parts[0], dkn + parts[1], dkr + parts[2])
            if diag:
                dv_sc[...] = dv
                dkn_sc[...] = dkn
                dkr_sc[...] = dkr
            else:
                dv_sc[...] += dv
                dkn_sc[...] += dkn
                dkr_sc[...] += dkr

        @pl.when(i > j)
        def _():
            update(False)

        @pl.when(i == j)
        def _():
            update(True)

        @pl.when(i == nq - 1)
        def _():
            dkv_ref[:, :LANES] = dkn_sc[...].astype(dkv_ref.dtype)
            dkv_ref[:, LANES:] = dv_sc[...].astype(dkv_ref.dtype)
            dkr_ref[...] = dkr_sc[...]

    qblk = lambda c: (lambda h, pp, qi_, kj_: (qi_[pp], c(h)))
    kblk = lambda c: (lambda h, pp, qi_, kj_: (kj_[pp], c(h)))
    return pl.pallas_call(
        body, name=name,
        grid_spec=pltpu.PrefetchScalarGridSpec(
            num_scalar_prefetch=2, grid=(MLA_HEADS, int(qi.shape[0])),
            in_specs=[pl.BlockSpec((tq, LANES), qblk(lambda h: 2 * h)), pl.BlockSpec((tq, LANES), qblk(lambda h: 2 * h + 1)),
                      pl.BlockSpec((tk, LANES), kblk(lambda h: 2 * h)), pl.BlockSpec((tk, LANES), kblk(lambda h: 2 * h + 1)),
                      pl.BlockSpec((tk, LANES), kblk(lambda h: 0)),
                      pl.BlockSpec((tq, LANES), qblk(lambda h: h)), pl.BlockSpec((tq, LANES), qblk(lambda h: h)),
                      pl.BlockSpec((None, tq, 1), lambda h, pp, qi_, kj_: (h, qi_[pp], 0))],
            out_specs=[pl.BlockSpec((tk, 2 * LANES), kblk(lambda h: h)), pl.BlockSpec((tk, LANES), kblk(lambda h: h))],
            scratch_shapes=[pltpu.VMEM((tk, LANES), F32), pltpu.VMEM((tk, LANES), F32), pltpu.VMEM((tk, LANES), F32)]),
        out_shape=[jax.ShapeDtypeStruct((t, MLA_HEADS * 2 * LANES), BF16), jax.ShapeDtypeStruct((t, MLA_HEADS * LANES), F32)],
        compiler_params=_params("parallel", "arbitrary"),
    )(qi, kj, qf, qf, kvf, kvf, kr, o, do, lse)


def _rope_tables(positions):
    t = positions.shape[0]
    inv = 1.0 / (ROPE_BASE ** (jnp.arange(0, MLA_ROPE, 2, dtype=F32) / MLA_ROPE))
    ang = positions.astype(F32)[:, None] * inv
    cos, sin = jnp.cos(ang), jnp.sin(ang)
    z32, z64 = jnp.zeros((t, 32), F32), jnp.zeros((t, 64), F32)
    cpad = jnp.concatenate([cos, cos, z64], axis=1)
    s_lo = jnp.concatenate([-sin, z32, z64], axis=1)
    s_hi = jnp.concatenate([z32, sin, z64], axis=1)
    return cpad, s_lo, s_hi


def _mla_fwd(h, w, pre_g, post_g, rope, tq):
    cpad, s_lo, s_hi = rope
    hn = _prenorm(h, pre_g, "mla_prenorm")
    cin = _mm(hn, w["in"], tm=1024, tn=512, tk=1024, name="mla_in")

    def lat(c, cp, sl, sh, qg, kvg):
        cq, ckv, kr = c[:, :MLA_Q_LORA], c[:, MLA_Q_LORA:MLA_Q_LORA + MLA_KV_LORA], c[:, MLA_Q_LORA + MLA_KV_LORA:]
        return _rms(cq, qg), _rms(ckv, kvg), _rope128(kr, cp, sl, sh)
    cqn, ckvn, kr = _rowwise(lat, [cin, cpad, s_lo, s_hi], [w["q_norm_g"], w["kv_norm_g"]],
                             [(MLA_Q_LORA, BF16), (MLA_KV_LORA, BF16), (LANES, BF16)], name="mla_latent")
    q_raw = _mm(cqn, w["uq"], tm=1024, tn=1024, tk=MLA_Q_LORA, name="mla_uq")

    def rope_q(q, cp, sl, sh):
        pieces = []
        for hd in range(MLA_HEADS):
            pieces.append(q[:, 256 * hd:256 * hd + LANES])
            pieces.append(_rope128(q[:, 256 * hd + LANES:256 * hd + 256], cp, sl, sh))
        return (tuple(pieces),)
    qf = _rowwise(rope_q, [q_raw, cpad, s_lo, s_hi], [], [(4096, BF16)], name="mla_rope_q")
    kvf = _mm(ckvn, w["ukv"], out_dtype=BF16, tm=1024, tn=1024, tk=MLA_KV_LORA, name="mla_ukv")
    o, lse = _attn_fwd(qf, kvf, kr, tq=tq, name="mla_attn")
    mixed = _mm(o, w["out"], tm=1024, tn=1024, tk=2048, name="mla_out")
    out = _postnorm_residual(h, mixed, post_g, 1.0, "mla_postnorm")
    return out, (h, hn, cin, cqn, ckvn, kr, qf, kvf, o, lse, mixed)


def _mla_bwd(dh, saved, w, pre_g, post_g, rope, tq):
    cpad, s_lo, s_hi = rope
    h, hn, cin, cqn, ckvn, kr, qf, kvf, o, lse, mixed = saved
    dmixed, d_post = _postnorm_bwd(mixed, dh, post_g, 1.0, "mla_postnorm_bwd")
    do = _mm(dmixed, w["out"], tb=True, out_dtype=BF16, tm=1024, tn=1024, tk=1024, name="mla_out_dx")
    d_out = _mm(o, dmixed, ta=True, tm=1024, tn=1024, tk=512, name="mla_out_dw")
    dq = _attn_bwd_dq(qf, kvf, kr, o, do, lse, tq=tq, name="mla_attn_dq")
    dkvf, dkr_heads = _attn_bwd_dkv(qf, kvf, kr, o, do, lse, tq=tq, name="mla_attn_dkv")

    def unrope_q(d, cp, sl, sh):
        pieces = []
        for hd in range(MLA_HEADS):
            pieces.append(d[:, 256 * hd:256 * hd + LANES])
            pieces.append(_rope128(d[:, 256 * hd + LANES:256 * hd + 256], cp, -sl, -sh))
        return (tuple(pieces),)
    dq_raw = _rowwise(unrope_q, [dq, cpad, s_lo, s_hi], [], [(4096, BF16)], name="mla_rope_q_bwd")
    dcqn = _mm(dq_raw, w["uq"], tb=True, tm=1024, tn=256, tk=1024, name="mla_uq_dx")
    d_uq = _mm(cqn, dq_raw, ta=True, tm=256, tn=1024, tk=512, name="mla_uq_dw")
    dckvn = _mm(dkvf, w["ukv"], tb=True, tm=1024, tn=128, tk=1024, name="mla_ukv_dx")
    d_ukv = _mm(ckvn, dkvf, ta=True, tm=128, tn=1024, tk=512, name="mla_ukv_dw")

    def lat_bwd(c, dq_, dkv_, dkrh, cp, sl, sh, qg, kvg):
        cq, ckv = c[:, :MLA_Q_LORA], c[:, MLA_Q_LORA:MLA_Q_LORA + MLA_KV_LORA]
        dcq, dqg = _rms_bwd(cq, qg, dq_)
        dckv, dkvg = _rms_bwd(ckv, kvg, dkv_)
        dkr = dkrh[:, :LANES]
        for hd in range(1, MLA_HEADS):
            dkr = dkr + dkrh[:, hd * LANES:(hd + 1) * LANES]
        return (dcq, dckv, _rope128(dkr, cp, -sl, -sh)), _colsum(dqg), _colsum(dkvg)
    dcin, d_qg, d_kvg = _rowwise(lat_bwd, [cin, dcqn, dckvn, dkr_heads, cpad, s_lo, s_hi], [w["q_norm_g"], w["kv_norm_g"]],
                                 [(MLA_IN_PAD, BF16)], [(1, MLA_Q_LORA), (1, MLA_KV_LORA)], name="mla_latent_bwd")
    dhn = _mm(dcin, w["in"], tb=True, tm=1024, tn=1024, tk=512, name="mla_in_dx")
    d_in = _mm(hn, dcin, ta=True, tm=1024, tn=512, tk=512, name="mla_in_dw")
    dh_in, d_pre = _prenorm_bwd(h, [dhn], dh, pre_g, "mla_prenorm_bwd")
    return dh_in, dict(w_in=d_in, q_norm_g=d_qg, kv_norm_g=d_kvg, w_uq=d_uq, w_ukv=d_ukv, w_out=d_out,
                       pre_g=d_pre, post_g=d_post)


def _hyb_fwd(h, w, pre_g, post_g):
    hn = _prenorm(h, pre_g, "hyb_prenorm")
    proj = _mm(hn, w["main"], tm=1024, tn=512, tk=1024, name="hyb_in")
    dtr = _mm(hn, w["dt"], tm=1024, tn=LANES, tk=1024, name="hyb_in_dt")
    ya = _gmlp_fwd(proj, w["gm_w"], w["gm_bt"], w["gm_ln_g"], w["gm_ln_b"], "gmlp")
    xbc = proj[:, 3072:]
    xsh = [_shift_down(xbc, 3 - k) for k in range(4)]
    act = _rowwise(lambda x0, x1, x2, x3, cw, cb: _silu(_conv_pre(x0, x1, x2, x3, cw, cb)), xsh,
                   [w["conv_w"], w["conv_b"]], [(SSD_CONV_CH, F32)], name="ssd_conv")
    dt_pad = _rowwise(lambda d, b: jax.nn.softplus(d + b), [dtr], [w["dt_bias"]], [(LANES, F32)], name="ssd_dt")
    y, states = _ssd_fwd(act, dt_pad, w["a"], w["d"], "ssd_scan")

    def gate_norm(y_, p_, ng):
        yg = y_ * _silu(p_[:, 2048:3072])
        return ((_rms(yg[:, :512], ng[:, :512]), _rms(yg[:, 512:], ng[:, 512:])),)
    yb = _rowwise(gate_norm, [y, proj], [w["norm_g"]], [(SSD_INNER, BF16)], name="ssd_gate_norm")
    yab = jnp.concatenate([ya, yb], axis=1)
    mixed = _mm(yab, w["out"], tm=1024, tn=1024, tk=2048, name="hyb_out")
    out = _postnorm_residual(h, mixed, post_g, 1.0, "hyb_postnorm")
    return out, (h, hn, proj, dtr, xsh, act, dt_pad, y, states, yab, mixed)


def _hyb_bwd(dh, saved, w, pre_g, post_g):
    h, hn, proj, dtr, xsh, act, dt_pad, y, states, yab, mixed = saved
    dmixed, d_post = _postnorm_bwd(mixed, dh, post_g, 1.0, "hyb_postnorm_bwd")
    dyab = _mm(dmixed, w["out"], tb=True, tm=1024, tn=1024, tk=1024, name="hyb_out_dx")
    d_out = _mm(yab, dmixed, ta=True, tm=1024, tn=1024, tk=512, name="hyb_out_dw")

    def gate_norm_bwd(y_, p_, d, ng):
        z = p_[:, 2048:3072]
        sz = _silu(z)
        yg = y_ * sz
        d_lo, g_lo = _rms_bwd(yg[:, :512], ng[:, :512], d[:, 1024:1536])
        d_hi, g_hi = _rms_bwd(yg[:, 512:], ng[:, 512:], d[:, 1536:])
        dyg = jnp.concatenate([d_lo, d_hi], axis=1)
        return dyg * sz, dyg * y_ * _silu_grad(z), _colsum(jnp.concatenate([g_lo, g_hi], axis=1))
    dy, dz, d_norm = _rowwise(gate_norm_bwd, [y, proj, dyab], [w["norm_g"]], [(SSD_INNER, F32), (SSD_INNER, BF16)],
                              [(1, SSD_INNER)], name="ssd_gate_norm_bwd")
    dact, ddt, da_sum, dd_sum = _ssd_bwd(act, dt_pad, w["a"], w["d"], states, dy, "ssd_scan_bwd")

    def conv_bwd(x0, x1, x2, x3, da_, cw, cb):
        dpre = da_ * _silu_grad(_conv_pre(x0, x1, x2, x3, cw, cb))
        dw = jnp.concatenate([_colsum(dpre * x0), _colsum(dpre * x1), _colsum(dpre * x2), _colsum(dpre * x3)], axis=0)
        return dpre, dw, _colsum(dpre)
    dconv, d_conv_w, d_conv_b = _rowwise(conv_bwd, [*xsh, dact], [w["conv_w"], w["conv_b"]], [(SSD_CONV_CH, F32)],
                                         [(4, SSD_CONV_CH), (1, SSD_CONV_CH)], name="ssd_conv_bwd")
    dsh = [_shift_up(dconv, 3 - k) for k in range(4)]
    dxbc = _rowwise(lambda d0, d1, d2, d3, cw: d0 * cw[0:1] + d1 * cw[1:2] + d2 * cw[2:3] + d3 * cw[3:4], dsh,
                    [w["conv_w"]], [(SSD_CONV_CH, BF16)], name="ssd_conv_dx")

    def dt_bwd(dd, d, b):
        g = dd * jax.nn.sigmoid(d + b)
        g = jnp.where(lax.broadcasted_iota(jnp.int32, g.shape, 1) < SSD_HEADS, g, 0.0)
        return g, _colsum(g)
    ddtr, d_dt_bias = _rowwise(dt_bwd, [ddt, dtr], [w["dt_bias"]], [(LANES, BF16)], [(1, LANES)], name="ssd_dt_bwd")
    duv, d_gm_w, d_gm_b, d_ln_g, d_ln_b = _gmlp_bwd(proj, dyab, w["gm_w"], w["gm_bt"], w["gm_ln_g"], w["gm_ln_b"],
                                                    "gmlp_bwd")
    dproj = jnp.concatenate([duv, dz, dxbc], axis=1)
    dhn_a = _mm(dproj, w["main"], tb=True, tm=1024, tn=1024, tk=512, name="hyb_in_dx")
    dhn_b = _mm(ddtr, w["dt"], tb=True, tm=1024, tn=1024, tk=LANES, name="hyb_in_dt_dx")
    d_main = _mm(hn, dproj, ta=True, tm=1024, tn=512, tk=512, name="hyb_in_dw")
    d_dt = _mm(hn, ddtr, ta=True, tm=1024, tn=LANES, tk=512, name="hyb_in_dt_dw")
    dh_in, d_pre = _prenorm_bwd(h, [dhn_a, dhn_b], dh, pre_g, "hyb_prenorm_bwd")
    grads = dict(w_in=jnp.concatenate([d_main, d_dt[:, :SSD_HEADS]], axis=1), gm_ln_g=d_ln_g, gm_ln_b=d_ln_b,
                 gm_w_s=d_gm_w, gm_b_s=d_gm_b[:, :, 0], conv_w=d_conv_w, conv_b=d_conv_b,
                 dt_bias=d_dt_bias[:, :SSD_HEADS], a_log=(da_sum * w["a"])[:, :SSD_HEADS], d=dd_sum[:, :SSD_HEADS],
                 norm_g=d_norm, w_out=d_out, pre_g=d_pre, post_g=d_post)
    return dh_in, grads


def _row(v):
    return v.reshape(1, -1).astype(F32)


def _pad_lanes(v, n=LANES):
    v = _row(v)
    return jnp.pad(v, ((0, 0), (0, n - v.shape[1])))


def _layer_weights(fw, sm, i):
    j = i // 2
    lw = dict(
        ffn1=dict({"in": fw["ffn1_w_in"][i], "down": fw["ffn1_w_down"][i]}),
        ffn2=dict({"in": fw["ffn2_w_in"][i], "down": fw["ffn2_w_down"][i]}),
        ple=dict(gate=fw["ple_w_gate"][i], proj=fw["ple_w_proj"][i]),
    )
    if i % 2 == 0:
        w_in = fw["hyb_w_in"][j]
        causal = jnp.tril(jnp.ones((CHUNK, CHUNK), dtype=bool))
        lw["mix"] = {
            "main": w_in[:, :HYB_MAIN], "dt": jnp.pad(w_in[:, HYB_MAIN:], ((0, 0), (0, LANES - SSD_HEADS))),
            "gm_w": jnp.where(causal[None], sm["gm_w_s"][j], 0.0).astype(BF16),
            "gm_bt": jnp.pad(sm["gm_b_s"][j].T, ((0, 0), (0, LANES - GM_HEADS))),
            "gm_ln_g": _row(sm["gm_ln_g"][j]), "gm_ln_b": _row(sm["gm_ln_b"][j]),
            "conv_w": fw["ssd_conv_w"][j], "conv_b": _row(sm["ssd_conv_b"][j]),
            "dt_bias": _pad_lanes(sm["ssd_dt_bias"][j]), "a": _pad_lanes(-jnp.exp(sm["ssd_a_log"][j])),
            "d": _pad_lanes(sm["ssd_d"][j]), "norm_g": _row(sm["ssd_norm_g"][j]), "out": fw["hyb_w_out"][j],
        }
    else:
        uq = fw["mla_w_uq"][j].reshape(MLA_Q_LORA, MLA_HEADS, 192)
        uq = jnp.pad(uq, ((0, 0), (0, 0), (0, 64))).reshape(MLA_Q_LORA, MLA_HEADS * 256)
        lw["mix"] = {
            "in": jnp.pad(fw["mla_w_in"][j], ((0, 0), (0, MLA_IN_PAD - MLA_IN))), "uq": uq, "ukv": fw["mla_w_ukv"][j],
            "out": fw["mla_w_out"][j], "q_norm_g": _row(fw["mla_q_norm_g"][j]), "kv_norm_g": _row(sm["mla_kv_norm_g"][j]),
        }
    return lw


def _device_step(x, p, positions, target, fw, sm):
    t = x.shape[0]
    tq = _pick(t, (512, 256, 128))
    rope = _rope_tables(positions)
    h = x
    saved, lws = [], []
    for i in range(DEPTH):
        lw = _layer_weights(fw, sm, i)
        lws.append(lw)
        h, s1 = _ffn_fwd(h, lw["ffn1"], _row(sm["ffn1_pre_g"][i]), _row(sm["ffn1_post_g"][i]), "ffn")
        if i % 2 == 0:
            h, s2 = _hyb_fwd(h, lw["mix"], _row(sm["mix_pre_g"][i]), _row(sm["mix_post_g"][i]))
        else:
            h, s2 = _mla_fwd(h, lw["mix"], _row(sm["mix_pre_g"][i]), _row(sm["mix_post_g"][i]), rope, tq)
        h, s3 = _ffn_fwd(h, lw["ffn2"], _row(sm["ffn2_pre_g"][i]), _row(sm["ffn2_post_g"][i]), "ffn")
        h, s4 = _ple_fwd(h, p[i], lw["ple"], _row(sm["ple_pre_g"][i]), _row(sm["ple_post_g"][i]))
        saved.append((s1, s2, s3, s4))

    def loss_fn(y, tg):
        err = y - tg
        return err * (1.0 / D_MODEL), jnp.sum(_colsum(err * err), axis=1, keepdims=True)
    dh, loss_sum = _rowwise(loss_fn, [h, target], [], [(D_MODEL, F32)], [(1, 1)], name="loss")
    loss = loss_sum[0, 0] * (0.5 / D_MODEL)

    per_layer = {n: [None] * DEPTH for n in WEIGHTS if n.startswith(("ffn", "mix", "ple"))}
    per_mixer = {n: [None] * (DEPTH // 2) for n in WEIGHTS if n.startswith(("hyb", "gm", "ssd", "mla"))}
    for i in reversed(range(DEPTH)):
        lw = lws[i]
        s1, s2, s3, s4 = saved[i]
        j = i // 2
        dh, g = _ple_bwd(dh, s4, p[i], lw["ple"], _row(sm["ple_pre_g"][i]), _row(sm["ple_post_g"][i]))
        for k, v in g.items():
            per_layer["ple_" + k][i] = v
        dh, g = _ffn_bwd(dh, s3, lw["ffn2"], _row(sm["ffn2_pre_g"][i]), _row(sm["ffn2_post_g"][i]), "ffn")
        for k, v in g.items():
            per_layer["ffn2_" + k][i] = v
        if i % 2 == 0:
            dh, g = _hyb_bwd(dh, s2, lw["mix"], _row(sm["mix_pre_g"][i]), _row(sm["mix_post_g"][i]))
            names = dict(w_in="hyb_w_in", gm_ln_g="gm_ln_g", gm_ln_b="gm_ln_b", gm_w_s="gm_w_s", gm_b_s="gm_b_s",
                         conv_w="ssd_conv_w", conv_b="ssd_conv_b", dt_bias="ssd_dt_bias", a_log="ssd_a_log", d="ssd_d",
                         norm_g="ssd_norm_g", w_out="hyb_w_out")
        else:
            dh, g = _mla_bwd(dh, s2, lw["mix"], _row(sm["mix_pre_g"][i]), _row(sm["mix_post_g"][i]), rope, tq)
            g["w_in"] = g["w_in"][:, :MLA_IN]
            g["w_uq"] = g["w_uq"].reshape(MLA_Q_LORA, MLA_HEADS, 256)[:, :, :192].reshape(MLA_Q_LORA, MLA_HEADS * 192)
            names = dict(w_in="mla_w_in", q_norm_g="mla_q_norm_g", kv_norm_g="mla_kv_norm_g", w_uq="mla_w_uq",
                         w_ukv="mla_w_ukv", w_out="mla_w_out")
        per_layer["mix_pre_g"][i] = g.pop("pre_g")
        per_layer["mix_post_g"][i] = g.pop("post_g")
        for k, v in g.items():
            per_mixer[names[k]][j] = v
        dh, g = _ffn_bwd(dh, s1, lw["ffn1"], _row(sm["ffn1_pre_g"][i]), _row(sm["ffn1_post_g"][i]), "ffn")
        for k, v in g.items():
            per_layer["ffn1_" + k][i] = v

    grads = {}
    for n, parts in {**per_layer, **per_mixer}.items():
        stacked = jnp.stack(parts, axis=0)
        if stacked.ndim == 3 and stacked.shape[1] == 1:
            stacked = stacked[:, 0]
        grads[n] = stacked
    return loss, dh, grads


MESH_AXES = ("x", "y", "c")
EXCHANGE_MAX_COPIES = 56


def _exchange(src, axes, mode, name):
    n = 2 ** len(axes)
    blk = src.shape[-2:]
    flips = [tuple(a for a, bit in zip(axes, np.binary_repr(f, len(axes))) if bit == "1") for f in range(1, n)]
    prefs = tuple(c for c in (16, 8, 4, 2, 1) if c * (n - 1) <= EXCHANGE_MAX_COPIES)
    pieces = _pick(blk[0] // 16, prefs) if blk[0] % 16 == 0 else 1
    rows = blk[0] // pieces

    def body(src_ref, out_ref, send_sems, recv_sems, local_sems):
        pos = {a: lax.axis_index(a) for a in MESH_AXES}

        def index(where):
            idx = 0
            for a in axes:
                idx = idx * 2 + where[a]
            return idx

        me = index(pos)
        own = src_ref.at[me] if mode == "a2a" else src_ref
        copies = []
        for q in range(pieces):
            part = pl.ds(q * rows, rows)
            cp = pltpu.make_async_copy(own.at[part], out_ref.at[me, part], local_sems.at[q])
            cp.start()
            copies.append(cp)
        for k, flip in enumerate(flips):
            peer = {a: (1 - pos[a]) if a in flip else pos[a] for a in MESH_AXES}
            payload = src_ref.at[index(peer)] if mode == "a2a" else src_ref
            for q in range(pieces):
                part = pl.ds(q * rows, rows)
                cp = pltpu.make_async_remote_copy(
                    src_ref=payload.at[part], dst_ref=out_ref.at[me, part], send_sem=send_sems.at[k * pieces + q],
                    recv_sem=recv_sems.at[k * pieces + q], device_id=(peer["x"], peer["y"], peer["c"]),
                    device_id_type=pl.DeviceIdType.MESH)
                cp.start()
                copies.append(cp)
        for cp in copies:
            cp.wait()

    n_sems = (n - 1) * pieces
    return pl.pallas_call(
        body, name=name, in_specs=[pl.BlockSpec(memory_space=pl.ANY)], out_specs=pl.BlockSpec(memory_space=pl.ANY),
        out_shape=jax.ShapeDtypeStruct((n, *blk), src.dtype),
        scratch_shapes=[pltpu.SemaphoreType.DMA((n_sems,)), pltpu.SemaphoreType.DMA((n_sems,)),
                        pltpu.SemaphoreType.DMA((pieces,))],
    )(src)


def _pack_rows(n_elems):
    return -(-n_elems // (16 * PACK_W)) * 16


def _pack(parts, lead=()):
    nl = len(lead)
    rows = []
    for a in parts:
        flat = a.reshape(*lead, -1)
        r = _pack_rows(flat.shape[-1])
        flat = jnp.pad(flat, [(0, 0)] * nl + [(0, r * PACK_W - flat.shape[-1])])
        rows.append(flat.reshape(*lead, r, PACK_W))
    total = sum(r.shape[nl] for r in rows)
    pad = -total % PACK_TM
    if pad:
        rows.append(jnp.zeros((*lead, pad, PACK_W), rows[0].dtype))
    return jnp.concatenate(rows, axis=nl)


def _unpack(buf, shapes, lead=()):
    nl = len(lead)
    out, r0 = [], 0
    for shp in shapes:
        n = int(np.prod(shp))
        r = _pack_rows(n)
        piece = lax.slice_in_dim(buf, r0, r0 + r, axis=nl).reshape(*lead, r * PACK_W)
        out.append(lax.slice_in_dim(piece, 0, n, axis=nl).reshape(*lead, *shp))
        r0 += r
    return out


def _split_for_devices(g, axis):
    shp = g.shape
    g = g.reshape(*shp[:axis], N_DEV, shp[axis] // N_DEV, *shp[axis + 1:])
    return jnp.moveaxis(g, axis, 0)


def _join_from_devices(parts, axis):
    parts = jnp.moveaxis(parts, 0, axis)
    shp = parts.shape
    return parts.reshape(*shp[:axis], shp[axis] * shp[axis + 1], *shp[axis + 2:])


def _adamw_terms(w, g, m, v):
    m = ADAM_B1 * m + (1.0 - ADAM_B1) * g
    v = ADAM_B2 * v + (1.0 - ADAM_B2) * (g * g)
    m_hat = m / (1.0 - ADAM_B1 ** ADAM_STEP)
    v_hat = v / (1.0 - ADAM_B2 ** ADAM_STEP)
    delta = -ADAM_LR * (m_hat / (jnp.sqrt(v_hat) + ADAM_EPS) + ADAM_WD * w)
    return delta, m, v


def _adamw_packed(w, m, v, partials, n_partials, name):
    def fn(w_, m_, v_, *parts):
        g = parts[0].astype(F32)
        for part in parts[1:]:
            g = g + part.astype(F32)
        return (g,) + _adamw_terms(w_, g, m_, v_)
    return _rowwise(fn, [w, m, v] + [(partials, s) for s in range(n_partials)], [], [(PACK_W, F32)] * 4,
                    tm=PACK_TM, name=name)


def kernel(x, p, positions, ffn1_pre_g, ffn1_w_in, ffn1_w_down, ffn1_post_g, mix_pre_g, mix_post_g, ffn2_pre_g, ffn2_w_in, ffn2_w_down, ffn2_post_g, ple_pre_g, ple_w_gate, ple_w_proj, ple_post_g, hyb_w_in, gm_ln_g, gm_ln_b, gm_w_s, gm_b_s, ssd_conv_w, ssd_conv_b, ssd_dt_bias, ssd_a_log, ssd_d, ssd_norm_g, hyb_w_out, mla_w_in, mla_q_norm_g, mla_kv_norm_g, mla_w_uq, mla_w_ukv, mla_w_out, loss_target, m_ffn1_pre_g, m_ffn1_w_in, m_ffn1_w_down, m_ffn1_post_g, m_mix_pre_g, m_mix_post_g, m_ffn2_pre_g, m_ffn2_w_in, m_ffn2_w_down, m_ffn2_post_g, m_ple_pre_g, m_ple_w_gate, m_ple_w_proj, m_ple_post_g, m_hyb_w_in, m_gm_ln_g, m_gm_ln_b, m_gm_w_s, m_gm_b_s, m_ssd_conv_w, m_ssd_conv_b, m_ssd_dt_bias, m_ssd_a_log, m_ssd_d, m_ssd_norm_g, m_hyb_w_out, m_mla_w_in, m_mla_q_norm_g, m_mla_kv_norm_g, m_mla_w_uq, m_mla_w_ukv, m_mla_w_out, v_ffn1_pre_g, v_ffn1_w_in, v_ffn1_w_down, v_ffn1_post_g, v_mix_pre_g, v_mix_post_g, v_ffn2_pre_g, v_ffn2_w_in, v_ffn2_w_down, v_ffn2_post_g, v_ple_pre_g, v_ple_w_gate, v_ple_w_proj, v_ple_post_g, v_hyb_w_in, v_gm_ln_g, v_gm_ln_b, v_gm_w_s, v_gm_b_s, v_ssd_conv_w, v_ssd_conv_b, v_ssd_dt_bias, v_ssd_a_log, v_ssd_d, v_ssd_norm_g, v_hyb_w_out, v_mla_w_in, v_mla_q_norm_g, v_mla_kv_norm_g, v_mla_w_uq, v_mla_w_ukv, v_mla_w_out):
    given = dict(locals())
    w = {n: given[n] for n in WEIGHTS}
    mom = {n: given["m_" + n] for n in WEIGHTS}
    var = {n: given["v_" + n] for n in WEIGHTS}
    shard_shapes = [w[n].shape for n in SHARDED]
    repl_shapes = [w[n].shape for n in REPLICATED]

    pack16 = _pack([w[n].astype(BF16) for n in SHARDED_BF16])
    by_chip = _exchange(pack16, ("x", "y"), "gather", "gather_weights_ici")
    by_core = _exchange(by_chip.reshape(-1, PACK_W), ("c",), "gather", "gather_weights_d2d")
    gathered = by_core.reshape(2, 4, -1, PACK_W).transpose(1, 0, 2, 3).reshape(N_DEV, -1, PACK_W)
    fw = {n: _join_from_devices(a, SHARD_AXIS[n])
          for n, a in zip(SHARDED_BF16, _unpack(gathered, [w[n].shape for n in SHARDED_BF16], (N_DEV,)))}
    small = _exchange(_pack([w[n] for n in SHARDED_F32]), MESH_AXES, "gather", "gather_weights_f32")
    fw.update({n: _join_from_devices(a, SHARD_AXIS[n])
               for n, a in zip(SHARDED_F32, _unpack(small, [w[n].shape for n in SHARDED_F32], (N_DEV,)))})

    loss_local, grad_x, grads = _device_step(x[0], p[:, 0], positions[0], loss_target[0], fw, w)
    loss = lax.psum(loss_local, MESH_AXES)

    per_dev = [_split_for_devices(grads[n], SHARD_AXIS[n]) for n in SHARDED]
    per_dev = [a.reshape(4, 2, *a.shape[1:]).swapaxes(0, 1) for a in per_dev]
    gpack = _pack(per_dev, (2, 4))
    rows = gpack.shape[2]
    pair = _exchange(gpack.reshape(2, 4 * rows, PACK_W), ("c",), "a2a", "reduce_grads_d2d")
    chip_sum = _rowwise(lambda a, b: a + b, [(pair, 0), (pair, 1)], [], [(PACK_W, BF16)], tm=PACK_TM, name="reduce_grads_pair")
    quads = _exchange(chip_sum.reshape(4, rows, PACK_W), ("x", "y"), "a2a", "reduce_grads_ici")
    g_s, d_s, m_s, v_s = _adamw_packed(_pack([w[n] for n in SHARDED]), _pack([mom[n] for n in SHARDED]),
                                       _pack([var[n] for n in SHARDED]), quads, 4, "adamw_sharded")

    rpack = _pack([grads[n].reshape(w[n].shape) for n in REPLICATED])
    everyone = _exchange(rpack, MESH_AXES, "gather", "gather_small_grads")
    g_r, d_r, m_r, v_r = _adamw_packed(_pack([w[n] for n in REPLICATED]), _pack([mom[n] for n in REPLICATED]),
                                       _pack([var[n] for n in REPLICATED]), everyone, N_DEV, "adamw_replicated")

    outs = []
    for sharded_buf, repl_buf in ((g_s, g_r), (d_s, d_r), (m_s, m_r), (v_s, v_r)):
        vals = dict(zip(SHARDED, _unpack(sharded_buf, shard_shapes)))
        vals.update(zip(REPLICATED, _unpack(repl_buf, repl_shapes)))
        outs.extend(vals[n] for n in WEIGHTS)
    return (loss, grad_x[None], *outs)
```

```python
import functools
import math

import jax
import jax.numpy as jnp
import numpy as np
from jax import lax
from jax.experimental import pallas as pl
from jax.experimental.pallas import tpu as pltpu

F32 = jnp.float32
BF16 = jnp.bfloat16
HIGHEST = lax.Precision.HIGHEST

V7X_VMEM_LIMIT_BYTES = 52 * 1024 * 1024
LANES = 128

D_MODEL = 1024
DEPTH = 4
D_FF = 2816
PLE_DIM = 256
NORM_EPS = 1e-6
LN_EPS = 1e-5
CHUNK = 128
GM_HEADS = 8
SSD_HEADS = 16
SSD_HEAD_DIM = 64
SSD_INNER = 1024
SSD_STATE = 128
SSD_BC = 256
SSD_CONV_CH = 1536
HYB_MAIN = 4608
MLA_HEADS = 16
MLA_Q_LORA = 256
MLA_KV_LORA = 128
MLA_ROPE = 64
MLA_IN = 448
MLA_IN_PAD = 512
ATTN_SCALE = 192.0 ** -0.5
LOG2_E = 1.4426950408889634
LN_2 = 0.6931471805599453
ATTN_QSCALE = ATTN_SCALE * LOG2_E
ROPE_BASE = 10000.0

ADAM_LR = 0.001
ADAM_B1 = 0.9
ADAM_B2 = 0.999
ADAM_EPS = 1e-08
ADAM_WD = 0.01
ADAM_STEP = 10

N_DEV = 8
PACK_W = 1024
PACK_TM = 256

WEIGHTS = ['ffn1_pre_g', 'ffn1_w_in', 'ffn1_w_down', 'ffn1_post_g', 'mix_pre_g', 'mix_post_g', 'ffn2_pre_g',
           'ffn2_w_in', 'ffn2_w_down', 'ffn2_post_g', 'ple_pre_g', 'ple_w_gate', 'ple_w_proj', 'ple_post_g',
           'hyb_w_in', 'gm_ln_g', 'gm_ln_b', 'gm_w_s', 'gm_b_s', 'ssd_conv_w', 'ssd_conv_b', 'ssd_dt_bias',
           'ssd_a_log', 'ssd_d', 'ssd_norm_g', 'hyb_w_out', 'mla_w_in', 'mla_q_norm_g', 'mla_kv_norm_g',
           'mla_w_uq', 'mla_w_ukv', 'mla_w_out']
SHARD_AXIS = {'ffn1_w_in': 2, 'ffn1_w_down': 1, 'ffn2_w_in': 2, 'ffn2_w_down': 1, 'ple_w_gate': 1, 'ple_w_proj': 2,
              'hyb_w_in': 2, 'ssd_conv_w': 2, 'hyb_w_out': 1, 'mla_w_in': 1, 'mla_q_norm_g': 1, 'mla_w_uq': 2,
              'mla_w_ukv': 2, 'mla_w_out': 1}
SHARDED = [n for n in WEIGHTS if n in SHARD_AXIS]
REPLICATED = [n for n in WEIGHTS if n not in SHARD_AXIS]
SHARDED_F32 = ['ssd_conv_w', 'mla_q_norm_g']
SHARDED_BF16 = [n for n in SHARDED if n not in SHARDED_F32]


def _params(*sem):
    return pltpu.CompilerParams(dimension_semantics=sem or None, vmem_limit_bytes=V7X_VMEM_LIMIT_BYTES)


def _pick(n, prefs):
    for t in prefs:
        if t <= n and n % t == 0:
            return t
    return n


def _mm(a, b, *, ta=False, tb=False, out_dtype=F32, tm=1024, tn=512, tk=512, name):
    m, k = (a.shape[1], a.shape[0]) if ta else a.shape
    n = b.shape[0] if tb else b.shape[1]
    assert k == (b.shape[1] if tb else b.shape[0]), (a.shape, b.shape, ta, tb)
    tm, tn, tk = _pick(m, (tm, 512, 256, 128)), _pick(n, (tn, 512, 256, 128)), _pick(k, (tk, 512, 256, 128))
    nk = k // tk
    dims = (((0 if ta else 1,), (1 if tb else 0,)), ((), ()))

    def body(a_ref, b_ref, o_ref, *acc):
        part = lax.dot_general(a_ref[...].astype(BF16), b_ref[...].astype(BF16), dims, preferred_element_type=F32)
        if nk == 1:
            o_ref[...] = part.astype(o_ref.dtype)
            return
        acc_ref, = acc
        kk = pl.program_id(2)

        @pl.when(kk == 0)
        def _():
            acc_ref[...] = part

        @pl.when(kk > 0)
        def _():
            acc_ref[...] += part

        @pl.when(kk == nk - 1)
        def _():
            o_ref[...] = acc_ref[...].astype(o_ref.dtype)

    a_spec = pl.BlockSpec((tk, tm), lambda i, j, kk: (kk, i)) if ta else pl.BlockSpec((tm, tk), lambda i, j, kk: (i, kk))
    b_spec = pl.BlockSpec((tn, tk), lambda i, j, kk: (j, kk)) if tb else pl.BlockSpec((tk, tn), lambda i, j, kk: (kk, j))
    return pl.pallas_call(
        body, name=name, grid=(m // tm, n // tn, nk), in_specs=[a_spec, b_spec],
        out_specs=pl.BlockSpec((tm, tn), lambda i, j, kk: (i, j)), out_shape=jax.ShapeDtypeStruct((m, n), out_dtype),
        scratch_shapes=[] if nk == 1 else [pltpu.VMEM((tm, tn), F32)],
        compiler_params=_params("parallel", "parallel", "arbitrary"),
    )(a, b)


def _rowwise(fn, rows, consts, outs, accs=(), *, tm=256, name):
    first = rows[0][0] if isinstance(rows[0], tuple) else rows[0]
    t = first.shape[-2]
    tm = _pick(t, (tm, 256, 128, 64, 32, 16, 8))
    n_r, n_c, n_o = len(rows), len(consts), len(outs)

    def body(*refs):
        vals = [r[...] for r in refs[:n_r + n_c]]
        res = fn(*vals)
        res = res if isinstance(res, tuple) else (res,)
        o_refs, a_refs = refs[n_r + n_c:n_r + n_c + n_o], refs[n_r + n_c + n_o:]
        for o_ref, v in zip(o_refs, res[:n_o]):
            if isinstance(v, (tuple, list)):
                off = 0
                for piece in v:
                    o_ref[:, off:off + piece.shape[1]] = piece.astype(o_ref.dtype)
                    off += piece.shape[1]
            else:
                o_ref[...] = v.astype(o_ref.dtype)
        if a_refs:
            terms = res[n_o:]
            is_first = pl.program_id(0) == 0

            @pl.when(is_first)
            def _():
                for a_ref, v in zip(a_refs, terms):
                    a_ref[...] = v

            @pl.when(jnp.logical_not(is_first))
            def _():
                for a_ref, v in zip(a_refs, terms):
                    a_ref[...] += v

    in_specs, args = [], []
    for r in rows:
        if isinstance(r, tuple):
            arr, slot = r
            in_specs.append(pl.BlockSpec((None, tm, arr.shape[2]), functools.partial(lambda i, s: (s, i, 0), s=slot)))
        else:
            arr = r
            in_specs.append(pl.BlockSpec((tm, arr.shape[1]), lambda i: (i, 0)))
        args.append(arr)
    for c in consts:
        in_specs.append(pl.BlockSpec(c.shape, lambda i: (0, 0)))
        args.append(c)
    out_specs = [pl.BlockSpec((tm, c), lambda i: (i, 0)) for c, _ in outs]
    out_shape = [jax.ShapeDtypeStruct((t, c), dt) for c, dt in outs]
    for shp in accs:
        out_specs.append(pl.BlockSpec(shp, lambda i: (0, 0)))
        out_shape.append(jax.ShapeDtypeStruct(shp, F32))
    res = pl.pallas_call(
        body, name=name, grid=(t // tm,), in_specs=in_specs, out_specs=out_specs, out_shape=out_shape,
        compiler_params=_params("arbitrary" if accs else "parallel"),
    )(*args)
    return res[0] if len(res) == 1 else tuple(res)


def _colsum(v):
    return jnp.sum(v, axis=0, keepdims=True)


def _rms(x, g, eps=NORM_EPS):
    r = lax.rsqrt(jnp.mean(x * x, axis=-1, keepdims=True) + eps)
    return x * r * g


def _rms_bwd(x, g, dy, eps=NORM_EPS):
    r = lax.rsqrt(jnp.mean(x * x, axis=-1, keepdims=True) + eps)
    xh = x * r
    dyg = dy * g
    dx = r * (dyg - xh * jnp.mean(dyg * xh, axis=-1, keepdims=True))
    return dx, dy * xh


def _silu(x):
    return x * jax.nn.sigmoid(x)


def _silu_grad(x):
    s = jax.nn.sigmoid(x)
    return s * (1.0 + x * (1.0 - s))


_GELU_K = math.sqrt(2.0 / math.pi)


def _gelu(x):
    return 0.5 * x * (1.0 + jnp.tanh(_GELU_K * (x + 0.044715 * x * x * x)))


def _gelu_grad(x):
    t = jnp.tanh(_GELU_K * (x + 0.044715 * x * x * x))
    return 0.5 * (1.0 + t) + 0.5 * x * (1.0 - t * t) * _GELU_K * (1.0 + 3.0 * 0.044715 * x * x)


def _prenorm(h, g, name):
    return _rowwise(lambda x, gg: _rms(x, gg), [h], [g], [(D_MODEL, BF16)], name=name)


def _postnorm_residual(h, f, g, scale, name):
    return _rowwise(lambda x, ff, gg: x + scale * _rms(ff, gg), [h, f], [g], [(D_MODEL, F32)], name=name)


def _postnorm_bwd(f, dh, g, scale, name):
    def fn(ff, d, gg):
        dx, dgt = _rms_bwd(ff, gg, scale * d)
        return dx, _colsum(dgt)
    return _rowwise(fn, [f, dh], [g], [(D_MODEL, BF16)], [(1, D_MODEL)], name=name)


def _prenorm_bwd(h, das, dh, g, name):
    n = len(das)

    def fn(x, *rest):
        da = rest[0]
        for extra in rest[1:n]:
            da = da + extra
        d, gg = rest[n], rest[n + 1]
        dx, dgt = _rms_bwd(x, gg, da)
        return d + dx, _colsum(dgt)
    return _rowwise(fn, [h, *das, dh], [g], [(D_MODEL, F32)], [(1, D_MODEL)], name=name)


def _ffn_fwd(h, w, pre_g, post_g, tag):
    a = _prenorm(h, pre_g, tag + "_prenorm")
    gu = _mm(a, w["in"], tm=1024, tn=512, tk=1024, name=tag + "_in")
    s = _rowwise(lambda x: _silu(x[:, :D_FF]) * x[:, D_FF:], [gu], [], [(D_FF, BF16)], name=tag + "_swiglu")
    f = _mm(s, w["down"], tm=1024, tn=1024, tk=D_FF, name=tag + "_down")
    out = _postnorm_residual(h, f, post_g, 0.5, tag + "_postnorm")
    return out, (h, a, gu, s, f)


def _ffn_bwd(dh, saved, w, pre_g, post_g, tag):
    h, a, gu, s, f = saved
    df, d_post = _postnorm_bwd(f, dh, post_g, 0.5, tag + "_postnorm_bwd")
    ds = _mm(df, w["down"], tb=True, tm=1024, tn=1408, tk=1024, name=tag + "_down_dx")
    d_down = _mm(s, df, ta=True, tm=1408, tn=1024, tk=1024, name=tag + "_down_dw")

    def swiglu_bwd(x, d):
        gate, up = x[:, :D_FF], x[:, D_FF:]
        return ((d * up * _silu_grad(gate), d * _silu(gate)),)
    dgu = _rowwise(swiglu_bwd, [gu, ds], [], [(2 * D_FF, BF16)], name=tag + "_swiglu_bwd")
    da = _mm(dgu, w["in"], tb=True, tm=1024, tn=1024, tk=1408, name=tag + "_in_dx")
    d_in = _mm(a, dgu, ta=True, tm=1024, tn=512, tk=2048, name=tag + "_in_dw")
    dh_in, d_pre = _prenorm_bwd(h, [da], dh, pre_g, tag + "_prenorm_bwd")
    return dh_in, dict(w_in=d_in, w_down=d_down, pre_g=d_pre, post_g=d_post)


def _ple_fwd(h, p_i, w, pre_g, post_g):
    a = _prenorm(h, pre_g, "ple_prenorm")
    gl = _mm(a, w["gate"], tm=1024, tn=1024, tk=1024, name="ple_gate")
    e = _mm(p_i, w["proj"], tm=1024, tn=1024, tk=PLE_DIM, name="ple_proj")
    out = _rowwise(lambda x, g_, e_, gg: x + _rms(jax.nn.sigmoid(g_) * e_, gg), [h, gl, e], [post_g],
                   [(D_MODEL, F32)], name="ple_out")
    return out, (h, a, gl, e)


def _ple_bwd(dh, saved, p_i, w, pre_g, post_g):
    h, a, gl, e = saved

    def fn(g_, e_, d, gg):
        sg = jax.nn.sigmoid(g_)
        du, dgt = _rms_bwd(sg * e_, gg, d)
        return du * e_ * sg * (1.0 - sg), du * sg, _colsum(dgt)
    dgl, de, d_post = _rowwise(fn, [gl, e, dh], [post_g], [(D_MODEL, BF16), (D_MODEL, BF16)], [(1, D_MODEL)],
                               name="ple_out_bwd")
    da = _mm(dgl, w["gate"], tb=True, tm=1024, tn=1024, tk=1024, name="ple_gate_dx")
    d_gate = _mm(a, dgl, ta=True, tm=1024, tn=1024, tk=2048, name="ple_gate_dw")
    d_proj = _mm(p_i, de, ta=True, tm=PLE_DIM, tn=1024, tk=2048, name="ple_proj_dw")
    dh_in, d_pre = _prenorm_bwd(h, [da], dh, pre_g, "ple_prenorm_bwd")
    return dh_in, dict(w_gate=d_gate, w_proj=d_proj, pre_g=d_pre, post_g=d_post)


def _gm_layernorm(v, g, b):
    mu = jnp.mean(v, axis=-1, keepdims=True)
    xc = v - mu
    rstd = lax.rsqrt(jnp.mean(xc * xc, axis=-1, keepdims=True) + LN_EPS)
    vhat = xc * rstd
    return vhat, rstd, vhat * g + b


def _gmlp_fwd(proj, wm, bias_t, ln_g, ln_b, name):
    t = proj.shape[0]

    def body(uv_ref, wm_ref, bt_ref, g_ref, b_ref, o_ref):
        for hd in range(GM_HEADS):
            lo = hd * LANES
            u = _gelu(uv_ref[:, lo:lo + LANES])
            v = _gelu(uv_ref[:, 1024 + lo:1024 + lo + LANES])
            _, _, vln = _gm_layernorm(v, g_ref[:, lo:lo + LANES], b_ref[:, lo:lo + LANES])
            mixed = jnp.dot(wm_ref[hd], vln.astype(BF16), preferred_element_type=F32) + bt_ref[:, hd:hd + 1]
            o_ref[:, lo:lo + LANES] = (u * mixed).astype(o_ref.dtype)

    return pl.pallas_call(
        body, name=name, grid=(t // CHUNK,),
        in_specs=[pl.BlockSpec((CHUNK, 2048), lambda i: (i, 0)), pl.BlockSpec(wm.shape, lambda i: (0, 0, 0)),
                  pl.BlockSpec(bias_t.shape, lambda i: (0, 0)), pl.BlockSpec(ln_g.shape, lambda i: (0, 0)),
                  pl.BlockSpec(ln_b.shape, lambda i: (0, 0))],
        out_specs=pl.BlockSpec((CHUNK, 1024), lambda i: (i, 0)), out_shape=jax.ShapeDtypeStruct((t, 1024), BF16),
        compiler_params=_params("parallel"),
    )(proj, wm, bias_t, ln_g, ln_b)


def _gmlp_bwd(proj, dyab, wm, bias_t, ln_g, ln_b, name):
    t = proj.shape[0]
    nc = t // CHUNK

    def body(uv_ref, dy_ref, wm_ref, bt_ref, g_ref, b_ref, duv_ref, dw_ref, db_ref, dg_ref, dbeta_ref, dbacc):
        c = pl.program_id(0)

        @pl.when(c == 0)
        def _():
            dw_ref[...] = jnp.zeros_like(dw_ref)
            dbacc[...] = jnp.zeros_like(dbacc)
            dg_ref[...] = jnp.zeros_like(dg_ref)
            dbeta_ref[...] = jnp.zeros_like(dbeta_ref)

        for hd in range(GM_HEADS):
            lo = hd * LANES
            xu = uv_ref[:, lo:lo + LANES]
            xv = uv_ref[:, 1024 + lo:1024 + lo + LANES]
            u = _gelu(xu)
            g_h = g_ref[:, lo:lo + LANES]
            vhat, rstd, vln = _gm_layernorm(_gelu(xv), g_h, b_ref[:, lo:lo + LANES])
            vln16 = vln.astype(BF16)
            mixed = jnp.dot(wm_ref[hd], vln16, preferred_element_type=F32) + bt_ref[:, hd:hd + 1]
            dy = dy_ref[:, lo:lo + LANES]
            du = dy * mixed
            dmix = dy * u
            dmix16 = dmix.astype(BF16)
            dw_ref[hd] += lax.dot_general(dmix16, vln16, (((1,), (1,)), ((), ())), preferred_element_type=F32)
            dbacc[hd] += dmix
            dvln = lax.dot_general(wm_ref[hd], dmix16, (((0,), (0,)), ((), ())), preferred_element_type=F32)
            dg_ref[:, lo:lo + LANES] += _colsum(dvln * vhat)
            dbeta_ref[:, lo:lo + LANES] += _colsum(dvln)
            dvh = dvln * g_h
            dv = rstd * (dvh - jnp.mean(dvh, axis=-1, keepdims=True)
                         - vhat * jnp.mean(dvh * vhat, axis=-1, keepdims=True))
            duv_ref[:, lo:lo + LANES] = (du * _gelu_grad(xu)).astype(duv_ref.dtype)
            duv_ref[:, 1024 + lo:1024 + lo + LANES] = (dv * _gelu_grad(xv)).astype(duv_ref.dtype)

        @pl.when(c == nc - 1)
        def _():
            row = lax.broadcasted_iota(jnp.int32, (CHUNK, CHUNK), 0)
            col = lax.broadcasted_iota(jnp.int32, (CHUNK, CHUNK), 1)
            for hd in range(GM_HEADS):
                dw_ref[hd] = jnp.where(col <= row, dw_ref[hd], 0.0)
                db_ref[hd] = jnp.sum(dbacc[hd], axis=1, keepdims=True)

    return pl.pallas_call(
        body, name=name, grid=(nc,),
        in_specs=[pl.BlockSpec((CHUNK, 2048), lambda i: (i, 0)), pl.BlockSpec((CHUNK, 1024), lambda i: (i, 0)),
                  pl.BlockSpec(wm.shape, lambda i: (0, 0, 0)), pl.BlockSpec(bias_t.shape, lambda i: (0, 0)),
                  pl.BlockSpec(ln_g.shape, lambda i: (0, 0)), pl.BlockSpec(ln_b.shape, lambda i: (0, 0))],
        out_specs=[pl.BlockSpec((CHUNK, 2048), lambda i: (i, 0)), pl.BlockSpec((GM_HEADS, CHUNK, CHUNK), lambda i: (0, 0, 0)),
                   pl.BlockSpec((GM_HEADS, CHUNK, 1), lambda i: (0, 0, 0)), pl.BlockSpec((1, 1024), lambda i: (0, 0)),
                   pl.BlockSpec((1, 1024), lambda i: (0, 0))],
        out_shape=[jax.ShapeDtypeStruct((t, 2048), BF16), jax.ShapeDtypeStruct((GM_HEADS, CHUNK, CHUNK), F32),
                   jax.ShapeDtypeStruct((GM_HEADS, CHUNK, 1), F32), jax.ShapeDtypeStruct((1, 1024), F32),
                   jax.ShapeDtypeStruct((1, 1024), F32)],
        scratch_shapes=[pltpu.VMEM((GM_HEADS, CHUNK, CHUNK), F32)],
        compiler_params=_params("arbitrary"),
    )(proj, dyab, wm, bias_t, ln_g, ln_b)


def _ssd_chunk_terms(dt_pad, a_pad):
    row = lax.broadcasted_iota(jnp.int32, (CHUNK, CHUNK), 0)
    col = lax.broadcasted_iota(jnp.int32, (CHUNK, CHUNK), 1)
    tril = jnp.where(col <= row, 1.0, 0.0).astype(F32)
    a_cs = jnp.dot(tril, dt_pad * a_pad, precision=HIGHEST, preferred_element_type=F32)
    return a_cs, a_cs.T


def _pair_cols(mat, hd_a, lane_lt64):
    return jnp.where(lane_lt64, mat[:, hd_a:hd_a + 1], mat[:, hd_a + 1:hd_a + 2])


def _head_decay(a_cs, a_cs_t, hd, causal):
    seg = a_cs[:, hd:hd + 1] - a_cs_t[hd:hd + 1, :]
    return jnp.exp(jnp.where(causal, seg, -jnp.inf))


def _ssd_fwd(act, dt_pad, a_pad, d_pad, name):
    t = act.shape[0]
    nc = t // CHUNK

    def body(act_ref, dt_ref, a_ref, d_ref, y_ref, st_ref, h_sc):
        c = pl.program_id(0)

        @pl.when(c == 0)
        def _():
            h_sc[...] = jnp.zeros_like(h_sc)

        st_ref[...] = h_sc[...]
        row = lax.broadcasted_iota(jnp.int32, (CHUNK, CHUNK), 0)
        col = lax.broadcasted_iota(jnp.int32, (CHUNK, CHUNK), 1)
        causal = col <= row
        lane_lt64 = lax.broadcasted_iota(jnp.int32, (CHUNK, LANES), 1) < SSD_HEAD_DIM
        row_lt64 = lax.broadcasted_iota(jnp.int32, (LANES, 1), 0) < SSD_HEAD_DIM
        dt = dt_ref[...]
        a_cs, a_cs_t = _ssd_chunk_terms(dt, a_ref[...])
        last = a_cs[CHUNK - 1:CHUNK, :]
        for g in range(2):
            b16 = act_ref[:, SSD_INNER + g * SSD_STATE:SSD_INNER + (g + 1) * SSD_STATE].astype(BF16)
            c16 = act_ref[:, SSD_INNER + SSD_BC + g * SSD_STATE:SSD_INNER + SSD_BC + (g + 1) * SSD_STATE].astype(BF16)
            cb = lax.dot_general(c16, b16, (((1,), (1,)), ((), ())), preferred_element_type=F32)
            for pr in range(4):
                ha = g * 8 + pr * 2
                lo = ha * SSD_HEAD_DIM
                xs = act_ref[:, lo:lo + LANES]
                xd = xs * _pair_cols(dt, ha, lane_lt64)
                xd16 = xd.astype(BF16)
                ya = jnp.dot((cb * _head_decay(a_cs, a_cs_t, ha, causal)).astype(BF16), xd16, preferred_element_type=F32)
                yb = jnp.dot((cb * _head_decay(a_cs, a_cs_t, ha + 1, causal)).astype(BF16), xd16, preferred_element_type=F32)
                a_p = _pair_cols(a_cs, ha, lane_lt64)
                hp = h_sc[lo:lo + LANES, :]
                y_off = lax.dot_general(c16, hp.astype(BF16), (((1,), (1,)), ((), ())), preferred_element_type=F32)
                d_p = jnp.where(lane_lt64[:1], d_ref[:, ha:ha + 1], d_ref[:, ha + 1:ha + 2])
                y_ref[:, lo:lo + LANES] = jnp.where(lane_lt64, ya, yb) + y_off * jnp.exp(a_p) + d_p * xs
                last_p = jnp.where(lane_lt64[:1], last[:, ha:ha + 1], last[:, ha + 1:ha + 2])
                xw16 = (xd * jnp.exp(last_p - a_p)).astype(BF16)
                s_new = lax.dot_general(xw16, b16, (((0,), (0,)), ((), ())), preferred_element_type=F32)
                t_col = jnp.where(row_lt64, jnp.exp(last[:, ha:ha + 1]), jnp.exp(last[:, ha + 1:ha + 2]))
                h_sc[lo:lo + LANES, :] = t_col * hp + s_new

    return pl.pallas_call(
        body, name=name, grid=(nc,),
        in_specs=[pl.BlockSpec((CHUNK, SSD_CONV_CH), lambda i: (i, 0)), pl.BlockSpec((CHUNK, LANES), lambda i: (i, 0)),
                  pl.BlockSpec((1, LANES), lambda i: (0, 0)), pl.BlockSpec((1, LANES), lambda i: (0, 0))],
        out_specs=[pl.BlockSpec((CHUNK, SSD_INNER), lambda i: (i, 0)),
                   pl.BlockSpec((None, SSD_INNER, SSD_STATE), lambda i: (i, 0, 0))],
        out_shape=[jax.ShapeDtypeStruct((t, SSD_INNER), F32), jax.ShapeDtypeStruct((nc, SSD_INNER, SSD_STATE), F32)],
        scratch_shapes=[pltpu.VMEM((SSD_INNER, SSD_STATE), F32)],
        compiler_params=_params("arbitrary"),
    )(act, dt_pad, a_pad, d_pad)


def _ssd_bwd(act, dt_pad, a_pad, d_pad, states, dy, name):
    t = act.shape[0]
    nc = t // CHUNK

    def body(act_ref, dt_ref, a_ref, d_ref, st_ref, dy_ref, dact_ref, ddt_ref, da_ref, dd_ref, dh_sc):
        c = pl.program_id(0)

        @pl.when(c == 0)
        def _():
            dh_sc[...] = jnp.zeros_like(dh_sc)
            da_ref[...] = jnp.zeros_like(da_ref)
            dd_ref[...] = jnp.zeros_like(dd_ref)

        row = lax.broadcasted_iota(jnp.int32, (CHUNK, CHUNK), 0)
        col = lax.broadcasted_iota(jnp.int32, (CHUNK, CHUNK), 1)
        causal = col <= row
        lane = lax.broadcasted_iota(jnp.int32, (CHUNK, LANES), 1)
        lane_lt64 = lane < SSD_HEAD_DIM
        row_lt64 = lax.broadcasted_iota(jnp.int32, (LANES, 1), 0) < SSD_HEAD_DIM
        is_last = lax.broadcasted_iota(jnp.int32, (CHUNK, 1), 0) == CHUNK - 1
        dt = dt_ref[...]
        a_cs, a_cs_t = _ssd_chunk_terms(dt, a_ref[...])
        last = a_cs[CHUNK - 1:CHUNK, :]
        d_acs = jnp.zeros((CHUNK, LANES), F32)
        ddt_x = jnp.zeros((CHUNK, LANES), F32)
        dd_acc = jnp.zeros((1, LANES), F32)

        def head_sum(v, first):
            return jnp.sum(jnp.where(lane_lt64 if first else jnp.logical_not(lane_lt64), v, 0.0), axis=1, keepdims=True)

        for g in range(2):
            b_lo = SSD_INNER + g * SSD_STATE
            c_lo = SSD_INNER + SSD_BC + g * SSD_STATE
            b16 = act_ref[:, b_lo:b_lo + SSD_STATE].astype(BF16)
            c16 = act_ref[:, c_lo:c_lo + SSD_STATE].astype(BF16)
            cb = lax.dot_general(c16, b16, (((1,), (1,)), ((), ())), preferred_element_type=F32)
            dcb = jnp.zeros((CHUNK, CHUNK), F32)
            db_g = jnp.zeros((CHUNK, SSD_STATE), F32)
            dc_g = jnp.zeros((CHUNK, SSD_STATE), F32)
            for pr in range(4):
                ha = g * 8 + pr * 2
                lo = ha * SSD_HEAD_DIM
                xs = act_ref[:, lo:lo + LANES]
                dt_p = _pair_cols(dt, ha, lane_lt64)
                xd = xs * dt_p
                xd16 = xd.astype(BF16)
                a_p = _pair_cols(a_cs, ha, lane_lt64)
                exp_a = jnp.exp(a_p)
                last_p = jnp.where(lane_lt64[:1], last[:, ha:ha + 1], last[:, ha + 1:ha + 2])
                w_p = jnp.exp(last_p - a_p)
                hp = st_ref[lo:lo + LANES, :]
                hp16 = hp.astype(BF16)
                dhn = dh_sc[lo:lo + LANES, :]
                dhn16 = dhn.astype(BF16)
                dyp = dy_ref[:, lo:lo + LANES]
                d_p = jnp.where(lane_lt64[:1], d_ref[:, ha:ha + 1], d_ref[:, ha + 1:ha + 2])
                dd_acc = dd_acc + jnp.where(lane[:1] == ha, jnp.sum(head_sum(dyp * xs, True), axis=0, keepdims=True), 0.0) \
                    + jnp.where(lane[:1] == ha + 1, jnp.sum(head_sum(dyp * xs, False), axis=0, keepdims=True), 0.0)
                g_off = lax.dot_general(c16, hp16, (((1,), (1,)), ((), ())), preferred_element_type=F32)
                dg16 = (dyp * exp_a).astype(BF16)
                dc_g = dc_g + jnp.dot(dg16, hp16, preferred_element_type=F32)
                dh_prev = lax.dot_general(dg16, c16, (((0,), (0,)), ((), ())), preferred_element_type=F32)
                off_term = dyp * g_off * exp_a
                q = lax.dot_general(b16, dhn16, (((1,), (1,)), ((), ())), preferred_element_type=F32)
                xw16 = (xd * w_p).astype(BF16)
                db_g = db_g + jnp.dot(xw16, dhn16, preferred_element_type=F32)
                dw_term = xd * q * w_p
                dxd = w_p * q
                dt_all = dhn * hp
                dyp16 = dyp.astype(BF16)
                for k, first in ((0, True), (1, False)):
                    hd = ha + k
                    sel = lane_lt64 if first else jnp.logical_not(lane_lt64)
                    decay = _head_decay(a_cs, a_cs_t, hd, causal)
                    m = cb * decay
                    dy_h = jnp.where(sel, dyp16, jnp.zeros_like(dyp16))
                    dm = lax.dot_general(dy_h, xd16, (((1,), (1,)), ((), ())), preferred_element_type=F32)
                    dcb = dcb + dm * decay
                    dseg = dm * m
                    dxd = dxd + jnp.where(sel, lax.dot_general(m.astype(BF16), dyp16, (((0,), (0,)), ((), ())),
                                                               preferred_element_type=F32), 0.0)
                    d_col = jnp.sum(dseg, axis=1, keepdims=True) - jnp.sum(dseg.T, axis=1, keepdims=True)
                    dw_col = head_sum(dw_term, first)
                    d_col = d_col + head_sum(off_term, first) - dw_col
                    t_h = jnp.exp(last[:, hd:hd + 1])
                    dt_sum = jnp.sum(jnp.sum(jnp.where(row_lt64 if first else jnp.logical_not(row_lt64), dt_all, 0.0),
                                             axis=1, keepdims=True), axis=0, keepdims=True)
                    end_term = jnp.sum(dw_col, axis=0, keepdims=True) + dt_sum * t_h
                    d_col = d_col + jnp.where(is_last, end_term, 0.0)
                    d_acs = d_acs + jnp.where(lane == hd, d_col, 0.0)
                t_col = jnp.where(row_lt64, jnp.exp(last[:, ha:ha + 1]), jnp.exp(last[:, ha + 1:ha + 2]))
                dh_sc[lo:lo + LANES, :] = t_col * dhn + dh_prev
                dact_ref[:, lo:lo + LANES] = d_p * dyp + dxd * dt_p
                ddt_all = dxd * xs
                ddt_x = ddt_x + jnp.where(lane == ha, head_sum(ddt_all, True), 0.0) \
                    + jnp.where(lane == ha + 1, head_sum(ddt_all, False), 0.0)
            dcb16 = dcb.astype(BF16)
            dact_ref[:, b_lo:b_lo + SSD_STATE] = db_g + lax.dot_general(dcb16, c16, (((0,), (0,)), ((), ())),
                                                                          preferred_element_type=F32)
            dact_ref[:, c_lo:c_lo + SSD_STATE] = dc_g + jnp.dot(dcb16, b16, preferred_element_type=F32)
        triu = jnp.where(col >= row, 1.0, 0.0).astype(F32)
        dda = jnp.dot(triu, d_acs, precision=HIGHEST, preferred_element_type=F32)
        ddt_ref[...] = dda * a_ref[...] + ddt_x
        da_ref[...] += _colsum(dda * dt)
        dd_ref[...] += dd_acc

    rev = lambda i: (nc - 1 - i, 0)
    return pl.pallas_call(
        body, name=name, grid=(nc,),
        in_specs=[pl.BlockSpec((CHUNK, SSD_CONV_CH), rev), pl.BlockSpec((CHUNK, LANES), rev),
                  pl.BlockSpec((1, LANES), lambda i: (0, 0)), pl.BlockSpec((1, LANES), lambda i: (0, 0)),
                  pl.BlockSpec((None, SSD_INNER, SSD_STATE), lambda i: (nc - 1 - i, 0, 0)),
                  pl.BlockSpec((CHUNK, SSD_INNER), rev)],
        out_specs=[pl.BlockSpec((CHUNK, SSD_CONV_CH), rev), pl.BlockSpec((CHUNK, LANES), rev),
                   pl.BlockSpec((1, LANES), lambda i: (0, 0)), pl.BlockSpec((1, LANES), lambda i: (0, 0))],
        out_shape=[jax.ShapeDtypeStruct((t, SSD_CONV_CH), F32), jax.ShapeDtypeStruct((t, LANES), F32),
                   jax.ShapeDtypeStruct((1, LANES), F32), jax.ShapeDtypeStruct((1, LANES), F32)],
        scratch_shapes=[pltpu.VMEM((SSD_INNER, SSD_STATE), F32)],
        compiler_params=_params("arbitrary"),
    )(act, dt_pad, a_pad, d_pad, states, dy)


def _shift_down(x, k):
    return x if k == 0 else jnp.pad(x, ((k, 0), (0, 0)))[:x.shape[0]]


def _shift_up(x, k):
    return x if k == 0 else jnp.pad(x, ((0, k), (0, 0)))[k:]


def _conv_pre(x0, x1, x2, x3, w, b):
    return x0 * w[0:1] + x1 * w[1:2] + x2 * w[2:3] + x3 * w[3:4] + b


def _rope128(x, cpad, s_lo, s_hi):
    return x * cpad + pltpu.roll(x, 96, 1) * s_lo + pltpu.roll(x, 32, 1) * s_hi


ATTN_ROW_SPLIT = 2


def _diag_mask(rows, cols, row0):
    return lax.broadcasted_iota(jnp.int32, (rows, cols), 1) <= row0 + lax.broadcasted_iota(jnp.int32, (rows, cols), 0)


def _attn_scores(q, k):
    return lax.dot_general(q, k, (((1,), (1,)), ((), ())), preferred_element_type=F32)


def _causal_pairs(nq, by_key):
    if by_key:
        pairs = [(i, j) for j in range(nq) for i in range(j, nq)]
    else:
        pairs = [(i, j) for i in range(nq) for j in range(i + 1)]
    return (jnp.asarray([pr[0] for pr in pairs], jnp.int32), jnp.asarray([pr[1] for pr in pairs], jnp.int32))


def _attn_fwd(qf, kf, kvf, *, tq, name):
    t = qf.shape[0]
    nq = t // tq
    tk = tq
    qi, kj = _causal_pairs(nq, by_key=False)
    rs = tq // ATTN_ROW_SPLIT

    def body(qi_ref, kj_ref, q_ref, k_ref, v_ref, o_ref, lse_ref, m_sc, l_sc, acc_sc):
        pp = pl.program_id(1)
        i, j = qi_ref[pp], kj_ref[pp]

        @pl.when(j == 0)
        def _():
            m_sc[...] = jnp.full_like(m_sc, -jnp.inf)
            l_sc[...] = jnp.zeros_like(l_sc)
            acc_sc[...] = jnp.zeros_like(acc_sc)

        def update(diag):
            for r in range(ATTN_ROW_SPLIT):
                rows = slice(r * rs, (r + 1) * rs)
                s = _attn_scores(q_ref[rows, :], k_ref[...])
                if diag:
                    s = jnp.where(_diag_mask(rs, tk, r * rs), s, -jnp.inf)
                m_prev = m_sc[rows, :]
                m_new = jnp.maximum(m_prev, jnp.max(s, axis=1, keepdims=True))
                p = jnp.exp2(s - m_new)
                alpha = jnp.exp2(m_prev - m_new)
                l_new = alpha * l_sc[rows, :] + jnp.sum(p, axis=1, keepdims=True)
                acc = alpha * acc_sc[rows, :] + jnp.dot(p.astype(BF16), v_ref[...], preferred_element_type=F32)
                if diag:
                    o_ref[rows, :] = (acc / l_new).astype(o_ref.dtype)
                    lse_ref[rows, :] = m_new + jnp.log2(l_new)
                else:
                    l_sc[rows, :] = l_new
                    acc_sc[rows, :] = acc
                    m_sc[rows, :] = m_new

        @pl.when(j < i)
        def _():
            update(False)

        @pl.when(j == i)
        def _():
            update(True)

    return pl.pallas_call(
        body, name=name,
        grid_spec=pltpu.PrefetchScalarGridSpec(
            num_scalar_prefetch=2, grid=(MLA_HEADS, int(qi.shape[0])),
            in_specs=[pl.BlockSpec((tq, 2 * LANES), lambda h, pp, qi_, kj_: (qi_[pp], h)),
                      pl.BlockSpec((tk, 2 * LANES), lambda h, pp, qi_, kj_: (kj_[pp], h)),
                      pl.BlockSpec((tk, LANES), lambda h, pp, qi_, kj_: (kj_[pp], 2 * h + 1))],
            out_specs=[pl.BlockSpec((tq, LANES), lambda h, pp, qi_, kj_: (qi_[pp], h)),
                       pl.BlockSpec((None, tq, 1), lambda h, pp, qi_, kj_: (h, qi_[pp], 0))],
            scratch_shapes=[pltpu.VMEM((tq, 1), F32), pltpu.VMEM((tq, 1), F32), pltpu.VMEM((tq, LANES), F32)]),
        out_shape=[jax.ShapeDtypeStruct((t, MLA_HEADS * LANES), BF16), jax.ShapeDtypeStruct((MLA_HEADS, t, 1), F32)],
        compiler_params=_params("parallel", "arbitrary"),
    )(qi, kj, qf, kf, kvf)


def _attn_bwd_dq(qf, kf, kvf, o, do, lse, *, tq, name):
    t = qf.shape[0]
    nq = t // tq
    tk = tq
    qi, kj = _causal_pairs(nq, by_key=False)
    rs = tq // ATTN_ROW_SPLIT

    def body(qi_ref, kj_ref, q_ref, k_ref, v_ref, o_ref, do_ref, lse_ref, dq_ref, acc_sc, delta_sc):
        pp = pl.program_id(1)
        i, j = qi_ref[pp], kj_ref[pp]

        @pl.when(j == 0)
        def _():
            acc_sc[...] = jnp.zeros_like(acc_sc)
            delta_sc[...] = jnp.sum(do_ref[...].astype(F32) * o_ref[...].astype(F32), axis=1, keepdims=True)

        def update(diag):
            for r in range(ATTN_ROW_SPLIT):
                rows = slice(r * rs, (r + 1) * rs)
                s = _attn_scores(q_ref[rows, :], k_ref[...])
                p = jnp.exp2(s - lse_ref[rows, :])
                if diag:
                    p = jnp.where(_diag_mask(rs, tk, r * rs), p, 0.0)
                dp = lax.dot_general(do_ref[rows, :], v_ref[...], (((1,), (1,)), ((), ())), preferred_element_type=F32)
                ds = (p * (dp - delta_sc[rows, :])).astype(BF16)
                dq = acc_sc[rows, :] + jnp.dot(ds, k_ref[...], preferred_element_type=F32)
                if diag:
                    dq_ref[rows, :] = dq * ATTN_SCALE
                else:
                    acc_sc[rows, :] = dq

        @pl.when(j < i)
        def _():
            update(False)

        @pl.when(j == i)
        def _():
            update(True)

    qblk = lambda c: (lambda h, pp, qi_, kj_: (qi_[pp], c(h)))
    kblk = lambda c: (lambda h, pp, qi_, kj_: (kj_[pp], c(h)))
    return pl.pallas_call(
        body, name=name,
        grid_spec=pltpu.PrefetchScalarGridSpec(
            num_scalar_prefetch=2, grid=(MLA_HEADS, int(qi.shape[0])),
            in_specs=[pl.BlockSpec((tq, 2 * LANES), qblk(lambda h: h)), pl.BlockSpec((tk, 2 * LANES), kblk(lambda h: h)),
                      pl.BlockSpec((tk, LANES), kblk(lambda h: 2 * h + 1)),
                      pl.BlockSpec((tq, LANES), qblk(lambda h: h)), pl.BlockSpec((tq, LANES), qblk(lambda h: h)),
                      pl.BlockSpec((None, tq, 1), lambda h, pp, qi_, kj_: (h, qi_[pp], 0))],
            out_specs=pl.BlockSpec((tq, 2 * LANES), qblk(lambda h: h)),
            scratch_shapes=[pltpu.VMEM((tq, 2 * LANES), F32), pltpu.VMEM((tq, 1), F32)]),
        out_shape=jax.ShapeDtypeStruct((t, MLA_HEADS * 2 * LANES), F32),
        compiler_params=_params("parallel", "arbitrary"),
    )(qi, kj, qf, kf, kvf, o, do, lse)


def _attn_bwd_dkv(qf, kf, kvf, o, do, lse, *, tq, name):
    t = qf.shape[0]
    nq = t // tq
    tk = tq
    qi, kj = _causal_pairs(nq, by_key=True)
    rs = tq // ATTN_ROW_SPLIT

    def body(qi_ref, kj_ref, q_ref, k_ref, v_ref, o_ref, do_ref, lse_ref, dkv_ref, dkr_ref, dk_sc, dv_sc):
        pp = pl.program_id(1)
        i, j = qi_ref[pp], kj_ref[pp]
        tn = (((0,), (0,)), ((), ()))

        def update(diag):
            dv = dk = None
            for r in range(ATTN_ROW_SPLIT):
                rows = slice(r * rs, (r + 1) * rs)
                do_ = do_ref[rows, :]
                delta = jnp.sum(do_.astype(F32) * o_ref[rows, :].astype(F32), axis=1, keepdims=True)
                s = _attn_scores(q_ref[rows, :], k_ref[...])
                p = jnp.exp2(s - lse_ref[rows, :])
                if diag:
                    p = jnp.where(_diag_mask(rs, tk, r * rs), p, 0.0)
                dp = lax.dot_general(do_, v_ref[...], (((1,), (1,)), ((), ())), preferred_element_type=F32)
                ds = (p * (dp - delta)).astype(BF16)
                parts = (lax.dot_general(p.astype(BF16), do_, tn, preferred_element_type=F32),
                         lax.dot_general(ds, q_ref[rows, :], tn, preferred_element_type=F32))
                dv, dk = parts if dv is None else (dv + parts[0], dk + parts[1])
            if diag:
                dv_sc[...] = dv
                dk_sc[...] = dk
            else:
                dv_sc[...] += dv
                dk_sc[...] += dk

        @pl.when(i > j)
        def _():
            update(False)

        @pl.when(i == j)
        def _():
            update(True)

        @pl.when(i == nq - 1)
        def _():
            dkv_ref[:, :LANES] = (dk_sc[:, :LANES] * LN_2).astype(dkv_ref.dtype)
            dkv_ref[:, LANES:] = dv_sc[...].astype(dkv_ref.dtype)
            dkr_ref[...] = dk_sc[:, LANES:] * LN_2

    qblk = lambda c: (lambda h, pp, qi_, kj_: (qi_[pp], c(h)))
    kblk = lambda c: (lambda h, pp, qi_, kj_: (kj_[pp], c(h)))
    return pl.pallas_call(
        body, name=name,
        grid_spec=pltpu.PrefetchScalarGridSpec(
            num_scalar_prefetch=2, grid=(MLA_HEADS, int(qi.shape[0])),
            in_specs=[pl.BlockSpec((tq, 2 * LANES), qblk(lambda h: h)), pl.BlockSpec((tk, 2 * LANES), kblk(lambda h: h)),
                      pl.BlockSpec((tk, LANES), kblk(lambda h: 2 * h + 1)),
                      pl.BlockSpec((tq, LANES), qblk(lambda h: h)), pl.BlockSpec((tq, LANES), qblk(lambda h: h)),
                      pl.BlockSpec((None, tq, 1), lambda h, pp, qi_, kj_: (h, qi_[pp], 0))],
            out_specs=[pl.BlockSpec((tk, 2 * LANES), kblk(lambda h: h)), pl.BlockSpec((tk, LANES), kblk(lambda h: h))],
            scratch_shapes=[pltpu.VMEM((tk, 2 * LANES), F32), pltpu.VMEM((tk, LANES), F32)]),
        out_shape=[jax.ShapeDtypeStruct((t, MLA_HEADS * 2 * LANES), BF16), jax.ShapeDtypeStruct((t, MLA_HEADS * LANES), F32)],
        compiler_params=_params("parallel", "arbitrary"),
    )(qi, kj, qf, kf, kvf, o, do, lse)


def _rope_tables(positions):
    t = positions.shape[0]
    inv = 1.0 / (ROPE_BASE ** (jnp.arange(0, MLA_ROPE, 2, dtype=F32) / MLA_ROPE))
    ang = positions.astype(F32)[:, None] * inv
    cos, sin = jnp.cos(ang), jnp.sin(ang)
    z32, z64 = jnp.zeros((t, 32), F32), jnp.zeros((t, 64), F32)
    cpad = jnp.concatenate([cos, cos, z64], axis=1)
    s_lo = jnp.concatenate([-sin, z32, z64], axis=1)
    s_hi = jnp.concatenate([z32, sin, z64], axis=1)
    return cpad, s_lo, s_hi


def _mla_fwd(h, w, pre_g, post_g, rope, tq):
    cpad, s_lo, s_hi = rope
    hn = _prenorm(h, pre_g, "mla_prenorm")
    cin = _mm(hn, w["in"], tm=1024, tn=512, tk=1024, name="mla_in")

    def lat(c, cp, sl, sh, qg, kvg):
        cq, ckv, kr = c[:, :MLA_Q_LORA], c[:, MLA_Q_LORA:MLA_Q_LORA + MLA_KV_LORA], c[:, MLA_Q_LORA + MLA_KV_LORA:]
        return _rms(cq, qg), _rms(ckv, kvg), _rope128(kr, cp, sl, sh)
    cqn, ckvn, kr = _rowwise(lat, [cin, cpad, s_lo, s_hi], [w["q_norm_g"], w["kv_norm_g"]],
                             [(MLA_Q_LORA, BF16), (MLA_KV_LORA, BF16), (LANES, BF16)], name="mla_latent")
    q_raw = _mm(cqn, w["uq"], tm=1024, tn=1024, tk=MLA_Q_LORA, name="mla_uq")

    def rope_q(q, cp, sl, sh):
        pieces = []
        for hd in range(MLA_HEADS):
            pieces.append(q[:, 256 * hd:256 * hd + LANES] * ATTN_QSCALE)
            pieces.append(_rope128(q[:, 256 * hd + LANES:256 * hd + 256], cp, sl, sh) * ATTN_QSCALE)
        return (tuple(pieces),)
    qf = _rowwise(rope_q, [q_raw, cpad, s_lo, s_hi], [], [(4096, BF16)], name="mla_rope_q")
    kvf = _mm(ckvn, w["ukv"], out_dtype=BF16, tm=1024, tn=1024, tk=MLA_KV_LORA, name="mla_ukv")
    t = h.shape[0]
    k_nope = kvf.reshape(t, MLA_HEADS, 2 * LANES)[:, :, :LANES]
    kf = jnp.concatenate([k_nope, jnp.broadcast_to(kr[:, None, :], k_nope.shape)], axis=2).reshape(t, MLA_HEADS * 2 * LANES)
    o, lse = _attn_fwd(qf, kf, kvf, tq=tq, name="mla_attn")
    mixed = _mm(o, w["out"], tm=1024, tn=1024, tk=2048, name="mla_out")
    out = _postnorm_residual(h, mixed, post_g, 1.0, "mla_postnorm")
    return out, (h, hn, cin, cqn, ckvn, qf, kf, kvf, o, lse, mixed)


def _mla_bwd(dh, saved, w, pre_g, post_g, rope, tq):
    cpad, s_lo, s_hi = rope
    h, hn, cin, cqn, ckvn, qf, kf, kvf, o, lse, mixed = saved
    dmixed, d_post = _postnorm_bwd(mixed, dh, post_g, 1.0, "mla_postnorm_bwd")
    do = _mm(dmixed, w["out"], tb=True, out_dtype=BF16, tm=1024, tn=1024, tk=1024, name="mla_out_dx")
    d_out = _mm(o, dmixed, ta=True, tm=1024, tn=1024, tk=2048, name="mla_out_dw")
    dq = _attn_bwd_dq(qf, kf, kvf, o, do, lse, tq=tq, name="mla_attn_dq")
    dkvf, dkr_heads = _attn_bwd_dkv(qf, kf, kvf, o, do, lse, tq=tq, name="mla_attn_dkv")

    def unrope_q(d, cp, sl, sh):
        pieces = []
        for hd in range(MLA_HEADS):
            pieces.append(d[:, 256 * hd:256 * hd + LANES])
            pieces.append(_rope128(d[:, 256 * hd + LANES:256 * hd + 256], cp, -sl, -sh))
        return (tuple(pieces),)
    dq_raw = _rowwise(unrope_q, [dq, cpad, s_lo, s_hi], [], [(4096, BF16)], name="mla_rope_q_bwd")
    dcqn = _mm(dq_raw, w["uq"], tb=True, tm=1024, tn=256, tk=1024, name="mla_uq_dx")
    d_uq = _mm(cqn, dq_raw, ta=True, tm=256, tn=1024, tk=2048, name="mla_uq_dw")
    dckvn = _mm(dkvf, w["ukv"], tb=True, tm=1024, tn=128, tk=1024, name="mla_ukv_dx")
    d_ukv = _mm(ckvn, dkvf, ta=True, tm=128, tn=1024, tk=2048, name="mla_ukv_dw")

    def lat_bwd(c, dq_, dkv_, dkrh, cp, sl, sh, qg, kvg):
        cq, ckv = c[:, :MLA_Q_LORA], c[:, MLA_Q_LORA:MLA_Q_LORA + MLA_KV_LORA]
        dcq, dqg = _rms_bwd(cq, qg, dq_)
        dckv, dkvg = _rms_bwd(ckv, kvg, dkv_)
        dkr = dkrh[:, :LANES]
        for hd in range(1, MLA_HEADS):
            dkr = dkr + dkrh[:, hd * LANES:(hd + 1) * LANES]
        return (dcq, dckv, _rope128(dkr, cp, -sl, -sh)), _colsum(dqg), _colsum(dkvg)
    dcin, d_qg, d_kvg = _rowwise(lat_bwd, [cin, dcqn, dckvn, dkr_heads, cpad, s_lo, s_hi], [w["q_norm_g"], w["kv_norm_g"]],
                                 [(MLA_IN_PAD, BF16)], [(1, MLA_Q_LORA), (1, MLA_KV_LORA)], name="mla_latent_bwd")
    dhn = _mm(dcin, w["in"], tb=True, tm=1024, tn=1024, tk=512, name="mla_in_dx")
    d_in = _mm(hn, dcin, ta=True, tm=1024, tn=512, tk=2048, name="mla_in_dw")
    dh_in, d_pre = _prenorm_bwd(h, [dhn], dh, pre_g, "mla_prenorm_bwd")
    return dh_in, dict(w_in=d_in, q_norm_g=d_qg, kv_norm_g=d_kvg, w_uq=d_uq, w_ukv=d_ukv, w_out=d_out,
                       pre_g=d_pre, post_g=d_post)


def _hyb_fwd(h, w, pre_g, post_g):
    hn = _prenorm(h, pre_g, "hyb_prenorm")
    proj = _mm(hn, w["main"], tm=1024, tn=512, tk=1024, name="hyb_in")
    dtr = _mm(hn, w["dt"], tm=1024, tn=LANES, tk=1024, name="hyb_in_dt")
    ya = _gmlp_fwd(proj, w["gm_w"], w["gm_bt"], w["gm_ln_g"], w["gm_ln_b"], "gmlp")
    xbc = proj[:, 3072:]
    xsh = [_shift_down(xbc, 3 - k) for k in range(4)]
    act = _rowwise(lambda x0, x1, x2, x3, cw, cb: _silu(_conv_pre(x0, x1, x2, x3, cw, cb)), xsh,
                   [w["conv_w"], w["conv_b"]], [(SSD_CONV_CH, F32)], name="ssd_conv")
    dt_pad = _rowwise(lambda d, b: jax.nn.softplus(d + b), [dtr], [w["dt_bias"]], [(LANES, F32)], name="ssd_dt")
    y, states = _ssd_fwd(act, dt_pad, w["a"], w["d"], "ssd_scan")

    def gate_norm(y_, p_, ng):
        yg = y_ * _silu(p_[:, 2048:3072])
        return ((_rms(yg[:, :512], ng[:, :512]), _rms(yg[:, 512:], ng[:, 512:])),)
    yb = _rowwise(gate_norm, [y, proj], [w["norm_g"]], [(SSD_INNER, BF16)], name="ssd_gate_norm")
    yab = jnp.concatenate([ya, yb], axis=1)
    mixed = _mm(yab, w["out"], tm=1024, tn=1024, tk=2048, name="hyb_out")
    out = _postnorm_residual(h, mixed, post_g, 1.0, "hyb_postnorm")
    return out, (h, hn, proj, dtr, xsh, act, dt_pad, y, states, yab, mixed)


def _hyb_bwd(dh, saved, w, pre_g, post_g):
    h, hn, proj, dtr, xsh, act, dt_pad, y, states, yab, mixed = saved
    dmixed, d_post = _postnorm_bwd(mixed, dh, post_g, 1.0, "hyb_postnorm_bwd")
    dyab = _mm(dmixed, w["out"], tb=True, tm=1024, tn=1024, tk=1024, name="hyb_out_dx")
    d_out = _mm(yab, dmixed, ta=True, tm=1024, tn=1024, tk=2048, name="hyb_out_dw")

    def gate_norm_bwd(y_, p_, d, ng):
        z = p_[:, 2048:3072]
        sz = _silu(z)
        yg = y_ * sz
        d_lo, g_lo = _rms_bwd(yg[:, :512], ng[:, :512], d[:, 1024:1536])
        d_hi, g_hi = _rms_bwd(yg[:, 512:], ng[:, 512:], d[:, 1536:])
        dyg = jnp.concatenate([d_lo, d_hi], axis=1)
        return dyg * sz, dyg * y_ * _silu_grad(z), _colsum(jnp.concatenate([g_lo, g_hi], axis=1))
    dy, dz, d_norm = _rowwise(gate_norm_bwd, [y, proj, dyab], [w["norm_g"]], [(SSD_INNER, F32), (SSD_INNER, BF16)],
                              [(1, SSD_INNER)], name="ssd_gate_norm_bwd")
    dact, ddt, da_sum, dd_sum = _ssd_bwd(act, dt_pad, w["a"], w["d"], states, dy, "ssd_scan_bwd")

    def conv_bwd(x0, x1, x2, x3, da_, cw, cb):
        dpre = da_ * _silu_grad(_conv_pre(x0, x1, x2, x3, cw, cb))
        dw = jnp.concatenate([_colsum(dpre * x0), _colsum(dpre * x1), _colsum(dpre * x2), _colsum(dpre * x3)], axis=0)
        return dpre, dw, _colsum(dpre)
    dconv, d_conv_w, d_conv_b = _rowwise(conv_bwd, [*xsh, dact], [w["conv_w"], w["conv_b"]], [(SSD_CONV_CH, F32)],
                                         [(4, SSD_CONV_CH), (1, SSD_CONV_CH)], name="ssd_conv_bwd")
    dsh = [_shift_up(dconv, 3 - k) for k in range(4)]
    dxbc = _rowwise(lambda d0, d1, d2, d3, cw: d0 * cw[0:1] + d1 * cw[1:2] + d2 * cw[2:3] + d3 * cw[3:4], dsh,
                    [w["conv_w"]], [(SSD_CONV_CH, BF16)], name="ssd_conv_dx")

    def dt_bwd(dd, d, b):
        g = dd * jax.nn.sigmoid(d + b)
        g = jnp.where(lax.broadcasted_iota(jnp.int32, g.shape, 1) < SSD_HEADS, g, 0.0)
        return g, _colsum(g)
    ddtr, d_dt_bias = _rowwise(dt_bwd, [ddt, dtr], [w["dt_bias"]], [(LANES, BF16)], [(1, LANES)], name="ssd_dt_bwd")
    duv, d_gm_w, d_gm_b, d_ln_g, d_ln_b = _gmlp_bwd(proj, dyab, w["gm_w"], w["gm_bt"], w["gm_ln_g"], w["gm_ln_b"],
                                                    "gmlp_bwd")
    dproj = jnp.concatenate([duv, dz, dxbc], axis=1)
    dhn_a = _mm(dproj, w["main"], tb=True, tm=1024, tn=1024, tk=1536, name="hyb_in_dx")
    dhn_b = _mm(ddtr, w["dt"], tb=True, tm=1024, tn=1024, tk=LANES, name="hyb_in_dt_dx")
    d_main = _mm(hn, dproj, ta=True, tm=1024, tn=512, tk=2048, name="hyb_in_dw")
    d_dt = _mm(hn, ddtr, ta=True, tm=1024, tn=LANES, tk=2048, name="hyb_in_dt_dw")
    dh_in, d_pre = _prenorm_bwd(h, [dhn_a, dhn_b], dh, pre_g, "hyb_prenorm_bwd")
    grads = dict(w_in=jnp.concatenate([d_main, d_dt[:, :SSD_HEADS]], axis=1), gm_ln_g=d_ln_g, gm_ln_b=d_ln_b,
                 gm_w_s=d_gm_w, gm_b_s=d_gm_b[:, :, 0], conv_w=d_conv_w, conv_b=d_conv_b,
                 dt_bias=d_dt_bias[:, :SSD_HEADS], a_log=(da_sum * w["a"])[:, :SSD_HEADS], d=dd_sum[:, :SSD_HEADS],
                 norm_g=d_norm, w_out=d_out, pre_g=d_pre, post_g=d_post)
    return dh_in, grads


def _row(v):
    return v.reshape(1, -1).astype(F32)


def _pad_lanes(v, n=LANES):
    v = _row(v)
    return jnp.pad(v, ((0, 0), (0, n - v.shape[1])))


def _layer_weights(fw, sm, i):
    j = i // 2
    lw = dict(
        ffn1=dict({"in": fw["ffn1_w_in"][i], "down": fw["ffn1_w_down"][i]}),
        ffn2=dict({"in": fw["ffn2_w_in"][i], "down": fw["ffn2_w_down"][i]}),
        ple=dict(gate=fw["ple_w_gate"][i], proj=fw["ple_w_proj"][i]),
    )
    if i % 2 == 0:
        w_in = fw["hyb_w_in"][j]
        causal = jnp.tril(jnp.ones((CHUNK, CHUNK), dtype=bool))
        lw["mix"] = {
            "main": w_in[:, :HYB_MAIN], "dt": jnp.pad(w_in[:, HYB_MAIN:], ((0, 0), (0, LANES - SSD_HEADS))),
            "gm_w": jnp.where(causal[None], sm["gm_w_s"][j], 0.0).astype(BF16),
            "gm_bt": jnp.pad(sm["gm_b_s"][j].T, ((0, 0), (0, LANES - GM_HEADS))),
            "gm_ln_g": _row(sm["gm_ln_g"][j]), "gm_ln_b": _row(sm["gm_ln_b"][j]),
            "conv_w": fw["ssd_conv_w"][j], "conv_b": _row(sm["ssd_conv_b"][j]),
            "dt_bias": _pad_lanes(sm["ssd_dt_bias"][j]), "a": _pad_lanes(-jnp.exp(sm["ssd_a_log"][j])),
            "d": _pad_lanes(sm["ssd_d"][j]), "norm_g": _row(sm["ssd_norm_g"][j]), "out": fw["hyb_w_out"][j],
        }
    else:
        uq = fw["mla_w_uq"][j].reshape(MLA_Q_LORA, MLA_HEADS, 192)
        uq = jnp.pad(uq, ((0, 0), (0, 0), (0, 64))).reshape(MLA_Q_LORA, MLA_HEADS * 256)
        lw["mix"] = {
            "in": jnp.pad(fw["mla_w_in"][j], ((0, 0), (0, MLA_IN_PAD - MLA_IN))), "uq": uq, "ukv": fw["mla_w_ukv"][j],
            "out": fw["mla_w_out"][j], "q_norm_g": _row(fw["mla_q_norm_g"][j]), "kv_norm_g": _row(sm["mla_kv_norm_g"][j]),
        }
    return lw


def _device_step(x, p, positions, target, fw, sm):
    t = x.shape[0]
    tq = _pick(t, (512, 256, 128))
    rope = _rope_tables(positions)
    h = x
    saved, lws = [], []
    for i in range(DEPTH):
        lw = _layer_weights(fw, sm, i)
        lws.append(lw)
        h, s1 = _ffn_fwd(h, lw["ffn1"], _row(sm["ffn1_pre_g"][i]), _row(sm["ffn1_post_g"][i]), "ffn")
        if i % 2 == 0:
            h, s2 = _hyb_fwd(h, lw["mix"], _row(sm["mix_pre_g"][i]), _row(sm["mix_post_g"][i]))
        else:
            h, s2 = _mla_fwd(h, lw["mix"], _row(sm["mix_pre_g"][i]), _row(sm["mix_post_g"][i]), rope, tq)
        h, s3 = _ffn_fwd(h, lw["ffn2"], _row(sm["ffn2_pre_g"][i]), _row(sm["ffn2_post_g"][i]), "ffn")
        h, s4 = _ple_fwd(h, p[i], lw["ple"], _row(sm["ple_pre_g"][i]), _row(sm["ple_post_g"][i]))
        saved.append((s1, s2, s3, s4))

    def loss_fn(y, tg):
        err = y - tg
        return err * (1.0 / D_MODEL), jnp.sum(_colsum(err * err), axis=1, keepdims=True)
    dh, loss_sum = _rowwise(loss_fn, [h, target], [], [(D_MODEL, F32)], [(1, 1)], name="loss")
    loss = loss_sum[0, 0] * (0.5 / D_MODEL)

    per_layer = {n: [None] * DEPTH for n in WEIGHTS if n.startswith(("ffn", "mix", "ple"))}
    per_mixer = {n: [None] * (DEPTH // 2) for n in WEIGHTS if n.startswith(("hyb", "gm", "ssd", "mla"))}
    for i in reversed(range(DEPTH)):
        lw = lws[i]
        s1, s2, s3, s4 = saved[i]
        j = i // 2
        dh, g = _ple_bwd(dh, s4, p[i], lw["ple"], _row(sm["ple_pre_g"][i]), _row(sm["ple_post_g"][i]))
        for k, v in g.items():
            per_layer["ple_" + k][i] = v
        dh, g = _ffn_bwd(dh, s3, lw["ffn2"], _row(sm["ffn2_pre_g"][i]), _row(sm["ffn2_post_g"][i]), "ffn")
        for k, v in g.items():
            per_layer["ffn2_" + k][i] = v
        if i % 2 == 0:
            dh, g = _hyb_bwd(dh, s2, lw["mix"], _row(sm["mix_pre_g"][i]), _row(sm["mix_post_g"][i]))
            names = dict(w_in="hyb_w_in", gm_ln_g="gm_ln_g", gm_ln_b="gm_ln_b", gm_w_s="gm_w_s", gm_b_s="gm_b_s",
                         conv_w="ssd_conv_w", conv_b="ssd_conv_b", dt_bias="ssd_dt_bias", a_log="ssd_a_log", d="ssd_d",
                         norm_g="ssd_norm_g", w_out="hyb_w_out")
        else:
            dh, g = _mla_bwd(dh, s2, lw["mix"], _row(sm["mix_pre_g"][i]), _row(sm["mix_post_g"][i]), rope, tq)
            g["w_in"] = g["w_in"][:, :MLA_IN]
            g["w_uq"] = g["w_uq"].reshape(MLA_Q_LORA, MLA_HEADS, 256)[:, :, :192].reshape(MLA_Q_LORA, MLA_HEADS * 192)
            names = dict(w_in="mla_w_in", q_norm_g="mla_q_norm_g", kv_norm_g="mla_kv_norm_g", w_uq="mla_w_uq",
                         w_ukv="mla_w_ukv", w_out="mla_w_out")
        per_layer["mix_pre_g"][i] = g.pop("pre_g")
        per_layer["mix_post_g"][i] = g.pop("post_g")
        for k, v in g.items():
            per_mixer[names[k]][j] = v
        dh, g = _ffn_bwd(dh, s1, lw["ffn1"], _row(sm["ffn1_pre_g"][i]), _row(sm["ffn1_post_g"][i]), "ffn")
        for k, v in g.items():
            per_layer["ffn1_" + k][i] = v

    grads = {}
    for n, parts in {**per_layer, **per_mixer}.items():
        stacked = jnp.stack(parts, axis=0)
        if stacked.ndim == 3 and stacked.shape[1] == 1:
            stacked = stacked[:, 0]
        grads[n] = stacked
    return loss, dh, grads


MESH_AXES = ("x", "y", "c")
EXCHANGE_MAX_COPIES = 56


def _exchange(src, axes, mode, name):
    n = 2 ** len(axes)
    blk = src.shape[-2:]
    flips = [tuple(a for a, bit in zip(axes, np.binary_repr(f, len(axes))) if bit == "1") for f in range(1, n)]
    prefs = tuple(c for c in (16, 8, 4, 2, 1) if c * (n - 1) <= EXCHANGE_MAX_COPIES)
    pieces = _pick(blk[0] // 16, prefs) if blk[0] % 16 == 0 else 1
    rows = blk[0] // pieces

    def index(where):
        idx = 0
        for a in axes:
            idx = idx * 2 + where[a]
        return idx

    me_out = index({a: lax.axis_index(a) for a in MESH_AXES})
    own = lax.dynamic_index_in_dim(src, me_out, 0, keepdims=False) if mode == "a2a" else src
    landing = lax.dynamic_update_index_in_dim(jnp.zeros((n, *blk), src.dtype), own, me_out, 0)

    def body(src_ref, landing_ref, out_ref, send_sems, recv_sems):
        del landing_ref
        pos = {a: lax.axis_index(a) for a in MESH_AXES}
        me = index(pos)
        copies = []
        for k, flip in enumerate(flips):
            peer = {a: (1 - pos[a]) if a in flip else pos[a] for a in MESH_AXES}
            payload = src_ref.at[index(peer)] if mode == "a2a" else src_ref
            for q in range(pieces):
                part = pl.ds(q * rows, rows)
                cp = pltpu.make_async_remote_copy(
                    src_ref=payload.at[part], dst_ref=out_ref.at[me, part], send_sem=send_sems.at[k * pieces + q],
                    recv_sem=recv_sems.at[k * pieces + q], device_id=(peer["x"], peer["y"], peer["c"]),
                    device_id_type=pl.DeviceIdType.MESH)
                cp.start()
                copies.append(cp)
        for cp in copies:
            cp.wait()

    n_sems = (n - 1) * pieces
    return pl.pallas_call(
        body, name=name, in_specs=[pl.BlockSpec(memory_space=pl.ANY), pl.BlockSpec(memory_space=pl.ANY)],
        out_specs=pl.BlockSpec(memory_space=pl.ANY), out_shape=jax.ShapeDtypeStruct((n, *blk), src.dtype),
        input_output_aliases={1: 0},
        scratch_shapes=[pltpu.SemaphoreType.DMA((n_sems,)), pltpu.SemaphoreType.DMA((n_sems,))],
    )(src, landing)


def _pack_rows(n_elems):
    return -(-n_elems // (16 * PACK_W)) * 16


def _pack(parts, lead=()):
    nl = len(lead)
    rows = []
    for a in parts:
        flat = a.reshape(*lead, -1)
        r = _pack_rows(flat.shape[-1])
        flat = jnp.pad(flat, [(0, 0)] * nl + [(0, r * PACK_W - flat.shape[-1])])
        rows.append(flat.reshape(*lead, r, PACK_W))
    total = sum(r.shape[nl] for r in rows)
    pad = -total % PACK_TM
    if pad:
        rows.append(jnp.zeros((*lead, pad, PACK_W), rows[0].dtype))
    return jnp.concatenate(rows, axis=nl)


def _unpack(buf, shapes, lead=()):
    nl = len(lead)
    out, r0 = [], 0
    for shp in shapes:
        n = int(np.prod(shp))
        r = _pack_rows(n)
        piece = lax.slice_in_dim(buf, r0, r0 + r, axis=nl).reshape(*lead, r * PACK_W)
        out.append(lax.slice_in_dim(piece, 0, n, axis=nl).reshape(*lead, *shp))
        r0 += r
    return out


def _split_for_devices(g, axis):
    shp = g.shape
    g = g.reshape(*shp[:axis], N_DEV, shp[axis] // N_DEV, *shp[axis + 1:])
    return jnp.moveaxis(g, axis, 0)


def _join_from_devices(parts, axis):
    parts = jnp.moveaxis(parts, 0, axis)
    shp = parts.shape
    return parts.reshape(*shp[:axis], shp[axis] * shp[axis + 1], *shp[axis + 2:])


def _adamw_terms(w, g, m, v):
    m = ADAM_B1 * m + (1.0 - ADAM_B1) * g
    v = ADAM_B2 * v + (1.0 - ADAM_B2) * (g * g)
    m_hat = m / (1.0 - ADAM_B1 ** ADAM_STEP)
    v_hat = v / (1.0 - ADAM_B2 ** ADAM_STEP)
    delta = -ADAM_LR * (m_hat / (jnp.sqrt(v_hat) + ADAM_EPS) + ADAM_WD * w)
    return delta, m, v


def _adamw_packed(w, m, v, partials, n_partials, name):
    def fn(w_, m_, v_, *parts):
        g = parts[0].astype(F32)
        for part in parts[1:]:
            g = g + part.astype(F32)
        return (g,) + _adamw_terms(w_, g, m_, v_)
    return _rowwise(fn, [w, m, v] + [(partials, s) for s in range(n_partials)], [], [(PACK_W, F32)] * 4,
                    tm=PACK_TM, name=name)


def kernel(x, p, positions, ffn1_pre_g, ffn1_w_in, ffn1_w_down, ffn1_post_g, mix_pre_g, mix_post_g, ffn2_pre_g, ffn2_w_in, ffn2_w_down, ffn2_post_g, ple_pre_g, ple_w_gate, ple_w_proj, ple_post_g, hyb_w_in, gm_ln_g, gm_ln_b, gm_w_s, gm_b_s, ssd_conv_w, ssd_conv_b, ssd_dt_bias, ssd_a_log, ssd_d, ssd_norm_g, hyb_w_out, mla_w_in, mla_q_norm_g, mla_kv_norm_g, mla_w_uq, mla_w_ukv, mla_w_out, loss_target, m_ffn1_pre_g, m_ffn1_w_in, m_ffn1_w_down, m_ffn1_post_g, m_mix_pre_g, m_mix_post_g, m_ffn2_pre_g, m_ffn2_w_in, m_ffn2_w_down, m_ffn2_post_g, m_ple_pre_g, m_ple_w_gate, m_ple_w_proj, m_ple_post_g, m_hyb_w_in, m_gm_ln_g, m_gm_ln_b, m_gm_w_s, m_gm_b_s, m_ssd_conv_w, m_ssd_conv_b, m_ssd_dt_bias, m_ssd_a_log, m_ssd_d, m_ssd_norm_g, m_hyb_w_out, m_mla_w_in, m_mla_q_norm_g, m_mla_kv_norm_g, m_mla_w_uq, m_mla_w_ukv, m_mla_w_out, v_ffn1_pre_g, v_ffn1_w_in, v_ffn1_w_down, v_ffn1_post_g, v_mix_pre_g, v_mix_post_g, v_ffn2_pre_g, v_ffn2_w_in, v_ffn2_w_down, v_ffn2_post_g, v_ple_pre_g, v_ple_w_gate, v_ple_w_proj, v_ple_post_g, v_hyb_w_in, v_gm_ln_g, v_gm_ln_b, v_gm_w_s, v_gm_b_s, v_ssd_conv_w, v_ssd_conv_b, v_ssd_dt_bias, v_ssd_a_log, v_ssd_d, v_ssd_norm_g, v_hyb_w_out, v_mla_w_in, v_mla_q_norm_g, v_mla_kv_norm_g, v_mla_w_uq, v_mla_w_ukv, v_mla_w_out):
    given = dict(locals())
    w = {n: given[n] for n in WEIGHTS}
    mom = {n: given["m_" + n] for n in WEIGHTS}
    var = {n: given["v_" + n] for n in WEIGHTS}
    shard_shapes = [w[n].shape for n in SHARDED]
    repl_shapes = [w[n].shape for n in REPLICATED]

    pack16 = _pack([w[n].astype(BF16) for n in SHARDED_BF16])
    by_chip = _exchange(pack16, ("x", "y"), "gather", "gather_weights_ici")
    by_core = _exchange(by_chip.reshape(-1, PACK_W), ("c",), "gather", "gather_weights_d2d")
    gathered = by_core.reshape(2, 4, -1, PACK_W).transpose(1, 0, 2, 3).reshape(N_DEV, -1, PACK_W)
    fw = {n: _join_from_devices(a, SHARD_AXIS[n])
          for n, a in zip(SHARDED_BF16, _unpack(gathered, [w[n].shape for n in SHARDED_BF16], (N_DEV,)))}
    small = _exchange(_pack([w[n] for n in SHARDED_F32]), MESH_AXES, "gather", "gather_weights_f32")
    fw.update({n: _join_from_devices(a, SHARD_AXIS[n])
               for n, a in zip(SHARDED_F32, _unpack(small, [w[n].shape for n in SHARDED_F32], (N_DEV,)))})

    loss_local, grad_x, grads = _device_step(x[0], p[:, 0], positions[0], loss_target[0], fw, w)
    loss = lax.psum(loss_local, MESH_AXES)

    per_dev = [_split_for_devices(grads[n], SHARD_AXIS[n]) for n in SHARDED]
    per_dev = [a.reshape(4, 2, *a.shape[1:]).swapaxes(0, 1) for a in per_dev]
    gpack = _pack(per_dev, (2, 4))
    rows = gpack.shape[2]
    pair = _exchange(gpack.reshape(2, 4 * rows, PACK_W), ("c",), "a2a", "reduce_grads_d2d")
    chip_sum = _rowwise(lambda a, b: a + b, [(pair, 0), (pair, 1)], [], [(PACK_W, BF16)], tm=PACK_TM, name="reduce_grads_pair")
    quads = _exchange(chip_sum.reshape(4, rows, PACK_W), ("x", "y"), "a2a", "reduce_grads_ici")
    g_s, d_s, m_s, v_s = _adamw_packed(_pack([w[n] for n in SHARDED]), _pack([mom[n] for n in SHARDED]),
                                       _pack([var[n] for n in SHARDED]), quads, 4, "adamw_sharded")

    rpack = _pack([grads[n].reshape(w[n].shape) for n in REPLICATED])
    everyone = _exchange(rpack, MESH_AXES, "gather", "gather_small_grads")
    g_r, d_r, m_r, v_r = _adamw_packed(_pack([w[n] for n in REPLICATED]), _pack([mom[n] for n in REPLICATED]),
                                       _pack([var[n] for n in REPLICATED]), everyone, N_DEV, "adamw_replicated")

    outs = []
    for sharded_buf, repl_buf in ((g_s, g_r), (d_s, d_r), (m_s, m_r), (v_s, v_r)):
        vals = dict(zip(SHARDED, _unpack(sharded_buf, shard_shapes)))
        vals.update(zip(REPLICATED, _unpack(repl_buf, repl_shapes)))
        outs.extend(vals[n] for n in WEIGHTS)
    return (loss, grad_x[None], *outs)
```

```python
import functools
import math

import jax
import jax.numpy as jnp
import numpy as np
from jax import lax
from jax.experimental import pallas as pl
from jax.experimental.pallas import tpu as pltpu

F32 = jnp.float32
BF16 = jnp.bfloat16
HIGHEST = lax.Precision.HIGHEST

V7X_VMEM_LIMIT_BYTES = 52 * 1024 * 1024
LANES = 128

D_MODEL = 1024
DEPTH = 4
D_FF = 2816
PLE_DIM = 256
NORM_EPS = 1e-6
LN_EPS = 1e-5
CHUNK = 128
GM_HEADS = 8
SSD_HEADS = 16
SSD_HEAD_DIM = 64
SSD_INNER = 1024
SSD_STATE = 128
SSD_BC = 256
SSD_CONV_CH = 1536
HYB_MAIN = 4608
MLA_HEADS = 16
MLA_Q_LORA = 256
MLA_KV_LORA = 128
MLA_ROPE = 64
MLA_IN = 448
MLA_IN_PAD = 512
ATTN_SCALE = 192.0 ** -0.5
LOG2_E = 1.4426950408889634
LN_2 = 0.6931471805599453
ATTN_QSCALE = ATTN_SCALE * LOG2_E
ROPE_BASE = 10000.0

ADAM_LR = 0.001
ADAM_B1 = 0.9
ADAM_B2 = 0.999
ADAM_EPS = 1e-08
ADAM_WD = 0.01
ADAM_STEP = 10

N_DEV = 8
PACK_W = 1024
PACK_TM = 256

WEIGHTS = ['ffn1_pre_g', 'ffn1_w_in', 'ffn1_w_down', 'ffn1_post_g', 'mix_pre_g', 'mix_post_g', 'ffn2_pre_g',
           'ffn2_w_in', 'ffn2_w_down', 'ffn2_post_g', 'ple_pre_g', 'ple_w_gate', 'ple_w_proj', 'ple_post_g',
           'hyb_w_in', 'gm_ln_g', 'gm_ln_b', 'gm_w_s', 'gm_b_s', 'ssd_conv_w', 'ssd_conv_b', 'ssd_dt_bias',
           'ssd_a_log', 'ssd_d', 'ssd_norm_g', 'hyb_w_out', 'mla_w_in', 'mla_q_norm_g', 'mla_kv_norm_g',
           'mla_w_uq', 'mla_w_ukv', 'mla_w_out']
SHARD_AXIS = {'ffn1_w_in': 2, 'ffn1_w_down': 1, 'ffn2_w_in': 2, 'ffn2_w_down': 1, 'ple_w_gate': 1, 'ple_w_proj': 2,
              'hyb_w_in': 2, 'ssd_conv_w': 2, 'hyb_w_out': 1, 'mla_w_in': 1, 'mla_q_norm_g': 1, 'mla_w_uq': 2,
              'mla_w_ukv': 2, 'mla_w_out': 1}
SHARDED = [n for n in WEIGHTS if n in SHARD_AXIS]
REPLICATED = [n for n in WEIGHTS if n not in SHARD_AXIS]
SHARDED_F32 = ['ssd_conv_w', 'mla_q_norm_g']
SHARDED_BF16 = [n for n in SHARDED if n not in SHARDED_F32]


def _params(*sem):
    return pltpu.CompilerParams(dimension_semantics=sem or None, vmem_limit_bytes=V7X_VMEM_LIMIT_BYTES)


def _pick(n, prefs):
    for t in prefs:
        if t <= n and n % t == 0:
            return t
    return n


def _mm(a, b, *, ta=False, tb=False, out_dtype=F32, tm=1024, tn=512, tk=512, name):
    m, k = (a.shape[1], a.shape[0]) if ta else a.shape
    n = b.shape[0] if tb else b.shape[1]
    assert k == (b.shape[1] if tb else b.shape[0]), (a.shape, b.shape, ta, tb)
    tm, tn, tk = _pick(m, (tm, 512, 256, 128)), _pick(n, (tn, 512, 256, 128)), _pick(k, (tk, 512, 256, 128))
    nk = k // tk
    dims = (((0 if ta else 1,), (1 if tb else 0,)), ((), ()))

    def body(a_ref, b_ref, o_ref, *acc):
        part = lax.dot_general(a_ref[...].astype(BF16), b_ref[...].astype(BF16), dims, preferred_element_type=F32)
        if nk == 1:
            o_ref[...] = part.astype(o_ref.dtype)
            return
        acc_ref, = acc
        kk = pl.program_id(2)

        @pl.when(kk == 0)
        def _():
            acc_ref[...] = part

        @pl.when(kk > 0)
        def _():
            acc_ref[...] += part

        @pl.when(kk == nk - 1)
        def _():
            o_ref[...] = acc_ref[...].astype(o_ref.dtype)

    a_spec = pl.BlockSpec((tk, tm), lambda i, j, kk: (kk, i)) if ta else pl.BlockSpec((tm, tk), lambda i, j, kk: (i, kk))
    b_spec = pl.BlockSpec((tn, tk), lambda i, j, kk: (j, kk)) if tb else pl.BlockSpec((tk, tn), lambda i, j, kk: (kk, j))
    return pl.pallas_call(
        body, name=name, grid=(m // tm, n // tn, nk), in_specs=[a_spec, b_spec],
        out_specs=pl.BlockSpec((tm, tn), lambda i, j, kk: (i, j)), out_shape=jax.ShapeDtypeStruct((m, n), out_dtype),
        scratch_shapes=[] if nk == 1 else [pltpu.VMEM((tm, tn), F32)],
        compiler_params=_params("parallel", "parallel", "arbitrary"),
    )(a, b)


def _rowwise(fn, rows, consts, outs, accs=(), *, tm=256, name):
    first = rows[0][0] if isinstance(rows[0], tuple) else rows[0]
    t = first.shape[-2]
    tm = _pick(t, (tm, 256, 128, 64, 32, 16, 8))
    n_r, n_c, n_o = len(rows), len(consts), len(outs)

    def body(*refs):
        vals = [r[...] for r in refs[:n_r + n_c]]
        res = fn(*vals)
        res = res if isinstance(res, tuple) else (res,)
        o_refs, a_refs = refs[n_r + n_c:n_r + n_c + n_o], refs[n_r + n_c + n_o:]
        for o_ref, v in zip(o_refs, res[:n_o]):
            if isinstance(v, (tuple, list)):
                off = 0
                for piece in v:
                    o_ref[:, off:off + piece.shape[1]] = piece.astype(o_ref.dtype)
                    off += piece.shape[1]
            else:
                o_ref[...] = v.astype(o_ref.dtype)
        if a_refs:
            terms = res[n_o:]
            is_first = pl.program_id(0) == 0

            @pl.when(is_first)
            def _():
                for a_ref, v in zip(a_refs, terms):
                    a_ref[...] = v

            @pl.when(jnp.logical_not(is_first))
            def _():
                for a_ref, v in zip(a_refs, terms):
                    a_ref[...] += v

    in_specs, args = [], []
    for r in rows:
        if isinstance(r, tuple):
            arr, slot = r
            in_specs.append(pl.BlockSpec((None, tm, arr.shape[2]), functools.partial(lambda i, s: (s, i, 0), s=slot)))
        else:
            arr = r
            in_specs.append(pl.BlockSpec((tm, arr.shape[1]), lambda i: (i, 0)))
        args.append(arr)
    for c in consts:
        in_specs.append(pl.BlockSpec(c.shape, lambda i: (0, 0)))
        args.append(c)
    out_specs = [pl.BlockSpec((tm, c), lambda i: (i, 0)) for c, _ in outs]
    out_shape = [jax.ShapeDtypeStruct((t, c), dt) for c, dt in outs]
    for shp in accs:
        out_specs.append(pl.BlockSpec(shp, lambda i: (0, 0)))
        out_shape.append(jax.ShapeDtypeStruct(shp, F32))
    res = pl.pallas_call(
        body, name=name, grid=(t // tm,), in_specs=in_specs, out_specs=out_specs, out_shape=out_shape,
        compiler_params=_params("arbitrary" if accs else "parallel"),
    )(*args)
    return res[0] if len(res) == 1 else tuple(res)


def _colsum(v):
    return jnp.sum(v, axis=0, keepdims=True)


def _rms(x, g, eps=NORM_EPS):
    r = lax.rsqrt(jnp.mean(x * x, axis=-1, keepdims=True) + eps)
    return x * r * g


def _rms_bwd(x, g, dy, eps=NORM_EPS):
    r = lax.rsqrt(jnp.mean(x * x, axis=-1, keepdims=True) + eps)
    xh = x * r
    dyg = dy * g
    dx = r * (dyg - xh * jnp.mean(dyg * xh, axis=-1, keepdims=True))
    return dx, dy * xh


def _silu(x):
    return x * jax.nn.sigmoid(x)


def _silu_grad(x):
    s = jax.nn.sigmoid(x)
    return s * (1.0 + x * (1.0 - s))


_GELU_K = math.sqrt(2.0 / math.pi)


def _gelu(x):
    return 0.5 * x * (1.0 + jnp.tanh(_GELU_K * (x + 0.044715 * x * x * x)))


def _gelu_grad(x):
    t = jnp.tanh(_GELU_K * (x + 0.044715 * x * x * x))
    return 0.5 * (1.0 + t) + 0.5 * x * (1.0 - t * t) * _GELU_K * (1.0 + 3.0 * 0.044715 * x * x)


def _prenorm(h, g, name):
    return _rowwise(lambda x, gg: _rms(x, gg), [h], [g], [(D_MODEL, BF16)], name=name)


def _postnorm_residual(h, f, g, scale, name):
    return _rowwise(lambda x, ff, gg: x + scale * _rms(ff, gg), [h, f], [g], [(D_MODEL, F32)], name=name)


def _postnorm_bwd(f, dh, g, scale, name):
    def fn(ff, d, gg):
        dx, dgt = _rms_bwd(ff, gg, scale * d)
        return dx, _colsum(dgt)
    return _rowwise(fn, [f, dh], [g], [(D_MODEL, BF16)], [(1, D_MODEL)], name=name)


def _prenorm_bwd(h, das, dh, g, name):
    n = len(das)

    def fn(x, *rest):
        da = rest[0]
        for extra in rest[1:n]:
            da = da + extra
        d, gg = rest[n], rest[n + 1]
        dx, dgt = _rms_bwd(x, gg, da)
        return d + dx, _colsum(dgt)
    return _rowwise(fn, [h, *das, dh], [g], [(D_MODEL, F32)], [(1, D_MODEL)], name=name)


def _ffn_fwd(h, w, pre_g, post_g, tag):
    a = _prenorm(h, pre_g, tag + "_prenorm")
    gu = _mm(a, w["in"], tm=1024, tn=512, tk=1024, name=tag + "_in")
    s = _rowwise(lambda x: _silu(x[:, :D_FF]) * x[:, D_FF:], [gu], [], [(D_FF, BF16)], name=tag + "_swiglu")
    f = _mm(s, w["down"], tm=1024, tn=1024, tk=D_FF, name=tag + "_down")
    out = _postnorm_residual(h, f, post_g, 0.5, tag + "_postnorm")
    return out, (h, a, gu, s, f)


def _ffn_bwd(dh, saved, w, pre_g, post_g, tag):
    h, a, gu, s, f = saved
    df, d_post = _postnorm_bwd(f, dh, post_g, 0.5, tag + "_postnorm_bwd")
    ds = _mm(df, w["down"], tb=True, tm=1024, tn=1408, tk=1024, name=tag + "_down_dx")
    d_down = _mm(s, df, ta=True, tm=1408, tn=1024, tk=1024, name=tag + "_down_dw")

    def swiglu_bwd(x, d):
        gate, up = x[:, :D_FF], x[:, D_FF:]
        return ((d * up * _silu_grad(gate), d * _silu(gate)),)
    dgu = _rowwise(swiglu_bwd, [gu, ds], [], [(2 * D_FF, BF16)], name=tag + "_swiglu_bwd")
    da = _mm(dgu, w["in"], tb=True, tm=1024, tn=1024, tk=1408, name=tag + "_in_dx")
    d_in = _mm(a, dgu, ta=True, tm=1024, tn=512, tk=2048, name=tag + "_in_dw")
    dh_in, d_pre = _prenorm_bwd(h, [da], dh, pre_g, tag + "_prenorm_bwd")
    return dh_in, dict(w_in=d_in, w_down=d_down, pre_g=d_pre, post_g=d_post)


def _ple_fwd(h, p_i, w, pre_g, post_g):
    a = _prenorm(h, pre_g, "ple_prenorm")
    gl = _mm(a, w["gate"], tm=1024, tn=1024, tk=1024, name="ple_gate")
    e = _mm(p_i, w["proj"], tm=1024, tn=1024, tk=PLE_DIM, name="ple_proj")
    out = _rowwise(lambda x, g_, e_, gg: x + _rms(jax.nn.sigmoid(g_) * e_, gg), [h, gl, e], [post_g],
                   [(D_MODEL, F32)], name="ple_out")
    return out, (h, a, gl, e)


def _ple_bwd(dh, saved, p_i, w, pre_g, post_g):
    h, a, gl, e = saved

    def fn(g_, e_, d, gg):
        sg = jax.nn.sigmoid(g_)
        du, dgt = _rms_bwd(sg * e_, gg, d)
        return du * e_ * sg * (1.0 - sg), du * sg, _colsum(dgt)
    dgl, de, d_post = _rowwise(fn, [gl, e, dh], [post_g], [(D_MODEL, BF16), (D_MODEL, BF16)], [(1, D_MODEL)],
                               name="ple_out_bwd")
    da = _mm(dgl, w["gate"], tb=True, tm=1024, tn=1024, tk=1024, name="ple_gate_dx")
    d_gate = _mm(a, dgl, ta=True, tm=1024, tn=1024, tk=2048, name="ple_gate_dw")
    d_proj = _mm(p_i, de, ta=True, tm=PLE_DIM, tn=1024, tk=2048, name="ple_proj_dw")
    dh_in, d_pre = _prenorm_bwd(h, [da], dh, pre_g, "ple_prenorm_bwd")
    return dh_in, dict(w_gate=d_gate, w_proj=d_proj, pre_g=d_pre, post_g=d_post)


def _gm_layernorm(v, g, b):
    mu = jnp.mean(v, axis=-1, keepdims=True)
    xc = v - mu
    rstd = lax.rsqrt(jnp.mean(xc * xc, axis=-1, keepdims=True) + LN_EPS)
    vhat = xc * rstd
    return vhat, rstd, vhat * g + b


def _gmlp_fwd(proj, wm, bias_t, ln_g, ln_b, name):
    t = proj.shape[0]

    def body(uv_ref, wm_ref, bt_ref, g_ref, b_ref, o_ref):
        for hd in range(GM_HEADS):
            lo = hd * LANES
            u = _gelu(uv_ref[:, lo:lo + LANES])
            v = _gelu(uv_ref[:, 1024 + lo:1024 + lo + LANES])
            _, _, vln = _gm_layernorm(v, g_ref[:, lo:lo + LANES], b_ref[:, lo:lo + LANES])
            mixed = jnp.dot(wm_ref[hd], vln.astype(BF16), preferred_element_type=F32) + bt_ref[:, hd:hd + 1]
            o_ref[:, lo:lo + LANES] = (u * mixed).astype(o_ref.dtype)

    return pl.pallas_call(
        body, name=name, grid=(t // CHUNK,),
        in_specs=[pl.BlockSpec((CHUNK, 2048), lambda i: (i, 0)), pl.BlockSpec(wm.shape, lambda i: (0, 0, 0)),
                  pl.BlockSpec(bias_t.shape, lambda i: (0, 0)), pl.BlockSpec(ln_g.shape, lambda i: (0, 0)),
                  pl.BlockSpec(ln_b.shape, lambda i: (0, 0))],
        out_specs=pl.BlockSpec((CHUNK, 1024), lambda i: (i, 0)), out_shape=jax.ShapeDtypeStruct((t, 1024), BF16),
        compiler_params=_params("parallel"),
    )(proj, wm, bias_t, ln_g, ln_b)


def _gmlp_bwd(proj, dyab, wm, bias_t, ln_g, ln_b, name):
    t = proj.shape[0]
    nc = t // CHUNK

    def body(uv_ref, dy_ref, wm_ref, bt_ref, g_ref, b_ref, duv_ref, dw_ref, db_ref, dg_ref, dbeta_ref, dbacc):
        c = pl.program_id(0)

        @pl.when(c == 0)
        def _():
            dw_ref[...] = jnp.zeros_like(dw_ref)
            dbacc[...] = jnp.zeros_like(dbacc)
            dg_ref[...] = jnp.zeros_like(dg_ref)
            dbeta_ref[...] = jnp.zeros_like(dbeta_ref)

        for hd in range(GM_HEADS):
            lo = hd * LANES
            xu = uv_ref[:, lo:lo + LANES]
            xv = uv_ref[:, 1024 + lo:1024 + lo + LANES]
            u = _gelu(xu)
            g_h = g_ref[:, lo:lo + LANES]
            vhat, rstd, vln = _gm_layernorm(_gelu(xv), g_h, b_ref[:, lo:lo + LANES])
            vln16 = vln.astype(BF16)
            mixed = jnp.dot(wm_ref[hd], vln16, preferred_element_type=F32) + bt_ref[:, hd:hd + 1]
            dy = dy_ref[:, lo:lo + LANES]
            du = dy * mixed
            dmix = dy * u
            dmix16 = dmix.astype(BF16)
            dw_ref[hd] += lax.dot_general(dmix16, vln16, (((1,), (1,)), ((), ())), preferred_element_type=F32)
            dbacc[hd] += dmix
            dvln = lax.dot_general(wm_ref[hd], dmix16, (((0,), (0,)), ((), ())), preferred_element_type=F32)
            dg_ref[:, lo:lo + LANES] += _colsum(dvln * vhat)
            dbeta_ref[:, lo:lo + LANES] += _colsum(dvln)
            dvh = dvln * g_h
            dv = rstd * (dvh - jnp.mean(dvh, axis=-1, keepdims=True)
                         - vhat * jnp.mean(dvh * vhat, axis=-1, keepdims=True))
            duv_ref[:, lo:lo + LANES] = (du * _gelu_grad(xu)).astype(duv_ref.dtype)
            duv_ref[:, 1024 + lo:1024 + lo + LANES] = (dv * _gelu_grad(xv)).astype(duv_ref.dtype)

        @pl.when(c == nc - 1)
        def _():
            row = lax.broadcasted_iota(jnp.int32, (CHUNK, CHUNK), 0)
            col = lax.broadcasted_iota(jnp.int32, (CHUNK, CHUNK), 1)
            for hd in range(GM_HEADS):
                dw_ref[hd] = jnp.where(col <= row, dw_ref[hd], 0.0)
                db_ref[hd] = jnp.sum(dbacc[hd], axis=1, keepdims=True)

    return pl.pallas_call(
        body, name=name, grid=(nc,),
        in_specs=[pl.BlockSpec((CHUNK, 2048), lambda i: (i, 0)), pl.BlockSpec((CHUNK, 1024), lambda i: (i, 0)),
                  pl.BlockSpec(wm.shape, lambda i: (0, 0, 0)), pl.BlockSpec(bias_t.shape, lambda i: (0, 0)),
                  pl.BlockSpec(ln_g.shape, lambda i: (0, 0)), pl.BlockSpec(ln_b.shape, lambda i: (0, 0))],
        out_specs=[pl.BlockSpec((CHUNK, 2048), lambda i: (i, 0)), pl.BlockSpec((GM_HEADS, CHUNK, CHUNK), lambda i: (0, 0, 0)),
                   pl.BlockSpec((GM_HEADS, CHUNK, 1), lambda i: (0, 0, 0)), pl.BlockSpec((1, 1024), lambda i: (0, 0)),
                   pl.BlockSpec((1, 1024), lambda i: (0, 0))],
        out_shape=[jax.ShapeDtypeStruct((t, 2048), BF16), jax.ShapeDtypeStruct((GM_HEADS, CHUNK, CHUNK), F32),
                   jax.ShapeDtypeStruct((GM_HEADS, CHUNK, 1), F32), jax.ShapeDtypeStruct((1, 1024), F32),
                   jax.ShapeDtypeStruct((1, 1024), F32)],
        scratch_shapes=[pltpu.VMEM((GM_HEADS, CHUNK, CHUNK), F32)],
        compiler_params=_params("arbitrary"),
    )(proj, dyab, wm, bias_t, ln_g, ln_b)


def _ssd_chunk_terms(dt_pad, a_pad):
    row = lax.broadcasted_iota(jnp.int32, (CHUNK, CHUNK), 0)
    col = lax.broadcasted_iota(jnp.int32, (CHUNK, CHUNK), 1)
    tril = jnp.where(col <= row, 1.0, 0.0).astype(F32)
    a_cs = jnp.dot(tril, dt_pad * a_pad, precision=HIGHEST, preferred_element_type=F32)
    return a_cs, a_cs.T


def _pair_cols(mat, hd_a, lane_lt64):
    return jnp.where(lane_lt64, mat[:, hd_a:hd_a + 1], mat[:, hd_a + 1:hd_a + 2])


def _head_decay(a_cs, a_cs_t, hd, causal):
    seg = a_cs[:, hd:hd + 1] - a_cs_t[hd:hd + 1, :]
    return jnp.exp(jnp.where(causal, seg, -jnp.inf))


def _ssd_fwd(act, dt_pad, a_pad, d_pad, name):
    t = act.shape[0]
    nc = t // CHUNK

    def body(act_ref, dt_ref, a_ref, d_ref, y_ref, st_ref, h_sc):
        c = pl.program_id(0)

        @pl.when(c == 0)
        def _():
            h_sc[...] = jnp.zeros_like(h_sc)

        st_ref[...] = h_sc[...]
        row = lax.broadcasted_iota(jnp.int32, (CHUNK, CHUNK), 0)
        col = lax.broadcasted_iota(jnp.int32, (CHUNK, CHUNK), 1)
        causal = col <= row
        lane_lt64 = lax.broadcasted_iota(jnp.int32, (CHUNK, LANES), 1) < SSD_HEAD_DIM
        row_lt64 = lax.broadcasted_iota(jnp.int32, (LANES, 1), 0) < SSD_HEAD_DIM
        dt = dt_ref[...]
        a_cs, a_cs_t = _ssd_chunk_terms(dt, a_ref[...])
        last = a_cs[CHUNK - 1:CHUNK, :]
        for g in range(2):
            b16 = act_ref[:, SSD_INNER + g * SSD_STATE:SSD_INNER + (g + 1) * SSD_STATE].astype(BF16)
            c16 = act_ref[:, SSD_INNER + SSD_BC + g * SSD_STATE:SSD_INNER + SSD_BC + (g + 1) * SSD_STATE].astype(BF16)
            cb = lax.dot_general(c16, b16, (((1,), (1,)), ((), ())), preferred_element_type=F32)
            for pr in range(4):
                ha = g * 8 + pr * 2
                lo = ha * SSD_HEAD_DIM
                xs = act_ref[:, lo:lo + LANES]
                xd = xs * _pair_cols(dt, ha, lane_lt64)
                xd16 = xd.astype(BF16)
                ya = jnp.dot((cb * _head_decay(a_cs, a_cs_t, ha, causal)).astype(BF16), xd16, preferred_element_type=F32)
                yb = jnp.dot((cb * _head_decay(a_cs, a_cs_t, ha + 1, causal)).astype(BF16), xd16, preferred_element_type=F32)
                a_p = _pair_cols(a_cs, ha, lane_lt64)
                hp = h_sc[lo:lo + LANES, :]
                y_off = lax.dot_general(c16, hp.astype(BF16), (((1,), (1,)), ((), ())), preferred_element_type=F32)
                d_p = jnp.where(lane_lt64[:1], d_ref[:, ha:ha + 1], d_ref[:, ha + 1:ha + 2])
                y_ref[:, lo:lo + LANES] = jnp.where(lane_lt64, ya, yb) + y_off * jnp.exp(a_p) + d_p * xs
                last_p = jnp.where(lane_lt64[:1], last[:, ha:ha + 1], last[:, ha + 1:ha + 2])
                xw16 = (xd * jnp.exp(last_p - a_p)).astype(BF16)
                s_new = lax.dot_general(xw16, b16, (((0,), (0,)), ((), ())), preferred_element_type=F32)
                t_col = jnp.where(row_lt64, jnp.exp(last[:, ha:ha + 1]), jnp.exp(last[:, ha + 1:ha + 2]))
                h_sc[lo:lo + LANES, :] = t_col * hp + s_new

    return pl.pallas_call(
        body, name=name, grid=(nc,),
        in_specs=[pl.BlockSpec((CHUNK, SSD_CONV_CH), lambda i: (i, 0)), pl.BlockSpec((CHUNK, LANES), lambda i: (i, 0)),
                  pl.BlockSpec((1, LANES), lambda i: (0, 0)), pl.BlockSpec((1, LANES), lambda i: (0, 0))],
        out_specs=[pl.BlockSpec((CHUNK, SSD_INNER), lambda i: (i, 0)),
                   pl.BlockSpec((None, SSD_INNER, SSD_STATE), lambda i: (i, 0, 0))],
        out_shape=[jax.ShapeDtypeStruct((t, SSD_INNER), F32), jax.ShapeDtypeStruct((nc, SSD_INNER, SSD_STATE), F32)],
        scratch_shapes=[pltpu.VMEM((SSD_INNER, SSD_STATE), F32)],
        compiler_params=_params("arbitrary"),
    )(act, dt_pad, a_pad, d_pad)


def _ssd_bwd(act, dt_pad, a_pad, d_pad, states, dy, name):
    t = act.shape[0]
    nc = t // CHUNK

    def body(act_ref, dt_ref, a_ref, d_ref, st_ref, dy_ref, dact_ref, ddt_ref, da_ref, dd_ref, dh_sc):
        c = pl.program_id(0)

        @pl.when(c == 0)
        def _():
            dh_sc[...] = jnp.zeros_like(dh_sc)
            da_ref[...] = jnp.zeros_like(da_ref)
            dd_ref[...] = jnp.zeros_like(dd_ref)

        row = lax.broadcasted_iota(jnp.int32, (CHUNK, CHUNK), 0)
        col = lax.broadcasted_iota(jnp.int32, (CHUNK, CHUNK), 1)
        causal = col <= row
        lane = lax.broadcasted_iota(jnp.int32, (CHUNK, LANES), 1)
        lane_lt64 = lane < SSD_HEAD_DIM
        row_lt64 = lax.broadcasted_iota(jnp.int32, (LANES, 1), 0) < SSD_HEAD_DIM
        is_last = lax.broadcasted_iota(jnp.int32, (CHUNK, 1), 0) == CHUNK - 1
        dt = dt_ref[...]
        a_cs, a_cs_t = _ssd_chunk_terms(dt, a_ref[...])
        last = a_cs[CHUNK - 1:CHUNK, :]
        d_acs = jnp.zeros((CHUNK, LANES), F32)
        ddt_x = jnp.zeros((CHUNK, LANES), F32)
        dd_acc = jnp.zeros((1, LANES), F32)

        def head_sum(v, first):
            return jnp.sum(jnp.where(lane_lt64 if first else jnp.logical_not(lane_lt64), v, 0.0), axis=1, keepdims=True)

        for g in range(2):
            b_lo = SSD_INNER + g * SSD_STATE
            c_lo = SSD_INNER + SSD_BC + g * SSD_STATE
            b16 = act_ref[:, b_lo:b_lo + SSD_STATE].astype(BF16)
            c16 = act_ref[:, c_lo:c_lo + SSD_STATE].astype(BF16)
            cb = lax.dot_general(c16, b16, (((1,), (1,)), ((), ())), preferred_element_type=F32)
            dcb = jnp.zeros((CHUNK, CHUNK), F32)
            db_g = jnp.zeros((CHUNK, SSD_STATE), F32)
            dc_g = jnp.zeros((CHUNK, SSD_STATE), F32)
            for pr in range(4):
                ha = g * 8 + pr * 2
                lo = ha * SSD_HEAD_DIM
                xs = act_ref[:, lo:lo + LANES]
                dt_p = _pair_cols(dt, ha, lane_lt64)
                xd = xs * dt_p
                xd16 = xd.astype(BF16)
                a_p = _pair_cols(a_cs, ha, lane_lt64)
                exp_a = jnp.exp(a_p)
                last_p = jnp.where(lane_lt64[:1], last[:, ha:ha + 1], last[:, ha + 1:ha + 2])
                w_p = jnp.exp(last_p - a_p)
                hp = st_ref[lo:lo + LANES, :]
                hp16 = hp.astype(BF16)
                dhn = dh_sc[lo:lo + LANES, :]
                dhn16 = dhn.astype(BF16)
                dyp = dy_ref[:, lo:lo + LANES]
                d_p = jnp.where(lane_lt64[:1], d_ref[:, ha:ha + 1], d_ref[:, ha + 1:ha + 2])
                dd_acc = dd_acc + jnp.where(lane[:1] == ha, jnp.sum(head_sum(dyp * xs, True), axis=0, keepdims=True), 0.0) \
                    + jnp.where(lane[:1] == ha + 1, jnp.sum(head_sum(dyp * xs, False), axis=0, keepdims=True), 0.0)
                g_off = lax.dot_general(c16, hp16, (((1,), (1,)), ((), ())), preferred_element_type=F32)
                dg16 = (dyp * exp_a).astype(BF16)
                dc_g = dc_g + jnp.dot(dg16, hp16, preferred_element_type=F32)
                dh_prev = lax.dot_general(dg16, c16, (((0,), (0,)), ((), ())), preferred_element_type=F32)
                off_term = dyp * g_off * exp_a
                q = lax.dot_general(b16, dhn16, (((1,), (1,)), ((), ())), preferred_element_type=F32)
                xw16 = (xd * w_p).astype(BF16)
                db_g = db_g + jnp.dot(xw16, dhn16, preferred_element_type=F32)
                dw_term = xd * q * w_p
                dxd = w_p * q
                dt_all = dhn * hp
                dyp16 = dyp.astype(BF16)
                for k, first in ((0, True), (1, False)):
                    hd = ha + k
                    sel = lane_lt64 if first else jnp.logical_not(lane_lt64)
                    decay = _head_decay(a_cs, a_cs_t, hd, causal)
                    m = cb * decay
                    dy_h = jnp.where(sel, dyp16, jnp.zeros_like(dyp16))
                    dm = lax.dot_general(dy_h, xd16, (((1,), (1,)), ((), ())), preferred_element_type=F32)
                    dcb = dcb + dm * decay
                    dseg = dm * m
                    dxd = dxd + jnp.where(sel, lax.dot_general(m.astype(BF16), dyp16, (((0,), (0,)), ((), ())),
                                                               preferred_element_type=F32), 0.0)
                    d_col = jnp.sum(dseg, axis=1, keepdims=True) - jnp.sum(dseg.T, axis=1, keepdims=True)
                    dw_col = head_sum(dw_term, first)
                    d_col = d_col + head_sum(off_term, first) - dw_col
                    t_h = jnp.exp(last[:, hd:hd + 1])
                    dt_sum = jnp.sum(jnp.sum(jnp.where(row_lt64 if first else jnp.logical_not(row_lt64), dt_all, 0.0),
                                             axis=1, keepdims=True), axis=0, keepdims=True)
                    end_term = jnp.sum(dw_col, axis=0, keepdims=True) + dt_sum * t_h
                    d_col = d_col + jnp.where(is_last, end_term, 0.0)
                    d_acs = d_acs + jnp.where(lane == hd, d_col, 0.0)
                t_col = jnp.where(row_lt64, jnp.exp(last[:, ha:ha + 1]), jnp.exp(last[:, ha + 1:ha + 2]))
                dh_sc[lo:lo + LANES, :] = t_col * dhn + dh_prev
                dact_ref[:, lo:lo + LANES] = d_p * dyp + dxd * dt_p
                ddt_all = dxd * xs
                ddt_x = ddt_x + jnp.where(lane == ha, head_sum(ddt_all, True), 0.0) \
                    + jnp.where(lane == ha + 1, head_sum(ddt_all, False), 0.0)
            dcb16 = dcb.astype(BF16)
            dact_ref[:, b_lo:b_lo + SSD_STATE] = db_g + lax.dot_general(dcb16, c16, (((0,), (0,)), ((), ())),
                                                                          preferred_element_type=F32)
            dact_ref[:, c_lo:c_lo + SSD_STATE] = dc_g + jnp.dot(dcb16, b16, preferred_element_type=F32)
        triu = jnp.where(col >= row, 1.0, 0.0).astype(F32)
        dda = jnp.dot(triu, d_acs, precision=HIGHEST, preferred_element_type=F32)
        ddt_ref[...] = dda * a_ref[...] + ddt_x
        da_ref[...] += _colsum(dda * dt)
        dd_ref[...] += dd_acc

    rev = lambda i: (nc - 1 - i, 0)
    return pl.pallas_call(
        body, name=name, grid=(nc,),
        in_specs=[pl.BlockSpec((CHUNK, SSD_CONV_CH), rev), pl.BlockSpec((CHUNK, LANES), rev),
                  pl.BlockSpec((1, LANES), lambda i: (0, 0)), pl.BlockSpec((1, LANES), lambda i: (0, 0)),
                  pl.BlockSpec((None, SSD_INNER, SSD_STATE), lambda i: (nc - 1 - i, 0, 0)),
                  pl.BlockSpec((CHUNK, SSD_INNER), rev)],
        out_specs=[pl.BlockSpec((CHUNK, SSD_CONV_CH), rev), pl.BlockSpec((CHUNK, LANES), rev),
                   pl.BlockSpec((1, LANES), lambda i: (0, 0)), pl.BlockSpec((1, LANES), lambda i: (0, 0))],
        out_shape=[jax.ShapeDtypeStruct((t, SSD_CONV_CH), F32), jax.ShapeDtypeStruct((t, LANES), F32),
                   jax.ShapeDtypeStruct((1, LANES), F32), jax.ShapeDtypeStruct((1, LANES), F32)],
        scratch_shapes=[pltpu.VMEM((SSD_INNER, SSD_STATE), F32)],
        compiler_params=_params("arbitrary"),
    )(act, dt_pad, a_pad, d_pad, states, dy)


def _shift_down(x, k):
    return x if k == 0 else jnp.pad(x, ((k, 0), (0, 0)))[:x.shape[0]]


def _shift_up(x, k):
    return x if k == 0 else jnp.pad(x, ((0, k), (0, 0)))[k:]


def _conv_pre(x0, x1, x2, x3, w, b):
    return x0 * w[0:1] + x1 * w[1:2] + x2 * w[2:3] + x3 * w[3:4] + b


def _rope128(x, cpad, s_lo, s_hi):
    return x * cpad + pltpu.roll(x, 96, 1) * s_lo + pltpu.roll(x, 32, 1) * s_hi


ATTN_ROW_SPLIT = 4
ATTN_ROW_SPLIT_DKV = 2


def _diag_mask(rows, cols, row0):
    return lax.broadcasted_iota(jnp.int32, (rows, cols), 1) <= row0 + lax.broadcasted_iota(jnp.int32, (rows, cols), 0)


def _attn_scores(q, k):
    return lax.dot_general(q, k, (((1,), (1,)), ((), ())), preferred_element_type=F32)


def _causal_pairs(nq, by_key):
    if by_key:
        pairs = [(i, j) for j in range(nq) for i in range(j, nq)]
    else:
        pairs = [(i, j) for i in range(nq) for j in range(i + 1)]
    return (jnp.asarray([pr[0] for pr in pairs], jnp.int32), jnp.asarray([pr[1] for pr in pairs], jnp.int32))


def _attn_fwd(qf, kf, kvf, *, tq, name):
    t = qf.shape[0]
    nq = t // tq
    tk = tq
    qi, kj = _causal_pairs(nq, by_key=False)
    rs = tq // ATTN_ROW_SPLIT

    def body(qi_ref, kj_ref, q_ref, k_ref, v_ref, o_ref, lse_ref, m_sc, l_sc, acc_sc):
        pp = pl.program_id(1)
        i, j = qi_ref[pp], kj_ref[pp]

        @pl.when(j == 0)
        def _():
            m_sc[...] = jnp.full_like(m_sc, -jnp.inf)
            l_sc[...] = jnp.zeros_like(l_sc)
            acc_sc[...] = jnp.zeros_like(acc_sc)

        def update(diag):
            for r in range(ATTN_ROW_SPLIT):
                rows = slice(r * rs, (r + 1) * rs)
                s = _attn_scores(q_ref[rows, :], k_ref[...])
                if diag:
                    s = jnp.where(_diag_mask(rs, tk, r * rs), s, -jnp.inf)
                m_prev = m_sc[rows, :]
                m_new = jnp.maximum(m_prev, jnp.max(s, axis=1, keepdims=True))
                p = jnp.exp2(s - m_new)
                alpha = jnp.exp2(m_prev - m_new)
                l_new = alpha * l_sc[rows, :] + jnp.sum(p, axis=1, keepdims=True)
                acc = alpha * acc_sc[rows, :] + jnp.dot(p.astype(BF16), v_ref[...], preferred_element_type=F32)
                if diag:
                    o_ref[rows, :] = (acc / l_new).astype(o_ref.dtype)
                    lse_ref[rows, :] = m_new + jnp.log2(l_new)
                else:
                    l_sc[rows, :] = l_new
                    acc_sc[rows, :] = acc
                    m_sc[rows, :] = m_new

        @pl.when(j < i)
        def _():
            update(False)

        @pl.when(j == i)
        def _():
            update(True)

    return pl.pallas_call(
        body, name=name,
        grid_spec=pltpu.PrefetchScalarGridSpec(
            num_scalar_prefetch=2, grid=(MLA_HEADS, int(qi.shape[0])),
            in_specs=[pl.BlockSpec((tq, 2 * LANES), lambda h, pp, qi_, kj_: (qi_[pp], h)),
                      pl.BlockSpec((tk, 2 * LANES), lambda h, pp, qi_, kj_: (kj_[pp], h)),
                      pl.BlockSpec((tk, LANES), lambda h, pp, qi_, kj_: (kj_[pp], 2 * h + 1))],
            out_specs=[pl.BlockSpec((tq, LANES), lambda h, pp, qi_, kj_: (qi_[pp], h)),
                       pl.BlockSpec((None, tq, 1), lambda h, pp, qi_, kj_: (h, qi_[pp], 0))],
            scratch_shapes=[pltpu.VMEM((tq, 1), F32), pltpu.VMEM((tq, 1), F32), pltpu.VMEM((tq, LANES), F32)]),
        out_shape=[jax.ShapeDtypeStruct((t, MLA_HEADS * LANES), BF16), jax.ShapeDtypeStruct((MLA_HEADS, t, 1), F32)],
        compiler_params=_params("parallel", "arbitrary"),
    )(qi, kj, qf, kf, kvf)


def _attn_bwd_dq(qf, kf, kvf, o, do, lse, *, tq, name):
    t = qf.shape[0]
    nq = t // tq
    tk = tq
    qi, kj = _causal_pairs(nq, by_key=False)
    rs = tq // ATTN_ROW_SPLIT

    def body(qi_ref, kj_ref, q_ref, k_ref, v_ref, o_ref, do_ref, lse_ref, dq_ref, acc_sc, delta_sc):
        pp = pl.program_id(1)
        i, j = qi_ref[pp], kj_ref[pp]

        @pl.when(j == 0)
        def _():
            acc_sc[...] = jnp.zeros_like(acc_sc)
            delta_sc[...] = jnp.sum(do_ref[...].astype(F32) * o_ref[...].astype(F32), axis=1, keepdims=True)

        def update(diag):
            for r in range(ATTN_ROW_SPLIT):
                rows = slice(r * rs, (r + 1) * rs)
                s = _attn_scores(q_ref[rows, :], k_ref[...])
                p = jnp.exp2(s - lse_ref[rows, :])
                if diag:
                    p = jnp.where(_diag_mask(rs, tk, r * rs), p, 0.0)
                dp = lax.dot_general(do_ref[rows, :], v_ref[...], (((1,), (1,)), ((), ())), preferred_element_type=F32)
                ds = (p * (dp - delta_sc[rows, :])).astype(BF16)
                dq = acc_sc[rows, :] + jnp.dot(ds, k_ref[...], preferred_element_type=F32)
                if diag:
                    dq_ref[rows, :] = dq * ATTN_SCALE
                else:
                    acc_sc[rows, :] = dq

        @pl.when(j < i)
        def _():
            update(False)

        @pl.when(j == i)
        def _():
            update(True)

    qblk = lambda c: (lambda h, pp, qi_, kj_: (qi_[pp], c(h)))
    kblk = lambda c: (lambda h, pp, qi_, kj_: (kj_[pp], c(h)))
    return pl.pallas_call(
        body, name=name,
        grid_spec=pltpu.PrefetchScalarGridSpec(
            num_scalar_prefetch=2, grid=(MLA_HEADS, int(qi.shape[0])),
            in_specs=[pl.BlockSpec((tq, 2 * LANES), qblk(lambda h: h)), pl.BlockSpec((tk, 2 * LANES), kblk(lambda h: h)),
                      pl.BlockSpec((tk, LANES), kblk(lambda h: 2 * h + 1)),
                      pl.BlockSpec((tq, LANES), qblk(lambda h: h)), pl.BlockSpec((tq, LANES), qblk(lambda h: h)),
                      pl.BlockSpec((None, tq, 1), lambda h, pp, qi_, kj_: (h, qi_[pp], 0))],
            out_specs=pl.BlockSpec((tq, 2 * LANES), qblk(lambda h: h)),
            scratch_shapes=[pltpu.VMEM((tq, 2 * LANES), F32), pltpu.VMEM((tq, 1), F32)]),
        out_shape=jax.ShapeDtypeStruct((t, MLA_HEADS * 2 * LANES), F32),
        compiler_params=_params("parallel", "arbitrary"),
    )(qi, kj, qf, kf, kvf, o, do, lse)


def _attn_bwd_dkv(qf, kf, kvf, o, do, lse, *, tq, name):
    t = qf.shape[0]
    nq = t // tq
    tk = tq
    qi, kj = _causal_pairs(nq, by_key=True)
    rs = tq // ATTN_ROW_SPLIT_DKV

    def body(qi_ref, kj_ref, q_ref, k_ref, v_ref, o_ref, do_ref, lse_ref, dkv_ref, dkr_ref, dk_sc, dv_sc):
        pp = pl.program_id(1)
        i, j = qi_ref[pp], kj_ref[pp]
        tn = (((0,), (0,)), ((), ()))

        def update(diag):
            dv = dk = None
            for r in range(ATTN_ROW_SPLIT_DKV):
                rows = slice(r * rs, (r + 1) * rs)
                do_ = do_ref[rows, :]
                delta = jnp.sum(do_.astype(F32) * o_ref[rows, :].astype(F32), axis=1, keepdims=True)
                s = _attn_scores(q_ref[rows, :], k_ref[...])
                p = jnp.exp2(s - lse_ref[rows, :])
                if diag:
                    p = jnp.where(_diag_mask(rs, tk, r * rs), p, 0.0)
                dp = lax.dot_general(do_, v_ref[...], (((1,), (1,)), ((), ())), preferred_element_type=F32)
                ds = (p * (dp - delta)).astype(BF16)
                parts = (lax.dot_general(p.astype(BF16), do_, tn, preferred_element_type=F32),
                         lax.dot_general(ds, q_ref[rows, :], tn, preferred_element_type=F32))
                dv, dk = parts if dv is None else (dv + parts[0], dk + parts[1])
            if diag:
                dv_sc[...] = dv
                dk_sc[...] = dk
            else:
                dv_sc[...] += dv
                dk_sc[...] += dk

        @pl.when(i > j)
        def _():
            update(False)

        @pl.when(i == j)
        def _():
            update(True)

        @pl.when(i == nq - 1)
        def _():
            dkv_ref[:, :LANES] = (dk_sc[:, :LANES] * LN_2).astype(dkv_ref.dtype)
            dkv_ref[:, LANES:] = dv_sc[...].astype(dkv_ref.dtype)
            dkr_ref[...] = dk_sc[:, LANES:] * LN_2

    qblk = lambda c: (lambda h, pp, qi_, kj_: (qi_[pp], c(h)))
    kblk = lambda c: (lambda h, pp, qi_, kj_: (kj_[pp], c(h)))
    return pl.pallas_call(
        body, name=name,
        grid_spec=pltpu.PrefetchScalarGridSpec(
            num_scalar_prefetch=2, grid=(MLA_HEADS, int(qi.shape[0])),
            in_specs=[pl.BlockSpec((tq, 2 * LANES), qblk(lambda h: h)), pl.BlockSpec((tk, 2 * LANES), kblk(lambda h: h)),
                      pl.BlockSpec((tk, LANES), kblk(lambda h: 2 * h + 1)),
                      pl.BlockSpec((tq, LANES), qblk(lambda h: h)), pl.BlockSpec((tq, LANES), qblk(lambda h: h)),
                      pl.BlockSpec((None, tq, 1), lambda h, pp, qi_, kj_: (h, qi_[pp], 0))],
            out_specs=[pl.BlockSpec((tk, 2 * LANES), kblk(lambda h: h)), pl.BlockSpec((tk, LANES), kblk(lambda h: h))],
            scratch_shapes=[pltpu.VMEM((tk, 2 * LANES), F32), pltpu.VMEM((tk, LANES), F32)]),
        out_shape=[jax.ShapeDtypeStruct((t, MLA_HEADS * 2 * LANES), BF16), jax.ShapeDtypeStruct((t, MLA_HEADS * LANES), F32)],
        compiler_params=_params("parallel", "arbitrary"),
    )(qi, kj, qf, kf, kvf, o, do, lse)


def _rope_tables(positions):
    t = positions.shape[0]
    inv = 1.0 / (ROPE_BASE ** (jnp.arange(0, MLA_ROPE, 2, dtype=F32) / MLA_ROPE))
    ang = positions.astype(F32)[:, None] * inv
    cos, sin = jnp.cos(ang), jnp.sin(ang)
    z32, z64 = jnp.zeros((t, 32), F32), jnp.zeros((t, 64), F32)
    cpad = jnp.concatenate([cos, cos, z64], axis=1)
    s_lo = jnp.concatenate([-sin, z32, z64], axis=1)
    s_hi = jnp.concatenate([z32, sin, z64], axis=1)
    return cpad, s_lo, s_hi


def _mla_fwd(h, w, pre_g, post_g, rope, tq):
    cpad, s_lo, s_hi = rope
    hn = _prenorm(h, pre_g, "mla_prenorm")
    cin = _mm(hn, w["in"], tm=1024, tn=512, tk=1024, name="mla_in")

    def lat(c, cp, sl, sh, qg, kvg):
        cq, ckv, kr = c[:, :MLA_Q_LORA], c[:, MLA_Q_LORA:MLA_Q_LORA + MLA_KV_LORA], c[:, MLA_Q_LORA + MLA_KV_LORA:]
        return _rms(cq, qg), _rms(ckv, kvg), _rope128(kr, cp, sl, sh)
    cqn, ckvn, kr = _rowwise(lat, [cin, cpad, s_lo, s_hi], [w["q_norm_g"], w["kv_norm_g"]],
                             [(MLA_Q_LORA, BF16), (MLA_KV_LORA, BF16), (LANES, BF16)], name="mla_latent")
    q_raw = _mm(cqn, w["uq"], tm=1024, tn=1024, tk=MLA_Q_LORA, name="mla_uq")

    def rope_q(q, cp, sl, sh):
        pieces = []
        for hd in range(MLA_HEADS):
            pieces.append(q[:, 256 * hd:256 * hd + LANES] * ATTN_QSCALE)
            pieces.append(_rope128(q[:, 256 * hd + LANES:256 * hd + 256], cp, sl, sh) * ATTN_QSCALE)
        return (tuple(pieces),)
    qf = _rowwise(rope_q, [q_raw, cpad, s_lo, s_hi], [], [(4096, BF16)], name="mla_rope_q")
    kvf = _mm(ckvn, w["ukv"], out_dtype=BF16, tm=1024, tn=1024, tk=MLA_KV_LORA, name="mla_ukv")
    t = h.shape[0]
    k_nope = kvf.reshape(t, MLA_HEADS, 2 * LANES)[:, :, :LANES]
    kf = jnp.concatenate([k_nope, jnp.broadcast_to(kr[:, None, :], k_nope.shape)], axis=2).reshape(t, MLA_HEADS * 2 * LANES)
    o, lse = _attn_fwd(qf, kf, kvf, tq=tq, name="mla_attn")
    mixed = _mm(o, w["out"], tm=1024, tn=1024, tk=2048, name="mla_out")
    out = _postnorm_residual(h, mixed, post_g, 1.0, "mla_postnorm")
    return out, (h, hn, cin, cqn, ckvn, qf, kf, kvf, o, lse, mixed)


def _mla_bwd(dh, saved, w, pre_g, post_g, rope, tq):
    cpad, s_lo, s_hi = rope
    h, hn, cin, cqn, ckvn, qf, kf, kvf, o, lse, mixed = saved
    dmixed, d_post = _postnorm_bwd(mixed, dh, post_g, 1.0, "mla_postnorm_bwd")
    do = _mm(dmixed, w["out"], tb=True, out_dtype=BF16, tm=1024, tn=1024, tk=1024, name="mla_out_dx")
    d_out = _mm(o, dmixed, ta=True, tm=1024, tn=1024, tk=2048, name="mla_out_dw")
    dq = _attn_bwd_dq(qf, kf, kvf, o, do, lse, tq=tq, name="mla_attn_dq")
    dkvf, dkr_heads = _attn_bwd_dkv(qf, kf, kvf, o, do, lse, tq=tq, name="mla_attn_dkv")

    def unrope_q(d, cp, sl, sh):
        pieces = []
        for hd in range(MLA_HEADS):
            pieces.append(d[:, 256 * hd:256 * hd + LANES])
            pieces.append(_rope128(d[:, 256 * hd + LANES:256 * hd + 256], cp, -sl, -sh))
        return (tuple(pieces),)
    dq_raw = _rowwise(unrope_q, [dq, cpad, s_lo, s_hi], [], [(4096, BF16)], name="mla_rope_q_bwd")
    dcqn = _mm(dq_raw, w["uq"], tb=True, tm=1024, tn=256, tk=1024, name="mla_uq_dx")
    d_uq = _mm(cqn, dq_raw, ta=True, tm=256, tn=1024, tk=2048, name="mla_uq_dw")
    dckvn = _mm(dkvf, w["ukv"], tb=True, tm=1024, tn=128, tk=1024, name="mla_ukv_dx")
    d_ukv = _mm(ckvn, dkvf, ta=True, tm=128, tn=1024, tk=2048, name="mla_ukv_dw")

    def lat_bwd(c, dq_, dkv_, dkrh, cp, sl, sh, qg, kvg):
        cq, ckv = c[:, :MLA_Q_LORA], c[:, MLA_Q_LORA:MLA_Q_LORA + MLA_KV_LORA]
        dcq, dqg = _rms_bwd(cq, qg, dq_)
        dckv, dkvg = _rms_bwd(ckv, kvg, dkv_)
        dkr = dkrh[:, :LANES]
        for hd in range(1, MLA_HEADS):
            dkr = dkr + dkrh[:, hd * LANES:(hd + 1) * LANES]
        return (dcq, dckv, _rope128(dkr, cp, -sl, -sh)), _colsum(dqg), _colsum(dkvg)
    dcin, d_qg, d_kvg = _rowwise(lat_bwd, [cin, dcqn, dckvn, dkr_heads, cpad, s_lo, s_hi], [w["q_norm_g"], w["kv_norm_g"]],
                                 [(MLA_IN_PAD, BF16)], [(1, MLA_Q_LORA), (1, MLA_KV_LORA)], name="mla_latent_bwd")
    dhn = _mm(dcin, w["in"], tb=True, tm=1024, tn=1024, tk=512, name="mla_in_dx")
    d_in = _mm(hn, dcin, ta=True, tm=1024, tn=512, tk=2048, name="mla_in_dw")
    dh_in, d_pre = _prenorm_bwd(h, [dhn], dh, pre_g, "mla_prenorm_bwd")
    return dh_in, dict(w_in=d_in, q_norm_g=d_qg, kv_norm_g=d_kvg, w_uq=d_uq, w_ukv=d_ukv, w_out=d_out,
                       pre_g=d_pre, post_g=d_post)


def _hyb_fwd(h, w, pre_g, post_g):
    hn = _prenorm(h, pre_g, "hyb_prenorm")
    proj = _mm(hn, w["main"], tm=1024, tn=512, tk=1024, name="hyb_in")
    dtr = _mm(hn, w["dt"], tm=1024, tn=LANES, tk=1024, name="hyb_in_dt")
    ya = _gmlp_fwd(proj, w["gm_w"], w["gm_bt"], w["gm_ln_g"], w["gm_ln_b"], "gmlp")
    xbc = proj[:, 3072:]
    xsh = [_shift_down(xbc, 3 - k) for k in range(4)]
    act = _rowwise(lambda x0, x1, x2, x3, cw, cb: _silu(_conv_pre(x0, x1, x2, x3, cw, cb)), xsh,
                   [w["conv_w"], w["conv_b"]], [(SSD_CONV_CH, F32)], name="ssd_conv")
    dt_pad = _rowwise(lambda d, b: jax.nn.softplus(d + b), [dtr], [w["dt_bias"]], [(LANES, F32)], name="ssd_dt")
    y, states = _ssd_fwd(act, dt_pad, w["a"], w["d"], "ssd_scan")

    def gate_norm(y_, p_, ng):
        yg = y_ * _silu(p_[:, 2048:3072])
        return ((_rms(yg[:, :512], ng[:, :512]), _rms(yg[:, 512:], ng[:, 512:])),)
    yb = _rowwise(gate_norm, [y, proj], [w["norm_g"]], [(SSD_INNER, BF16)], name="ssd_gate_norm")
    yab = jnp.concatenate([ya, yb], axis=1)
    mixed = _mm(yab, w["out"], tm=1024, tn=1024, tk=2048, name="hyb_out")
    out = _postnorm_residual(h, mixed, post_g, 1.0, "hyb_postnorm")
    return out, (h, hn, proj, dtr, xsh, act, dt_pad, y, states, yab, mixed)


def _hyb_bwd(dh, saved, w, pre_g, post_g):
    h, hn, proj, dtr, xsh, act, dt_pad, y, states, yab, mixed = saved
    dmixed, d_post = _postnorm_bwd(mixed, dh, post_g, 1.0, "hyb_postnorm_bwd")
    dyab = _mm(dmixed, w["out"], tb=True, tm=1024, tn=1024, tk=1024, name="hyb_out_dx")
    d_out = _mm(yab, dmixed, ta=True, tm=1024, tn=1024, tk=2048, name="hyb_out_dw")

    def gate_norm_bwd(y_, p_, d, ng):
        z = p_[:, 2048:3072]
        sz = _silu(z)
        yg = y_ * sz
        d_lo, g_lo = _rms_bwd(yg[:, :512], ng[:, :512], d[:, 1024:1536])
        d_hi, g_hi = _rms_bwd(yg[:, 512:], ng[:, 512:], d[:, 1536:])
        dyg = jnp.concatenate([d_lo, d_hi], axis=1)
        return dyg * sz, dyg * y_ * _silu_grad(z), _colsum(jnp.concatenate([g_lo, g_hi], axis=1))
    dy, dz, d_norm = _rowwise(gate_norm_bwd, [y, proj, dyab], [w["norm_g"]], [(SSD_INNER, F32), (SSD_INNER, BF16)],
                              [(1, SSD_INNER)], name="ssd_gate_norm_bwd")
    dact, ddt, da_sum, dd_sum = _ssd_bwd(act, dt_pad, w["a"], w["d"], states, dy, "ssd_scan_bwd")

    def conv_bwd(x0, x1, x2, x3, da_, cw, cb):
        dpre = da_ * _silu_grad(_conv_pre(x0, x1, x2, x3, cw, cb))
        dw = jnp.concatenate([_colsum(dpre * x0), _colsum(dpre * x1), _colsum(dpre * x2), _colsum(dpre * x3)], axis=0)
        return dpre, dw, _colsum(dpre)
    dconv, d_conv_w, d_conv_b = _rowwise(conv_bwd, [*xsh, dact], [w["conv_w"], w["conv_b"]], [(SSD_CONV_CH, F32)],
                                         [(4, SSD_CONV_CH), (1, SSD_CONV_CH)], name="ssd_conv_bwd")
    dsh = [_shift_up(dconv, 3 - k) for k in range(4)]
    dxbc = _rowwise(lambda d0, d1, d2, d3, cw: d0 * cw[0:1] + d1 * cw[1:2] + d2 * cw[2:3] + d3 * cw[3:4], dsh,
                    [w["conv_w"]], [(SSD_CONV_CH, BF16)], name="ssd_conv_dx")

    def dt_bwd(dd, d, b):
        g = dd * jax.nn.sigmoid(d + b)
        g = jnp.where(lax.broadcasted_iota(jnp.int32, g.shape, 1) < SSD_HEADS, g, 0.0)
        return g, _colsum(g)
    ddtr, d_dt_bias = _rowwise(dt_bwd, [ddt, dtr], [w["dt_bias"]], [(LANES, BF16)], [(1, LANES)], name="ssd_dt_bwd")
    duv, d_gm_w, d_gm_b, d_ln_g, d_ln_b = _gmlp_bwd(proj, dyab, w["gm_w"], w["gm_bt"], w["gm_ln_g"], w["gm_ln_b"],
                                                    "gmlp_bwd")
    dproj = jnp.concatenate([duv, dz, dxbc], axis=1)
    dhn_a = _mm(dproj, w["main"], tb=True, tm=1024, tn=1024, tk=1536, name="hyb_in_dx")
    dhn_b = _mm(ddtr, w["dt"], tb=True, tm=1024, tn=1024, tk=LANES, name="hyb_in_dt_dx")
    d_main = _mm(hn, dproj, ta=True, tm=1024, tn=512, tk=2048, name="hyb_in_dw")
    d_dt = _mm(hn, ddtr, ta=True, tm=1024, tn=LANES, tk=2048, name="hyb_in_dt_dw")
    dh_in, d_pre = _prenorm_bwd(h, [dhn_a, dhn_b], dh, pre_g, "hyb_prenorm_bwd")
    grads = dict(w_in=jnp.concatenate([d_main, d_dt[:, :SSD_HEADS]], axis=1), gm_ln_g=d_ln_g, gm_ln_b=d_ln_b,
                 gm_w_s=d_gm_w, gm_b_s=d_gm_b[:, :, 0], conv_w=d_conv_w, conv_b=d_conv_b,
                 dt_bias=d_dt_bias[:, :SSD_HEADS], a_log=(da_sum * w["a"])[:, :SSD_HEADS], d=dd_sum[:, :SSD_HEADS],
                 norm_g=d_norm, w_out=d_out, pre_g=d_pre, post_g=d_post)
    return dh_in, grads


def _row(v):
    return v.reshape(1, -1).astype(F32)


def _pad_lanes(v, n=LANES):
    v = _row(v)
    return jnp.pad(v, ((0, 0), (0, n - v.shape[1])))


HYB_IN = 4624
HYB_SHARD = HYB_IN // N_DEV
HYB_SHARD_PAD = 640


def _hyb_unblock_matrix():
    n = N_DEV * HYB_SHARD_PAD
    r = lax.broadcasted_iota(jnp.int32, (n, n), 0)
    c = lax.broadcasted_iota(jnp.int32, (n, n), 1)
    j = r % HYB_SHARD_PAD
    return jnp.logical_and(j < HYB_SHARD, c == HYB_SHARD * (r // HYB_SHARD_PAD) + j).astype(BF16)


def _layer_weights(fw, sm, i):
    j = i // 2
    lw = dict(
        ffn1=dict({"in": fw["ffn1_w_in"][i], "down": fw["ffn1_w_down"][i]}),
        ffn2=dict({"in": fw["ffn2_w_in"][i], "down": fw["ffn2_w_down"][i]}),
        ple=dict(gate=fw["ple_w_gate"][i], proj=fw["ple_w_proj"][i]),
    )
    if i % 2 == 0:
        w_in = _mm(fw["hyb_w_in"][j], _hyb_unblock_matrix(), out_dtype=BF16, tm=1024, tn=512, tk=1024, name="hyb_w_unblock")
        causal = jnp.tril(jnp.ones((CHUNK, CHUNK), dtype=bool))
        lw["mix"] = {
            "main": w_in[:, :HYB_MAIN], "dt": w_in[:, HYB_MAIN:HYB_MAIN + LANES],
            "gm_w": jnp.where(causal[None], sm["gm_w_s"][j], 0.0).astype(BF16),
            "gm_bt": jnp.pad(sm["gm_b_s"][j].T, ((0, 0), (0, LANES - GM_HEADS))),
            "gm_ln_g": _row(sm["gm_ln_g"][j]), "gm_ln_b": _row(sm["gm_ln_b"][j]),
            "conv_w": fw["ssd_conv_w"][j], "conv_b": _row(sm["ssd_conv_b"][j]),
            "dt_bias": _pad_lanes(sm["ssd_dt_bias"][j]), "a": _pad_lanes(-jnp.exp(sm["ssd_a_log"][j])),
            "d": _pad_lanes(sm["ssd_d"][j]), "norm_g": _row(sm["ssd_norm_g"][j]), "out": fw["hyb_w_out"][j],
        }
    else:
        uq = fw["mla_w_uq"][j].reshape(MLA_Q_LORA, MLA_HEADS, 192)
        uq = jnp.pad(uq, ((0, 0), (0, 0), (0, 64))).reshape(MLA_Q_LORA, MLA_HEADS * 256)
        lw["mix"] = {
            "in": jnp.pad(fw["mla_w_in"][j], ((0, 0), (0, MLA_IN_PAD - MLA_IN))), "uq": uq, "ukv": fw["mla_w_ukv"][j],
            "out": fw["mla_w_out"][j], "q_norm_g": _row(fw["mla_q_norm_g"][j]), "kv_norm_g": _row(sm["mla_kv_norm_g"][j]),
        }
    return lw


def _device_step(x, p, positions, target, fw, sm):
    t = x.shape[0]
    tq = _pick(t, (1024, 512, 256, 128))
    rope = _rope_tables(positions)
    h = x
    saved, lws = [], []
    for i in range(DEPTH):
        lw = _layer_weights(fw, sm, i)
        lws.append(lw)
        h, s1 = _ffn_fwd(h, lw["ffn1"], _row(sm["ffn1_pre_g"][i]), _row(sm["ffn1_post_g"][i]), "ffn")
        if i % 2 == 0:
            h, s2 = _hyb_fwd(h, lw["mix"], _row(sm["mix_pre_g"][i]), _row(sm["mix_post_g"][i]))
        else:
            h, s2 = _mla_fwd(h, lw["mix"], _row(sm["mix_pre_g"][i]), _row(sm["mix_post_g"][i]), rope, tq)
        h, s3 = _ffn_fwd(h, lw["ffn2"], _row(sm["ffn2_pre_g"][i]), _row(sm["ffn2_post_g"][i]), "ffn")
        h, s4 = _ple_fwd(h, p[i], lw["ple"], _row(sm["ple_pre_g"][i]), _row(sm["ple_post_g"][i]))
        saved.append((s1, s2, s3, s4))

    def loss_fn(y, tg):
        err = y - tg
        return err * (1.0 / D_MODEL), jnp.sum(_colsum(err * err), axis=1, keepdims=True)
    dh, loss_sum = _rowwise(loss_fn, [h, target], [], [(D_MODEL, F32)], [(1, 1)], name="loss")
    loss = loss_sum[0, 0] * (0.5 / D_MODEL)

    per_layer = {n: [None] * DEPTH for n in WEIGHTS if n.startswith(("ffn", "mix", "ple"))}
    per_mixer = {n: [None] * (DEPTH // 2) for n in WEIGHTS if n.startswith(("hyb", "gm", "ssd", "mla"))}
    for i in reversed(range(DEPTH)):
        lw = lws[i]
        s1, s2, s3, s4 = saved[i]
        j = i // 2
        dh, g = _ple_bwd(dh, s4, p[i], lw["ple"], _row(sm["ple_pre_g"][i]), _row(sm["ple_post_g"][i]))
        for k, v in g.items():
            per_layer["ple_" + k][i] = v
        dh, g = _ffn_bwd(dh, s3, lw["ffn2"], _row(sm["ffn2_pre_g"][i]), _row(sm["ffn2_post_g"][i]), "ffn")
        for k, v in g.items():
            per_layer["ffn2_" + k][i] = v
        if i % 2 == 0:
            dh, g = _hyb_bwd(dh, s2, lw["mix"], _row(sm["mix_pre_g"][i]), _row(sm["mix_post_g"][i]))
            names = dict(w_in="hyb_w_in", gm_ln_g="gm_ln_g", gm_ln_b="gm_ln_b", gm_w_s="gm_w_s", gm_b_s="gm_b_s",
                         conv_w="ssd_conv_w", conv_b="ssd_conv_b", dt_bias="ssd_dt_bias", a_log="ssd_a_log", d="ssd_d",
                         norm_g="ssd_norm_g", w_out="hyb_w_out")
        else:
            dh, g = _mla_bwd(dh, s2, lw["mix"], _row(sm["mix_pre_g"][i]), _row(sm["mix_post_g"][i]), rope, tq)
            g["w_in"] = g["w_in"][:, :MLA_IN]
            g["w_uq"] = g["w_uq"].reshape(MLA_Q_LORA, MLA_HEADS, 256)[:, :, :192].reshape(MLA_Q_LORA, MLA_HEADS * 192)
            names = dict(w_in="mla_w_in", q_norm_g="mla_q_norm_g", kv_norm_g="mla_kv_norm_g", w_uq="mla_w_uq",
                         w_ukv="mla_w_ukv", w_out="mla_w_out")
        per_layer["mix_pre_g"][i] = g.pop("pre_g")
        per_layer["mix_post_g"][i] = g.pop("post_g")
        for k, v in g.items():
            per_mixer[names[k]][j] = v
        dh, g = _ffn_bwd(dh, s1, lw["ffn1"], _row(sm["ffn1_pre_g"][i]), _row(sm["ffn1_post_g"][i]), "ffn")
        for k, v in g.items():
            per_layer["ffn1_" + k][i] = v

    return loss, dh, {**per_layer, **per_mixer}


def _stack_layers(parts, shape):
    return jnp.stack(parts, axis=0).reshape(shape)


MESH_AXES = ("x", "y", "c")
EXCHANGE_MAX_COPIES = 56


def _exchange(src, axes, mode, name):
    n = 2 ** len(axes)
    blk = src.shape[-2:]
    flips = [tuple(a for a, bit in zip(axes, np.binary_repr(f, len(axes))) if bit == "1") for f in range(1, n)]
    prefs = tuple(c for c in (16, 8, 4, 2, 1) if c * (n - 1) <= EXCHANGE_MAX_COPIES)
    pieces = _pick(blk[0] // 16, prefs) if blk[0] % 16 == 0 else 1
    rows = blk[0] // pieces

    def index(where):
        idx = 0
        for a in axes:
            idx = idx * 2 + where[a]
        return idx

    me_out = index({a: lax.axis_index(a) for a in MESH_AXES})
    own = lax.dynamic_index_in_dim(src, me_out, 0, keepdims=False) if mode == "a2a" else src
    landing = lax.dynamic_update_index_in_dim(lax.empty((n, *blk), src.dtype), own, me_out, 0)

    def body(src_ref, landing_ref, out_ref, send_sems, recv_sems):
        del landing_ref
        pos = {a: lax.axis_index(a) for a in MESH_AXES}
        me = index(pos)
        copies = []
        for k, flip in enumerate(flips):
            peer = {a: (1 - pos[a]) if a in flip else pos[a] for a in MESH_AXES}
            payload = src_ref.at[index(peer)] if mode == "a2a" else src_ref
            for q in range(pieces):
                part = pl.ds(q * rows, rows)
                cp = pltpu.make_async_remote_copy(
                    src_ref=payload.at[part], dst_ref=out_ref.at[me, part], send_sem=send_sems.at[k * pieces + q],
                    recv_sem=recv_sems.at[k * pieces + q], device_id=(peer["x"], peer["y"], peer["c"]),
                    device_id_type=pl.DeviceIdType.MESH)
                cp.start()
                copies.append(cp)
        for cp in copies:
            cp.wait()

    n_sems = (n - 1) * pieces
    return pl.pallas_call(
        body, name=name, in_specs=[pl.BlockSpec(memory_space=pl.ANY), pl.BlockSpec(memory_space=pl.ANY)],
        out_specs=pl.BlockSpec(memory_space=pl.ANY), out_shape=jax.ShapeDtypeStruct((n, *blk), src.dtype),
        input_output_aliases={1: 0},
        scratch_shapes=[pltpu.SemaphoreType.DMA((n_sems,)), pltpu.SemaphoreType.DMA((n_sems,))],
    )(src, landing)


def _pack_rows(n_elems):
    return -(-n_elems // (16 * PACK_W)) * 16


def _pack(parts, lead=()):
    nl = len(lead)
    rows = []
    for a in parts:
        flat = a.reshape(*lead, -1)
        r = _pack_rows(flat.shape[-1])
        flat = jnp.pad(flat, [(0, 0)] * nl + [(0, r * PACK_W - flat.shape[-1])])
        rows.append(flat.reshape(*lead, r, PACK_W))
    total = sum(r.shape[nl] for r in rows)
    pad = -total % PACK_TM
    if pad:
        rows.append(jnp.zeros((*lead, pad, PACK_W), rows[0].dtype))
    return jnp.concatenate(rows, axis=nl)


def _unpack(buf, shapes, lead=()):
    nl = len(lead)
    out, r0 = [], 0
    for shp in shapes:
        n = int(np.prod(shp))
        r = _pack_rows(n)
        piece = lax.slice_in_dim(buf, r0, r0 + r, axis=nl).reshape(*lead, r * PACK_W)
        out.append(lax.slice_in_dim(piece, 0, n, axis=nl).reshape(*lead, *shp))
        r0 += r
    return out


def _split_for_devices(g, axis):
    shp = g.shape
    g = g.reshape(*shp[:axis], N_DEV, shp[axis] // N_DEV, *shp[axis + 1:])
    return jnp.moveaxis(g, axis, 0)


def _join_from_devices(parts, axis):
    parts = jnp.moveaxis(parts, 0, axis)
    shp = parts.shape
    return parts.reshape(*shp[:axis], shp[axis] * shp[axis + 1], *shp[axis + 2:])


def _adamw_terms(w, g, m, v):
    m = ADAM_B1 * m + (1.0 - ADAM_B1) * g
    v = ADAM_B2 * v + (1.0 - ADAM_B2) * (g * g)
    m_hat = m / (1.0 - ADAM_B1 ** ADAM_STEP)
    v_hat = v / (1.0 - ADAM_B2 ** ADAM_STEP)
    delta = -ADAM_LR * (m_hat / (jnp.sqrt(v_hat) + ADAM_EPS) + ADAM_WD * w)
    return delta, m, v


def _adamw_packed(w, m, v, partials, n_partials, name):
    def fn(w_, m_, v_, *parts):
        g = parts[0].astype(F32)
        for part in parts[1:]:
            g = g + part.astype(F32)
        return (g,) + _adamw_terms(w_, g, m_, v_)
    return _rowwise(fn, [w, m, v] + [(partials, s) for s in range(n_partials)], [], [(PACK_W, F32)] * 4,
                    tm=PACK_TM, name=name)


def kernel(x, p, positions, ffn1_pre_g, ffn1_w_in, ffn1_w_down, ffn1_post_g, mix_pre_g, mix_post_g, ffn2_pre_g, ffn2_w_in, ffn2_w_down, ffn2_post_g, ple_pre_g, ple_w_gate, ple_w_proj, ple_post_g, hyb_w_in, gm_ln_g, gm_ln_b, gm_w_s, gm_b_s, ssd_conv_w, ssd_conv_b, ssd_dt_bias, ssd_a_log, ssd_d, ssd_norm_g, hyb_w_out, mla_w_in, mla_q_norm_g, mla_kv_norm_g, mla_w_uq, mla_w_ukv, mla_w_out, loss_target, m_ffn1_pre_g, m_ffn1_w_in, m_ffn1_w_down, m_ffn1_post_g, m_mix_pre_g, m_mix_post_g, m_ffn2_pre_g, m_ffn2_w_in, m_ffn2_w_down, m_ffn2_post_g, m_ple_pre_g, m_ple_w_gate, m_ple_w_proj, m_ple_post_g, m_hyb_w_in, m_gm_ln_g, m_gm_ln_b, m_gm_w_s, m_gm_b_s, m_ssd_conv_w, m_ssd_conv_b, m_ssd_dt_bias, m_ssd_a_log, m_ssd_d, m_ssd_norm_g, m_hyb_w_out, m_mla_w_in, m_mla_q_norm_g, m_mla_kv_norm_g, m_mla_w_uq, m_mla_w_ukv, m_mla_w_out, v_ffn1_pre_g, v_ffn1_w_in, v_ffn1_w_down, v_ffn1_post_g, v_mix_pre_g, v_mix_post_g, v_ffn2_pre_g, v_ffn2_w_in, v_ffn2_w_down, v_ffn2_post_g, v_ple_pre_g, v_ple_w_gate, v_ple_w_proj, v_ple_post_g, v_hyb_w_in, v_gm_ln_g, v_gm_ln_b, v_gm_w_s, v_gm_b_s, v_ssd_conv_w, v_ssd_conv_b, v_ssd_dt_bias, v_ssd_a_log, v_ssd_d, v_ssd_norm_g, v_hyb_w_out, v_mla_w_in, v_mla_q_norm_g, v_mla_kv_norm_g, v_mla_w_uq, v_mla_w_ukv, v_mla_w_out):
    given = dict(locals())
    w = {n: given[n] for n in WEIGHTS}
    mom = {n: given["m_" + n] for n in WEIGHTS}
    var = {n: given["v_" + n] for n in WEIGHTS}
    shard_shapes = [w[n].shape for n in SHARDED]
    repl_shapes = [w[n].shape for n in REPLICATED]

    send16 = {n: w[n].astype(BF16) for n in SHARDED_BF16}
    send16["hyb_w_in"] = jnp.pad(send16["hyb_w_in"], ((0, 0), (0, 0), (0, HYB_SHARD_PAD - HYB_SHARD)))
    pack16 = _pack([send16[n] for n in SHARDED_BF16])
    by_chip = _exchange(pack16, ("x", "y"), "gather", "gather_weights_ici")
    by_core = _exchange(by_chip.reshape(-1, PACK_W), ("c",), "gather", "gather_weights_d2d")
    gathered = by_core.reshape(2, 4, -1, PACK_W).transpose(1, 0, 2, 3).reshape(N_DEV, -1, PACK_W)
    fw = {n: _join_from_devices(a, SHARD_AXIS[n])
          for n, a in zip(SHARDED_BF16, _unpack(gathered, [send16[n].shape for n in SHARDED_BF16], (N_DEV,)))}
    small = _exchange(_pack([w[n] for n in SHARDED_F32]), MESH_AXES, "gather", "gather_weights_f32")
    fw.update({n: _join_from_devices(a, SHARD_AXIS[n])
               for n, a in zip(SHARDED_F32, _unpack(small, [w[n].shape for n in SHARDED_F32], (N_DEV,)))})

    loss_local, grad_x, grads = _device_step(x[0], p[:, 0], positions[0], loss_target[0], fw, w)
    loss = lax.psum(loss_local, MESH_AXES)

    per_dev = []
    for n in SHARDED:
        layers = w[n].shape[0]
        whole = (1, *w[n].shape[1:SHARD_AXIS[n]], N_DEV * w[n].shape[SHARD_AXIS[n]], *w[n].shape[SHARD_AXIS[n] + 1:])
        if int(np.prod(w[n].shape[1:])) % (16 * PACK_W) == 0:
            parts = [g.reshape(whole) for g in grads[n]]
        else:
            parts = [_stack_layers(grads[n], (layers, *whole[1:]))]
        per_dev.extend(_split_for_devices(g, SHARD_AXIS[n]) for g in parts)
    per_dev = [a.reshape(4, 2, *a.shape[1:]).swapaxes(0, 1) for a in per_dev]
    gpack = _pack(per_dev, (2, 4))
    rows = gpack.shape[2]
    pair = _exchange(gpack.reshape(2, 4 * rows, PACK_W), ("c",), "a2a", "reduce_grads_d2d")
    chip_sum = _rowwise(lambda a, b: a + b, [(pair, 0), (pair, 1)], [], [(PACK_W, BF16)], tm=PACK_TM, name="reduce_grads_pair")
    quads = _exchange(chip_sum.reshape(4, rows, PACK_W), ("x", "y"), "a2a", "reduce_grads_ici")
    g_s, d_s, m_s, v_s = _adamw_packed(_pack([w[n] for n in SHARDED]), _pack([mom[n] for n in SHARDED]),
                                       _pack([var[n] for n in SHARDED]), quads, 4, "adamw_sharded")

    rpack = _pack([_stack_layers(grads[n], w[n].shape) for n in REPLICATED])
    everyone = _exchange(rpack, MESH_AXES, "gather", "gather_small_grads")
    g_r, d_r, m_r, v_r = _adamw_packed(_pack([w[n] for n in REPLICATED]), _pack([mom[n] for n in REPLICATED]),
                                       _pack([var[n] for n in REPLICATED]), everyone, N_DEV, "adamw_replicated")

    outs = []
    for sharded_buf, repl_buf in ((g_s, g_r), (d_s, d_r), (m_s, m_r), (v_s, v_r)):
        vals = dict(zip(SHARDED, _unpack(sharded_buf, shard_shapes)))
        vals.update(zip(REPLICATED, _unpack(repl_buf, repl_shapes)))
        outs.extend(vals[n] for n in WEIGHTS)
    return (loss, grad_x[None], *outs)
```

```python
import functools
import math

import jax
import jax.numpy as jnp
import numpy as np
from jax import lax
from jax.experimental import pallas as pl
from jax.experimental.pallas import tpu as pltpu

F32 = jnp.float32
BF16 = jnp.bfloat16
HIGHEST = lax.Precision.HIGHEST

V7X_VMEM_LIMIT_BYTES = 52 * 1024 * 1024
LANES = 128

D_MODEL = 1024
DEPTH = 4
D_FF = 2816
PLE_DIM = 256
NORM_EPS = 1e-6
LN_EPS = 1e-5
CHUNK = 128
GM_HEADS = 8
SSD_HEADS = 16
SSD_HEAD_DIM = 64
SSD_INNER = 1024
SSD_STATE = 128
SSD_BC = 256
SSD_CONV_CH = 1536
HYB_MAIN = 4608
MLA_HEADS = 16
MLA_Q_LORA = 256
MLA_KV_LORA = 128
MLA_ROPE = 64
MLA_IN = 448
MLA_IN_PAD = 512
ATTN_SCALE = 192.0 ** -0.5
LOG2_E = 1.4426950408889634
LN_2 = 0.6931471805599453
ATTN_QSCALE = ATTN_SCALE * LOG2_E
ROPE_BASE = 10000.0

ADAM_LR = 0.001
ADAM_B1 = 0.9
ADAM_B2 = 0.999
ADAM_EPS = 1e-08
ADAM_WD = 0.01
ADAM_STEP = 10

N_DEV = 8
PACK_W = 1024
PACK_TM = 256

WEIGHTS = ['ffn1_pre_g', 'ffn1_w_in', 'ffn1_w_down', 'ffn1_post_g', 'mix_pre_g', 'mix_post_g', 'ffn2_pre_g',
           'ffn2_w_in', 'ffn2_w_down', 'ffn2_post_g', 'ple_pre_g', 'ple_w_gate', 'ple_w_proj', 'ple_post_g',
           'hyb_w_in', 'gm_ln_g', 'gm_ln_b', 'gm_w_s', 'gm_b_s', 'ssd_conv_w', 'ssd_conv_b', 'ssd_dt_bias',
           'ssd_a_log', 'ssd_d', 'ssd_norm_g', 'hyb_w_out', 'mla_w_in', 'mla_q_norm_g', 'mla_kv_norm_g',
           'mla_w_uq', 'mla_w_ukv', 'mla_w_out']
SHARD_AXIS = {'ffn1_w_in': 2, 'ffn1_w_down': 1, 'ffn2_w_in': 2, 'ffn2_w_down': 1, 'ple_w_gate': 1, 'ple_w_proj': 2,
              'hyb_w_in': 2, 'ssd_conv_w': 2, 'hyb_w_out': 1, 'mla_w_in': 1, 'mla_q_norm_g': 1, 'mla_w_uq': 2,
              'mla_w_ukv': 2, 'mla_w_out': 1}
SHARDED = [n for n in WEIGHTS if n in SHARD_AXIS]
REPLICATED = [n for n in WEIGHTS if n not in SHARD_AXIS]
SHARDED_F32 = ['ssd_conv_w', 'mla_q_norm_g']
SHARDED_BF16 = [n for n in SHARDED if n not in SHARDED_F32]


def _params(*sem):
    return pltpu.CompilerParams(dimension_semantics=sem or None, vmem_limit_bytes=V7X_VMEM_LIMIT_BYTES)


def _pick(n, prefs):
    for t in prefs:
        if t <= n and n % t == 0:
            return t
    return n


def _mm(a, b, *, ta=False, tb=False, out_dtype=F32, tm=1024, tn=512, tk=512, b_k0=0, name):
    m, k = (a.shape[1], a.shape[0]) if ta else a.shape
    n = b.shape[0] if tb else b.shape[1]
    b_k = b.shape[1] if tb else b.shape[0]
    assert k == b_k or (tb and b_k0 + k <= b_k), (a.shape, b.shape, ta, tb, b_k0)
    tm, tn, tk = _pick(m, (tm, 512, 256, 128)), _pick(n, (tn, 512, 256, 128)), _pick(k, (tk, 512, 256, 128))
    nk = k // tk
    assert b_k0 % tk == 0
    kb0 = b_k0 // tk
    dims = (((0 if ta else 1,), (1 if tb else 0,)), ((), ()))

    def body(a_ref, b_ref, o_ref, *acc):
        part = lax.dot_general(a_ref[...].astype(BF16), b_ref[...].astype(BF16), dims, preferred_element_type=F32)
        if nk == 1:
            o_ref[...] = part.astype(o_ref.dtype)
            return
        acc_ref, = acc
        kk = pl.program_id(2)

        @pl.when(kk == 0)
        def _():
            acc_ref[...] = part

        @pl.when(kk > 0)
        def _():
            acc_ref[...] += part

        @pl.when(kk == nk - 1)
        def _():
            o_ref[...] = acc_ref[...].astype(o_ref.dtype)

    a_spec = pl.BlockSpec((tk, tm), lambda i, j, kk: (kk, i)) if ta else pl.BlockSpec((tm, tk), lambda i, j, kk: (i, kk))
    b_spec = pl.BlockSpec((tn, tk), lambda i, j, kk: (j, kk + kb0)) if tb else pl.BlockSpec((tk, tn), lambda i, j, kk: (kk, j))
    return pl.pallas_call(
        body, name=name, grid=(m // tm, n // tn, nk), in_specs=[a_spec, b_spec],
        out_specs=pl.BlockSpec((tm, tn), lambda i, j, kk: (i, j)), out_shape=jax.ShapeDtypeStruct((m, n), out_dtype),
        scratch_shapes=[] if nk == 1 else [pltpu.VMEM((tm, tn), F32)],
        compiler_params=_params("parallel", "parallel", "arbitrary"),
    )(a, b)


def _rowwise(fn, rows, consts, outs, accs=(), *, tm=256, name):
    first = rows[0][0] if isinstance(rows[0], tuple) else rows[0]
    t = first.shape[-2]
    tm = _pick(t, (tm, 256, 128, 64, 32, 16, 8))
    n_r, n_c, n_o = len(rows), len(consts), len(outs)

    def body(*refs):
        vals = [r[...] for r in refs[:n_r + n_c]]
        res = fn(*vals)
        res = res if isinstance(res, tuple) else (res,)
        o_refs, a_refs = refs[n_r + n_c:n_r + n_c + n_o], refs[n_r + n_c + n_o:]
        for o_ref, v in zip(o_refs, res[:n_o]):
            if isinstance(v, (tuple, list)):
                off = 0
                for piece in v:
                    o_ref[:, off:off + piece.shape[1]] = piece.astype(o_ref.dtype)
                    off += piece.shape[1]
            else:
                o_ref[...] = v.astype(o_ref.dtype)
        if a_refs:
            terms = res[n_o:]
            is_first = pl.program_id(0) == 0

            @pl.when(is_first)
            def _():
                for a_ref, v in zip(a_refs, terms):
                    a_ref[...] = v

            @pl.when(jnp.logical_not(is_first))
            def _():
                for a_ref, v in zip(a_refs, terms):
                    a_ref[...] += v

    in_specs, args = [], []
    for r in rows:
        if isinstance(r, tuple):
            arr, slot = r
            in_specs.append(pl.BlockSpec((None, tm, arr.shape[2]), functools.partial(lambda i, s: (s, i, 0), s=slot)))
        else:
            arr = r
            in_specs.append(pl.BlockSpec((tm, arr.shape[1]), lambda i: (i, 0)))
        args.append(arr)
    for c in consts:
        in_specs.append(pl.BlockSpec(c.shape, lambda i: (0, 0)))
        args.append(c)
    out_specs = [pl.BlockSpec((tm, c), lambda i: (i, 0)) for c, _ in outs]
    out_shape = [jax.ShapeDtypeStruct((t, c), dt) for c, dt in outs]
    for shp in accs:
        out_specs.append(pl.BlockSpec(shp, lambda i: (0, 0)))
        out_shape.append(jax.ShapeDtypeStruct(shp, F32))
    res = pl.pallas_call(
        body, name=name, grid=(t // tm,), in_specs=in_specs, out_specs=out_specs, out_shape=out_shape,
        compiler_params=_params("arbitrary" if accs else "parallel"),
    )(*args)
    return res[0] if len(res) == 1 else tuple(res)


def _colsum(v):
    return jnp.sum(v, axis=0, keepdims=True)


def _rms(x, g, eps=NORM_EPS):
    r = lax.rsqrt(jnp.mean(x * x, axis=-1, keepdims=True) + eps)
    return x * r * g


def _rms_bwd(x, g, dy, eps=NORM_EPS):
    r = lax.rsqrt(jnp.mean(x * x, axis=-1, keepdims=True) + eps)
    xh = x * r
    dyg = dy * g
    dx = r * (dyg - xh * jnp.mean(dyg * xh, axis=-1, keepdims=True))
    return dx, dy * xh


def _silu(x):
    return x * jax.nn.sigmoid(x)


def _silu_grad(x):
    s = jax.nn.sigmoid(x)
    return s * (1.0 + x * (1.0 - s))


_GELU_K = math.sqrt(2.0 / math.pi)


def _gelu(x):
    return 0.5 * x * (1.0 + jnp.tanh(_GELU_K * (x + 0.044715 * x * x * x)))


def _gelu_grad(x):
    t = jnp.tanh(_GELU_K * (x + 0.044715 * x * x * x))
    return 0.5 * (1.0 + t) + 0.5 * x * (1.0 - t * t) * _GELU_K * (1.0 + 3.0 * 0.044715 * x * x)


def _prenorm(h, g, name):
    return _rowwise(lambda x, gg: _rms(x, gg), [h], [g], [(D_MODEL, BF16)], name=name)


def _postnorm_residual(h, f, g, scale, name):
    return _rowwise(lambda x, ff, gg: x + scale * _rms(ff, gg), [h, f], [g], [(D_MODEL, F32)], name=name)


def _postnorm_bwd(f, dh, g, scale, name):
    def fn(ff, d, gg):
        dx, dgt = _rms_bwd(ff, gg, scale * d)
        return dx, _colsum(dgt)
    return _rowwise(fn, [f, dh], [g], [(D_MODEL, BF16)], [(1, D_MODEL)], name=name)


def _prenorm_bwd(h, das, dh, g, name):
    n = len(das)

    def fn(x, *rest):
        da = rest[0]
        for extra in rest[1:n]:
            da = da + extra
        d, gg = rest[n], rest[n + 1]
        dx, dgt = _rms_bwd(x, gg, da)
        return d + dx, _colsum(dgt)
    return _rowwise(fn, [h, *das, dh], [g], [(D_MODEL, F32)], [(1, D_MODEL)], name=name)


FFN_TM = 1024
FFN_TN = 256


def _ffn_in_swiglu(a, w_in, name):
    t = a.shape[0]
    tm = _pick(t, (FFN_TM, 512, 256, 128))
    nj = D_FF // FFN_TN

    def body(a_ref, wg_ref, wu_ref, gate_ref, up_ref, s_ref):
        av = a_ref[...]
        gate = jnp.dot(av, wg_ref[...], preferred_element_type=F32)
        up = jnp.dot(av, wu_ref[...], preferred_element_type=F32)
        gate_ref[...] = gate
        up_ref[...] = up
        s_ref[...] = (_silu(gate) * up).astype(s_ref.dtype)

    tile = pl.BlockSpec((tm, FFN_TN), lambda i, j: (i, j))
    return pl.pallas_call(
        body, name=name, grid=(t // tm, nj),
        in_specs=[pl.BlockSpec((tm, D_MODEL), lambda i, j: (i, 0)), pl.BlockSpec((D_MODEL, FFN_TN), lambda i, j: (0, j)),
                  pl.BlockSpec((D_MODEL, FFN_TN), lambda i, j: (0, j + nj))],
        out_specs=[tile, tile, tile],
        out_shape=[jax.ShapeDtypeStruct((t, D_FF), F32), jax.ShapeDtypeStruct((t, D_FF), F32),
                   jax.ShapeDtypeStruct((t, D_FF), BF16)],
        compiler_params=_params("parallel", "parallel"),
    )(a, w_in, w_in)


def _ffn_down_dx_swiglu(df, w_down, gate, up, name):
    t = df.shape[0]
    tm = _pick(t, (FFN_TM, 512, 256, 128))

    def body(df_ref, wd_ref, gate_ref, up_ref, dgate_ref, dup_ref):
        ds = lax.dot_general(df_ref[...], wd_ref[...], (((1,), (1,)), ((), ())), preferred_element_type=F32)
        gate = gate_ref[...]
        dgate_ref[...] = (ds * up_ref[...] * _silu_grad(gate)).astype(dgate_ref.dtype)
        dup_ref[...] = (ds * _silu(gate)).astype(dup_ref.dtype)

    tile = pl.BlockSpec((tm, FFN_TN), lambda i, j: (i, j))
    return pl.pallas_call(
        body, name=name, grid=(t // tm, D_FF // FFN_TN),
        in_specs=[pl.BlockSpec((tm, D_MODEL), lambda i, j: (i, 0)), pl.BlockSpec((FFN_TN, D_MODEL), lambda i, j: (j, 0)),
                  tile, tile],
        out_specs=[tile, tile],
        out_shape=[jax.ShapeDtypeStruct((t, D_FF), BF16), jax.ShapeDtypeStruct((t, D_FF), BF16)],
        compiler_params=_params("parallel", "parallel"),
    )(df, w_down, gate, up)


def _ffn_fwd(h, w, pre_g, post_g, tag):
    a = _prenorm(h, pre_g, tag + "_prenorm")
    gate, up, s = _ffn_in_swiglu(a, w["in"], tag + "_in_swiglu")
    f = _mm(s, w["down"], tm=1024, tn=1024, tk=D_FF, name=tag + "_down")
    out = _postnorm_residual(h, f, post_g, 0.5, tag + "_postnorm")
    return out, (h, a, gate, up, s, f)


def _ffn_bwd(dh, saved, w, pre_g, post_g, tag):
    h, a, gate, up, s, f = saved
    df, d_post = _postnorm_bwd(f, dh, post_g, 0.5, tag + "_postnorm_bwd")
    dgate, dup = _ffn_down_dx_swiglu(df, w["down"], gate, up, tag + "_down_dx_swiglu")
    d_down = _mm(s, df, ta=True, tm=1408, tn=1024, tk=1024, name=tag + "_down_dw")
    da_gate = _mm(dgate, w["in"], tb=True, tm=1024, tn=1024, tk=1408, name=tag + "_in_dx_gate")
    da_up = _mm(dup, w["in"], tb=True, tm=1024, tn=1024, tk=1408, b_k0=D_FF, name=tag + "_in_dx_up")
    d_in = jnp.concatenate([_mm(a, dgate, ta=True, tm=1024, tn=1408, tk=1024, name=tag + "_in_dw_gate"),
                            _mm(a, dup, ta=True, tm=1024, tn=1408, tk=1024, name=tag + "_in_dw_up")], axis=1)
    dh_in, d_pre = _prenorm_bwd(h, [da_gate, da_up], dh, pre_g, tag + "_prenorm_bwd")
    return dh_in, dict(w_in=d_in, w_down=d_down, pre_g=d_pre, post_g=d_post)


def _ple_fwd(h, p_i, w, pre_g, post_g):
    a = _prenorm(h, pre_g, "ple_prenorm")
    gl = _mm(a, w["gate"], tm=1024, tn=1024, tk=1024, name="ple_gate")
    e = _mm(p_i, w["proj"], tm=1024, tn=1024, tk=PLE_DIM, name="ple_proj")
    out = _rowwise(lambda x, g_, e_, gg: x + _rms(jax.nn.sigmoid(g_) * e_, gg), [h, gl, e], [post_g],
                   [(D_MODEL, F32)], name="ple_out")
    return out, (h, a, gl, e)


def _ple_bwd(dh, saved, p_i, w, pre_g, post_g):
    h, a, gl, e = saved

    def fn(g_, e_, d, gg):
        sg = jax.nn.sigmoid(g_)
        du, dgt = _rms_bwd(sg * e_, gg, d)
        return du * e_ * sg * (1.0 - sg), du * sg, _colsum(dgt)
    dgl, de, d_post = _rowwise(fn, [gl, e, dh], [post_g], [(D_MODEL, BF16), (D_MODEL, BF16)], [(1, D_MODEL)],
                               name="ple_out_bwd")
    da = _mm(dgl, w["gate"], tb=True, tm=1024, tn=1024, tk=1024, name="ple_gate_dx")
    d_gate = _mm(a, dgl, ta=True, tm=1024, tn=1024, tk=2048, name="ple_gate_dw")
    d_proj = _mm(p_i, de, ta=True, tm=PLE_DIM, tn=1024, tk=2048, name="ple_proj_dw")
    dh_in, d_pre = _prenorm_bwd(h, [da], dh, pre_g, "ple_prenorm_bwd")
    return dh_in, dict(w_gate=d_gate, w_proj=d_proj, pre_g=d_pre, post_g=d_post)


def _gm_layernorm(v, g, b):
    mu = jnp.mean(v, axis=-1, keepdims=True)
    xc = v - mu
    rstd = lax.rsqrt(jnp.mean(xc * xc, axis=-1, keepdims=True) + LN_EPS)
    vhat = xc * rstd
    return vhat, rstd, vhat * g + b


def _gmlp_fwd(proj, wm, bias_t, ln_g, ln_b, name):
    t = proj.shape[0]

    def body(uv_ref, wm_ref, bt_ref, g_ref, b_ref, o_ref):
        for hd in range(GM_HEADS):
            lo = hd * LANES
            u = _gelu(uv_ref[:, lo:lo + LANES])
            v = _gelu(uv_ref[:, 1024 + lo:1024 + lo + LANES])
            _, _, vln = _gm_layernorm(v, g_ref[:, lo:lo + LANES], b_ref[:, lo:lo + LANES])
            mixed = jnp.dot(wm_ref[hd], vln.astype(BF16), preferred_element_type=F32) + bt_ref[:, hd:hd + 1]
            o_ref[:, lo:lo + LANES] = (u * mixed).astype(o_ref.dtype)

    return pl.pallas_call(
        body, name=name, grid=(t // CHUNK,),
        in_specs=[pl.BlockSpec((CHUNK, 2048), lambda i: (i, 0)), pl.BlockSpec(wm.shape, lambda i: (0, 0, 0)),
                  pl.BlockSpec(bias_t.shape, lambda i: (0, 0)), pl.BlockSpec(ln_g.shape, lambda i: (0, 0)),
                  pl.BlockSpec(ln_b.shape, lambda i: (0, 0))],
        out_specs=pl.BlockSpec((CHUNK, 1024), lambda i: (i, 0)), out_shape=jax.ShapeDtypeStruct((t, 1024), BF16),
        compiler_params=_params("parallel"),
    )(proj, wm, bias_t, ln_g, ln_b)


def _gmlp_bwd(proj, dyab, wm, bias_t, ln_g, ln_b, name):
    t = proj.shape[0]
    nc = t // CHUNK

    def body(uv_ref, dy_ref, wm_ref, bt_ref, g_ref, b_ref, duv_ref, dw_ref, db_ref, dg_ref, dbeta_ref, dbacc):
        c = pl.program_id(0)

        @pl.when(c == 0)
        def _():
            dw_ref[...] = jnp.zeros_like(dw_ref)
            dbacc[...] = jnp.zeros_like(dbacc)
            dg_ref[...] = jnp.zeros_like(dg_ref)
            dbeta_ref[...] = jnp.zeros_like(dbeta_ref)

        for hd in range(GM_HEADS):
            lo = hd * LANES
            xu = uv_ref[:, lo:lo + LANES]
            xv = uv_ref[:, 1024 + lo:1024 + lo + LANES]
            u = _gelu(xu)
            g_h = g_ref[:, lo:lo + LANES]
            vhat, rstd, vln = _gm_layernorm(_gelu(xv), g_h, b_ref[:, lo:lo + LANES])
            vln16 = vln.astype(BF16)
            mixed = jnp.dot(wm_ref[hd], vln16, preferred_element_type=F32) + bt_ref[:, hd:hd + 1]
            dy = dy_ref[:, lo:lo + LANES]
            du = dy * mixed
            dmix = dy * u
            dmix16 = dmix.astype(BF16)
            dw_ref[hd] += lax.dot_general(dmix16, vln16, (((1,), (1,)), ((), ())), preferred_element_type=F32)
            dbacc[hd] += dmix
            dvln = lax.dot_general(wm_ref[hd], dmix16, (((0,), (0,)), ((), ())), preferred_element_type=F32)
            dg_ref[:, lo:lo + LANES] += _colsum(dvln * vhat)
            dbeta_ref[:, lo:lo + LANES] += _colsum(dvln)
            dvh = dvln * g_h
            dv = rstd * (dvh - jnp.mean(dvh, axis=-1, keepdims=True)
                         - vhat * jnp.mean(dvh * vhat, axis=-1, keepdims=True))
            duv_ref[:, lo:lo + LANES] = (du * _gelu_grad(xu)).astype(duv_ref.dtype)
            duv_ref[:, 1024 + lo:1024 + lo + LANES] = (dv * _gelu_grad(xv)).astype(duv_ref.dtype)

        @pl.when(c == nc - 1)
        def _():
            row = lax.broadcasted_iota(jnp.int32, (CHUNK, CHUNK), 0)
            col = lax.broadcasted_iota(jnp.int32, (CHUNK, CHUNK), 1)
            for hd in range(GM_HEADS):
                dw_ref[hd] = jnp.where(col <= row, dw_ref[hd], 0.0)
                db_ref[hd] = jnp.sum(dbacc[hd], axis=1, keepdims=True)

    return pl.pallas_call(
        body, name=name, grid=(nc,),
        in_specs=[pl.BlockSpec((CHUNK, 2048), lambda i: (i, 0)), pl.BlockSpec((CHUNK, 1024), lambda i: (i, 0)),
                  pl.BlockSpec(wm.shape, lambda i: (0, 0, 0)), pl.BlockSpec(bias_t.shape, lambda i: (0, 0)),
                  pl.BlockSpec(ln_g.shape, lambda i: (0, 0)), pl.BlockSpec(ln_b.shape, lambda i: (0, 0))],
        out_specs=[pl.BlockSpec((CHUNK, 2048), lambda i: (i, 0)), pl.BlockSpec((GM_HEADS, CHUNK, CHUNK), lambda i: (0, 0, 0)),
                   pl.BlockSpec((GM_HEADS, CHUNK, 1), lambda i: (0, 0, 0)), pl.BlockSpec((1, 1024), lambda i: (0, 0)),
                   pl.BlockSpec((1, 1024), lambda i: (0, 0))],
        out_shape=[jax.ShapeDtypeStruct((t, 2048), BF16), jax.ShapeDtypeStruct((GM_HEADS, CHUNK, CHUNK), F32),
                   jax.ShapeDtypeStruct((GM_HEADS, CHUNK, 1), F32), jax.ShapeDtypeStruct((1, 1024), F32),
                   jax.ShapeDtypeStruct((1, 1024), F32)],
        scratch_shapes=[pltpu.VMEM((GM_HEADS, CHUNK, CHUNK), F32)],
        compiler_params=_params("arbitrary"),
    )(proj, dyab, wm, bias_t, ln_g, ln_b)


def _ssd_chunk_terms(dt_pad, a_pad):
    row = lax.broadcasted_iota(jnp.int32, (CHUNK, CHUNK), 0)
    col = lax.broadcasted_iota(jnp.int32, (CHUNK, CHUNK), 1)
    tril = jnp.where(col <= row, 1.0, 0.0).astype(F32)
    a_cs = jnp.dot(tril, dt_pad * a_pad, precision=HIGHEST, preferred_element_type=F32)
    return a_cs, a_cs.T


def _pair_cols(mat, hd_a, lane_lt64):
    return jnp.where(lane_lt64, mat[:, hd_a:hd_a + 1], mat[:, hd_a + 1:hd_a + 2])


def _head_decay(a_cs, a_cs_t, hd, causal):
    seg = a_cs[:, hd:hd + 1] - a_cs_t[hd:hd + 1, :]
    return jnp.exp(jnp.where(causal, seg, -jnp.inf))


def _ssd_fwd(act, dt_pad, a_pad, d_pad, name):
    t = act.shape[0]
    nc = t // CHUNK

    def body(act_ref, dt_ref, a_ref, d_ref, y_ref, st_ref, h_sc):
        c = pl.program_id(0)

        @pl.when(c == 0)
        def _():
            h_sc[...] = jnp.zeros_like(h_sc)

        st_ref[...] = h_sc[...]
        row = lax.broadcasted_iota(jnp.int32, (CHUNK, CHUNK), 0)
        col = lax.broadcasted_iota(jnp.int32, (CHUNK, CHUNK), 1)
        causal = col <= row
        lane_lt64 = lax.broadcasted_iota(jnp.int32, (CHUNK, LANES), 1) < SSD_HEAD_DIM
        row_lt64 = lax.broadcasted_iota(jnp.int32, (LANES, 1), 0) < SSD_HEAD_DIM
        dt = dt_ref[...]
        a_cs, a_cs_t = _ssd_chunk_terms(dt, a_ref[...])
        last = a_cs[CHUNK - 1:CHUNK, :]
        for g in range(2):
            b16 = act_ref[:, SSD_INNER + g * SSD_STATE:SSD_INNER + (g + 1) * SSD_STATE].astype(BF16)
            c16 = act_ref[:, SSD_INNER + SSD_BC + g * SSD_STATE:SSD_INNER + SSD_BC + (g + 1) * SSD_STATE].astype(BF16)
            cb = lax.dot_general(c16, b16, (((1,), (1,)), ((), ())), preferred_element_type=F32)
            for pr in range(4):
                ha = g * 8 + pr * 2
                lo = ha * SSD_HEAD_DIM
                xs = act_ref[:, lo:lo + LANES]
                xd = xs * _pair_cols(dt, ha, lane_lt64)
                xd16 = xd.astype(BF16)
                ya = jnp.dot((cb * _head_decay(a_cs, a_cs_t, ha, causal)).astype(BF16), xd16, preferred_element_type=F32)
                yb = jnp.dot((cb * _head_decay(a_cs, a_cs_t, ha + 1, causal)).astype(BF16), xd16, preferred_element_type=F32)
                a_p = _pair_cols(a_cs, ha, lane_lt64)
                hp = h_sc[lo:lo + LANES, :]
                y_off = lax.dot_general(c16, hp.astype(BF16), (((1,), (1,)), ((), ())), preferred_element_type=F32)
                d_p = jnp.where(lane_lt64[:1], d_ref[:, ha:ha + 1], d_ref[:, ha + 1:ha + 2])
                y_ref[:, lo:lo + LANES] = jnp.where(lane_lt64, ya, yb) + y_off * jnp.exp(a_p) + d_p * xs
                last_p = jnp.where(lane_lt64[:1], last[:, ha:ha + 1], last[:, ha + 1:ha + 2])
                xw16 = (xd * jnp.exp(last_p - a_p)).astype(BF16)
                s_new = lax.dot_general(xw16, b16, (((0,), (0,)), ((), ())), preferred_element_type=F32)
                t_col = jnp.where(row_lt64, jnp.exp(last[:, ha:ha + 1]), jnp.exp(last[:, ha + 1:ha + 2]))
                h_sc[lo:lo + LANES, :] = t_col * hp + s_new

    return pl.pallas_call(
        body, name=name, grid=(nc,),
        in_specs=[pl.BlockSpec((CHUNK, SSD_CONV_CH), lambda i: (i, 0)), pl.BlockSpec((CHUNK, LANES), lambda i: (i, 0)),
                  pl.BlockSpec((1, LANES), lambda i: (0, 0)), pl.BlockSpec((1, LANES), lambda i: (0, 0))],
        out_specs=[pl.BlockSpec((CHUNK, SSD_INNER), lambda i: (i, 0)),
                   pl.BlockSpec((None, SSD_INNER, SSD_STATE), lambda i: (i, 0, 0))],
        out_shape=[jax.ShapeDtypeStruct((t, SSD_INNER), F32), jax.ShapeDtypeStruct((nc, SSD_INNER, SSD_STATE), F32)],
        scratch_shapes=[pltpu.VMEM((SSD_INNER, SSD_STATE), F32)],
        compiler_params=_params("arbitrary"),
    )(act, dt_pad, a_pad, d_pad)


def _ssd_bwd(act, dt_pad, a_pad, d_pad, states, dy, name):
    t = act.shape[0]
    nc = t // CHUNK

    def body(act_ref, dt_ref, a_ref, d_ref, st_ref, dy_ref, dact_ref, ddt_ref, da_ref, dd_ref, dh_sc):
        c = pl.program_id(0)

        @pl.when(c == 0)
        def _():
            dh_sc[...] = jnp.zeros_like(dh_sc)
            da_ref[...] = jnp.zeros_like(da_ref)
            dd_ref[...] = jnp.zeros_like(dd_ref)

        row = lax.broadcasted_iota(jnp.int32, (CHUNK, CHUNK), 0)
        col = lax.broadcasted_iota(jnp.int32, (CHUNK, CHUNK), 1)
        causal = col <= row
        lane = lax.broadcasted_iota(jnp.int32, (CHUNK, LANES), 1)
        lane_lt64 = lane < SSD_HEAD_DIM
        row_lt64 = lax.broadcasted_iota(jnp.int32, (LANES, 1), 0) < SSD_HEAD_DIM
        is_last = lax.broadcasted_iota(jnp.int32, (CHUNK, 1), 0) == CHUNK - 1
        dt = dt_ref[...]
        a_cs, a_cs_t = _ssd_chunk_terms(dt, a_ref[...])
        last = a_cs[CHUNK - 1:CHUNK, :]
        d_acs = jnp.zeros((CHUNK, LANES), F32)
        ddt_x = jnp.zeros((CHUNK, LANES), F32)
        dd_acc = jnp.zeros((1, LANES), F32)

        def head_sum(v, first):
            return jnp.sum(jnp.where(lane_lt64 if first else jnp.logical_not(lane_lt64), v, 0.0), axis=1, keepdims=True)

        for g in range(2):
            b_lo = SSD_INNER + g * SSD_STATE
            c_lo = SSD_INNER + SSD_BC + g * SSD_STATE
            b16 = act_ref[:, b_lo:b_lo + SSD_STATE].astype(BF16)
            c16 = act_ref[:, c_lo:c_lo + SSD_STATE].astype(BF16)
            cb = lax.dot_general(c16, b16, (((1,), (1,)), ((), ())), preferred_element_type=F32)
            dcb = jnp.zeros((CHUNK, CHUNK), F32)
            db_g = jnp.zeros((CHUNK, SSD_STATE), F32)
            dc_g = jnp.zeros((CHUNK, SSD_STATE), F32)
            for pr in range(4):
                ha = g * 8 + pr * 2
                lo = ha * SSD_HEAD_DIM
                xs = act_ref[:, lo:lo + LANES]
                dt_p = _pair_cols(dt, ha, lane_lt64)
                xd = xs * dt_p
                xd16 = xd.astype(BF16)
                a_p = _pair_cols(a_cs, ha, lane_lt64)
                exp_a = jnp.exp(a_p)
                last_p = jnp.where(lane_lt64[:1], last[:, ha:ha + 1], last[:, ha + 1:ha + 2])
                w_p = jnp.exp(last_p - a_p)
                hp = st_ref[lo:lo + LANES, :]
                hp16 = hp.astype(BF16)
                dhn = dh_sc[lo:lo + LANES, :]
                dhn16 = dhn.astype(BF16)
                dyp = dy_ref[:, lo:lo + LANES]
                d_p = jnp.where(lane_lt64[:1], d_ref[:, ha:ha + 1], d_ref[:, ha + 1:ha + 2])
                dd_acc = dd_acc + jnp.where(lane[:1] == ha, jnp.sum(head_sum(dyp * xs, True), axis=0, keepdims=True), 0.0) \
                    + jnp.where(lane[:1] == ha + 1, jnp.sum(head_sum(dyp * xs, False), axis=0, keepdims=True), 0.0)
                g_off = lax.dot_general(c16, hp16, (((1,), (1,)), ((), ())), preferred_element_type=F32)
                dg16 = (dyp * exp_a).astype(BF16)
                dc_g = dc_g + jnp.dot(dg16, hp16, preferred_element_type=F32)
                dh_prev = lax.dot_general(dg16, c16, (((0,), (0,)), ((), ())), preferred_element_type=F32)
                off_term = dyp * g_off * exp_a
                q = lax.dot_general(b16, dhn16, (((1,), (1,)), ((), ())), preferred_element_type=F32)
                xw16 = (xd * w_p).astype(BF16)
                db_g = db_g + jnp.dot(xw16, dhn16, preferred_element_type=F32)
                dw_term = xd * q * w_p
                dxd = w_p * q
                dt_all = dhn * hp
                dyp16 = dyp.astype(BF16)
                for k, first in ((0, True), (1, False)):
                    hd = ha + k
                    sel = lane_lt64 if first else jnp.logical_not(lane_lt64)
                    decay = _head_decay(a_cs, a_cs_t, hd, causal)
                    m = cb * decay
                    dy_h = jnp.where(sel, dyp16, jnp.zeros_like(dyp16))
                    dm = lax.dot_general(dy_h, xd16, (((1,), (1,)), ((), ())), preferred_element_type=F32)
                    dcb = dcb + dm * decay
                    dseg = dm * m
                    dxd = dxd + jnp.where(sel, lax.dot_general(m.astype(BF16), dyp16, (((0,), (0,)), ((), ())),
                                                               preferred_element_type=F32), 0.0)
                    d_col = jnp.sum(dseg, axis=1, keepdims=True) - jnp.sum(dseg.T, axis=1, keepdims=True)
                    dw_col = head_sum(dw_term, first)
                    d_col = d_col + head_sum(off_term, first) - dw_col
                    t_h = jnp.exp(last[:, hd:hd + 1])
                    dt_sum = jnp.sum(jnp.sum(jnp.where(row_lt64 if first else jnp.logical_not(row_lt64), dt_all, 0.0),
                                             axis=1, keepdims=True), axis=0, keepdims=True)
                    end_term = jnp.sum(dw_col, axis=0, keepdims=True) + dt_sum * t_h
                    d_col = d_col + jnp.where(is_last, end_term, 0.0)
                    d_acs = d_acs + jnp.where(lane == hd, d_col, 0.0)
                t_col = jnp.where(row_lt64, jnp.exp(last[:, ha:ha + 1]), jnp.exp(last[:, ha + 1:ha + 2]))
                dh_sc[lo:lo + LANES, :] = t_col * dhn + dh_prev
                dact_ref[:, lo:lo + LANES] = d_p * dyp + dxd * dt_p
                ddt_all = dxd * xs
                ddt_x = ddt_x + jnp.where(lane == ha, head_sum(ddt_all, True), 0.0) \
                    + jnp.where(lane == ha + 1, head_sum(ddt_all, False), 0.0)
            dcb16 = dcb.astype(BF16)
            dact_ref[:, b_lo:b_lo + SSD_STATE] = db_g + lax.dot_general(dcb16, c16, (((0,), (0,)), ((), ())),
                                                                          preferred_element_type=F32)
            dact_ref[:, c_lo:c_lo + SSD_STATE] = dc_g + jnp.dot(dcb16, b16, preferred_element_type=F32)
        triu = jnp.where(col >= row, 1.0, 0.0).astype(F32)
        dda = jnp.dot(triu, d_acs, precision=HIGHEST, preferred_element_type=F32)
        ddt_ref[...] = dda * a_ref[...] + ddt_x
        da_ref[...] += _colsum(dda * dt)
        dd_ref[...] += dd_acc

    rev = lambda i: (nc - 1 - i, 0)
    return pl.pallas_call(
        body, name=name, grid=(nc,),
        in_specs=[pl.BlockSpec((CHUNK, SSD_CONV_CH), rev), pl.BlockSpec((CHUNK, LANES), rev),
                  pl.BlockSpec((1, LANES), lambda i: (0, 0)), pl.BlockSpec((1, LANES), lambda i: (0, 0)),
                  pl.BlockSpec((None, SSD_INNER, SSD_STATE), lambda i: (nc - 1 - i, 0, 0)),
                  pl.BlockSpec((CHUNK, SSD_INNER), rev)],
        out_specs=[pl.BlockSpec((CHUNK, SSD_CONV_CH), rev), pl.BlockSpec((CHUNK, LANES), rev),
                   pl.BlockSpec((1, LANES), lambda i: (0, 0)), pl.BlockSpec((1, LANES), lambda i: (0, 0))],
        out_shape=[jax.ShapeDtypeStruct((t, SSD_CONV_CH), F32), jax.ShapeDtypeStruct((t, LANES), F32),
                   jax.ShapeDtypeStruct((1, LANES), F32), jax.ShapeDtypeStruct((1, LANES), F32)],
        scratch_shapes=[pltpu.VMEM((SSD_INNER, SSD_STATE), F32)],
        compiler_params=_params("arbitrary"),
    )(act, dt_pad, a_pad, d_pad, states, dy)


def _shift_down(x, k):
    return x if k == 0 else jnp.pad(x, ((k, 0), (0, 0)))[:x.shape[0]]


def _shift_up(x, k):
    return x if k == 0 else jnp.pad(x, ((0, k), (0, 0)))[k:]


def _conv_pre(x0, x1, x2, x3, w, b):
    return x0 * w[0:1] + x1 * w[1:2] + x2 * w[2:3] + x3 * w[3:4] + b


def _rope128(x, cpad, s_lo, s_hi):
    return x * cpad + pltpu.roll(x, 96, 1) * s_lo + pltpu.roll(x, 32, 1) * s_hi


ATTN_ROW_SPLIT = 4
ATTN_ROW_SPLIT_DKV = 2


def _diag_mask(rows, cols, row0):
    return lax.broadcasted_iota(jnp.int32, (rows, cols), 1) <= row0 + lax.broadcasted_iota(jnp.int32, (rows, cols), 0)


def _attn_scores(q, k):
    return lax.dot_general(q, k, (((1,), (1,)), ((), ())), preferred_element_type=F32)


def _causal_pairs(nq, by_key):
    if by_key:
        pairs = [(i, j) for j in range(nq) for i in range(j, nq)]
    else:
        pairs = [(i, j) for i in range(nq) for j in range(i + 1)]
    return (jnp.asarray([pr[0] for pr in pairs], jnp.int32), jnp.asarray([pr[1] for pr in pairs], jnp.int32))


def _attn_fwd(qf, kf, kvf, *, tq, name):
    t = qf.shape[0]
    nq = t // tq
    tk = tq
    qi, kj = _causal_pairs(nq, by_key=False)
    rs = tq // ATTN_ROW_SPLIT

    def body(qi_ref, kj_ref, q_ref, k_ref, v_ref, o_ref, lse_ref, m_sc, l_sc, acc_sc):
        pp = pl.program_id(1)
        i, j = qi_ref[pp], kj_ref[pp]

        @pl.when(j == 0)
        def _():
            m_sc[...] = jnp.full_like(m_sc, -jnp.inf)
            l_sc[...] = jnp.zeros_like(l_sc)
            acc_sc[...] = jnp.zeros_like(acc_sc)

        def update(diag):
            for r in range(ATTN_ROW_SPLIT):
                rows = slice(r * rs, (r + 1) * rs)
                keys = slice(0, (r + 1) * rs if diag else tk)
                s = _attn_scores(q_ref[rows, :], k_ref[keys, :])
                if diag:
                    s = jnp.where(_diag_mask(rs, keys.stop, r * rs), s, -jnp.inf)
                m_prev = m_sc[rows, :]
                m_new = jnp.maximum(m_prev, jnp.max(s, axis=1, keepdims=True))
                p = jnp.exp2(s - m_new)
                alpha = jnp.exp2(m_prev - m_new)
                l_new = alpha * l_sc[rows, :] + jnp.sum(p, axis=1, keepdims=True)
                acc = alpha * acc_sc[rows, :] + jnp.dot(p.astype(BF16), v_ref[keys, :], preferred_element_type=F32)
                if diag:
                    o_ref[rows, :] = (acc / l_new).astype(o_ref.dtype)
                    lse_ref[rows, :] = m_new + jnp.log2(l_new)
                else:
                    l_sc[rows, :] = l_new
                    acc_sc[rows, :] = acc
                    m_sc[rows, :] = m_new

        @pl.when(j < i)
        def _():
            update(False)

        @pl.when(j == i)
        def _():
            update(True)

    return pl.pallas_call(
        body, name=name,
        grid_spec=pltpu.PrefetchScalarGridSpec(
            num_scalar_prefetch=2, grid=(MLA_HEADS, int(qi.shape[0])),
            in_specs=[pl.BlockSpec((tq, 2 * LANES), lambda h, pp, qi_, kj_: (qi_[pp], h)),
                      pl.BlockSpec((tk, 2 * LANES), lambda h, pp, qi_, kj_: (kj_[pp], h)),
                      pl.BlockSpec((tk, LANES), lambda h, pp, qi_, kj_: (kj_[pp], 2 * h + 1))],
            out_specs=[pl.BlockSpec((tq, LANES), lambda h, pp, qi_, kj_: (qi_[pp], h)),
                       pl.BlockSpec((None, tq, 1), lambda h, pp, qi_, kj_: (h, qi_[pp], 0))],
            scratch_shapes=[pltpu.VMEM((tq, 1), F32), pltpu.VMEM((tq, 1), F32), pltpu.VMEM((tq, LANES), F32)]),
        out_shape=[jax.ShapeDtypeStruct((t, MLA_HEADS * LANES), BF16), jax.ShapeDtypeStruct((MLA_HEADS, t, 1), F32)],
        compiler_params=_params("parallel", "arbitrary"),
    )(qi, kj, qf, kf, kvf)


def _attn_bwd_dq(qf, kf, kvf, o, do, lse, *, tq, name):
    t = qf.shape[0]
    nq = t // tq
    tk = tq
    qi, kj = _causal_pairs(nq, by_key=False)
    rs = tq // ATTN_ROW_SPLIT

    def body(qi_ref, kj_ref, q_ref, k_ref, v_ref, o_ref, do_ref, lse_ref, dq_ref, acc_sc, delta_sc):
        pp = pl.program_id(1)
        i, j = qi_ref[pp], kj_ref[pp]

        @pl.when(j == 0)
        def _():
            acc_sc[...] = jnp.zeros_like(acc_sc)
            delta_sc[...] = jnp.sum(do_ref[...].astype(F32) * o_ref[...].astype(F32), axis=1, keepdims=True)

        def update(diag):
            for r in range(ATTN_ROW_SPLIT):
                rows = slice(r * rs, (r + 1) * rs)
                keys = slice(0, (r + 1) * rs if diag else tk)
                s = _attn_scores(q_ref[rows, :], k_ref[keys, :])
                p = jnp.exp2(s - lse_ref[rows, :])
                if diag:
                    p = jnp.where(_diag_mask(rs, keys.stop, r * rs), p, 0.0)
                dp = lax.dot_general(do_ref[rows, :], v_ref[keys, :], (((1,), (1,)), ((), ())), preferred_element_type=F32)
                ds = (p * (dp - delta_sc[rows, :])).astype(BF16)
                dq = acc_sc[rows, :] + jnp.dot(ds, k_ref[keys, :], preferred_element_type=F32)
                if diag:
                    dq_ref[rows, :] = dq * ATTN_SCALE
                else:
                    acc_sc[rows, :] = dq

        @pl.when(j < i)
        def _():
            update(False)

        @pl.when(j == i)
        def _():
            update(True)

    qblk = lambda c: (lambda h, pp, qi_, kj_: (qi_[pp], c(h)))
    kblk = lambda c: (lambda h, pp, qi_, kj_: (kj_[pp], c(h)))
    return pl.pallas_call(
        body, name=name,
        grid_spec=pltpu.PrefetchScalarGridSpec(
            num_scalar_prefetch=2, grid=(MLA_HEADS, int(qi.shape[0])),
            in_specs=[pl.BlockSpec((tq, 2 * LANES), qblk(lambda h: h)), pl.BlockSpec((tk, 2 * LANES), kblk(lambda h: h)),
                      pl.BlockSpec((tk, LANES), kblk(lambda h: 2 * h + 1)),
                      pl.BlockSpec((tq, LANES), qblk(lambda h: h)), pl.BlockSpec((tq, LANES), qblk(lambda h: h)),
                      pl.BlockSpec((None, tq, 1), lambda h, pp, qi_, kj_: (h, qi_[pp], 0))],
            out_specs=pl.BlockSpec((tq, 2 * LANES), qblk(lambda h: h)),
            scratch_shapes=[pltpu.VMEM((tq, 2 * LANES), F32), pltpu.VMEM((tq, 1), F32)]),
        out_shape=jax.ShapeDtypeStruct((t, MLA_HEADS * 2 * LANES), F32),
        compiler_params=_params("parallel", "arbitrary"),
    )(qi, kj, qf, kf, kvf, o, do, lse)


def _attn_bwd_dkv(qf, kf, kvf, o, do, lse, *, tq, name):
    t = qf.shape[0]
    nq = t // tq
    tk = tq
    qi, kj = _causal_pairs(nq, by_key=True)
    rs = tq // ATTN_ROW_SPLIT_DKV

    def body(qi_ref, kj_ref, q_ref, k_ref, v_ref, o_ref, do_ref, lse_ref, dkv_ref, dkr_ref, dk_sc, dv_sc):
        pp = pl.program_id(1)
        i, j = qi_ref[pp], kj_ref[pp]
        tn = (((0,), (0,)), ((), ()))

        def update(diag):
            if diag:
                dv_sc[...] = jnp.zeros_like(dv_sc)
                dk_sc[...] = jnp.zeros_like(dk_sc)
            for r in range(ATTN_ROW_SPLIT_DKV):
                rows = slice(r * rs, (r + 1) * rs)
                keys = slice(0, (r + 1) * rs if diag else tk)
                do_ = do_ref[rows, :]
                delta = jnp.sum(do_.astype(F32) * o_ref[rows, :].astype(F32), axis=1, keepdims=True)
                s = _attn_scores(q_ref[rows, :], k_ref[keys, :])
                p = jnp.exp2(s - lse_ref[rows, :])
                if diag:
                    p = jnp.where(_diag_mask(rs, keys.stop, r * rs), p, 0.0)
                dp = lax.dot_general(do_, v_ref[keys, :], (((1,), (1,)), ((), ())), preferred_element_type=F32)
                ds = (p * (dp - delta)).astype(BF16)
                dv_sc[keys, :] += lax.dot_general(p.astype(BF16), do_, tn, preferred_element_type=F32)
                dk_sc[keys, :] += lax.dot_general(ds, q_ref[rows, :], tn, preferred_element_type=F32)

        @pl.when(i > j)
        def _():
            update(False)

        @pl.when(i == j)
        def _():
            update(True)

        @pl.when(i == nq - 1)
        def _():
            dkv_ref[:, :LANES] = (dk_sc[:, :LANES] * LN_2).astype(dkv_ref.dtype)
            dkv_ref[:, LANES:] = dv_sc[...].astype(dkv_ref.dtype)
            dkr_ref[...] = dk_sc[:, LANES:] * LN_2

    qblk = lambda c: (lambda h, pp, qi_, kj_: (qi_[pp], c(h)))
    kblk = lambda c: (lambda h, pp, qi_, kj_: (kj_[pp], c(h)))
    return pl.pallas_call(
        body, name=name,
        grid_spec=pltpu.PrefetchScalarGridSpec(
            num_scalar_prefetch=2, grid=(MLA_HEADS, int(qi.shape[0])),
            in_specs=[pl.BlockSpec((tq, 2 * LANES), qblk(lambda h: h)), pl.BlockSpec((tk, 2 * LANES), kblk(lambda h: h)),
                      pl.BlockSpec((tk, LANES), kblk(lambda h: 2 * h + 1)),
                      pl.BlockSpec((tq, LANES), qblk(lambda h: h)), pl.BlockSpec((tq, LANES), qblk(lambda h: h)),
                      pl.BlockSpec((None, tq, 1), lambda h, pp, qi_, kj_: (h, qi_[pp], 0))],
            out_specs=[pl.BlockSpec((tk, 2 * LANES), kblk(lambda h: h)), pl.BlockSpec((tk, LANES), kblk(lambda h: h))],
            scratch_shapes=[pltpu.VMEM((tk, 2 * LANES), F32), pltpu.VMEM((tk, LANES), F32)]),
        out_shape=[jax.ShapeDtypeStruct((t, MLA_HEADS * 2 * LANES), BF16), jax.ShapeDtypeStruct((t, MLA_HEADS * LANES), F32)],
        compiler_params=_params("parallel", "arbitrary"),
    )(qi, kj, qf, kf, kvf, o, do, lse)


def _rope_tables(positions):
    t = positions.shape[0]
    inv = 1.0 / (ROPE_BASE ** (jnp.arange(0, MLA_ROPE, 2, dtype=F32) / MLA_ROPE))
    ang = positions.astype(F32)[:, None] * inv
    cos, sin = jnp.cos(ang), jnp.sin(ang)
    z32, z64 = jnp.zeros((t, 32), F32), jnp.zeros((t, 64), F32)
    cpad = jnp.concatenate([cos, cos, z64], axis=1)
    s_lo = jnp.concatenate([-sin, z32, z64], axis=1)
    s_hi = jnp.concatenate([z32, sin, z64], axis=1)
    return cpad, s_lo, s_hi


def _mla_fwd(h, w, pre_g, post_g, rope, tq):
    cpad, s_lo, s_hi = rope
    hn = _prenorm(h, pre_g, "mla_prenorm")
    cin = _mm(hn, w["in"], tm=1024, tn=512, tk=1024, name="mla_in")

    def lat(c, cp, sl, sh, qg, kvg):
        cq, ckv, kr = c[:, :MLA_Q_LORA], c[:, MLA_Q_LORA:MLA_Q_LORA + MLA_KV_LORA], c[:, MLA_Q_LORA + MLA_KV_LORA:]
        return _rms(cq, qg), _rms(ckv, kvg), _rope128(kr, cp, sl, sh)
    cqn, ckvn, kr = _rowwise(lat, [cin, cpad, s_lo, s_hi], [w["q_norm_g"], w["kv_norm_g"]],
                             [(MLA_Q_LORA, BF16), (MLA_KV_LORA, BF16), (LANES, BF16)], name="mla_latent")
    q_raw = _mm(cqn, w["uq"], tm=1024, tn=1024, tk=MLA_Q_LORA, name="mla_uq")

    def rope_q(q, cp, sl, sh):
        pieces = []
        for hd in range(MLA_HEADS):
            pieces.append(q[:, 256 * hd:256 * hd + LANES] * ATTN_QSCALE)
            pieces.append(_rope128(q[:, 256 * hd + LANES:256 * hd + 256], cp, sl, sh) * ATTN_QSCALE)
        return (tuple(pieces),)
    qf = _rowwise(rope_q, [q_raw, cpad, s_lo, s_hi], [], [(4096, BF16)], name="mla_rope_q")
    kvf = _mm(ckvn, w["ukv"], out_dtype=BF16, tm=1024, tn=1024, tk=MLA_KV_LORA, name="mla_ukv")
    t = h.shape[0]
    k_nope = kvf.reshape(t, MLA_HEADS, 2 * LANES)[:, :, :LANES]
    kf = jnp.concatenate([k_nope, jnp.broadcast_to(kr[:, None, :], k_nope.shape)], axis=2).reshape(t, MLA_HEADS * 2 * LANES)
    o, lse = _attn_fwd(qf, kf, kvf, tq=tq, name="mla_attn")
    mixed = _mm(o, w["out"], tm=1024, tn=1024, tk=2048, name="mla_out")
    out = _postnorm_residual(h, mixed, post_g, 1.0, "mla_postnorm")
    return out, (h, hn, cin, cqn, ckvn, qf, kf, kvf, o, lse, mixed)


def _mla_bwd(dh, saved, w, pre_g, post_g, rope, tq):
    cpad, s_lo, s_hi = rope
    h, hn, cin, cqn, ckvn, qf, kf, kvf, o, lse, mixed = saved
    dmixed, d_post = _postnorm_bwd(mixed, dh, post_g, 1.0, "mla_postnorm_bwd")
    do = _mm(dmixed, w["out"], tb=True, out_dtype=BF16, tm=1024, tn=1024, tk=1024, name="mla_out_dx")
    d_out = _mm(o, dmixed, ta=True, tm=1024, tn=1024, tk=2048, name="mla_out_dw")
    dq = _attn_bwd_dq(qf, kf, kvf, o, do, lse, tq=tq, name="mla_attn_dq")
    dkvf, dkr_heads = _attn_bwd_dkv(qf, kf, kvf, o, do, lse, tq=tq, name="mla_attn_dkv")

    def unrope_q(d, cp, sl, sh):
        pieces = []
        for hd in range(MLA_HEADS):
            pieces.append(d[:, 256 * hd:256 * hd + LANES])
            pieces.append(_rope128(d[:, 256 * hd + LANES:256 * hd + 256], cp, -sl, -sh))
        return (tuple(pieces),)
    dq_raw = _rowwise(unrope_q, [dq, cpad, s_lo, s_hi], [], [(4096, BF16)], name="mla_rope_q_bwd")
    dcqn = _mm(dq_raw, w["uq"], tb=True, tm=1024, tn=256, tk=1024, name="mla_uq_dx")
    d_uq = _mm(cqn, dq_raw, ta=True, tm=256, tn=1024, tk=2048, name="mla_uq_dw")
    dckvn = _mm(dkvf, w["ukv"], tb=True, tm=1024, tn=128, tk=1024, name="mla_ukv_dx")
    d_ukv = _mm(ckvn, dkvf, ta=True, tm=128, tn=1024, tk=2048, name="mla_ukv_dw")

    def lat_bwd(c, dq_, dkv_, dkrh, cp, sl, sh, qg, kvg):
        cq, ckv = c[:, :MLA_Q_LORA], c[:, MLA_Q_LORA:MLA_Q_LORA + MLA_KV_LORA]
        dcq, dqg = _rms_bwd(cq, qg, dq_)
        dckv, dkvg = _rms_bwd(ckv, kvg, dkv_)
        dkr = dkrh[:, :LANES]
        for hd in range(1, MLA_HEADS):
            dkr = dkr + dkrh[:, hd * LANES:(hd + 1) * LANES]
        return (dcq, dckv, _rope128(dkr, cp, -sl, -sh)), _colsum(dqg), _colsum(dkvg)
    dcin, d_qg, d_kvg = _rowwise(lat_bwd, [cin, dcqn, dckvn, dkr_heads, cpad, s_lo, s_hi], [w["q_norm_g"], w["kv_norm_g"]],
                                 [(MLA_IN_PAD, BF16)], [(1, MLA_Q_LORA), (1, MLA_KV_LORA)], name="mla_latent_bwd")
    dhn = _mm(dcin, w["in"], tb=True, tm=1024, tn=1024, tk=512, name="mla_in_dx")
    d_in = _mm(hn, dcin, ta=True, tm=1024, tn=512, tk=2048, name="mla_in_dw")
    dh_in, d_pre = _prenorm_bwd(h, [dhn], dh, pre_g, "mla_prenorm_bwd")
    return dh_in, dict(w_in=d_in, q_norm_g=d_qg, kv_norm_g=d_kvg, w_uq=d_uq, w_ukv=d_ukv, w_out=d_out,
                       pre_g=d_pre, post_g=d_post)


def _hyb_fwd(h, w, pre_g, post_g):
    hn = _prenorm(h, pre_g, "hyb_prenorm")
    proj = _mm(hn, w["main"], tm=1024, tn=512, tk=1024, name="hyb_in")
    dtr = _mm(hn, w["dt"], tm=1024, tn=LANES, tk=1024, name="hyb_in_dt")
    ya = _gmlp_fwd(proj, w["gm_w"], w["gm_bt"], w["gm_ln_g"], w["gm_ln_b"], "gmlp")
    xbc = proj[:, 3072:]
    xsh = [_shift_down(xbc, 3 - k) for k in range(4)]
    act = _rowwise(lambda x0, x1, x2, x3, cw, cb: _silu(_conv_pre(x0, x1, x2, x3, cw, cb)), xsh,
                   [w["conv_w"], w["conv_b"]], [(SSD_CONV_CH, F32)], name="ssd_conv")
    dt_pad = _rowwise(lambda d, b: jax.nn.softplus(d + b), [dtr], [w["dt_bias"]], [(LANES, F32)], name="ssd_dt")
    y, states = _ssd_fwd(act, dt_pad, w["a"], w["d"], "ssd_scan")

    def gate_norm(y_, p_, ng):
        yg = y_ * _silu(p_[:, 2048:3072])
        return ((_rms(yg[:, :512], ng[:, :512]), _rms(yg[:, 512:], ng[:, 512:])),)
    yb = _rowwise(gate_norm, [y, proj], [w["norm_g"]], [(SSD_INNER, BF16)], name="ssd_gate_norm")
    yab = jnp.concatenate([ya, yb], axis=1)
    mixed = _mm(yab, w["out"], tm=1024, tn=1024, tk=2048, name="hyb_out")
    out = _postnorm_residual(h, mixed, post_g, 1.0, "hyb_postnorm")
    return out, (h, hn, proj, dtr, xsh, act, dt_pad, y, states, yab, mixed)


def _hyb_bwd(dh, saved, w, pre_g, post_g):
    h, hn, proj, dtr, xsh, act, dt_pad, y, states, yab, mixed = saved
    dmixed, d_post = _postnorm_bwd(mixed, dh, post_g, 1.0, "hyb_postnorm_bwd")
    dyab = _mm(dmixed, w["out"], tb=True, tm=1024, tn=1024, tk=1024, name="hyb_out_dx")
    d_out = _mm(yab, dmixed, ta=True, tm=1024, tn=1024, tk=2048, name="hyb_out_dw")

    def gate_norm_bwd(y_, p_, d, ng):
        z = p_[:, 2048:3072]
        sz = _silu(z)
        yg = y_ * sz
        d_lo, g_lo = _rms_bwd(yg[:, :512], ng[:, :512], d[:, 1024:1536])
        d_hi, g_hi = _rms_bwd(yg[:, 512:], ng[:, 512:], d[:, 1536:])
        dyg = jnp.concatenate([d_lo, d_hi], axis=1)
        return dyg * sz, dyg * y_ * _silu_grad(z), _colsum(jnp.concatenate([g_lo, g_hi], axis=1))
    dy, dz, d_norm = _rowwise(gate_norm_bwd, [y, proj, dyab], [w["norm_g"]], [(SSD_INNER, F32), (SSD_INNER, BF16)],
                              [(1, SSD_INNER)], name="ssd_gate_norm_bwd")
    dact, ddt, da_sum, dd_sum = _ssd_bwd(act, dt_pad, w["a"], w["d"], states, dy, "ssd_scan_bwd")

    def conv_bwd(x0, x1, x2, x3, da_, cw, cb):
        dpre = da_ * _silu_grad(_conv_pre(x0, x1, x2, x3, cw, cb))
        dw = jnp.concatenate([_colsum(dpre * x0), _colsum(dpre * x1), _colsum(dpre * x2), _colsum(dpre * x3)], axis=0)
        return dpre, dw, _colsum(dpre)
    dconv, d_conv_w, d_conv_b = _rowwise(conv_bwd, [*xsh, dact], [w["conv_w"], w["conv_b"]], [(SSD_CONV_CH, F32)],
                                         [(4, SSD_CONV_CH), (1, SSD_CONV_CH)], name="ssd_conv_bwd")
    dsh = [_shift_up(dconv, 3 - k) for k in range(4)]
    dxbc = _rowwise(lambda d0, d1, d2, d3, cw: d0 * cw[0:1] + d1 * cw[1:2] + d2 * cw[2:3] + d3 * cw[3:4], dsh,
                    [w["conv_w"]], [(SSD_CONV_CH, BF16)], name="ssd_conv_dx")

    def dt_bwd(dd, d, b):
        g = dd * jax.nn.sigmoid(d + b)
        g = jnp.where(lax.broadcasted_iota(jnp.int32, g.shape, 1) < SSD_HEADS, g, 0.0)
        return g, _colsum(g)
    ddtr, d_dt_bias = _rowwise(dt_bwd, [ddt, dtr], [w["dt_bias"]], [(LANES, BF16)], [(1, LANES)], name="ssd_dt_bwd")
    duv, d_gm_w, d_gm_b, d_ln_g, d_ln_b = _gmlp_bwd(proj, dyab, w["gm_w"], w["gm_bt"], w["gm_ln_g"], w["gm_ln_b"],
                                                    "gmlp_bwd")
    dproj = jnp.concatenate([duv, dz, dxbc], axis=1)
    dhn_a = _mm(dproj, w["main"], tb=True, tm=1024, tn=1024, tk=1536, name="hyb_in_dx")
    dhn_b = _mm(ddtr, w["dt"], tb=True, tm=1024, tn=1024, tk=LANES, name="hyb_in_dt_dx")
    d_main = _mm(hn, dproj, ta=True, tm=1024, tn=512, tk=2048, name="hyb_in_dw")
    d_dt = _mm(hn, ddtr, ta=True, tm=1024, tn=LANES, tk=2048, name="hyb_in_dt_dw")
    dh_in, d_pre = _prenorm_bwd(h, [dhn_a, dhn_b], dh, pre_g, "hyb_prenorm_bwd")
    grads = dict(w_in=jnp.concatenate([d_main, d_dt[:, :SSD_HEADS]], axis=1), gm_ln_g=d_ln_g, gm_ln_b=d_ln_b,
                 gm_w_s=d_gm_w, gm_b_s=d_gm_b[:, :, 0], conv_w=d_conv_w, conv_b=d_conv_b,
                 dt_bias=d_dt_bias[:, :SSD_HEADS], a_log=(da_sum * w["a"])[:, :SSD_HEADS], d=dd_sum[:, :SSD_HEADS],
                 norm_g=d_norm, w_out=d_out, pre_g=d_pre, post_g=d_post)
    return dh_in, grads


def _row(v):
    return v.reshape(1, -1).astype(F32)


def _pad_lanes(v, n=LANES):
    v = _row(v)
    return jnp.pad(v, ((0, 0), (0, n - v.shape[1])))


HYB_IN = 4624
HYB_SHARD = HYB_IN // N_DEV
HYB_SHARD_PAD = 640


def _hyb_unblock_matrix():
    n = N_DEV * HYB_SHARD_PAD
    r = lax.broadcasted_iota(jnp.int32, (n, n), 0)
    c = lax.broadcasted_iota(jnp.int32, (n, n), 1)
    j = r % HYB_SHARD_PAD
    return jnp.logical_and(j < HYB_SHARD, c == HYB_SHARD * (r // HYB_SHARD_PAD) + j).astype(BF16)


def _layer_weights(fw, sm, i):
    j = i // 2
    lw = dict(
        ffn1=dict({"in": fw["ffn1_w_in"][i], "down": fw["ffn1_w_down"][i]}),
        ffn2=dict({"in": fw["ffn2_w_in"][i], "down": fw["ffn2_w_down"][i]}),
        ple=dict(gate=fw["ple_w_gate"][i], proj=fw["ple_w_proj"][i]),
    )
    if i % 2 == 0:
        w_in = _mm(fw["hyb_w_in"][j], _hyb_unblock_matrix(), out_dtype=BF16, tm=1024, tn=512, tk=1024, name="hyb_w_unblock")
        causal = jnp.tril(jnp.ones((CHUNK, CHUNK), dtype=bool))
        lw["mix"] = {
            "main": w_in[:, :HYB_MAIN], "dt": w_in[:, HYB_MAIN:HYB_MAIN + LANES],
            "gm_w": jnp.where(causal[None], sm["gm_w_s"][j], 0.0).astype(BF16),
            "gm_bt": jnp.pad(sm["gm_b_s"][j].T, ((0, 0), (0, LANES - GM_HEADS))),
            "gm_ln_g": _row(sm["gm_ln_g"][j]), "gm_ln_b": _row(sm["gm_ln_b"][j]),
            "conv_w": fw["ssd_conv_w"][j], "conv_b": _row(sm["ssd_conv_b"][j]),
            "dt_bias": _pad_lanes(sm["ssd_dt_bias"][j]), "a": _pad_lanes(-jnp.exp(sm["ssd_a_log"][j])),
            "d": _pad_lanes(sm["ssd_d"][j]), "norm_g": _row(sm["ssd_norm_g"][j]), "out": fw["hyb_w_out"][j],
        }
    else:
        uq = fw["mla_w_uq"][j].reshape(MLA_Q_LORA, MLA_HEADS, 192)
        uq = jnp.pad(uq, ((0, 0), (0, 0), (0, 64))).reshape(MLA_Q_LORA, MLA_HEADS * 256)
        lw["mix"] = {
            "in": jnp.pad(fw["mla_w_in"][j], ((0, 0), (0, MLA_IN_PAD - MLA_IN))), "uq": uq, "ukv": fw["mla_w_ukv"][j],
            "out": fw["mla_w_out"][j], "q_norm_g": _row(fw["mla_q_norm_g"][j]), "kv_norm_g": _row(sm["mla_kv_norm_g"][j]),
        }
    return lw


def _device_step(x, p, positions, target, fw, sm):
    t = x.shape[0]
    tq = _pick(t, (1024, 512, 256, 128))
    rope = _rope_tables(positions)
    h = x
    saved, lws = [], []
    for i in range(DEPTH):
        lw = _layer_weights(fw, sm, i)
        lws.append(lw)
        h, s1 = _ffn_fwd(h, lw["ffn1"], _row(sm["ffn1_pre_g"][i]), _row(sm["ffn1_post_g"][i]), "ffn")
        if i % 2 == 0:
            h, s2 = _hyb_fwd(h, lw["mix"], _row(sm["mix_pre_g"][i]), _row(sm["mix_post_g"][i]))
        else:
            h, s2 = _mla_fwd(h, lw["mix"], _row(sm["mix_pre_g"][i]), _row(sm["mix_post_g"][i]), rope, tq)
        h, s3 = _ffn_fwd(h, lw["ffn2"], _row(sm["ffn2_pre_g"][i]), _row(sm["ffn2_post_g"][i]), "ffn")
        h, s4 = _ple_fwd(h, p[i], lw["ple"], _row(sm["ple_pre_g"][i]), _row(sm["ple_post_g"][i]))
        saved.append((s1, s2, s3, s4))

    def loss_fn(y, tg):
        err = y - tg
        return err * (1.0 / D_MODEL), jnp.sum(_colsum(err * err), axis=1, keepdims=True)
    dh, loss_sum = _rowwise(loss_fn, [h, target], [], [(D_MODEL, F32)], [(1, 1)], name="loss")
    loss = loss_sum[0, 0] * (0.5 / D_MODEL)

    per_layer = {n: [None] * DEPTH for n in WEIGHTS if n.startswith(("ffn", "mix", "ple"))}
    per_mixer = {n: [None] * (DEPTH // 2) for n in WEIGHTS if n.startswith(("hyb", "gm", "ssd", "mla"))}
    for i in reversed(range(DEPTH)):
        lw = lws[i]
        s1, s2, s3, s4 = saved[i]
        j = i // 2
        dh, g = _ple_bwd(dh, s4, p[i], lw["ple"], _row(sm["ple_pre_g"][i]), _row(sm["ple_post_g"][i]))
        for k, v in g.items():
            per_layer["ple_" + k][i] = v
        dh, g = _ffn_bwd(dh, s3, lw["ffn2"], _row(sm["ffn2_pre_g"][i]), _row(sm["ffn2_post_g"][i]), "ffn")
        for k, v in g.items():
            per_layer["ffn2_" + k][i] = v
        if i % 2 == 0:
            dh, g = _hyb_bwd(dh, s2, lw["mix"], _row(sm["mix_pre_g"][i]), _row(sm["mix_post_g"][i]))
            names = dict(w_in="hyb_w_in", gm_ln_g="gm_ln_g", gm_ln_b="gm_ln_b", gm_w_s="gm_w_s", gm_b_s="gm_b_s",
                         conv_w="ssd_conv_w", conv_b="ssd_conv_b", dt_bias="ssd_dt_bias", a_log="ssd_a_log", d="ssd_d",
                         norm_g="ssd_norm_g", w_out="hyb_w_out")
        else:
            dh, g = _mla_bwd(dh, s2, lw["mix"], _row(sm["mix_pre_g"][i]), _row(sm["mix_post_g"][i]), rope, tq)
            g["w_in"] = g["w_in"][:, :MLA_IN]
            g["w_uq"] = g["w_uq"].reshape(MLA_Q_LORA, MLA_HEADS, 256)[:, :, :192].reshape(MLA_Q_LORA, MLA_HEADS * 192)
            names = dict(w_in="mla_w_in", q_norm_g="mla_q_norm_g", kv_norm_g="mla_kv_norm_g", w_uq="mla_w_uq",
                         w_ukv="mla_w_ukv", w_out="mla_w_out")
        per_layer["mix_pre_g"][i] = g.pop("pre_g")
        per_layer["mix_post_g"][i] = g.pop("post_g")
        for k, v in g.items():
            per_mixer[names[k]][j] = v
        dh, g = _ffn_bwd(dh, s1, lw["ffn1"], _row(sm["ffn1_pre_g"][i]), _row(sm["ffn1_post_g"][i]), "ffn")
        for k, v in g.items():
            per_layer["ffn1_" + k][i] = v

    return loss, dh, {**per_layer, **per_mixer}


def _stack_layers(parts, shape):
    return jnp.stack(parts, axis=0).reshape(shape)


MESH_AXES = ("x", "y", "c")
EXCHANGE_MAX_COPIES = 56


def _exchange(src, axes, mode, name):
    n = 2 ** len(axes)
    blk = src.shape[-2:]
    flips = [tuple(a for a, bit in zip(axes, np.binary_repr(f, len(axes))) if bit == "1") for f in range(1, n)]
    prefs = tuple(c for c in (16, 8, 4, 2, 1) if c * (n - 1) <= EXCHANGE_MAX_COPIES)
    pieces = _pick(blk[0] // 16, prefs) if blk[0] % 16 == 0 else 1
    rows = blk[0] // pieces

    def index(where):
        idx = 0
        for a in axes:
            idx = idx * 2 + where[a]
        return idx

    me_out = index({a: lax.axis_index(a) for a in MESH_AXES})
    own = lax.dynamic_index_in_dim(src, me_out, 0, keepdims=False) if mode == "a2a" else src
    landing = lax.dynamic_update_index_in_dim(lax.empty((n, *blk), src.dtype), own, me_out, 0)

    def body(src_ref, landing_ref, out_ref, send_sems, recv_sems):
        del landing_ref
        pos = {a: lax.axis_index(a) for a in MESH_AXES}
        me = index(pos)
        copies = []
        for k, flip in enumerate(flips):
            peer = {a: (1 - pos[a]) if a in flip else pos[a] for a in MESH_AXES}
            payload = src_ref.at[index(peer)] if mode == "a2a" else src_ref
            for q in range(pieces):
                part = pl.ds(q * rows, rows)
                cp = pltpu.make_async_remote_copy(
                    src_ref=payload.at[part], dst_ref=out_ref.at[me, part], send_sem=send_sems.at[k * pieces + q],
                    recv_sem=recv_sems.at[k * pieces + q], device_id=(peer["x"], peer["y"], peer["c"]),
                    device_id_type=pl.DeviceIdType.MESH)
                cp.start()
                copies.append(cp)
        for cp in copies:
            cp.wait()

    n_sems = (n - 1) * pieces
    return pl.pallas_call(
        body, name=name, in_specs=[pl.BlockSpec(memory_space=pl.ANY), pl.BlockSpec(memory_space=pl.ANY)],
        out_specs=pl.BlockSpec(memory_space=pl.ANY), out_shape=jax.ShapeDtypeStruct((n, *blk), src.dtype),
        input_output_aliases={1: 0},
        scratch_shapes=[pltpu.SemaphoreType.DMA((n_sems,)), pltpu.SemaphoreType.DMA((n_sems,))],
    )(src, landing)


def _pack_rows(n_elems):
    return -(-n_elems // (16 * PACK_W)) * 16


def _pack(parts, lead=()):
    nl = len(lead)
    rows = []
    for a in parts:
        flat = a.reshape(*lead, -1)
        r = _pack_rows(flat.shape[-1])
        flat = jnp.pad(flat, [(0, 0)] * nl + [(0, r * PACK_W - flat.shape[-1])])
        rows.append(flat.reshape(*lead, r, PACK_W))
    total = sum(r.shape[nl] for r in rows)
    pad = -total % PACK_TM
    if pad:
        rows.append(jnp.zeros((*lead, pad, PACK_W), rows[0].dtype))
    return jnp.concatenate(rows, axis=nl)


def _unpack(buf, shapes, lead=()):
    nl = len(lead)
    out, r0 = [], 0
    for shp in shapes:
        n = int(np.prod(shp))
        r = _pack_rows(n)
        piece = lax.slice_in_dim(buf, r0, r0 + r, axis=nl).reshape(*lead, r * PACK_W)
        out.append(lax.slice_in_dim(piece, 0, n, axis=nl).reshape(*lead, *shp))
        r0 += r
    return out


def _split_for_devices(g, axis):
    shp = g.shape
    g = g.reshape(*shp[:axis], N_DEV, shp[axis] // N_DEV, *shp[axis + 1:])
    return jnp.moveaxis(g, axis, 0)


def _join_from_devices(parts, axis):
    parts = jnp.moveaxis(parts, 0, axis)
    shp = parts.shape
    return parts.reshape(*shp[:axis], shp[axis] * shp[axis + 1], *shp[axis + 2:])


def _adamw_terms(w, g, m, v):
    m = ADAM_B1 * m + (1.0 - ADAM_B1) * g
    v = ADAM_B2 * v + (1.0 - ADAM_B2) * (g * g)
    m_hat = m / (1.0 - ADAM_B1 ** ADAM_STEP)
    v_hat = v / (1.0 - ADAM_B2 ** ADAM_STEP)
    delta = -ADAM_LR * (m_hat / (jnp.sqrt(v_hat) + ADAM_EPS) + ADAM_WD * w)
    return delta, m, v


def _adamw_packed(w, m, v, partials, n_partials, name):
    def fn(w_, m_, v_, *parts):
        g = parts[0].astype(F32)
        for part in parts[1:]:
            g = g + part.astype(F32)
        return (g,) + _adamw_terms(w_, g, m_, v_)
    return _rowwise(fn, [w, m, v] + [(partials, s) for s in range(n_partials)], [], [(PACK_W, F32)] * 4,
                    tm=PACK_TM, name=name)


def kernel(x, p, positions, ffn1_pre_g, ffn1_w_in, ffn1_w_down, ffn1_post_g, mix_pre_g, mix_post_g, ffn2_pre_g, ffn2_w_in, ffn2_w_down, ffn2_post_g, ple_pre_g, ple_w_gate, ple_w_proj, ple_post_g, hyb_w_in, gm_ln_g, gm_ln_b, gm_w_s, gm_b_s, ssd_conv_w, ssd_conv_b, ssd_dt_bias, ssd_a_log, ssd_d, ssd_norm_g, hyb_w_out, mla_w_in, mla_q_norm_g, mla_kv_norm_g, mla_w_uq, mla_w_ukv, mla_w_out, loss_target, m_ffn1_pre_g, m_ffn1_w_in, m_ffn1_w_down, m_ffn1_post_g, m_mix_pre_g, m_mix_post_g, m_ffn2_pre_g, m_ffn2_w_in, m_ffn2_w_down, m_ffn2_post_g, m_ple_pre_g, m_ple_w_gate, m_ple_w_proj, m_ple_post_g, m_hyb_w_in, m_gm_ln_g, m_gm_ln_b, m_gm_w_s, m_gm_b_s, m_ssd_conv_w, m_ssd_conv_b, m_ssd_dt_bias, m_ssd_a_log, m_ssd_d, m_ssd_norm_g, m_hyb_w_out, m_mla_w_in, m_mla_q_norm_g, m_mla_kv_norm_g, m_mla_w_uq, m_mla_w_ukv, m_mla_w_out, v_ffn1_pre_g, v_ffn1_w_in, v_ffn1_w_down, v_ffn1_post_g, v_mix_pre_g, v_mix_post_g, v_ffn2_pre_g, v_ffn2_w_in, v_ffn2_w_down, v_ffn2_post_g, v_ple_pre_g, v_ple_w_gate, v_ple_w_proj, v_ple_post_g, v_hyb_w_in, v_gm_ln_g, v_gm_ln_b, v_gm_w_s, v_gm_b_s, v_ssd_conv_w, v_ssd_conv_b, v_ssd_dt_bias, v_ssd_a_log, v_ssd_d, v_ssd_norm_g, v_hyb_w_out, v_mla_w_in, v_mla_q_norm_g, v_mla_kv_norm_g, v_mla_w_uq, v_mla_w_ukv, v_mla_w_out):
    given = dict(locals())
    w = {n: given[n] for n in WEIGHTS}
    mom = {n: given["m_" + n] for n in WEIGHTS}
    var = {n: given["v_" + n] for n in WEIGHTS}
    shard_shapes = [w[n].shape for n in SHARDED]
    repl_shapes = [w[n].shape for n in REPLICATED]

    send16 = {n: w[n].astype(BF16) for n in SHARDED_BF16}
    send16["hyb_w_in"] = jnp.pad(send16["hyb_w_in"], ((0, 0), (0, 0), (0, HYB_SHARD_PAD - HYB_SHARD)))
    pack16 = _pack([send16[n] for n in SHARDED_BF16])
    by_chip = _exchange(pack16, ("x", "y"), "gather", "gather_weights_ici")
    by_core = _exchange(by_chip.reshape(-1, PACK_W), ("c",), "gather", "gather_weights_d2d")
    gathered = by_core.reshape(2, 4, -1, PACK_W).transpose(1, 0, 2, 3).reshape(N_DEV, -1, PACK_W)
    fw = {n: _join_from_devices(a, SHARD_AXIS[n])
          for n, a in zip(SHARDED_BF16, _unpack(gathered, [send16[n].shape for n in SHARDED_BF16], (N_DEV,)))}
    small = _exchange(_pack([w[n] for n in SHARDED_F32]), MESH_AXES, "gather", "gather_weights_f32")
    fw.update({n: _join_from_devices(a, SHARD_AXIS[n])
               for n, a in zip(SHARDED_F32, _unpack(small, [w[n].shape for n in SHARDED_F32], (N_DEV,)))})

    loss_local, grad_x, grads = _device_step(x[0], p[:, 0], positions[0], loss_target[0], fw, w)
    loss = lax.psum(loss_local, MESH_AXES)

    per_dev = []
    for n in SHARDED:
        layers = w[n].shape[0]
        whole = (1, *w[n].shape[1:SHARD_AXIS[n]], N_DEV * w[n].shape[SHARD_AXIS[n]], *w[n].shape[SHARD_AXIS[n] + 1:])
        if int(np.prod(w[n].shape[1:])) % (16 * PACK_W) == 0:
            parts = [g.reshape(whole) for g in grads[n]]
        else:
            parts = [_stack_layers(grads[n], (layers, *whole[1:]))]
        per_dev.extend(_split_for_devices(g, SHARD_AXIS[n]) for g in parts)
    per_dev = [a.reshape(4, 2, *a.shape[1:]).swapaxes(0, 1) for a in per_dev]
    gpack = _pack(per_dev, (2, 4))
    rows = gpack.shape[2]
    pair = _exchange(gpack.reshape(2, 4 * rows, PACK_W), ("c",), "a2a", "reduce_grads_d2d")
    chip_sum = _rowwise(lambda a, b: a + b, [(pair, 0), (pair, 1)], [], [(PACK_W, BF16)], tm=PACK_TM, name="reduce_grads_pair")
    quads = _exchange(chip_sum.reshape(4, rows, PACK_W), ("x", "y"), "a2a", "reduce_grads_ici")
    g_s, d_s, m_s, v_s = _adamw_packed(_pack([w[n] for n in SHARDED]), _pack([mom[n] for n in SHARDED]),
                                       _pack([var[n] for n in SHARDED]), quads, 4, "adamw_sharded")

    rpack = _pack([_stack_layers(grads[n], w[n].shape) for n in REPLICATED])
    everyone = _exchange(rpack, MESH_AXES, "gather", "gather_small_grads")
    g_r, d_r, m_r, v_r = _adamw_packed(_pack([w[n] for n in REPLICATED]), _pack([mom[n] for n in REPLICATED]),
                                       _pack([var[n] for n in REPLICATED]), everyone, N_DEV, "adamw_replicated")

    outs = []
    for sharded_buf, repl_buf in ((g_s, g_r), (d_s, d_r), (m_s, m_r), (v_s, v_r)):
        vals = dict(zip(SHARDED, _unpack(sharded_buf, shard_shapes)))
        vals.update(zip(REPLICATED, _unpack(repl_buf, repl_shapes)))
        outs.extend(vals[n] for n in WEIGHTS)
    return (loss, grad_x[None], *outs)
```

```python
import functools
import math

import jax
import jax.numpy as jnp
import numpy as np
from jax import lax
from jax.experimental import pallas as pl
from jax.experimental.pallas import tpu as pltpu

F32 = jnp.float32
BF16 = jnp.bfloat16
HIGHEST = lax.Precision.HIGHEST

V7X_VMEM_LIMIT_BYTES = 52 * 1024 * 1024
LANES = 128

D_MODEL = 1024
DEPTH = 4
D_FF = 2816
PLE_DIM = 256
NORM_EPS = 1e-6
LN_EPS = 1e-5
CHUNK = 128
GM_HEADS = 8
SSD_HEADS = 16
SSD_HEAD_DIM = 64
SSD_INNER = 1024
SSD_STATE = 128
SSD_BC = 256
SSD_CONV_CH = 1536
HYB_MAIN = 4608
MLA_HEADS = 16
MLA_Q_LORA = 256
MLA_KV_LORA = 128
MLA_ROPE = 64
MLA_IN = 448
MLA_IN_PAD = 512
ATTN_SCALE = 192.0 ** -0.5
LOG2_E = 1.4426950408889634
LN_2 = 0.6931471805599453
ATTN_QSCALE = ATTN_SCALE * LOG2_E
ROPE_BASE = 10000.0

ADAM_LR = 0.001
ADAM_B1 = 0.9
ADAM_B2 = 0.999
ADAM_EPS = 1e-08
ADAM_WD = 0.01
ADAM_STEP = 10

N_DEV = 8
PACK_W = 1024
PACK_TM = 256

WEIGHTS = ['ffn1_pre_g', 'ffn1_w_in', 'ffn1_w_down', 'ffn1_post_g', 'mix_pre_g', 'mix_post_g', 'ffn2_pre_g',
           'ffn2_w_in', 'ffn2_w_down', 'ffn2_post_g', 'ple_pre_g', 'ple_w_gate', 'ple_w_proj', 'ple_post_g',
           'hyb_w_in', 'gm_ln_g', 'gm_ln_b', 'gm_w_s', 'gm_b_s', 'ssd_conv_w', 'ssd_conv_b', 'ssd_dt_bias',
           'ssd_a_log', 'ssd_d', 'ssd_norm_g', 'hyb_w_out', 'mla_w_in', 'mla_q_norm_g', 'mla_kv_norm_g',
           'mla_w_uq', 'mla_w_ukv', 'mla_w_out']
SHARD_AXIS = {'ffn1_w_in': 2, 'ffn1_w_down': 1, 'ffn2_w_in': 2, 'ffn2_w_down': 1, 'ple_w_gate': 1, 'ple_w_proj': 2,
              'hyb_w_in': 2, 'ssd_conv_w': 2, 'hyb_w_out': 1, 'mla_w_in': 1, 'mla_q_norm_g': 1, 'mla_w_uq': 2,
              'mla_w_ukv': 2, 'mla_w_out': 1}
SHARDED = [n for n in WEIGHTS if n in SHARD_AXIS]
REPLICATED = [n for n in WEIGHTS if n not in SHARD_AXIS]
SHARDED_F32 = ['ssd_conv_w', 'mla_q_norm_g']
SHARDED_BF16 = [n for n in SHARDED if n not in SHARDED_F32]


def _params(*sem):
    return pltpu.CompilerParams(dimension_semantics=sem or None, vmem_limit_bytes=V7X_VMEM_LIMIT_BYTES)


def _pick(n, prefs):
    for t in prefs:
        if t <= n and n % t == 0:
            return t
    return n


def _mm(a, b, *, ta=False, tb=False, out_dtype=F32, tm=1024, tn=512, tk=512, b_k0=0, name):
    m, k = (a.shape[1], a.shape[0]) if ta else a.shape
    n = b.shape[0] if tb else b.shape[1]
    b_k = b.shape[1] if tb else b.shape[0]
    assert k == b_k or (tb and b_k0 + k <= b_k), (a.shape, b.shape, ta, tb, b_k0)
    tm, tn, tk = _pick(m, (tm, 512, 256, 128)), _pick(n, (tn, 512, 256, 128)), _pick(k, (tk, 512, 256, 128))
    nk = k // tk
    assert b_k0 % tk == 0
    kb0 = b_k0 // tk
    dims = (((0 if ta else 1,), (1 if tb else 0,)), ((), ()))

    def body(a_ref, b_ref, o_ref, *acc):
        part = lax.dot_general(a_ref[...].astype(BF16), b_ref[...].astype(BF16), dims, preferred_element_type=F32)
        if nk == 1:
            o_ref[...] = part.astype(o_ref.dtype)
            return
        acc_ref, = acc
        kk = pl.program_id(2)

        @pl.when(kk == 0)
        def _():
            acc_ref[...] = part

        @pl.when(kk > 0)
        def _():
            acc_ref[...] += part

        @pl.when(kk == nk - 1)
        def _():
            o_ref[...] = acc_ref[...].astype(o_ref.dtype)

    a_spec = pl.BlockSpec((tk, tm), lambda i, j, kk: (kk, i)) if ta else pl.BlockSpec((tm, tk), lambda i, j, kk: (i, kk))
    b_spec = pl.BlockSpec((tn, tk), lambda i, j, kk: (j, kk + kb0)) if tb else pl.BlockSpec((tk, tn), lambda i, j, kk: (kk, j))
    return pl.pallas_call(
        body, name=name, grid=(m // tm, n // tn, nk), in_specs=[a_spec, b_spec],
        out_specs=pl.BlockSpec((tm, tn), lambda i, j, kk: (i, j)), out_shape=jax.ShapeDtypeStruct((m, n), out_dtype),
        scratch_shapes=[] if nk == 1 else [pltpu.VMEM((tm, tn), F32)],
        compiler_params=_params("parallel", "parallel", "arbitrary"),
    )(a, b)


def _rowwise(fn, rows, consts, outs, accs=(), *, tm=256, name):
    first = rows[0][0] if isinstance(rows[0], tuple) else rows[0]
    t = first.shape[-2]
    tm = _pick(t, (tm, 256, 128, 64, 32, 16, 8))
    n_r, n_c, n_o = len(rows), len(consts), len(outs)

    def body(*refs):
        vals = [r[...] for r in refs[:n_r + n_c]]
        res = fn(*vals)
        res = res if isinstance(res, tuple) else (res,)
        o_refs, a_refs = refs[n_r + n_c:n_r + n_c + n_o], refs[n_r + n_c + n_o:]
        for o_ref, v in zip(o_refs, res[:n_o]):
            if isinstance(v, (tuple, list)):
                off = 0
                for piece in v:
                    o_ref[:, off:off + piece.shape[1]] = piece.astype(o_ref.dtype)
                    off += piece.shape[1]
            else:
                o_ref[...] = v.astype(o_ref.dtype)
        if a_refs:
            terms = res[n_o:]
            is_first = pl.program_id(0) == 0

            @pl.when(is_first)
            def _():
                for a_ref, v in zip(a_refs, terms):
                    a_ref[...] = v

            @pl.when(jnp.logical_not(is_first))
            def _():
                for a_ref, v in zip(a_refs, terms):
                    a_ref[...] += v

    in_specs, args = [], []
    for r in rows:
        if isinstance(r, tuple):
            arr, slot = r
            in_specs.append(pl.BlockSpec((None, tm, arr.shape[2]), functools.partial(lambda i, s: (s, i, 0), s=slot)))
        else:
            arr = r
            in_specs.append(pl.BlockSpec((tm, arr.shape[1]), lambda i: (i, 0)))
        args.append(arr)
    for c in consts:
        in_specs.append(pl.BlockSpec(c.shape, lambda i: (0, 0)))
        args.append(c)
    out_specs = [pl.BlockSpec((tm, c), lambda i: (i, 0)) for c, _ in outs]
    out_shape = [jax.ShapeDtypeStruct((t, c), dt) for c, dt in outs]
    for shp in accs:
        out_specs.append(pl.BlockSpec(shp, lambda i: (0, 0)))
        out_shape.append(jax.ShapeDtypeStruct(shp, F32))
    res = pl.pallas_call(
        body, name=name, grid=(t // tm,), in_specs=in_specs, out_specs=out_specs, out_shape=out_shape,
        compiler_params=_params("arbitrary" if accs else "parallel"),
    )(*args)
    return res[0] if len(res) == 1 else tuple(res)


def _colsum(v):
    return jnp.sum(v, axis=0, keepdims=True)


def _rms(x, g, eps=NORM_EPS):
    r = lax.rsqrt(jnp.mean(x * x, axis=-1, keepdims=True) + eps)
    return x * r * g


def _rms_bwd(x, g, dy, eps=NORM_EPS):
    r = lax.rsqrt(jnp.mean(x * x, axis=-1, keepdims=True) + eps)
    xh = x * r
    dyg = dy * g
    dx = r * (dyg - xh * jnp.mean(dyg * xh, axis=-1, keepdims=True))
    return dx, dy * xh


def _silu(x):
    return x * jax.nn.sigmoid(x)


def _silu_grad(x):
    s = jax.nn.sigmoid(x)
    return s * (1.0 + x * (1.0 - s))


_GELU_K = math.sqrt(2.0 / math.pi)


def _gelu(x):
    return 0.5 * x * (1.0 + jnp.tanh(_GELU_K * (x + 0.044715 * x * x * x)))


def _gelu_grad(x):
    t = jnp.tanh(_GELU_K * (x + 0.044715 * x * x * x))
    return 0.5 * (1.0 + t) + 0.5 * x * (1.0 - t * t) * _GELU_K * (1.0 + 3.0 * 0.044715 * x * x)


def _prenorm(h, g, name):
    return _rowwise(lambda x, gg: _rms(x, gg), [h], [g], [(D_MODEL, BF16)], name=name)


def _postnorm_residual(h, f, g, scale, name):
    return _rowwise(lambda x, ff, gg: x + scale * _rms(ff, gg), [h, f], [g], [(D_MODEL, F32)], name=name)


def _postnorm_bwd(f, dh, g, scale, name):
    def fn(ff, d, gg):
        dx, dgt = _rms_bwd(ff, gg, scale * d)
        return dx, _colsum(dgt)
    return _rowwise(fn, [f, dh], [g], [(D_MODEL, BF16)], [(1, D_MODEL)], name=name)


def _prenorm_bwd(h, das, dh, g, name):
    n = len(das)

    def fn(x, *rest):
        da = rest[0]
        for extra in rest[1:n]:
            da = da + extra
        d, gg = rest[n], rest[n + 1]
        dx, dgt = _rms_bwd(x, gg, da)
        return d + dx, _colsum(dgt)
    return _rowwise(fn, [h, *das, dh], [g], [(D_MODEL, F32)], [(1, D_MODEL)], name=name)


FFN_TM = 1024
FFN_TN = 256


def _ffn_in_swiglu(a, w_in, name):
    t = a.shape[0]
    tm = _pick(t, (FFN_TM, 512, 256, 128))
    nj = D_FF // FFN_TN

    def body(a_ref, wg_ref, wu_ref, gate_ref, up_ref, s_ref):
        av = a_ref[...]
        gate = jnp.dot(av, wg_ref[...], preferred_element_type=F32)
        up = jnp.dot(av, wu_ref[...], preferred_element_type=F32)
        gate_ref[...] = gate
        up_ref[...] = up
        s_ref[...] = (_silu(gate) * up).astype(s_ref.dtype)

    tile = pl.BlockSpec((tm, FFN_TN), lambda i, j: (i, j))
    return pl.pallas_call(
        body, name=name, grid=(t // tm, nj),
        in_specs=[pl.BlockSpec((tm, D_MODEL), lambda i, j: (i, 0)), pl.BlockSpec((D_MODEL, FFN_TN), lambda i, j: (0, j)),
                  pl.BlockSpec((D_MODEL, FFN_TN), lambda i, j: (0, j + nj))],
        out_specs=[tile, tile, tile],
        out_shape=[jax.ShapeDtypeStruct((t, D_FF), F32), jax.ShapeDtypeStruct((t, D_FF), F32),
                   jax.ShapeDtypeStruct((t, D_FF), BF16)],
        compiler_params=_params("parallel", "parallel"),
    )(a, w_in, w_in)


def _ffn_down_dx_swiglu(df, w_down, gate, up, name):
    t = df.shape[0]
    tm = _pick(t, (FFN_TM, 512, 256, 128))

    def body(df_ref, wd_ref, gate_ref, up_ref, dgate_ref, dup_ref):
        ds = lax.dot_general(df_ref[...], wd_ref[...], (((1,), (1,)), ((), ())), preferred_element_type=F32)
        gate = gate_ref[...]
        sg = jax.nn.sigmoid(gate)
        dgate_ref[...] = (ds * up_ref[...] * (sg * (1.0 + gate * (1.0 - sg)))).astype(dgate_ref.dtype)
        dup_ref[...] = (ds * (gate * sg)).astype(dup_ref.dtype)

    tile = pl.BlockSpec((tm, FFN_TN), lambda i, j: (i, j))
    return pl.pallas_call(
        body, name=name, grid=(t // tm, D_FF // FFN_TN),
        in_specs=[pl.BlockSpec((tm, D_MODEL), lambda i, j: (i, 0)), pl.BlockSpec((FFN_TN, D_MODEL), lambda i, j: (j, 0)),
                  tile, tile],
        out_specs=[tile, tile],
        out_shape=[jax.ShapeDtypeStruct((t, D_FF), BF16), jax.ShapeDtypeStruct((t, D_FF), BF16)],
        compiler_params=_params("parallel", "parallel"),
    )(df, w_down, gate, up)


def _ffn_fwd(h, w, pre_g, post_g, tag):
    a = _prenorm(h, pre_g, tag + "_prenorm")
    gate, up, s = _ffn_in_swiglu(a, w["in"], tag + "_in_swiglu")
    f = _mm(s, w["down"], tm=1024, tn=1024, tk=D_FF, name=tag + "_down")
    out = _postnorm_residual(h, f, post_g, 0.5, tag + "_postnorm")
    return out, (h, a, gate, up, s, f)


def _ffn_bwd(dh, saved, w, pre_g, post_g, tag):
    h, a, gate, up, s, f = saved
    df, d_post = _postnorm_bwd(f, dh, post_g, 0.5, tag + "_postnorm_bwd")
    dgate, dup = _ffn_down_dx_swiglu(df, w["down"], gate, up, tag + "_down_dx_swiglu")
    d_down = _mm(s, df, ta=True, tm=1408, tn=1024, tk=1024, name=tag + "_down_dw")
    da_gate = _mm(dgate, w["in"], tb=True, tm=1024, tn=1024, tk=1408, name=tag + "_in_dx_gate")
    da_up = _mm(dup, w["in"], tb=True, tm=1024, tn=1024, tk=1408, b_k0=D_FF, name=tag + "_in_dx_up")
    d_in = jnp.concatenate([_mm(a, dgate, ta=True, tm=1024, tn=1408, tk=1024, name=tag + "_in_dw_gate"),
                            _mm(a, dup, ta=True, tm=1024, tn=1408, tk=1024, name=tag + "_in_dw_up")], axis=1)
    dh_in, d_pre = _prenorm_bwd(h, [da_gate, da_up], dh, pre_g, tag + "_prenorm_bwd")
    return dh_in, dict(w_in=d_in, w_down=d_down, pre_g=d_pre, post_g=d_post)


def _ple_fwd(h, p_i, w, pre_g, post_g):
    a = _prenorm(h, pre_g, "ple_prenorm")
    gl = _mm(a, w["gate"], tm=1024, tn=1024, tk=1024, name="ple_gate")
    e = _mm(p_i, w["proj"], tm=1024, tn=1024, tk=PLE_DIM, name="ple_proj")
    out = _rowwise(lambda x, g_, e_, gg: x + _rms(jax.nn.sigmoid(g_) * e_, gg), [h, gl, e], [post_g],
                   [(D_MODEL, F32)], name="ple_out")
    return out, (h, a, gl, e)


def _ple_bwd(dh, saved, p_i, w, pre_g, post_g):
    h, a, gl, e = saved

    def fn(g_, e_, d, gg):
        sg = jax.nn.sigmoid(g_)
        du, dgt = _rms_bwd(sg * e_, gg, d)
        return du * e_ * sg * (1.0 - sg), du * sg, _colsum(dgt)
    dgl, de, d_post = _rowwise(fn, [gl, e, dh], [post_g], [(D_MODEL, BF16), (D_MODEL, BF16)], [(1, D_MODEL)],
                               name="ple_out_bwd")
    da = _mm(dgl, w["gate"], tb=True, tm=1024, tn=1024, tk=1024, name="ple_gate_dx")
    d_gate = _mm(a, dgl, ta=True, tm=1024, tn=1024, tk=2048, name="ple_gate_dw")
    d_proj = _mm(p_i, de, ta=True, tm=PLE_DIM, tn=1024, tk=2048, name="ple_proj_dw")
    dh_in, d_pre = _prenorm_bwd(h, [da], dh, pre_g, "ple_prenorm_bwd")
    return dh_in, dict(w_gate=d_gate, w_proj=d_proj, pre_g=d_pre, post_g=d_post)


def _gm_layernorm(v, g, b):
    mu = jnp.mean(v, axis=-1, keepdims=True)
    xc = v - mu
    rstd = lax.rsqrt(jnp.mean(xc * xc, axis=-1, keepdims=True) + LN_EPS)
    vhat = xc * rstd
    return vhat, rstd, vhat * g + b


def _gmlp_fwd(proj, wm, bias_t, ln_g, ln_b, name):
    t = proj.shape[0]

    def body(uv_ref, wm_ref, bt_ref, g_ref, b_ref, o_ref):
        for hd in range(GM_HEADS):
            lo = hd * LANES
            u = _gelu(uv_ref[:, lo:lo + LANES])
            v = _gelu(uv_ref[:, 1024 + lo:1024 + lo + LANES])
            _, _, vln = _gm_layernorm(v, g_ref[:, lo:lo + LANES], b_ref[:, lo:lo + LANES])
            mixed = jnp.dot(wm_ref[hd], vln.astype(BF16), preferred_element_type=F32) + bt_ref[:, hd:hd + 1]
            o_ref[:, lo:lo + LANES] = (u * mixed).astype(o_ref.dtype)

    return pl.pallas_call(
        body, name=name, grid=(t // CHUNK,),
        in_specs=[pl.BlockSpec((CHUNK, 2048), lambda i: (i, 0)), pl.BlockSpec(wm.shape, lambda i: (0, 0, 0)),
                  pl.BlockSpec(bias_t.shape, lambda i: (0, 0)), pl.BlockSpec(ln_g.shape, lambda i: (0, 0)),
                  pl.BlockSpec(ln_b.shape, lambda i: (0, 0))],
        out_specs=pl.BlockSpec((CHUNK, 1024), lambda i: (i, 0)), out_shape=jax.ShapeDtypeStruct((t, 1024), BF16),
        compiler_params=_params("parallel"),
    )(proj, wm, bias_t, ln_g, ln_b)


def _gmlp_bwd(proj, dyab, wm, bias_t, ln_g, ln_b, name):
    t = proj.shape[0]
    nc = t // CHUNK

    def body(uv_ref, dy_ref, wm_ref, bt_ref, g_ref, b_ref, duv_ref, dw_ref, db_ref, dg_ref, dbeta_ref, dbacc):
        c = pl.program_id(0)

        @pl.when(c == 0)
        def _():
            dw_ref[...] = jnp.zeros_like(dw_ref)
            dbacc[...] = jnp.zeros_like(dbacc)
            dg_ref[...] = jnp.zeros_like(dg_ref)
            dbeta_ref[...] = jnp.zeros_like(dbeta_ref)

        for hd in range(GM_HEADS):
            lo = hd * LANES
            xu = uv_ref[:, lo:lo + LANES]
            xv = uv_ref[:, 1024 + lo:1024 + lo + LANES]
            u = _gelu(xu)
            g_h = g_ref[:, lo:lo + LANES]
            vhat, rstd, vln = _gm_layernorm(_gelu(xv), g_h, b_ref[:, lo:lo + LANES])
            vln16 = vln.astype(BF16)
            mixed = jnp.dot(wm_ref[hd], vln16, preferred_element_type=F32) + bt_ref[:, hd:hd + 1]
            dy = dy_ref[:, lo:lo + LANES]
            du = dy * mixed
            dmix = dy * u
            dmix16 = dmix.astype(BF16)
            dw_ref[hd] += lax.dot_general(dmix16, vln16, (((1,), (1,)), ((), ())), preferred_element_type=F32)
            dbacc[hd] += dmix
            dvln = lax.dot_general(wm_ref[hd], dmix16, (((0,), (0,)), ((), ())), preferred_element_type=F32)
            dg_ref[:, lo:lo + LANES] += _colsum(dvln * vhat)
            dbeta_ref[:, lo:lo + LANES] += _colsum(dvln)
            dvh = dvln * g_h
            dv = rstd * (dvh - jnp.mean(dvh, axis=-1, keepdims=True)
                         - vhat * jnp.mean(dvh * vhat, axis=-1, keepdims=True))
            duv_ref[:, lo:lo + LANES] = (du * _gelu_grad(xu)).astype(duv_ref.dtype)
            duv_ref[:, 1024 + lo:1024 + lo + LANES] = (dv * _gelu_grad(xv)).astype(duv_ref.dtype)

        @pl.when(c == nc - 1)
        def _():
            row = lax.broadcasted_iota(jnp.int32, (CHUNK, CHUNK), 0)
            col = lax.broadcasted_iota(jnp.int32, (CHUNK, CHUNK), 1)
            for hd in range(GM_HEADS):
                dw_ref[hd] = jnp.where(col <= row, dw_ref[hd], 0.0)
                db_ref[hd] = jnp.sum(dbacc[hd], axis=1, keepdims=True)

    return pl.pallas_call(
        body, name=name, grid=(nc,),
        in_specs=[pl.BlockSpec((CHUNK, 2048), lambda i: (i, 0)), pl.BlockSpec((CHUNK, 1024), lambda i: (i, 0)),
                  pl.BlockSpec(wm.shape, lambda i: (0, 0, 0)), pl.BlockSpec(bias_t.shape, lambda i: (0, 0)),
                  pl.BlockSpec(ln_g.shape, lambda i: (0, 0)), pl.BlockSpec(ln_b.shape, lambda i: (0, 0))],
        out_specs=[pl.BlockSpec((CHUNK, 2048), lambda i: (i, 0)), pl.BlockSpec((GM_HEADS, CHUNK, CHUNK), lambda i: (0, 0, 0)),
                   pl.BlockSpec((GM_HEADS, CHUNK, 1), lambda i: (0, 0, 0)), pl.BlockSpec((1, 1024), lambda i: (0, 0)),
                   pl.BlockSpec((1, 1024), lambda i: (0, 0))],
        out_shape=[jax.ShapeDtypeStruct((t, 2048), BF16), jax.ShapeDtypeStruct((GM_HEADS, CHUNK, CHUNK), F32),
                   jax.ShapeDtypeStruct((GM_HEADS, CHUNK, 1), F32), jax.ShapeDtypeStruct((1, 1024), F32),
                   jax.ShapeDtypeStruct((1, 1024), F32)],
        scratch_shapes=[pltpu.VMEM((GM_HEADS, CHUNK, CHUNK), F32)],
        compiler_params=_params("arbitrary"),
    )(proj, dyab, wm, bias_t, ln_g, ln_b)


def _ssd_chunk_terms(dt_pad, a_pad):
    row = lax.broadcasted_iota(jnp.int32, (CHUNK, CHUNK), 0)
    col = lax.broadcasted_iota(jnp.int32, (CHUNK, CHUNK), 1)
    tril = jnp.where(col <= row, 1.0, 0.0).astype(F32)
    a_cs = jnp.dot(tril, dt_pad * a_pad, precision=HIGHEST, preferred_element_type=F32)
    return a_cs, a_cs.T


def _pair_cols(mat, hd_a, lane_lt64):
    return jnp.where(lane_lt64, mat[:, hd_a:hd_a + 1], mat[:, hd_a + 1:hd_a + 2])


def _head_decay(a_cs, a_cs_t, hd, causal):
    seg = a_cs[:, hd:hd + 1] - a_cs_t[hd:hd + 1, :]
    return jnp.exp(jnp.where(causal, seg, -jnp.inf))


def _ssd_fwd(act, dt_pad, a_pad, d_pad, name):
    t = act.shape[0]
    nc = t // CHUNK

    def body(act_ref, dt_ref, a_ref, d_ref, y_ref, st_ref, h_sc):
        c = pl.program_id(0)

        @pl.when(c == 0)
        def _():
            h_sc[...] = jnp.zeros_like(h_sc)

        st_ref[...] = h_sc[...]
        row = lax.broadcasted_iota(jnp.int32, (CHUNK, CHUNK), 0)
        col = lax.broadcasted_iota(jnp.int32, (CHUNK, CHUNK), 1)
        causal = col <= row
        lane_lt64 = lax.broadcasted_iota(jnp.int32, (CHUNK, LANES), 1) < SSD_HEAD_DIM
        row_lt64 = lax.broadcasted_iota(jnp.int32, (LANES, 1), 0) < SSD_HEAD_DIM
        dt = dt_ref[...]
        a_cs, a_cs_t = _ssd_chunk_terms(dt, a_ref[...])
        last = a_cs[CHUNK - 1:CHUNK, :]
        for g in range(2):
            b16 = act_ref[:, SSD_INNER + g * SSD_STATE:SSD_INNER + (g + 1) * SSD_STATE].astype(BF16)
            c16 = act_ref[:, SSD_INNER + SSD_BC + g * SSD_STATE:SSD_INNER + SSD_BC + (g + 1) * SSD_STATE].astype(BF16)
            cb = lax.dot_general(c16, b16, (((1,), (1,)), ((), ())), preferred_element_type=F32)
            for pr in range(4):
                ha = g * 8 + pr * 2
                lo = ha * SSD_HEAD_DIM
                xs = act_ref[:, lo:lo + LANES]
                xd = xs * _pair_cols(dt, ha, lane_lt64)
                xd16 = xd.astype(BF16)
                ya = jnp.dot((cb * _head_decay(a_cs, a_cs_t, ha, causal)).astype(BF16), xd16, preferred_element_type=F32)
                yb = jnp.dot((cb * _head_decay(a_cs, a_cs_t, ha + 1, causal)).astype(BF16), xd16, preferred_element_type=F32)
                a_p = _pair_cols(a_cs, ha, lane_lt64)
                hp = h_sc[lo:lo + LANES, :]
                y_off = lax.dot_general(c16, hp.astype(BF16), (((1,), (1,)), ((), ())), preferred_element_type=F32)
                d_p = jnp.where(lane_lt64[:1], d_ref[:, ha:ha + 1], d_ref[:, ha + 1:ha + 2])
                y_ref[:, lo:lo + LANES] = jnp.where(lane_lt64, ya, yb) + y_off * jnp.exp(a_p) + d_p * xs
                last_p = jnp.where(lane_lt64[:1], last[:, ha:ha + 1], last[:, ha + 1:ha + 2])
                xw16 = (xd * jnp.exp(last_p - a_p)).astype(BF16)
                s_new = lax.dot_general(xw16, b16, (((0,), (0,)), ((), ())), preferred_element_type=F32)
                t_col = jnp.where(row_lt64, jnp.exp(last[:, ha:ha + 1]), jnp.exp(last[:, ha + 1:ha + 2]))
                h_sc[lo:lo + LANES, :] = t_col * hp + s_new

    return pl.pallas_call(
        body, name=name, grid=(nc,),
        in_specs=[pl.BlockSpec((CHUNK, SSD_CONV_CH), lambda i: (i, 0)), pl.BlockSpec((CHUNK, LANES), lambda i: (i, 0)),
                  pl.BlockSpec((1, LANES), lambda i: (0, 0)), pl.BlockSpec((1, LANES), lambda i: (0, 0))],
        out_specs=[pl.BlockSpec((CHUNK, SSD_INNER), lambda i: (i, 0)),
                   pl.BlockSpec((None, SSD_INNER, SSD_STATE), lambda i: (i, 0, 0))],
        out_shape=[jax.ShapeDtypeStruct((t, SSD_INNER), F32), jax.ShapeDtypeStruct((nc, SSD_INNER, SSD_STATE), F32)],
        scratch_shapes=[pltpu.VMEM((SSD_INNER, SSD_STATE), F32)],
        compiler_params=_params("arbitrary"),
    )(act, dt_pad, a_pad, d_pad)


def _ssd_bwd(act, dt_pad, a_pad, d_pad, states, dy, name):
    t = act.shape[0]
    nc = t // CHUNK

    def body(act_ref, dt_ref, a_ref, d_ref, st_ref, dy_ref, dact_ref, ddt_ref, da_ref, dd_ref, dh_sc):
        c = pl.program_id(0)

        @pl.when(c == 0)
        def _():
            dh_sc[...] = jnp.zeros_like(dh_sc)
            da_ref[...] = jnp.zeros_like(da_ref)
            dd_ref[...] = jnp.zeros_like(dd_ref)

        row = lax.broadcasted_iota(jnp.int32, (CHUNK, CHUNK), 0)
        col = lax.broadcasted_iota(jnp.int32, (CHUNK, CHUNK), 1)
        causal = col <= row
        lane = lax.broadcasted_iota(jnp.int32, (CHUNK, LANES), 1)
        lane_lt64 = lane < SSD_HEAD_DIM
        row_lt64 = lax.broadcasted_iota(jnp.int32, (LANES, 1), 0) < SSD_HEAD_DIM
        is_last = lax.broadcasted_iota(jnp.int32, (CHUNK, 1), 0) == CHUNK - 1
        dt = dt_ref[...]
        a_cs, a_cs_t = _ssd_chunk_terms(dt, a_ref[...])
        last = a_cs[CHUNK - 1:CHUNK, :]
        d_acs = jnp.zeros((CHUNK, LANES), F32)
        ddt_x = jnp.zeros((CHUNK, LANES), F32)
        dd_acc = jnp.zeros((1, LANES), F32)

        def head_sum(v, first):
            return jnp.sum(jnp.where(lane_lt64 if first else jnp.logical_not(lane_lt64), v, 0.0), axis=1, keepdims=True)

        for g in range(2):
            b_lo = SSD_INNER + g * SSD_STATE
            c_lo = SSD_INNER + SSD_BC + g * SSD_STATE
            b16 = act_ref[:, b_lo:b_lo + SSD_STATE].astype(BF16)
            c16 = act_ref[:, c_lo:c_lo + SSD_STATE].astype(BF16)
            cb = lax.dot_general(c16, b16, (((1,), (1,)), ((), ())), preferred_element_type=F32)
            dcb = jnp.zeros((CHUNK, CHUNK), F32)
            db_g = jnp.zeros((CHUNK, SSD_STATE), F32)
            dc_g = jnp.zeros((CHUNK, SSD_STATE), F32)
            for pr in range(4):
                ha = g * 8 + pr * 2
                lo = ha * SSD_HEAD_DIM
                xs = act_ref[:, lo:lo + LANES]
                dt_p = _pair_cols(dt, ha, lane_lt64)
                xd = xs * dt_p
                xd16 = xd.astype(BF16)
                a_p = _pair_cols(a_cs, ha, lane_lt64)
                exp_a = jnp.exp(a_p)
                last_p = jnp.where(lane_lt64[:1], last[:, ha:ha + 1], last[:, ha + 1:ha + 2])
                w_p = jnp.exp(last_p - a_p)
                hp = st_ref[lo:lo + LANES, :]
                hp16 = hp.astype(BF16)
                dhn = dh_sc[lo:lo + LANES, :]
                dhn16 = dhn.astype(BF16)
                dyp = dy_ref[:, lo:lo + LANES]
                d_p = jnp.where(lane_lt64[:1], d_ref[:, ha:ha + 1], d_ref[:, ha + 1:ha + 2])
                dd_acc = dd_acc + jnp.where(lane[:1] == ha, jnp.sum(head_sum(dyp * xs, True), axis=0, keepdims=True), 0.0) \
                    + jnp.where(lane[:1] == ha + 1, jnp.sum(head_sum(dyp * xs, False), axis=0, keepdims=True), 0.0)
                g_off = lax.dot_general(c16, hp16, (((1,), (1,)), ((), ())), preferred_element_type=F32)
                dg16 = (dyp * exp_a).astype(BF16)
                dc_g = dc_g + jnp.dot(dg16, hp16, preferred_element_type=F32)
                dh_prev = lax.dot_general(dg16, c16, (((0,), (0,)), ((), ())), preferred_element_type=F32)
                off_term = dyp * g_off * exp_a
                q = lax.dot_general(b16, dhn16, (((1,), (1,)), ((), ())), preferred_element_type=F32)
                xw16 = (xd * w_p).astype(BF16)
                db_g = db_g + jnp.dot(xw16, dhn16, preferred_element_type=F32)
                dw_term = xd * q * w_p
                dxd = w_p * q
                dt_all = dhn * hp
                dyp16 = dyp.astype(BF16)
                for k, first in ((0, True), (1, False)):
                    hd = ha + k
                    sel = lane_lt64 if first else jnp.logical_not(lane_lt64)
                    decay = _head_decay(a_cs, a_cs_t, hd, causal)
                    m = cb * decay
                    dy_h = jnp.where(sel, dyp16, jnp.zeros_like(dyp16))
                    dm = lax.dot_general(dy_h, xd16, (((1,), (1,)), ((), ())), preferred_element_type=F32)
                    dcb = dcb + dm * decay
                    dseg = dm * m
                    dxd = dxd + jnp.where(sel, lax.dot_general(m.astype(BF16), dyp16, (((0,), (0,)), ((), ())),
                                                               preferred_element_type=F32), 0.0)
                    d_col = jnp.sum(dseg, axis=1, keepdims=True) - jnp.sum(dseg.T, axis=1, keepdims=True)
                    dw_col = head_sum(dw_term, first)
                    d_col = d_col + head_sum(off_term, first) - dw_col
                    t_h = jnp.exp(last[:, hd:hd + 1])
                    dt_sum = jnp.sum(jnp.sum(jnp.where(row_lt64 if first else jnp.logical_not(row_lt64), dt_all, 0.0),
                                             axis=1, keepdims=True), axis=0, keepdims=True)
                    end_term = jnp.sum(dw_col, axis=0, keepdims=True) + dt_sum * t_h
                    d_col = d_col + jnp.where(is_last, end_term, 0.0)
                    d_acs = d_acs + jnp.where(lane == hd, d_col, 0.0)
                t_col = jnp.where(row_lt64, jnp.exp(last[:, ha:ha + 1]), jnp.exp(last[:, ha + 1:ha + 2]))
                dh_sc[lo:lo + LANES, :] = t_col * dhn + dh_prev
                dact_ref[:, lo:lo + LANES] = d_p * dyp + dxd * dt_p
                ddt_all = dxd * xs
                ddt_x = ddt_x + jnp.where(lane == ha, head_sum(ddt_all, True), 0.0) \
                    + jnp.where(lane == ha + 1, head_sum(ddt_all, False), 0.0)
            dcb16 = dcb.astype(BF16)
            dact_ref[:, b_lo:b_lo + SSD_STATE] = db_g + lax.dot_general(dcb16, c16, (((0,), (0,)), ((), ())),
                                                                          preferred_element_type=F32)
            dact_ref[:, c_lo:c_lo + SSD_STATE] = dc_g + jnp.dot(dcb16, b16, preferred_element_type=F32)
        triu = jnp.where(col >= row, 1.0, 0.0).astype(F32)
        dda = jnp.dot(triu, d_acs, precision=HIGHEST, preferred_element_type=F32)
        ddt_ref[...] = dda * a_ref[...] + ddt_x
        da_ref[...] += _colsum(dda * dt)
        dd_ref[...] += dd_acc

    rev = lambda i: (nc - 1 - i, 0)
    return pl.pallas_call(
        body, name=name, grid=(nc,),
        in_specs=[pl.BlockSpec((CHUNK, SSD_CONV_CH), rev), pl.BlockSpec((CHUNK, LANES), rev),
                  pl.BlockSpec((1, LANES), lambda i: (0, 0)), pl.BlockSpec((1, LANES), lambda i: (0, 0)),
                  pl.BlockSpec((None, SSD_INNER, SSD_STATE), lambda i: (nc - 1 - i, 0, 0)),
                  pl.BlockSpec((CHUNK, SSD_INNER), rev)],
        out_specs=[pl.BlockSpec((CHUNK, SSD_CONV_CH), rev), pl.BlockSpec((CHUNK, LANES), rev),
                   pl.BlockSpec((1, LANES), lambda i: (0, 0)), pl.BlockSpec((1, LANES), lambda i: (0, 0))],
        out_shape=[jax.ShapeDtypeStruct((t, SSD_CONV_CH), F32), jax.ShapeDtypeStruct((t, LANES), F32),
                   jax.ShapeDtypeStruct((1, LANES), F32), jax.ShapeDtypeStruct((1, LANES), F32)],
        scratch_shapes=[pltpu.VMEM((SSD_INNER, SSD_STATE), F32)],
        compiler_params=_params("arbitrary"),
    )(act, dt_pad, a_pad, d_pad, states, dy)


def _shift_down(x, k):
    return x if k == 0 else jnp.pad(x, ((k, 0), (0, 0)))[:x.shape[0]]


def _shift_up(x, k):
    return x if k == 0 else jnp.pad(x, ((0, k), (0, 0)))[k:]


def _conv_pre(x0, x1, x2, x3, w, b):
    return x0 * w[0:1] + x1 * w[1:2] + x2 * w[2:3] + x3 * w[3:4] + b


def _rope128(x, cpad, s_lo, s_hi):
    return x * cpad + pltpu.roll(x, 96, 1) * s_lo + pltpu.roll(x, 32, 1) * s_hi


ATTN_ROW_SPLIT = 4
ATTN_ROW_SPLIT_DKV = 4


def _diag_mask(rows, cols, row0):
    return lax.broadcasted_iota(jnp.int32, (rows, cols), 1) <= row0 + lax.broadcasted_iota(jnp.int32, (rows, cols), 0)


def _attn_scores(q, k):
    return lax.dot_general(q, k, (((1,), (1,)), ((), ())), preferred_element_type=F32)


def _causal_pairs(nq, by_key):
    if by_key:
        pairs = [(i, j) for j in range(nq) for i in range(j, nq)]
    else:
        pairs = [(i, j) for i in range(nq) for j in range(i + 1)]
    return (jnp.asarray([pr[0] for pr in pairs], jnp.int32), jnp.asarray([pr[1] for pr in pairs], jnp.int32))


def _attn_fwd(qf, kf, kvf, *, tq, name):
    t = qf.shape[0]
    nq = t // tq
    tk = tq
    qi, kj = _causal_pairs(nq, by_key=False)
    rs = tq // ATTN_ROW_SPLIT

    def body(qi_ref, kj_ref, q_ref, k_ref, v_ref, o_ref, lse_ref, m_sc, l_sc, acc_sc):
        pp = pl.program_id(1)
        i, j = qi_ref[pp], kj_ref[pp]

        @pl.when(j == 0)
        def _():
            m_sc[...] = jnp.full_like(m_sc, -jnp.inf)
            l_sc[...] = jnp.zeros_like(l_sc)
            acc_sc[...] = jnp.zeros_like(acc_sc)

        def update(diag):
            for r in range(ATTN_ROW_SPLIT):
                rows = slice(r * rs, (r + 1) * rs)
                keys = slice(0, (r + 1) * rs if diag else tk)
                s = _attn_scores(q_ref[rows, :], k_ref[keys, :])
                if diag:
                    s = jnp.where(_diag_mask(rs, keys.stop, r * rs), s, -jnp.inf)
                m_prev = m_sc[rows, :]
                m_new = jnp.maximum(m_prev, jnp.max(s, axis=1, keepdims=True))
                p = jnp.exp2(s - m_new)
                alpha = jnp.exp2(m_prev - m_new)
                l_new = alpha * l_sc[rows, :] + jnp.sum(p, axis=1, keepdims=True)
                acc = alpha * acc_sc[rows, :] + jnp.dot(p.astype(BF16), v_ref[keys, :], preferred_element_type=F32)
                if diag:
                    o_ref[rows, :] = (acc / l_new).astype(o_ref.dtype)
                    lse_ref[rows, :] = m_new + jnp.log2(l_new)
                else:
                    l_sc[rows, :] = l_new
                    acc_sc[rows, :] = acc
                    m_sc[rows, :] = m_new

        @pl.when(j < i)
        def _():
            update(False)

        @pl.when(j == i)
        def _():
            update(True)

    return pl.pallas_call(
        body, name=name,
        grid_spec=pltpu.PrefetchScalarGridSpec(
            num_scalar_prefetch=2, grid=(MLA_HEADS, int(qi.shape[0])),
            in_specs=[pl.BlockSpec((tq, 2 * LANES), lambda h, pp, qi_, kj_: (qi_[pp], h)),
                      pl.BlockSpec((tk, 2 * LANES), lambda h, pp, qi_, kj_: (kj_[pp], h)),
                      pl.BlockSpec((tk, LANES), lambda h, pp, qi_, kj_: (kj_[pp], 2 * h + 1))],
            out_specs=[pl.BlockSpec((tq, LANES), lambda h, pp, qi_, kj_: (qi_[pp], h)),
                       pl.BlockSpec((None, tq, 1), lambda h, pp, qi_, kj_: (h, qi_[pp], 0))],
            scratch_shapes=[pltpu.VMEM((tq, 1), F32), pltpu.VMEM((tq, 1), F32), pltpu.VMEM((tq, LANES), F32)]),
        out_shape=[jax.ShapeDtypeStruct((t, MLA_HEADS * LANES), BF16), jax.ShapeDtypeStruct((MLA_HEADS, t, 1), F32)],
        compiler_params=_params("parallel", "arbitrary"),
    )(qi, kj, qf, kf, kvf)


def _attn_bwd(qf, kf, kvf, o, do, lse, *, tq, name):
    t = qf.shape[0]
    nq = t // tq
    tk = tq
    qi, kj = _causal_pairs(nq, by_key=True)
    rs = tq // ATTN_ROW_SPLIT_DKV

    def body(qi_ref, kj_ref, q_ref, k_ref, v_ref, o_ref, do_ref, lse_ref, dkv_ref, dkr_ref, dq_ref, dk_sc, dv_sc):
        pp = pl.program_id(1)
        i, j = qi_ref[pp], kj_ref[pp]
        tn = (((0,), (0,)), ((), ()))

        @pl.when(pp == 0)
        def _():
            dq_ref[...] = jnp.zeros_like(dq_ref)

        def update(diag):
            if diag:
                dv_sc[...] = jnp.zeros_like(dv_sc)
                dk_sc[...] = jnp.zeros_like(dk_sc)
            for r in range(ATTN_ROW_SPLIT_DKV):
                rows = slice(r * rs, (r + 1) * rs)
                keys = slice(0, (r + 1) * rs if diag else tk)
                do_ = do_ref[rows, :]
                delta = jnp.sum(do_.astype(F32) * o_ref[rows, :].astype(F32), axis=1, keepdims=True)
                s = _attn_scores(q_ref[rows, :], k_ref[keys, :])
                p = jnp.exp2(s - lse_ref[rows, :])
                if diag:
                    p = jnp.where(_diag_mask(rs, keys.stop, r * rs), p, 0.0)
                dp = lax.dot_general(do_, v_ref[keys, :], (((1,), (1,)), ((), ())), preferred_element_type=F32)
                ds = (p * (dp - delta)).astype(BF16)
                dv_sc[keys, :] += lax.dot_general(p.astype(BF16), do_, tn, preferred_element_type=F32)
                dk_sc[keys, :] += lax.dot_general(ds, q_ref[rows, :], tn, preferred_element_type=F32)
                q_rows = pl.ds(pl.multiple_of(i * tq + r * rs, rs), rs)
                dq_ref[q_rows, :] += jnp.dot(ds, k_ref[keys, :], preferred_element_type=F32)

        @pl.when(i > j)
        def _():
            update(False)

        @pl.when(i == j)
        def _():
            update(True)

        @pl.when(i == nq - 1)
        def _():
            dkv_ref[:, :LANES] = (dk_sc[:, :LANES] * LN_2).astype(dkv_ref.dtype)
            dkv_ref[:, LANES:] = dv_sc[...].astype(dkv_ref.dtype)
            dkr_ref[...] = dk_sc[:, LANES:] * LN_2

    qblk = lambda c: (lambda h, pp, qi_, kj_: (qi_[pp], c(h)))
    kblk = lambda c: (lambda h, pp, qi_, kj_: (kj_[pp], c(h)))
    return pl.pallas_call(
        body, name=name,
        grid_spec=pltpu.PrefetchScalarGridSpec(
            num_scalar_prefetch=2, grid=(MLA_HEADS, int(qi.shape[0])),
            in_specs=[pl.BlockSpec((tq, 2 * LANES), qblk(lambda h: h)), pl.BlockSpec((tk, 2 * LANES), kblk(lambda h: h)),
                      pl.BlockSpec((tk, LANES), kblk(lambda h: 2 * h + 1)),
                      pl.BlockSpec((tq, LANES), qblk(lambda h: h)), pl.BlockSpec((tq, LANES), qblk(lambda h: h)),
                      pl.BlockSpec((None, tq, 1), lambda h, pp, qi_, kj_: (h, qi_[pp], 0))],
            out_specs=[pl.BlockSpec((tk, 2 * LANES), kblk(lambda h: h)), pl.BlockSpec((tk, LANES), kblk(lambda h: h)),
                       pl.BlockSpec((t, 2 * LANES), lambda h, pp, qi_, kj_: (0, h))],
            scratch_shapes=[pltpu.VMEM((tk, 2 * LANES), F32), pltpu.VMEM((tk, LANES), F32)]),
        out_shape=[jax.ShapeDtypeStruct((t, MLA_HEADS * 2 * LANES), BF16), jax.ShapeDtypeStruct((t, MLA_HEADS * LANES), F32),
                   jax.ShapeDtypeStruct((t, MLA_HEADS * 2 * LANES), F32)],
        compiler_params=_params("parallel", "arbitrary"),
    )(qi, kj, qf, kf, kvf, o, do, lse)


def _rope_tables(positions):
    t = positions.shape[0]
    inv = 1.0 / (ROPE_BASE ** (jnp.arange(0, MLA_ROPE, 2, dtype=F32) / MLA_ROPE))
    ang = positions.astype(F32)[:, None] * inv
    cos, sin = jnp.cos(ang), jnp.sin(ang)
    z32, z64 = jnp.zeros((t, 32), F32), jnp.zeros((t, 64), F32)
    cpad = jnp.concatenate([cos, cos, z64], axis=1)
    s_lo = jnp.concatenate([-sin, z32, z64], axis=1)
    s_hi = jnp.concatenate([z32, sin, z64], axis=1)
    return cpad, s_lo, s_hi


def _mla_fwd(h, w, pre_g, post_g, rope, tq):
    cpad, s_lo, s_hi = rope
    hn = _prenorm(h, pre_g, "mla_prenorm")
    cin = _mm(hn, w["in"], tm=1024, tn=512, tk=1024, name="mla_in")

    def lat(c, cp, sl, sh, qg, kvg):
        cq, ckv, kr = c[:, :MLA_Q_LORA], c[:, MLA_Q_LORA:MLA_Q_LORA + MLA_KV_LORA], c[:, MLA_Q_LORA + MLA_KV_LORA:]
        return _rms(cq, qg), _rms(ckv, kvg), _rope128(kr, cp, sl, sh)
    cqn, ckvn, kr = _rowwise(lat, [cin, cpad, s_lo, s_hi], [w["q_norm_g"], w["kv_norm_g"]],
                             [(MLA_Q_LORA, BF16), (MLA_KV_LORA, BF16), (LANES, BF16)], name="mla_latent")
    q_raw = _mm(cqn, w["uq"], tm=1024, tn=1024, tk=MLA_Q_LORA, name="mla_uq")

    def rope_q(q, cp, sl, sh):
        pieces = []
        for hd in range(MLA_HEADS):
            pieces.append(q[:, 256 * hd:256 * hd + LANES] * ATTN_QSCALE)
            pieces.append(_rope128(q[:, 256 * hd + LANES:256 * hd + 256], cp, sl, sh) * ATTN_QSCALE)
        return (tuple(pieces),)
    qf = _rowwise(rope_q, [q_raw, cpad, s_lo, s_hi], [], [(4096, BF16)], name="mla_rope_q")
    kvf = _mm(ckvn, w["ukv"], out_dtype=BF16, tm=1024, tn=1024, tk=MLA_KV_LORA, name="mla_ukv")
    t = h.shape[0]
    k_nope = kvf.reshape(t, MLA_HEADS, 2 * LANES)[:, :, :LANES]
    kf = jnp.concatenate([k_nope, jnp.broadcast_to(kr[:, None, :], k_nope.shape)], axis=2).reshape(t, MLA_HEADS * 2 * LANES)
    o, lse = _attn_fwd(qf, kf, kvf, tq=tq, name="mla_attn")
    mixed = _mm(o, w["out"], tm=1024, tn=1024, tk=2048, name="mla_out")
    out = _postnorm_residual(h, mixed, post_g, 1.0, "mla_postnorm")
    return out, (h, hn, cin, cqn, ckvn, qf, kf, kvf, o, lse, mixed)


def _mla_bwd(dh, saved, w, pre_g, post_g, rope, tq):
    cpad, s_lo, s_hi = rope
    h, hn, cin, cqn, ckvn, qf, kf, kvf, o, lse, mixed = saved
    dmixed, d_post = _postnorm_bwd(mixed, dh, post_g, 1.0, "mla_postnorm_bwd")
    do = _mm(dmixed, w["out"], tb=True, out_dtype=BF16, tm=1024, tn=1024, tk=1024, name="mla_out_dx")
    d_out = _mm(o, dmixed, ta=True, tm=1024, tn=1024, tk=2048, name="mla_out_dw")
    dkvf, dkr_heads, dq = _attn_bwd(qf, kf, kvf, o, do, lse, tq=tq, name="mla_attn_bwd")

    def unrope_q(d, cp, sl, sh):
        pieces = []
        for hd in range(MLA_HEADS):
            pieces.append(d[:, 256 * hd:256 * hd + LANES] * ATTN_SCALE)
            pieces.append(_rope128(d[:, 256 * hd + LANES:256 * hd + 256], cp, -sl, -sh) * ATTN_SCALE)
        return (tuple(pieces),)
    dq_raw = _rowwise(unrope_q, [dq, cpad, s_lo, s_hi], [], [(4096, BF16)], name="mla_rope_q_bwd")
    dcqn = _mm(dq_raw, w["uq"], tb=True, tm=1024, tn=256, tk=1024, name="mla_uq_dx")
    d_uq = _mm(cqn, dq_raw, ta=True, tm=256, tn=1024, tk=2048, name="mla_uq_dw")
    dckvn = _mm(dkvf, w["ukv"], tb=True, tm=1024, tn=128, tk=1024, name="mla_ukv_dx")
    d_ukv = _mm(ckvn, dkvf, ta=True, tm=128, tn=1024, tk=2048, name="mla_ukv_dw")

    def lat_bwd(c, dq_, dkv_, dkrh, cp, sl, sh, qg, kvg):
        cq, ckv = c[:, :MLA_Q_LORA], c[:, MLA_Q_LORA:MLA_Q_LORA + MLA_KV_LORA]
        dcq, dqg = _rms_bwd(cq, qg, dq_)
        dckv, dkvg = _rms_bwd(ckv, kvg, dkv_)
        dkr = dkrh[:, :LANES]
        for hd in range(1, MLA_HEADS):
            dkr = dkr + dkrh[:, hd * LANES:(hd + 1) * LANES]
        return (dcq, dckv, _rope128(dkr, cp, -sl, -sh)), _colsum(dqg), _colsum(dkvg)
    dcin, d_qg, d_kvg = _rowwise(lat_bwd, [cin, dcqn, dckvn, dkr_heads, cpad, s_lo, s_hi], [w["q_norm_g"], w["kv_norm_g"]],
                                 [(MLA_IN_PAD, BF16)], [(1, MLA_Q_LORA), (1, MLA_KV_LORA)], name="mla_latent_bwd")
    dhn = _mm(dcin, w["in"], tb=True, tm=1024, tn=1024, tk=512, name="mla_in_dx")
    d_in = _mm(hn, dcin, ta=True, tm=1024, tn=512, tk=2048, name="mla_in_dw")
    dh_in, d_pre = _prenorm_bwd(h, [dhn], dh, pre_g, "mla_prenorm_bwd")
    return dh_in, dict(w_in=d_in, q_norm_g=d_qg, kv_norm_g=d_kvg, w_uq=d_uq, w_ukv=d_ukv, w_out=d_out,
                       pre_g=d_pre, post_g=d_post)


def _hyb_fwd(h, w, pre_g, post_g):
    hn = _prenorm(h, pre_g, "hyb_prenorm")
    proj = _mm(hn, w["main"], tm=1024, tn=512, tk=1024, name="hyb_in")
    dtr = _mm(hn, w["dt"], tm=1024, tn=LANES, tk=1024, name="hyb_in_dt")
    ya = _gmlp_fwd(proj, w["gm_w"], w["gm_bt"], w["gm_ln_g"], w["gm_ln_b"], "gmlp")
    xbc = proj[:, 3072:]
    xsh = [_shift_down(xbc, 3 - k) for k in range(4)]
    act = _rowwise(lambda x0, x1, x2, x3, cw, cb: _silu(_conv_pre(x0, x1, x2, x3, cw, cb)), xsh,
                   [w["conv_w"], w["conv_b"]], [(SSD_CONV_CH, F32)], name="ssd_conv")
    dt_pad = _rowwise(lambda d, b: jax.nn.softplus(d + b), [dtr], [w["dt_bias"]], [(LANES, F32)], name="ssd_dt")
    y, states = _ssd_fwd(act, dt_pad, w["a"], w["d"], "ssd_scan")

    def gate_norm(y_, p_, ng):
        yg = y_ * _silu(p_[:, 2048:3072])
        return ((_rms(yg[:, :512], ng[:, :512]), _rms(yg[:, 512:], ng[:, 512:])),)
    yb = _rowwise(gate_norm, [y, proj], [w["norm_g"]], [(SSD_INNER, BF16)], name="ssd_gate_norm")
    yab = jnp.concatenate([ya, yb], axis=1)
    mixed = _mm(yab, w["out"], tm=1024, tn=1024, tk=2048, name="hyb_out")
    out = _postnorm_residual(h, mixed, post_g, 1.0, "hyb_postnorm")
    return out, (h, hn, proj, dtr, xsh, act, dt_pad, y, states, yab, mixed)


def _hyb_bwd(dh, saved, w, pre_g, post_g):
    h, hn, proj, dtr, xsh, act, dt_pad, y, states, yab, mixed = saved
    dmixed, d_post = _postnorm_bwd(mixed, dh, post_g, 1.0, "hyb_postnorm_bwd")
    dyab = _mm(dmixed, w["out"], tb=True, tm=1024, tn=1024, tk=1024, name="hyb_out_dx")
    d_out = _mm(yab, dmixed, ta=True, tm=1024, tn=1024, tk=2048, name="hyb_out_dw")

    def gate_norm_bwd(y_, p_, d, ng):
        z = p_[:, 2048:3072]
        sz = _silu(z)
        yg = y_ * sz
        d_lo, g_lo = _rms_bwd(yg[:, :512], ng[:, :512], d[:, 1024:1536])
        d_hi, g_hi = _rms_bwd(yg[:, 512:], ng[:, 512:], d[:, 1536:])
        dyg = jnp.concatenate([d_lo, d_hi], axis=1)
        return dyg * sz, dyg * y_ * _silu_grad(z), _colsum(jnp.concatenate([g_lo, g_hi], axis=1))
    dy, dz, d_norm = _rowwise(gate_norm_bwd, [y, proj, dyab], [w["norm_g"]], [(SSD_INNER, F32), (SSD_INNER, BF16)],
                              [(1, SSD_INNER)], name="ssd_gate_norm_bwd")
    dact, ddt, da_sum, dd_sum = _ssd_bwd(act, dt_pad, w["a"], w["d"], states, dy, "ssd_scan_bwd")

    def conv_bwd(x0, x1, x2, x3, da_, cw, cb):
        dpre = da_ * _silu_grad(_conv_pre(x0, x1, x2, x3, cw, cb))
        dw = jnp.concatenate([_colsum(dpre * x0), _colsum(dpre * x1), _colsum(dpre * x2), _colsum(dpre * x3)], axis=0)
        return dpre, dw, _colsum(dpre)
    dconv, d_conv_w, d_conv_b = _rowwise(conv_bwd, [*xsh, dact], [w["conv_w"], w["conv_b"]], [(SSD_CONV_CH, F32)],
                                         [(4, SSD_CONV_CH), (1, SSD_CONV_CH)], name="ssd_conv_bwd")
    dsh = [_shift_up(dconv, 3 - k) for k in range(4)]
    dxbc = _rowwise(lambda d0, d1, d2, d3, cw: d0 * cw[0:1] + d1 * cw[1:2] + d2 * cw[2:3] + d3 * cw[3:4], dsh,
                    [w["conv_w"]], [(SSD_CONV_CH, BF16)], name="ssd_conv_dx")

    def dt_bwd(dd, d, b):
        g = dd * jax.nn.sigmoid(d + b)
        g = jnp.where(lax.broadcasted_iota(jnp.int32, g.shape, 1) < SSD_HEADS, g, 0.0)
        return g, _colsum(g)
    ddtr, d_dt_bias = _rowwise(dt_bwd, [ddt, dtr], [w["dt_bias"]], [(LANES, BF16)], [(1, LANES)], name="ssd_dt_bwd")
    duv, d_gm_w, d_gm_b, d_ln_g, d_ln_b = _gmlp_bwd(proj, dyab, w["gm_w"], w["gm_bt"], w["gm_ln_g"], w["gm_ln_b"],
                                                    "gmlp_bwd")
    dproj = jnp.concatenate([duv, dz, dxbc], axis=1)
    dhn_a = _mm(dproj, w["main"], tb=True, tm=1024, tn=1024, tk=1536, name="hyb_in_dx")
    dhn_b = _mm(ddtr, w["dt"], tb=True, tm=1024, tn=1024, tk=LANES, name="hyb_in_dt_dx")
    d_main = _mm(hn, dproj, ta=True, tm=1024, tn=512, tk=2048, name="hyb_in_dw")
    d_dt = _mm(hn, ddtr, ta=True, tm=1024, tn=LANES, tk=2048, name="hyb_in_dt_dw")
    dh_in, d_pre = _prenorm_bwd(h, [dhn_a, dhn_b], dh, pre_g, "hyb_prenorm_bwd")
    grads = dict(w_in=jnp.concatenate([d_main, d_dt[:, :SSD_HEADS]], axis=1), gm_ln_g=d_ln_g, gm_ln_b=d_ln_b,
                 gm_w_s=d_gm_w, gm_b_s=d_gm_b[:, :, 0], conv_w=d_conv_w, conv_b=d_conv_b,
                 dt_bias=d_dt_bias[:, :SSD_HEADS], a_log=(da_sum * w["a"])[:, :SSD_HEADS], d=dd_sum[:, :SSD_HEADS],
                 norm_g=d_norm, w_out=d_out, pre_g=d_pre, post_g=d_post)
    return dh_in, grads


def _row(v):
    return v.reshape(1, -1).astype(F32)


def _pad_lanes(v, n=LANES):
    v = _row(v)
    return jnp.pad(v, ((0, 0), (0, n - v.shape[1])))


HYB_IN = 4624
HYB_SHARD = HYB_IN // N_DEV
HYB_SHARD_PAD = 640


def _hyb_unblock_matrix():
    n = N_DEV * HYB_SHARD_PAD
    r = lax.broadcasted_iota(jnp.int32, (n, n), 0)
    c = lax.broadcasted_iota(jnp.int32, (n, n), 1)
    j = r % HYB_SHARD_PAD
    return jnp.logical_and(j < HYB_SHARD, c == HYB_SHARD * (r // HYB_SHARD_PAD) + j).astype(BF16)


def _layer_weights(fw, sm, i):
    j = i // 2
    lw = dict(
        ffn1=dict({"in": fw["ffn1_w_in"][i], "down": fw["ffn1_w_down"][i]}),
        ffn2=dict({"in": fw["ffn2_w_in"][i], "down": fw["ffn2_w_down"][i]}),
        ple=dict(gate=fw["ple_w_gate"][i], proj=fw["ple_w_proj"][i]),
    )
    if i % 2 == 0:
        w_in = _mm(fw["hyb_w_in"][j], _hyb_unblock_matrix(), out_dtype=BF16, tm=1024, tn=512, tk=1024, name="hyb_w_unblock")
        causal = jnp.tril(jnp.ones((CHUNK, CHUNK), dtype=bool))
        lw["mix"] = {
            "main": w_in[:, :HYB_MAIN], "dt": w_in[:, HYB_MAIN:HYB_MAIN + LANES],
            "gm_w": jnp.where(causal[None], sm["gm_w_s"][j], 0.0).astype(BF16),
            "gm_bt": jnp.pad(sm["gm_b_s"][j].T, ((0, 0), (0, LANES - GM_HEADS))),
            "gm_ln_g": _row(sm["gm_ln_g"][j]), "gm_ln_b": _row(sm["gm_ln_b"][j]),
            "conv_w": fw["ssd_conv_w"][j], "conv_b": _row(sm["ssd_conv_b"][j]),
            "dt_bias": _pad_lanes(sm["ssd_dt_bias"][j]), "a": _pad_lanes(-jnp.exp(sm["ssd_a_log"][j])),
            "d": _pad_lanes(sm["ssd_d"][j]), "norm_g": _row(sm["ssd_norm_g"][j]), "out": fw["hyb_w_out"][j],
        }
    else:
        uq = fw["mla_w_uq"][j].reshape(MLA_Q_LORA, MLA_HEADS, 192)
        uq = jnp.pad(uq, ((0, 0), (0, 0), (0, 64))).reshape(MLA_Q_LORA, MLA_HEADS * 256)
        lw["mix"] = {
            "in": jnp.pad(fw["mla_w_in"][j], ((0, 0), (0, MLA_IN_PAD - MLA_IN))), "uq": uq, "ukv": fw["mla_w_ukv"][j],
            "out": fw["mla_w_out"][j], "q_norm_g": _row(fw["mla_q_norm_g"][j]), "kv_norm_g": _row(sm["mla_kv_norm_g"][j]),
        }
    return lw


def _device_step(x, p, positions, target, fw, sm):
    t = x.shape[0]
    tq = _pick(t, (1024, 512, 256, 128))
    rope = _rope_tables(positions)
    h = x
    saved, lws = [], []
    for i in range(DEPTH):
        lw = _layer_weights(fw, sm, i)
        lws.append(lw)
        h, s1 = _ffn_fwd(h, lw["ffn1"], _row(sm["ffn1_pre_g"][i]), _row(sm["ffn1_post_g"][i]), "ffn")
        if i % 2 == 0:
            h, s2 = _hyb_fwd(h, lw["mix"], _row(sm["mix_pre_g"][i]), _row(sm["mix_post_g"][i]))
        else:
            h, s2 = _mla_fwd(h, lw["mix"], _row(sm["mix_pre_g"][i]), _row(sm["mix_post_g"][i]), rope, tq)
        h, s3 = _ffn_fwd(h, lw["ffn2"], _row(sm["ffn2_pre_g"][i]), _row(sm["ffn2_post_g"][i]), "ffn")
        h, s4 = _ple_fwd(h, p[i], lw["ple"], _row(sm["ple_pre_g"][i]), _row(sm["ple_post_g"][i]))
        saved.append((s1, s2, s3, s4))

    def loss_fn(y, tg):
        err = y - tg
        return err * (1.0 / D_MODEL), jnp.sum(_colsum(err * err), axis=1, keepdims=True)
    dh, loss_sum = _rowwise(loss_fn, [h, target], [], [(D_MODEL, F32)], [(1, 1)], name="loss")
    loss = loss_sum[0, 0] * (0.5 / D_MODEL)

    per_layer = {n: [None] * DEPTH for n in WEIGHTS if n.startswith(("ffn", "mix", "ple"))}
    per_mixer = {n: [None] * (DEPTH // 2) for n in WEIGHTS if n.startswith(("hyb", "gm", "ssd", "mla"))}
    for i in reversed(range(DEPTH)):
        lw = lws[i]
        s1, s2, s3, s4 = saved[i]
        j = i // 2
        dh, g = _ple_bwd(dh, s4, p[i], lw["ple"], _row(sm["ple_pre_g"][i]), _row(sm["ple_post_g"][i]))
        for k, v in g.items():
            per_layer["ple_" + k][i] = v
        dh, g = _ffn_bwd(dh, s3, lw["ffn2"], _row(sm["ffn2_pre_g"][i]), _row(sm["ffn2_post_g"][i]), "ffn")
        for k, v in g.items():
            per_layer["ffn2_" + k][i] = v
        if i % 2 == 0:
            dh, g = _hyb_bwd(dh, s2, lw["mix"], _row(sm["mix_pre_g"][i]), _row(sm["mix_post_g"][i]))
            names = dict(w_in="hyb_w_in", gm_ln_g="gm_ln_g", gm_ln_b="gm_ln_b", gm_w_s="gm_w_s", gm_b_s="gm_b_s",
                         conv_w="ssd_conv_w", conv_b="ssd_conv_b", dt_bias="ssd_dt_bias", a_log="ssd_a_log", d="ssd_d",
                         norm_g="ssd_norm_g", w_out="hyb_w_out")
        else:
            dh, g = _mla_bwd(dh, s2, lw["mix"], _row(sm["mix_pre_g"][i]), _row(sm["mix_post_g"][i]), rope, tq)
            g["w_in"] = g["w_in"][:, :MLA_IN]
            g["w_uq"] = g["w_uq"].reshape(MLA_Q_LORA, MLA_HEADS, 256)[:, :, :192].reshape(MLA_Q_LORA, MLA_HEADS * 192)
            names = dict(w_in="mla_w_in", q_norm_g="mla_q_norm_g", kv_norm_g="mla_kv_norm_g", w_uq="mla_w_uq",
                         w_ukv="mla_w_ukv", w_out="mla_w_out")
        per_layer["mix_pre_g"][i] = g.pop("pre_g")
        per_layer["mix_post_g"][i] = g.pop("post_g")
        for k, v in g.items():
            per_mixer[names[k]][j] = v
        dh, g = _ffn_bwd(dh, s1, lw["ffn1"], _row(sm["ffn1_pre_g"][i]), _row(sm["ffn1_post_g"][i]), "ffn")
        for k, v in g.items():
            per_layer["ffn1_" + k][i] = v

    return loss, dh, {**per_layer, **per_mixer}


def _stack_layers(parts, shape):
    return jnp.stack(parts, axis=0).reshape(shape)


MESH_AXES = ("x", "y", "c")
EXCHANGE_MAX_COPIES = 56


def _exchange(src, axes, mode, name):
    n = 2 ** len(axes)
    blk = src.shape[-2:]
    flips = [tuple(a for a, bit in zip(axes, np.binary_repr(f, len(axes))) if bit == "1") for f in range(1, n)]
    prefs = tuple(c for c in (16, 8, 4, 2, 1) if c * (n - 1) <= EXCHANGE_MAX_COPIES)
    pieces = _pick(blk[0] // 16, prefs) if blk[0] % 16 == 0 else 1
    rows = blk[0] // pieces

    def index(where):
        idx = 0
        for a in axes:
            idx = idx * 2 + where[a]
        return idx

    me_out = index({a: lax.axis_index(a) for a in MESH_AXES})
    own = lax.dynamic_index_in_dim(src, me_out, 0, keepdims=False) if mode == "a2a" else src
    landing = lax.dynamic_update_index_in_dim(lax.empty((n, *blk), src.dtype), own, me_out, 0)

    def body(src_ref, landing_ref, out_ref, send_sems, recv_sems):
        del landing_ref
        pos = {a: lax.axis_index(a) for a in MESH_AXES}
        me = index(pos)
        copies = []
        for k, flip in enumerate(flips):
            peer = {a: (1 - pos[a]) if a in flip else pos[a] for a in MESH_AXES}
            payload = src_ref.at[index(peer)] if mode == "a2a" else src_ref
            for q in range(pieces):
                part = pl.ds(q * rows, rows)
                cp = pltpu.make_async_remote_copy(
                    src_ref=payload.at[part], dst_ref=out_ref.at[me, part], send_sem=send_sems.at[k * pieces + q],
                    recv_sem=recv_sems.at[k * pieces + q], device_id=(peer["x"], peer["y"], peer["c"]),
                    device_id_type=pl.DeviceIdType.MESH)
                cp.start()
                copies.append(cp)
        for cp in copies:
            cp.wait()

    n_sems = (n - 1) * pieces
    return pl.pallas_call(
        body, name=name, in_specs=[pl.BlockSpec(memory_space=pl.ANY), pl.BlockSpec(memory_space=pl.ANY)],
        out_specs=pl.BlockSpec(memory_space=pl.ANY), out_shape=jax.ShapeDtypeStruct((n, *blk), src.dtype),
        input_output_aliases={1: 0},
        scratch_shapes=[pltpu.SemaphoreType.DMA((n_sems,)), pltpu.SemaphoreType.DMA((n_sems,))],
    )(src, landing)


def _pack_rows(n_elems):
    return -(-n_elems // (16 * PACK_W)) * 16


def _pack(parts, lead=()):
    nl = len(lead)
    rows = []
    for a in parts:
        flat = a.reshape(*lead, -1)
        r = _pack_rows(flat.shape[-1])
        flat = jnp.pad(flat, [(0, 0)] * nl + [(0, r * PACK_W - flat.shape[-1])])
        rows.append(flat.reshape(*lead, r, PACK_W))
    total = sum(r.shape[nl] for r in rows)
    pad = -total % PACK_TM
    if pad:
        rows.append(jnp.zeros((*lead, pad, PACK_W), rows[0].dtype))
    return jnp.concatenate(rows, axis=nl)


def _unpack(buf, shapes, lead=()):
    nl = len(lead)
    out, r0 = [], 0
    for shp in shapes:
        n = int(np.prod(shp))
        r = _pack_rows(n)
        piece = lax.slice_in_dim(buf, r0, r0 + r, axis=nl).reshape(*lead, r * PACK_W)
        out.append(lax.slice_in_dim(piece, 0, n, axis=nl).reshape(*lead, *shp))
        r0 += r
    return out


def _split_for_devices(g, axis):
    shp = g.shape
    g = g.reshape(*shp[:axis], N_DEV, shp[axis] // N_DEV, *shp[axis + 1:])
    return jnp.moveaxis(g, axis, 0)


def _join_from_devices(parts, axis):
    parts = jnp.moveaxis(parts, 0, axis)
    shp = parts.shape
    return parts.reshape(*shp[:axis], shp[axis] * shp[axis + 1], *shp[axis + 2:])


def _adamw_terms(w, g, m, v):
    m = ADAM_B1 * m + (1.0 - ADAM_B1) * g
    v = ADAM_B2 * v + (1.0 - ADAM_B2) * (g * g)
    m_hat = m / (1.0 - ADAM_B1 ** ADAM_STEP)
    v_hat = v / (1.0 - ADAM_B2 ** ADAM_STEP)
    delta = -ADAM_LR * (m_hat / (jnp.sqrt(v_hat) + ADAM_EPS) + ADAM_WD * w)
    return delta, m, v


def _adamw_packed(w, m, v, partials, n_partials, name):
    def fn(w_, m_, v_, *parts):
        g = parts[0].astype(F32)
        for part in parts[1:]:
            g = g + part.astype(F32)
        return (g,) + _adamw_terms(w_, g, m_, v_)
    return _rowwise(fn, [w, m, v] + [(partials, s) for s in range(n_partials)], [], [(PACK_W, F32)] * 4,
                    tm=PACK_TM, name=name)


def kernel(x, p, positions, ffn1_pre_g, ffn1_w_in, ffn1_w_down, ffn1_post_g, mix_pre_g, mix_post_g, ffn2_pre_g, ffn2_w_in, ffn2_w_down, ffn2_post_g, ple_pre_g, ple_w_gate, ple_w_proj, ple_post_g, hyb_w_in, gm_ln_g, gm_ln_b, gm_w_s, gm_b_s, ssd_conv_w, ssd_conv_b, ssd_dt_bias, ssd_a_log, ssd_d, ssd_norm_g, hyb_w_out, mla_w_in, mla_q_norm_g, mla_kv_norm_g, mla_w_uq, mla_w_ukv, mla_w_out, loss_target, m_ffn1_pre_g, m_ffn1_w_in, m_ffn1_w_down, m_ffn1_post_g, m_mix_pre_g, m_mix_post_g, m_ffn2_pre_g, m_ffn2_w_in, m_ffn2_w_down, m_ffn2_post_g, m_ple_pre_g, m_ple_w_gate, m_ple_w_proj, m_ple_post_g, m_hyb_w_in, m_gm_ln_g, m_gm_ln_b, m_gm_w_s, m_gm_b_s, m_ssd_conv_w, m_ssd_conv_b, m_ssd_dt_bias, m_ssd_a_log, m_ssd_d, m_ssd_norm_g, m_hyb_w_out, m_mla_w_in, m_mla_q_norm_g, m_mla_kv_norm_g, m_mla_w_uq, m_mla_w_ukv, m_mla_w_out, v_ffn1_pre_g, v_ffn1_w_in, v_ffn1_w_down, v_ffn1_post_g, v_mix_pre_g, v_mix_post_g, v_ffn2_pre_g, v_ffn2_w_in, v_ffn2_w_down, v_ffn2_post_g, v_ple_pre_g, v_ple_w_gate, v_ple_w_proj, v_ple_post_g, v_hyb_w_in, v_gm_ln_g, v_gm_ln_b, v_gm_w_s, v_gm_b_s, v_ssd_conv_w, v_ssd_conv_b, v_ssd_dt_bias, v_ssd_a_log, v_ssd_d, v_ssd_norm_g, v_hyb_w_out, v_mla_w_in, v_mla_q_norm_g, v_mla_kv_norm_g, v_mla_w_uq, v_mla_w_ukv, v_mla_w_out):
    given = dict(locals())
    w = {n: given[n] for n in WEIGHTS}
    mom = {n: given["m_" + n] for n in WEIGHTS}
    var = {n: given["v_" + n] for n in WEIGHTS}
    shard_shapes = [w[n].shape for n in SHARDED]
    repl_shapes = [w[n].shape for n in REPLICATED]

    send16 = {n: w[n].astype(BF16) for n in SHARDED_BF16}
    send16["hyb_w_in"] = jnp.pad(send16["hyb_w_in"], ((0, 0), (0, 0), (0, HYB_SHARD_PAD - HYB_SHARD)))
    pack16 = _pack([send16[n] for n in SHARDED_BF16])
    by_chip = _exchange(pack16, ("x", "y"), "gather", "gather_weights_ici")
    by_core = _exchange(by_chip.reshape(-1, PACK_W), ("c",), "gather", "gather_weights_d2d")
    gathered = by_core.reshape(2, 4, -1, PACK_W).transpose(1, 0, 2, 3).reshape(N_DEV, -1, PACK_W)
    fw = {n: _join_from_devices(a, SHARD_AXIS[n])
          for n, a in zip(SHARDED_BF16, _unpack(gathered, [send16[n].shape for n in SHARDED_BF16], (N_DEV,)))}
    small = _exchange(_pack([w[n] for n in SHARDED_F32]), MESH_AXES, "gather", "gather_weights_f32")
    fw.update({n: _join_from_devices(a, SHARD_AXIS[n])
               for n, a in zip(SHARDED_F32, _unpack(small, [w[n].shape for n in SHARDED_F32], (N_DEV,)))})

    loss_local, grad_x, grads = _device_step(x[0], p[:, 0], positions[0], loss_target[0], fw, w)
    loss = lax.psum(loss_local, MESH_AXES)

    per_dev = []
    for n in SHARDED:
        layers = w[n].shape[0]
        whole = (1, *w[n].shape[1:SHARD_AXIS[n]], N_DEV * w[n].shape[SHARD_AXIS[n]], *w[n].shape[SHARD_AXIS[n] + 1:])
        if int(np.prod(w[n].shape[1:])) % (16 * PACK_W) == 0:
            parts = [g.reshape(whole) for g in grads[n]]
        else:
            parts = [_stack_layers(grads[n], (layers, *whole[1:]))]
        per_dev.extend(_split_for_devices(g, SHARD_AXIS[n]) for g in parts)
    per_dev = [a.reshape(4, 2, *a.shape[1:]).swapaxes(0, 1) for a in per_dev]
    gpack = _pack(per_dev, (2, 4))
    rows = gpack.shape[2]
    pair = _exchange(gpack.reshape(2, 4 * rows, PACK_W), ("c",), "a2a", "reduce_grads_d2d")
    chip_sum = _rowwise(lambda a, b: a + b, [(pair, 0), (pair, 1)], [], [(PACK_W, BF16)], tm=PACK_TM, name="reduce_grads_pair")
    quads = _exchange(chip_sum.reshape(4, rows, PACK_W), ("x", "y"), "a2a", "reduce_grads_ici")
    g_s, d_s, m_s, v_s = _adamw_packed(_pack([w[n] for n in SHARDED]), _pack([mom[n] for n in SHARDED]),
                                       _pack([var[n] for n in SHARDED]), quads, 4, "adamw_sharded")

    rpack = _pack([_stack_layers(grads[n], w[n].shape) for n in REPLICATED])
    everyone = _exchange(rpack, MESH_AXES, "gather", "gather_small_grads")
    g_r, d_r, m_r, v_r = _adamw_packed(_pack([w[n] for n in REPLICATED]), _pack([mom[n] for n in REPLICATED]),
                                       _pack([var[n] for n in REPLICATED]), everyone, N_DEV, "adamw_replicated")

    outs = []
    for sharded_buf, repl_buf in ((g_s, g_r), (d_s, d_r), (m_s, m_r), (v_s, v_r)):
        vals = dict(zip(SHARDED, _unpack(sharded_buf, shard_shapes)))
        vals.update(zip(REPLICATED, _unpack(repl_buf, repl_shapes)))
        outs.extend(vals[n] for n in WEIGHTS)
    return (loss, grad_x[None], *outs)
```

```python
import functools
import math

import jax
import jax.numpy as jnp
import numpy as np
from jax import lax
from jax.experimental import pallas as pl
from jax.experimental.pallas import tpu as pltpu

F32 = jnp.float32
BF16 = jnp.bfloat16
HIGHEST = lax.Precision.HIGHEST

V7X_VMEM_LIMIT_BYTES = 52 * 1024 * 1024
LANES = 128

D_MODEL = 1024
DEPTH = 4
D_FF = 2816
PLE_DIM = 256
NORM_EPS = 1e-6
LN_EPS = 1e-5
CHUNK = 128
GM_HEADS = 8
SSD_HEADS = 16
SSD_HEAD_DIM = 64
SSD_INNER = 1024
SSD_STATE = 128
SSD_BC = 256
SSD_CONV_CH = 1536
HYB_MAIN = 4608
MLA_HEADS = 16
MLA_Q_LORA = 256
MLA_KV_LORA = 128
MLA_ROPE = 64
MLA_IN = 448
MLA_IN_PAD = 512
ATTN_SCALE = 192.0 ** -0.5
LOG2_E = 1.4426950408889634
LN_2 = 0.6931471805599453
ATTN_QSCALE = ATTN_SCALE * LOG2_E
ROPE_BASE = 10000.0

ADAM_LR = 0.001
ADAM_B1 = 0.9
ADAM_B2 = 0.999
ADAM_EPS = 1e-08
ADAM_WD = 0.01
ADAM_STEP = 10

N_DEV = 8
PACK_W = 1024
PACK_TM = 256

WEIGHTS = ['ffn1_pre_g', 'ffn1_w_in', 'ffn1_w_down', 'ffn1_post_g', 'mix_pre_g', 'mix_post_g', 'ffn2_pre_g',
           'ffn2_w_in', 'ffn2_w_down', 'ffn2_post_g', 'ple_pre_g', 'ple_w_gate', 'ple_w_proj', 'ple_post_g',
           'hyb_w_in', 'gm_ln_g', 'gm_ln_b', 'gm_w_s', 'gm_b_s', 'ssd_conv_w', 'ssd_conv_b', 'ssd_dt_bias',
           'ssd_a_log', 'ssd_d', 'ssd_norm_g', 'hyb_w_out', 'mla_w_in', 'mla_q_norm_g', 'mla_kv_norm_g',
           'mla_w_uq', 'mla_w_ukv', 'mla_w_out']
SHARD_AXIS = {'ffn1_w_in': 2, 'ffn1_w_down': 1, 'ffn2_w_in': 2, 'ffn2_w_down': 1, 'ple_w_gate': 1, 'ple_w_proj': 2,
              'hyb_w_in': 2, 'ssd_conv_w': 2, 'hyb_w_out': 1, 'mla_w_in': 1, 'mla_q_norm_g': 1, 'mla_w_uq': 2,
              'mla_w_ukv': 2, 'mla_w_out': 1}
SHARDED = [n for n in WEIGHTS if n in SHARD_AXIS]
REPLICATED = [n for n in WEIGHTS if n not in SHARD_AXIS]
SHARDED_F32 = ['ssd_conv_w', 'mla_q_norm_g']
SHARDED_BF16 = [n for n in SHARDED if n not in SHARDED_F32]


def _params(*sem):
    return pltpu.CompilerParams(dimension_semantics=sem or None, vmem_limit_bytes=V7X_VMEM_LIMIT_BYTES)


def _pick(n, prefs):
    for t in prefs:
        if t <= n and n % t == 0:
            return t
    return n


def _mm(a, b, *, ta=False, tb=False, out_dtype=F32, tm=1024, tn=512, tk=512, b_k0=0, name):
    m, k = (a.shape[1], a.shape[0]) if ta else a.shape
    n = b.shape[0] if tb else b.shape[1]
    b_k = b.shape[1] if tb else b.shape[0]
    assert k == b_k or (tb and b_k0 + k <= b_k), (a.shape, b.shape, ta, tb, b_k0)
    tm, tn, tk = _pick(m, (tm, 512, 256, 128)), _pick(n, (tn, 512, 256, 128)), _pick(k, (tk, 512, 256, 128))
    nk = k // tk
    assert b_k0 % tk == 0
    kb0 = b_k0 // tk
    dims = (((0 if ta else 1,), (1 if tb else 0,)), ((), ()))

    def body(a_ref, b_ref, o_ref, *acc):
        part = lax.dot_general(a_ref[...].astype(BF16), b_ref[...].astype(BF16), dims, preferred_element_type=F32)
        if nk == 1:
            o_ref[...] = part.astype(o_ref.dtype)
            return
        acc_ref, = acc
        kk = pl.program_id(2)

        @pl.when(kk == 0)
        def _():
            acc_ref[...] = part

        @pl.when(kk > 0)
        def _():
            acc_ref[...] += part

        @pl.when(kk == nk - 1)
        def _():
            o_ref[...] = acc_ref[...].astype(o_ref.dtype)

    a_spec = pl.BlockSpec((tk, tm), lambda i, j, kk: (kk, i)) if ta else pl.BlockSpec((tm, tk), lambda i, j, kk: (i, kk))
    b_spec = pl.BlockSpec((tn, tk), lambda i, j, kk: (j, kk + kb0)) if tb else pl.BlockSpec((tk, tn), lambda i, j, kk: (kk, j))
    return pl.pallas_call(
        body, name=name, grid=(m // tm, n // tn, nk), in_specs=[a_spec, b_spec],
        out_specs=pl.BlockSpec((tm, tn), lambda i, j, kk: (i, j)), out_shape=jax.ShapeDtypeStruct((m, n), out_dtype),
        scratch_shapes=[] if nk == 1 else [pltpu.VMEM((tm, tn), F32)],
        compiler_params=_params("parallel", "parallel", "arbitrary"),
    )(a, b)


def _rowwise(fn, rows, consts, outs, accs=(), *, tm=256, name):
    first = rows[0][0] if isinstance(rows[0], tuple) else rows[0]
    t = first.shape[-2]
    tm = _pick(t, (tm, 256, 128, 64, 32, 16, 8))
    n_r, n_c, n_o = len(rows), len(consts), len(outs)

    def body(*refs):
        vals = [r[...] for r in refs[:n_r + n_c]]
        res = fn(*vals)
        res = res if isinstance(res, tuple) else (res,)
        o_refs, a_refs = refs[n_r + n_c:n_r + n_c + n_o], refs[n_r + n_c + n_o:]
        for o_ref, v in zip(o_refs, res[:n_o]):
            if isinstance(v, (tuple, list)):
                off = 0
                for piece in v:
                    o_ref[:, off:off + piece.shape[1]] = piece.astype(o_ref.dtype)
                    off += piece.shape[1]
            else:
                o_ref[...] = v.astype(o_ref.dtype)
        if a_refs:
            terms = res[n_o:]
            is_first = pl.program_id(0) == 0

            @pl.when(is_first)
            def _():
                for a_ref, v in zip(a_refs, terms):
                    a_ref[...] = v

            @pl.when(jnp.logical_not(is_first))
            def _():
                for a_ref, v in zip(a_refs, terms):
                    a_ref[...] += v

    in_specs, args = [], []
    for r in rows:
        if isinstance(r, tuple) and len(r) == 3:
            arr, width, cb = r
            in_specs.append(pl.BlockSpec((tm, width), functools.partial(lambda i, c: (i, c), c=cb)))
        elif isinstance(r, tuple):
            arr, slot = r
            in_specs.append(pl.BlockSpec((None, tm, arr.shape[2]), functools.partial(lambda i, s: (s, i, 0), s=slot)))
        else:
            arr = r
            in_specs.append(pl.BlockSpec((tm, arr.shape[1]), lambda i: (i, 0)))
        args.append(arr)
    for c in consts:
        in_specs.append(pl.BlockSpec(c.shape, lambda i: (0, 0)))
        args.append(c)
    out_specs = [pl.BlockSpec((tm, c), lambda i: (i, 0)) for c, _ in outs]
    out_shape = [jax.ShapeDtypeStruct((t, c), dt) for c, dt in outs]
    for shp in accs:
        out_specs.append(pl.BlockSpec(shp, lambda i: (0, 0)))
        out_shape.append(jax.ShapeDtypeStruct(shp, F32))
    res = pl.pallas_call(
        body, name=name, grid=(t // tm,), in_specs=in_specs, out_specs=out_specs, out_shape=out_shape,
        compiler_params=_params("arbitrary" if accs else "parallel"),
    )(*args)
    return res[0] if len(res) == 1 else tuple(res)


def _colsum(v):
    return jnp.sum(v, axis=0, keepdims=True)


def _rms(x, g, eps=NORM_EPS):
    r = lax.rsqrt(jnp.mean(x * x, axis=-1, keepdims=True) + eps)
    return x * r * g


def _rms_bwd(x, g, dy, eps=NORM_EPS):
    r = lax.rsqrt(jnp.mean(x * x, axis=-1, keepdims=True) + eps)
    xh = x * r
    dyg = dy * g
    dx = r * (dyg - xh * jnp.mean(dyg * xh, axis=-1, keepdims=True))
    return dx, dy * xh


def _silu(x):
    return x * jax.nn.sigmoid(x)


def _silu_grad(x):
    s = jax.nn.sigmoid(x)
    return s * (1.0 + x * (1.0 - s))


_GELU_K = math.sqrt(2.0 / math.pi)


def _gelu(x):
    return 0.5 * x * (1.0 + jnp.tanh(_GELU_K * (x + 0.044715 * x * x * x)))


def _gelu_grad(x):
    t = jnp.tanh(_GELU_K * (x + 0.044715 * x * x * x))
    return 0.5 * (1.0 + t) + 0.5 * x * (1.0 - t * t) * _GELU_K * (1.0 + 3.0 * 0.044715 * x * x)


def _prenorm(h, g, name):
    return _rowwise(lambda x, gg: _rms(x, gg), [h], [g], [(D_MODEL, BF16)], name=name)


def _postnorm_residual(h, f, g, scale, next_g, name):
    if next_g is None:
        return _rowwise(lambda x, ff, gg: x + scale * _rms(ff, gg), [h, f], [g], [(D_MODEL, F32)], name=name), None

    def fn(x, ff, gg, ng):
        out = x + scale * _rms(ff, gg)
        return out, _rms(out, ng)
    return _rowwise(fn, [h, f], [g, next_g], [(D_MODEL, F32), (D_MODEL, BF16)], name=name + "_prenorm")


def _postnorm_bwd(f, dh, g, scale, name):
    def fn(ff, d, gg):
        dx, dgt = _rms_bwd(ff, gg, scale * d)
        return dx, _colsum(dgt)
    return _rowwise(fn, [f, dh], [g], [(D_MODEL, BF16)], [(1, D_MODEL)], name=name)


def _prenorm_bwd(h, das, dh, g, name):
    n = len(das)

    def fn(x, *rest):
        da = rest[0]
        for extra in rest[1:n]:
            da = da + extra
        d, gg = rest[n], rest[n + 1]
        dx, dgt = _rms_bwd(x, gg, da)
        return d + dx, _colsum(dgt)
    return _rowwise(fn, [h, *das, dh], [g], [(D_MODEL, F32)], [(1, D_MODEL)], name=name)


FFN_TM = 1024
FFN_TN = 256


def _ffn_in_swiglu(a, w_in, name):
    t = a.shape[0]
    tm = _pick(t, (FFN_TM, 512, 256, 128))
    nj = D_FF // FFN_TN

    def body(a_ref, wg_ref, wu_ref, gate_ref, up_ref, s_ref):
        av = a_ref[...]
        gate = jnp.dot(av, wg_ref[...], preferred_element_type=F32)
        up = jnp.dot(av, wu_ref[...], preferred_element_type=F32)
        gate_ref[...] = gate
        up_ref[...] = up
        s_ref[...] = (_silu(gate) * up).astype(s_ref.dtype)

    tile = pl.BlockSpec((tm, FFN_TN), lambda i, j: (i, j))
    return pl.pallas_call(
        body, name=name, grid=(t // tm, nj),
        in_specs=[pl.BlockSpec((tm, D_MODEL), lambda i, j: (i, 0)), pl.BlockSpec((D_MODEL, FFN_TN), lambda i, j: (0, j)),
                  pl.BlockSpec((D_MODEL, FFN_TN), lambda i, j: (0, j + nj))],
        out_specs=[tile, tile, tile],
        out_shape=[jax.ShapeDtypeStruct((t, D_FF), F32), jax.ShapeDtypeStruct((t, D_FF), F32),
                   jax.ShapeDtypeStruct((t, D_FF), BF16)],
        compiler_params=_params("parallel", "parallel"),
    )(a, w_in, w_in)


def _ffn_down_dx_swiglu(df, w_down, gate, up, name):
    t = df.shape[0]
    tm = _pick(t, (FFN_TM, 512, 256, 128))

    def body(df_ref, wd_ref, gate_ref, up_ref, dgate_ref, dup_ref):
        ds = lax.dot_general(df_ref[...], wd_ref[...], (((1,), (1,)), ((), ())), preferred_element_type=F32)
        gate = gate_ref[...]
        sg = jax.nn.sigmoid(gate)
        dgate_ref[...] = (ds * up_ref[...] * (sg * (1.0 + gate * (1.0 - sg)))).astype(dgate_ref.dtype)
        dup_ref[...] = (ds * (gate * sg)).astype(dup_ref.dtype)

    tile = pl.BlockSpec((tm, FFN_TN), lambda i, j: (i, j))
    return pl.pallas_call(
        body, name=name, grid=(t // tm, D_FF // FFN_TN),
        in_specs=[pl.BlockSpec((tm, D_MODEL), lambda i, j: (i, 0)), pl.BlockSpec((FFN_TN, D_MODEL), lambda i, j: (j, 0)),
                  tile, tile],
        out_specs=[tile, tile],
        out_shape=[jax.ShapeDtypeStruct((t, D_FF), BF16), jax.ShapeDtypeStruct((t, D_FF), BF16)],
        compiler_params=_params("parallel", "parallel"),
    )(df, w_down, gate, up)


def _ffn_fwd(h, a, w, pre_g, post_g, next_g, tag):
    if a is None:
        a = _prenorm(h, pre_g, tag + "_prenorm")
    gate, up, s = _ffn_in_swiglu(a, w["in"], tag + "_in_swiglu")
    f = _mm(s, w["down"], tm=1024, tn=1024, tk=D_FF, name=tag + "_down")
    out, a_next = _postnorm_residual(h, f, post_g, 0.5, next_g, tag + "_postnorm")
    return out, a_next, (h, a, gate, up, s, f)


def _ffn_bwd(dh, saved, w, pre_g, post_g, tag):
    h, a, gate, up, s, f = saved
    df, d_post = _postnorm_bwd(f, dh, post_g, 0.5, tag + "_postnorm_bwd")
    dgate, dup = _ffn_down_dx_swiglu(df, w["down"], gate, up, tag + "_down_dx_swiglu")
    d_down = _mm(s, df, ta=True, tm=1408, tn=1024, tk=1024, name=tag + "_down_dw")
    da_gate = _mm(dgate, w["in"], tb=True, tm=1024, tn=1024, tk=1408, name=tag + "_in_dx_gate")
    da_up = _mm(dup, w["in"], tb=True, tm=1024, tn=1024, tk=1408, b_k0=D_FF, name=tag + "_in_dx_up")
    d_in = (_mm(a, dgate, ta=True, tm=1024, tn=1408, tk=1024, name=tag + "_in_dw_gate"),
            _mm(a, dup, ta=True, tm=1024, tn=1408, tk=1024, name=tag + "_in_dw_up"))
    dh_in, d_pre = _prenorm_bwd(h, [da_gate, da_up], dh, pre_g, tag + "_prenorm_bwd")
    return dh_in, dict(w_in=d_in, w_down=d_down, pre_g=d_pre, post_g=d_post)


def _ple_fwd(h, a, p_i, w, post_g, next_g):
    gl = _mm(a, w["gate"], tm=1024, tn=1024, tk=1024, name="ple_gate")
    e = _mm(p_i, w["proj"], tm=1024, tn=1024, tk=PLE_DIM, name="ple_proj")
    if next_g is None:
        out = _rowwise(lambda x, g_, e_, gg: x + _rms(jax.nn.sigmoid(g_) * e_, gg), [h, gl, e], [post_g],
                       [(D_MODEL, F32)], name="ple_out")
        return out, None, (h, a, gl, e)

    def fn(x, g_, e_, gg, ng):
        out = x + _rms(jax.nn.sigmoid(g_) * e_, gg)
        return out, _rms(out, ng)
    out, a_next = _rowwise(fn, [h, gl, e], [post_g, next_g], [(D_MODEL, F32), (D_MODEL, BF16)], name="ple_out_prenorm")
    return out, a_next, (h, a, gl, e)


def _ple_bwd(dh, saved, p_i, w, pre_g, post_g):
    h, a, gl, e = saved

    def fn(g_, e_, d, gg):
        sg = jax.nn.sigmoid(g_)
        du, dgt = _rms_bwd(sg * e_, gg, d)
        return du * e_ * sg * (1.0 - sg), du * sg, _colsum(dgt)
    dgl, de, d_post = _rowwise(fn, [gl, e, dh], [post_g], [(D_MODEL, BF16), (D_MODEL, BF16)], [(1, D_MODEL)],
                               name="ple_out_bwd")
    da = _mm(dgl, w["gate"], tb=True, tm=1024, tn=1024, tk=1024, name="ple_gate_dx")
    d_gate = _mm(a, dgl, ta=True, tm=1024, tn=1024, tk=2048, name="ple_gate_dw")
    d_proj = _mm(p_i, de, ta=True, tm=PLE_DIM, tn=1024, tk=2048, name="ple_proj_dw")
    dh_in, d_pre = _prenorm_bwd(h, [da], dh, pre_g, "ple_prenorm_bwd")
    return dh_in, dict(w_gate=d_gate, w_proj=d_proj, pre_g=d_pre, post_g=d_post)


def _gm_layernorm(v, g, b):
    mu = jnp.mean(v, axis=-1, keepdims=True)
    xc = v - mu
    rstd = lax.rsqrt(jnp.mean(xc * xc, axis=-1, keepdims=True) + LN_EPS)
    vhat = xc * rstd
    return vhat, rstd, vhat * g + b


def _gmlp_fwd(proj, wm, bias_t, ln_g, ln_b, name):
    t = proj.shape[0]

    def body(uv_ref, wm_ref, bt_ref, g_ref, b_ref, o_ref):
        for hd in range(GM_HEADS):
            lo = hd * LANES
            u = _gelu(uv_ref[:, lo:lo + LANES])
            v = _gelu(uv_ref[:, 1024 + lo:1024 + lo + LANES])
            _, _, vln = _gm_layernorm(v, g_ref[:, lo:lo + LANES], b_ref[:, lo:lo + LANES])
            mixed = jnp.dot(wm_ref[hd], vln.astype(BF16), preferred_element_type=F32) + bt_ref[:, hd:hd + 1]
            o_ref[:, lo:lo + LANES] = (u * mixed).astype(o_ref.dtype)

    return pl.pallas_call(
        body, name=name, grid=(t // CHUNK,),
        in_specs=[pl.BlockSpec((CHUNK, 2048), lambda i: (i, 0)), pl.BlockSpec(wm.shape, lambda i: (0, 0, 0)),
                  pl.BlockSpec(bias_t.shape, lambda i: (0, 0)), pl.BlockSpec(ln_g.shape, lambda i: (0, 0)),
                  pl.BlockSpec(ln_b.shape, lambda i: (0, 0))],
        out_specs=pl.BlockSpec((CHUNK, 1024), lambda i: (i, 0)), out_shape=jax.ShapeDtypeStruct((t, 1024), BF16),
        compiler_params=_params("parallel"),
    )(proj, wm, bias_t, ln_g, ln_b)


def _gmlp_bwd(proj, dyab, wm, bias_t, ln_g, ln_b, name):
    t = proj.shape[0]
    nc = t // CHUNK

    def body(uv_ref, dy_ref, wm_ref, bt_ref, g_ref, b_ref, duv_ref, dw_ref, db_ref, dg_ref, dbeta_ref, dbacc):
        c = pl.program_id(0)

        @pl.when(c == 0)
        def _():
            dw_ref[...] = jnp.zeros_like(dw_ref)
            dbacc[...] = jnp.zeros_like(dbacc)
            dg_ref[...] = jnp.zeros_like(dg_ref)
            dbeta_ref[...] = jnp.zeros_like(dbeta_ref)

        for hd in range(GM_HEADS):
            lo = hd * LANES
            xu = uv_ref[:, lo:lo + LANES]
            xv = uv_ref[:, 1024 + lo:1024 + lo + LANES]
            u = _gelu(xu)
            g_h = g_ref[:, lo:lo + LANES]
            vhat, rstd, vln = _gm_layernorm(_gelu(xv), g_h, b_ref[:, lo:lo + LANES])
            vln16 = vln.astype(BF16)
            mixed = jnp.dot(wm_ref[hd], vln16, preferred_element_type=F32) + bt_ref[:, hd:hd + 1]
            dy = dy_ref[:, lo:lo + LANES]
            du = dy * mixed
            dmix = dy * u
            dmix16 = dmix.astype(BF16)
            dw_ref[hd] += lax.dot_general(dmix16, vln16, (((1,), (1,)), ((), ())), preferred_element_type=F32)
            dbacc[hd] += dmix
            dvln = lax.dot_general(wm_ref[hd], dmix16, (((0,), (0,)), ((), ())), preferred_element_type=F32)
            dg_ref[:, lo:lo + LANES] += _colsum(dvln * vhat)
            dbeta_ref[:, lo:lo + LANES] += _colsum(dvln)
            dvh = dvln * g_h
            dv = rstd * (dvh - jnp.mean(dvh, axis=-1, keepdims=True)
                         - vhat * jnp.mean(dvh * vhat, axis=-1, keepdims=True))
            duv_ref[:, lo:lo + LANES] = (du * _gelu_grad(xu)).astype(duv_ref.dtype)
            duv_ref[:, 1024 + lo:1024 + lo + LANES] = (dv * _gelu_grad(xv)).astype(duv_ref.dtype)

        @pl.when(c == nc - 1)
        def _():
            row = lax.broadcasted_iota(jnp.int32, (CHUNK, CHUNK), 0)
            col = lax.broadcasted_iota(jnp.int32, (CHUNK, CHUNK), 1)
            for hd in range(GM_HEADS):
                dw_ref[hd] = jnp.where(col <= row, dw_ref[hd], 0.0)
                db_ref[hd] = jnp.sum(dbacc[hd], axis=1, keepdims=True)

    return pl.pallas_call(
        body, name=name, grid=(nc,),
        in_specs=[pl.BlockSpec((CHUNK, 2048), lambda i: (i, 0)), pl.BlockSpec((CHUNK, 1024), lambda i: (i, 0)),
                  pl.BlockSpec(wm.shape, lambda i: (0, 0, 0)), pl.BlockSpec(bias_t.shape, lambda i: (0, 0)),
                  pl.BlockSpec(ln_g.shape, lambda i: (0, 0)), pl.BlockSpec(ln_b.shape, lambda i: (0, 0))],
        out_specs=[pl.BlockSpec((CHUNK, 2048), lambda i: (i, 0)), pl.BlockSpec((GM_HEADS, CHUNK, CHUNK), lambda i: (0, 0, 0)),
                   pl.BlockSpec((GM_HEADS, CHUNK, 1), lambda i: (0, 0, 0)), pl.BlockSpec((1, 1024), lambda i: (0, 0)),
                   pl.BlockSpec((1, 1024), lambda i: (0, 0))],
        out_shape=[jax.ShapeDtypeStruct((t, 2048), BF16), jax.ShapeDtypeStruct((GM_HEADS, CHUNK, CHUNK), F32),
                   jax.ShapeDtypeStruct((GM_HEADS, CHUNK, 1), F32), jax.ShapeDtypeStruct((1, 1024), F32),
                   jax.ShapeDtypeStruct((1, 1024), F32)],
        scratch_shapes=[pltpu.VMEM((GM_HEADS, CHUNK, CHUNK), F32)],
        compiler_params=_params("arbitrary"),
    )(proj, dyab, wm, bias_t, ln_g, ln_b)


def _ssd_chunk_terms(dt_pad, a_pad):
    row = lax.broadcasted_iota(jnp.int32, (CHUNK, CHUNK), 0)
    col = lax.broadcasted_iota(jnp.int32, (CHUNK, CHUNK), 1)
    tril = jnp.where(col <= row, 1.0, 0.0).astype(F32)
    a_cs = jnp.dot(tril, dt_pad * a_pad, precision=HIGHEST, preferred_element_type=F32)
    return a_cs, a_cs.T


def _pair_cols(mat, hd_a, lane_lt64):
    return jnp.where(lane_lt64, mat[:, hd_a:hd_a + 1], mat[:, hd_a + 1:hd_a + 2])


def _head_decay(a_cs, a_cs_t, hd, causal):
    seg = a_cs[:, hd:hd + 1] - a_cs_t[hd:hd + 1, :]
    return jnp.exp(jnp.where(causal, seg, -jnp.inf))


def _ssd_fwd(act, dt_pad, a_pad, d_pad, name):
    t = act.shape[0]
    nc = t // CHUNK

    def body(act_ref, dt_ref, a_ref, d_ref, y_ref, st_ref, h_sc):
        c = pl.program_id(0)

        @pl.when(c == 0)
        def _():
            h_sc[...] = jnp.zeros_like(h_sc)

        st_ref[...] = h_sc[...]
        row = lax.broadcasted_iota(jnp.int32, (CHUNK, CHUNK), 0)
        col = lax.broadcasted_iota(jnp.int32, (CHUNK, CHUNK), 1)
        causal = col <= row
        lane_lt64 = lax.broadcasted_iota(jnp.int32, (CHUNK, LANES), 1) < SSD_HEAD_DIM
        row_lt64 = lax.broadcasted_iota(jnp.int32, (LANES, 1), 0) < SSD_HEAD_DIM
        dt = dt_ref[...]
        a_cs, a_cs_t = _ssd_chunk_terms(dt, a_ref[...])
        last = a_cs[CHUNK - 1:CHUNK, :]
        for g in range(2):
            b16 = act_ref[:, SSD_INNER + g * SSD_STATE:SSD_INNER + (g + 1) * SSD_STATE].astype(BF16)
            c16 = act_ref[:, SSD_INNER + SSD_BC + g * SSD_STATE:SSD_INNER + SSD_BC + (g + 1) * SSD_STATE].astype(BF16)
            cb = lax.dot_general(c16, b16, (((1,), (1,)), ((), ())), preferred_element_type=F32)
            for pr in range(4):
                ha = g * 8 + pr * 2
                lo = ha * SSD_HEAD_DIM
                xs = act_ref[:, lo:lo + LANES]
                xd = xs * _pair_cols(dt, ha, lane_lt64)
                xd16 = xd.astype(BF16)
                ya = jnp.dot((cb * _head_decay(a_cs, a_cs_t, ha, causal)).astype(BF16), xd16, preferred_element_type=F32)
                yb = jnp.dot((cb * _head_decay(a_cs, a_cs_t, ha + 1, causal)).astype(BF16), xd16, preferred_element_type=F32)
                a_p = _pair_cols(a_cs, ha, lane_lt64)
                hp = h_sc[lo:lo + LANES, :]
                y_off = lax.dot_general(c16, hp.astype(BF16), (((1,), (1,)), ((), ())), preferred_element_type=F32)
                d_p = jnp.where(lane_lt64[:1], d_ref[:, ha:ha + 1], d_ref[:, ha + 1:ha + 2])
                y_ref[:, lo:lo + LANES] = jnp.where(lane_lt64, ya, yb) + y_off * jnp.exp(a_p) + d_p * xs
                last_p = jnp.where(lane_lt64[:1], last[:, ha:ha + 1], last[:, ha + 1:ha + 2])
                xw16 = (xd * jnp.exp(last_p - a_p)).astype(BF16)
                s_new = lax.dot_general(xw16, b16, (((0,), (0,)), ((), ())), preferred_element_type=F32)
                t_col = jnp.where(row_lt64, jnp.exp(last[:, ha:ha + 1]), jnp.exp(last[:, ha + 1:ha + 2]))
                h_sc[lo:lo + LANES, :] = t_col * hp + s_new

    return pl.pallas_call(
        body, name=name, grid=(nc,),
        in_specs=[pl.BlockSpec((CHUNK, SSD_CONV_CH), lambda i: (i, 0)), pl.BlockSpec((CHUNK, LANES), lambda i: (i, 0)),
                  pl.BlockSpec((1, LANES), lambda i: (0, 0)), pl.BlockSpec((1, LANES), lambda i: (0, 0))],
        out_specs=[pl.BlockSpec((CHUNK, SSD_INNER), lambda i: (i, 0)),
                   pl.BlockSpec((None, SSD_INNER, SSD_STATE), lambda i: (i, 0, 0))],
        out_shape=[jax.ShapeDtypeStruct((t, SSD_INNER), F32), jax.ShapeDtypeStruct((nc, SSD_INNER, SSD_STATE), F32)],
        scratch_shapes=[pltpu.VMEM((SSD_INNER, SSD_STATE), F32)],
        compiler_params=_params("arbitrary"),
    )(act, dt_pad, a_pad, d_pad)


def _ssd_bwd(act, dt_pad, a_pad, d_pad, states, dy, name):
    t = act.shape[0]
    nc = t // CHUNK

    def body(act_ref, dt_ref, a_ref, d_ref, st_ref, dy_ref, dact_ref, ddt_ref, da_ref, dd_ref, dh_sc):
        c = pl.program_id(0)

        @pl.when(c == 0)
        def _():
            dh_sc[...] = jnp.zeros_like(dh_sc)
            da_ref[...] = jnp.zeros_like(da_ref)
            dd_ref[...] = jnp.zeros_like(dd_ref)

        row = lax.broadcasted_iota(jnp.int32, (CHUNK, CHUNK), 0)
        col = lax.broadcasted_iota(jnp.int32, (CHUNK, CHUNK), 1)
        causal = col <= row
        lane = lax.broadcasted_iota(jnp.int32, (CHUNK, LANES), 1)
        lane_lt64 = lane < SSD_HEAD_DIM
        row_lt64 = lax.broadcasted_iota(jnp.int32, (LANES, 1), 0) < SSD_HEAD_DIM
        is_last = lax.broadcasted_iota(jnp.int32, (CHUNK, 1), 0) == CHUNK - 1
        dt = dt_ref[...]
        a_cs, a_cs_t = _ssd_chunk_terms(dt, a_ref[...])
        last = a_cs[CHUNK - 1:CHUNK, :]
        d_acs = jnp.zeros((CHUNK, LANES), F32)
        ddt_x = jnp.zeros((CHUNK, LANES), F32)
        dd_acc = jnp.zeros((1, LANES), F32)

        def head_sum(v, first):
            return jnp.sum(jnp.where(lane_lt64 if first else jnp.logical_not(lane_lt64), v, 0.0), axis=1, keepdims=True)

        for g in range(2):
            b_lo = SSD_INNER + g * SSD_STATE
            c_lo = SSD_INNER + SSD_BC + g * SSD_STATE
            b16 = act_ref[:, b_lo:b_lo + SSD_STATE].astype(BF16)
            c16 = act_ref[:, c_lo:c_lo + SSD_STATE].astype(BF16)
            cb = lax.dot_general(c16, b16, (((1,), (1,)), ((), ())), preferred_element_type=F32)
            dcb = jnp.zeros((CHUNK, CHUNK), F32)
            db_g = jnp.zeros((CHUNK, SSD_STATE), F32)
            dc_g = jnp.zeros((CHUNK, SSD_STATE), F32)
            for pr in range(4):
                ha = g * 8 + pr * 2
                lo = ha * SSD_HEAD_DIM
                xs = act_ref[:, lo:lo + LANES]
                dt_p = _pair_cols(dt, ha, lane_lt64)
                xd = xs * dt_p
                xd16 = xd.astype(BF16)
                a_p = _pair_cols(a_cs, ha, lane_lt64)
                exp_a = jnp.exp(a_p)
                last_p = jnp.where(lane_lt64[:1], last[:, ha:ha + 1], last[:, ha + 1:ha + 2])
                w_p = jnp.exp(last_p - a_p)
                hp = st_ref[lo:lo + LANES, :]
                hp16 = hp.astype(BF16)
                dhn = dh_sc[lo:lo + LANES, :]
                dhn16 = dhn.astype(BF16)
                dyp = dy_ref[:, lo:lo + LANES]
                d_p = jnp.where(lane_lt64[:1], d_ref[:, ha:ha + 1], d_ref[:, ha + 1:ha + 2])
                dd_acc = dd_acc + jnp.where(lane[:1] == ha, jnp.sum(head_sum(dyp * xs, True), axis=0, keepdims=True), 0.0) \
                    + jnp.where(lane[:1] == ha + 1, jnp.sum(head_sum(dyp * xs, False), axis=0, keepdims=True), 0.0)
                g_off = lax.dot_general(c16, hp16, (((1,), (1,)), ((), ())), preferred_element_type=F32)
                dg16 = (dyp * exp_a).astype(BF16)
                dc_g = dc_g + jnp.dot(dg16, hp16, preferred_element_type=F32)
                dh_prev = lax.dot_general(dg16, c16, (((0,), (0,)), ((), ())), preferred_element_type=F32)
                off_term = dyp * g_off * exp_a
                q = lax.dot_general(b16, dhn16, (((1,), (1,)), ((), ())), preferred_element_type=F32)
                xw16 = (xd * w_p).astype(BF16)
                db_g = db_g + jnp.dot(xw16, dhn16, preferred_element_type=F32)
                dw_term = xd * q * w_p
                dxd = w_p * q
                dt_all = dhn * hp
                dyp16 = dyp.astype(BF16)
                for k, first in ((0, True), (1, False)):
                    hd = ha + k
                    sel = lane_lt64 if first else jnp.logical_not(lane_lt64)
                    decay = _head_decay(a_cs, a_cs_t, hd, causal)
                    m = cb * decay
                    dy_h = jnp.where(sel, dyp16, jnp.zeros_like(dyp16))
                    dm = lax.dot_general(dy_h, xd16, (((1,), (1,)), ((), ())), preferred_element_type=F32)
                    dcb = dcb + dm * decay
                    dseg = dm * m
                    dxd = dxd + jnp.where(sel, lax.dot_general(m.astype(BF16), dyp16, (((0,), (0,)), ((), ())),
                                                               preferred_element_type=F32), 0.0)
                    d_col = jnp.sum(dseg, axis=1, keepdims=True) - jnp.sum(dseg.T, axis=1, keepdims=True)
                    dw_col = head_sum(dw_term, first)
                    d_col = d_col + head_sum(off_term, first) - dw_col
                    t_h = jnp.exp(last[:, hd:hd + 1])
                    dt_sum = jnp.sum(jnp.sum(jnp.where(row_lt64 if first else jnp.logical_not(row_lt64), dt_all, 0.0),
                                             axis=1, keepdims=True), axis=0, keepdims=True)
                    end_term = jnp.sum(dw_col, axis=0, keepdims=True) + dt_sum * t_h
                    d_col = d_col + jnp.where(is_last, end_term, 0.0)
                    d_acs = d_acs + jnp.where(lane == hd, d_col, 0.0)
                t_col = jnp.where(row_lt64, jnp.exp(last[:, ha:ha + 1]), jnp.exp(last[:, ha + 1:ha + 2]))
                dh_sc[lo:lo + LANES, :] = t_col * dhn + dh_prev
                dact_ref[:, lo:lo + LANES] = d_p * dyp + dxd * dt_p
                ddt_all = dxd * xs
                ddt_x = ddt_x + jnp.where(lane == ha, head_sum(ddt_all, True), 0.0) \
                    + jnp.where(lane == ha + 1, head_sum(ddt_all, False), 0.0)
            dcb16 = dcb.astype(BF16)
            dact_ref[:, b_lo:b_lo + SSD_STATE] = db_g + lax.dot_general(dcb16, c16, (((0,), (0,)), ((), ())),
                                                                          preferred_element_type=F32)
            dact_ref[:, c_lo:c_lo + SSD_STATE] = dc_g + jnp.dot(dcb16, b16, preferred_element_type=F32)
        triu = jnp.where(col >= row, 1.0, 0.0).astype(F32)
        dda = jnp.dot(triu, d_acs, precision=HIGHEST, preferred_element_type=F32)
        ddt_ref[...] = dda * a_ref[...] + ddt_x
        da_ref[...] += _colsum(dda * dt)
        dd_ref[...] += dd_acc

    rev = lambda i: (nc - 1 - i, 0)
    return pl.pallas_call(
        body, name=name, grid=(nc,),
        in_specs=[pl.BlockSpec((CHUNK, SSD_CONV_CH), rev), pl.BlockSpec((CHUNK, LANES), rev),
                  pl.BlockSpec((1, LANES), lambda i: (0, 0)), pl.BlockSpec((1, LANES), lambda i: (0, 0)),
                  pl.BlockSpec((None, SSD_INNER, SSD_STATE), lambda i: (nc - 1 - i, 0, 0)),
                  pl.BlockSpec((CHUNK, SSD_INNER), rev)],
        out_specs=[pl.BlockSpec((CHUNK, SSD_CONV_CH), rev), pl.BlockSpec((CHUNK, LANES), rev),
                   pl.BlockSpec((1, LANES), lambda i: (0, 0)), pl.BlockSpec((1, LANES), lambda i: (0, 0))],
        out_shape=[jax.ShapeDtypeStruct((t, SSD_CONV_CH), F32), jax.ShapeDtypeStruct((t, LANES), F32),
                   jax.ShapeDtypeStruct((1, LANES), F32), jax.ShapeDtypeStruct((1, LANES), F32)],
        scratch_shapes=[pltpu.VMEM((SSD_INNER, SSD_STATE), F32)],
        compiler_params=_params("arbitrary"),
    )(act, dt_pad, a_pad, d_pad, states, dy)


def _shift_down(x, k):
    return x if k == 0 else jnp.pad(x, ((k, 0), (0, 0)))[:x.shape[0]]


def _shift_up(x, k):
    return x if k == 0 else jnp.pad(x, ((0, k), (0, 0)))[k:]


def _conv_pre(x0, x1, x2, x3, w, b):
    return x0 * w[0:1] + x1 * w[1:2] + x2 * w[2:3] + x3 * w[3:4] + b


def _rope128(x, cpad, s_lo, s_hi):
    return x * cpad + pltpu.roll(x, 96, 1) * s_lo + pltpu.roll(x, 32, 1) * s_hi


ATTN_ROW_SPLIT = 4
ATTN_ROW_SPLIT_DKV = 4


def _diag_mask(rows, cols, row0):
    return lax.broadcasted_iota(jnp.int32, (rows, cols), 1) <= row0 + lax.broadcasted_iota(jnp.int32, (rows, cols), 0)


def _attn_scores(q, k):
    return lax.dot_general(q, k, (((1,), (1,)), ((), ())), preferred_element_type=F32)


def _causal_pairs(nq, by_key):
    if by_key:
        pairs = [(i, j) for j in range(nq) for i in range(j, nq)]
    else:
        pairs = [(i, j) for i in range(nq) for j in range(i + 1)]
    return (jnp.asarray([pr[0] for pr in pairs], jnp.int32), jnp.asarray([pr[1] for pr in pairs], jnp.int32))


def _attn_fwd(qf, kf, kvf, *, tq, name):
    t = qf.shape[0]
    nq = t // tq
    tk = tq
    qi, kj = _causal_pairs(nq, by_key=False)
    rs = tq // ATTN_ROW_SPLIT

    def body(qi_ref, kj_ref, q_ref, k_ref, v_ref, o_ref, lse_ref, m_sc, acc_sc, v1_sc):
        pp = pl.program_id(1)
        i, j = qi_ref[pp], kj_ref[pp]

        @pl.when(pp == 0)
        def _():
            v1_sc[:, LANES:] = jnp.ones((tk, LANES), BF16)

        @pl.when(j == 0)
        def _():
            m_sc[...] = jnp.full_like(m_sc, -jnp.inf)
            acc_sc[...] = jnp.zeros_like(acc_sc)

        v1_sc[:, :LANES] = v_ref[...]

        def update(diag):
            for r in range(ATTN_ROW_SPLIT):
                rows = slice(r * rs, (r + 1) * rs)
                keys = slice(0, (r + 1) * rs if diag else tk)
                s = _attn_scores(q_ref[rows, :], k_ref[keys, :])
                if diag:
                    s = jnp.where(_diag_mask(rs, keys.stop, r * rs), s, -jnp.inf)
                m_prev = m_sc[rows, :]
                m_new = jnp.maximum(m_prev, jnp.max(s, axis=1, keepdims=True))
                p = jnp.exp2(s - m_new).astype(BF16)
                alpha = jnp.exp2(m_prev - m_new)
                acc = alpha * acc_sc[rows, :] + jnp.dot(p, v1_sc[keys, :], preferred_element_type=F32)
                if diag:
                    o_ref[rows, :] = (acc[:, :LANES] / acc[:, LANES:]).astype(o_ref.dtype)
                    lse_ref[rows, :] = m_new + jnp.log2(acc[:, LANES:LANES + 1])
                else:
                    acc_sc[rows, :] = acc
                    m_sc[rows, :] = m_new

        @pl.when(j < i)
        def _():
            update(False)

        @pl.when(j == i)
        def _():
            update(True)

    return pl.pallas_call(
        body, name=name,
        grid_spec=pltpu.PrefetchScalarGridSpec(
            num_scalar_prefetch=2, grid=(MLA_HEADS, int(qi.shape[0])),
            in_specs=[pl.BlockSpec((tq, 2 * LANES), lambda h, pp, qi_, kj_: (qi_[pp], h)),
                      pl.BlockSpec((tk, 2 * LANES), lambda h, pp, qi_, kj_: (kj_[pp], h)),
                      pl.BlockSpec((tk, LANES), lambda h, pp, qi_, kj_: (kj_[pp], 2 * h + 1))],
            out_specs=[pl.BlockSpec((tq, LANES), lambda h, pp, qi_, kj_: (qi_[pp], h)),
                       pl.BlockSpec((None, tq, 1), lambda h, pp, qi_, kj_: (h, qi_[pp], 0))],
            scratch_shapes=[pltpu.VMEM((tq, 1), F32), pltpu.VMEM((tq, 2 * LANES), F32), pltpu.VMEM((tk, 2 * LANES), BF16)]),
        out_shape=[jax.ShapeDtypeStruct((t, MLA_HEADS * LANES), BF16), jax.ShapeDtypeStruct((MLA_HEADS, t, 1), F32)],
        compiler_params=_params("arbitrary", "arbitrary"),
    )(qi, kj, qf, kf, kvf)


def _attn_bwd(qf, kf, kvf, o, do, lse, *, tq, name):
    t = qf.shape[0]
    nq = t // tq
    tk = tq
    qi, kj = _causal_pairs(nq, by_key=True)
    rs = tq // ATTN_ROW_SPLIT_DKV

    def body(qi_ref, kj_ref, q_ref, k_ref, v_ref, o_ref, do_ref, lse_ref, dkv_ref, dkr_ref, dq_ref, dk_sc, dv_sc):
        pp = pl.program_id(1)
        i, j = qi_ref[pp], kj_ref[pp]
        tn = (((0,), (0,)), ((), ()))

        @pl.when(pp == 0)
        def _():
            dq_ref[...] = jnp.zeros_like(dq_ref)

        def update(diag):
            if diag:
                dv_sc[...] = jnp.zeros_like(dv_sc)
                dk_sc[...] = jnp.zeros_like(dk_sc)
            for r in range(ATTN_ROW_SPLIT_DKV):
                rows = slice(r * rs, (r + 1) * rs)
                keys = slice(0, (r + 1) * rs if diag else tk)
                do_ = do_ref[rows, :]
                delta = jnp.sum(do_.astype(F32) * o_ref[rows, :].astype(F32), axis=1, keepdims=True)
                s = _attn_scores(q_ref[rows, :], k_ref[keys, :])
                p = jnp.exp2(s - lse_ref[rows, :])
                if diag:
                    p = jnp.where(_diag_mask(rs, keys.stop, r * rs), p, 0.0)
                dp = lax.dot_general(do_, v_ref[keys, :], (((1,), (1,)), ((), ())), preferred_element_type=F32)
                ds = (p * (dp - delta)).astype(BF16)
                dv_sc[keys, :] += lax.dot_general(p.astype(BF16), do_, tn, preferred_element_type=F32)
                dk_sc[keys, :] += lax.dot_general(ds, q_ref[rows, :], tn, preferred_element_type=F32)
                q_rows = pl.ds(pl.multiple_of(i * tq + r * rs, rs), rs)
                dq_ref[q_rows, :] += jnp.dot(ds, k_ref[keys, :], preferred_element_type=F32)

        @pl.when(i > j)
        def _():
            update(False)

        @pl.when(i == j)
        def _():
            update(True)

        @pl.when(i == nq - 1)
        def _():
            dkv_ref[:, :LANES] = (dk_sc[:, :LANES] * LN_2).astype(dkv_ref.dtype)
            dkv_ref[:, LANES:] = dv_sc[...].astype(dkv_ref.dtype)
            dkr_ref[...] = dk_sc[:, LANES:] * LN_2

    qblk = lambda c: (lambda h, pp, qi_, kj_: (qi_[pp], c(h)))
    kblk = lambda c: (lambda h, pp, qi_, kj_: (kj_[pp], c(h)))
    return pl.pallas_call(
        body, name=name,
        grid_spec=pltpu.PrefetchScalarGridSpec(
            num_scalar_prefetch=2, grid=(MLA_HEADS, int(qi.shape[0])),
            in_specs=[pl.BlockSpec((tq, 2 * LANES), qblk(lambda h: h)), pl.BlockSpec((tk, 2 * LANES), kblk(lambda h: h)),
                      pl.BlockSpec((tk, LANES), kblk(lambda h: 2 * h + 1)),
                      pl.BlockSpec((tq, LANES), qblk(lambda h: h)), pl.BlockSpec((tq, LANES), qblk(lambda h: h)),
                      pl.BlockSpec((None, tq, 1), lambda h, pp, qi_, kj_: (h, qi_[pp], 0))],
            out_specs=[pl.BlockSpec((tk, 2 * LANES), kblk(lambda h: h)), pl.BlockSpec((tk, LANES), kblk(lambda h: h)),
                       pl.BlockSpec((t, 2 * LANES), lambda h, pp, qi_, kj_: (0, h))],
            scratch_shapes=[pltpu.VMEM((tk, 2 * LANES), F32), pltpu.VMEM((tk, LANES), F32)]),
        out_shape=[jax.ShapeDtypeStruct((t, MLA_HEADS * 2 * LANES), BF16), jax.ShapeDtypeStruct((t, MLA_HEADS * LANES), F32),
                   jax.ShapeDtypeStruct((t, MLA_HEADS * 2 * LANES), F32)],
        compiler_params=_params("parallel", "arbitrary"),
    )(qi, kj, qf, kf, kvf, o, do, lse)


def _rope_tables(positions):
    t = positions.shape[0]
    inv = 1.0 / (ROPE_BASE ** (jnp.arange(0, MLA_ROPE, 2, dtype=F32) / MLA_ROPE))
    ang = positions.astype(F32)[:, None] * inv
    cos, sin = jnp.cos(ang), jnp.sin(ang)
    z32, z64 = jnp.zeros((t, 32), F32), jnp.zeros((t, 64), F32)
    cpad = jnp.concatenate([cos, cos, z64], axis=1)
    s_lo = jnp.concatenate([-sin, z32, z64], axis=1)
    s_hi = jnp.concatenate([z32, sin, z64], axis=1)
    return cpad, s_lo, s_hi


def _mla_fwd(h, hn, w, post_g, next_g, rope, tq):
    cpad, s_lo, s_hi = rope
    cin = _mm(hn, w["in"], tm=1024, tn=512, tk=1024, name="mla_in")

    def lat(c, cp, sl, sh, qg, kvg):
        cq, ckv, kr = c[:, :MLA_Q_LORA], c[:, MLA_Q_LORA:MLA_Q_LORA + MLA_KV_LORA], c[:, MLA_Q_LORA + MLA_KV_LORA:]
        return _rms(cq, qg), _rms(ckv, kvg), _rope128(kr, cp, sl, sh)
    cqn, ckvn, kr = _rowwise(lat, [cin, cpad, s_lo, s_hi], [w["q_norm_g"], w["kv_norm_g"]],
                             [(MLA_Q_LORA, BF16), (MLA_KV_LORA, BF16), (LANES, BF16)], name="mla_latent")
    q_raw = _mm(cqn, w["uq"], tm=1024, tn=1024, tk=MLA_Q_LORA, name="mla_uq")

    def rope_q(q, cp, sl, sh):
        pieces = []
        for hd in range(MLA_HEADS):
            pieces.append(q[:, 256 * hd:256 * hd + LANES] * ATTN_QSCALE)
            pieces.append(_rope128(q[:, 256 * hd + LANES:256 * hd + 256], cp, sl, sh) * ATTN_QSCALE)
        return (tuple(pieces),)
    qf = _rowwise(rope_q, [q_raw, cpad, s_lo, s_hi], [], [(4096, BF16)], name="mla_rope_q")
    kvf = _mm(ckvn, w["ukv"], out_dtype=BF16, tm=1024, tn=1024, tk=MLA_KV_LORA, name="mla_ukv")
    t = h.shape[0]
    k_nope = kvf.reshape(t, MLA_HEADS, 2 * LANES)[:, :, :LANES]
    kf = jnp.concatenate([k_nope, jnp.broadcast_to(kr[:, None, :], k_nope.shape)], axis=2).reshape(t, MLA_HEADS * 2 * LANES)
    o, lse = _attn_fwd(qf, kf, kvf, tq=tq, name="mla_attn")
    mixed = _mm(o, w["out"], tm=1024, tn=1024, tk=2048, name="mla_out")
    out, a_next = _postnorm_residual(h, mixed, post_g, 1.0, next_g, "mla_postnorm")
    return out, a_next, (h, hn, cin, cqn, ckvn, qf, kf, kvf, o, lse, mixed)


def _mla_bwd(dh, saved, w, pre_g, post_g, rope, tq):
    cpad, s_lo, s_hi = rope
    h, hn, cin, cqn, ckvn, qf, kf, kvf, o, lse, mixed = saved
    dmixed, d_post = _postnorm_bwd(mixed, dh, post_g, 1.0, "mla_postnorm_bwd")
    do = _mm(dmixed, w["out"], tb=True, out_dtype=BF16, tm=1024, tn=1024, tk=1024, name="mla_out_dx")
    d_out = _mm(o, dmixed, ta=True, tm=1024, tn=1024, tk=2048, name="mla_out_dw")
    dkvf, dkr_heads, dq = _attn_bwd(qf, kf, kvf, o, do, lse, tq=tq, name="mla_attn_bwd")

    def unrope_q(d, cp, sl, sh):
        pieces = []
        for hd in range(MLA_HEADS):
            pieces.append(d[:, 256 * hd:256 * hd + LANES] * ATTN_SCALE)
            pieces.append(_rope128(d[:, 256 * hd + LANES:256 * hd + 256], cp, -sl, -sh) * ATTN_SCALE)
        return (tuple(pieces),)
    dq_raw = _rowwise(unrope_q, [dq, cpad, s_lo, s_hi], [], [(4096, BF16)], name="mla_rope_q_bwd")
    dcqn = _mm(dq_raw, w["uq"], tb=True, tm=1024, tn=256, tk=1024, name="mla_uq_dx")
    d_uq = _mm(cqn, dq_raw, ta=True, tm=256, tn=1024, tk=2048, name="mla_uq_dw")
    dckvn = _mm(dkvf, w["ukv"], tb=True, tm=1024, tn=128, tk=1024, name="mla_ukv_dx")
    d_ukv = _mm(ckvn, dkvf, ta=True, tm=128, tn=1024, tk=2048, name="mla_ukv_dw")

    def lat_bwd(c, dq_, dkv_, dkrh, cp, sl, sh, qg, kvg):
        cq, ckv = c[:, :MLA_Q_LORA], c[:, MLA_Q_LORA:MLA_Q_LORA + MLA_KV_LORA]
        dcq, dqg = _rms_bwd(cq, qg, dq_)
        dckv, dkvg = _rms_bwd(ckv, kvg, dkv_)
        dkr = dkrh[:, :LANES]
        for hd in range(1, MLA_HEADS):
            dkr = dkr + dkrh[:, hd * LANES:(hd + 1) * LANES]
        return (dcq, dckv, _rope128(dkr, cp, -sl, -sh)), _colsum(dqg), _colsum(dkvg)
    dcin, d_qg, d_kvg = _rowwise(lat_bwd, [cin, dcqn, dckvn, dkr_heads, cpad, s_lo, s_hi], [w["q_norm_g"], w["kv_norm_g"]],
                                 [(MLA_IN_PAD, BF16)], [(1, MLA_Q_LORA), (1, MLA_KV_LORA)], name="mla_latent_bwd")
    dhn = _mm(dcin, w["in"], tb=True, tm=1024, tn=1024, tk=512, name="mla_in_dx")
    d_in = _mm(hn, dcin, ta=True, tm=1024, tn=512, tk=2048, name="mla_in_dw")
    dh_in, d_pre = _prenorm_bwd(h, [dhn], dh, pre_g, "mla_prenorm_bwd")
    return dh_in, dict(w_in=d_in, q_norm_g=d_qg, kv_norm_g=d_kvg, w_uq=d_uq, w_ukv=d_ukv, w_out=d_out,
                       pre_g=d_pre, post_g=d_post)


def _hyb_fwd(h, hn, w, post_g, next_g):
    proj = _mm(hn, w["main"], tm=1024, tn=512, tk=1024, name="hyb_in")
    dtr = _mm(hn, w["dt"], tm=1024, tn=LANES, tk=1024, name="hyb_in_dt")
    ya = _gmlp_fwd(proj, w["gm_w"], w["gm_bt"], w["gm_ln_g"], w["gm_ln_b"], "gmlp")
    xbc = proj[:, 3072:]
    xsh = [_shift_down(xbc, 3 - k) for k in range(3)] + [(proj, SSD_CONV_CH, 2)]
    act = _rowwise(lambda x0, x1, x2, x3, cw, cb: _silu(_conv_pre(x0, x1, x2, x3, cw, cb)), xsh,
                   [w["conv_w"], w["conv_b"]], [(SSD_CONV_CH, F32)], name="ssd_conv")
    dt_pad = _rowwise(lambda d, b: jax.nn.softplus(d + b), [dtr], [w["dt_bias"]], [(LANES, F32)], name="ssd_dt")
    y, states = _ssd_fwd(act, dt_pad, w["a"], w["d"], "ssd_scan")

    def gate_norm(y_, z, ng):
        yg = y_ * _silu(z)
        return ((_rms(yg[:, :512], ng[:, :512]), _rms(yg[:, 512:], ng[:, 512:])),)
    yb = _rowwise(gate_norm, [y, (proj, SSD_INNER, 2)], [w["norm_g"]], [(SSD_INNER, BF16)], name="ssd_gate_norm")
    yab = jnp.concatenate([ya, yb], axis=1)
    mixed = _mm(yab, w["out"], tm=1024, tn=1024, tk=2048, name="hyb_out")
    out, a_next = _postnorm_residual(h, mixed, post_g, 1.0, next_g, "hyb_postnorm")
    return out, a_next, (h, hn, proj, dtr, xsh, act, dt_pad, y, states, yab, mixed)


def _hyb_bwd(dh, saved, w, pre_g, post_g):
    h, hn, proj, dtr, xsh, act, dt_pad, y, states, yab, mixed = saved
    dmixed, d_post = _postnorm_bwd(mixed, dh, post_g, 1.0, "hyb_postnorm_bwd")
    dyab = _mm(dmixed, w["out"], tb=True, tm=1024, tn=1024, tk=1024, name="hyb_out_dx")
    d_out = _mm(yab, dmixed, ta=True, tm=1024, tn=1024, tk=2048, name="hyb_out_dw")

    def gate_norm_bwd(y_, z, d, ng):
        sz = _silu(z)
        yg = y_ * sz
        d_lo, g_lo = _rms_bwd(yg[:, :512], ng[:, :512], d[:, 1024:1536])
        d_hi, g_hi = _rms_bwd(yg[:, 512:], ng[:, 512:], d[:, 1536:])
        dyg = jnp.concatenate([d_lo, d_hi], axis=1)
        return dyg * sz, dyg * y_ * _silu_grad(z), _colsum(jnp.concatenate([g_lo, g_hi], axis=1))
    dy, dz, d_norm = _rowwise(gate_norm_bwd, [y, (proj, SSD_INNER, 2), dyab], [w["norm_g"]], [(SSD_INNER, F32), (SSD_INNER, BF16)],
                              [(1, SSD_INNER)], name="ssd_gate_norm_bwd")
    dact, ddt, da_sum, dd_sum = _ssd_bwd(act, dt_pad, w["a"], w["d"], states, dy, "ssd_scan_bwd")

    def conv_bwd(x0, x1, x2, x3, da_, cw, cb):
        dpre = da_ * _silu_grad(_conv_pre(x0, x1, x2, x3, cw, cb))
        dw = jnp.concatenate([_colsum(dpre * x0), _colsum(dpre * x1), _colsum(dpre * x2), _colsum(dpre * x3)], axis=0)
        return dpre, dw, _colsum(dpre)
    dconv, d_conv_w, d_conv_b = _rowwise(conv_bwd, [*xsh, dact], [w["conv_w"], w["conv_b"]], [(SSD_CONV_CH, F32)],
                                         [(4, SSD_CONV_CH), (1, SSD_CONV_CH)], name="ssd_conv_bwd")
    dsh = [_shift_up(dconv, 3 - k) for k in range(4)]
    dxbc = _rowwise(lambda d0, d1, d2, d3, cw: d0 * cw[0:1] + d1 * cw[1:2] + d2 * cw[2:3] + d3 * cw[3:4], dsh,
                    [w["conv_w"]], [(SSD_CONV_CH, BF16)], name="ssd_conv_dx")

    def dt_bwd(dd, d, b):
        g = dd * jax.nn.sigmoid(d + b)
        g = jnp.where(lax.broadcasted_iota(jnp.int32, g.shape, 1) < SSD_HEADS, g, 0.0)
        return g, _colsum(g)
    ddtr, d_dt_bias = _rowwise(dt_bwd, [ddt, dtr], [w["dt_bias"]], [(LANES, BF16)], [(1, LANES)], name="ssd_dt_bwd")
    duv, d_gm_w, d_gm_b, d_ln_g, d_ln_b = _gmlp_bwd(proj, dyab, w["gm_w"], w["gm_bt"], w["gm_ln_g"], w["gm_ln_b"],
                                                    "gmlp_bwd")
    dproj = jnp.concatenate([duv, dz, dxbc], axis=1)
    dhn_a = _mm(dproj, w["main"], tb=True, tm=1024, tn=1024, tk=1536, name="hyb_in_dx")
    dhn_b = _mm(ddtr, w["dt"], tb=True, tm=1024, tn=1024, tk=LANES, name="hyb_in_dt_dx")
    d_main = _mm(hn, dproj, ta=True, tm=1024, tn=512, tk=2048, name="hyb_in_dw")
    d_dt = _mm(hn, ddtr, ta=True, tm=1024, tn=LANES, tk=2048, name="hyb_in_dt_dw")
    dh_in, d_pre = _prenorm_bwd(h, [dhn_a, dhn_b], dh, pre_g, "hyb_prenorm_bwd")
    grads = dict(w_in=jnp.concatenate([d_main, d_dt[:, :SSD_HEADS]], axis=1), gm_ln_g=d_ln_g, gm_ln_b=d_ln_b,
                 gm_w_s=d_gm_w, gm_b_s=d_gm_b[:, :, 0], conv_w=d_conv_w, conv_b=d_conv_b,
                 dt_bias=d_dt_bias[:, :SSD_HEADS], a_log=(da_sum * w["a"])[:, :SSD_HEADS], d=dd_sum[:, :SSD_HEADS],
                 norm_g=d_norm, w_out=d_out, pre_g=d_pre, post_g=d_post)
    return dh_in, grads


def _row(v):
    return v.reshape(1, -1).astype(F32)


def _pad_lanes(v, n=LANES):
    v = _row(v)
    return jnp.pad(v, ((0, 0), (0, n - v.shape[1])))


HYB_IN = 4624
HYB_SHARD = HYB_IN // N_DEV
HYB_SHARD_PAD = 640


def _hyb_unblock_matrix():
    n = N_DEV * HYB_SHARD_PAD
    r = lax.broadcasted_iota(jnp.int32, (n, n), 0)
    c = lax.broadcasted_iota(jnp.int32, (n, n), 1)
    j = r % HYB_SHARD_PAD
    return jnp.logical_and(j < HYB_SHARD, c == HYB_SHARD * (r // HYB_SHARD_PAD) + j).astype(BF16)


def _layer_weights(fw, sm, i):
    j = i // 2
    lw = dict(
        ffn1=dict({"in": fw["ffn1_w_in"][i], "down": fw["ffn1_w_down"][i]}),
        ffn2=dict({"in": fw["ffn2_w_in"][i], "down": fw["ffn2_w_down"][i]}),
        ple=dict(gate=fw["ple_w_gate"][i], proj=fw["ple_w_proj"][i]),
    )
    if i % 2 == 0:
        w_in = _mm(fw["hyb_w_in"][j], _hyb_unblock_matrix(), out_dtype=BF16, tm=1024, tn=512, tk=1024, name="hyb_w_unblock")
        causal = jnp.tril(jnp.ones((CHUNK, CHUNK), dtype=bool))
        lw["mix"] = {
            "main": w_in[:, :HYB_MAIN], "dt": w_in[:, HYB_MAIN:HYB_MAIN + LANES],
            "gm_w": jnp.where(causal[None], sm["gm_w_s"][j], 0.0).astype(BF16),
            "gm_bt": jnp.pad(sm["gm_b_s"][j].T, ((0, 0), (0, LANES - GM_HEADS))),
            "gm_ln_g": _row(sm["gm_ln_g"][j]), "gm_ln_b": _row(sm["gm_ln_b"][j]),
            "conv_w": fw["ssd_conv_w"][j], "conv_b": _row(sm["ssd_conv_b"][j]),
            "dt_bias": _pad_lanes(sm["ssd_dt_bias"][j]), "a": _pad_lanes(-jnp.exp(sm["ssd_a_log"][j])),
            "d": _pad_lanes(sm["ssd_d"][j]), "norm_g": _row(sm["ssd_norm_g"][j]), "out": fw["hyb_w_out"][j],
        }
    else:
        uq = fw["mla_w_uq"][j].reshape(MLA_Q_LORA, MLA_HEADS, 192)
        uq = jnp.pad(uq, ((0, 0), (0, 0), (0, 64))).reshape(MLA_Q_LORA, MLA_HEADS * 256)
        lw["mix"] = {
            "in": jnp.pad(fw["mla_w_in"][j], ((0, 0), (0, MLA_IN_PAD - MLA_IN))), "uq": uq, "ukv": fw["mla_w_ukv"][j],
            "out": fw["mla_w_out"][j], "q_norm_g": _row(fw["mla_q_norm_g"][j]), "kv_norm_g": _row(sm["mla_kv_norm_g"][j]),
        }
    return lw


def _device_step(x, p, positions, target, fw, sm):
    t = x.shape[0]
    tq = _pick(t, (1024, 512, 256, 128))
    rope = _rope_tables(positions)
    h = x
    saved, lws = [], []
    a = None
    for i in range(DEPTH):
        lw = _layer_weights(fw, sm, i)
        lws.append(lw)
        after = _row(sm["ffn1_pre_g"][i + 1]) if i + 1 < DEPTH else None
        h, a, s1 = _ffn_fwd(h, a, lw["ffn1"], _row(sm["ffn1_pre_g"][i]), _row(sm["ffn1_post_g"][i]),
                            _row(sm["mix_pre_g"][i]), "ffn")
        if i % 2 == 0:
            h, a, s2 = _hyb_fwd(h, a, lw["mix"], _row(sm["mix_post_g"][i]), _row(sm["ffn2_pre_g"][i]))
        else:
            h, a, s2 = _mla_fwd(h, a, lw["mix"], _row(sm["mix_post_g"][i]), _row(sm["ffn2_pre_g"][i]), rope, tq)
        h, a, s3 = _ffn_fwd(h, a, lw["ffn2"], _row(sm["ffn2_pre_g"][i]), _row(sm["ffn2_post_g"][i]),
                            _row(sm["ple_pre_g"][i]), "ffn")
        h, a, s4 = _ple_fwd(h, a, p[i], lw["ple"], _row(sm["ple_post_g"][i]), after)
        saved.append((s1, s2, s3, s4))

    def loss_fn(y, tg):
        err = y - tg
        return err * (1.0 / D_MODEL), jnp.sum(_colsum(err * err), axis=1, keepdims=True)
    dh, loss_sum = _rowwise(loss_fn, [h, target], [], [(D_MODEL, F32)], [(1, 1)], name="loss")
    loss = loss_sum[0, 0] * (0.5 / D_MODEL)

    per_layer = {n: [None] * DEPTH for n in WEIGHTS if n.startswith(("ffn", "mix", "ple"))}
    per_mixer = {n: [None] * (DEPTH // 2) for n in WEIGHTS if n.startswith(("hyb", "gm", "ssd", "mla"))}
    for i in reversed(range(DEPTH)):
        lw = lws[i]
        s1, s2, s3, s4 = saved[i]
        j = i // 2
        dh, g = _ple_bwd(dh, s4, p[i], lw["ple"], _row(sm["ple_pre_g"][i]), _row(sm["ple_post_g"][i]))
        for k, v in g.items():
            per_layer["ple_" + k][i] = v
        dh, g = _ffn_bwd(dh, s3, lw["ffn2"], _row(sm["ffn2_pre_g"][i]), _row(sm["ffn2_post_g"][i]), "ffn")
        for k, v in g.items():
            per_layer["ffn2_" + k][i] = v
        if i % 2 == 0:
            dh, g = _hyb_bwd(dh, s2, lw["mix"], _row(sm["mix_pre_g"][i]), _row(sm["mix_post_g"][i]))
            names = dict(w_in="hyb_w_in", gm_ln_g="gm_ln_g", gm_ln_b="gm_ln_b", gm_w_s="gm_w_s", gm_b_s="gm_b_s",
                         conv_w="ssd_conv_w", conv_b="ssd_conv_b", dt_bias="ssd_dt_bias", a_log="ssd_a_log", d="ssd_d",
                         norm_g="ssd_norm_g", w_out="hyb_w_out")
        else:
            dh, g = _mla_bwd(dh, s2, lw["mix"], _row(sm["mix_pre_g"][i]), _row(sm["mix_post_g"][i]), rope, tq)
            g["w_in"] = g["w_in"][:, :MLA_IN]
            g["w_uq"] = g["w_uq"].reshape(MLA_Q_LORA, MLA_HEADS, 256)[:, :, :192].reshape(MLA_Q_LORA, MLA_HEADS * 192)
            names = dict(w_in="mla_w_in", q_norm_g="mla_q_norm_g", kv_norm_g="mla_kv_norm_g", w_uq="mla_w_uq",
                         w_ukv="mla_w_ukv", w_out="mla_w_out")
        per_layer["mix_pre_g"][i] = g.pop("pre_g")
        per_layer["mix_post_g"][i] = g.pop("post_g")
        for k, v in g.items():
            per_mixer[names[k]][j] = v
        dh, g = _ffn_bwd(dh, s1, lw["ffn1"], _row(sm["ffn1_pre_g"][i]), _row(sm["ffn1_post_g"][i]), "ffn")
        for k, v in g.items():
            per_layer["ffn1_" + k][i] = v

    return loss, dh, {**per_layer, **per_mixer}


def _stack_layers(parts, shape):
    return jnp.stack(parts, axis=0).reshape(shape)


MESH_AXES = ("x", "y", "c")
EXCHANGE_MAX_COPIES = 56


def _exchange(src, axes, mode, name):
    n = 2 ** len(axes)
    blk = src.shape[-2:]
    flips = [tuple(a for a, bit in zip(axes, np.binary_repr(f, len(axes))) if bit == "1") for f in range(1, n)]
    prefs = tuple(c for c in (16, 8, 4, 2, 1) if c * (n - 1) <= EXCHANGE_MAX_COPIES)
    pieces = _pick(blk[0] // 16, prefs) if blk[0] % 16 == 0 else 1
    rows = blk[0] // pieces

    def index(where):
        idx = 0
        for a in axes:
            idx = idx * 2 + where[a]
        return idx

    me_out = index({a: lax.axis_index(a) for a in MESH_AXES})
    own = lax.dynamic_index_in_dim(src, me_out, 0, keepdims=False) if mode == "a2a" else src
    landing = lax.dynamic_update_index_in_dim(lax.empty((n, *blk), src.dtype), own, me_out, 0)

    def body(src_ref, landing_ref, out_ref, send_sems, recv_sems):
        del landing_ref
        pos = {a: lax.axis_index(a) for a in MESH_AXES}
        me = index(pos)
        copies = []
        for k, flip in enumerate(flips):
            peer = {a: (1 - pos[a]) if a in flip else pos[a] for a in MESH_AXES}
            payload = src_ref.at[index(peer)] if mode == "a2a" else src_ref
            for q in range(pieces):
                part = pl.ds(q * rows, rows)
                cp = pltpu.make_async_remote_copy(
                    src_ref=payload.at[part], dst_ref=out_ref.at[me, part], send_sem=send_sems.at[k * pieces + q],
                    recv_sem=recv_sems.at[k * pieces + q], device_id=(peer["x"], peer["y"], peer["c"]),
                    device_id_type=pl.DeviceIdType.MESH)
                cp.start()
                copies.append(cp)
        for cp in copies:
            cp.wait()

    n_sems = (n - 1) * pieces
    return pl.pallas_call(
        body, name=name, in_specs=[pl.BlockSpec(memory_space=pl.ANY), pl.BlockSpec(memory_space=pl.ANY)],
        out_specs=pl.BlockSpec(memory_space=pl.ANY), out_shape=jax.ShapeDtypeStruct((n, *blk), src.dtype),
        input_output_aliases={1: 0},
        scratch_shapes=[pltpu.SemaphoreType.DMA((n_sems,)), pltpu.SemaphoreType.DMA((n_sems,))],
    )(src, landing)


def _pack_rows(n_elems):
    return -(-n_elems // (16 * PACK_W)) * 16


def _pack(parts, lead=()):
    nl = len(lead)
    rows = []
    for a in parts:
        flat = a.reshape(*lead, -1)
        r = _pack_rows(flat.shape[-1])
        flat = jnp.pad(flat, [(0, 0)] * nl + [(0, r * PACK_W - flat.shape[-1])])
        rows.append(flat.reshape(*lead, r, PACK_W))
    total = sum(r.shape[nl] for r in rows)
    pad = -total % PACK_TM
    if pad:
        rows.append(jnp.zeros((*lead, pad, PACK_W), rows[0].dtype))
    return jnp.concatenate(rows, axis=nl)


def _unpack(buf, shapes, lead=()):
    nl = len(lead)
    out, r0 = [], 0
    for shp in shapes:
        n = int(np.prod(shp))
        r = _pack_rows(n)
        piece = lax.slice_in_dim(buf, r0, r0 + r, axis=nl).reshape(*lead, r * PACK_W)
        out.append(lax.slice_in_dim(piece, 0, n, axis=nl).reshape(*lead, *shp))
        r0 += r
    return out


def _split_for_devices(g, axis):
    if isinstance(g, tuple):
        return jnp.concatenate([_split_for_devices_n(h, axis, N_DEV // len(g)) for h in g], axis=0)
    return _split_for_devices_n(g, axis, N_DEV)


def _split_for_devices_n(g, axis, n):
    shp = g.shape
    g = g.reshape(*shp[:axis], n, shp[axis] // n, *shp[axis + 1:])
    return jnp.moveaxis(g, axis, 0)


def _join_from_devices(parts, axis):
    parts = jnp.moveaxis(parts, 0, axis)
    shp = parts.shape
    return parts.reshape(*shp[:axis], shp[axis] * shp[axis + 1], *shp[axis + 2:])


def _adamw_terms(w, g, m, v):
    m = ADAM_B1 * m + (1.0 - ADAM_B1) * g
    v = ADAM_B2 * v + (1.0 - ADAM_B2) * (g * g)
    m_hat = m / (1.0 - ADAM_B1 ** ADAM_STEP)
    v_hat = v / (1.0 - ADAM_B2 ** ADAM_STEP)
    delta = -ADAM_LR * (m_hat / (jnp.sqrt(v_hat) + ADAM_EPS) + ADAM_WD * w)
    return delta, m, v


def _adamw_packed(w, m, v, partials, n_partials, name):
    def fn(w_, m_, v_, *parts):
        g = parts[0].astype(F32)
        for part in parts[1:]:
            g = g + part.astype(F32)
        return (g,) + _adamw_terms(w_, g, m_, v_)
    return _rowwise(fn, [w, m, v] + [(partials, s) for s in range(n_partials)], [], [(PACK_W, F32)] * 4,
                    tm=PACK_TM, name=name)


def kernel(x, p, positions, ffn1_pre_g, ffn1_w_in, ffn1_w_down, ffn1_post_g, mix_pre_g, mix_post_g, ffn2_pre_g, ffn2_w_in, ffn2_w_down, ffn2_post_g, ple_pre_g, ple_w_gate, ple_w_proj, ple_post_g, hyb_w_in, gm_ln_g, gm_ln_b, gm_w_s, gm_b_s, ssd_conv_w, ssd_conv_b, ssd_dt_bias, ssd_a_log, ssd_d, ssd_norm_g, hyb_w_out, mla_w_in, mla_q_norm_g, mla_kv_norm_g, mla_w_uq, mla_w_ukv, mla_w_out, loss_target, m_ffn1_pre_g, m_ffn1_w_in, m_ffn1_w_down, m_ffn1_post_g, m_mix_pre_g, m_mix_post_g, m_ffn2_pre_g, m_ffn2_w_in, m_ffn2_w_down, m_ffn2_post_g, m_ple_pre_g, m_ple_w_gate, m_ple_w_proj, m_ple_post_g, m_hyb_w_in, m_gm_ln_g, m_gm_ln_b, m_gm_w_s, m_gm_b_s, m_ssd_conv_w, m_ssd_conv_b, m_ssd_dt_bias, m_ssd_a_log, m_ssd_d, m_ssd_norm_g, m_hyb_w_out, m_mla_w_in, m_mla_q_norm_g, m_mla_kv_norm_g, m_mla_w_uq, m_mla_w_ukv, m_mla_w_out, v_ffn1_pre_g, v_ffn1_w_in, v_ffn1_w_down, v_ffn1_post_g, v_mix_pre_g, v_mix_post_g, v_ffn2_pre_g, v_ffn2_w_in, v_ffn2_w_down, v_ffn2_post_g, v_ple_pre_g, v_ple_w_gate, v_ple_w_proj, v_ple_post_g, v_hyb_w_in, v_gm_ln_g, v_gm_ln_b, v_gm_w_s, v_gm_b_s, v_ssd_conv_w, v_ssd_conv_b, v_ssd_dt_bias, v_ssd_a_log, v_ssd_d, v_ssd_norm_g, v_hyb_w_out, v_mla_w_in, v_mla_q_norm_g, v_mla_kv_norm_g, v_mla_w_uq, v_mla_w_ukv, v_mla_w_out):
    given = dict(locals())
    w = {n: given[n] for n in WEIGHTS}
    mom = {n: given["m_" + n] for n in WEIGHTS}
    var = {n: given["v_" + n] for n in WEIGHTS}
    shard_shapes = [w[n].shape for n in SHARDED]
    repl_shapes = [w[n].shape for n in REPLICATED]

    send16 = {n: w[n].astype(BF16) for n in SHARDED_BF16}
    send16["hyb_w_in"] = jnp.pad(send16["hyb_w_in"], ((0, 0), (0, 0), (0, HYB_SHARD_PAD - HYB_SHARD)))
    pack16 = _pack([send16[n] for n in SHARDED_BF16])
    by_chip = _exchange(pack16, ("x", "y"), "gather", "gather_weights_ici")
    by_core = _exchange(by_chip.reshape(-1, PACK_W), ("c",), "gather", "gather_weights_d2d")
    gathered = by_core.reshape(2, 4, -1, PACK_W).transpose(1, 0, 2, 3).reshape(N_DEV, -1, PACK_W)
    fw = {n: _join_from_devices(a, SHARD_AXIS[n])
          for n, a in zip(SHARDED_BF16, _unpack(gathered, [send16[n].shape for n in SHARDED_BF16], (N_DEV,)))}
    small = _exchange(_pack([w[n] for n in SHARDED_F32]), MESH_AXES, "gather", "gather_weights_f32")
    fw.update({n: _join_from_devices(a, SHARD_AXIS[n])
               for n, a in zip(SHARDED_F32, _unpack(small, [w[n].shape for n in SHARDED_F32], (N_DEV,)))})

    loss_local, grad_x, grads = _device_step(x[0], p[:, 0], positions[0], loss_target[0], fw, w)
    loss = lax.psum(loss_local, MESH_AXES)

    per_dev = []
    for n in SHARDED:
        layers = w[n].shape[0]
        whole = (1, *w[n].shape[1:SHARD_AXIS[n]], N_DEV * w[n].shape[SHARD_AXIS[n]], *w[n].shape[SHARD_AXIS[n] + 1:])
        if int(np.prod(w[n].shape[1:])) % (16 * PACK_W) == 0:
            parts = [tuple(h[None] for h in g) if isinstance(g, tuple) else g.reshape(whole) for g in grads[n]]
        else:
            parts = [_stack_layers(grads[n], (layers, *whole[1:]))]
        per_dev.extend(_split_for_devices(g, SHARD_AXIS[n]) for g in parts)
    per_dev = [a.reshape(4, 2, *a.shape[1:]).swapaxes(0, 1) for a in per_dev]
    gpack = _pack(per_dev, (2, 4))
    rows = gpack.shape[2]
    pair = _exchange(gpack.reshape(2, 4 * rows, PACK_W), ("c",), "a2a", "reduce_grads_d2d")
    chip_sum = _rowwise(lambda a, b: a + b, [(pair, 0), (pair, 1)], [], [(PACK_W, BF16)], tm=PACK_TM, name="reduce_grads_pair")
    quads = _exchange(chip_sum.reshape(4, rows, PACK_W), ("x", "y"), "a2a", "reduce_grads_ici")
    g_s, d_s, m_s, v_s = _adamw_packed(_pack([w[n] for n in SHARDED]), _pack([mom[n] for n in SHARDED]),
                                       _pack([var[n] for n in SHARDED]), quads, 4, "adamw_sharded")

    rpack = _pack([_stack_layers(grads[n], w[n].shape) for n in REPLICATED])
    everyone = _exchange(rpack, MESH_AXES, "gather", "gather_small_grads")
    g_r, d_r, m_r, v_r = _adamw_packed(_pack([w[n] for n in REPLICATED]), _pack([mom[n] for n in REPLICATED]),
                                       _pack([var[n] for n in REPLICATED]), everyone, N_DEV, "adamw_replicated")

    outs = []
    for sharded_buf, repl_buf in ((g_s, g_r), (d_s, d_r), (m_s, m_r), (v_s, v_r)):
        vals = dict(zip(SHARDED, _unpack(sharded_buf, shard_shapes)))
        vals.update(zip(REPLICATED, _unpack(repl_buf, repl_shapes)))
        outs.extend(vals[n] for n in WEIGHTS)
    return (loss, grad_x[None], *outs)
```

```python
import functools
import math

import jax
import jax.numpy as jnp
import numpy as np
from jax import lax
from jax.experimental import pallas as pl
from jax.experimental.pallas import tpu as pltpu

F32 = jnp.float32
BF16 = jnp.bfloat16
HIGHEST = lax.Precision.HIGHEST

V7X_VMEM_LIMIT_BYTES = 52 * 1024 * 1024
LANES = 128

D_MODEL = 1024
DEPTH = 4
D_FF = 2816
PLE_DIM = 256
NORM_EPS = 1e-6
LN_EPS = 1e-5
CHUNK = 128
GM_HEADS = 8
SSD_HEADS = 16
SSD_HEAD_DIM = 64
SSD_INNER = 1024
SSD_STATE = 128
SSD_BC = 256
SSD_CONV_CH = 1536
HYB_MAIN = 4608
MLA_HEADS = 16
MLA_Q_LORA = 256
MLA_KV_LORA = 128
MLA_ROPE = 64
MLA_IN = 448
MLA_IN_PAD = 512
ATTN_SCALE = 192.0 ** -0.5
LOG2_E = 1.4426950408889634
LN_2 = 0.6931471805599453
ATTN_QSCALE = ATTN_SCALE * LOG2_E
ROPE_BASE = 10000.0

ADAM_LR = 0.001
ADAM_B1 = 0.9
ADAM_B2 = 0.999
ADAM_EPS = 1e-08
ADAM_WD = 0.01
ADAM_STEP = 10

N_DEV = 8
PACK_W = 1024
PACK_TM = 256

WEIGHTS = ['ffn1_pre_g', 'ffn1_w_in', 'ffn1_w_down', 'ffn1_post_g', 'mix_pre_g', 'mix_post_g', 'ffn2_pre_g',
           'ffn2_w_in', 'ffn2_w_down', 'ffn2_post_g', 'ple_pre_g', 'ple_w_gate', 'ple_w_proj', 'ple_post_g',
           'hyb_w_in', 'gm_ln_g', 'gm_ln_b', 'gm_w_s', 'gm_b_s', 'ssd_conv_w', 'ssd_conv_b', 'ssd_dt_bias',
           'ssd_a_log', 'ssd_d', 'ssd_norm_g', 'hyb_w_out', 'mla_w_in', 'mla_q_norm_g', 'mla_kv_norm_g',
           'mla_w_uq', 'mla_w_ukv', 'mla_w_out']
SHARD_AXIS = {'ffn1_w_in': 2, 'ffn1_w_down': 1, 'ffn2_w_in': 2, 'ffn2_w_down': 1, 'ple_w_gate': 1, 'ple_w_proj': 2,
              'hyb_w_in': 2, 'ssd_conv_w': 2, 'hyb_w_out': 1, 'mla_w_in': 1, 'mla_q_norm_g': 1, 'mla_w_uq': 2,
              'mla_w_ukv': 2, 'mla_w_out': 1}
SHARDED = [n for n in WEIGHTS if n in SHARD_AXIS]
REPLICATED = [n for n in WEIGHTS if n not in SHARD_AXIS]
SHARDED_F32 = ['ssd_conv_w', 'mla_q_norm_g']
SHARDED_BF16 = [n for n in SHARDED if n not in SHARDED_F32]


def _params(*sem):
    return pltpu.CompilerParams(dimension_semantics=sem or None, vmem_limit_bytes=V7X_VMEM_LIMIT_BYTES)


def _pick(n, prefs):
    for t in prefs:
        if t <= n and n % t == 0:
            return t
    return n


def _mm(a, b, *, ta=False, tb=False, out_dtype=F32, tm=1024, tn=512, tk=512, name):
    m, k = (a.shape[1], a.shape[0]) if ta else a.shape
    n = b.shape[0] if tb else b.shape[1]
    assert k == (b.shape[1] if tb else b.shape[0]), (a.shape, b.shape, ta, tb)
    tm, tn, tk = _pick(m, (tm, 512, 256, 128)), _pick(n, (tn, 512, 256, 128)), _pick(k, (tk, 512, 256, 128))
    nk = k // tk
    dims = (((0 if ta else 1,), (1 if tb else 0,)), ((), ()))

    def body(a_ref, b_ref, o_ref, *acc):
        part = lax.dot_general(a_ref[...].astype(BF16), b_ref[...].astype(BF16), dims, preferred_element_type=F32)
        if nk == 1:
            o_ref[...] = part.astype(o_ref.dtype)
            return
        acc_ref, = acc
        kk = pl.program_id(2)

        @pl.when(kk == 0)
        def _():
            acc_ref[...] = part

        @pl.when(kk > 0)
        def _():
            acc_ref[...] += part

        @pl.when(kk == nk - 1)
        def _():
            o_ref[...] = acc_ref[...].astype(o_ref.dtype)

    a_spec = pl.BlockSpec((tk, tm), lambda i, j, kk: (kk, i)) if ta else pl.BlockSpec((tm, tk), lambda i, j, kk: (i, kk))
    b_spec = pl.BlockSpec((tn, tk), lambda i, j, kk: (j, kk)) if tb else pl.BlockSpec((tk, tn), lambda i, j, kk: (kk, j))
    return pl.pallas_call(
        body, name=name, grid=(m // tm, n // tn, nk), in_specs=[a_spec, b_spec],
        out_specs=pl.BlockSpec((tm, tn), lambda i, j, kk: (i, j)), out_shape=jax.ShapeDtypeStruct((m, n), out_dtype),
        scratch_shapes=[] if nk == 1 else [pltpu.VMEM((tm, tn), F32)],
        compiler_params=_params("parallel", "parallel", "arbitrary"),
    )(a, b)


def _rowwise(fn, rows, consts, outs, accs=(), *, tm=256, name):
    first = rows[0][0] if isinstance(rows[0], tuple) else rows[0]
    t = first.shape[-2]
    tm = _pick(t, (tm, 256, 128, 64, 32, 16, 8))
    n_r, n_c, n_o = len(rows), len(consts), len(outs)

    def body(*refs):
        vals = [r[...] for r in refs[:n_r + n_c]]
        res = fn(*vals)
        res = res if isinstance(res, tuple) else (res,)
        o_refs, a_refs = refs[n_r + n_c:n_r + n_c + n_o], refs[n_r + n_c + n_o:]
        for o_ref, v in zip(o_refs, res[:n_o]):
            if isinstance(v, (tuple, list)):
                off = 0
                for piece in v:
                    o_ref[:, off:off + piece.shape[1]] = piece.astype(o_ref.dtype)
                    off += piece.shape[1]
            else:
                o_ref[...] = v.astype(o_ref.dtype)
        if a_refs:
            terms = res[n_o:]
            is_first = pl.program_id(0) == 0

            @pl.when(is_first)
            def _():
                for a_ref, v in zip(a_refs, terms):
                    a_ref[...] = v

            @pl.when(jnp.logical_not(is_first))
            def _():
                for a_ref, v in zip(a_refs, terms):
                    a_ref[...] += v

    in_specs, args = [], []
    for r in rows:
        if isinstance(r, tuple) and len(r) == 3:
            arr, width, cb = r
            in_specs.append(pl.BlockSpec((tm, width), functools.partial(lambda i, c: (i, c), c=cb)))
        elif isinstance(r, tuple):
            arr, slot = r
            in_specs.append(pl.BlockSpec((None, tm, arr.shape[2]), functools.partial(lambda i, s: (s, i, 0), s=slot)))
        else:
            arr = r
            in_specs.append(pl.BlockSpec((tm, arr.shape[1]), lambda i: (i, 0)))
        args.append(arr)
    for c in consts:
        in_specs.append(pl.BlockSpec(c.shape, lambda i: (0, 0)))
        args.append(c)
    out_specs = [pl.BlockSpec((tm, c), lambda i: (i, 0)) for c, _ in outs]
    out_shape = [jax.ShapeDtypeStruct((t, c), dt) for c, dt in outs]
    for shp in accs:
        out_specs.append(pl.BlockSpec(shp, lambda i: (0, 0)))
        out_shape.append(jax.ShapeDtypeStruct(shp, F32))
    res = pl.pallas_call(
        body, name=name, grid=(t // tm,), in_specs=in_specs, out_specs=out_specs, out_shape=out_shape,
        compiler_params=_params("arbitrary" if accs else "parallel"),
    )(*args)
    return res[0] if len(res) == 1 else tuple(res)


def _colsum(v):
    return jnp.sum(v, axis=0, keepdims=True)


def _rms(x, g, eps=NORM_EPS):
    r = lax.rsqrt(jnp.mean(x * x, axis=-1, keepdims=True) + eps)
    return x * r * g


def _rms_bwd(x, g, dy, eps=NORM_EPS):
    r = lax.rsqrt(jnp.mean(x * x, axis=-1, keepdims=True) + eps)
    xh = x * r
    dyg = dy * g
    dx = r * (dyg - xh * jnp.mean(dyg * xh, axis=-1, keepdims=True))
    return dx, dy * xh


def _silu(x):
    return x * jax.nn.sigmoid(x)


def _silu_grad(x):
    s = jax.nn.sigmoid(x)
    return s * (1.0 + x * (1.0 - s))


_GELU_K = math.sqrt(2.0 / math.pi)


def _gelu(x):
    return 0.5 * x * (1.0 + jnp.tanh(_GELU_K * (x + 0.044715 * x * x * x)))


def _gelu_grad(x):
    t = jnp.tanh(_GELU_K * (x + 0.044715 * x * x * x))
    return 0.5 * (1.0 + t) + 0.5 * x * (1.0 - t * t) * _GELU_K * (1.0 + 3.0 * 0.044715 * x * x)


def _prenorm(h, g, name):
    return _rowwise(lambda x, gg: _rms(x, gg), [h], [g], [(D_MODEL, BF16)], name=name)


def _postnorm_residual(h, f, g, scale, next_g, name):
    if next_g is None:
        return _rowwise(lambda x, ff, gg: x + scale * _rms(ff, gg), [h, f], [g], [(D_MODEL, F32)], name=name), None

    def fn(x, ff, gg, ng):
        out = x + scale * _rms(ff, gg)
        return out, _rms(out, ng)
    return _rowwise(fn, [h, f], [g, next_g], [(D_MODEL, F32), (D_MODEL, BF16)], name=name + "_prenorm")


def _postnorm_bwd(f, dh, g, scale, name):
    def fn(ff, d, gg):
        dx, dgt = _rms_bwd(ff, gg, scale * d)
        return dx, _colsum(dgt)
    return _rowwise(fn, [f, dh], [g], [(D_MODEL, BF16)], [(1, D_MODEL)], name=name)


def _prenorm_bwd(h, das, dh, g, name):
    n = len(das)

    def fn(x, *rest):
        da = rest[0]
        for extra in rest[1:n]:
            da = da + extra
        d, gg = rest[n], rest[n + 1]
        dx, dgt = _rms_bwd(x, gg, da)
        return d + dx, _colsum(dgt)
    return _rowwise(fn, [h, *das, dh], [g], [(D_MODEL, F32)], [(1, D_MODEL)], name=name)


FFN_TM = 1024
FFN_TN = 256


def _ffn_in_swiglu(a, w_in, name):
    t = a.shape[0]
    tm = _pick(t, (FFN_TM, 512, 256, 128))
    nj = D_FF // FFN_TN

    def body(a_ref, wg_ref, wu_ref, gate_ref, up_ref, s_ref):
        av = a_ref[...]
        gate = jnp.dot(av, wg_ref[...], preferred_element_type=F32)
        up = jnp.dot(av, wu_ref[...], preferred_element_type=F32)
        gate_ref[...] = gate
        up_ref[...] = up
        s_ref[...] = (_silu(gate) * up).astype(s_ref.dtype)

    tile = pl.BlockSpec((tm, FFN_TN), lambda i, j: (i, j))
    return pl.pallas_call(
        body, name=name, grid=(t // tm, nj),
        in_specs=[pl.BlockSpec((tm, D_MODEL), lambda i, j: (i, 0)), pl.BlockSpec((D_MODEL, FFN_TN), lambda i, j: (0, j)),
                  pl.BlockSpec((D_MODEL, FFN_TN), lambda i, j: (0, j + nj))],
        out_specs=[tile, tile, tile],
        out_shape=[jax.ShapeDtypeStruct((t, D_FF), F32), jax.ShapeDtypeStruct((t, D_FF), F32),
                   jax.ShapeDtypeStruct((t, D_FF), BF16)],
        compiler_params=_params("parallel", "parallel"),
    )(a, w_in, w_in)


def _ffn_down_dx_swiglu(df, w_down, gate, up, name):
    t = df.shape[0]
    tm = _pick(t, (FFN_TM, 512, 256, 128))

    def body(df_ref, wd_ref, gate_ref, up_ref, dgate_ref, dup_ref):
        ds = lax.dot_general(df_ref[...], wd_ref[...], (((1,), (1,)), ((), ())), preferred_element_type=F32)
        gate = gate_ref[...]
        sg = jax.nn.sigmoid(gate)
        dgate_ref[...] = (ds * up_ref[...] * (sg * (1.0 + gate * (1.0 - sg)))).astype(dgate_ref.dtype)
        dup_ref[...] = (ds * (gate * sg)).astype(dup_ref.dtype)

    tile = pl.BlockSpec((tm, FFN_TN), lambda i, j: (i, j))
    return pl.pallas_call(
        body, name=name, grid=(t // tm, D_FF // FFN_TN),
        in_specs=[pl.BlockSpec((tm, D_MODEL), lambda i, j: (i, 0)), pl.BlockSpec((FFN_TN, D_MODEL), lambda i, j: (j, 0)),
                  tile, tile],
        out_specs=[tile, tile],
        out_shape=[jax.ShapeDtypeStruct((t, D_FF), BF16), jax.ShapeDtypeStruct((t, D_FF), BF16)],
        compiler_params=_params("parallel", "parallel"),
    )(df, w_down, gate, up)


def _ffn_fwd(h, a, w, pre_g, post_g, next_g, tag):
    if a is None:
        a = _prenorm(h, pre_g, tag + "_prenorm")
    gate, up, s = _ffn_in_swiglu(a, w["in"], tag + "_in_swiglu")
    f = _mm(s, w["down"], tm=1024, tn=1024, tk=D_FF, name=tag + "_down")
    out, a_next = _postnorm_residual(h, f, post_g, 0.5, next_g, tag + "_postnorm")
    return out, a_next, (h, a, gate, up, s, f)


FFN_DX_TM = 512
FFN_DX_TK = 1408


def _ffn_in_dx_prenorm_bwd(dgate, dup, w_in, h, dh, pre_g, name):
    t = h.shape[0]
    tm = _pick(t, (FFN_DX_TM, 256, 128))
    half = D_FF // FFN_DX_TK
    nk = 2 * half

    def body(dgate_ref, dup_ref, w_ref, h_ref, dh_ref, g_ref, out_ref, dg_ref, acc_ref):
        i, kk = pl.program_id(0), pl.program_id(1)
        nt = (((1,), (1,)), ((), ()))

        @pl.when(kk == 0)
        def _():
            acc_ref[...] = lax.dot_general(dgate_ref[...], w_ref[...], nt, preferred_element_type=F32)

        @pl.when(jnp.logical_and(kk > 0, kk < half))
        def _():
            acc_ref[...] += lax.dot_general(dgate_ref[...], w_ref[...], nt, preferred_element_type=F32)

        @pl.when(kk >= half)
        def _():
            acc_ref[...] += lax.dot_general(dup_ref[...], w_ref[...], nt, preferred_element_type=F32)

        @pl.when(kk == nk - 1)
        def _():
            dx, dgt = _rms_bwd(h_ref[...], g_ref[...], acc_ref[...])
            out_ref[...] = dh_ref[...] + dx
            dg = _colsum(dgt)

            @pl.when(i == 0)
            def _():
                dg_ref[...] = dg

            @pl.when(i > 0)
            def _():
                dg_ref[...] += dg

    row = pl.BlockSpec((tm, D_MODEL), lambda i, kk: (i, 0))
    return pl.pallas_call(
        body, name=name, grid=(t // tm, nk),
        in_specs=[pl.BlockSpec((tm, FFN_DX_TK), lambda i, kk: (i, jnp.minimum(kk, half - 1))),
                  pl.BlockSpec((tm, FFN_DX_TK), lambda i, kk: (i, jnp.maximum(kk - half, 0))),
                  pl.BlockSpec((D_MODEL, FFN_DX_TK), lambda i, kk: (0, kk)), row, row,
                  pl.BlockSpec((1, D_MODEL), lambda i, kk: (0, 0))],
        out_specs=[row, pl.BlockSpec((1, D_MODEL), lambda i, kk: (0, 0))],
        out_shape=[jax.ShapeDtypeStruct((t, D_MODEL), F32), jax.ShapeDtypeStruct((1, D_MODEL), F32)],
        scratch_shapes=[pltpu.VMEM((tm, D_MODEL), F32)],
        compiler_params=_params("arbitrary", "arbitrary"),
    )(dgate, dup, w_in, h, dh, pre_g)


def _ffn_bwd(dh, saved, w, pre_g, post_g, tag):
    h, a, gate, up, s, f = saved
    df, d_post = _postnorm_bwd(f, dh, post_g, 0.5, tag + "_postnorm_bwd")
    dgate, dup = _ffn_down_dx_swiglu(df, w["down"], gate, up, tag + "_down_dx_swiglu")
    d_down = _mm(s, df, ta=True, tm=1408, tn=1024, tk=1024, name=tag + "_down_dw")
    d_in = (_mm(a, dgate, ta=True, tm=1024, tn=1408, tk=1024, name=tag + "_in_dw_gate"),
            _mm(a, dup, ta=True, tm=1024, tn=1408, tk=1024, name=tag + "_in_dw_up"))
    dh_in, d_pre = _ffn_in_dx_prenorm_bwd(dgate, dup, w["in"], h, dh, pre_g, tag + "_in_dx_prenorm_bwd")
    return dh_in, dict(w_in=d_in, w_down=d_down, pre_g=d_pre, post_g=d_post)


def _ple_fwd(h, a, p_i, w, post_g, next_g):
    gl = _mm(a, w["gate"], tm=1024, tn=1024, tk=1024, name="ple_gate")
    e = _mm(p_i, w["proj"], tm=1024, tn=1024, tk=PLE_DIM, name="ple_proj")
    if next_g is None:
        out = _rowwise(lambda x, g_, e_, gg: x + _rms(jax.nn.sigmoid(g_) * e_, gg), [h, gl, e], [post_g],
                       [(D_MODEL, F32)], name="ple_out")
        return out, None, (h, a, gl, e)

    def fn(x, g_, e_, gg, ng):
        out = x + _rms(jax.nn.sigmoid(g_) * e_, gg)
        return out, _rms(out, ng)
    out, a_next = _rowwise(fn, [h, gl, e], [post_g, next_g], [(D_MODEL, F32), (D_MODEL, BF16)], name="ple_out_prenorm")
    return out, a_next, (h, a, gl, e)


def _ple_bwd(dh, saved, p_i, w, pre_g, post_g):
    h, a, gl, e = saved

    def fn(g_, e_, d, gg):
        sg = jax.nn.sigmoid(g_)
        du, dgt = _rms_bwd(sg * e_, gg, d)
        return du * e_ * sg * (1.0 - sg), du * sg, _colsum(dgt)
    dgl, de, d_post = _rowwise(fn, [gl, e, dh], [post_g], [(D_MODEL, BF16), (D_MODEL, BF16)], [(1, D_MODEL)],
                               name="ple_out_bwd")
    da = _mm(dgl, w["gate"], tb=True, tm=1024, tn=1024, tk=1024, name="ple_gate_dx")
    d_gate = _mm(a, dgl, ta=True, tm=1024, tn=1024, tk=2048, name="ple_gate_dw")
    d_proj = _mm(p_i, de, ta=True, tm=PLE_DIM, tn=1024, tk=2048, name="ple_proj_dw")
    dh_in, d_pre = _prenorm_bwd(h, [da], dh, pre_g, "ple_prenorm_bwd")
    return dh_in, dict(w_gate=d_gate, w_proj=d_proj, pre_g=d_pre, post_g=d_post)


def _gm_layernorm(v, g, b):
    mu = jnp.mean(v, axis=-1, keepdims=True)
    xc = v - mu
    rstd = lax.rsqrt(jnp.mean(xc * xc, axis=-1, keepdims=True) + LN_EPS)
    vhat = xc * rstd
    return vhat, rstd, vhat * g + b


def _gmlp_fwd(proj, wm, bias_t, ln_g, ln_b, name):
    t = proj.shape[0]

    def body(uv_ref, wm_ref, bt_ref, g_ref, b_ref, o_ref):
        for hd in range(GM_HEADS):
            lo = hd * LANES
            u = _gelu(uv_ref[:, lo:lo + LANES])
            v = _gelu(uv_ref[:, 1024 + lo:1024 + lo + LANES])
            _, _, vln = _gm_layernorm(v, g_ref[:, lo:lo + LANES], b_ref[:, lo:lo + LANES])
            mixed = jnp.dot(wm_ref[hd], vln.astype(BF16), preferred_element_type=F32) + bt_ref[:, hd:hd + 1]
            o_ref[:, lo:lo + LANES] = (u * mixed).astype(o_ref.dtype)

    return pl.pallas_call(
        body, name=name, grid=(t // CHUNK,),
        in_specs=[pl.BlockSpec((CHUNK, 2048), lambda i: (i, 0)), pl.BlockSpec(wm.shape, lambda i: (0, 0, 0)),
                  pl.BlockSpec(bias_t.shape, lambda i: (0, 0)), pl.BlockSpec(ln_g.shape, lambda i: (0, 0)),
                  pl.BlockSpec(ln_b.shape, lambda i: (0, 0))],
        out_specs=pl.BlockSpec((CHUNK, 1024), lambda i: (i, 0)), out_shape=jax.ShapeDtypeStruct((t, 1024), BF16),
        compiler_params=_params("parallel"),
    )(proj, wm, bias_t, ln_g, ln_b)


def _gmlp_bwd(proj, dyab, wm, bias_t, ln_g, ln_b, name):
    t = proj.shape[0]
    nc = t // CHUNK

    def body(uv_ref, dy_ref, wm_ref, bt_ref, g_ref, b_ref, duv_ref, dw_ref, db_ref, dg_ref, dbeta_ref, dbacc):
        c = pl.program_id(0)

        @pl.when(c == 0)
        def _():
            dw_ref[...] = jnp.zeros_like(dw_ref)
            dbacc[...] = jnp.zeros_like(dbacc)
            dg_ref[...] = jnp.zeros_like(dg_ref)
            dbeta_ref[...] = jnp.zeros_like(dbeta_ref)

        for hd in range(GM_HEADS):
            lo = hd * LANES
            xu = uv_ref[:, lo:lo + LANES]
            xv = uv_ref[:, 1024 + lo:1024 + lo + LANES]
            u = _gelu(xu)
            g_h = g_ref[:, lo:lo + LANES]
            vhat, rstd, vln = _gm_layernorm(_gelu(xv), g_h, b_ref[:, lo:lo + LANES])
            vln16 = vln.astype(BF16)
            mixed = jnp.dot(wm_ref[hd], vln16, preferred_element_type=F32) + bt_ref[:, hd:hd + 1]
            dy = dy_ref[:, lo:lo + LANES]
            du = dy * mixed
            dmix = dy * u
            dmix16 = dmix.astype(BF16)
            dw_ref[hd] += lax.dot_general(dmix16, vln16, (((1,), (1,)), ((), ())), preferred_element_type=F32)
            dbacc[hd] += dmix
            dvln = lax.dot_general(wm_ref[hd], dmix16, (((0,), (0,)), ((), ())), preferred_element_type=F32)
            dg_ref[:, lo:lo + LANES] += _colsum(dvln * vhat)
            dbeta_ref[:, lo:lo + LANES] += _colsum(dvln)
            dvh = dvln * g_h
            dv = rstd * (dvh - jnp.mean(dvh, axis=-1, keepdims=True)
                         - vhat * jnp.mean(dvh * vhat, axis=-1, keepdims=True))
            duv_ref[:, lo:lo + LANES] = (du * _gelu_grad(xu)).astype(duv_ref.dtype)
            duv_ref[:, 1024 + lo:1024 + lo + LANES] = (dv * _gelu_grad(xv)).astype(duv_ref.dtype)

        @pl.when(c == nc - 1)
        def _():
            row = lax.broadcasted_iota(jnp.int32, (CHUNK, CHUNK), 0)
            col = lax.broadcasted_iota(jnp.int32, (CHUNK, CHUNK), 1)
            for hd in range(GM_HEADS):
                dw_ref[hd] = jnp.where(col <= row, dw_ref[hd], 0.0)
                db_ref[hd] = jnp.sum(dbacc[hd], axis=1, keepdims=True)

    return pl.pallas_call(
        body, name=name, grid=(nc,),
        in_specs=[pl.BlockSpec((CHUNK, 2048), lambda i: (i, 0)), pl.BlockSpec((CHUNK, 1024), lambda i: (i, 0)),
                  pl.BlockSpec(wm.shape, lambda i: (0, 0, 0)), pl.BlockSpec(bias_t.shape, lambda i: (0, 0)),
                  pl.BlockSpec(ln_g.shape, lambda i: (0, 0)), pl.BlockSpec(ln_b.shape, lambda i: (0, 0))],
        out_specs=[pl.BlockSpec((CHUNK, 2048), lambda i: (i, 0)), pl.BlockSpec((GM_HEADS, CHUNK, CHUNK), lambda i: (0, 0, 0)),
                   pl.BlockSpec((GM_HEADS, CHUNK, 1), lambda i: (0, 0, 0)), pl.BlockSpec((1, 1024), lambda i: (0, 0)),
                   pl.BlockSpec((1, 1024), lambda i: (0, 0))],
        out_shape=[jax.ShapeDtypeStruct((t, 2048), BF16), jax.ShapeDtypeStruct((GM_HEADS, CHUNK, CHUNK), F32),
                   jax.ShapeDtypeStruct((GM_HEADS, CHUNK, 1), F32), jax.ShapeDtypeStruct((1, 1024), F32),
                   jax.ShapeDtypeStruct((1, 1024), F32)],
        scratch_shapes=[pltpu.VMEM((GM_HEADS, CHUNK, CHUNK), F32)],
        compiler_params=_params("arbitrary"),
    )(proj, dyab, wm, bias_t, ln_g, ln_b)


def _ssd_chunk_terms(dt_pad, a_pad):
    row = lax.broadcasted_iota(jnp.int32, (CHUNK, CHUNK), 0)
    col = lax.broadcasted_iota(jnp.int32, (CHUNK, CHUNK), 1)
    tril = jnp.where(col <= row, 1.0, 0.0).astype(F32)
    a_cs = jnp.dot(tril, dt_pad * a_pad, precision=HIGHEST, preferred_element_type=F32)
    return a_cs, a_cs.T


def _pair_cols(mat, hd_a, lane_lt64):
    return jnp.where(lane_lt64, mat[:, hd_a:hd_a + 1], mat[:, hd_a + 1:hd_a + 2])


def _head_decay(a_cs, a_cs_t, hd, causal):
    seg = a_cs[:, hd:hd + 1] - a_cs_t[hd:hd + 1, :]
    return jnp.exp(jnp.where(causal, seg, -jnp.inf))


def _ssd_fwd(act, dt_pad, a_pad, d_pad, name):
    t = act.shape[0]
    nc = t // CHUNK

    def body(act_ref, dt_ref, a_ref, d_ref, y_ref, st_ref, h_sc):
        c = pl.program_id(0)

        @pl.when(c == 0)
        def _():
            h_sc[...] = jnp.zeros_like(h_sc)

        st_ref[...] = h_sc[...]
        row = lax.broadcasted_iota(jnp.int32, (CHUNK, CHUNK), 0)
        col = lax.broadcasted_iota(jnp.int32, (CHUNK, CHUNK), 1)
        causal = col <= row
        lane_lt64 = lax.broadcasted_iota(jnp.int32, (CHUNK, LANES), 1) < SSD_HEAD_DIM
        row_lt64 = lax.broadcasted_iota(jnp.int32, (LANES, 1), 0) < SSD_HEAD_DIM
        dt = dt_ref[...]
        a_cs, a_cs_t = _ssd_chunk_terms(dt, a_ref[...])
        last = a_cs[CHUNK - 1:CHUNK, :]
        for g in range(2):
            b16 = act_ref[:, SSD_INNER + g * SSD_STATE:SSD_INNER + (g + 1) * SSD_STATE].astype(BF16)
            c16 = act_ref[:, SSD_INNER + SSD_BC + g * SSD_STATE:SSD_INNER + SSD_BC + (g + 1) * SSD_STATE].astype(BF16)
            cb = lax.dot_general(c16, b16, (((1,), (1,)), ((), ())), preferred_element_type=F32)
            for pr in range(4):
                ha = g * 8 + pr * 2
                lo = ha * SSD_HEAD_DIM
                xs = act_ref[:, lo:lo + LANES]
                xd = xs * _pair_cols(dt, ha, lane_lt64)
                xd16 = xd.astype(BF16)
                ya = jnp.dot((cb * _head_decay(a_cs, a_cs_t, ha, causal)).astype(BF16), xd16, preferred_element_type=F32)
                yb = jnp.dot((cb * _head_decay(a_cs, a_cs_t, ha + 1, causal)).astype(BF16), xd16, preferred_element_type=F32)
                a_p = _pair_cols(a_cs, ha, lane_lt64)
                hp = h_sc[lo:lo + LANES, :]
                y_off = lax.dot_general(c16, hp.astype(BF16), (((1,), (1,)), ((), ())), preferred_element_type=F32)
                d_p = jnp.where(lane_lt64[:1], d_ref[:, ha:ha + 1], d_ref[:, ha + 1:ha + 2])
                y_ref[:, lo:lo + LANES] = jnp.where(lane_lt64, ya, yb) + y_off * jnp.exp(a_p) + d_p * xs
                last_p = jnp.where(lane_lt64[:1], last[:, ha:ha + 1], last[:, ha + 1:ha + 2])
                xw16 = (xd * jnp.exp(last_p - a_p)).astype(BF16)
                s_new = lax.dot_general(xw16, b16, (((0,), (0,)), ((), ())), preferred_element_type=F32)
                t_col = jnp.where(row_lt64, jnp.exp(last[:, ha:ha + 1]), jnp.exp(last[:, ha + 1:ha + 2]))
                h_sc[lo:lo + LANES, :] = t_col * hp + s_new

    return pl.pallas_call(
        body, name=name, grid=(nc,),
        in_specs=[pl.BlockSpec((CHUNK, SSD_CONV_CH), lambda i: (i, 0)), pl.BlockSpec((CHUNK, LANES), lambda i: (i, 0)),
                  pl.BlockSpec((1, LANES), lambda i: (0, 0)), pl.BlockSpec((1, LANES), lambda i: (0, 0))],
        out_specs=[pl.BlockSpec((CHUNK, SSD_INNER), lambda i: (i, 0)),
                   pl.BlockSpec((None, SSD_INNER, SSD_STATE), lambda i: (i, 0, 0))],
        out_shape=[jax.ShapeDtypeStruct((t, SSD_INNER), F32), jax.ShapeDtypeStruct((nc, SSD_INNER, SSD_STATE), F32)],
        scratch_shapes=[pltpu.VMEM((SSD_INNER, SSD_STATE), F32)],
        compiler_params=_params("arbitrary"),
    )(act, dt_pad, a_pad, d_pad)


def _ssd_bwd(act, dt_pad, a_pad, d_pad, states, dy, name):
    t = act.shape[0]
    nc = t // CHUNK

    def body(act_ref, dt_ref, a_ref, d_ref, st_ref, dy_ref, dact_ref, ddt_ref, da_ref, dd_ref, dh_sc):
        c = pl.program_id(0)

        @pl.when(c == 0)
        def _():
            dh_sc[...] = jnp.zeros_like(dh_sc)
            da_ref[...] = jnp.zeros_like(da_ref)
            dd_ref[...] = jnp.zeros_like(dd_ref)

        row = lax.broadcasted_iota(jnp.int32, (CHUNK, CHUNK), 0)
        col = lax.broadcasted_iota(jnp.int32, (CHUNK, CHUNK), 1)
        causal = col <= row
        lane = lax.broadcasted_iota(jnp.int32, (CHUNK, LANES), 1)
        lane_lt64 = lane < SSD_HEAD_DIM
        row_lt64 = lax.broadcasted_iota(jnp.int32, (LANES, 1), 0) < SSD_HEAD_DIM
        is_last = lax.broadcasted_iota(jnp.int32, (CHUNK, 1), 0) == CHUNK - 1
        dt = dt_ref[...]
        a_cs, a_cs_t = _ssd_chunk_terms(dt, a_ref[...])
        last = a_cs[CHUNK - 1:CHUNK, :]
        d_acs = jnp.zeros((CHUNK, LANES), F32)
        d_acs_rows = jnp.zeros((LANES, CHUNK), F32)
        head_row = lax.broadcasted_iota(jnp.int32, (LANES, CHUNK), 0)
        ddt_x = jnp.zeros((CHUNK, LANES), F32)
        dd_acc = jnp.zeros((1, LANES), F32)

        def head_sum(v, first):
            return jnp.sum(jnp.where(lane_lt64 if first else jnp.logical_not(lane_lt64), v, 0.0), axis=1, keepdims=True)

        for g in range(2):
            b_lo = SSD_INNER + g * SSD_STATE
            c_lo = SSD_INNER + SSD_BC + g * SSD_STATE
            b16 = act_ref[:, b_lo:b_lo + SSD_STATE].astype(BF16)
            c16 = act_ref[:, c_lo:c_lo + SSD_STATE].astype(BF16)
            cb = lax.dot_general(c16, b16, (((1,), (1,)), ((), ())), preferred_element_type=F32)
            dcb = jnp.zeros((CHUNK, CHUNK), F32)
            db_g = jnp.zeros((CHUNK, SSD_STATE), F32)
            dc_g = jnp.zeros((CHUNK, SSD_STATE), F32)
            for pr in range(4):
                ha = g * 8 + pr * 2
                lo = ha * SSD_HEAD_DIM
                xs = act_ref[:, lo:lo + LANES]
                dt_p = _pair_cols(dt, ha, lane_lt64)
                xd = xs * dt_p
                xd16 = xd.astype(BF16)
                a_p = _pair_cols(a_cs, ha, lane_lt64)
                exp_a = jnp.exp(a_p)
                last_p = jnp.where(lane_lt64[:1], last[:, ha:ha + 1], last[:, ha + 1:ha + 2])
                w_p = jnp.exp(last_p - a_p)
                hp = st_ref[lo:lo + LANES, :]
                hp16 = hp.astype(BF16)
                dhn = dh_sc[lo:lo + LANES, :]
                dhn16 = dhn.astype(BF16)
                dyp = dy_ref[:, lo:lo + LANES]
                d_p = jnp.where(lane_lt64[:1], d_ref[:, ha:ha + 1], d_ref[:, ha + 1:ha + 2])
                dd_row = _colsum(dyp * xs)
                dd_acc = dd_acc + jnp.where(lane[:1] == ha, jnp.sum(jnp.where(lane_lt64[:1], dd_row, 0.0), axis=1, keepdims=True), 0.0) \
                    + jnp.where(lane[:1] == ha + 1, jnp.sum(jnp.where(lane_lt64[:1], 0.0, dd_row), axis=1, keepdims=True), 0.0)
                g_off = lax.dot_general(c16, hp16, (((1,), (1,)), ((), ())), preferred_element_type=F32)
                dg16 = (dyp * exp_a).astype(BF16)
                dc_g = dc_g + jnp.dot(dg16, hp16, preferred_element_type=F32)
                dh_prev = lax.dot_general(dg16, c16, (((0,), (0,)), ((), ())), preferred_element_type=F32)
                off_term = dyp * g_off * exp_a
                q = lax.dot_general(b16, dhn16, (((1,), (1,)), ((), ())), preferred_element_type=F32)
                xw16 = (xd * w_p).astype(BF16)
                db_g = db_g + jnp.dot(xw16, dhn16, preferred_element_type=F32)
                dw_term = xd * q * w_p
                dxd = w_p * q
                dt_all = dhn * hp
                dyp16 = dyp.astype(BF16)
                for k, first in ((0, True), (1, False)):
                    hd = ha + k
                    sel = lane_lt64 if first else jnp.logical_not(lane_lt64)
                    decay = _head_decay(a_cs, a_cs_t, hd, causal)
                    m = cb * decay
                    dy_h = jnp.where(sel, dyp16, jnp.zeros_like(dyp16))
                    dm = lax.dot_general(dy_h, xd16, (((1,), (1,)), ((), ())), preferred_element_type=F32)
                    dcb = dcb + dm * decay
                    dseg = dm * m
                    dxd = dxd + jnp.where(sel, lax.dot_general(m.astype(BF16), dyp16, (((0,), (0,)), ((), ())),
                                                               preferred_element_type=F32), 0.0)
                    d_col = jnp.sum(dseg, axis=1, keepdims=True)
                    d_acs_rows = d_acs_rows + jnp.where(head_row == hd, jnp.sum(dseg, axis=0, keepdims=True), 0.0)
                    dw_col = head_sum(dw_term, first)
                    d_col = d_col + head_sum(off_term, first) - dw_col
                    t_h = jnp.exp(last[:, hd:hd + 1])
                    dt_sum = jnp.sum(jnp.sum(jnp.where(row_lt64 if first else jnp.logical_not(row_lt64), dt_all, 0.0),
                                             axis=0, keepdims=True), axis=1, keepdims=True)
                    end_term = jnp.sum(dw_col, axis=0, keepdims=True) + dt_sum * t_h
                    d_col = d_col + jnp.where(is_last, end_term, 0.0)
                    d_acs = d_acs + jnp.where(lane == hd, d_col, 0.0)
                t_col = jnp.where(row_lt64, jnp.exp(last[:, ha:ha + 1]), jnp.exp(last[:, ha + 1:ha + 2]))
                dh_sc[lo:lo + LANES, :] = t_col * dhn + dh_prev
                dact_ref[:, lo:lo + LANES] = d_p * dyp + dxd * dt_p
                ddt_all = dxd * xs
                ddt_x = ddt_x + jnp.where(lane == ha, head_sum(ddt_all, True), 0.0) \
                    + jnp.where(lane == ha + 1, head_sum(ddt_all, False), 0.0)
            dcb16 = dcb.astype(BF16)
            dact_ref[:, b_lo:b_lo + SSD_STATE] = db_g + lax.dot_general(dcb16, c16, (((0,), (0,)), ((), ())),
                                                                          preferred_element_type=F32)
            dact_ref[:, c_lo:c_lo + SSD_STATE] = dc_g + jnp.dot(dcb16, b16, preferred_element_type=F32)
        triu = jnp.where(col >= row, 1.0, 0.0).astype(F32)
        dda = jnp.dot(triu, d_acs - d_acs_rows.T, precision=HIGHEST, preferred_element_type=F32)
        ddt_ref[...] = dda * a_ref[...] + ddt_x
        da_ref[...] += _colsum(dda * dt)
        dd_ref[...] += dd_acc

    rev = lambda i: (nc - 1 - i, 0)
    return pl.pallas_call(
        body, name=name, grid=(nc,),
        in_specs=[pl.BlockSpec((CHUNK, SSD_CONV_CH), rev), pl.BlockSpec((CHUNK, LANES), rev),
                  pl.BlockSpec((1, LANES), lambda i: (0, 0)), pl.BlockSpec((1, LANES), lambda i: (0, 0)),
                  pl.BlockSpec((None, SSD_INNER, SSD_STATE), lambda i: (nc - 1 - i, 0, 0)),
                  pl.BlockSpec((CHUNK, SSD_INNER), rev)],
        out_specs=[pl.BlockSpec((CHUNK, SSD_CONV_CH), rev), pl.BlockSpec((CHUNK, LANES), rev),
                   pl.BlockSpec((1, LANES), lambda i: (0, 0)), pl.BlockSpec((1, LANES), lambda i: (0, 0))],
        out_shape=[jax.ShapeDtypeStruct((t, SSD_CONV_CH), F32), jax.ShapeDtypeStruct((t, LANES), F32),
                   jax.ShapeDtypeStruct((1, LANES), F32), jax.ShapeDtypeStruct((1, LANES), F32)],
        scratch_shapes=[pltpu.VMEM((SSD_INNER, SSD_STATE), F32)],
        compiler_params=_params("arbitrary"),
    )(act, dt_pad, a_pad, d_pad, states, dy)


def _shift_down(x, k):
    return x if k == 0 else jnp.pad(x, ((k, 0), (0, 0)))[:x.shape[0]]


def _shift_up(x, k):
    return x if k == 0 else jnp.pad(x, ((0, k), (0, 0)))[k:]


def _conv_pre(x0, x1, x2, x3, w, b):
    return x0 * w[0:1] + x1 * w[1:2] + x2 * w[2:3] + x3 * w[3:4] + b


def _rope128(x, cpad, s_lo, s_hi):
    return x * cpad + pltpu.roll(x, 96, 1) * s_lo + pltpu.roll(x, 32, 1) * s_hi


ATTN_ROW_SPLIT = 4
ATTN_ROW_SPLIT_DKV = 4


def _diag_mask(rows, cols, row0):
    return lax.broadcasted_iota(jnp.int32, (rows, cols), 1) <= row0 + lax.broadcasted_iota(jnp.int32, (rows, cols), 0)


def _attn_scores(q, k):
    return lax.dot_general(q, k, (((1,), (1,)), ((), ())), preferred_element_type=F32)


def _causal_pairs(nq, by_key):
    if by_key:
        pairs = [(i, j) for j in range(nq) for i in range(j, nq)]
    else:
        pairs = [(i, j) for i in range(nq) for j in range(i + 1)]
    return (jnp.asarray([pr[0] for pr in pairs], jnp.int32), jnp.asarray([pr[1] for pr in pairs], jnp.int32))


def _attn_fwd(qf, kf, kvf, *, tq, name):
    t = qf.shape[0]
    nq = t // tq
    tk = tq
    qi, kj = _causal_pairs(nq, by_key=False)
    rs = tq // ATTN_ROW_SPLIT

    def body(qi_ref, kj_ref, q_ref, k_ref, v_ref, o_ref, lse_ref, m_sc, acc_sc, v1_sc):
        pp = pl.program_id(1)
        i, j = qi_ref[pp], kj_ref[pp]

        @pl.when(pp == 0)
        def _():
            v1_sc[:, LANES:] = jnp.ones((tk, LANES), BF16)

        @pl.when(j == 0)
        def _():
            m_sc[...] = jnp.full_like(m_sc, -jnp.inf)
            acc_sc[...] = jnp.zeros_like(acc_sc)

        v1_sc[:, :LANES] = v_ref[...]

        def update(diag):
            for r in range(ATTN_ROW_SPLIT):
                rows = slice(r * rs, (r + 1) * rs)
                keys = slice(0, (r + 1) * rs if diag else tk)
                s = _attn_scores(q_ref[rows, :], k_ref[keys, :])
                if diag:
                    s = jnp.where(_diag_mask(rs, keys.stop, r * rs), s, -jnp.inf)
                m_prev = m_sc[rows, :]
                m_new = jnp.maximum(m_prev, jnp.max(s, axis=1, keepdims=True))
                p = jnp.exp2(s - m_new).astype(BF16)
                alpha = jnp.exp2(m_prev - m_new)
                acc = alpha * acc_sc[rows, :] + jnp.dot(p, v1_sc[keys, :], preferred_element_type=F32)
                if diag:
                    o_ref[rows, :] = (acc[:, :LANES] / acc[:, LANES:]).astype(o_ref.dtype)
                    lse_ref[rows, :] = m_new + jnp.log2(acc[:, LANES:LANES + 1])
                else:
                    acc_sc[rows, :] = acc
                    m_sc[rows, :] = m_new

        @pl.when(j < i)
        def _():
            update(False)

        @pl.when(j == i)
        def _():
            update(True)

    return pl.pallas_call(
        body, name=name,
        grid_spec=pltpu.PrefetchScalarGridSpec(
            num_scalar_prefetch=2, grid=(MLA_HEADS, int(qi.shape[0])),
            in_specs=[pl.BlockSpec((tq, 2 * LANES), lambda h, pp, qi_, kj_: (qi_[pp], h)),
                      pl.BlockSpec((tk, 2 * LANES), lambda h, pp, qi_, kj_: (kj_[pp], h)),
                      pl.BlockSpec((tk, LANES), lambda h, pp, qi_, kj_: (kj_[pp], 2 * h + 1))],
            out_specs=[pl.BlockSpec((tq, LANES), lambda h, pp, qi_, kj_: (qi_[pp], h)),
                       pl.BlockSpec((None, tq, 1), lambda h, pp, qi_, kj_: (h, qi_[pp], 0))],
            scratch_shapes=[pltpu.VMEM((tq, 1), F32), pltpu.VMEM((tq, 2 * LANES), F32), pltpu.VMEM((tk, 2 * LANES), BF16)]),
        out_shape=[jax.ShapeDtypeStruct((t, MLA_HEADS * LANES), BF16), jax.ShapeDtypeStruct((MLA_HEADS, t, 1), F32)],
        compiler_params=_params("arbitrary", "arbitrary"),
    )(qi, kj, qf, kf, kvf)


def _attn_bwd(qf, kf, kvf, o, do, lse, *, tq, name):
    t = qf.shape[0]
    nq = t // tq
    tk = tq
    qi, kj = _causal_pairs(nq, by_key=True)
    rs = tq // ATTN_ROW_SPLIT_DKV

    def body(qi_ref, kj_ref, q_ref, k_ref, v_ref, o_ref, do_ref, lse_ref, dkv_ref, dkr_ref, dq_ref, dk_sc, dv_sc):
        pp = pl.program_id(1)
        i, j = qi_ref[pp], kj_ref[pp]
        tn = (((0,), (0,)), ((), ()))

        @pl.when(pp == 0)
        def _():
            dq_ref[...] = jnp.zeros_like(dq_ref)

        def update(diag):
            if diag:
                dv_sc[...] = jnp.zeros_like(dv_sc)
                dk_sc[...] = jnp.zeros_like(dk_sc)
            for r in range(ATTN_ROW_SPLIT_DKV):
                rows = slice(r * rs, (r + 1) * rs)
                keys = slice(0, (r + 1) * rs if diag else tk)
                do_ = do_ref[rows, :]
                delta = jnp.sum(do_.astype(F32) * o_ref[rows, :].astype(F32), axis=1, keepdims=True)
                s = _attn_scores(q_ref[rows, :], k_ref[keys, :])
                p = jnp.exp2(s - lse_ref[rows, :])
                if diag:
                    p = jnp.where(_diag_mask(rs, keys.stop, r * rs), p, 0.0)
                dp = lax.dot_general(do_, v_ref[keys, :], (((1,), (1,)), ((), ())), preferred_element_type=F32)
                ds = (p * (dp - delta)).astype(BF16)
                dv_sc[keys, :] += lax.dot_general(p.astype(BF16), do_, tn, preferred_element_type=F32)
                dk_sc[keys, :] += lax.dot_general(ds, q_ref[rows, :], tn, preferred_element_type=F32)
                q_rows = pl.ds(pl.multiple_of(i * tq + r * rs, rs), rs)
                dq_ref[q_rows, :] += jnp.dot(ds, k_ref[keys, :], preferred_element_type=F32)

        @pl.when(i > j)
        def _():
            update(False)

        @pl.when(i == j)
        def _():
            update(True)

        @pl.when(i == nq - 1)
        def _():
            dkv_ref[:, :LANES] = (dk_sc[:, :LANES] * LN_2).astype(dkv_ref.dtype)
            dkv_ref[:, LANES:] = dv_sc[...].astype(dkv_ref.dtype)
            dkr_ref[...] = dk_sc[:, LANES:] * LN_2

    qblk = lambda c: (lambda h, pp, qi_, kj_: (qi_[pp], c(h)))
    kblk = lambda c: (lambda h, pp, qi_, kj_: (kj_[pp], c(h)))
    return pl.pallas_call(
        body, name=name,
        grid_spec=pltpu.PrefetchScalarGridSpec(
            num_scalar_prefetch=2, grid=(MLA_HEADS, int(qi.shape[0])),
            in_specs=[pl.BlockSpec((tq, 2 * LANES), qblk(lambda h: h)), pl.BlockSpec((tk, 2 * LANES), kblk(lambda h: h)),
                      pl.BlockSpec((tk, LANES), kblk(lambda h: 2 * h + 1)),
                      pl.BlockSpec((tq, LANES), qblk(lambda h: h)), pl.BlockSpec((tq, LANES), qblk(lambda h: h)),
                      pl.BlockSpec((None, tq, 1), lambda h, pp, qi_, kj_: (h, qi_[pp], 0))],
            out_specs=[pl.BlockSpec((tk, 2 * LANES), kblk(lambda h: h)), pl.BlockSpec((tk, LANES), kblk(lambda h: h)),
                       pl.BlockSpec((t, 2 * LANES), lambda h, pp, qi_, kj_: (0, h))],
            scratch_shapes=[pltpu.VMEM((tk, 2 * LANES), F32), pltpu.VMEM((tk, LANES), F32)]),
        out_shape=[jax.ShapeDtypeStruct((t, MLA_HEADS * 2 * LANES), BF16), jax.ShapeDtypeStruct((t, MLA_HEADS * LANES), F32),
                   jax.ShapeDtypeStruct((t, MLA_HEADS * 2 * LANES), F32)],
        compiler_params=_params("parallel", "arbitrary"),
    )(qi, kj, qf, kf, kvf, o, do, lse)


def _rope_tables(positions):
    t = positions.shape[0]
    inv = 1.0 / (ROPE_BASE ** (jnp.arange(0, MLA_ROPE, 2, dtype=F32) / MLA_ROPE))
    ang = positions.astype(F32)[:, None] * inv
    cos, sin = jnp.cos(ang), jnp.sin(ang)
    z32, z64 = jnp.zeros((t, 32), F32), jnp.zeros((t, 64), F32)
    cpad = jnp.concatenate([cos, cos, z64], axis=1)
    s_lo = jnp.concatenate([-sin, z32, z64], axis=1)
    s_hi = jnp.concatenate([z32, sin, z64], axis=1)
    return cpad, s_lo, s_hi


def _mla_fwd(h, hn, w, post_g, next_g, rope, tq):
    cpad, s_lo, s_hi = rope
    cin = _mm(hn, w["in"], tm=1024, tn=512, tk=1024, name="mla_in")

    def lat(c, cp, sl, sh, qg, kvg):
        cq, ckv, kr = c[:, :MLA_Q_LORA], c[:, MLA_Q_LORA:MLA_Q_LORA + MLA_KV_LORA], c[:, MLA_Q_LORA + MLA_KV_LORA:]
        return _rms(cq, qg), _rms(ckv, kvg), _rope128(kr, cp, sl, sh)
    cqn, ckvn, kr = _rowwise(lat, [cin, cpad, s_lo, s_hi], [w["q_norm_g"], w["kv_norm_g"]],
                             [(MLA_Q_LORA, BF16), (MLA_KV_LORA, BF16), (LANES, BF16)], name="mla_latent")
    q_raw = _mm(cqn, w["uq"], tm=1024, tn=1024, tk=MLA_Q_LORA, name="mla_uq")

    def rope_q(q, cp, sl, sh):
        pieces = []
        for hd in range(MLA_HEADS):
            pieces.append(q[:, 256 * hd:256 * hd + LANES] * ATTN_QSCALE)
            pieces.append(_rope128(q[:, 256 * hd + LANES:256 * hd + 256], cp, sl, sh) * ATTN_QSCALE)
        return (tuple(pieces),)
    qf = _rowwise(rope_q, [q_raw, cpad, s_lo, s_hi], [], [(4096, BF16)], name="mla_rope_q")
    kvf = _mm(ckvn, w["ukv"], out_dtype=BF16, tm=1024, tn=1024, tk=MLA_KV_LORA, name="mla_ukv")
    t = h.shape[0]
    k_nope = kvf.reshape(t, MLA_HEADS, 2 * LANES)[:, :, :LANES]
    kf = jnp.concatenate([k_nope, jnp.broadcast_to(kr[:, None, :], k_nope.shape)], axis=2).reshape(t, MLA_HEADS * 2 * LANES)
    o, lse = _attn_fwd(qf, kf, kvf, tq=tq, name="mla_attn")
    mixed = _mm(o, w["out"], tm=1024, tn=1024, tk=2048, name="mla_out")
    out, a_next = _postnorm_residual(h, mixed, post_g, 1.0, next_g, "mla_postnorm")
    return out, a_next, (h, hn, cin, cqn, ckvn, qf, kf, kvf, o, lse, mixed)


def _mla_bwd(dh, saved, w, pre_g, post_g, rope, tq):
    cpad, s_lo, s_hi = rope
    h, hn, cin, cqn, ckvn, qf, kf, kvf, o, lse, mixed = saved
    dmixed, d_post = _postnorm_bwd(mixed, dh, post_g, 1.0, "mla_postnorm_bwd")
    do = _mm(dmixed, w["out"], tb=True, out_dtype=BF16, tm=1024, tn=1024, tk=1024, name="mla_out_dx")
    d_out = _mm(o, dmixed, ta=True, tm=1024, tn=1024, tk=2048, name="mla_out_dw")
    dkvf, dkr_heads, dq = _attn_bwd(qf, kf, kvf, o, do, lse, tq=tq, name="mla_attn_bwd")

    def unrope_q(d, cp, sl, sh):
        pieces = []
        for hd in range(MLA_HEADS):
            pieces.append(d[:, 256 * hd:256 * hd + LANES] * ATTN_SCALE)
            pieces.append(_rope128(d[:, 256 * hd + LANES:256 * hd + 256], cp, -sl, -sh) * ATTN_SCALE)
        return (tuple(pieces),)
    dq_raw = _rowwise(unrope_q, [dq, cpad, s_lo, s_hi], [], [(4096, BF16)], name="mla_rope_q_bwd")
    dcqn = _mm(dq_raw, w["uq"], tb=True, tm=1024, tn=256, tk=1024, name="mla_uq_dx")
    d_uq = _mm(cqn, dq_raw, ta=True, tm=256, tn=1024, tk=2048, name="mla_uq_dw")
    dckvn = _mm(dkvf, w["ukv"], tb=True, tm=1024, tn=128, tk=1024, name="mla_ukv_dx")
    d_ukv = _mm(ckvn, dkvf, ta=True, tm=128, tn=1024, tk=2048, name="mla_ukv_dw")

    def lat_bwd(c, dq_, dkv_, dkrh, cp, sl, sh, qg, kvg):
        cq, ckv = c[:, :MLA_Q_LORA], c[:, MLA_Q_LORA:MLA_Q_LORA + MLA_KV_LORA]
        dcq, dqg = _rms_bwd(cq, qg, dq_)
        dckv, dkvg = _rms_bwd(ckv, kvg, dkv_)
        dkr = dkrh[:, :LANES]
        for hd in range(1, MLA_HEADS):
            dkr = dkr + dkrh[:, hd * LANES:(hd + 1) * LANES]
        return (dcq, dckv, _rope128(dkr, cp, -sl, -sh)), _colsum(dqg), _colsum(dkvg)
    dcin, d_qg, d_kvg = _rowwise(lat_bwd, [cin, dcqn, dckvn, dkr_heads, cpad, s_lo, s_hi], [w["q_norm_g"], w["kv_norm_g"]],
                                 [(MLA_IN_PAD, BF16)], [(1, MLA_Q_LORA), (1, MLA_KV_LORA)], name="mla_latent_bwd")
    dhn = _mm(dcin, w["in"], tb=True, tm=1024, tn=1024, tk=512, name="mla_in_dx")
    d_in = _mm(hn, dcin, ta=True, tm=1024, tn=512, tk=2048, name="mla_in_dw")
    dh_in, d_pre = _prenorm_bwd(h, [dhn], dh, pre_g, "mla_prenorm_bwd")
    return dh_in, dict(w_in=d_in, q_norm_g=d_qg, kv_norm_g=d_kvg, w_uq=d_uq, w_ukv=d_ukv, w_out=d_out,
                       pre_g=d_pre, post_g=d_post)


def _hyb_fwd(h, hn, w, post_g, next_g):
    proj = _mm(hn, w["main"], tm=1024, tn=512, tk=1024, name="hyb_in")
    dtr = _mm(hn, w["dt"], tm=1024, tn=LANES, tk=1024, name="hyb_in_dt")
    ya = _gmlp_fwd(proj, w["gm_w"], w["gm_bt"], w["gm_ln_g"], w["gm_ln_b"], "gmlp")
    xbc = proj[:, 3072:]
    xsh = [_shift_down(xbc, 3 - k) for k in range(3)] + [(proj, SSD_CONV_CH, 2)]
    act = _rowwise(lambda x0, x1, x2, x3, cw, cb: _silu(_conv_pre(x0, x1, x2, x3, cw, cb)), xsh,
                   [w["conv_w"], w["conv_b"]], [(SSD_CONV_CH, F32)], name="ssd_conv")
    dt_pad = _rowwise(lambda d, b: jax.nn.softplus(d + b), [dtr], [w["dt_bias"]], [(LANES, F32)], name="ssd_dt")
    y, states = _ssd_fwd(act, dt_pad, w["a"], w["d"], "ssd_scan")

    def gate_norm(y_, z, ng):
        yg = y_ * _silu(z)
        return ((_rms(yg[:, :512], ng[:, :512]), _rms(yg[:, 512:], ng[:, 512:])),)
    yb = _rowwise(gate_norm, [y, (proj, SSD_INNER, 2)], [w["norm_g"]], [(SSD_INNER, BF16)], name="ssd_gate_norm")
    yab = jnp.concatenate([ya, yb], axis=1)
    mixed = _mm(yab, w["out"], tm=1024, tn=1024, tk=2048, name="hyb_out")
    out, a_next = _postnorm_residual(h, mixed, post_g, 1.0, next_g, "hyb_postnorm")
    return out, a_next, (h, hn, proj, dtr, xsh, act, dt_pad, y, states, yab, mixed)


def _hyb_bwd(dh, saved, w, pre_g, post_g):
    h, hn, proj, dtr, xsh, act, dt_pad, y, states, yab, mixed = saved
    dmixed, d_post = _postnorm_bwd(mixed, dh, post_g, 1.0, "hyb_postnorm_bwd")
    dyab = _mm(dmixed, w["out"], tb=True, tm=1024, tn=1024, tk=1024, name="hyb_out_dx")
    d_out = _mm(yab, dmixed, ta=True, tm=1024, tn=1024, tk=2048, name="hyb_out_dw")

    def gate_norm_bwd(y_, z, d, ng):
        sz = _silu(z)
        yg = y_ * sz
        d_lo, g_lo = _rms_bwd(yg[:, :512], ng[:, :512], d[:, 1024:1536])
        d_hi, g_hi = _rms_bwd(yg[:, 512:], ng[:, 512:], d[:, 1536:])
        dyg = jnp.concatenate([d_lo, d_hi], axis=1)
        return dyg * sz, dyg * y_ * _silu_grad(z), _colsum(jnp.concatenate([g_lo, g_hi], axis=1))
    dy, dz, d_norm = _rowwise(gate_norm_bwd, [y, (proj, SSD_INNER, 2), dyab], [w["norm_g"]], [(SSD_INNER, F32), (SSD_INNER, BF16)],
                              [(1, SSD_INNER)], name="ssd_gate_norm_bwd")
    dact, ddt, da_sum, dd_sum = _ssd_bwd(act, dt_pad, w["a"], w["d"], states, dy, "ssd_scan_bwd")

    def conv_bwd(x0, x1, x2, x3, da_, cw, cb):
        dpre = da_ * _silu_grad(_conv_pre(x0, x1, x2, x3, cw, cb))
        dw = jnp.concatenate([_colsum(dpre * x0), _colsum(dpre * x1), _colsum(dpre * x2), _colsum(dpre * x3)], axis=0)
        return dpre, dw, _colsum(dpre)
    dconv, d_conv_w, d_conv_b = _rowwise(conv_bwd, [*xsh, dact], [w["conv_w"], w["conv_b"]], [(SSD_CONV_CH, F32)],
                                         [(4, SSD_CONV_CH), (1, SSD_CONV_CH)], name="ssd_conv_bwd")
    dsh = [_shift_up(dconv, 3 - k) for k in range(4)]
    dxbc = _rowwise(lambda d0, d1, d2, d3, cw: d0 * cw[0:1] + d1 * cw[1:2] + d2 * cw[2:3] + d3 * cw[3:4], dsh,
                    [w["conv_w"]], [(SSD_CONV_CH, BF16)], name="ssd_conv_dx")

    def dt_bwd(dd, d, b):
        g = dd * jax.nn.sigmoid(d + b)
        g = jnp.where(lax.broadcasted_iota(jnp.int32, g.shape, 1) < SSD_HEADS, g, 0.0)
        return g, _colsum(g)
    ddtr, d_dt_bias = _rowwise(dt_bwd, [ddt, dtr], [w["dt_bias"]], [(LANES, BF16)], [(1, LANES)], name="ssd_dt_bwd")
    duv, d_gm_w, d_gm_b, d_ln_g, d_ln_b = _gmlp_bwd(proj, dyab, w["gm_w"], w["gm_bt"], w["gm_ln_g"], w["gm_ln_b"],
                                                    "gmlp_bwd")
    dproj = jnp.concatenate([duv, dz, dxbc], axis=1)
    dhn_a = _mm(dproj, w["main"], tb=True, tm=1024, tn=1024, tk=1536, name="hyb_in_dx")
    dhn_b = _mm(ddtr, w["dt"], tb=True, tm=1024, tn=1024, tk=LANES, name="hyb_in_dt_dx")
    d_main = _mm(hn, dproj, ta=True, tm=1024, tn=512, tk=2048, name="hyb_in_dw")
    d_dt = _mm(hn, ddtr, ta=True, tm=1024, tn=LANES, tk=2048, name="hyb_in_dt_dw")
    dh_in, d_pre = _prenorm_bwd(h, [dhn_a, dhn_b], dh, pre_g, "hyb_prenorm_bwd")
    grads = dict(w_in=jnp.concatenate([d_main, d_dt[:, :SSD_HEADS]], axis=1), gm_ln_g=d_ln_g, gm_ln_b=d_ln_b,
                 gm_w_s=d_gm_w, gm_b_s=d_gm_b[:, :, 0], conv_w=d_conv_w, conv_b=d_conv_b,
                 dt_bias=d_dt_bias[:, :SSD_HEADS], a_log=(da_sum * w["a"])[:, :SSD_HEADS], d=dd_sum[:, :SSD_HEADS],
                 norm_g=d_norm, w_out=d_out, pre_g=d_pre, post_g=d_post)
    return dh_in, grads


def _row(v):
    return v.reshape(1, -1).astype(F32)


def _pad_lanes(v, n=LANES):
    v = _row(v)
    return jnp.pad(v, ((0, 0), (0, n - v.shape[1])))


HYB_IN = 4624
HYB_SHARD = HYB_IN // N_DEV
HYB_SHARD_PAD = 640


def _hyb_unblock_matrix():
    n = N_DEV * HYB_SHARD_PAD
    r = lax.broadcasted_iota(jnp.int32, (n, n), 0)
    c = lax.broadcasted_iota(jnp.int32, (n, n), 1)
    j = r % HYB_SHARD_PAD
    return jnp.logical_and(j < HYB_SHARD, c == HYB_SHARD * (r // HYB_SHARD_PAD) + j).astype(BF16)


def _layer_weights(fw, sm, i):
    j = i // 2
    lw = dict(
        ffn1=dict({"in": fw["ffn1_w_in"][i], "down": fw["ffn1_w_down"][i]}),
        ffn2=dict({"in": fw["ffn2_w_in"][i], "down": fw["ffn2_w_down"][i]}),
        ple=dict(gate=fw["ple_w_gate"][i], proj=fw["ple_w_proj"][i]),
    )
    if i % 2 == 0:
        w_in = _mm(fw["hyb_w_in"][j], _hyb_unblock_matrix(), out_dtype=BF16, tm=1024, tn=512, tk=1024, name="hyb_w_unblock")
        causal = jnp.tril(jnp.ones((CHUNK, CHUNK), dtype=bool))
        lw["mix"] = {
            "main": w_in[:, :HYB_MAIN], "dt": w_in[:, HYB_MAIN:HYB_MAIN + LANES],
            "gm_w": jnp.where(causal[None], sm["gm_w_s"][j], 0.0).astype(BF16),
            "gm_bt": jnp.pad(sm["gm_b_s"][j].T, ((0, 0), (0, LANES - GM_HEADS))),
            "gm_ln_g": _row(sm["gm_ln_g"][j]), "gm_ln_b": _row(sm["gm_ln_b"][j]),
            "conv_w": fw["ssd_conv_w"][j], "conv_b": _row(sm["ssd_conv_b"][j]),
            "dt_bias": _pad_lanes(sm["ssd_dt_bias"][j]), "a": _pad_lanes(-jnp.exp(sm["ssd_a_log"][j])),
            "d": _pad_lanes(sm["ssd_d"][j]), "norm_g": _row(sm["ssd_norm_g"][j]), "out": fw["hyb_w_out"][j],
        }
    else:
        uq = fw["mla_w_uq"][j].reshape(MLA_Q_LORA, MLA_HEADS, 192)
        uq = jnp.pad(uq, ((0, 0), (0, 0), (0, 64))).reshape(MLA_Q_LORA, MLA_HEADS * 256)
        lw["mix"] = {
            "in": jnp.pad(fw["mla_w_in"][j], ((0, 0), (0, MLA_IN_PAD - MLA_IN))), "uq": uq, "ukv": fw["mla_w_ukv"][j],
            "out": fw["mla_w_out"][j], "q_norm_g": _row(fw["mla_q_norm_g"][j]), "kv_norm_g": _row(sm["mla_kv_norm_g"][j]),
        }
    return lw


def _device_step(x, p, positions, target, fw, sm):
    t = x.shape[0]
    tq = _pick(t, (1024, 512, 256, 128))
    rope = _rope_tables(positions)
    h = x
    saved, lws = [], []
    a = None
    for i in range(DEPTH):
        lw = _layer_weights(fw, sm, i)
        lws.append(lw)
        after = _row(sm["ffn1_pre_g"][i + 1]) if i + 1 < DEPTH else None
        h, a, s1 = _ffn_fwd(h, a, lw["ffn1"], _row(sm["ffn1_pre_g"][i]), _row(sm["ffn1_post_g"][i]),
                            _row(sm["mix_pre_g"][i]), "ffn")
        if i % 2 == 0:
            h, a, s2 = _hyb_fwd(h, a, lw["mix"], _row(sm["mix_post_g"][i]), _row(sm["ffn2_pre_g"][i]))
        else:
            h, a, s2 = _mla_fwd(h, a, lw["mix"], _row(sm["mix_post_g"][i]), _row(sm["ffn2_pre_g"][i]), rope, tq)
        h, a, s3 = _ffn_fwd(h, a, lw["ffn2"], _row(sm["ffn2_pre_g"][i]), _row(sm["ffn2_post_g"][i]),
                            _row(sm["ple_pre_g"][i]), "ffn")
        h, a, s4 = _ple_fwd(h, a, p[i], lw["ple"], _row(sm["ple_post_g"][i]), after)
        saved.append((s1, s2, s3, s4))

    def loss_fn(y, tg):
        err = y - tg
        return err * (1.0 / D_MODEL), jnp.sum(_colsum(err * err), axis=1, keepdims=True)
    dh, loss_sum = _rowwise(loss_fn, [h, target], [], [(D_MODEL, F32)], [(1, 1)], name="loss")
    loss = loss_sum[0, 0] * (0.5 / D_MODEL)

    per_layer = {n: [None] * DEPTH for n in WEIGHTS if n.startswith(("ffn", "mix", "ple"))}
    per_mixer = {n: [None] * (DEPTH // 2) for n in WEIGHTS if n.startswith(("hyb", "gm", "ssd", "mla"))}
    for i in reversed(range(DEPTH)):
        lw = lws[i]
        s1, s2, s3, s4 = saved[i]
        j = i // 2
        dh, g = _ple_bwd(dh, s4, p[i], lw["ple"], _row(sm["ple_pre_g"][i]), _row(sm["ple_post_g"][i]))
        for k, v in g.items():
            per_layer["ple_" + k][i] = v
        dh, g = _ffn_bwd(dh, s3, lw["ffn2"], _row(sm["ffn2_pre_g"][i]), _row(sm["ffn2_post_g"][i]), "ffn")
        for k, v in g.items():
            per_layer["ffn2_" + k][i] = v
        if i % 2 == 0:
            dh, g = _hyb_bwd(dh, s2, lw["mix"], _row(sm["mix_pre_g"][i]), _row(sm["mix_post_g"][i]))
            names = dict(w_in="hyb_w_in", gm_ln_g="gm_ln_g", gm_ln_b="gm_ln_b", gm_w_s="gm_w_s", gm_b_s="gm_b_s",
                         conv_w="ssd_conv_w", conv_b="ssd_conv_b", dt_bias="ssd_dt_bias", a_log="ssd_a_log", d="ssd_d",
                         norm_g="ssd_norm_g", w_out="hyb_w_out")
        else:
            dh, g = _mla_bwd(dh, s2, lw["mix"], _row(sm["mix_pre_g"][i]), _row(sm["mix_post_g"][i]), rope, tq)
            g["w_in"] = g["w_in"][:, :MLA_IN]
            g["w_uq"] = g["w_uq"].reshape(MLA_Q_LORA, MLA_HEADS, 256)[:, :, :192].reshape(MLA_Q_LORA, MLA_HEADS * 192)
            names = dict(w_in="mla_w_in", q_norm_g="mla_q_norm_g", kv_norm_g="mla_kv_norm_g", w_uq="mla_w_uq",
                         w_ukv="mla_w_ukv", w_out="mla_w_out")
        per_layer["mix_pre_g"][i] = g.pop("pre_g")
        per_layer["mix_post_g"][i] = g.pop("post_g")
        for k, v in g.items():
            per_mixer[names[k]][j] = v
        dh, g = _ffn_bwd(dh, s1, lw["ffn1"], _row(sm["ffn1_pre_g"][i]), _row(sm["ffn1_post_g"][i]), "ffn")
        for k, v in g.items():
            per_layer["ffn1_" + k][i] = v

    return loss, dh, {**per_layer, **per_mixer}


def _stack_layers(parts, shape):
    return jnp.stack(parts, axis=0).reshape(shape)


MESH_AXES = ("x", "y", "c")
EXCHANGE_MAX_COPIES = 56


def _exchange(src, axes, mode, name):
    n = 2 ** len(axes)
    blk = src.shape[-2:]
    flips = [tuple(a for a, bit in zip(axes, np.binary_repr(f, len(axes))) if bit == "1") for f in range(1, n)]
    prefs = tuple(c for c in (16, 8, 4, 2, 1) if c * (n - 1) <= EXCHANGE_MAX_COPIES)
    pieces = _pick(blk[0] // 16, prefs) if blk[0] % 16 == 0 else 1
    rows = blk[0] // pieces

    def index(where):
        idx = 0
        for a in axes:
            idx = idx * 2 + where[a]
        return idx

    me_out = index({a: lax.axis_index(a) for a in MESH_AXES})
    own = lax.dynamic_index_in_dim(src, me_out, 0, keepdims=False) if mode == "a2a" else src
    landing = lax.dynamic_update_index_in_dim(lax.empty((n, *blk), src.dtype), own, me_out, 0)

    def body(src_ref, landing_ref, out_ref, send_sems, recv_sems):
        del landing_ref
        pos = {a: lax.axis_index(a) for a in MESH_AXES}
        me = index(pos)
        copies = []
        for k, flip in enumerate(flips):
            peer = {a: (1 - pos[a]) if a in flip else pos[a] for a in MESH_AXES}
            payload = src_ref.at[index(peer)] if mode == "a2a" else src_ref
            for q in range(pieces):
                part = pl.ds(q * rows, rows)
                cp = pltpu.make_async_remote_copy(
                    src_ref=payload.at[part], dst_ref=out_ref.at[me, part], send_sem=send_sems.at[k * pieces + q],
                    recv_sem=recv_sems.at[k * pieces + q], device_id=(peer["x"], peer["y"], peer["c"]),
                    device_id_type=pl.DeviceIdType.MESH)
                cp.start()
                copies.append(cp)
        for cp in copies:
            cp.wait()

    n_sems = (n - 1) * pieces
    return pl.pallas_call(
        body, name=name, in_specs=[pl.BlockSpec(memory_space=pl.ANY), pl.BlockSpec(memory_space=pl.ANY)],
        out_specs=pl.BlockSpec(memory_space=pl.ANY), out_shape=jax.ShapeDtypeStruct((n, *blk), src.dtype),
        input_output_aliases={1: 0},
        scratch_shapes=[pltpu.SemaphoreType.DMA((n_sems,)), pltpu.SemaphoreType.DMA((n_sems,))],
    )(src, landing)


def _pack_rows(n_elems):
    return -(-n_elems // (16 * PACK_W)) * 16


def _pack(parts, lead=()):
    nl = len(lead)
    rows = []
    for a in parts:
        flat = a.reshape(*lead, -1)
        r = _pack_rows(flat.shape[-1])
        flat = jnp.pad(flat, [(0, 0)] * nl + [(0, r * PACK_W - flat.shape[-1])])
        rows.append(flat.reshape(*lead, r, PACK_W))
    total = sum(r.shape[nl] for r in rows)
    pad = -total % PACK_TM
    if pad:
        rows.append(jnp.zeros((*lead, pad, PACK_W), rows[0].dtype))
    return jnp.concatenate(rows, axis=nl)


def _unpack(buf, shapes, lead=()):
    nl = len(lead)
    out, r0 = [], 0
    for shp in shapes:
        n = int(np.prod(shp))
        r = _pack_rows(n)
        piece = lax.slice_in_dim(buf, r0, r0 + r, axis=nl).reshape(*lead, r * PACK_W)
        out.append(lax.slice_in_dim(piece, 0, n, axis=nl).reshape(*lead, *shp))
        r0 += r
    return out


def _split_for_devices(g, axis):
    if isinstance(g, tuple):
        return jnp.concatenate([_split_for_devices_n(h, axis, N_DEV // len(g)) for h in g], axis=0)
    return _split_for_devices_n(g, axis, N_DEV)


def _split_for_devices_n(g, axis, n):
    shp = g.shape
    g = g.reshape(*shp[:axis], n, shp[axis] // n, *shp[axis + 1:])
    return jnp.moveaxis(g, axis, 0)


def _join_from_devices(parts, axis):
    parts = jnp.moveaxis(parts, 0, axis)
    shp = parts.shape
    return parts.reshape(*shp[:axis], shp[axis] * shp[axis + 1], *shp[axis + 2:])


def _adamw_terms(w, g, m, v):
    m = ADAM_B1 * m + (1.0 - ADAM_B1) * g
    v = ADAM_B2 * v + (1.0 - ADAM_B2) * (g * g)
    m_hat = m / (1.0 - ADAM_B1 ** ADAM_STEP)
    v_hat = v / (1.0 - ADAM_B2 ** ADAM_STEP)
    delta = -ADAM_LR * (m_hat / (jnp.sqrt(v_hat) + ADAM_EPS) + ADAM_WD * w)
    return delta, m, v


def _adamw_packed(w, m, v, partials, n_partials, name):
    def fn(w_, m_, v_, *parts):
        g = parts[0].astype(F32)
        for part in parts[1:]:
            g = g + part.astype(F32)
        return (g,) + _adamw_terms(w_, g, m_, v_)
    return _rowwise(fn, [w, m, v] + [(partials, s) for s in range(n_partials)], [], [(PACK_W, F32)] * 4,
                    tm=PACK_TM, name=name)


def kernel(x, p, positions, ffn1_pre_g, ffn1_w_in, ffn1_w_down, ffn1_post_g, mix_pre_g, mix_post_g, ffn2_pre_g, ffn2_w_in, ffn2_w_down, ffn2_post_g, ple_pre_g, ple_w_gate, ple_w_proj, ple_post_g, hyb_w_in, gm_ln_g, gm_ln_b, gm_w_s, gm_b_s, ssd_conv_w, ssd_conv_b, ssd_dt_bias, ssd_a_log, ssd_d, ssd_norm_g, hyb_w_out, mla_w_in, mla_q_norm_g, mla_kv_norm_g, mla_w_uq, mla_w_ukv, mla_w_out, loss_target, m_ffn1_pre_g, m_ffn1_w_in, m_ffn1_w_down, m_ffn1_post_g, m_mix_pre_g, m_mix_post_g, m_ffn2_pre_g, m_ffn2_w_in, m_ffn2_w_down, m_ffn2_post_g, m_ple_pre_g, m_ple_w_gate, m_ple_w_proj, m_ple_post_g, m_hyb_w_in, m_gm_ln_g, m_gm_ln_b, m_gm_w_s, m_gm_b_s, m_ssd_conv_w, m_ssd_conv_b, m_ssd_dt_bias, m_ssd_a_log, m_ssd_d, m_ssd_norm_g, m_hyb_w_out, m_mla_w_in, m_mla_q_norm_g, m_mla_kv_norm_g, m_mla_w_uq, m_mla_w_ukv, m_mla_w_out, v_ffn1_pre_g, v_ffn1_w_in, v_ffn1_w_down, v_ffn1_post_g, v_mix_pre_g, v_mix_post_g, v_ffn2_pre_g, v_ffn2_w_in, v_ffn2_w_down, v_ffn2_post_g, v_ple_pre_g, v_ple_w_gate, v_ple_w_proj, v_ple_post_g, v_hyb_w_in, v_gm_ln_g, v_gm_ln_b, v_gm_w_s, v_gm_b_s, v_ssd_conv_w, v_ssd_conv_b, v_ssd_dt_bias, v_ssd_a_log, v_ssd_d, v_ssd_norm_g, v_hyb_w_out, v_mla_w_in, v_mla_q_norm_g, v_mla_kv_norm_g, v_mla_w_uq, v_mla_w_ukv, v_mla_w_out):
    given = dict(locals())
    w = {n: given[n] for n in WEIGHTS}
    mom = {n: given["m_" + n] for n in WEIGHTS}
    var = {n: given["v_" + n] for n in WEIGHTS}
    shard_shapes = [w[n].shape for n in SHARDED]
    repl_shapes = [w[n].shape for n in REPLICATED]

    send16 = {n: w[n].astype(BF16) for n in SHARDED_BF16}
    send16["hyb_w_in"] = jnp.pad(send16["hyb_w_in"], ((0, 0), (0, 0), (0, HYB_SHARD_PAD - HYB_SHARD)))
    pack16 = _pack([send16[n] for n in SHARDED_BF16])
    by_chip = _exchange(pack16, ("x", "y"), "gather", "gather_weights_ici")
    by_core = _exchange(by_chip.reshape(-1, PACK_W), ("c",), "gather", "gather_weights_d2d")
    gathered = by_core.reshape(2, 4, -1, PACK_W).transpose(1, 0, 2, 3).reshape(N_DEV, -1, PACK_W)
    fw = {n: _join_from_devices(a, SHARD_AXIS[n])
          for n, a in zip(SHARDED_BF16, _unpack(gathered, [send16[n].shape for n in SHARDED_BF16], (N_DEV,)))}
    small = _exchange(_pack([w[n] for n in SHARDED_F32]), MESH_AXES, "gather", "gather_weights_f32")
    fw.update({n: _join_from_devices(a, SHARD_AXIS[n])
               for n, a in zip(SHARDED_F32, _unpack(small, [w[n].shape for n in SHARDED_F32], (N_DEV,)))})

    loss_local, grad_x, grads = _device_step(x[0], p[:, 0], positions[0], loss_target[0], fw, w)
    loss = lax.psum(loss_local, MESH_AXES)

    per_dev = []
    for n in SHARDED:
        layers = w[n].shape[0]
        whole = (1, *w[n].shape[1:SHARD_AXIS[n]], N_DEV * w[n].shape[SHARD_AXIS[n]], *w[n].shape[SHARD_AXIS[n] + 1:])
        if int(np.prod(w[n].shape[1:])) % (16 * PACK_W) == 0:
            parts = [tuple(h[None] for h in g) if isinstance(g, tuple) else g.reshape(whole) for g in grads[n]]
        else:
            parts = [_stack_layers(grads[n], (layers, *whole[1:]))]
        per_dev.extend(_split_for_devices(g, SHARD_AXIS[n]) for g in parts)
    per_dev = [a.reshape(4, 2, *a.shape[1:]).swapaxes(0, 1) for a in per_dev]
    gpack = _pack(per_dev, (2, 4))
    rows = gpack.shape[2]
    pair = _exchange(gpack.reshape(2, 4 * rows, PACK_W), ("c",), "a2a", "reduce_grads_d2d")
    chip_sum = _rowwise(lambda a, b: a + b, [(pair, 0), (pair, 1)], [], [(PACK_W, BF16)], tm=PACK_TM, name="reduce_grads_pair")
    quads = _exchange(chip_sum.reshape(4, rows, PACK_W), ("x", "y"), "a2a", "reduce_grads_ici")
    g_s, d_s, m_s, v_s = _adamw_packed(_pack([w[n] for n in SHARDED]), _pack([mom[n] for n in SHARDED]),
                                       _pack([var[n] for n in SHARDED]), quads, 4, "adamw_sharded")

    rpack = _pack([_stack_layers(grads[n], w[n].shape) for n in REPLICATED])
    everyone = _exchange(rpack, MESH_AXES, "gather", "gather_small_grads")
    g_r, d_r, m_r, v_r = _adamw_packed(_pack([w[n] for n in REPLICATED]), _pack([mom[n] for n in REPLICATED]),
                                       _pack([var[n] for n in REPLICATED]), everyone, N_DEV, "adamw_replicated")

    outs = []
    for sharded_buf, repl_buf in ((g_s, g_r), (d_s, d_r), (m_s, m_r), (v_s, v_r)):
        vals = dict(zip(SHARDED, _unpack(sharded_buf, shard_shapes)))
        vals.update(zip(REPLICATED, _unpack(repl_buf, repl_shapes)))
        outs.extend(vals[n] for n in WEIGHTS)
    return (loss, grad_x[None], *outs)
```

```python
import functools
import math

import jax
import jax.numpy as jnp
import numpy as np
from jax import lax
from jax.experimental import pallas as pl
from jax.experimental.pallas import tpu as pltpu

F32 = jnp.float32
BF16 = jnp.bfloat16
HIGHEST = lax.Precision.HIGHEST

V7X_VMEM_LIMIT_BYTES = 52 * 1024 * 1024
LANES = 128

D_MODEL = 1024
DEPTH = 4
D_FF = 2816
PLE_DIM = 256
NORM_EPS = 1e-6
LN_EPS = 1e-5
CHUNK = 128
GM_HEADS = 8
SSD_HEADS = 16
SSD_HEAD_DIM = 64
SSD_INNER = 1024
SSD_STATE = 128
SSD_BC = 256
SSD_CONV_CH = 1536
HYB_MAIN = 4608
MLA_HEADS = 16
MLA_Q_LORA = 256
MLA_KV_LORA = 128
MLA_ROPE = 64
MLA_IN = 448
MLA_IN_PAD = 512
ATTN_SCALE = 192.0 ** -0.5
LOG2_E = 1.4426950408889634
LN_2 = 0.6931471805599453
ATTN_QSCALE = ATTN_SCALE * LOG2_E
ROPE_BASE = 10000.0

ADAM_LR = 0.001
ADAM_B1 = 0.9
ADAM_B2 = 0.999
ADAM_EPS = 1e-08
ADAM_WD = 0.01
ADAM_STEP = 10

N_DEV = 8
PACK_W = 1024
PACK_TM = 256

WEIGHTS = ['ffn1_pre_g', 'ffn1_w_in', 'ffn1_w_down', 'ffn1_post_g', 'mix_pre_g', 'mix_post_g', 'ffn2_pre_g',
           'ffn2_w_in', 'ffn2_w_down', 'ffn2_post_g', 'ple_pre_g', 'ple_w_gate', 'ple_w_proj', 'ple_post_g',
           'hyb_w_in', 'gm_ln_g', 'gm_ln_b', 'gm_w_s', 'gm_b_s', 'ssd_conv_w', 'ssd_conv_b', 'ssd_dt_bias',
           'ssd_a_log', 'ssd_d', 'ssd_norm_g', 'hyb_w_out', 'mla_w_in', 'mla_q_norm_g', 'mla_kv_norm_g',
           'mla_w_uq', 'mla_w_ukv', 'mla_w_out']
SHARD_AXIS = {'ffn1_w_in': 2, 'ffn1_w_down': 1, 'ffn2_w_in': 2, 'ffn2_w_down': 1, 'ple_w_gate': 1, 'ple_w_proj': 2,
              'hyb_w_in': 2, 'ssd_conv_w': 2, 'hyb_w_out': 1, 'mla_w_in': 1, 'mla_q_norm_g': 1, 'mla_w_uq': 2,
              'mla_w_ukv': 2, 'mla_w_out': 1}
SHARDED = [n for n in WEIGHTS if n in SHARD_AXIS]
REPLICATED = [n for n in WEIGHTS if n not in SHARD_AXIS]
SHARDED_F32 = ['ssd_conv_w', 'mla_q_norm_g']
SHARDED_BF16 = [n for n in SHARDED if n not in SHARDED_F32]


def _params(*sem):
    return pltpu.CompilerParams(dimension_semantics=sem or None, vmem_limit_bytes=V7X_VMEM_LIMIT_BYTES)


def _pick(n, prefs):
    for t in prefs:
        if t <= n and n % t == 0:
            return t
    return n


def _mm(a, b, *, ta=False, tb=False, out_dtype=F32, tm=1024, tn=512, tk=512, name):
    m, k = (a.shape[1], a.shape[0]) if ta else a.shape
    n = b.shape[0] if tb else b.shape[1]
    assert k == (b.shape[1] if tb else b.shape[0]), (a.shape, b.shape, ta, tb)
    tm, tn, tk = _pick(m, (tm, 512, 256, 128)), _pick(n, (tn, 512, 256, 128)), _pick(k, (tk, 512, 256, 128))
    nk = k // tk
    dims = (((0 if ta else 1,), (1 if tb else 0,)), ((), ()))

    def body(a_ref, b_ref, o_ref, *acc):
        part = lax.dot_general(a_ref[...].astype(BF16), b_ref[...].astype(BF16), dims, preferred_element_type=F32)
        if nk == 1:
            o_ref[...] = part.astype(o_ref.dtype)
            return
        acc_ref, = acc
        kk = pl.program_id(2)

        @pl.when(kk == 0)
        def _():
            acc_ref[...] = part

        @pl.when(kk > 0)
        def _():
            acc_ref[...] += part

        @pl.when(kk == nk - 1)
        def _():
            o_ref[...] = acc_ref[...].astype(o_ref.dtype)

    a_spec = pl.BlockSpec((tk, tm), lambda i, j, kk: (kk, i)) if ta else pl.BlockSpec((tm, tk), lambda i, j, kk: (i, kk))
    b_spec = pl.BlockSpec((tn, tk), lambda i, j, kk: (j, kk)) if tb else pl.BlockSpec((tk, tn), lambda i, j, kk: (kk, j))
    return pl.pallas_call(
        body, name=name, grid=(m // tm, n // tn, nk), in_specs=[a_spec, b_spec],
        out_specs=pl.BlockSpec((tm, tn), lambda i, j, kk: (i, j)), out_shape=jax.ShapeDtypeStruct((m, n), out_dtype),
        scratch_shapes=[] if nk == 1 else [pltpu.VMEM((tm, tn), F32)],
        compiler_params=_params("parallel", "parallel", "arbitrary"),
    )(a, b)


def _rowwise(fn, rows, consts, outs, accs=(), *, tm=256, name):
    first = rows[0][0] if isinstance(rows[0], tuple) else rows[0]
    t = first.shape[-2]
    tm = _pick(t, (tm, 256, 128, 64, 32, 16, 8))
    n_r, n_c, n_o = len(rows), len(consts), len(outs)

    def body(*refs):
        vals = [r[...] for r in refs[:n_r + n_c]]
        res = fn(*vals)
        res = res if isinstance(res, tuple) else (res,)
        o_refs, a_refs = refs[n_r + n_c:n_r + n_c + n_o], refs[n_r + n_c + n_o:]
        for o_ref, v in zip(o_refs, res[:n_o]):
            if isinstance(v, (tuple, list)):
                off = 0
                for piece in v:
                    o_ref[:, off:off + piece.shape[1]] = piece.astype(o_ref.dtype)
                    off += piece.shape[1]
            else:
                o_ref[...] = v.astype(o_ref.dtype)
        if a_refs:
            terms = res[n_o:]
            is_first = pl.program_id(0) == 0

            @pl.when(is_first)
            def _():
                for a_ref, v in zip(a_refs, terms):
                    a_ref[...] = v

            @pl.when(jnp.logical_not(is_first))
            def _():
                for a_ref, v in zip(a_refs, terms):
                    a_ref[...] += v

    in_specs, args = [], []
    for r in rows:
        if isinstance(r, tuple) and len(r) == 3:
            arr, width, cb = r
            in_specs.append(pl.BlockSpec((tm, width), functools.partial(lambda i, c: (i, c), c=cb)))
        elif isinstance(r, tuple):
            arr, slot = r
            in_specs.append(pl.BlockSpec((None, tm, arr.shape[2]), functools.partial(lambda i, s: (s, i, 0), s=slot)))
        else:
            arr = r
            in_specs.append(pl.BlockSpec((tm, arr.shape[1]), lambda i: (i, 0)))
        args.append(arr)
    for c in consts:
        in_specs.append(pl.BlockSpec(c.shape, lambda i: (0, 0)))
        args.append(c)
    out_specs = [pl.BlockSpec((tm, c), lambda i: (i, 0)) for c, _ in outs]
    out_shape = [jax.ShapeDtypeStruct((t, c), dt) for c, dt in outs]
    for shp in accs:
        out_specs.append(pl.BlockSpec(shp, lambda i: (0, 0)))
        out_shape.append(jax.ShapeDtypeStruct(shp, F32))
    res = pl.pallas_call(
        body, name=name, grid=(t // tm,), in_specs=in_specs, out_specs=out_specs, out_shape=out_shape,
        compiler_params=_params("arbitrary" if accs else "parallel"),
    )(*args)
    return res[0] if len(res) == 1 else tuple(res)


def _colsum(v):
    return jnp.sum(v, axis=0, keepdims=True)


def _rms(x, g, eps=NORM_EPS):
    r = lax.rsqrt(jnp.mean(x * x, axis=-1, keepdims=True) + eps)
    return x * r * g


def _rms_bwd(x, g, dy, eps=NORM_EPS):
    r = lax.rsqrt(jnp.mean(x * x, axis=-1, keepdims=True) + eps)
    xh = x * r
    dyg = dy * g
    dx = r * (dyg - xh * jnp.mean(dyg * xh, axis=-1, keepdims=True))
    return dx, dy * xh


def _silu(x):
    return x * jax.nn.sigmoid(x)


def _silu_grad(x):
    s = jax.nn.sigmoid(x)
    return s * (1.0 + x * (1.0 - s))


_GELU_K = math.sqrt(2.0 / math.pi)


def _gelu(x):
    return 0.5 * x * (1.0 + jnp.tanh(_GELU_K * (x + 0.044715 * x * x * x)))


def _gelu_grad(x):
    t = jnp.tanh(_GELU_K * (x + 0.044715 * x * x * x))
    return 0.5 * (1.0 + t) + 0.5 * x * (1.0 - t * t) * _GELU_K * (1.0 + 3.0 * 0.044715 * x * x)


def _prenorm(h, g, name):
    return _rowwise(lambda x, gg: _rms(x, gg), [h], [g], [(D_MODEL, BF16)], name=name)


def _postnorm_residual(h, f, g, scale, next_g, name):
    if next_g is None:
        return _rowwise(lambda x, ff, gg: x + scale * _rms(ff, gg), [h, f], [g], [(D_MODEL, F32)], name=name), None

    def fn(x, ff, gg, ng):
        out = x + scale * _rms(ff, gg)
        return out, _rms(out, ng)
    return _rowwise(fn, [h, f], [g, next_g], [(D_MODEL, F32), (D_MODEL, BF16)], name=name + "_prenorm")


def _postnorm_bwd(f, dh, g, scale, name):
    def fn(ff, d, gg):
        dx, dgt = _rms_bwd(ff, gg, scale * d)
        return dx, _colsum(dgt)
    return _rowwise(fn, [f, dh], [g], [(D_MODEL, BF16)], [(1, D_MODEL)], name=name)


FFN_TM = 1024
FFN_TN = 256


def _ffn_in_swiglu(a, w_in, name):
    t = a.shape[0]
    tm = _pick(t, (FFN_TM, 512, 256, 128))
    nj = D_FF // FFN_TN

    def body(a_ref, wg_ref, wu_ref, gate_ref, up_ref, s_ref):
        av = a_ref[...]
        gate = jnp.dot(av, wg_ref[...], preferred_element_type=F32)
        up = jnp.dot(av, wu_ref[...], preferred_element_type=F32)
        gate_ref[...] = gate
        up_ref[...] = up
        s_ref[...] = (_silu(gate) * up).astype(s_ref.dtype)

    tile = pl.BlockSpec((tm, FFN_TN), lambda i, j: (i, j))
    return pl.pallas_call(
        body, name=name, grid=(t // tm, nj),
        in_specs=[pl.BlockSpec((tm, D_MODEL), lambda i, j: (i, 0)), pl.BlockSpec((D_MODEL, FFN_TN), lambda i, j: (0, j)),
                  pl.BlockSpec((D_MODEL, FFN_TN), lambda i, j: (0, j + nj))],
        out_specs=[tile, tile, tile],
        out_shape=[jax.ShapeDtypeStruct((t, D_FF), F32), jax.ShapeDtypeStruct((t, D_FF), F32),
                   jax.ShapeDtypeStruct((t, D_FF), BF16)],
        compiler_params=_params("parallel", "parallel"),
    )(a, w_in, w_in)


def _ffn_down_dx_swiglu(df, w_down, gate, up, name):
    t = df.shape[0]
    tm = _pick(t, (FFN_TM, 512, 256, 128))

    def body(df_ref, wd_ref, gate_ref, up_ref, dgate_ref, dup_ref):
        ds = lax.dot_general(df_ref[...], wd_ref[...], (((1,), (1,)), ((), ())), preferred_element_type=F32)
        gate = gate_ref[...]
        sg = jax.nn.sigmoid(gate)
        dgate_ref[...] = (ds * up_ref[...] * (sg * (1.0 + gate * (1.0 - sg)))).astype(dgate_ref.dtype)
        dup_ref[...] = (ds * (gate * sg)).astype(dup_ref.dtype)

    tile = pl.BlockSpec((tm, FFN_TN), lambda i, j: (i, j))
    return pl.pallas_call(
        body, name=name, grid=(t // tm, D_FF // FFN_TN),
        in_specs=[pl.BlockSpec((tm, D_MODEL), lambda i, j: (i, 0)), pl.BlockSpec((FFN_TN, D_MODEL), lambda i, j: (j, 0)),
                  tile, tile],
        out_specs=[tile, tile],
        out_shape=[jax.ShapeDtypeStruct((t, D_FF), BF16), jax.ShapeDtypeStruct((t, D_FF), BF16)],
        compiler_params=_params("parallel", "parallel"),
    )(df, w_down, gate, up)


def _ffn_fwd(h, a, w, pre_g, post_g, next_g, tag):
    if a is None:
        a = _prenorm(h, pre_g, tag + "_prenorm")
    gate, up, s = _ffn_in_swiglu(a, w["in"], tag + "_in_swiglu")
    f = _mm(s, w["down"], tm=1024, tn=1024, tk=D_FF, name=tag + "_down")
    out, a_next = _postnorm_residual(h, f, post_g, 0.5, next_g, tag + "_postnorm")
    return out, a_next, (h, a, gate, up, s, f)


DX_TM = 512


def _dx_prenorm_bwd(parts, h, dh, pre_g, name):
    t = h.shape[0]
    tm = _pick(t, (DX_TM, 256, 128))
    counts = [d.shape[1] // tk for d, _, _, tk in parts]
    starts = [sum(counts[:p]) for p in range(len(parts))]
    nk = sum(counts)
    n_p = len(parts)

    def body(*refs):
        h_ref, dh_ref, g_ref, out_ref, dg_ref, acc_ref = refs[2 * n_p:]
        i, kk = pl.program_id(0), pl.program_id(1)
        for p in range(n_p):
            d_ref, w_ref = refs[2 * p], refs[2 * p + 1]

            def contribution(d_ref=d_ref, w_ref=w_ref):
                return lax.dot_general(d_ref[...], w_ref[...], (((1,), (1,)), ((), ())), preferred_element_type=F32)

            lo = starts[p] + (1 if p == 0 else 0)
            if p == 0:
                @pl.when(kk == 0)
                def _(contribution=contribution):
                    acc_ref[...] = contribution()

            if starts[p] + counts[p] > lo:
                @pl.when(jnp.logical_and(kk >= lo, kk < starts[p] + counts[p]))
                def _(contribution=contribution):
                    acc_ref[...] += contribution()

        @pl.when(kk == nk - 1)
        def _():
            dx, dgt = _rms_bwd(h_ref[...], g_ref[...], acc_ref[...])
            out_ref[...] = dh_ref[...] + dx
            dg = _colsum(dgt)

            @pl.when(i == 0)
            def _():
                dg_ref[...] = dg

            @pl.when(i > 0)
            def _():
                dg_ref[...] += dg

    in_specs, args = [], []
    for (d, w, k0, tk), start, count in zip(parts, starts, counts):
        assert d.shape[1] % tk == 0 and k0 % tk == 0, (d.shape, k0, tk)
        step = functools.partial(lambda kk, s, c: jnp.clip(kk - s, 0, c - 1), s=start, c=count)
        in_specs.append(pl.BlockSpec((tm, tk), functools.partial(lambda i, kk, st: (i, st(kk)), st=step)))
        in_specs.append(pl.BlockSpec((D_MODEL, tk), functools.partial(lambda i, kk, st, b0: (0, b0 + st(kk)), st=step, b0=k0 // tk)))
        args += [d, w]
    row = pl.BlockSpec((tm, D_MODEL), lambda i, kk: (i, 0))
    vec = pl.BlockSpec((1, D_MODEL), lambda i, kk: (0, 0))
    return pl.pallas_call(
        body, name=name, grid=(t // tm, nk), in_specs=in_specs + [row, row, vec], out_specs=[row, vec],
        out_shape=[jax.ShapeDtypeStruct((t, D_MODEL), F32), jax.ShapeDtypeStruct((1, D_MODEL), F32)],
        scratch_shapes=[pltpu.VMEM((tm, D_MODEL), F32)],
        compiler_params=_params("arbitrary", "arbitrary"),
    )(*args, h, dh, pre_g)


def _ffn_bwd(dh, saved, w, pre_g, post_g, tag):
    h, a, gate, up, s, f = saved
    df, d_post = _postnorm_bwd(f, dh, post_g, 0.5, tag + "_postnorm_bwd")
    dgate, dup = _ffn_down_dx_swiglu(df, w["down"], gate, up, tag + "_down_dx_swiglu")
    d_down = _mm(s, df, ta=True, tm=1408, tn=1024, tk=1024, name=tag + "_down_dw")
    d_in = (_mm(a, dgate, ta=True, tm=1024, tn=1408, tk=1024, name=tag + "_in_dw_gate"),
            _mm(a, dup, ta=True, tm=1024, tn=1408, tk=1024, name=tag + "_in_dw_up"))
    dh_in, d_pre = _dx_prenorm_bwd([(dgate, w["in"], 0, 1408), (dup, w["in"], D_FF, 1408)], h, dh, pre_g,
                                   tag + "_in_dx_prenorm_bwd")
    return dh_in, dict(w_in=d_in, w_down=d_down, pre_g=d_pre, post_g=d_post)


def _ple_fwd(h, a, p_i, w, post_g, next_g):
    gl = _mm(a, w["gate"], tm=1024, tn=1024, tk=1024, name="ple_gate")
    e = _mm(p_i, w["proj"], tm=1024, tn=1024, tk=PLE_DIM, name="ple_proj")
    if next_g is None:
        out = _rowwise(lambda x, g_, e_, gg: x + _rms(jax.nn.sigmoid(g_) * e_, gg), [h, gl, e], [post_g],
                       [(D_MODEL, F32)], name="ple_out")
        return out, None, (h, a, gl, e)

    def fn(x, g_, e_, gg, ng):
        out = x + _rms(jax.nn.sigmoid(g_) * e_, gg)
        return out, _rms(out, ng)
    out, a_next = _rowwise(fn, [h, gl, e], [post_g, next_g], [(D_MODEL, F32), (D_MODEL, BF16)], name="ple_out_prenorm")
    return out, a_next, (h, a, gl, e)


def _ple_bwd(dh, saved, p_i, w, pre_g, post_g):
    h, a, gl, e = saved

    def fn(g_, e_, d, gg):
        sg = jax.nn.sigmoid(g_)
        du, dgt = _rms_bwd(sg * e_, gg, d)
        return du * e_ * sg * (1.0 - sg), du * sg, _colsum(dgt)
    dgl, de, d_post = _rowwise(fn, [gl, e, dh], [post_g], [(D_MODEL, BF16), (D_MODEL, BF16)], [(1, D_MODEL)],
                               name="ple_out_bwd")
    d_gate = _mm(a, dgl, ta=True, tm=1024, tn=1024, tk=2048, name="ple_gate_dw")
    d_proj = _mm(p_i, de, ta=True, tm=PLE_DIM, tn=1024, tk=2048, name="ple_proj_dw")
    dh_in, d_pre = _dx_prenorm_bwd([(dgl, w["gate"], 0, 1024)], h, dh, pre_g, "ple_gate_dx_prenorm_bwd")
    return dh_in, dict(w_gate=d_gate, w_proj=d_proj, pre_g=d_pre, post_g=d_post)


def _gm_layernorm(v, g, b):
    mu = jnp.mean(v, axis=-1, keepdims=True)
    xc = v - mu
    rstd = lax.rsqrt(jnp.mean(xc * xc, axis=-1, keepdims=True) + LN_EPS)
    vhat = xc * rstd
    return vhat, rstd, vhat * g + b


def _gmlp_fwd(proj, wm, bias_t, ln_g, ln_b, name):
    t = proj.shape[0]

    def body(uv_ref, wm_ref, bt_ref, g_ref, b_ref, o_ref):
        for hd in range(GM_HEADS):
            lo = hd * LANES
            u = _gelu(uv_ref[:, lo:lo + LANES])
            v = _gelu(uv_ref[:, 1024 + lo:1024 + lo + LANES])
            _, _, vln = _gm_layernorm(v, g_ref[:, lo:lo + LANES], b_ref[:, lo:lo + LANES])
            mixed = jnp.dot(wm_ref[hd], vln.astype(BF16), preferred_element_type=F32) + bt_ref[:, hd:hd + 1]
            o_ref[:, lo:lo + LANES] = (u * mixed).astype(o_ref.dtype)

    return pl.pallas_call(
        body, name=name, grid=(t // CHUNK,),
        in_specs=[pl.BlockSpec((CHUNK, 2048), lambda i: (i, 0)), pl.BlockSpec(wm.shape, lambda i: (0, 0, 0)),
                  pl.BlockSpec(bias_t.shape, lambda i: (0, 0)), pl.BlockSpec(ln_g.shape, lambda i: (0, 0)),
                  pl.BlockSpec(ln_b.shape, lambda i: (0, 0))],
        out_specs=pl.BlockSpec((CHUNK, 1024), lambda i: (i, 0)), out_shape=jax.ShapeDtypeStruct((t, 1024), BF16),
        compiler_params=_params("parallel"),
    )(proj, wm, bias_t, ln_g, ln_b)


def _gmlp_bwd(proj, dyab, wm, bias_t, ln_g, ln_b, name):
    t = proj.shape[0]
    nc = t // CHUNK

    def body(uv_ref, dy_ref, wm_ref, bt_ref, g_ref, b_ref, duv_ref, dw_ref, db_ref, dg_ref, dbeta_ref, dbacc):
        c = pl.program_id(0)

        @pl.when(c == 0)
        def _():
            dw_ref[...] = jnp.zeros_like(dw_ref)
            dbacc[...] = jnp.zeros_like(dbacc)
            dg_ref[...] = jnp.zeros_like(dg_ref)
            dbeta_ref[...] = jnp.zeros_like(dbeta_ref)

        for hd in range(GM_HEADS):
            lo = hd * LANES
            xu = uv_ref[:, lo:lo + LANES]
            xv = uv_ref[:, 1024 + lo:1024 + lo + LANES]
            u = _gelu(xu)
            g_h = g_ref[:, lo:lo + LANES]
            vhat, rstd, vln = _gm_layernorm(_gelu(xv), g_h, b_ref[:, lo:lo + LANES])
            vln16 = vln.astype(BF16)
            mixed = jnp.dot(wm_ref[hd], vln16, preferred_element_type=F32) + bt_ref[:, hd:hd + 1]
            dy = dy_ref[:, lo:lo + LANES]
            du = dy * mixed
            dmix = dy * u
            dmix16 = dmix.astype(BF16)
            dw_ref[hd] += lax.dot_general(dmix16, vln16, (((1,), (1,)), ((), ())), preferred_element_type=F32)
            dbacc[hd] += dmix
            dvln = lax.dot_general(wm_ref[hd], dmix16, (((0,), (0,)), ((), ())), preferred_element_type=F32)
            dg_ref[:, lo:lo + LANES] += _colsum(dvln * vhat)
            dbeta_ref[:, lo:lo + LANES] += _colsum(dvln)
            dvh = dvln * g_h
            dv = rstd * (dvh - jnp.mean(dvh, axis=-1, keepdims=True)
                         - vhat * jnp.mean(dvh * vhat, axis=-1, keepdims=True))
            duv_ref[:, lo:lo + LANES] = (du * _gelu_grad(xu)).astype(duv_ref.dtype)
            duv_ref[:, 1024 + lo:1024 + lo + LANES] = (dv * _gelu_grad(xv)).astype(duv_ref.dtype)

        @pl.when(c == nc - 1)
        def _():
            row = lax.broadcasted_iota(jnp.int32, (CHUNK, CHUNK), 0)
            col = lax.broadcasted_iota(jnp.int32, (CHUNK, CHUNK), 1)
            for hd in range(GM_HEADS):
                dw_ref[hd] = jnp.where(col <= row, dw_ref[hd], 0.0)
                db_ref[hd] = jnp.sum(dbacc[hd], axis=1, keepdims=True)

    return pl.pallas_call(
        body, name=name, grid=(nc,),
        in_specs=[pl.BlockSpec((CHUNK, 2048), lambda i: (i, 0)), pl.BlockSpec((CHUNK, 1024), lambda i: (i, 0)),
                  pl.BlockSpec(wm.shape, lambda i: (0, 0, 0)), pl.BlockSpec(bias_t.shape, lambda i: (0, 0)),
                  pl.BlockSpec(ln_g.shape, lambda i: (0, 0)), pl.BlockSpec(ln_b.shape, lambda i: (0, 0))],
        out_specs=[pl.BlockSpec((CHUNK, 2048), lambda i: (i, 0)), pl.BlockSpec((GM_HEADS, CHUNK, CHUNK), lambda i: (0, 0, 0)),
                   pl.BlockSpec((GM_HEADS, CHUNK, 1), lambda i: (0, 0, 0)), pl.BlockSpec((1, 1024), lambda i: (0, 0)),
                   pl.BlockSpec((1, 1024), lambda i: (0, 0))],
        out_shape=[jax.ShapeDtypeStruct((t, 2048), BF16), jax.ShapeDtypeStruct((GM_HEADS, CHUNK, CHUNK), F32),
                   jax.ShapeDtypeStruct((GM_HEADS, CHUNK, 1), F32), jax.ShapeDtypeStruct((1, 1024), F32),
                   jax.ShapeDtypeStruct((1, 1024), F32)],
        scratch_shapes=[pltpu.VMEM((GM_HEADS, CHUNK, CHUNK), F32)],
        compiler_params=_params("arbitrary"),
    )(proj, dyab, wm, bias_t, ln_g, ln_b)


def _ssd_chunk_terms(dt_pad, a_pad):
    row = lax.broadcasted_iota(jnp.int32, (CHUNK, CHUNK), 0)
    col = lax.broadcasted_iota(jnp.int32, (CHUNK, CHUNK), 1)
    tril = jnp.where(col <= row, 1.0, 0.0).astype(F32)
    a_cs = jnp.dot(tril, dt_pad * a_pad, precision=HIGHEST, preferred_element_type=F32)
    return a_cs, a_cs.T


def _pair_cols(mat, hd_a, lane_lt64):
    return jnp.where(lane_lt64, mat[:, hd_a:hd_a + 1], mat[:, hd_a + 1:hd_a + 2])


def _head_decay(a_cs, a_cs_t, hd, causal):
    seg = a_cs[:, hd:hd + 1] - a_cs_t[hd:hd + 1, :]
    return jnp.exp(jnp.where(causal, seg, -jnp.inf))


def _ssd_fwd(act, dt_pad, a_pad, d_pad, name):
    t = act.shape[0]
    nc = t // CHUNK

    def body(act_ref, dt_ref, a_ref, d_ref, y_ref, st_ref, h_sc):
        c = pl.program_id(0)

        @pl.when(c == 0)
        def _():
            h_sc[...] = jnp.zeros_like(h_sc)

        st_ref[...] = h_sc[...]
        row = lax.broadcasted_iota(jnp.int32, (CHUNK, CHUNK), 0)
        col = lax.broadcasted_iota(jnp.int32, (CHUNK, CHUNK), 1)
        causal = col <= row
        lane_lt64 = lax.broadcasted_iota(jnp.int32, (CHUNK, LANES), 1) < SSD_HEAD_DIM
        row_lt64 = lax.broadcasted_iota(jnp.int32, (LANES, 1), 0) < SSD_HEAD_DIM
        dt = dt_ref[...]
        a_cs, a_cs_t = _ssd_chunk_terms(dt, a_ref[...])
        last = a_cs[CHUNK - 1:CHUNK, :]
        for g in range(2):
            b16 = act_ref[:, SSD_INNER + g * SSD_STATE:SSD_INNER + (g + 1) * SSD_STATE].astype(BF16)
            c16 = act_ref[:, SSD_INNER + SSD_BC + g * SSD_STATE:SSD_INNER + SSD_BC + (g + 1) * SSD_STATE].astype(BF16)
            cb = lax.dot_general(c16, b16, (((1,), (1,)), ((), ())), preferred_element_type=F32)
            for pr in range(4):
                ha = g * 8 + pr * 2
                lo = ha * SSD_HEAD_DIM
                xs = act_ref[:, lo:lo + LANES]
                xd = xs * _pair_cols(dt, ha, lane_lt64)
                xd16 = xd.astype(BF16)
                ya = jnp.dot((cb * _head_decay(a_cs, a_cs_t, ha, causal)).astype(BF16), xd16, preferred_element_type=F32)
                yb = jnp.dot((cb * _head_decay(a_cs, a_cs_t, ha + 1, causal)).astype(BF16), xd16, preferred_element_type=F32)
                a_p = _pair_cols(a_cs, ha, lane_lt64)
                hp = h_sc[lo:lo + LANES, :]
                y_off = lax.dot_general(c16, hp.astype(BF16), (((1,), (1,)), ((), ())), preferred_element_type=F32)
                d_p = jnp.where(lane_lt64[:1], d_ref[:, ha:ha + 1], d_ref[:, ha + 1:ha + 2])
                y_ref[:, lo:lo + LANES] = jnp.where(lane_lt64, ya, yb) + y_off * jnp.exp(a_p) + d_p * xs
                last_p = jnp.where(lane_lt64[:1], last[:, ha:ha + 1], last[:, ha + 1:ha + 2])
                xw16 = (xd * jnp.exp(last_p - a_p)).astype(BF16)
                s_new = lax.dot_general(xw16, b16, (((0,), (0,)), ((), ())), preferred_element_type=F32)
                t_col = jnp.where(row_lt64, jnp.exp(last[:, ha:ha + 1]), jnp.exp(last[:, ha + 1:ha + 2]))
                h_sc[lo:lo + LANES, :] = t_col * hp + s_new

    return pl.pallas_call(
        body, name=name, grid=(nc,),
        in_specs=[pl.BlockSpec((CHUNK, SSD_CONV_CH), lambda i: (i, 0)), pl.BlockSpec((CHUNK, LANES), lambda i: (i, 0)),
                  pl.BlockSpec((1, LANES), lambda i: (0, 0)), pl.BlockSpec((1, LANES), lambda i: (0, 0))],
        out_specs=[pl.BlockSpec((CHUNK, SSD_INNER), lambda i: (i, 0)),
                   pl.BlockSpec((None, SSD_INNER, SSD_STATE), lambda i: (i, 0, 0))],
        out_shape=[jax.ShapeDtypeStruct((t, SSD_INNER), F32), jax.ShapeDtypeStruct((nc, SSD_INNER, SSD_STATE), F32)],
        scratch_shapes=[pltpu.VMEM((SSD_INNER, SSD_STATE), F32)],
        compiler_params=_params("arbitrary"),
    )(act, dt_pad, a_pad, d_pad)


def _ssd_bwd(act, dt_pad, a_pad, d_pad, states, dy, name):
    t = act.shape[0]
    nc = t // CHUNK

    def body(act_ref, dt_ref, a_ref, d_ref, st_ref, dy_ref, dact_ref, ddt_ref, da_ref, dd_ref, dh_sc):
        c = pl.program_id(0)

        @pl.when(c == 0)
        def _():
            dh_sc[...] = jnp.zeros_like(dh_sc)
            da_ref[...] = jnp.zeros_like(da_ref)
            dd_ref[...] = jnp.zeros_like(dd_ref)

        row = lax.broadcasted_iota(jnp.int32, (CHUNK, CHUNK), 0)
        col = lax.broadcasted_iota(jnp.int32, (CHUNK, CHUNK), 1)
        causal = col <= row
        lane = lax.broadcasted_iota(jnp.int32, (CHUNK, LANES), 1)
        lane_lt64 = lane < SSD_HEAD_DIM
        row_lt64 = lax.broadcasted_iota(jnp.int32, (LANES, 1), 0) < SSD_HEAD_DIM
        is_last = lax.broadcasted_iota(jnp.int32, (CHUNK, 1), 0) == CHUNK - 1
        dt = dt_ref[...]
        a_cs, a_cs_t = _ssd_chunk_terms(dt, a_ref[...])
        last = a_cs[CHUNK - 1:CHUNK, :]
        d_acs = jnp.zeros((CHUNK, LANES), F32)
        d_acs_rows = jnp.zeros((LANES, CHUNK), F32)
        head_row = lax.broadcasted_iota(jnp.int32, (LANES, CHUNK), 0)
        ddt_x = jnp.zeros((CHUNK, LANES), F32)
        dd_acc = jnp.zeros((1, LANES), F32)

        def head_sum(v, first):
            return jnp.sum(jnp.where(lane_lt64 if first else jnp.logical_not(lane_lt64), v, 0.0), axis=1, keepdims=True)

        for g in range(2):
            b_lo = SSD_INNER + g * SSD_STATE
            c_lo = SSD_INNER + SSD_BC + g * SSD_STATE
            b16 = act_ref[:, b_lo:b_lo + SSD_STATE].astype(BF16)
            c16 = act_ref[:, c_lo:c_lo + SSD_STATE].astype(BF16)
            cb = lax.dot_general(c16, b16, (((1,), (1,)), ((), ())), preferred_element_type=F32)
            dcb = jnp.zeros((CHUNK, CHUNK), F32)
            db_g = jnp.zeros((CHUNK, SSD_STATE), F32)
            dc_g = jnp.zeros((CHUNK, SSD_STATE), F32)
            for pr in range(4):
                ha = g * 8 + pr * 2
                lo = ha * SSD_HEAD_DIM
                xs = act_ref[:, lo:lo + LANES]
                dt_p = _pair_cols(dt, ha, lane_lt64)
                xd = xs * dt_p
                xd16 = xd.astype(BF16)
                a_p = _pair_cols(a_cs, ha, lane_lt64)
                exp_a = jnp.exp(a_p)
                last_p = jnp.where(lane_lt64[:1], last[:, ha:ha + 1], last[:, ha + 1:ha + 2])
                w_p = jnp.exp(last_p - a_p)
                hp = st_ref[lo:lo + LANES, :]
                hp16 = hp.astype(BF16)
                dhn = dh_sc[lo:lo + LANES, :]
                dhn16 = dhn.astype(BF16)
                dyp = dy_ref[:, lo:lo + LANES]
                d_p = jnp.where(lane_lt64[:1], d_ref[:, ha:ha + 1], d_ref[:, ha + 1:ha + 2])
                dd_row = _colsum(dyp * xs)
                dd_acc = dd_acc + jnp.where(lane[:1] == ha, jnp.sum(jnp.where(lane_lt64[:1], dd_row, 0.0), axis=1, keepdims=True), 0.0) \
                    + jnp.where(lane[:1] == ha + 1, jnp.sum(jnp.where(lane_lt64[:1], 0.0, dd_row), axis=1, keepdims=True), 0.0)
                g_off = lax.dot_general(c16, hp16, (((1,), (1,)), ((), ())), preferred_element_type=F32)
                dg16 = (dyp * exp_a).astype(BF16)
                dc_g = dc_g + jnp.dot(dg16, hp16, preferred_element_type=F32)
                dh_prev = lax.dot_general(dg16, c16, (((0,), (0,)), ((), ())), preferred_element_type=F32)
                off_term = dyp * g_off * exp_a
                q = lax.dot_general(b16, dhn16, (((1,), (1,)), ((), ())), preferred_element_type=F32)
                xw16 = (xd * w_p).astype(BF16)
                db_g = db_g + jnp.dot(xw16, dhn16, preferred_element_type=F32)
                dw_term = xd * q * w_p
                dxd = w_p * q
                dt_all = dhn * hp
                dyp16 = dyp.astype(BF16)
                for k, first in ((0, True), (1, False)):
                    hd = ha + k
                    sel = lane_lt64 if first else jnp.logical_not(lane_lt64)
                    decay = _head_decay(a_cs, a_cs_t, hd, causal)
                    m = cb * decay
                    dy_h = jnp.where(sel, dyp16, jnp.zeros_like(dyp16))
                    dm = lax.dot_general(dy_h, xd16, (((1,), (1,)), ((), ())), preferred_element_type=F32)
                    dcb = dcb + dm * decay
                    dseg = dm * m
                    dxd = dxd + jnp.where(sel, lax.dot_general(m.astype(BF16), dyp16, (((0,), (0,)), ((), ())),
                                                               preferred_element_type=F32), 0.0)
                    d_col = jnp.sum(dseg, axis=1, keepdims=True)
                    d_acs_rows = d_acs_rows + jnp.where(head_row == hd, jnp.sum(dseg, axis=0, keepdims=True), 0.0)
                    dw_col = head_sum(dw_term, first)
                    d_col = d_col + head_sum(off_term, first) - dw_col
                    t_h = jnp.exp(last[:, hd:hd + 1])
                    dt_sum = jnp.sum(jnp.sum(jnp.where(row_lt64 if first else jnp.logical_not(row_lt64), dt_all, 0.0),
                                             axis=0, keepdims=True), axis=1, keepdims=True)
                    end_term = jnp.sum(dw_col, axis=0, keepdims=True) + dt_sum * t_h
                    d_col = d_col + jnp.where(is_last, end_term, 0.0)
                    d_acs = d_acs + jnp.where(lane == hd, d_col, 0.0)
                t_col = jnp.where(row_lt64, jnp.exp(last[:, ha:ha + 1]), jnp.exp(last[:, ha + 1:ha + 2]))
                dh_sc[lo:lo + LANES, :] = t_col * dhn + dh_prev
                dact_ref[:, lo:lo + LANES] = d_p * dyp + dxd * dt_p
                ddt_all = dxd * xs
                ddt_x = ddt_x + jnp.where(lane == ha, head_sum(ddt_all, True), 0.0) \
                    + jnp.where(lane == ha + 1, head_sum(ddt_all, False), 0.0)
            dcb16 = dcb.astype(BF16)
            dact_ref[:, b_lo:b_lo + SSD_STATE] = db_g + lax.dot_general(dcb16, c16, (((0,), (0,)), ((), ())),
                                                                          preferred_element_type=F32)
            dact_ref[:, c_lo:c_lo + SSD_STATE] = dc_g + jnp.dot(dcb16, b16, preferred_element_type=F32)
        triu = jnp.where(col >= row, 1.0, 0.0).astype(F32)
        dda = jnp.dot(triu, d_acs - d_acs_rows.T, precision=HIGHEST, preferred_element_type=F32)
        ddt_ref[...] = dda * a_ref[...] + ddt_x
        da_ref[...] += _colsum(dda * dt)
        dd_ref[...] += dd_acc

    rev = lambda i: (nc - 1 - i, 0)
    return pl.pallas_call(
        body, name=name, grid=(nc,),
        in_specs=[pl.BlockSpec((CHUNK, SSD_CONV_CH), rev), pl.BlockSpec((CHUNK, LANES), rev),
                  pl.BlockSpec((1, LANES), lambda i: (0, 0)), pl.BlockSpec((1, LANES), lambda i: (0, 0)),
                  pl.BlockSpec((None, SSD_INNER, SSD_STATE), lambda i: (nc - 1 - i, 0, 0)),
                  pl.BlockSpec((CHUNK, SSD_INNER), rev)],
        out_specs=[pl.BlockSpec((CHUNK, SSD_CONV_CH), rev), pl.BlockSpec((CHUNK, LANES), rev),
                   pl.BlockSpec((1, LANES), lambda i: (0, 0)), pl.BlockSpec((1, LANES), lambda i: (0, 0))],
        out_shape=[jax.ShapeDtypeStruct((t, SSD_CONV_CH), F32), jax.ShapeDtypeStruct((t, LANES), F32),
                   jax.ShapeDtypeStruct((1, LANES), F32), jax.ShapeDtypeStruct((1, LANES), F32)],
        scratch_shapes=[pltpu.VMEM((SSD_INNER, SSD_STATE), F32)],
        compiler_params=_params("arbitrary"),
    )(act, dt_pad, a_pad, d_pad, states, dy)


def _shift_down(x, k):
    return x if k == 0 else jnp.pad(x, ((k, 0), (0, 0)))[:x.shape[0]]


def _shift_up(x, k):
    return x if k == 0 else jnp.pad(x, ((0, k), (0, 0)))[k:]


def _conv_pre(x0, x1, x2, x3, w, b):
    return x0 * w[0:1] + x1 * w[1:2] + x2 * w[2:3] + x3 * w[3:4] + b


def _rope128(x, cpad, s_lo, s_hi):
    return x * cpad + pltpu.roll(x, 96, 1) * s_lo + pltpu.roll(x, 32, 1) * s_hi


ATTN_ROW_SPLIT = 4
ATTN_ROW_SPLIT_DKV = 4


def _diag_mask(rows, cols, row0):
    return lax.broadcasted_iota(jnp.int32, (rows, cols), 1) <= row0 + lax.broadcasted_iota(jnp.int32, (rows, cols), 0)


def _attn_scores(q, k):
    return lax.dot_general(q, k, (((1,), (1,)), ((), ())), preferred_element_type=F32)


def _causal_pairs(nq, by_key):
    if by_key:
        pairs = [(i, j) for j in range(nq) for i in range(j, nq)]
    else:
        pairs = [(i, j) for i in range(nq) for j in range(i + 1)]
    return (jnp.asarray([pr[0] for pr in pairs], jnp.int32), jnp.asarray([pr[1] for pr in pairs], jnp.int32))


def _attn_fwd(qf, kf, kvf, *, tq, name):
    t = qf.shape[0]
    nq = t // tq
    tk = tq
    qi, kj = _causal_pairs(nq, by_key=False)
    rs = tq // ATTN_ROW_SPLIT

    def body(qi_ref, kj_ref, q_ref, k_ref, v_ref, o_ref, lse_ref, m_sc, acc_sc, v1_sc):
        pp = pl.program_id(1)
        i, j = qi_ref[pp], kj_ref[pp]

        @pl.when(pp == 0)
        def _():
            v1_sc[:, LANES:] = jnp.ones((tk, LANES), BF16)

        @pl.when(j == 0)
        def _():
            m_sc[...] = jnp.full_like(m_sc, -jnp.inf)
            acc_sc[...] = jnp.zeros_like(acc_sc)

        v1_sc[:, :LANES] = v_ref[...]

        def update(diag):
            for r in range(ATTN_ROW_SPLIT):
                rows = slice(r * rs, (r + 1) * rs)
                keys = slice(0, (r + 1) * rs if diag else tk)
                s = _attn_scores(q_ref[rows, :], k_ref[keys, :])
                if diag:
                    s = jnp.where(_diag_mask(rs, keys.stop, r * rs), s, -jnp.inf)
                m_prev = m_sc[rows, :]
                m_new = jnp.maximum(m_prev, jnp.max(s, axis=1, keepdims=True))
                p = jnp.exp2(s - m_new).astype(BF16)
                alpha = jnp.exp2(m_prev - m_new)
                acc = alpha * acc_sc[rows, :] + jnp.dot(p, v1_sc[keys, :], preferred_element_type=F32)
                if diag:
                    o_ref[rows, :] = (acc[:, :LANES] / acc[:, LANES:]).astype(o_ref.dtype)
                    lse_ref[rows, :] = m_new + jnp.log2(acc[:, LANES:LANES + 1])
                else:
                    acc_sc[rows, :] = acc
                    m_sc[rows, :] = m_new

        @pl.when(j < i)
        def _():
            update(False)

        @pl.when(j == i)
        def _():
            update(True)

    return pl.pallas_call(
        body, name=name,
        grid_spec=pltpu.PrefetchScalarGridSpec(
            num_scalar_prefetch=2, grid=(MLA_HEADS, int(qi.shape[0])),
            in_specs=[pl.BlockSpec((tq, 2 * LANES), lambda h, pp, qi_, kj_: (qi_[pp], h)),
                      pl.BlockSpec((tk, 2 * LANES), lambda h, pp, qi_, kj_: (kj_[pp], h)),
                      pl.BlockSpec((tk, LANES), lambda h, pp, qi_, kj_: (kj_[pp], 2 * h + 1))],
            out_specs=[pl.BlockSpec((tq, LANES), lambda h, pp, qi_, kj_: (qi_[pp], h)),
                       pl.BlockSpec((None, tq, 1), lambda h, pp, qi_, kj_: (h, qi_[pp], 0))],
            scratch_shapes=[pltpu.VMEM((tq, 1), F32), pltpu.VMEM((tq, 2 * LANES), F32), pltpu.VMEM((tk, 2 * LANES), BF16)]),
        out_shape=[jax.ShapeDtypeStruct((t, MLA_HEADS * LANES), BF16), jax.ShapeDtypeStruct((MLA_HEADS, t, 1), F32)],
        compiler_params=_params("arbitrary", "arbitrary"),
    )(qi, kj, qf, kf, kvf)


def _attn_bwd(qf, kf, kvf, o, do, lse, *, tq, name):
    t = qf.shape[0]
    nq = t // tq
    tk = tq
    qi, kj = _causal_pairs(nq, by_key=True)
    rs = tq // ATTN_ROW_SPLIT_DKV

    def body(qi_ref, kj_ref, q_ref, k_ref, v_ref, o_ref, do_ref, lse_ref, dkv_ref, dkr_ref, dq_ref, dk_sc, dv_sc):
        pp = pl.program_id(1)
        i, j = qi_ref[pp], kj_ref[pp]
        tn = (((0,), (0,)), ((), ()))

        @pl.when(pp == 0)
        def _():
            dq_ref[...] = jnp.zeros_like(dq_ref)

        def update(diag):
            if diag:
                dv_sc[...] = jnp.zeros_like(dv_sc)
                dk_sc[...] = jnp.zeros_like(dk_sc)
            for r in range(ATTN_ROW_SPLIT_DKV):
                rows = slice(r * rs, (r + 1) * rs)
                keys = slice(0, (r + 1) * rs if diag else tk)
                do_ = do_ref[rows, :]
                delta = jnp.sum(do_.astype(F32) * o_ref[rows, :].astype(F32), axis=1, keepdims=True)
                s = _attn_scores(q_ref[rows, :], k_ref[keys, :])
                p = jnp.exp2(s - lse_ref[rows, :])
                if diag:
                    p = jnp.where(_diag_mask(rs, keys.stop, r * rs), p, 0.0)
                dp = lax.dot_general(do_, v_ref[keys, :], (((1,), (1,)), ((), ())), preferred_element_type=F32)
                ds = (p * (dp - delta)).astype(BF16)
                dv_sc[keys, :] += lax.dot_general(p.astype(BF16), do_, tn, preferred_element_type=F32)
                dk_sc[keys, :] += lax.dot_general(ds, q_ref[rows, :], tn, preferred_element_type=F32)
                q_rows = pl.ds(pl.multiple_of(i * tq + r * rs, rs), rs)
                dq_ref[q_rows, :] += jnp.dot(ds, k_ref[keys, :], preferred_element_type=F32)

        @pl.when(i > j)
        def _():
            update(False)

        @pl.when(i == j)
        def _():
            update(True)

        @pl.when(i == nq - 1)
        def _():
            dkv_ref[:, :LANES] = (dk_sc[:, :LANES] * LN_2).astype(dkv_ref.dtype)
            dkv_ref[:, LANES:] = dv_sc[...].astype(dkv_ref.dtype)
            dkr_ref[...] = dk_sc[:, LANES:] * LN_2

    qblk = lambda c: (lambda h, pp, qi_, kj_: (qi_[pp], c(h)))
    kblk = lambda c: (lambda h, pp, qi_, kj_: (kj_[pp], c(h)))
    return pl.pallas_call(
        body, name=name,
        grid_spec=pltpu.PrefetchScalarGridSpec(
            num_scalar_prefetch=2, grid=(MLA_HEADS, int(qi.shape[0])),
            in_specs=[pl.BlockSpec((tq, 2 * LANES), qblk(lambda h: h)), pl.BlockSpec((tk, 2 * LANES), kblk(lambda h: h)),
                      pl.BlockSpec((tk, LANES), kblk(lambda h: 2 * h + 1)),
                      pl.BlockSpec((tq, LANES), qblk(lambda h: h)), pl.BlockSpec((tq, LANES), qblk(lambda h: h)),
                      pl.BlockSpec((None, tq, 1), lambda h, pp, qi_, kj_: (h, qi_[pp], 0))],
            out_specs=[pl.BlockSpec((tk, 2 * LANES), kblk(lambda h: h)), pl.BlockSpec((tk, LANES), kblk(lambda h: h)),
                       pl.BlockSpec((t, 2 * LANES), lambda h, pp, qi_, kj_: (0, h))],
            scratch_shapes=[pltpu.VMEM((tk, 2 * LANES), F32), pltpu.VMEM((tk, LANES), F32)]),
        out_shape=[jax.ShapeDtypeStruct((t, MLA_HEADS * 2 * LANES), BF16), jax.ShapeDtypeStruct((t, MLA_HEADS * LANES), F32),
                   jax.ShapeDtypeStruct((t, MLA_HEADS * 2 * LANES), F32)],
        compiler_params=_params("parallel", "arbitrary"),
    )(qi, kj, qf, kf, kvf, o, do, lse)


def _rope_tables(positions):
    t = positions.shape[0]
    inv = 1.0 / (ROPE_BASE ** (jnp.arange(0, MLA_ROPE, 2, dtype=F32) / MLA_ROPE))
    ang = positions.astype(F32)[:, None] * inv
    cos, sin = jnp.cos(ang), jnp.sin(ang)
    z32, z64 = jnp.zeros((t, 32), F32), jnp.zeros((t, 64), F32)
    cpad = jnp.concatenate([cos, cos, z64], axis=1)
    s_lo = jnp.concatenate([-sin, z32, z64], axis=1)
    s_hi = jnp.concatenate([z32, sin, z64], axis=1)
    return cpad, s_lo, s_hi


def _mla_fwd(h, hn, w, post_g, next_g, rope, tq):
    cpad, s_lo, s_hi = rope
    cin = _mm(hn, w["in"], tm=1024, tn=512, tk=1024, name="mla_in")

    def lat(c, cp, sl, sh, qg, kvg):
        cq, ckv, kr = c[:, :MLA_Q_LORA], c[:, MLA_Q_LORA:MLA_Q_LORA + MLA_KV_LORA], c[:, MLA_Q_LORA + MLA_KV_LORA:]
        return _rms(cq, qg), _rms(ckv, kvg), _rope128(kr, cp, sl, sh)
    cqn, ckvn, kr = _rowwise(lat, [cin, cpad, s_lo, s_hi], [w["q_norm_g"], w["kv_norm_g"]],
                             [(MLA_Q_LORA, BF16), (MLA_KV_LORA, BF16), (LANES, BF16)], name="mla_latent")
    q_raw = _mm(cqn, w["uq"], tm=1024, tn=1024, tk=MLA_Q_LORA, name="mla_uq")

    def rope_q(q, cp, sl, sh):
        pieces = []
        for hd in range(MLA_HEADS):
            pieces.append(q[:, 256 * hd:256 * hd + LANES] * ATTN_QSCALE)
            pieces.append(_rope128(q[:, 256 * hd + LANES:256 * hd + 256], cp, sl, sh) * ATTN_QSCALE)
        return (tuple(pieces),)
    qf = _rowwise(rope_q, [q_raw, cpad, s_lo, s_hi], [], [(4096, BF16)], name="mla_rope_q")
    kvf = _mm(ckvn, w["ukv"], out_dtype=BF16, tm=1024, tn=1024, tk=MLA_KV_LORA, name="mla_ukv")
    t = h.shape[0]
    k_nope = kvf.reshape(t, MLA_HEADS, 2 * LANES)[:, :, :LANES]
    kf = jnp.concatenate([k_nope, jnp.broadcast_to(kr[:, None, :], k_nope.shape)], axis=2).reshape(t, MLA_HEADS * 2 * LANES)
    o, lse = _attn_fwd(qf, kf, kvf, tq=tq, name="mla_attn")
    mixed = _mm(o, w["out"], tm=1024, tn=1024, tk=2048, name="mla_out")
    out, a_next = _postnorm_residual(h, mixed, post_g, 1.0, next_g, "mla_postnorm")
    return out, a_next, (h, hn, cin, cqn, ckvn, qf, kf, kvf, o, lse, mixed)


def _mla_bwd(dh, saved, w, pre_g, post_g, rope, tq):
    cpad, s_lo, s_hi = rope
    h, hn, cin, cqn, ckvn, qf, kf, kvf, o, lse, mixed = saved
    dmixed, d_post = _postnorm_bwd(mixed, dh, post_g, 1.0, "mla_postnorm_bwd")
    do = _mm(dmixed, w["out"], tb=True, out_dtype=BF16, tm=1024, tn=1024, tk=1024, name="mla_out_dx")
    d_out = _mm(o, dmixed, ta=True, tm=1024, tn=1024, tk=2048, name="mla_out_dw")
    dkvf, dkr_heads, dq = _attn_bwd(qf, kf, kvf, o, do, lse, tq=tq, name="mla_attn_bwd")

    def unrope_q(d, cp, sl, sh):
        pieces = []
        for hd in range(MLA_HEADS):
            pieces.append(d[:, 256 * hd:256 * hd + LANES] * ATTN_SCALE)
            pieces.append(_rope128(d[:, 256 * hd + LANES:256 * hd + 256], cp, -sl, -sh) * ATTN_SCALE)
        return (tuple(pieces),)
    dq_raw = _rowwise(unrope_q, [dq, cpad, s_lo, s_hi], [], [(4096, BF16)], name="mla_rope_q_bwd")
    dcqn = _mm(dq_raw, w["uq"], tb=True, tm=1024, tn=256, tk=1024, name="mla_uq_dx")
    d_uq = _mm(cqn, dq_raw, ta=True, tm=256, tn=1024, tk=2048, name="mla_uq_dw")
    dckvn = _mm(dkvf, w["ukv"], tb=True, tm=1024, tn=128, tk=1024, name="mla_ukv_dx")
    d_ukv = _mm(ckvn, dkvf, ta=True, tm=128, tn=1024, tk=2048, name="mla_ukv_dw")

    def lat_bwd(c, dq_, dkv_, dkrh, cp, sl, sh, qg, kvg):
        cq, ckv = c[:, :MLA_Q_LORA], c[:, MLA_Q_LORA:MLA_Q_LORA + MLA_KV_LORA]
        dcq, dqg = _rms_bwd(cq, qg, dq_)
        dckv, dkvg = _rms_bwd(ckv, kvg, dkv_)
        dkr = dkrh[:, :LANES]
        for hd in range(1, MLA_HEADS):
            dkr = dkr + dkrh[:, hd * LANES:(hd + 1) * LANES]
        return (dcq, dckv, _rope128(dkr, cp, -sl, -sh)), _colsum(dqg), _colsum(dkvg)
    dcin, d_qg, d_kvg = _rowwise(lat_bwd, [cin, dcqn, dckvn, dkr_heads, cpad, s_lo, s_hi], [w["q_norm_g"], w["kv_norm_g"]],
                                 [(MLA_IN_PAD, BF16)], [(1, MLA_Q_LORA), (1, MLA_KV_LORA)], name="mla_latent_bwd")
    d_in = _mm(hn, dcin, ta=True, tm=1024, tn=512, tk=2048, name="mla_in_dw")
    dh_in, d_pre = _dx_prenorm_bwd([(dcin, w["in"], 0, MLA_IN_PAD)], h, dh, pre_g, "mla_in_dx_prenorm_bwd")
    return dh_in, dict(w_in=d_in, q_norm_g=d_qg, kv_norm_g=d_kvg, w_uq=d_uq, w_ukv=d_ukv, w_out=d_out,
                       pre_g=d_pre, post_g=d_post)


def _hyb_fwd(h, hn, w, post_g, next_g):
    proj = _mm(hn, w["main"], tm=1024, tn=512, tk=1024, name="hyb_in")
    dtr = _mm(hn, w["dt"], tm=1024, tn=LANES, tk=1024, name="hyb_in_dt")
    ya = _gmlp_fwd(proj, w["gm_w"], w["gm_bt"], w["gm_ln_g"], w["gm_ln_b"], "gmlp")
    xbc = proj[:, 3072:]
    xsh = [_shift_down(xbc, 3 - k) for k in range(3)] + [(proj, SSD_CONV_CH, 2)]
    act = _rowwise(lambda x0, x1, x2, x3, cw, cb: _silu(_conv_pre(x0, x1, x2, x3, cw, cb)), xsh,
                   [w["conv_w"], w["conv_b"]], [(SSD_CONV_CH, F32)], name="ssd_conv")
    dt_pad = _rowwise(lambda d, b: jax.nn.softplus(d + b), [dtr], [w["dt_bias"]], [(LANES, F32)], name="ssd_dt")
    y, states = _ssd_fwd(act, dt_pad, w["a"], w["d"], "ssd_scan")

    def gate_norm(y_, z, ng):
        yg = y_ * _silu(z)
        return ((_rms(yg[:, :512], ng[:, :512]), _rms(yg[:, 512:], ng[:, 512:])),)
    yb = _rowwise(gate_norm, [y, (proj, SSD_INNER, 2)], [w["norm_g"]], [(SSD_INNER, BF16)], name="ssd_gate_norm")
    yab = jnp.concatenate([ya, yb], axis=1)
    mixed = _mm(yab, w["out"], tm=1024, tn=1024, tk=2048, name="hyb_out")
    out, a_next = _postnorm_residual(h, mixed, post_g, 1.0, next_g, "hyb_postnorm")
    return out, a_next, (h, hn, proj, dtr, xsh, act, dt_pad, y, states, yab, mixed)


def _hyb_bwd(dh, saved, w, pre_g, post_g):
    h, hn, proj, dtr, xsh, act, dt_pad, y, states, yab, mixed = saved
    dmixed, d_post = _postnorm_bwd(mixed, dh, post_g, 1.0, "hyb_postnorm_bwd")
    dyab = _mm(dmixed, w["out"], tb=True, tm=1024, tn=1024, tk=1024, name="hyb_out_dx")
    d_out = _mm(yab, dmixed, ta=True, tm=1024, tn=1024, tk=2048, name="hyb_out_dw")

    def gate_norm_bwd(y_, z, d, ng):
        sz = _silu(z)
        yg = y_ * sz
        d_lo, g_lo = _rms_bwd(yg[:, :512], ng[:, :512], d[:, 1024:1536])
        d_hi, g_hi = _rms_bwd(yg[:, 512:], ng[:, 512:], d[:, 1536:])
        dyg = jnp.concatenate([d_lo, d_hi], axis=1)
        return dyg * sz, dyg * y_ * _silu_grad(z), _colsum(jnp.concatenate([g_lo, g_hi], axis=1))
    dy, dz, d_norm = _rowwise(gate_norm_bwd, [y, (proj, SSD_INNER, 2), dyab], [w["norm_g"]], [(SSD_INNER, F32), (SSD_INNER, BF16)],
                              [(1, SSD_INNER)], name="ssd_gate_norm_bwd")
    dact, ddt, da_sum, dd_sum = _ssd_bwd(act, dt_pad, w["a"], w["d"], states, dy, "ssd_scan_bwd")

    def conv_bwd(x0, x1, x2, x3, da_, cw, cb):
        dpre = da_ * _silu_grad(_conv_pre(x0, x1, x2, x3, cw, cb))
        dw = jnp.concatenate([_colsum(dpre * x0), _colsum(dpre * x1), _colsum(dpre * x2), _colsum(dpre * x3)], axis=0)
        return dpre, dw, _colsum(dpre)
    dconv, d_conv_w, d_conv_b = _rowwise(conv_bwd, [*xsh, dact], [w["conv_w"], w["conv_b"]], [(SSD_CONV_CH, F32)],
                                         [(4, SSD_CONV_CH), (1, SSD_CONV_CH)], name="ssd_conv_bwd")
    dsh = [_shift_up(dconv, 3 - k) for k in range(4)]
    dxbc = _rowwise(lambda d0, d1, d2, d3, cw: d0 * cw[0:1] + d1 * cw[1:2] + d2 * cw[2:3] + d3 * cw[3:4], dsh,
                    [w["conv_w"]], [(SSD_CONV_CH, BF16)], name="ssd_conv_dx")

    def dt_bwd(dd, d, b):
        g = dd * jax.nn.sigmoid(d + b)
        g = jnp.where(lax.broadcasted_iota(jnp.int32, g.shape, 1) < SSD_HEADS, g, 0.0)
        return g, _colsum(g)
    ddtr, d_dt_bias = _rowwise(dt_bwd, [ddt, dtr], [w["dt_bias"]], [(LANES, BF16)], [(1, LANES)], name="ssd_dt_bwd")
    duv, d_gm_w, d_gm_b, d_ln_g, d_ln_b = _gmlp_bwd(proj, dyab, w["gm_w"], w["gm_bt"], w["gm_ln_g"], w["gm_ln_b"],
                                                    "gmlp_bwd")
    dproj = jnp.concatenate([duv, dz, dxbc], axis=1)
    d_main = _mm(hn, dproj, ta=True, tm=1024, tn=512, tk=2048, name="hyb_in_dw")
    d_dt = _mm(hn, ddtr, ta=True, tm=1024, tn=LANES, tk=2048, name="hyb_in_dt_dw")
    dh_in, d_pre = _dx_prenorm_bwd([(dproj, w["main"], 0, 1536), (ddtr, w["dt"], 0, LANES)], h, dh, pre_g,
                                   "hyb_in_dx_prenorm_bwd")
    grads = dict(w_in=jnp.concatenate([d_main, d_dt[:, :SSD_HEADS]], axis=1), gm_ln_g=d_ln_g, gm_ln_b=d_ln_b,
                 gm_w_s=d_gm_w, gm_b_s=d_gm_b[:, :, 0], conv_w=d_conv_w, conv_b=d_conv_b,
                 dt_bias=d_dt_bias[:, :SSD_HEADS], a_log=(da_sum * w["a"])[:, :SSD_HEADS], d=dd_sum[:, :SSD_HEADS],
                 norm_g=d_norm, w_out=d_out, pre_g=d_pre, post_g=d_post)
    return dh_in, grads


def _row(v):
    return v.reshape(1, -1).astype(F32)


def _pad_lanes(v, n=LANES):
    v = _row(v)
    return jnp.pad(v, ((0, 0), (0, n - v.shape[1])))


HYB_IN = 4624
HYB_SHARD = HYB_IN // N_DEV
HYB_SHARD_PAD = 640


def _hyb_unblock_matrix():
    n = N_DEV * HYB_SHARD_PAD
    r = lax.broadcasted_iota(jnp.int32, (n, n), 0)
    c = lax.broadcasted_iota(jnp.int32, (n, n), 1)
    j = r % HYB_SHARD_PAD
    return jnp.logical_and(j < HYB_SHARD, c == HYB_SHARD * (r // HYB_SHARD_PAD) + j).astype(BF16)


def _layer_weights(fw, sm, i):
    j = i // 2
    lw = dict(
        ffn1=dict({"in": fw["ffn1_w_in"][i], "down": fw["ffn1_w_down"][i]}),
        ffn2=dict({"in": fw["ffn2_w_in"][i], "down": fw["ffn2_w_down"][i]}),
        ple=dict(gate=fw["ple_w_gate"][i], proj=fw["ple_w_proj"][i]),
    )
    if i % 2 == 0:
        w_in = _mm(fw["hyb_w_in"][j], _hyb_unblock_matrix(), out_dtype=BF16, tm=1024, tn=512, tk=1024, name="hyb_w_unblock")
        causal = jnp.tril(jnp.ones((CHUNK, CHUNK), dtype=bool))
        lw["mix"] = {
            "main": w_in[:, :HYB_MAIN], "dt": w_in[:, HYB_MAIN:HYB_MAIN + LANES],
            "gm_w": jnp.where(causal[None], sm["gm_w_s"][j], 0.0).astype(BF16),
            "gm_bt": jnp.pad(sm["gm_b_s"][j].T, ((0, 0), (0, LANES - GM_HEADS))),
            "gm_ln_g": _row(sm["gm_ln_g"][j]), "gm_ln_b": _row(sm["gm_ln_b"][j]),
            "conv_w": fw["ssd_conv_w"][j], "conv_b": _row(sm["ssd_conv_b"][j]),
            "dt_bias": _pad_lanes(sm["ssd_dt_bias"][j]), "a": _pad_lanes(-jnp.exp(sm["ssd_a_log"][j])),
            "d": _pad_lanes(sm["ssd_d"][j]), "norm_g": _row(sm["ssd_norm_g"][j]), "out": fw["hyb_w_out"][j],
        }
    else:
        uq = fw["mla_w_uq"][j].reshape(MLA_Q_LORA, MLA_HEADS, 192)
        uq = jnp.pad(uq, ((0, 0), (0, 0), (0, 64))).reshape(MLA_Q_LORA, MLA_HEADS * 256)
        lw["mix"] = {
            "in": jnp.pad(fw["mla_w_in"][j], ((0, 0), (0, MLA_IN_PAD - MLA_IN))), "uq": uq, "ukv": fw["mla_w_ukv"][j],
            "out": fw["mla_w_out"][j], "q_norm_g": _row(fw["mla_q_norm_g"][j]), "kv_norm_g": _row(sm["mla_kv_norm_g"][j]),
        }
    return lw


def _device_step(x, p, positions, target, fw, sm):
    t = x.shape[0]
    tq = _pick(t, (1024, 512, 256, 128))
    rope = _rope_tables(positions)
    h = x
    saved, lws = [], []
    a = None
    for i in range(DEPTH):
        lw = _layer_weights(fw, sm, i)
        lws.append(lw)
        after = _row(sm["ffn1_pre_g"][i + 1]) if i + 1 < DEPTH else None
        h, a, s1 = _ffn_fwd(h, a, lw["ffn1"], _row(sm["ffn1_pre_g"][i]), _row(sm["ffn1_post_g"][i]),
                            _row(sm["mix_pre_g"][i]), "ffn")
        if i % 2 == 0:
            h, a, s2 = _hyb_fwd(h, a, lw["mix"], _row(sm["mix_post_g"][i]), _row(sm["ffn2_pre_g"][i]))
        else:
            h, a, s2 = _mla_fwd(h, a, lw["mix"], _row(sm["mix_post_g"][i]), _row(sm["ffn2_pre_g"][i]), rope, tq)
        h, a, s3 = _ffn_fwd(h, a, lw["ffn2"], _row(sm["ffn2_pre_g"][i]), _row(sm["ffn2_post_g"][i]),
                            _row(sm["ple_pre_g"][i]), "ffn")
        h, a, s4 = _ple_fwd(h, a, p[i], lw["ple"], _row(sm["ple_post_g"][i]), after)
        saved.append((s1, s2, s3, s4))

    def loss_fn(y, tg):
        err = y - tg
        return err * (1.0 / D_MODEL), jnp.sum(_colsum(err * err), axis=1, keepdims=True)
    dh, loss_sum = _rowwise(loss_fn, [h, target], [], [(D_MODEL, F32)], [(1, 1)], name="loss")
    loss = loss_sum[0, 0] * (0.5 / D_MODEL)

    per_layer = {n: [None] * DEPTH for n in WEIGHTS if n.startswith(("ffn", "mix", "ple"))}
    per_mixer = {n: [None] * (DEPTH // 2) for n in WEIGHTS if n.startswith(("hyb", "gm", "ssd", "mla"))}
    for i in reversed(range(DEPTH)):
        lw = lws[i]
        s1, s2, s3, s4 = saved[i]
        j = i // 2
        dh, g = _ple_bwd(dh, s4, p[i], lw["ple"], _row(sm["ple_pre_g"][i]), _row(sm["ple_post_g"][i]))
        for k, v in g.items():
            per_layer["ple_" + k][i] = v
        dh, g = _ffn_bwd(dh, s3, lw["ffn2"], _row(sm["ffn2_pre_g"][i]), _row(sm["ffn2_post_g"][i]), "ffn")
        for k, v in g.items():
            per_layer["ffn2_" + k][i] = v
        if i % 2 == 0:
            dh, g = _hyb_bwd(dh, s2, lw["mix"], _row(sm["mix_pre_g"][i]), _row(sm["mix_post_g"][i]))
            names = dict(w_in="hyb_w_in", gm_ln_g="gm_ln_g", gm_ln_b="gm_ln_b", gm_w_s="gm_w_s", gm_b_s="gm_b_s",
                         conv_w="ssd_conv_w", conv_b="ssd_conv_b", dt_bias="ssd_dt_bias", a_log="ssd_a_log", d="ssd_d",
                         norm_g="ssd_norm_g", w_out="hyb_w_out")
        else:
            dh, g = _mla_bwd(dh, s2, lw["mix"], _row(sm["mix_pre_g"][i]), _row(sm["mix_post_g"][i]), rope, tq)
            g["w_in"] = g["w_in"][:, :MLA_IN]
            g["w_uq"] = g["w_uq"].reshape(MLA_Q_LORA, MLA_HEADS, 256)[:, :, :192].reshape(MLA_Q_LORA, MLA_HEADS * 192)
            names = dict(w_in="mla_w_in", q_norm_g="mla_q_norm_g", kv_norm_g="mla_kv_norm_g", w_uq="mla_w_uq",
                         w_ukv="mla_w_ukv", w_out="mla_w_out")
        per_layer["mix_pre_g"][i] = g.pop("pre_g")
        per_layer["mix_post_g"][i] = g.pop("post_g")
        for k, v in g.items():
            per_mixer[names[k]][j] = v
        dh, g = _ffn_bwd(dh, s1, lw["ffn1"], _row(sm["ffn1_pre_g"][i]), _row(sm["ffn1_post_g"][i]), "ffn")
        for k, v in g.items():
            per_layer["ffn1_" + k][i] = v

    return loss, dh, {**per_layer, **per_mixer}


def _stack_layers(parts, shape):
    return jnp.stack(parts, axis=0).reshape(shape)


MESH_AXES = ("x", "y", "c")
EXCHANGE_MAX_COPIES = 56


def _exchange(src, axes, mode, name):
    n = 2 ** len(axes)
    blk = src.shape[-2:]
    flips = [tuple(a for a, bit in zip(axes, np.binary_repr(f, len(axes))) if bit == "1") for f in range(1, n)]
    prefs = tuple(c for c in (16, 8, 4, 2, 1) if c * (n - 1) <= EXCHANGE_MAX_COPIES)
    pieces = _pick(blk[0] // 16, prefs) if blk[0] % 16 == 0 else 1
    rows = blk[0] // pieces

    def index(where):
        idx = 0
        for a in axes:
            idx = idx * 2 + where[a]
        return idx

    me_out = index({a: lax.axis_index(a) for a in MESH_AXES})
    own = lax.dynamic_index_in_dim(src, me_out, 0, keepdims=False) if mode == "a2a" else src
    landing = lax.dynamic_update_index_in_dim(lax.empty((n, *blk), src.dtype), own, me_out, 0)

    def body(src_ref, landing_ref, out_ref, send_sems, recv_sems):
        del landing_ref
        pos = {a: lax.axis_index(a) for a in MESH_AXES}
        me = index(pos)
        copies = []
        for k, flip in enumerate(flips):
            peer = {a: (1 - pos[a]) if a in flip else pos[a] for a in MESH_AXES}
            payload = src_ref.at[index(peer)] if mode == "a2a" else src_ref
            for q in range(pieces):
                part = pl.ds(q * rows, rows)
                cp = pltpu.make_async_remote_copy(
                    src_ref=payload.at[part], dst_ref=out_ref.at[me, part], send_sem=send_sems.at[k * pieces + q],
                    recv_sem=recv_sems.at[k * pieces + q], device_id=(peer["x"], peer["y"], peer["c"]),
                    device_id_type=pl.DeviceIdType.MESH)
                cp.start()
                copies.append(cp)
        for cp in copies:
            cp.wait()

    n_sems = (n - 1) * pieces
    return pl.pallas_call(
        body, name=name, in_specs=[pl.BlockSpec(memory_space=pl.ANY), pl.BlockSpec(memory_space=pl.ANY)],
        out_specs=pl.BlockSpec(memory_space=pl.ANY), out_shape=jax.ShapeDtypeStruct((n, *blk), src.dtype),
        input_output_aliases={1: 0},
        scratch_shapes=[pltpu.SemaphoreType.DMA((n_sems,)), pltpu.SemaphoreType.DMA((n_sems,))],
    )(src, landing)


def _pack_rows(n_elems):
    return -(-n_elems // (16 * PACK_W)) * 16


def _pack(parts, lead=()):
    nl = len(lead)
    rows = []
    for a in parts:
        flat = a.reshape(*lead, -1)
        r = _pack_rows(flat.shape[-1])
        flat = jnp.pad(flat, [(0, 0)] * nl + [(0, r * PACK_W - flat.shape[-1])])
        rows.append(flat.reshape(*lead, r, PACK_W))
    total = sum(r.shape[nl] for r in rows)
    pad = -total % PACK_TM
    if pad:
        rows.append(jnp.zeros((*lead, pad, PACK_W), rows[0].dtype))
    return jnp.concatenate(rows, axis=nl)


def _unpack(buf, shapes, lead=()):
    nl = len(lead)
    out, r0 = [], 0
    for shp in shapes:
        n = int(np.prod(shp))
        r = _pack_rows(n)
        piece = lax.slice_in_dim(buf, r0, r0 + r, axis=nl).reshape(*lead, r * PACK_W)
        out.append(lax.slice_in_dim(piece, 0, n, axis=nl).reshape(*lead, *shp))
        r0 += r
    return out


def _split_for_devices(g, axis):
    if isinstance(g, tuple):
        return jnp.concatenate([_split_for_devices_n(h, axis, N_DEV // len(g)) for h in g], axis=0)
    return _split_for_devices_n(g, axis, N_DEV)


def _split_for_devices_n(g, axis, n):
    shp = g.shape
    g = g.reshape(*shp[:axis], n, shp[axis] // n, *shp[axis + 1:])
    return jnp.moveaxis(g, axis, 0)


def _join_from_devices(parts, axis):
    parts = jnp.moveaxis(parts, 0, axis)
    shp = parts.shape
    return parts.reshape(*shp[:axis], shp[axis] * shp[axis + 1], *shp[axis + 2:])


def _adamw_terms(w, g, m, v):
    m = ADAM_B1 * m + (1.0 - ADAM_B1) * g
    v = ADAM_B2 * v + (1.0 - ADAM_B2) * (g * g)
    m_hat = m / (1.0 - ADAM_B1 ** ADAM_STEP)
    v_hat = v / (1.0 - ADAM_B2 ** ADAM_STEP)
    delta = -ADAM_LR * (m_hat / (jnp.sqrt(v_hat) + ADAM_EPS) + ADAM_WD * w)
    return delta, m, v


def _adamw_packed(w, m, v, partials, n_partials, name):
    def fn(w_, m_, v_, *parts):
        g = parts[0].astype(F32)
        for part in parts[1:]:
            g = g + part.astype(F32)
        return (g,) + _adamw_terms(w_, g, m_, v_)
    return _rowwise(fn, [w, m, v] + [(partials, s) for s in range(n_partials)], [], [(PACK_W, F32)] * 4,
                    tm=PACK_TM, name=name)


def kernel(x, p, positions, ffn1_pre_g, ffn1_w_in, ffn1_w_down, ffn1_post_g, mix_pre_g, mix_post_g, ffn2_pre_g, ffn2_w_in, ffn2_w_down, ffn2_post_g, ple_pre_g, ple_w_gate, ple_w_proj, ple_post_g, hyb_w_in, gm_ln_g, gm_ln_b, gm_w_s, gm_b_s, ssd_conv_w, ssd_conv_b, ssd_dt_bias, ssd_a_log, ssd_d, ssd_norm_g, hyb_w_out, mla_w_in, mla_q_norm_g, mla_kv_norm_g, mla_w_uq, mla_w_ukv, mla_w_out, loss_target, m_ffn1_pre_g, m_ffn1_w_in, m_ffn1_w_down, m_ffn1_post_g, m_mix_pre_g, m_mix_post_g, m_ffn2_pre_g, m_ffn2_w_in, m_ffn2_w_down, m_ffn2_post_g, m_ple_pre_g, m_ple_w_gate, m_ple_w_proj, m_ple_post_g, m_hyb_w_in, m_gm_ln_g, m_gm_ln_b, m_gm_w_s, m_gm_b_s, m_ssd_conv_w, m_ssd_conv_b, m_ssd_dt_bias, m_ssd_a_log, m_ssd_d, m_ssd_norm_g, m_hyb_w_out, m_mla_w_in, m_mla_q_norm_g, m_mla_kv_norm_g, m_mla_w_uq, m_mla_w_ukv, m_mla_w_out, v_ffn1_pre_g, v_ffn1_w_in, v_ffn1_w_down, v_ffn1_post_g, v_mix_pre_g, v_mix_post_g, v_ffn2_pre_g, v_ffn2_w_in, v_ffn2_w_down, v_ffn2_post_g, v_ple_pre_g, v_ple_w_gate, v_ple_w_proj, v_ple_post_g, v_hyb_w_in, v_gm_ln_g, v_gm_ln_b, v_gm_w_s, v_gm_b_s, v_ssd_conv_w, v_ssd_conv_b, v_ssd_dt_bias, v_ssd_a_log, v_ssd_d, v_ssd_norm_g, v_hyb_w_out, v_mla_w_in, v_mla_q_norm_g, v_mla_kv_norm_g, v_mla_w_uq, v_mla_w_ukv, v_mla_w_out):
    given = dict(locals())
    w = {n: given[n] for n in WEIGHTS}
    mom = {n: given["m_" + n] for n in WEIGHTS}
    var = {n: given["v_" + n] for n in WEIGHTS}
    shard_shapes = [w[n].shape for n in SHARDED]
    repl_shapes = [w[n].shape for n in REPLICATED]

    send16 = {n: w[n].astype(BF16) for n in SHARDED_BF16}
    send16["hyb_w_in"] = jnp.pad(send16["hyb_w_in"], ((0, 0), (0, 0), (0, HYB_SHARD_PAD - HYB_SHARD)))
    pack16 = _pack([send16[n] for n in SHARDED_BF16])
    by_chip = _exchange(pack16, ("x", "y"), "gather", "gather_weights_ici")
    by_core = _exchange(by_chip.reshape(-1, PACK_W), ("c",), "gather", "gather_weights_d2d")
    gathered = by_core.reshape(2, 4, -1, PACK_W).transpose(1, 0, 2, 3).reshape(N_DEV, -1, PACK_W)
    fw = {n: _join_from_devices(a, SHARD_AXIS[n])
          for n, a in zip(SHARDED_BF16, _unpack(gathered, [send16[n].shape for n in SHARDED_BF16], (N_DEV,)))}
    small = _exchange(_pack([w[n] for n in SHARDED_F32]), MESH_AXES, "gather", "gather_weights_f32")
    fw.update({n: _join_from_devices(a, SHARD_AXIS[n])
               for n, a in zip(SHARDED_F32, _unpack(small, [w[n].shape for n in SHARDED_F32], (N_DEV,)))})

    loss_local, grad_x, grads = _device_step(x[0], p[:, 0], positions[0], loss_target[0], fw, w)
    loss = lax.psum(loss_local, MESH_AXES)

    per_dev = []
    for n in SHARDED:
        layers = w[n].shape[0]
        whole = (1, *w[n].shape[1:SHARD_AXIS[n]], N_DEV * w[n].shape[SHARD_AXIS[n]], *w[n].shape[SHARD_AXIS[n] + 1:])
        if int(np.prod(w[n].shape[1:])) % (16 * PACK_W) == 0:
            parts = [tuple(h[None] for h in g) if isinstance(g, tuple) else g.reshape(whole) for g in grads[n]]
        else:
            parts = [_stack_layers(grads[n], (layers, *whole[1:]))]
        per_dev.extend(_split_for_devices(g, SHARD_AXIS[n]) for g in parts)
    per_dev = [a.reshape(4, 2, *a.shape[1:]).swapaxes(0, 1) for a in per_dev]
    gpack = _pack(per_dev, (2, 4))
    rows = gpack.shape[2]
    pair = _exchange(gpack.reshape(2, 4 * rows, PACK_W), ("c",), "a2a", "reduce_grads_d2d")
    chip_sum = _rowwise(lambda a, b: a + b, [(pair, 0), (pair, 1)], [], [(PACK_W, BF16)], tm=PACK_TM, name="reduce_grads_pair")
    quads = _exchange(chip_sum.reshape(4, rows, PACK_W), ("x", "y"), "a2a", "reduce_grads_ici")
    g_s, d_s, m_s, v_s = _adamw_packed(_pack([w[n] for n in SHARDED]), _pack([mom[n] for n in SHARDED]),
                                       _pack([var[n] for n in SHARDED]), quads, 4, "adamw_sharded")

    rpack = _pack([_stack_layers(grads[n], w[n].shape) for n in REPLICATED])
    everyone = _exchange(rpack, MESH_AXES, "gather", "gather_small_grads")
    g_r, d_r, m_r, v_r = _adamw_packed(_pack([w[n] for n in REPLICATED]), _pack([mom[n] for n in REPLICATED]),
                                       _pack([var[n] for n in REPLICATED]), everyone, N_DEV, "adamw_replicated")

    outs = []
    for sharded_buf, repl_buf in ((g_s, g_r), (d_s, d_r), (m_s, m_r), (v_s, v_r)):
        vals = dict(zip(SHARDED, _unpack(sharded_buf, shard_shapes)))
        vals.update(zip(REPLICATED, _unpack(repl_buf, repl_shapes)))
        outs.extend(vals[n] for n in WEIGHTS)
    return (loss, grad_x[None], *outs)
```

```python
import functools
import math

import jax
import jax.numpy as jnp
import numpy as np
from jax import lax
from jax.experimental import pallas as pl
from jax.experimental.pallas import tpu as pltpu

F32 = jnp.float32
BF16 = jnp.bfloat16
HIGHEST = lax.Precision.HIGHEST

V7X_VMEM_LIMIT_BYTES = 52 * 1024 * 1024
LANES = 128

D_MODEL = 1024
DEPTH = 4
D_FF = 2816
PLE_DIM = 256
NORM_EPS = 1e-6
LN_EPS = 1e-5
CHUNK = 128
GM_HEADS = 8
SSD_HEADS = 16
SSD_HEAD_DIM = 64
SSD_INNER = 1024
SSD_STATE = 128
SSD_BC = 256
SSD_CONV_CH = 1536
HYB_MAIN = 4608
MLA_HEADS = 16
MLA_Q_LORA = 256
MLA_KV_LORA = 128
MLA_ROPE = 64
MLA_IN = 448
MLA_IN_PAD = 512
ATTN_SCALE = 192.0 ** -0.5
LOG2_E = 1.4426950408889634
LN_2 = 0.6931471805599453
ATTN_QSCALE = ATTN_SCALE * LOG2_E
ROPE_BASE = 10000.0

ADAM_LR = 0.001
ADAM_B1 = 0.9
ADAM_B2 = 0.999
ADAM_EPS = 1e-08
ADAM_WD = 0.01
ADAM_STEP = 10

N_DEV = 8
PACK_W = 1024
PACK_TM = 256

WEIGHTS = ['ffn1_pre_g', 'ffn1_w_in', 'ffn1_w_down', 'ffn1_post_g', 'mix_pre_g', 'mix_post_g', 'ffn2_pre_g',
           'ffn2_w_in', 'ffn2_w_down', 'ffn2_post_g', 'ple_pre_g', 'ple_w_gate', 'ple_w_proj', 'ple_post_g',
           'hyb_w_in', 'gm_ln_g', 'gm_ln_b', 'gm_w_s', 'gm_b_s', 'ssd_conv_w', 'ssd_conv_b', 'ssd_dt_bias',
           'ssd_a_log', 'ssd_d', 'ssd_norm_g', 'hyb_w_out', 'mla_w_in', 'mla_q_norm_g', 'mla_kv_norm_g',
           'mla_w_uq', 'mla_w_ukv', 'mla_w_out']
SHARD_AXIS = {'ffn1_w_in': 2, 'ffn1_w_down': 1, 'ffn2_w_in': 2, 'ffn2_w_down': 1, 'ple_w_gate': 1, 'ple_w_proj': 2,
              'hyb_w_in': 2, 'ssd_conv_w': 2, 'hyb_w_out': 1, 'mla_w_in': 1, 'mla_q_norm_g': 1, 'mla_w_uq': 2,
              'mla_w_ukv': 2, 'mla_w_out': 1}
SHARDED = [n for n in WEIGHTS if n in SHARD_AXIS]
REPLICATED = [n for n in WEIGHTS if n not in SHARD_AXIS]
SHARDED_F32 = ['ssd_conv_w', 'mla_q_norm_g']
SHARDED_BF16 = [n for n in SHARDED if n not in SHARDED_F32]


def _params(*sem):
    return pltpu.CompilerParams(dimension_semantics=sem or None, vmem_limit_bytes=V7X_VMEM_LIMIT_BYTES)


def _pick(n, prefs):
    for t in prefs:
        if t <= n and n % t == 0:
            return t
    return n


def _mm(a, b, *, ta=False, tb=False, out_dtype=F32, tm=1024, tn=512, tk=512, name):
    m, k = (a.shape[1], a.shape[0]) if ta else a.shape
    n = b.shape[0] if tb else b.shape[1]
    assert k == (b.shape[1] if tb else b.shape[0]), (a.shape, b.shape, ta, tb)
    tm, tn, tk = _pick(m, (tm, 512, 256, 128)), _pick(n, (tn, 512, 256, 128)), _pick(k, (tk, 512, 256, 128))
    nk = k // tk
    dims = (((0 if ta else 1,), (1 if tb else 0,)), ((), ()))

    def body(a_ref, b_ref, o_ref, *acc):
        part = lax.dot_general(a_ref[...].astype(BF16), b_ref[...].astype(BF16), dims, preferred_element_type=F32)
        if nk == 1:
            o_ref[...] = part.astype(o_ref.dtype)
            return
        acc_ref, = acc
        kk = pl.program_id(2)

        @pl.when(kk == 0)
        def _():
            acc_ref[...] = part

        @pl.when(kk > 0)
        def _():
            acc_ref[...] += part

        @pl.when(kk == nk - 1)
        def _():
            o_ref[...] = acc_ref[...].astype(o_ref.dtype)

    a_spec = pl.BlockSpec((tk, tm), lambda i, j, kk: (kk, i)) if ta else pl.BlockSpec((tm, tk), lambda i, j, kk: (i, kk))
    b_spec = pl.BlockSpec((tn, tk), lambda i, j, kk: (j, kk)) if tb else pl.BlockSpec((tk, tn), lambda i, j, kk: (kk, j))
    return pl.pallas_call(
        body, name=name, grid=(m // tm, n // tn, nk), in_specs=[a_spec, b_spec],
        out_specs=pl.BlockSpec((tm, tn), lambda i, j, kk: (i, j)), out_shape=jax.ShapeDtypeStruct((m, n), out_dtype),
        scratch_shapes=[] if nk == 1 else [pltpu.VMEM((tm, tn), F32)],
        compiler_params=_params("parallel", "parallel", "arbitrary"),
    )(a, b)


def _rowwise(fn, rows, consts, outs, accs=(), *, tm=256, name):
    first = rows[0][0] if isinstance(rows[0], tuple) else rows[0]
    t = first.shape[-2]
    tm = _pick(t, (tm, 256, 128, 64, 32, 16, 8))
    n_r, n_c, n_o = len(rows), len(consts), len(outs)

    def body(*refs):
        vals = [r[...] for r in refs[:n_r + n_c]]
        res = fn(*vals)
        res = res if isinstance(res, tuple) else (res,)
        o_refs, a_refs = refs[n_r + n_c:n_r + n_c + n_o], refs[n_r + n_c + n_o:]
        for o_ref, v in zip(o_refs, res[:n_o]):
            if isinstance(v, (tuple, list)):
                off = 0
                for piece in v:
                    o_ref[:, off:off + piece.shape[1]] = piece.astype(o_ref.dtype)
                    off += piece.shape[1]
            else:
                o_ref[...] = v.astype(o_ref.dtype)
        if a_refs:
            terms = res[n_o:]
            is_first = pl.program_id(0) == 0

            @pl.when(is_first)
            def _():
                for a_ref, v in zip(a_refs, terms):
                    a_ref[...] = v

            @pl.when(jnp.logical_not(is_first))
            def _():
                for a_ref, v in zip(a_refs, terms):
                    a_ref[...] += v

    in_specs, args = [], []
    for r in rows:
        if isinstance(r, tuple) and len(r) == 3:
            arr, width, cb = r
            in_specs.append(pl.BlockSpec((tm, width), functools.partial(lambda i, c: (i, c), c=cb)))
        elif isinstance(r, tuple):
            arr, slot = r
            in_specs.append(pl.BlockSpec((None, tm, arr.shape[2]), functools.partial(lambda i, s: (s, i, 0), s=slot)))
        else:
            arr = r
            in_specs.append(pl.BlockSpec((tm, arr.shape[1]), lambda i: (i, 0)))
        args.append(arr)
    for c in consts:
        in_specs.append(pl.BlockSpec(c.shape, lambda i: (0, 0)))
        args.append(c)
    out_specs = [pl.BlockSpec((tm, c), lambda i: (i, 0)) for c, _ in outs]
    out_shape = [jax.ShapeDtypeStruct((t, c), dt) for c, dt in outs]
    for shp in accs:
        out_specs.append(pl.BlockSpec(shp, lambda i: (0, 0)))
        out_shape.append(jax.ShapeDtypeStruct(shp, F32))
    res = pl.pallas_call(
        body, name=name, grid=(t // tm,), in_specs=in_specs, out_specs=out_specs, out_shape=out_shape,
        compiler_params=_params("arbitrary" if accs else "parallel"),
    )(*args)
    return res[0] if len(res) == 1 else tuple(res)


def _colsum(v):
    return jnp.sum(v, axis=0, keepdims=True)


def _rms(x, g, eps=NORM_EPS):
    r = lax.rsqrt(jnp.mean(x * x, axis=-1, keepdims=True) + eps)
    return x * r * g


def _rms_bwd(x, g, dy, eps=NORM_EPS):
    r = lax.rsqrt(jnp.mean(x * x, axis=-1, keepdims=True) + eps)
    xh = x * r
    dyg = dy * g
    dx = r * (dyg - xh * jnp.mean(dyg * xh, axis=-1, keepdims=True))
    return dx, dy * xh


def _silu(x):
    return x * jax.nn.sigmoid(x)


def _silu_grad(x):
    s = jax.nn.sigmoid(x)
    return s * (1.0 + x * (1.0 - s))


_GELU_K = math.sqrt(2.0 / math.pi)


def _gelu(x):
    return 0.5 * x * (1.0 + jnp.tanh(_GELU_K * (x + 0.044715 * x * x * x)))


def _gelu_grad(x):
    t = jnp.tanh(_GELU_K * (x + 0.044715 * x * x * x))
    return 0.5 * (1.0 + t) + 0.5 * x * (1.0 - t * t) * _GELU_K * (1.0 + 3.0 * 0.044715 * x * x)


def _prenorm(h, g, name):
    return _rowwise(lambda x, gg: _rms(x, gg), [h], [g], [(D_MODEL, BF16)], name=name)


def _postnorm_residual(h, f, g, scale, next_g, name):
    if next_g is None:
        return _rowwise(lambda x, ff, gg: x + scale * _rms(ff, gg), [h, f], [g], [(D_MODEL, F32)], name=name), None

    def fn(x, ff, gg, ng):
        out = x + scale * _rms(ff, gg)
        return out, _rms(out, ng)
    return _rowwise(fn, [h, f], [g, next_g], [(D_MODEL, F32), (D_MODEL, BF16)], name=name + "_prenorm")


def _postnorm_bwd(f, dh, g, scale, name):
    def fn(ff, d, gg):
        dx, dgt = _rms_bwd(ff, gg, scale * d)
        return dx, _colsum(dgt)
    return _rowwise(fn, [f, dh], [g], [(D_MODEL, BF16)], [(1, D_MODEL)], name=name)


FFN_TM = 1024
FFN_TN = 256


def _ffn_in_swiglu(a, w_in, name):
    t = a.shape[0]
    tm = _pick(t, (FFN_TM, 512, 256, 128))
    nj = D_FF // FFN_TN

    def body(a_ref, wg_ref, wu_ref, gate_ref, up_ref, s_ref):
        av = a_ref[...]
        gate = jnp.dot(av, wg_ref[...], preferred_element_type=F32)
        up = jnp.dot(av, wu_ref[...], preferred_element_type=F32)
        gate_ref[...] = gate
        up_ref[...] = up
        s_ref[...] = (_silu(gate) * up).astype(s_ref.dtype)

    tile = pl.BlockSpec((tm, FFN_TN), lambda i, j: (i, j))
    return pl.pallas_call(
        body, name=name, grid=(t // tm, nj),
        in_specs=[pl.BlockSpec((tm, D_MODEL), lambda i, j: (i, 0)), pl.BlockSpec((D_MODEL, FFN_TN), lambda i, j: (0, j)),
                  pl.BlockSpec((D_MODEL, FFN_TN), lambda i, j: (0, j + nj))],
        out_specs=[tile, tile, tile],
        out_shape=[jax.ShapeDtypeStruct((t, D_FF), F32), jax.ShapeDtypeStruct((t, D_FF), F32),
                   jax.ShapeDtypeStruct((t, D_FF), BF16)],
        compiler_params=_params("parallel", "parallel"),
    )(a, w_in, w_in)


def _ffn_down_dx_swiglu(df, w_down, gate, up, name):
    t = df.shape[0]
    tm = _pick(t, (FFN_TM, 512, 256, 128))

    def body(df_ref, wd_ref, gate_ref, up_ref, dgate_ref, dup_ref):
        ds = lax.dot_general(df_ref[...], wd_ref[...], (((1,), (1,)), ((), ())), preferred_element_type=F32)
        gate = gate_ref[...]
        sg = jax.nn.sigmoid(gate)
        dgate_ref[...] = (ds * up_ref[...] * (sg * (1.0 + gate * (1.0 - sg)))).astype(dgate_ref.dtype)
        dup_ref[...] = (ds * (gate * sg)).astype(dup_ref.dtype)

    tile = pl.BlockSpec((tm, FFN_TN), lambda i, j: (i, j))
    return pl.pallas_call(
        body, name=name, grid=(t // tm, D_FF // FFN_TN),
        in_specs=[pl.BlockSpec((tm, D_MODEL), lambda i, j: (i, 0)), pl.BlockSpec((FFN_TN, D_MODEL), lambda i, j: (j, 0)),
                  tile, tile],
        out_specs=[tile, tile],
        out_shape=[jax.ShapeDtypeStruct((t, D_FF), BF16), jax.ShapeDtypeStruct((t, D_FF), BF16)],
        compiler_params=_params("parallel", "parallel"),
    )(df, w_down, gate, up)


def _ffn_fwd(h, a, w, pre_g, post_g, next_g, tag):
    if a is None:
        a = _prenorm(h, pre_g, tag + "_prenorm")
    gate, up, s = _ffn_in_swiglu(a, w["in"], tag + "_in_swiglu")
    f = _mm(s, w["down"], tm=1024, tn=1024, tk=D_FF, name=tag + "_down")
    out, a_next = _postnorm_residual(h, f, post_g, 0.5, next_g, tag + "_postnorm")
    return out, a_next, (h, a, gate, up, s, f)


DX_TM = 512


def _dx_prenorm_bwd(parts, h, dh, pre_g, name):
    t = h.shape[0]
    tm = _pick(t, (DX_TM, 256, 128))
    counts = [d.shape[1] // tk for d, _, _, tk in parts]
    starts = [sum(counts[:p]) for p in range(len(parts))]
    nk = sum(counts)
    n_p = len(parts)
    windows, window_of = [], []
    for (d, w, k0, tk), start, count in zip(parts, starts, counts):
        assert d.shape[1] % tk == 0, (d.shape, tk)
        if w is None:
            pw, pb, ptk, ps, pc = windows[-1]
            assert ptk == tk
            windows[-1] = (pw, pb, ptk, ps, pc + count)
        else:
            assert k0 % tk == 0, (k0, tk)
            windows.append((w, k0 // tk, tk, start, count))
        window_of.append(len(windows) - 1)

    def body(*refs):
        h_ref, dh_ref, g_ref, out_ref, dg_ref, acc_ref = refs[n_p + len(windows):]
        i, kk = pl.program_id(0), pl.program_id(1)
        for p in range(n_p):
            d_ref, w_ref = refs[p], refs[n_p + window_of[p]]

            def contribution(d_ref=d_ref, w_ref=w_ref):
                return lax.dot_general(d_ref[...], w_ref[...], (((1,), (1,)), ((), ())), preferred_element_type=F32)

            lo = starts[p] + (1 if p == 0 else 0)
            if p == 0:
                @pl.when(kk == 0)
                def _(contribution=contribution):
                    acc_ref[...] = contribution()

            if starts[p] + counts[p] > lo:
                @pl.when(jnp.logical_and(kk >= lo, kk < starts[p] + counts[p]))
                def _(contribution=contribution):
                    acc_ref[...] += contribution()

        @pl.when(kk == nk - 1)
        def _():
            dx, dgt = _rms_bwd(h_ref[...], g_ref[...], acc_ref[...])
            out_ref[...] = dh_ref[...] + dx
            dg = _colsum(dgt)

            @pl.when(i == 0)
            def _():
                dg_ref[...] = dg

            @pl.when(i > 0)
            def _():
                dg_ref[...] += dg

    def walk(start, count):
        return functools.partial(lambda kk, s, c: jnp.clip(kk - s, 0, c - 1), s=start, c=count)

    in_specs, args = [], []
    for (d, _, _, tk), start, count in zip(parts, starts, counts):
        in_specs.append(pl.BlockSpec((tm, tk), functools.partial(lambda i, kk, st: (i, st(kk)), st=walk(start, count))))
        args.append(d)
    for w, b0, tk, start, count in windows:
        in_specs.append(pl.BlockSpec((D_MODEL, tk), functools.partial(lambda i, kk, st, b0: (0, b0 + st(kk)),
                                                                      st=walk(start, count), b0=b0)))
        args.append(w)
    row = pl.BlockSpec((tm, D_MODEL), lambda i, kk: (i, 0))
    vec = pl.BlockSpec((1, D_MODEL), lambda i, kk: (0, 0))
    return pl.pallas_call(
        body, name=name, grid=(t // tm, nk), in_specs=in_specs + [row, row, vec], out_specs=[row, vec],
        out_shape=[jax.ShapeDtypeStruct((t, D_MODEL), F32), jax.ShapeDtypeStruct((1, D_MODEL), F32)],
        scratch_shapes=[pltpu.VMEM((tm, D_MODEL), F32)],
        compiler_params=_params("arbitrary", "arbitrary"),
    )(*args, h, dh, pre_g)


def _ffn_bwd(dh, saved, w, pre_g, post_g, tag):
    h, a, gate, up, s, f = saved
    df, d_post = _postnorm_bwd(f, dh, post_g, 0.5, tag + "_postnorm_bwd")
    dgate, dup = _ffn_down_dx_swiglu(df, w["down"], gate, up, tag + "_down_dx_swiglu")
    d_down = _mm(s, df, ta=True, tm=1408, tn=1024, tk=1024, name=tag + "_down_dw")
    d_in = (_mm(a, dgate, ta=True, tm=1024, tn=1408, tk=1024, name=tag + "_in_dw_gate"),
            _mm(a, dup, ta=True, tm=1024, tn=1408, tk=1024, name=tag + "_in_dw_up"))
    dh_in, d_pre = _dx_prenorm_bwd([(dgate, w["in"], 0, 1408), (dup, None, None, 1408)], h, dh, pre_g,
                                   tag + "_in_dx_prenorm_bwd")
    return dh_in, dict(w_in=d_in, w_down=d_down, pre_g=d_pre, post_g=d_post)


def _ple_fwd(h, a, p_i, w, post_g, next_g):
    gl = _mm(a, w["gate"], tm=1024, tn=1024, tk=1024, name="ple_gate")
    e = _mm(p_i, w["proj"], tm=1024, tn=1024, tk=PLE_DIM, name="ple_proj")
    if next_g is None:
        out = _rowwise(lambda x, g_, e_, gg: x + _rms(jax.nn.sigmoid(g_) * e_, gg), [h, gl, e], [post_g],
                       [(D_MODEL, F32)], name="ple_out")
        return out, None, (h, a, gl, e)

    def fn(x, g_, e_, gg, ng):
        out = x + _rms(jax.nn.sigmoid(g_) * e_, gg)
        return out, _rms(out, ng)
    out, a_next = _rowwise(fn, [h, gl, e], [post_g, next_g], [(D_MODEL, F32), (D_MODEL, BF16)], name="ple_out_prenorm")
    return out, a_next, (h, a, gl, e)


def _ple_bwd(dh, saved, p_i, w, pre_g, post_g):
    h, a, gl, e = saved

    def fn(g_, e_, d, gg):
        sg = jax.nn.sigmoid(g_)
        du, dgt = _rms_bwd(sg * e_, gg, d)
        return du * e_ * sg * (1.0 - sg), du * sg, _colsum(dgt)
    dgl, de, d_post = _rowwise(fn, [gl, e, dh], [post_g], [(D_MODEL, BF16), (D_MODEL, BF16)], [(1, D_MODEL)],
                               name="ple_out_bwd")
    d_gate = _mm(a, dgl, ta=True, tm=1024, tn=1024, tk=2048, name="ple_gate_dw")
    d_proj = _mm(p_i, de, ta=True, tm=PLE_DIM, tn=1024, tk=2048, name="ple_proj_dw")
    dh_in, d_pre = _dx_prenorm_bwd([(dgl, w["gate"], 0, 1024)], h, dh, pre_g, "ple_gate_dx_prenorm_bwd")
    return dh_in, dict(w_gate=d_gate, w_proj=d_proj, pre_g=d_pre, post_g=d_post)


def _gm_layernorm(v, g, b):
    mu = jnp.mean(v, axis=-1, keepdims=True)
    xc = v - mu
    rstd = lax.rsqrt(jnp.mean(xc * xc, axis=-1, keepdims=True) + LN_EPS)
    vhat = xc * rstd
    return vhat, rstd, vhat * g + b


def _gmlp_fwd(proj, wm, bias_t, ln_g, ln_b, name):
    t = proj.shape[0]

    def body(uv_ref, wm_ref, bt_ref, g_ref, b_ref, o_ref):
        for hd in range(GM_HEADS):
            lo = hd * LANES
            u = _gelu(uv_ref[:, lo:lo + LANES])
            v = _gelu(uv_ref[:, 1024 + lo:1024 + lo + LANES])
            _, _, vln = _gm_layernorm(v, g_ref[:, lo:lo + LANES], b_ref[:, lo:lo + LANES])
            mixed = jnp.dot(wm_ref[hd], vln.astype(BF16), preferred_element_type=F32) + bt_ref[:, hd:hd + 1]
            o_ref[:, lo:lo + LANES] = (u * mixed).astype(o_ref.dtype)

    return pl.pallas_call(
        body, name=name, grid=(t // CHUNK,),
        in_specs=[pl.BlockSpec((CHUNK, 2048), lambda i: (i, 0)), pl.BlockSpec(wm.shape, lambda i: (0, 0, 0)),
                  pl.BlockSpec(bias_t.shape, lambda i: (0, 0)), pl.BlockSpec(ln_g.shape, lambda i: (0, 0)),
                  pl.BlockSpec(ln_b.shape, lambda i: (0, 0))],
        out_specs=pl.BlockSpec((CHUNK, 1024), lambda i: (i, 0)), out_shape=jax.ShapeDtypeStruct((t, 1024), BF16),
        compiler_params=_params("parallel"),
    )(proj, wm, bias_t, ln_g, ln_b)


def _gmlp_bwd(proj, dyab, wm, bias_t, ln_g, ln_b, name):
    t = proj.shape[0]
    nc = t // CHUNK

    def body(uv_ref, dy_ref, wm_ref, bt_ref, g_ref, b_ref, duv_ref, dw_ref, db_ref, dg_ref, dbeta_ref, dbacc):
        c = pl.program_id(0)

        @pl.when(c == 0)
        def _():
            dw_ref[...] = jnp.zeros_like(dw_ref)
            dbacc[...] = jnp.zeros_like(dbacc)
            dg_ref[...] = jnp.zeros_like(dg_ref)
            dbeta_ref[...] = jnp.zeros_like(dbeta_ref)

        for hd in range(GM_HEADS):
            lo = hd * LANES
            xu = uv_ref[:, lo:lo + LANES]
            xv = uv_ref[:, 1024 + lo:1024 + lo + LANES]
            u = _gelu(xu)
            g_h = g_ref[:, lo:lo + LANES]
            vhat, rstd, vln = _gm_layernorm(_gelu(xv), g_h, b_ref[:, lo:lo + LANES])
            vln16 = vln.astype(BF16)
            mixed = jnp.dot(wm_ref[hd], vln16, preferred_element_type=F32) + bt_ref[:, hd:hd + 1]
            dy = dy_ref[:, lo:lo + LANES]
            du = dy * mixed
            dmix = dy * u
            dmix16 = dmix.astype(BF16)
            dw_ref[hd] += lax.dot_general(dmix16, vln16, (((1,), (1,)), ((), ())), preferred_element_type=F32)
            dbacc[hd] += dmix
            dvln = lax.dot_general(wm_ref[hd], dmix16, (((0,), (0,)), ((), ())), preferred_element_type=F32)
            dg_ref[:, lo:lo + LANES] += _colsum(dvln * vhat)
            dbeta_ref[:, lo:lo + LANES] += _colsum(dvln)
            dvh = dvln * g_h
            dv = rstd * (dvh - jnp.mean(dvh, axis=-1, keepdims=True)
                         - vhat * jnp.mean(dvh * vhat, axis=-1, keepdims=True))
            duv_ref[:, lo:lo + LANES] = (du * _gelu_grad(xu)).astype(duv_ref.dtype)
            duv_ref[:, 1024 + lo:1024 + lo + LANES] = (dv * _gelu_grad(xv)).astype(duv_ref.dtype)

        @pl.when(c == nc - 1)
        def _():
            row = lax.broadcasted_iota(jnp.int32, (CHUNK, CHUNK), 0)
            col = lax.broadcasted_iota(jnp.int32, (CHUNK, CHUNK), 1)
            for hd in range(GM_HEADS):
                dw_ref[hd] = jnp.where(col <= row, dw_ref[hd], 0.0)
                db_ref[hd] = jnp.sum(dbacc[hd], axis=1, keepdims=True)

    return pl.pallas_call(
        body, name=name, grid=(nc,),
        in_specs=[pl.BlockSpec((CHUNK, 2048), lambda i: (i, 0)), pl.BlockSpec((CHUNK, 1024), lambda i: (i, 0)),
                  pl.BlockSpec(wm.shape, lambda i: (0, 0, 0)), pl.BlockSpec(bias_t.shape, lambda i: (0, 0)),
                  pl.BlockSpec(ln_g.shape, lambda i: (0, 0)), pl.BlockSpec(ln_b.shape, lambda i: (0, 0))],
        out_specs=[pl.BlockSpec((CHUNK, 2048), lambda i: (i, 0)), pl.BlockSpec((GM_HEADS, CHUNK, CHUNK), lambda i: (0, 0, 0)),
                   pl.BlockSpec((GM_HEADS, CHUNK, 1), lambda i: (0, 0, 0)), pl.BlockSpec((1, 1024), lambda i: (0, 0)),
                   pl.BlockSpec((1, 1024), lambda i: (0, 0))],
        out_shape=[jax.ShapeDtypeStruct((t, 2048), BF16), jax.ShapeDtypeStruct((GM_HEADS, CHUNK, CHUNK), F32),
                   jax.ShapeDtypeStruct((GM_HEADS, CHUNK, 1), F32), jax.ShapeDtypeStruct((1, 1024), F32),
                   jax.ShapeDtypeStruct((1, 1024), F32)],
        scratch_shapes=[pltpu.VMEM((GM_HEADS, CHUNK, CHUNK), F32)],
        compiler_params=_params("arbitrary"),
    )(proj, dyab, wm, bias_t, ln_g, ln_b)


def _ssd_chunk_terms(dt_pad, a_pad):
    row = lax.broadcasted_iota(jnp.int32, (CHUNK, CHUNK), 0)
    col = lax.broadcasted_iota(jnp.int32, (CHUNK, CHUNK), 1)
    tril = jnp.where(col <= row, 1.0, 0.0).astype(F32)
    a_cs = jnp.dot(tril, dt_pad * a_pad, precision=HIGHEST, preferred_element_type=F32)
    return a_cs, a_cs.T


def _pair_cols(mat, hd_a, lane_lt64):
    return jnp.where(lane_lt64, mat[:, hd_a:hd_a + 1], mat[:, hd_a + 1:hd_a + 2])


def _head_decay(a_cs, a_cs_t, hd, causal):
    seg = a_cs[:, hd:hd + 1] - a_cs_t[hd:hd + 1, :]
    return jnp.exp(jnp.where(causal, seg, -jnp.inf))


def _ssd_fwd(act, dt_pad, a_pad, d_pad, name):
    t = act.shape[0]
    nc = t // CHUNK

    def body(act_ref, dt_ref, a_ref, d_ref, y_ref, st_ref, h_sc):
        c = pl.program_id(0)

        @pl.when(c == 0)
        def _():
            h_sc[...] = jnp.zeros_like(h_sc)

        st_ref[...] = h_sc[...]
        row = lax.broadcasted_iota(jnp.int32, (CHUNK, CHUNK), 0)
        col = lax.broadcasted_iota(jnp.int32, (CHUNK, CHUNK), 1)
        causal = col <= row
        lane_lt64 = lax.broadcasted_iota(jnp.int32, (CHUNK, LANES), 1) < SSD_HEAD_DIM
        row_lt64 = lax.broadcasted_iota(jnp.int32, (LANES, 1), 0) < SSD_HEAD_DIM
        dt = dt_ref[...]
        a_cs, a_cs_t = _ssd_chunk_terms(dt, a_ref[...])
        last = a_cs[CHUNK - 1:CHUNK, :]
        for g in range(2):
            b16 = act_ref[:, SSD_INNER + g * SSD_STATE:SSD_INNER + (g + 1) * SSD_STATE].astype(BF16)
            c16 = act_ref[:, SSD_INNER + SSD_BC + g * SSD_STATE:SSD_INNER + SSD_BC + (g + 1) * SSD_STATE].astype(BF16)
            cb = lax.dot_general(c16, b16, (((1,), (1,)), ((), ())), preferred_element_type=F32)
            for pr in range(4):
                ha = g * 8 + pr * 2
                lo = ha * SSD_HEAD_DIM
                xs = act_ref[:, lo:lo + LANES]
                xd = xs * _pair_cols(dt, ha, lane_lt64)
                xd16 = xd.astype(BF16)
                ya = jnp.dot((cb * _head_decay(a_cs, a_cs_t, ha, causal)).astype(BF16), xd16, preferred_element_type=F32)
                yb = jnp.dot((cb * _head_decay(a_cs, a_cs_t, ha + 1, causal)).astype(BF16), xd16, preferred_element_type=F32)
                a_p = _pair_cols(a_cs, ha, lane_lt64)
                hp = h_sc[lo:lo + LANES, :]
                y_off = lax.dot_general(c16, hp.astype(BF16), (((1,), (1,)), ((), ())), preferred_element_type=F32)
                d_p = jnp.where(lane_lt64[:1], d_ref[:, ha:ha + 1], d_ref[:, ha + 1:ha + 2])
                y_ref[:, lo:lo + LANES] = jnp.where(lane_lt64, ya, yb) + y_off * jnp.exp(a_p) + d_p * xs
                last_p = jnp.where(lane_lt64[:1], last[:, ha:ha + 1], last[:, ha + 1:ha + 2])
                xw16 = (xd * jnp.exp(last_p - a_p)).astype(BF16)
                s_new = lax.dot_general(xw16, b16, (((0,), (0,)), ((), ())), preferred_element_type=F32)
                t_col = jnp.where(row_lt64, jnp.exp(last[:, ha:ha + 1]), jnp.exp(last[:, ha + 1:ha + 2]))
                h_sc[lo:lo + LANES, :] = t_col * hp + s_new

    return pl.pallas_call(
        body, name=name, grid=(nc,),
        in_specs=[pl.BlockSpec((CHUNK, SSD_CONV_CH), lambda i: (i, 0)), pl.BlockSpec((CHUNK, LANES), lambda i: (i, 0)),
                  pl.BlockSpec((1, LANES), lambda i: (0, 0)), pl.BlockSpec((1, LANES), lambda i: (0, 0))],
        out_specs=[pl.BlockSpec((CHUNK, SSD_INNER), lambda i: (i, 0)),
                   pl.BlockSpec((None, SSD_INNER, SSD_STATE), lambda i: (i, 0, 0))],
        out_shape=[jax.ShapeDtypeStruct((t, SSD_INNER), F32), jax.ShapeDtypeStruct((nc, SSD_INNER, SSD_STATE), F32)],
        scratch_shapes=[pltpu.VMEM((SSD_INNER, SSD_STATE), F32)],
        compiler_params=_params("arbitrary"),
    )(act, dt_pad, a_pad, d_pad)


def _ssd_bwd(act, dt_pad, a_pad, d_pad, states, dy, name):
    t = act.shape[0]
    nc = t // CHUNK

    def body(act_ref, dt_ref, a_ref, d_ref, st_ref, dy_ref, dact_ref, ddt_ref, da_ref, dd_ref, dh_sc):
        c = pl.program_id(0)

        @pl.when(c == 0)
        def _():
            dh_sc[...] = jnp.zeros_like(dh_sc)
            da_ref[...] = jnp.zeros_like(da_ref)
            dd_ref[...] = jnp.zeros_like(dd_ref)

        row = lax.broadcasted_iota(jnp.int32, (CHUNK, CHUNK), 0)
        col = lax.broadcasted_iota(jnp.int32, (CHUNK, CHUNK), 1)
        causal = col <= row
        lane = lax.broadcasted_iota(jnp.int32, (CHUNK, LANES), 1)
        lane_lt64 = lane < SSD_HEAD_DIM
        row_lt64 = lax.broadcasted_iota(jnp.int32, (LANES, 1), 0) < SSD_HEAD_DIM
        is_last = lax.broadcasted_iota(jnp.int32, (CHUNK, 1), 0) == CHUNK - 1
        dt = dt_ref[...]
        a_cs, a_cs_t = _ssd_chunk_terms(dt, a_ref[...])
        last = a_cs[CHUNK - 1:CHUNK, :]
        d_acs = jnp.zeros((CHUNK, LANES), F32)
        d_acs_rows = jnp.zeros((LANES, CHUNK), F32)
        head_row = lax.broadcasted_iota(jnp.int32, (LANES, CHUNK), 0)
        ddt_x = jnp.zeros((CHUNK, LANES), F32)
        dd_acc = jnp.zeros((1, LANES), F32)

        def head_sum(v, first):
            return jnp.sum(jnp.where(lane_lt64 if first else jnp.logical_not(lane_lt64), v, 0.0), axis=1, keepdims=True)

        for g in range(2):
            b_lo = SSD_INNER + g * SSD_STATE
            c_lo = SSD_INNER + SSD_BC + g * SSD_STATE
            b16 = act_ref[:, b_lo:b_lo + SSD_STATE].astype(BF16)
            c16 = act_ref[:, c_lo:c_lo + SSD_STATE].astype(BF16)
            cb = lax.dot_general(c16, b16, (((1,), (1,)), ((), ())), preferred_element_type=F32)
            dcb = jnp.zeros((CHUNK, CHUNK), F32)
            db_g = jnp.zeros((CHUNK, SSD_STATE), F32)
            dc_g = jnp.zeros((CHUNK, SSD_STATE), F32)
            for pr in range(4):
                ha = g * 8 + pr * 2
                lo = ha * SSD_HEAD_DIM
                xs = act_ref[:, lo:lo + LANES]
                dt_p = _pair_cols(dt, ha, lane_lt64)
                xd = xs * dt_p
                xd16 = xd.astype(BF16)
                a_p = _pair_cols(a_cs, ha, lane_lt64)
                exp_a = jnp.exp(a_p)
                last_p = jnp.where(lane_lt64[:1], last[:, ha:ha + 1], last[:, ha + 1:ha + 2])
                w_p = jnp.exp(last_p - a_p)
                hp = st_ref[lo:lo + LANES, :]
                hp16 = hp.astype(BF16)
                dhn = dh_sc[lo:lo + LANES, :]
                dhn16 = dhn.astype(BF16)
                dyp = dy_ref[:, lo:lo + LANES]
                d_p = jnp.where(lane_lt64[:1], d_ref[:, ha:ha + 1], d_ref[:, ha + 1:ha + 2])
                dd_row = _colsum(dyp * xs)
                dd_acc = dd_acc + jnp.where(lane[:1] == ha, jnp.sum(jnp.where(lane_lt64[:1], dd_row, 0.0), axis=1, keepdims=True), 0.0) \
                    + jnp.where(lane[:1] == ha + 1, jnp.sum(jnp.where(lane_lt64[:1], 0.0, dd_row), axis=1, keepdims=True), 0.0)
                g_off = lax.dot_general(c16, hp16, (((1,), (1,)), ((), ())), preferred_element_type=F32)
                dg16 = (dyp * exp_a).astype(BF16)
                dc_g = dc_g + jnp.dot(dg16, hp16, preferred_element_type=F32)
                dh_prev = lax.dot_general(dg16, c16, (((0,), (0,)), ((), ())), preferred_element_type=F32)
                off_term = dyp * g_off * exp_a
                q = lax.dot_general(b16, dhn16, (((1,), (1,)), ((), ())), preferred_element_type=F32)
                xw16 = (xd * w_p).astype(BF16)
                db_g = db_g + jnp.dot(xw16, dhn16, preferred_element_type=F32)
                dw_term = xd * q * w_p
                dxd = w_p * q
                dt_all = dhn * hp
                dyp16 = dyp.astype(BF16)
                for k, first in ((0, True), (1, False)):
                    hd = ha + k
                    sel = lane_lt64 if first else jnp.logical_not(lane_lt64)
                    decay = _head_decay(a_cs, a_cs_t, hd, causal)
                    m = cb * decay
                    dy_h = jnp.where(sel, dyp16, jnp.zeros_like(dyp16))
                    dm = lax.dot_general(dy_h, xd16, (((1,), (1,)), ((), ())), preferred_element_type=F32)
                    dcb = dcb + dm * decay
                    dseg = dm * m
                    dxd = dxd + jnp.where(sel, lax.dot_general(m.astype(BF16), dyp16, (((0,), (0,)), ((), ())),
                                                               preferred_element_type=F32), 0.0)
                    d_col = jnp.sum(dseg, axis=1, keepdims=True)
                    d_acs_rows = d_acs_rows + jnp.where(head_row == hd, jnp.sum(dseg, axis=0, keepdims=True), 0.0)
                    dw_col = head_sum(dw_term, first)
                    d_col = d_col + head_sum(off_term, first) - dw_col
                    t_h = jnp.exp(last[:, hd:hd + 1])
                    dt_sum = jnp.sum(jnp.sum(jnp.where(row_lt64 if first else jnp.logical_not(row_lt64), dt_all, 0.0),
                                             axis=0, keepdims=True), axis=1, keepdims=True)
                    end_term = jnp.sum(dw_col, axis=0, keepdims=True) + dt_sum * t_h
                    d_col = d_col + jnp.where(is_last, end_term, 0.0)
                    d_acs = d_acs + jnp.where(lane == hd, d_col, 0.0)
                t_col = jnp.where(row_lt64, jnp.exp(last[:, ha:ha + 1]), jnp.exp(last[:, ha + 1:ha + 2]))
                dh_sc[lo:lo + LANES, :] = t_col * dhn + dh_prev
                dact_ref[:, lo:lo + LANES] = d_p * dyp + dxd * dt_p
                ddt_all = dxd * xs
                ddt_x = ddt_x + jnp.where(lane == ha, head_sum(ddt_all, True), 0.0) \
                    + jnp.where(lane == ha + 1, head_sum(ddt_all, False), 0.0)
            dcb16 = dcb.astype(BF16)
            dact_ref[:, b_lo:b_lo + SSD_STATE] = db_g + lax.dot_general(dcb16, c16, (((0,), (0,)), ((), ())),
                                                                          preferred_element_type=F32)
            dact_ref[:, c_lo:c_lo + SSD_STATE] = dc_g + jnp.dot(dcb16, b16, preferred_element_type=F32)
        triu = jnp.where(col >= row, 1.0, 0.0).astype(F32)
        dda = jnp.dot(triu, d_acs - d_acs_rows.T, precision=HIGHEST, preferred_element_type=F32)
        ddt_ref[...] = dda * a_ref[...] + ddt_x
        da_ref[...] += _colsum(dda * dt)
        dd_ref[...] += dd_acc

    rev = lambda i: (nc - 1 - i, 0)
    return pl.pallas_call(
        body, name=name, grid=(nc,),
        in_specs=[pl.BlockSpec((CHUNK, SSD_CONV_CH), rev), pl.BlockSpec((CHUNK, LANES), rev),
                  pl.BlockSpec((1, LANES), lambda i: (0, 0)), pl.BlockSpec((1, LANES), lambda i: (0, 0)),
                  pl.BlockSpec((None, SSD_INNER, SSD_STATE), lambda i: (nc - 1 - i, 0, 0)),
                  pl.BlockSpec((CHUNK, SSD_INNER), rev)],
        out_specs=[pl.BlockSpec((CHUNK, SSD_CONV_CH), rev), pl.BlockSpec((CHUNK, LANES), rev),
                   pl.BlockSpec((1, LANES), lambda i: (0, 0)), pl.BlockSpec((1, LANES), lambda i: (0, 0))],
        out_shape=[jax.ShapeDtypeStruct((t, SSD_CONV_CH), F32), jax.ShapeDtypeStruct((t, LANES), F32),
                   jax.ShapeDtypeStruct((1, LANES), F32), jax.ShapeDtypeStruct((1, LANES), F32)],
        scratch_shapes=[pltpu.VMEM((SSD_INNER, SSD_STATE), F32)],
        compiler_params=_params("arbitrary"),
    )(act, dt_pad, a_pad, d_pad, states, dy)


def _shift_down(x, k):
    return x if k == 0 else jnp.pad(x, ((k, 0), (0, 0)))[:x.shape[0]]


def _shift_up(x, k):
    return x if k == 0 else jnp.pad(x, ((0, k), (0, 0)))[k:]


def _conv_pre(x0, x1, x2, x3, w, b):
    return x0 * w[0:1] + x1 * w[1:2] + x2 * w[2:3] + x3 * w[3:4] + b


def _rope128(x, cpad, s_lo, s_hi):
    return x * cpad + pltpu.roll(x, 96, 1) * s_lo + pltpu.roll(x, 32, 1) * s_hi


ATTN_ROW_SPLIT = 4
ATTN_ROW_SPLIT_DKV = 4


def _diag_mask(rows, cols, row0):
    return lax.broadcasted_iota(jnp.int32, (rows, cols), 1) <= row0 + lax.broadcasted_iota(jnp.int32, (rows, cols), 0)


def _attn_scores(q, k):
    return lax.dot_general(q, k, (((1,), (1,)), ((), ())), preferred_element_type=F32)


def _causal_pairs(nq, by_key):
    if by_key:
        pairs = [(i, j) for j in range(nq) for i in range(j, nq)]
    else:
        pairs = [(i, j) for i in range(nq) for j in range(i + 1)]
    return (jnp.asarray([pr[0] for pr in pairs], jnp.int32), jnp.asarray([pr[1] for pr in pairs], jnp.int32))


def _attn_fwd(qf, kf, kvf, *, tq, name):
    t = qf.shape[0]
    nq = t // tq
    tk = tq
    qi, kj = _causal_pairs(nq, by_key=False)
    rs = tq // ATTN_ROW_SPLIT

    def body(qi_ref, kj_ref, q_ref, k_ref, v_ref, o_ref, lse_ref, m_sc, acc_sc, v1_sc):
        pp = pl.program_id(1)
        i, j = qi_ref[pp], kj_ref[pp]

        @pl.when(pp == 0)
        def _():
            v1_sc[:, LANES:] = jnp.ones((tk, LANES), BF16)

        @pl.when(j == 0)
        def _():
            m_sc[...] = jnp.full_like(m_sc, -jnp.inf)
            acc_sc[...] = jnp.zeros_like(acc_sc)

        v1_sc[:, :LANES] = v_ref[...]

        def update(diag):
            for r in range(ATTN_ROW_SPLIT):
                rows = slice(r * rs, (r + 1) * rs)
                keys = slice(0, (r + 1) * rs if diag else tk)
                s = _attn_scores(q_ref[rows, :], k_ref[keys, :])
                if diag:
                    s = jnp.where(_diag_mask(rs, keys.stop, r * rs), s, -jnp.inf)
                m_prev = m_sc[rows, :]
                m_new = jnp.maximum(m_prev, jnp.max(s, axis=1, keepdims=True))
                p = jnp.exp2(s - m_new).astype(BF16)
                alpha = jnp.exp2(m_prev - m_new)
                acc = alpha * acc_sc[rows, :] + jnp.dot(p, v1_sc[keys, :], preferred_element_type=F32)
                if diag:
                    o_ref[rows, :] = (acc[:, :LANES] / acc[:, LANES:]).astype(o_ref.dtype)
                    lse_ref[rows, :] = m_new + jnp.log2(acc[:, LANES:LANES + 1])
                else:
                    acc_sc[rows, :] = acc
                    m_sc[rows, :] = m_new

        @pl.when(j < i)
        def _():
            update(False)

        @pl.when(j == i)
        def _():
            update(True)

    return pl.pallas_call(
        body, name=name,
        grid_spec=pltpu.PrefetchScalarGridSpec(
            num_scalar_prefetch=2, grid=(MLA_HEADS, int(qi.shape[0])),
            in_specs=[pl.BlockSpec((tq, 2 * LANES), lambda h, pp, qi_, kj_: (qi_[pp], h)),
                      pl.BlockSpec((tk, 2 * LANES), lambda h, pp, qi_, kj_: (kj_[pp], h)),
                      pl.BlockSpec((tk, LANES), lambda h, pp, qi_, kj_: (kj_[pp], 2 * h + 1))],
            out_specs=[pl.BlockSpec((tq, LANES), lambda h, pp, qi_, kj_: (qi_[pp], h)),
                       pl.BlockSpec((None, tq, 1), lambda h, pp, qi_, kj_: (h, qi_[pp], 0))],
            scratch_shapes=[pltpu.VMEM((tq, 1), F32), pltpu.VMEM((tq, 2 * LANES), F32), pltpu.VMEM((tk, 2 * LANES), BF16)]),
        out_shape=[jax.ShapeDtypeStruct((t, MLA_HEADS * LANES), BF16), jax.ShapeDtypeStruct((MLA_HEADS, t, 1), F32)],
        compiler_params=_params("arbitrary", "arbitrary"),
    )(qi, kj, qf, kf, kvf)


def _attn_bwd(qf, kf, kvf, o, do, lse, *, tq, name):
    t = qf.shape[0]
    nq = t // tq
    tk = tq
    qi, kj = _causal_pairs(nq, by_key=True)
    rs = tq // ATTN_ROW_SPLIT_DKV

    def body(qi_ref, kj_ref, q_ref, k_ref, v_ref, o_ref, do_ref, lse_ref, dkv_ref, dkr_ref, dq_ref, dk_sc, dv_sc):
        pp = pl.program_id(1)
        i, j = qi_ref[pp], kj_ref[pp]
        tn = (((0,), (0,)), ((), ()))

        @pl.when(pp == 0)
        def _():
            dq_ref[...] = jnp.zeros_like(dq_ref)

        def update(diag):
            if diag:
                dv_sc[...] = jnp.zeros_like(dv_sc)
                dk_sc[...] = jnp.zeros_like(dk_sc)
            for r in range(ATTN_ROW_SPLIT_DKV):
                rows = slice(r * rs, (r + 1) * rs)
                keys = slice(0, (r + 1) * rs if diag else tk)
                do_ = do_ref[rows, :]
                delta = jnp.sum(do_.astype(F32) * o_ref[rows, :].astype(F32), axis=1, keepdims=True)
                s = _attn_scores(q_ref[rows, :], k_ref[keys, :])
                p = jnp.exp2(s - lse_ref[rows, :])
                if diag:
                    p = jnp.where(_diag_mask(rs, keys.stop, r * rs), p, 0.0)
                dp = lax.dot_general(do_, v_ref[keys, :], (((1,), (1,)), ((), ())), preferred_element_type=F32)
                ds = (p * (dp - delta)).astype(BF16)
                dv_sc[keys, :] += lax.dot_general(p.astype(BF16), do_, tn, preferred_element_type=F32)
                dk_sc[keys, :] += lax.dot_general(ds, q_ref[rows, :], tn, preferred_element_type=F32)
                q_rows = pl.ds(pl.multiple_of(i * tq + r * rs, rs), rs)
                dq_ref[q_rows, :] += jnp.dot(ds, k_ref[keys, :], preferred_element_type=F32)

        @pl.when(i > j)
        def _():
            update(False)

        @pl.when(i == j)
        def _():
            update(True)

        @pl.when(i == nq - 1)
        def _():
            dkv_ref[:, :LANES] = (dk_sc[:, :LANES] * LN_2).astype(dkv_ref.dtype)
            dkv_ref[:, LANES:] = dv_sc[...].astype(dkv_ref.dtype)
            dkr_ref[...] = dk_sc[:, LANES:] * LN_2

    qblk = lambda c: (lambda h, pp, qi_, kj_: (qi_[pp], c(h)))
    kblk = lambda c: (lambda h, pp, qi_, kj_: (kj_[pp], c(h)))
    return pl.pallas_call(
        body, name=name,
        grid_spec=pltpu.PrefetchScalarGridSpec(
            num_scalar_prefetch=2, grid=(MLA_HEADS, int(qi.shape[0])),
            in_specs=[pl.BlockSpec((tq, 2 * LANES), qblk(lambda h: h)), pl.BlockSpec((tk, 2 * LANES), kblk(lambda h: h)),
                      pl.BlockSpec((tk, LANES), kblk(lambda h: 2 * h + 1)),
                      pl.BlockSpec((tq, LANES), qblk(lambda h: h)), pl.BlockSpec((tq, LANES), qblk(lambda h: h)),
                      pl.BlockSpec((None, tq, 1), lambda h, pp, qi_, kj_: (h, qi_[pp], 0))],
            out_specs=[pl.BlockSpec((tk, 2 * LANES), kblk(lambda h: h)), pl.BlockSpec((tk, LANES), kblk(lambda h: h)),
                       pl.BlockSpec((t, 2 * LANES), lambda h, pp, qi_, kj_: (0, h))],
            scratch_shapes=[pltpu.VMEM((tk, 2 * LANES), F32), pltpu.VMEM((tk, LANES), F32)]),
        out_shape=[jax.ShapeDtypeStruct((t, MLA_HEADS * 2 * LANES), BF16), jax.ShapeDtypeStruct((t, MLA_HEADS * LANES), F32),
                   jax.ShapeDtypeStruct((t, MLA_HEADS * 2 * LANES), F32)],
        compiler_params=_params("parallel", "arbitrary"),
    )(qi, kj, qf, kf, kvf, o, do, lse)


def _rope_tables(positions):
    t = positions.shape[0]
    inv = 1.0 / (ROPE_BASE ** (jnp.arange(0, MLA_ROPE, 2, dtype=F32) / MLA_ROPE))
    ang = positions.astype(F32)[:, None] * inv
    cos, sin = jnp.cos(ang), jnp.sin(ang)
    z32, z64 = jnp.zeros((t, 32), F32), jnp.zeros((t, 64), F32)
    cpad = jnp.concatenate([cos, cos, z64], axis=1)
    s_lo = jnp.concatenate([-sin, z32, z64], axis=1)
    s_hi = jnp.concatenate([z32, sin, z64], axis=1)
    return cpad, s_lo, s_hi


def _mla_fwd(h, hn, w, post_g, next_g, rope, tq):
    cpad, s_lo, s_hi = rope
    cin = _mm(hn, w["in"], tm=1024, tn=512, tk=1024, name="mla_in")

    def lat(c, cp, sl, sh, qg, kvg):
        cq, ckv, kr = c[:, :MLA_Q_LORA], c[:, MLA_Q_LORA:MLA_Q_LORA + MLA_KV_LORA], c[:, MLA_Q_LORA + MLA_KV_LORA:]
        return _rms(cq, qg), _rms(ckv, kvg), _rope128(kr, cp, sl, sh)
    cqn, ckvn, kr = _rowwise(lat, [cin, cpad, s_lo, s_hi], [w["q_norm_g"], w["kv_norm_g"]],
                             [(MLA_Q_LORA, BF16), (MLA_KV_LORA, BF16), (LANES, BF16)], name="mla_latent")
    q_raw = _mm(cqn, w["uq"], tm=1024, tn=1024, tk=MLA_Q_LORA, name="mla_uq")

    def rope_q(q, cp, sl, sh):
        pieces = []
        for hd in range(MLA_HEADS):
            pieces.append(q[:, 256 * hd:256 * hd + LANES] * ATTN_QSCALE)
            pieces.append(_rope128(q[:, 256 * hd + LANES:256 * hd + 256], cp, sl, sh) * ATTN_QSCALE)
        return (tuple(pieces),)
    qf = _rowwise(rope_q, [q_raw, cpad, s_lo, s_hi], [], [(4096, BF16)], name="mla_rope_q")
    kvf = _mm(ckvn, w["ukv"], out_dtype=BF16, tm=1024, tn=1024, tk=MLA_KV_LORA, name="mla_ukv")
    t = h.shape[0]
    k_nope = kvf.reshape(t, MLA_HEADS, 2 * LANES)[:, :, :LANES]
    kf = jnp.concatenate([k_nope, jnp.broadcast_to(kr[:, None, :], k_nope.shape)], axis=2).reshape(t, MLA_HEADS * 2 * LANES)
    o, lse = _attn_fwd(qf, kf, kvf, tq=tq, name="mla_attn")
    mixed = _mm(o, w["out"], tm=1024, tn=1024, tk=2048, name="mla_out")
    out, a_next = _postnorm_residual(h, mixed, post_g, 1.0, next_g, "mla_postnorm")
    return out, a_next, (h, hn, cin, cqn, ckvn, qf, kf, kvf, o, lse, mixed)


def _mla_bwd(dh, saved, w, pre_g, post_g, rope, tq):
    cpad, s_lo, s_hi = rope
    h, hn, cin, cqn, ckvn, qf, kf, kvf, o, lse, mixed = saved
    dmixed, d_post = _postnorm_bwd(mixed, dh, post_g, 1.0, "mla_postnorm_bwd")
    do = _mm(dmixed, w["out"], tb=True, out_dtype=BF16, tm=1024, tn=1024, tk=1024, name="mla_out_dx")
    d_out = _mm(o, dmixed, ta=True, tm=1024, tn=1024, tk=2048, name="mla_out_dw")
    dkvf, dkr_heads, dq = _attn_bwd(qf, kf, kvf, o, do, lse, tq=tq, name="mla_attn_bwd")

    def unrope_q(d, cp, sl, sh):
        pieces = []
        for hd in range(MLA_HEADS):
            pieces.append(d[:, 256 * hd:256 * hd + LANES] * ATTN_SCALE)
            pieces.append(_rope128(d[:, 256 * hd + LANES:256 * hd + 256], cp, -sl, -sh) * ATTN_SCALE)
        return (tuple(pieces),)
    dq_raw = _rowwise(unrope_q, [dq, cpad, s_lo, s_hi], [], [(4096, BF16)], name="mla_rope_q_bwd")
    dcqn = _mm(dq_raw, w["uq"], tb=True, tm=1024, tn=256, tk=1024, name="mla_uq_dx")
    d_uq = _mm(cqn, dq_raw, ta=True, tm=256, tn=1024, tk=2048, name="mla_uq_dw")
    dckvn = _mm(dkvf, w["ukv"], tb=True, tm=1024, tn=128, tk=1024, name="mla_ukv_dx")
    d_ukv = _mm(ckvn, dkvf, ta=True, tm=128, tn=1024, tk=2048, name="mla_ukv_dw")

    def lat_bwd(c, dq_, dkv_, dkrh, cp, sl, sh, qg, kvg):
        cq, ckv = c[:, :MLA_Q_LORA], c[:, MLA_Q_LORA:MLA_Q_LORA + MLA_KV_LORA]
        dcq, dqg = _rms_bwd(cq, qg, dq_)
        dckv, dkvg = _rms_bwd(ckv, kvg, dkv_)
        dkr = dkrh[:, :LANES]
        for hd in range(1, MLA_HEADS):
            dkr = dkr + dkrh[:, hd * LANES:(hd + 1) * LANES]
        return (dcq, dckv, _rope128(dkr, cp, -sl, -sh)), _colsum(dqg), _colsum(dkvg)
    dcin, d_qg, d_kvg = _rowwise(lat_bwd, [cin, dcqn, dckvn, dkr_heads, cpad, s_lo, s_hi], [w["q_norm_g"], w["kv_norm_g"]],
                                 [(MLA_IN_PAD, BF16)], [(1, MLA_Q_LORA), (1, MLA_KV_LORA)], name="mla_latent_bwd")
    d_in = _mm(hn, dcin, ta=True, tm=1024, tn=512, tk=2048, name="mla_in_dw")
    dh_in, d_pre = _dx_prenorm_bwd([(dcin, w["in"], 0, MLA_IN_PAD)], h, dh, pre_g, "mla_in_dx_prenorm_bwd")
    return dh_in, dict(w_in=d_in, q_norm_g=d_qg, kv_norm_g=d_kvg, w_uq=d_uq, w_ukv=d_ukv, w_out=d_out,
                       pre_g=d_pre, post_g=d_post)


def _hyb_fwd(h, hn, w, post_g, next_g):
    proj = _mm(hn, w["main"], tm=1024, tn=512, tk=1024, name="hyb_in")
    dtr = _mm(hn, w["dt"], tm=1024, tn=LANES, tk=1024, name="hyb_in_dt")
    ya = _gmlp_fwd(proj, w["gm_w"], w["gm_bt"], w["gm_ln_g"], w["gm_ln_b"], "gmlp")
    xbc = proj[:, 3072:]
    xsh = [_shift_down(xbc, 3 - k) for k in range(3)] + [(proj, SSD_CONV_CH, 2)]
    act = _rowwise(lambda x0, x1, x2, x3, cw, cb: _silu(_conv_pre(x0, x1, x2, x3, cw, cb)), xsh,
                   [w["conv_w"], w["conv_b"]], [(SSD_CONV_CH, F32)], name="ssd_conv")
    dt_pad = _rowwise(lambda d, b: jax.nn.softplus(d + b), [dtr], [w["dt_bias"]], [(LANES, F32)], name="ssd_dt")
    y, states = _ssd_fwd(act, dt_pad, w["a"], w["d"], "ssd_scan")

    def gate_norm(y_, z, ng):
        yg = y_ * _silu(z)
        return ((_rms(yg[:, :512], ng[:, :512]), _rms(yg[:, 512:], ng[:, 512:])),)
    yb = _rowwise(gate_norm, [y, (proj, SSD_INNER, 2)], [w["norm_g"]], [(SSD_INNER, BF16)], name="ssd_gate_norm")
    yab = jnp.concatenate([ya, yb], axis=1)
    mixed = _mm(yab, w["out"], tm=1024, tn=1024, tk=2048, name="hyb_out")
    out, a_next = _postnorm_residual(h, mixed, post_g, 1.0, next_g, "hyb_postnorm")
    return out, a_next, (h, hn, proj, dtr, xsh, act, dt_pad, y, states, yab, mixed)


def _hyb_bwd(dh, saved, w, pre_g, post_g):
    h, hn, proj, dtr, xsh, act, dt_pad, y, states, yab, mixed = saved
    dmixed, d_post = _postnorm_bwd(mixed, dh, post_g, 1.0, "hyb_postnorm_bwd")
    dyab = _mm(dmixed, w["out"], tb=True, tm=1024, tn=1024, tk=1024, name="hyb_out_dx")
    d_out = _mm(yab, dmixed, ta=True, tm=1024, tn=1024, tk=2048, name="hyb_out_dw")

    def gate_norm_bwd(y_, z, d, ng):
        sz = _silu(z)
        yg = y_ * sz
        d_lo, g_lo = _rms_bwd(yg[:, :512], ng[:, :512], d[:, 1024:1536])
        d_hi, g_hi = _rms_bwd(yg[:, 512:], ng[:, 512:], d[:, 1536:])
        dyg = jnp.concatenate([d_lo, d_hi], axis=1)
        return dyg * sz, dyg * y_ * _silu_grad(z), _colsum(jnp.concatenate([g_lo, g_hi], axis=1))
    dy, dz, d_norm = _rowwise(gate_norm_bwd, [y, (proj, SSD_INNER, 2), dyab], [w["norm_g"]], [(SSD_INNER, F32), (SSD_INNER, BF16)],
                              [(1, SSD_INNER)], name="ssd_gate_norm_bwd")
    dact, ddt, da_sum, dd_sum = _ssd_bwd(act, dt_pad, w["a"], w["d"], states, dy, "ssd_scan_bwd")

    def conv_bwd(x0, x1, x2, x3, da_, cw, cb):
        dpre = da_ * _silu_grad(_conv_pre(x0, x1, x2, x3, cw, cb))
        dw = jnp.concatenate([_colsum(dpre * x0), _colsum(dpre * x1), _colsum(dpre * x2), _colsum(dpre * x3)], axis=0)
        return dpre, dw, _colsum(dpre)
    dconv, d_conv_w, d_conv_b = _rowwise(conv_bwd, [*xsh, dact], [w["conv_w"], w["conv_b"]], [(SSD_CONV_CH, F32)],
                                         [(4, SSD_CONV_CH), (1, SSD_CONV_CH)], name="ssd_conv_bwd")
    dsh = [_shift_up(dconv, 3 - k) for k in range(4)]
    dxbc = _rowwise(lambda d0, d1, d2, d3, cw: d0 * cw[0:1] + d1 * cw[1:2] + d2 * cw[2:3] + d3 * cw[3:4], dsh,
                    [w["conv_w"]], [(SSD_CONV_CH, BF16)], name="ssd_conv_dx")

    def dt_bwd(dd, d, b):
        g = dd * jax.nn.sigmoid(d + b)
        g = jnp.where(lax.broadcasted_iota(jnp.int32, g.shape, 1) < SSD_HEADS, g, 0.0)
        return g, _colsum(g)
    ddtr, d_dt_bias = _rowwise(dt_bwd, [ddt, dtr], [w["dt_bias"]], [(LANES, BF16)], [(1, LANES)], name="ssd_dt_bwd")
    duv, d_gm_w, d_gm_b, d_ln_g, d_ln_b = _gmlp_bwd(proj, dyab, w["gm_w"], w["gm_bt"], w["gm_ln_g"], w["gm_ln_b"],
                                                    "gmlp_bwd")
    dproj = jnp.concatenate([duv, dz, dxbc], axis=1)
    d_main = _mm(hn, dproj, ta=True, tm=1024, tn=512, tk=2048, name="hyb_in_dw")
    d_dt = _mm(hn, ddtr, ta=True, tm=1024, tn=LANES, tk=2048, name="hyb_in_dt_dw")
    dh_in, d_pre = _dx_prenorm_bwd([(dproj, w["main"], 0, 1536), (ddtr, w["dt"], 0, LANES)], h, dh, pre_g,
                                   "hyb_in_dx_prenorm_bwd")
    grads = dict(w_in=jnp.concatenate([d_main, d_dt[:, :SSD_HEADS]], axis=1), gm_ln_g=d_ln_g, gm_ln_b=d_ln_b,
                 gm_w_s=d_gm_w, gm_b_s=d_gm_b[:, :, 0], conv_w=d_conv_w, conv_b=d_conv_b,
                 dt_bias=d_dt_bias[:, :SSD_HEADS], a_log=(da_sum * w["a"])[:, :SSD_HEADS], d=dd_sum[:, :SSD_HEADS],
                 norm_g=d_norm, w_out=d_out, pre_g=d_pre, post_g=d_post)
    return dh_in, grads


def _row(v):
    return v.reshape(1, -1).astype(F32)


def _pad_lanes(v, n=LANES):
    v = _row(v)
    return jnp.pad(v, ((0, 0), (0, n - v.shape[1])))


HYB_IN = 4624
HYB_SHARD = HYB_IN // N_DEV
HYB_SHARD_PAD = 640


def _hyb_unblock_matrix():
    n = N_DEV * HYB_SHARD_PAD
    r = lax.broadcasted_iota(jnp.int32, (n, n), 0)
    c = lax.broadcasted_iota(jnp.int32, (n, n), 1)
    j = r % HYB_SHARD_PAD
    return jnp.logical_and(j < HYB_SHARD, c == HYB_SHARD * (r // HYB_SHARD_PAD) + j).astype(BF16)


def _layer_weights(fw, sm, i):
    j = i // 2
    lw = dict(
        ffn1=dict({"in": fw["ffn1_w_in"][i], "down": fw["ffn1_w_down"][i]}),
        ffn2=dict({"in": fw["ffn2_w_in"][i], "down": fw["ffn2_w_down"][i]}),
        ple=dict(gate=fw["ple_w_gate"][i], proj=fw["ple_w_proj"][i]),
    )
    if i % 2 == 0:
        w_in = _mm(fw["hyb_w_in"][j], _hyb_unblock_matrix(), out_dtype=BF16, tm=1024, tn=512, tk=1024, name="hyb_w_unblock")
        causal = jnp.tril(jnp.ones((CHUNK, CHUNK), dtype=bool))
        lw["mix"] = {
            "main": w_in[:, :HYB_MAIN], "dt": w_in[:, HYB_MAIN:HYB_MAIN + LANES],
            "gm_w": jnp.where(causal[None], sm["gm_w_s"][j], 0.0).astype(BF16),
            "gm_bt": jnp.pad(sm["gm_b_s"][j].T, ((0, 0), (0, LANES - GM_HEADS))),
            "gm_ln_g": _row(sm["gm_ln_g"][j]), "gm_ln_b": _row(sm["gm_ln_b"][j]),
            "conv_w": fw["ssd_conv_w"][j], "conv_b": _row(sm["ssd_conv_b"][j]),
            "dt_bias": _pad_lanes(sm["ssd_dt_bias"][j]), "a": _pad_lanes(-jnp.exp(sm["ssd_a_log"][j])),
            "d": _pad_lanes(sm["ssd_d"][j]), "norm_g": _row(sm["ssd_norm_g"][j]), "out": fw["hyb_w_out"][j],
        }
    else:
        uq = fw["mla_w_uq"][j].reshape(MLA_Q_LORA, MLA_HEADS, 192)
        uq = jnp.pad(uq, ((0, 0), (0, 0), (0, 64))).reshape(MLA_Q_LORA, MLA_HEADS * 256)
        lw["mix"] = {
            "in": jnp.pad(fw["mla_w_in"][j], ((0, 0), (0, MLA_IN_PAD - MLA_IN))), "uq": uq, "ukv": fw["mla_w_ukv"][j],
            "out": fw["mla_w_out"][j], "q_norm_g": _row(fw["mla_q_norm_g"][j]), "kv_norm_g": _row(sm["mla_kv_norm_g"][j]),
        }
    return lw


def _device_step(x, p, positions, target, fw, sm):
    t = x.shape[0]
    tq = _pick(t, (1024, 512, 256, 128))
    rope = _rope_tables(positions)
    h = x
    saved, lws = [], []
    a = None
    for i in range(DEPTH):
        lw = _layer_weights(fw, sm, i)
        lws.append(lw)
        after = _row(sm["ffn1_pre_g"][i + 1]) if i + 1 < DEPTH else None
        h, a, s1 = _ffn_fwd(h, a, lw["ffn1"], _row(sm["ffn1_pre_g"][i]), _row(sm["ffn1_post_g"][i]),
                            _row(sm["mix_pre_g"][i]), "ffn")
        if i % 2 == 0:
            h, a, s2 = _hyb_fwd(h, a, lw["mix"], _row(sm["mix_post_g"][i]), _row(sm["ffn2_pre_g"][i]))
        else:
            h, a, s2 = _mla_fwd(h, a, lw["mix"], _row(sm["mix_post_g"][i]), _row(sm["ffn2_pre_g"][i]), rope, tq)
        h, a, s3 = _ffn_fwd(h, a, lw["ffn2"], _row(sm["ffn2_pre_g"][i]), _row(sm["ffn2_post_g"][i]),
                            _row(sm["ple_pre_g"][i]), "ffn")
        h, a, s4 = _ple_fwd(h, a, p[i], lw["ple"], _row(sm["ple_post_g"][i]), after)
        saved.append((s1, s2, s3, s4))

    def loss_fn(y, tg):
        err = y - tg
        return err * (1.0 / D_MODEL), jnp.sum(_colsum(err * err), axis=1, keepdims=True)
    dh, loss_sum = _rowwise(loss_fn, [h, target], [], [(D_MODEL, F32)], [(1, 1)], name="loss")
    loss = loss_sum[0, 0] * (0.5 / D_MODEL)

    per_layer = {n: [None] * DEPTH for n in WEIGHTS if n.startswith(("ffn", "mix", "ple"))}
    per_mixer = {n: [None] * (DEPTH // 2) for n in WEIGHTS if n.startswith(("hyb", "gm", "ssd", "mla"))}
    for i in reversed(range(DEPTH)):
        lw = lws[i]
        s1, s2, s3, s4 = saved[i]
        j = i // 2
        dh, g = _ple_bwd(dh, s4, p[i], lw["ple"], _row(sm["ple_pre_g"][i]), _row(sm["ple_post_g"][i]))
        for k, v in g.items():
            per_layer["ple_" + k][i] = v
        dh, g = _ffn_bwd(dh, s3, lw["ffn2"], _row(sm["ffn2_pre_g"][i]), _row(sm["ffn2_post_g"][i]), "ffn")
        for k, v in g.items():
            per_layer["ffn2_" + k][i] = v
        if i % 2 == 0:
            dh, g = _hyb_bwd(dh, s2, lw["mix"], _row(sm["mix_pre_g"][i]), _row(sm["mix_post_g"][i]))
            names = dict(w_in="hyb_w_in", gm_ln_g="gm_ln_g", gm_ln_b="gm_ln_b", gm_w_s="gm_w_s", gm_b_s="gm_b_s",
                         conv_w="ssd_conv_w", conv_b="ssd_conv_b", dt_bias="ssd_dt_bias", a_log="ssd_a_log", d="ssd_d",
                         norm_g="ssd_norm_g", w_out="hyb_w_out")
        else:
            dh, g = _mla_bwd(dh, s2, lw["mix"], _row(sm["mix_pre_g"][i]), _row(sm["mix_post_g"][i]), rope, tq)
            g["w_in"] = g["w_in"][:, :MLA_IN]
            g["w_uq"] = g["w_uq"].reshape(MLA_Q_LORA, MLA_HEADS, 256)[:, :, :192].reshape(MLA_Q_LORA, MLA_HEADS * 192)
            names = dict(w_in="mla_w_in", q_norm_g="mla_q_norm_g", kv_norm_g="mla_kv_norm_g", w_uq="mla_w_uq",
                         w_ukv="mla_w_ukv", w_out="mla_w_out")
        per_layer["mix_pre_g"][i] = g.pop("pre_g")
        per_layer["mix_post_g"][i] = g.pop("post_g")
        for k, v in g.items():
            per_mixer[names[k]][j] = v
        dh, g = _ffn_bwd(dh, s1, lw["ffn1"], _row(sm["ffn1_pre_g"][i]), _row(sm["ffn1_post_g"][i]), "ffn")
        for k, v in g.items():
            per_layer["ffn1_" + k][i] = v

    return loss, dh, {**per_layer, **per_mixer}


def _stack_layers(parts, shape):
    return jnp.stack(parts, axis=0).reshape(shape)


MESH_AXES = ("x", "y", "c")
EXCHANGE_MAX_COPIES = 56


def _exchange(src, axes, mode, name):
    n = 2 ** len(axes)
    blk = src.shape[-2:]
    flips = [tuple(a for a, bit in zip(axes, np.binary_repr(f, len(axes))) if bit == "1") for f in range(1, n)]
    prefs = tuple(c for c in (16, 8, 4, 2, 1) if c * (n - 1) <= EXCHANGE_MAX_COPIES)
    pieces = _pick(blk[0] // 16, prefs) if blk[0] % 16 == 0 else 1
    rows = blk[0] // pieces

    def index(where):
        idx = 0
        for a in axes:
            idx = idx * 2 + where[a]
        return idx

    me_out = index({a: lax.axis_index(a) for a in MESH_AXES})
    own = lax.dynamic_index_in_dim(src, me_out, 0, keepdims=False) if mode == "a2a" else src
    landing = lax.dynamic_update_index_in_dim(lax.empty((n, *blk), src.dtype), own, me_out, 0)

    def body(src_ref, landing_ref, out_ref, send_sems, recv_sems):
        del landing_ref
        pos = {a: lax.axis_index(a) for a in MESH_AXES}
        me = index(pos)
        copies = []
        for k, flip in enumerate(flips):
            peer = {a: (1 - pos[a]) if a in flip else pos[a] for a in MESH_AXES}
            payload = src_ref.at[index(peer)] if mode == "a2a" else src_ref
            for q in range(pieces):
                part = pl.ds(q * rows, rows)
                cp = pltpu.make_async_remote_copy(
                    src_ref=payload.at[part], dst_ref=out_ref.at[me, part], send_sem=send_sems.at[k * pieces + q],
                    recv_sem=recv_sems.at[k * pieces + q], device_id=(peer["x"], peer["y"], peer["c"]),
                    device_id_type=pl.DeviceIdType.MESH)
                cp.start()
                copies.append(cp)
        for cp in copies:
            cp.wait()

    n_sems = (n - 1) * pieces
    return pl.pallas_call(
        body, name=name, in_specs=[pl.BlockSpec(memory_space=pl.ANY), pl.BlockSpec(memory_space=pl.ANY)],
        out_specs=pl.BlockSpec(memory_space=pl.ANY), out_shape=jax.ShapeDtypeStruct((n, *blk), src.dtype),
        input_output_aliases={1: 0},
        scratch_shapes=[pltpu.SemaphoreType.DMA((n_sems,)), pltpu.SemaphoreType.DMA((n_sems,))],
    )(src, landing)


def _pack_rows(n_elems):
    return -(-n_elems // (16 * PACK_W)) * 16


def _pack(parts, lead=()):
    nl = len(lead)
    rows = []
    for a in parts:
        flat = a.reshape(*lead, -1)
        r = _pack_rows(flat.shape[-1])
        flat = jnp.pad(flat, [(0, 0)] * nl + [(0, r * PACK_W - flat.shape[-1])])
        rows.append(flat.reshape(*lead, r, PACK_W))
    total = sum(r.shape[nl] for r in rows)
    pad = -total % PACK_TM
    if pad:
        rows.append(jnp.zeros((*lead, pad, PACK_W), rows[0].dtype))
    return jnp.concatenate(rows, axis=nl)


def _unpack(buf, shapes, lead=()):
    nl = len(lead)
    out, r0 = [], 0
    for shp in shapes:
        n = int(np.prod(shp))
        r = _pack_rows(n)
        piece = lax.slice_in_dim(buf, r0, r0 + r, axis=nl).reshape(*lead, r * PACK_W)
        out.append(lax.slice_in_dim(piece, 0, n, axis=nl).reshape(*lead, *shp))
        r0 += r
    return out


def _split_for_devices(g, axis):
    if isinstance(g, tuple):
        return jnp.concatenate([_split_for_devices_n(h, axis, N_DEV // len(g)) for h in g], axis=0)
    return _split_for_devices_n(g, axis, N_DEV)


def _split_for_devices_n(g, axis, n):
    shp = g.shape
    g = g.reshape(*shp[:axis], n, shp[axis] // n, *shp[axis + 1:])
    return jnp.moveaxis(g, axis, 0)


def _join_from_devices(parts, axis):
    parts = jnp.moveaxis(parts, 0, axis)
    shp = parts.shape
    return parts.reshape(*shp[:axis], shp[axis] * shp[axis + 1], *shp[axis + 2:])


def _adamw_terms(w, g, m, v):
    m = ADAM_B1 * m + (1.0 - ADAM_B1) * g
    v = ADAM_B2 * v + (1.0 - ADAM_B2) * (g * g)
    m_hat = m / (1.0 - ADAM_B1 ** ADAM_STEP)
    v_hat = v / (1.0 - ADAM_B2 ** ADAM_STEP)
    delta = -ADAM_LR * (m_hat / (jnp.sqrt(v_hat) + ADAM_EPS) + ADAM_WD * w)
    return delta, m, v


def _adamw_packed(w, m, v, partials, n_partials, name):
    def fn(w_, m_, v_, *parts):
        g = parts[0].astype(F32)
        for part in parts[1:]:
            g = g + part.astype(F32)
        return (g,) + _adamw_terms(w_, g, m_, v_)
    return _rowwise(fn, [w, m, v] + [(partials, s) for s in range(n_partials)], [], [(PACK_W, F32)] * 4,
                    tm=PACK_TM, name=name)


def kernel(x, p, positions, ffn1_pre_g, ffn1_w_in, ffn1_w_down, ffn1_post_g, mix_pre_g, mix_post_g, ffn2_pre_g, ffn2_w_in, ffn2_w_down, ffn2_post_g, ple_pre_g, ple_w_gate, ple_w_proj, ple_post_g, hyb_w_in, gm_ln_g, gm_ln_b, gm_w_s, gm_b_s, ssd_conv_w, ssd_conv_b, ssd_dt_bias, ssd_a_log, ssd_d, ssd_norm_g, hyb_w_out, mla_w_in, mla_q_norm_g, mla_kv_norm_g, mla_w_uq, mla_w_ukv, mla_w_out, loss_target, m_ffn1_pre_g, m_ffn1_w_in, m_ffn1_w_down, m_ffn1_post_g, m_mix_pre_g, m_mix_post_g, m_ffn2_pre_g, m_ffn2_w_in, m_ffn2_w_down, m_ffn2_post_g, m_ple_pre_g, m_ple_w_gate, m_ple_w_proj, m_ple_post_g, m_hyb_w_in, m_gm_ln_g, m_gm_ln_b, m_gm_w_s, m_gm_b_s, m_ssd_conv_w, m_ssd_conv_b, m_ssd_dt_bias, m_ssd_a_log, m_ssd_d, m_ssd_norm_g, m_hyb_w_out, m_mla_w_in, m_mla_q_norm_g, m_mla_kv_norm_g, m_mla_w_uq, m_mla_w_ukv, m_mla_w_out, v_ffn1_pre_g, v_ffn1_w_in, v_ffn1_w_down, v_ffn1_post_g, v_mix_pre_g, v_mix_post_g, v_ffn2_pre_g, v_ffn2_w_in, v_ffn2_w_down, v_ffn2_post_g, v_ple_pre_g, v_ple_w_gate, v_ple_w_proj, v_ple_post_g, v_hyb_w_in, v_gm_ln_g, v_gm_ln_b, v_gm_w_s, v_gm_b_s, v_ssd_conv_w, v_ssd_conv_b, v_ssd_dt_bias, v_ssd_a_log, v_ssd_d, v_ssd_norm_g, v_hyb_w_out, v_mla_w_in, v_mla_q_norm_g, v_mla_kv_norm_g, v_mla_w_uq, v_mla_w_ukv, v_mla_w_out):
    given = dict(locals())
    w = {n: given[n] for n in WEIGHTS}
    mom = {n: given["m_" + n] for n in WEIGHTS}
    var = {n: given["v_" + n] for n in WEIGHTS}
    shard_shapes = [w[n].shape for n in SHARDED]
    repl_shapes = [w[n].shape for n in REPLICATED]

    send16 = {n: w[n].astype(BF16) for n in SHARDED_BF16}
    send16["hyb_w_in"] = jnp.pad(send16["hyb_w_in"], ((0, 0), (0, 0), (0, HYB_SHARD_PAD - HYB_SHARD)))
    pack16 = _pack([send16[n] for n in SHARDED_BF16])
    by_chip = _exchange(pack16, ("x", "y"), "gather", "gather_weights_ici")
    by_core = _exchange(by_chip.reshape(-1, PACK_W), ("c",), "gather", "gather_weights_d2d")
    gathered = by_core.reshape(2, 4, -1, PACK_W).transpose(1, 0, 2, 3).reshape(N_DEV, -1, PACK_W)
    fw = {n: _join_from_devices(a, SHARD_AXIS[n])
          for n, a in zip(SHARDED_BF16, _unpack(gathered, [send16[n].shape for n in SHARDED_BF16], (N_DEV,)))}
    small = _exchange(_pack([w[n] for n in SHARDED_F32]), MESH_AXES, "gather", "gather_weights_f32")
    fw.update({n: _join_from_devices(a, SHARD_AXIS[n])
               for n, a in zip(SHARDED_F32, _unpack(small, [w[n].shape for n in SHARDED_F32], (N_DEV,)))})

    loss_local, grad_x, grads = _device_step(x[0], p[:, 0], positions[0], loss_target[0], fw, w)
    loss = lax.psum(loss_local, MESH_AXES)

    per_dev = []
    for n in SHARDED:
        layers = w[n].shape[0]
        whole = (1, *w[n].shape[1:SHARD_AXIS[n]], N_DEV * w[n].shape[SHARD_AXIS[n]], *w[n].shape[SHARD_AXIS[n] + 1:])
        if int(np.prod(w[n].shape[1:])) % (16 * PACK_W) == 0:
            parts = [tuple(h[None] for h in g) if isinstance(g, tuple) else g.reshape(whole) for g in grads[n]]
        else:
            parts = [_stack_layers(grads[n], (layers, *whole[1:]))]
        per_dev.extend(_split_for_devices(g, SHARD_AXIS[n]) for g in parts)
    per_dev = [a.reshape(4, 2, *a.shape[1:]).swapaxes(0, 1) for a in per_dev]
    gpack = _pack(per_dev, (2, 4))
    rows = gpack.shape[2]
    pair = _exchange(gpack.reshape(2, 4 * rows, PACK_W), ("c",), "a2a", "reduce_grads_d2d")
    chip_sum = _rowwise(lambda a, b: a + b, [(pair, 0), (pair, 1)], [], [(PACK_W, BF16)], tm=PACK_TM, name="reduce_grads_pair")
    quads = _exchange(chip_sum.reshape(4, rows, PACK_W), ("x", "y"), "a2a", "reduce_grads_ici")
    g_s, d_s, m_s, v_s = _adamw_packed(_pack([w[n] for n in SHARDED]), _pack([mom[n] for n in SHARDED]),
                                       _pack([var[n] for n in SHARDED]), quads, 4, "adamw_sharded")

    rpack = _pack([_stack_layers(grads[n], w[n].shape) for n in REPLICATED])
    everyone = _exchange(rpack, MESH_AXES, "gather", "gather_small_grads")
    g_r, d_r, m_r, v_r = _adamw_packed(_pack([w[n] for n in REPLICATED]), _pack([mom[n] for n in REPLICATED]),
                                       _pack([var[n] for n in REPLICATED]), everyone, N_DEV, "adamw_replicated")

    outs = []
    for sharded_buf, repl_buf in ((g_s, g_r), (d_s, d_r), (m_s, m_r), (v_s, v_r)):
        vals = dict(zip(SHARDED, _unpack(sharded_buf, shard_shapes)))
        vals.update(zip(REPLICATED, _unpack(repl_buf, repl_shapes)))
        outs.extend(vals[n] for n in WEIGHTS)
    return (loss, grad_x[None], *outs)
```

```python
import functools
import math

import jax
import jax.numpy as jnp
import numpy as np
from jax import lax
from jax.experimental import pallas as pl
from jax.experimental.pallas import tpu as pltpu

F32 = jnp.float32
BF16 = jnp.bfloat16
HIGHEST = lax.Precision.HIGHEST

V7X_VMEM_LIMIT_BYTES = 52 * 1024 * 1024
LANES = 128

D_MODEL = 1024
DEPTH = 4
D_FF = 2816
PLE_DIM = 256
NORM_EPS = 1e-6
LN_EPS = 1e-5
CHUNK = 128
GM_HEADS = 8
SSD_HEADS = 16
SSD_HEAD_DIM = 64
SSD_INNER = 1024
SSD_STATE = 128
SSD_BC = 256
SSD_CONV_CH = 1536
HYB_MAIN = 4608
MLA_HEADS = 16
MLA_Q_LORA = 256
MLA_KV_LORA = 128
MLA_ROPE = 64
MLA_IN = 448
MLA_IN_PAD = 512
ATTN_SCALE = 192.0 ** -0.5
LOG2_E = 1.4426950408889634
LN_2 = 0.6931471805599453
ATTN_QSCALE = ATTN_SCALE * LOG2_E
ROPE_BASE = 10000.0

ADAM_LR = 0.001
ADAM_B1 = 0.9
ADAM_B2 = 0.999
ADAM_EPS = 1e-08
ADAM_WD = 0.01
ADAM_STEP = 10

N_DEV = 8
PACK_W = 1024
PACK_TM = 256

WEIGHTS = ['ffn1_pre_g', 'ffn1_w_in', 'ffn1_w_down', 'ffn1_post_g', 'mix_pre_g', 'mix_post_g', 'ffn2_pre_g',
           'ffn2_w_in', 'ffn2_w_down', 'ffn2_post_g', 'ple_pre_g', 'ple_w_gate', 'ple_w_proj', 'ple_post_g',
           'hyb_w_in', 'gm_ln_g', 'gm_ln_b', 'gm_w_s', 'gm_b_s', 'ssd_conv_w', 'ssd_conv_b', 'ssd_dt_bias',
           'ssd_a_log', 'ssd_d', 'ssd_norm_g', 'hyb_w_out', 'mla_w_in', 'mla_q_norm_g', 'mla_kv_norm_g',
           'mla_w_uq', 'mla_w_ukv', 'mla_w_out']
SHARD_AXIS = {'ffn1_w_in': 2, 'ffn1_w_down': 1, 'ffn2_w_in': 2, 'ffn2_w_down': 1, 'ple_w_gate': 1, 'ple_w_proj': 2,
              'hyb_w_in': 2, 'ssd_conv_w': 2, 'hyb_w_out': 1, 'mla_w_in': 1, 'mla_q_norm_g': 1, 'mla_w_uq': 2,
              'mla_w_ukv': 2, 'mla_w_out': 1}
SHARDED = [n for n in WEIGHTS if n in SHARD_AXIS]
REPLICATED = [n for n in WEIGHTS if n not in SHARD_AXIS]
SHARDED_F32 = ['ssd_conv_w', 'mla_q_norm_g']
SHARDED_BF16 = [n for n in SHARDED if n not in SHARDED_F32]


def _params(*sem):
    return pltpu.CompilerParams(dimension_semantics=sem or None, vmem_limit_bytes=V7X_VMEM_LIMIT_BYTES)


def _pick(n, prefs):
    for t in prefs:
        if t <= n and n % t == 0:
            return t
    return n


def _mm(a, b, *, ta=False, tb=False, out_dtype=F32, tm=1024, tn=512, tk=512, name):
    m, k = (a.shape[1], a.shape[0]) if ta else a.shape
    n = b.shape[0] if tb else b.shape[1]
    assert k == (b.shape[1] if tb else b.shape[0]), (a.shape, b.shape, ta, tb)
    tm, tn, tk = _pick(m, (tm, 512, 256, 128)), _pick(n, (tn, 512, 256, 128)), _pick(k, (tk, 512, 256, 128))
    nk = k // tk
    dims = (((0 if ta else 1,), (1 if tb else 0,)), ((), ()))

    def body(a_ref, b_ref, o_ref, *acc):
        part = lax.dot_general(a_ref[...].astype(BF16), b_ref[...].astype(BF16), dims, preferred_element_type=F32)
        if nk == 1:
            o_ref[...] = part.astype(o_ref.dtype)
            return
        acc_ref, = acc
        kk = pl.program_id(2)

        @pl.when(kk == 0)
        def _():
            acc_ref[...] = part

        @pl.when(kk > 0)
        def _():
            acc_ref[...] += part

        @pl.when(kk == nk - 1)
        def _():
            o_ref[...] = acc_ref[...].astype(o_ref.dtype)

    a_spec = pl.BlockSpec((tk, tm), lambda i, j, kk: (kk, i)) if ta else pl.BlockSpec((tm, tk), lambda i, j, kk: (i, kk))
    b_spec = pl.BlockSpec((tn, tk), lambda i, j, kk: (j, kk)) if tb else pl.BlockSpec((tk, tn), lambda i, j, kk: (kk, j))
    return pl.pallas_call(
        body, name=name, grid=(m // tm, n // tn, nk), in_specs=[a_spec, b_spec],
        out_specs=pl.BlockSpec((tm, tn), lambda i, j, kk: (i, j)), out_shape=jax.ShapeDtypeStruct((m, n), out_dtype),
        scratch_shapes=[] if nk == 1 else [pltpu.VMEM((tm, tn), F32)],
        compiler_params=_params("parallel", "parallel", "arbitrary"),
    )(a, b)


def _rowwise(fn, rows, consts, outs, accs=(), *, tm=256, name):
    first = rows[0][0] if isinstance(rows[0], tuple) else rows[0]
    t = first.shape[-2]
    tm = _pick(t, (tm, 256, 128, 64, 32, 16, 8))
    n_r, n_c, n_o = len(rows), len(consts), len(outs)

    def body(*refs):
        vals = [r[...] for r in refs[:n_r + n_c]]
        res = fn(*vals)
        res = res if isinstance(res, tuple) else (res,)
        o_refs, a_refs = refs[n_r + n_c:n_r + n_c + n_o], refs[n_r + n_c + n_o:]
        for o_ref, v in zip(o_refs, res[:n_o]):
            if isinstance(v, (tuple, list)):
                off = 0
                for piece in v:
                    o_ref[:, off:off + piece.shape[1]] = piece.astype(o_ref.dtype)
                    off += piece.shape[1]
            else:
                o_ref[...] = v.astype(o_ref.dtype)
        if a_refs:
            terms = res[n_o:]
            is_first = pl.program_id(0) == 0

            @pl.when(is_first)
            def _():
                for a_ref, v in zip(a_refs, terms):
                    a_ref[...] = v

            @pl.when(jnp.logical_not(is_first))
            def _():
                for a_ref, v in zip(a_refs, terms):
                    a_ref[...] += v

    in_specs, args = [], []
    for r in rows:
        if isinstance(r, tuple) and len(r) == 3:
            arr, width, cb = r
            in_specs.append(pl.BlockSpec((tm, width), functools.partial(lambda i, c: (i, c), c=cb)))
        elif isinstance(r, tuple):
            arr, slot = r
            in_specs.append(pl.BlockSpec((None, tm, arr.shape[2]), functools.partial(lambda i, s: (s, i, 0), s=slot)))
        else:
            arr = r
            in_specs.append(pl.BlockSpec((tm, arr.shape[1]), lambda i: (i, 0)))
        args.append(arr)
    for c in consts:
        in_specs.append(pl.BlockSpec(c.shape, lambda i: (0, 0)))
        args.append(c)
    out_specs = [pl.BlockSpec((tm, c), lambda i: (i, 0)) for c, _ in outs]
    out_shape = [jax.ShapeDtypeStruct((t, c), dt) for c, dt in outs]
    for shp in accs:
        out_specs.append(pl.BlockSpec(shp, lambda i: (0, 0)))
        out_shape.append(jax.ShapeDtypeStruct(shp, F32))
    res = pl.pallas_call(
        body, name=name, grid=(t // tm,), in_specs=in_specs, out_specs=out_specs, out_shape=out_shape,
        compiler_params=_params("arbitrary" if accs else "parallel"),
    )(*args)
    return res[0] if len(res) == 1 else tuple(res)


def _colsum(v):
    return jnp.sum(v, axis=0, keepdims=True)


def _rms(x, g, eps=NORM_EPS):
    r = lax.rsqrt(jnp.mean(x * x, axis=-1, keepdims=True) + eps)
    return x * r * g


def _rms_bwd(x, g, dy, eps=NORM_EPS):
    r = lax.rsqrt(jnp.mean(x * x, axis=-1, keepdims=True) + eps)
    xh = x * r
    dyg = dy * g
    dx = r * (dyg - xh * jnp.mean(dyg * xh, axis=-1, keepdims=True))
    return dx, dy * xh


def _silu(x):
    return x * jax.nn.sigmoid(x)


def _silu_grad(x):
    s = jax.nn.sigmoid(x)
    return s * (1.0 + x * (1.0 - s))


_GELU_K = math.sqrt(2.0 / math.pi)


def _gelu(x):
    return 0.5 * x * (1.0 + jnp.tanh(_GELU_K * (x + 0.044715 * x * x * x)))


def _gelu_grad(x):
    t = jnp.tanh(_GELU_K * (x + 0.044715 * x * x * x))
    return 0.5 * (1.0 + t) + 0.5 * x * (1.0 - t * t) * _GELU_K * (1.0 + 3.0 * 0.044715 * x * x)


def _prenorm(h, g, name):
    return _rowwise(lambda x, gg: _rms(x, gg), [h], [g], [(D_MODEL, BF16)], name=name)


def _postnorm_residual(h, f, g, scale, next_g, name):
    if next_g is None:
        return _rowwise(lambda x, ff, gg: x + scale * _rms(ff, gg), [h, f], [g], [(D_MODEL, F32)], name=name), None

    def fn(x, ff, gg, ng):
        out = x + scale * _rms(ff, gg)
        return out, _rms(out, ng)
    return _rowwise(fn, [h, f], [g, next_g], [(D_MODEL, F32), (D_MODEL, BF16)], name=name + "_prenorm")


def _postnorm_bwd(f, dh, g, scale, name):
    def fn(ff, d, gg):
        dx, dgt = _rms_bwd(ff, gg, scale * d)
        return dx, _colsum(dgt)
    return _rowwise(fn, [f, dh], [g], [(D_MODEL, BF16)], [(1, D_MODEL)], name=name)


FFN_TM = 1024
FFN_TN = 256


def _ffn_in_swiglu(a, w_in, name):
    t = a.shape[0]
    tm = _pick(t, (FFN_TM, 512, 256, 128))
    nj = D_FF // FFN_TN

    def body(a_ref, wg_ref, wu_ref, gate_ref, up_ref, s_ref):
        av = a_ref[...]
        gate = jnp.dot(av, wg_ref[...], preferred_element_type=F32)
        up = jnp.dot(av, wu_ref[...], preferred_element_type=F32)
        gate_ref[...] = gate
        up_ref[...] = up
        s_ref[...] = (_silu(gate) * up).astype(s_ref.dtype)

    tile = pl.BlockSpec((tm, FFN_TN), lambda i, j: (i, j))
    return pl.pallas_call(
        body, name=name, grid=(t // tm, nj),
        in_specs=[pl.BlockSpec((tm, D_MODEL), lambda i, j: (i, 0)), pl.BlockSpec((D_MODEL, FFN_TN), lambda i, j: (0, j)),
                  pl.BlockSpec((D_MODEL, FFN_TN), lambda i, j: (0, j + nj))],
        out_specs=[tile, tile, tile],
        out_shape=[jax.ShapeDtypeStruct((t, D_FF), F32), jax.ShapeDtypeStruct((t, D_FF), F32),
                   jax.ShapeDtypeStruct((t, D_FF), BF16)],
        compiler_params=_params("parallel", "parallel"),
    )(a, w_in, w_in)


def _ffn_down_dx_swiglu(df, w_down, gate, up, name):
    t = df.shape[0]
    tm = _pick(t, (FFN_TM, 512, 256, 128))

    def body(df_ref, wd_ref, gate_ref, up_ref, dgate_ref, dup_ref):
        ds = lax.dot_general(df_ref[...], wd_ref[...], (((1,), (1,)), ((), ())), preferred_element_type=F32)
        gate = gate_ref[...]
        sg = jax.nn.sigmoid(gate)
        dgate_ref[...] = (ds * up_ref[...] * (sg * (1.0 + gate * (1.0 - sg)))).astype(dgate_ref.dtype)
        dup_ref[...] = (ds * (gate * sg)).astype(dup_ref.dtype)

    tile = pl.BlockSpec((tm, FFN_TN), lambda i, j: (i, j))
    return pl.pallas_call(
        body, name=name, grid=(t // tm, D_FF // FFN_TN),
        in_specs=[pl.BlockSpec((tm, D_MODEL), lambda i, j: (i, 0)), pl.BlockSpec((FFN_TN, D_MODEL), lambda i, j: (j, 0)),
                  tile, tile],
        out_specs=[tile, tile],
        out_shape=[jax.ShapeDtypeStruct((t, D_FF), BF16), jax.ShapeDtypeStruct((t, D_FF), BF16)],
        compiler_params=_params("parallel", "parallel"),
    )(df, w_down, gate, up)


def _ffn_fwd(h, a, w, pre_g, post_g, next_g, tag):
    if a is None:
        a = _prenorm(h, pre_g, tag + "_prenorm")
    gate, up, s = _ffn_in_swiglu(a, w["in"], tag + "_in_swiglu")
    f = _mm(s, w["down"], tm=1024, tn=1024, tk=D_FF, name=tag + "_down")
    out, a_next = _postnorm_residual(h, f, post_g, 0.5, next_g, tag + "_postnorm")
    return out, a_next, (h, a, gate, up, s, f)


DX_TM = 1024


def _dx_prenorm_bwd(parts, h, dh, pre_g, name):
    t = h.shape[0]
    tm = _pick(t, (DX_TM, 256, 128))
    counts = [d.shape[1] // tk for d, _, _, tk in parts]
    starts = [sum(counts[:p]) for p in range(len(parts))]
    nk = sum(counts)
    n_p = len(parts)
    windows, window_of = [], []
    for (d, w, k0, tk), start, count in zip(parts, starts, counts):
        assert d.shape[1] % tk == 0, (d.shape, tk)
        if w is None:
            pw, pb, ptk, ps, pc = windows[-1]
            assert ptk == tk
            windows[-1] = (pw, pb, ptk, ps, pc + count)
        else:
            assert k0 % tk == 0, (k0, tk)
            windows.append((w, k0 // tk, tk, start, count))
        window_of.append(len(windows) - 1)

    def body(*refs):
        h_ref, dh_ref, g_ref, out_ref, dg_ref, acc_ref = refs[n_p + len(windows):]
        i, kk = pl.program_id(0), pl.program_id(1)
        for p in range(n_p):
            d_ref, w_ref = refs[p], refs[n_p + window_of[p]]

            def contribution(d_ref=d_ref, w_ref=w_ref):
                return lax.dot_general(d_ref[...], w_ref[...], (((1,), (1,)), ((), ())), preferred_element_type=F32)

            lo = starts[p] + (1 if p == 0 else 0)
            if p == 0:
                @pl.when(kk == 0)
                def _(contribution=contribution):
                    acc_ref[...] = contribution()

            if starts[p] + counts[p] > lo:
                @pl.when(jnp.logical_and(kk >= lo, kk < starts[p] + counts[p]))
                def _(contribution=contribution):
                    acc_ref[...] += contribution()

        @pl.when(kk == nk - 1)
        def _():
            dx, dgt = _rms_bwd(h_ref[...], g_ref[...], acc_ref[...])
            out_ref[...] = dh_ref[...] + dx
            dg = _colsum(dgt)

            @pl.when(i == 0)
            def _():
                dg_ref[...] = dg

            @pl.when(i > 0)
            def _():
                dg_ref[...] += dg

    def walk(start, count):
        return functools.partial(lambda kk, s, c: jnp.clip(kk - s, 0, c - 1), s=start, c=count)

    in_specs, args = [], []
    for (d, _, _, tk), start, count in zip(parts, starts, counts):
        in_specs.append(pl.BlockSpec((tm, tk), functools.partial(lambda i, kk, st: (i, st(kk)), st=walk(start, count))))
        args.append(d)
    for w, b0, tk, start, count in windows:
        in_specs.append(pl.BlockSpec((D_MODEL, tk), functools.partial(lambda i, kk, st, b0: (0, b0 + st(kk)),
                                                                      st=walk(start, count), b0=b0)))
        args.append(w)
    row = pl.BlockSpec((tm, D_MODEL), lambda i, kk: (i, 0))
    vec = pl.BlockSpec((1, D_MODEL), lambda i, kk: (0, 0))
    return pl.pallas_call(
        body, name=name, grid=(t // tm, nk), in_specs=in_specs + [row, row, vec], out_specs=[row, vec],
        out_shape=[jax.ShapeDtypeStruct((t, D_MODEL), F32), jax.ShapeDtypeStruct((1, D_MODEL), F32)],
        scratch_shapes=[pltpu.VMEM((tm, D_MODEL), F32)],
        compiler_params=_params("arbitrary", "arbitrary"),
    )(*args, h, dh, pre_g)


def _ffn_bwd(dh, saved, w, pre_g, post_g, tag):
    h, a, gate, up, s, f = saved
    df, d_post = _postnorm_bwd(f, dh, post_g, 0.5, tag + "_postnorm_bwd")
    dgate, dup = _ffn_down_dx_swiglu(df, w["down"], gate, up, tag + "_down_dx_swiglu")
    d_down = _mm(s, df, ta=True, tm=1408, tn=1024, tk=1024, name=tag + "_down_dw")
    d_in = (_mm(a, dgate, ta=True, tm=1024, tn=1408, tk=1024, name=tag + "_in_dw_gate"),
            _mm(a, dup, ta=True, tm=1024, tn=1408, tk=1024, name=tag + "_in_dw_up"))
    dh_in, d_pre = _dx_prenorm_bwd([(dgate, w["in"], 0, 1408), (dup, None, None, 1408)], h, dh, pre_g,
                                   tag + "_in_dx_prenorm_bwd")
    return dh_in, dict(w_in=d_in, w_down=d_down, pre_g=d_pre, post_g=d_post)


def _ple_fwd(h, a, p_i, w, post_g, next_g):
    gl = _mm(a, w["gate"], tm=1024, tn=1024, tk=1024, name="ple_gate")
    e = _mm(p_i, w["proj"], tm=1024, tn=1024, tk=PLE_DIM, name="ple_proj")
    if next_g is None:
        out = _rowwise(lambda x, g_, e_, gg: x + _rms(jax.nn.sigmoid(g_) * e_, gg), [h, gl, e], [post_g],
                       [(D_MODEL, F32)], name="ple_out")
        return out, None, (h, a, gl, e)

    def fn(x, g_, e_, gg, ng):
        out = x + _rms(jax.nn.sigmoid(g_) * e_, gg)
        return out, _rms(out, ng)
    out, a_next = _rowwise(fn, [h, gl, e], [post_g, next_g], [(D_MODEL, F32), (D_MODEL, BF16)], name="ple_out_prenorm")
    return out, a_next, (h, a, gl, e)


def _ple_bwd(dh, saved, p_i, w, pre_g, post_g):
    h, a, gl, e = saved

    def fn(g_, e_, d, gg):
        sg = jax.nn.sigmoid(g_)
        du, dgt = _rms_bwd(sg * e_, gg, d)
        return du * e_ * sg * (1.0 - sg), du * sg, _colsum(dgt)
    dgl, de, d_post = _rowwise(fn, [gl, e, dh], [post_g], [(D_MODEL, BF16), (D_MODEL, BF16)], [(1, D_MODEL)],
                               name="ple_out_bwd")
    d_gate = _mm(a, dgl, ta=True, tm=1024, tn=1024, tk=2048, name="ple_gate_dw")
    d_proj = _mm(p_i, de, ta=True, tm=PLE_DIM, tn=1024, tk=2048, name="ple_proj_dw")
    dh_in, d_pre = _dx_prenorm_bwd([(dgl, w["gate"], 0, 1024)], h, dh, pre_g, "ple_gate_dx_prenorm_bwd")
    return dh_in, dict(w_gate=d_gate, w_proj=d_proj, pre_g=d_pre, post_g=d_post)


def _gm_layernorm(v, g, b):
    mu = jnp.mean(v, axis=-1, keepdims=True)
    xc = v - mu
    rstd = lax.rsqrt(jnp.mean(xc * xc, axis=-1, keepdims=True) + LN_EPS)
    vhat = xc * rstd
    return vhat, rstd, vhat * g + b


def _gmlp_fwd(proj, wm, bias_t, ln_g, ln_b, name):
    t = proj.shape[0]

    def body(uv_ref, wm_ref, bt_ref, g_ref, b_ref, o_ref):
        for hd in range(GM_HEADS):
            lo = hd * LANES
            u = _gelu(uv_ref[:, lo:lo + LANES])
            v = _gelu(uv_ref[:, 1024 + lo:1024 + lo + LANES])
            _, _, vln = _gm_layernorm(v, g_ref[:, lo:lo + LANES], b_ref[:, lo:lo + LANES])
            mixed = jnp.dot(wm_ref[hd], vln.astype(BF16), preferred_element_type=F32) + bt_ref[:, hd:hd + 1]
            o_ref[:, lo:lo + LANES] = (u * mixed).astype(o_ref.dtype)

    return pl.pallas_call(
        body, name=name, grid=(t // CHUNK,),
        in_specs=[pl.BlockSpec((CHUNK, 2048), lambda i: (i, 0)), pl.BlockSpec(wm.shape, lambda i: (0, 0, 0)),
                  pl.BlockSpec(bias_t.shape, lambda i: (0, 0)), pl.BlockSpec(ln_g.shape, lambda i: (0, 0)),
                  pl.BlockSpec(ln_b.shape, lambda i: (0, 0))],
        out_specs=pl.BlockSpec((CHUNK, 1024), lambda i: (i, 0)), out_shape=jax.ShapeDtypeStruct((t, 1024), BF16),
        compiler_params=_params("parallel"),
    )(proj, wm, bias_t, ln_g, ln_b)


def _gmlp_bwd(proj, dyab, wm, bias_t, ln_g, ln_b, name):
    t = proj.shape[0]
    nc = t // CHUNK

    def body(uv_ref, dy_ref, wm_ref, bt_ref, g_ref, b_ref, duv_ref, dw_ref, db_ref, dg_ref, dbeta_ref, dbacc):
        c = pl.program_id(0)

        @pl.when(c == 0)
        def _():
            dw_ref[...] = jnp.zeros_like(dw_ref)
            dbacc[...] = jnp.zeros_like(dbacc)
            dg_ref[...] = jnp.zeros_like(dg_ref)
            dbeta_ref[...] = jnp.zeros_like(dbeta_ref)

        for hd in range(GM_HEADS):
            lo = hd * LANES
            xu = uv_ref[:, lo:lo + LANES]
            xv = uv_ref[:, 1024 + lo:1024 + lo + LANES]
            u = _gelu(xu)
            g_h = g_ref[:, lo:lo + LANES]
            vhat, rstd, vln = _gm_layernorm(_gelu(xv), g_h, b_ref[:, lo:lo + LANES])
            vln16 = vln.astype(BF16)
            mixed = jnp.dot(wm_ref[hd], vln16, preferred_element_type=F32) + bt_ref[:, hd:hd + 1]
            dy = dy_ref[:, lo:lo + LANES]
            du = dy * mixed
            dmix = dy * u
            dmix16 = dmix.astype(BF16)
            dw_ref[hd] += lax.dot_general(dmix16, vln16, (((1,), (1,)), ((), ())), preferred_element_type=F32)
            dbacc[hd] += dmix
            dvln = lax.dot_general(wm_ref[hd], dmix16, (((0,), (0,)), ((), ())), preferred_element_type=F32)
            dg_ref[:, lo:lo + LANES] += _colsum(dvln * vhat)
            dbeta_ref[:, lo:lo + LANES] += _colsum(dvln)
            dvh = dvln * g_h
            dv = rstd * (dvh - jnp.mean(dvh, axis=-1, keepdims=True)
                         - vhat * jnp.mean(dvh * vhat, axis=-1, keepdims=True))
            duv_ref[:, lo:lo + LANES] = (du * _gelu_grad(xu)).astype(duv_ref.dtype)
            duv_ref[:, 1024 + lo:1024 + lo + LANES] = (dv * _gelu_grad(xv)).astype(duv_ref.dtype)

        @pl.when(c == nc - 1)
        def _():
            row = lax.broadcasted_iota(jnp.int32, (CHUNK, CHUNK), 0)
            col = lax.broadcasted_iota(jnp.int32, (CHUNK, CHUNK), 1)
            for hd in range(GM_HEADS):
                dw_ref[hd] = jnp.where(col <= row, dw_ref[hd], 0.0)
                db_ref[hd] = jnp.sum(dbacc[hd], axis=1, keepdims=True)

    return pl.pallas_call(
        body, name=name, grid=(nc,),
        in_specs=[pl.BlockSpec((CHUNK, 2048), lambda i: (i, 0)), pl.BlockSpec((CHUNK, 1024), lambda i: (i, 0)),
                  pl.BlockSpec(wm.shape, lambda i: (0, 0, 0)), pl.BlockSpec(bias_t.shape, lambda i: (0, 0)),
                  pl.BlockSpec(ln_g.shape, lambda i: (0, 0)), pl.BlockSpec(ln_b.shape, lambda i: (0, 0))],
        out_specs=[pl.BlockSpec((CHUNK, 2048), lambda i: (i, 0)), pl.BlockSpec((GM_HEADS, CHUNK, CHUNK), lambda i: (0, 0, 0)),
                   pl.BlockSpec((GM_HEADS, CHUNK, 1), lambda i: (0, 0, 0)), pl.BlockSpec((1, 1024), lambda i: (0, 0)),
                   pl.BlockSpec((1, 1024), lambda i: (0, 0))],
        out_shape=[jax.ShapeDtypeStruct((t, 2048), BF16), jax.ShapeDtypeStruct((GM_HEADS, CHUNK, CHUNK), F32),
                   jax.ShapeDtypeStruct((GM_HEADS, CHUNK, 1), F32), jax.ShapeDtypeStruct((1, 1024), F32),
                   jax.ShapeDtypeStruct((1, 1024), F32)],
        scratch_shapes=[pltpu.VMEM((GM_HEADS, CHUNK, CHUNK), F32)],
        compiler_params=_params("arbitrary"),
    )(proj, dyab, wm, bias_t, ln_g, ln_b)


def _ssd_chunk_terms(dt_pad, a_pad):
    row = lax.broadcasted_iota(jnp.int32, (CHUNK, CHUNK), 0)
    col = lax.broadcasted_iota(jnp.int32, (CHUNK, CHUNK), 1)
    tril = jnp.where(col <= row, 1.0, 0.0).astype(F32)
    a_cs = jnp.dot(tril, dt_pad * a_pad, precision=HIGHEST, preferred_element_type=F32)
    return a_cs, a_cs.T


def _pair_cols(mat, hd_a, lane_lt64):
    return jnp.where(lane_lt64, mat[:, hd_a:hd_a + 1], mat[:, hd_a + 1:hd_a + 2])


def _head_decay(a_cs, a_cs_t, hd, causal):
    seg = a_cs[:, hd:hd + 1] - a_cs_t[hd:hd + 1, :]
    return jnp.exp(jnp.where(causal, seg, -jnp.inf))


def _ssd_fwd(act, dt_pad, a_pad, d_pad, name):
    t = act.shape[0]
    nc = t // CHUNK

    def body(act_ref, dt_ref, a_ref, d_ref, y_ref, st_ref, h_sc):
        c = pl.program_id(0)

        @pl.when(c == 0)
        def _():
            h_sc[...] = jnp.zeros_like(h_sc)

        st_ref[...] = h_sc[...]
        row = lax.broadcasted_iota(jnp.int32, (CHUNK, CHUNK), 0)
        col = lax.broadcasted_iota(jnp.int32, (CHUNK, CHUNK), 1)
        causal = col <= row
        lane_lt64 = lax.broadcasted_iota(jnp.int32, (CHUNK, LANES), 1) < SSD_HEAD_DIM
        row_lt64 = lax.broadcasted_iota(jnp.int32, (LANES, 1), 0) < SSD_HEAD_DIM
        dt = dt_ref[...]
        a_cs, a_cs_t = _ssd_chunk_terms(dt, a_ref[...])
        last = a_cs[CHUNK - 1:CHUNK, :]
        for g in range(2):
            b16 = act_ref[:, SSD_INNER + g * SSD_STATE:SSD_INNER + (g + 1) * SSD_STATE].astype(BF16)
            c16 = act_ref[:, SSD_INNER + SSD_BC + g * SSD_STATE:SSD_INNER + SSD_BC + (g + 1) * SSD_STATE].astype(BF16)
            cb = lax.dot_general(c16, b16, (((1,), (1,)), ((), ())), preferred_element_type=F32)
            for pr in range(4):
                ha = g * 8 + pr * 2
                lo = ha * SSD_HEAD_DIM
                xs = act_ref[:, lo:lo + LANES]
                xd = xs * _pair_cols(dt, ha, lane_lt64)
                xd16 = xd.astype(BF16)
                ya = jnp.dot((cb * _head_decay(a_cs, a_cs_t, ha, causal)).astype(BF16), xd16, preferred_element_type=F32)
                yb = jnp.dot((cb * _head_decay(a_cs, a_cs_t, ha + 1, causal)).astype(BF16), xd16, preferred_element_type=F32)
                a_p = _pair_cols(a_cs, ha, lane_lt64)
                hp = h_sc[lo:lo + LANES, :]
                y_off = lax.dot_general(c16, hp.astype(BF16), (((1,), (1,)), ((), ())), preferred_element_type=F32)
                d_p = jnp.where(lane_lt64[:1], d_ref[:, ha:ha + 1], d_ref[:, ha + 1:ha + 2])
                y_ref[:, lo:lo + LANES] = jnp.where(lane_lt64, ya, yb) + y_off * jnp.exp(a_p) + d_p * xs
                last_p = jnp.where(lane_lt64[:1], last[:, ha:ha + 1], last[:, ha + 1:ha + 2])
                xw16 = (xd * jnp.exp(last_p - a_p)).astype(BF16)
                s_new = lax.dot_general(xw16, b16, (((0,), (0,)), ((), ())), preferred_element_type=F32)
                t_col = jnp.where(row_lt64, jnp.exp(last[:, ha:ha + 1]), jnp.exp(last[:, ha + 1:ha + 2]))
                h_sc[lo:lo + LANES, :] = t_col * hp + s_new

    return pl.pallas_call(
        body, name=name, grid=(nc,),
        in_specs=[pl.BlockSpec((CHUNK, SSD_CONV_CH), lambda i: (i, 0)), pl.BlockSpec((CHUNK, LANES), lambda i: (i, 0)),
                  pl.BlockSpec((1, LANES), lambda i: (0, 0)), pl.BlockSpec((1, LANES), lambda i: (0, 0))],
        out_specs=[pl.BlockSpec((CHUNK, SSD_INNER), lambda i: (i, 0)),
                   pl.BlockSpec((None, SSD_INNER, SSD_STATE), lambda i: (i, 0, 0))],
        out_shape=[jax.ShapeDtypeStruct((t, SSD_INNER), F32), jax.ShapeDtypeStruct((nc, SSD_INNER, SSD_STATE), F32)],
        scratch_shapes=[pltpu.VMEM((SSD_INNER, SSD_STATE), F32)],
        compiler_params=_params("arbitrary"),
    )(act, dt_pad, a_pad, d_pad)


def _ssd_bwd(act, dt_pad, a_pad, d_pad, states, dy, name):
    t = act.shape[0]
    nc = t // CHUNK

    def body(act_ref, dt_ref, a_ref, d_ref, st_ref, dy_ref, dact_ref, ddt_ref, da_ref, dd_ref, dh_sc):
        c = pl.program_id(0)

        @pl.when(c == 0)
        def _():
            dh_sc[...] = jnp.zeros_like(dh_sc)
            da_ref[...] = jnp.zeros_like(da_ref)
            dd_ref[...] = jnp.zeros_like(dd_ref)

        row = lax.broadcasted_iota(jnp.int32, (CHUNK, CHUNK), 0)
        col = lax.broadcasted_iota(jnp.int32, (CHUNK, CHUNK), 1)
        causal = col <= row
        lane = lax.broadcasted_iota(jnp.int32, (CHUNK, LANES), 1)
        lane_lt64 = lane < SSD_HEAD_DIM
        row_lt64 = lax.broadcasted_iota(jnp.int32, (LANES, 1), 0) < SSD_HEAD_DIM
        is_last = lax.broadcasted_iota(jnp.int32, (CHUNK, 1), 0) == CHUNK - 1
        dt = dt_ref[...]
        a_cs, a_cs_t = _ssd_chunk_terms(dt, a_ref[...])
        last = a_cs[CHUNK - 1:CHUNK, :]
        d_acs = jnp.zeros((CHUNK, LANES), F32)
        d_acs_rows = jnp.zeros((LANES, CHUNK), F32)
        head_row = lax.broadcasted_iota(jnp.int32, (LANES, CHUNK), 0)
        ddt_x = jnp.zeros((CHUNK, LANES), F32)
        dd_acc = jnp.zeros((1, LANES), F32)

        def head_sum(v, first):
            return jnp.sum(jnp.where(lane_lt64 if first else jnp.logical_not(lane_lt64), v, 0.0), axis=1, keepdims=True)

        for g in range(2):
            b_lo = SSD_INNER + g * SSD_STATE
            c_lo = SSD_INNER + SSD_BC + g * SSD_STATE
            b16 = act_ref[:, b_lo:b_lo + SSD_STATE].astype(BF16)
            c16 = act_ref[:, c_lo:c_lo + SSD_STATE].astype(BF16)
            cb = lax.dot_general(c16, b16, (((1,), (1,)), ((), ())), preferred_element_type=F32)
            dcb = jnp.zeros((CHUNK, CHUNK), F32)
            db_g = jnp.zeros((CHUNK, SSD_STATE), F32)
            dc_g = jnp.zeros((CHUNK, SSD_STATE), F32)
            for pr in range(4):
                ha = g * 8 + pr * 2
                lo = ha * SSD_HEAD_DIM
                xs = act_ref[:, lo:lo + LANES]
                dt_p = _pair_cols(dt, ha, lane_lt64)
                xd = xs * dt_p
                xd16 = xd.astype(BF16)
                a_p = _pair_cols(a_cs, ha, lane_lt64)
                exp_a = jnp.exp(a_p)
                last_p = jnp.where(lane_lt64[:1], last[:, ha:ha + 1], last[:, ha + 1:ha + 2])
                w_p = jnp.exp(last_p - a_p)
                hp = st_ref[lo:lo + LANES, :]
                hp16 = hp.astype(BF16)
                dhn = dh_sc[lo:lo + LANES, :]
                dhn16 = dhn.astype(BF16)
                dyp = dy_ref[:, lo:lo + LANES]
                d_p = jnp.where(lane_lt64[:1], d_ref[:, ha:ha + 1], d_ref[:, ha + 1:ha + 2])
                dd_row = _colsum(dyp * xs)
                dd_acc = dd_acc + jnp.where(lane[:1] == ha, jnp.sum(jnp.where(lane_lt64[:1], dd_row, 0.0), axis=1, keepdims=True), 0.0) \
                    + jnp.where(lane[:1] == ha + 1, jnp.sum(jnp.where(lane_lt64[:1], 0.0, dd_row), axis=1, keepdims=True), 0.0)
                g_off = lax.dot_general(c16, hp16, (((1,), (1,)), ((), ())), preferred_element_type=F32)
                dg16 = (dyp * exp_a).astype(BF16)
                dc_g = dc_g + jnp.dot(dg16, hp16, preferred_element_type=F32)
                dh_prev = lax.dot_general(dg16, c16, (((0,), (0,)), ((), ())), preferred_element_type=F32)
                off_term = dyp * g_off * exp_a
                q = lax.dot_general(b16, dhn16, (((1,), (1,)), ((), ())), preferred_element_type=F32)
                xw16 = (xd * w_p).astype(BF16)
                db_g = db_g + jnp.dot(xw16, dhn16, preferred_element_type=F32)
                dw_term = xd * q * w_p
                dxd = w_p * q
                dt_all = dhn * hp
                dyp16 = dyp.astype(BF16)
                for k, first in ((0, True), (1, False)):
                    hd = ha + k
                    sel = lane_lt64 if first else jnp.logical_not(lane_lt64)
                    decay = _head_decay(a_cs, a_cs_t, hd, causal)
                    m = cb * decay
                    dy_h = jnp.where(sel, dyp16, jnp.zeros_like(dyp16))
                    dm = lax.dot_general(dy_h, xd16, (((1,), (1,)), ((), ())), preferred_element_type=F32)
                    dcb = dcb + dm * decay
                    dseg = dm * m
                    dxd = dxd + jnp.where(sel, lax.dot_general(m.astype(BF16), dyp16, (((0,), (0,)), ((), ())),
                                                               preferred_element_type=F32), 0.0)
                    d_col = jnp.sum(dseg, axis=1, keepdims=True)
                    d_acs_rows = d_acs_rows + jnp.where(head_row == hd, jnp.sum(dseg, axis=0, keepdims=True), 0.0)
                    dw_col = head_sum(dw_term, first)
                    d_col = d_col + head_sum(off_term, first) - dw_col
                    t_h = jnp.exp(last[:, hd:hd + 1])
                    dt_sum = jnp.sum(jnp.sum(jnp.where(row_lt64 if first else jnp.logical_not(row_lt64), dt_all, 0.0),
                                             axis=0, keepdims=True), axis=1, keepdims=True)
                    end_term = jnp.sum(dw_col, axis=0, keepdims=True) + dt_sum * t_h
                    d_col = d_col + jnp.where(is_last, end_term, 0.0)
                    d_acs = d_acs + jnp.where(lane == hd, d_col, 0.0)
                t_col = jnp.where(row_lt64, jnp.exp(last[:, ha:ha + 1]), jnp.exp(last[:, ha + 1:ha + 2]))
                dh_sc[lo:lo + LANES, :] = t_col * dhn + dh_prev
                dact_ref[:, lo:lo + LANES] = d_p * dyp + dxd * dt_p
                ddt_all = dxd * xs
                ddt_x = ddt_x + jnp.where(lane == ha, head_sum(ddt_all, True), 0.0) \
                    + jnp.where(lane == ha + 1, head_sum(ddt_all, False), 0.0)
            dcb16 = dcb.astype(BF16)
            dact_ref[:, b_lo:b_lo + SSD_STATE] = db_g + lax.dot_general(dcb16, c16, (((0,), (0,)), ((), ())),
                                                                          preferred_element_type=F32)
            dact_ref[:, c_lo:c_lo + SSD_STATE] = dc_g + jnp.dot(dcb16, b16, preferred_element_type=F32)
        triu = jnp.where(col >= row, 1.0, 0.0).astype(F32)
        dda = jnp.dot(triu, d_acs - d_acs_rows.T, precision=HIGHEST, preferred_element_type=F32)
        ddt_ref[...] = dda * a_ref[...] + ddt_x
        da_ref[...] += _colsum(dda * dt)
        dd_ref[...] += dd_acc

    rev = lambda i: (nc - 1 - i, 0)
    return pl.pallas_call(
        body, name=name, grid=(nc,),
        in_specs=[pl.BlockSpec((CHUNK, SSD_CONV_CH), rev), pl.BlockSpec((CHUNK, LANES), rev),
                  pl.BlockSpec((1, LANES), lambda i: (0, 0)), pl.BlockSpec((1, LANES), lambda i: (0, 0)),
                  pl.BlockSpec((None, SSD_INNER, SSD_STATE), lambda i: (nc - 1 - i, 0, 0)),
                  pl.BlockSpec((CHUNK, SSD_INNER), rev)],
        out_specs=[pl.BlockSpec((CHUNK, SSD_CONV_CH), rev), pl.BlockSpec((CHUNK, LANES), rev),
                   pl.BlockSpec((1, LANES), lambda i: (0, 0)), pl.BlockSpec((1, LANES), lambda i: (0, 0))],
        out_shape=[jax.ShapeDtypeStruct((t, SSD_CONV_CH), F32), jax.ShapeDtypeStruct((t, LANES), F32),
                   jax.ShapeDtypeStruct((1, LANES), F32), jax.ShapeDtypeStruct((1, LANES), F32)],
        scratch_shapes=[pltpu.VMEM((SSD_INNER, SSD_STATE), F32)],
        compiler_params=_params("arbitrary"),
    )(act, dt_pad, a_pad, d_pad, states, dy)


def _shift_down(x, k):
    return x if k == 0 else jnp.pad(x, ((k, 0), (0, 0)))[:x.shape[0]]


def _shift_up(x, k):
    return x if k == 0 else jnp.pad(x, ((0, k), (0, 0)))[k:]


def _conv_pre(x0, x1, x2, x3, w, b):
    return x0 * w[0:1] + x1 * w[1:2] + x2 * w[2:3] + x3 * w[3:4] + b


def _rope128(x, cpad, s_lo, s_hi):
    return x * cpad + pltpu.roll(x, 96, 1) * s_lo + pltpu.roll(x, 32, 1) * s_hi


ATTN_ROW_SPLIT = 4
ATTN_ROW_SPLIT_DKV = 4


def _diag_mask(rows, cols, row0):
    return lax.broadcasted_iota(jnp.int32, (rows, cols), 1) <= row0 + lax.broadcasted_iota(jnp.int32, (rows, cols), 0)


def _attn_scores(q, k):
    return lax.dot_general(q, k, (((1,), (1,)), ((), ())), preferred_element_type=F32)


def _causal_pairs(nq, by_key):
    if by_key:
        pairs = [(i, j) for j in range(nq) for i in range(j, nq)]
    else:
        pairs = [(i, j) for i in range(nq) for j in range(i + 1)]
    return (jnp.asarray([pr[0] for pr in pairs], jnp.int32), jnp.asarray([pr[1] for pr in pairs], jnp.int32))


def _attn_fwd(qf, kf, kvf, *, tq, name):
    t = qf.shape[0]
    nq = t // tq
    tk = tq
    qi, kj = _causal_pairs(nq, by_key=False)
    rs = tq // ATTN_ROW_SPLIT

    def body(qi_ref, kj_ref, q_ref, k_ref, v_ref, o_ref, lse_ref, m_sc, acc_sc, v1_sc):
        pp = pl.program_id(1)
        i, j = qi_ref[pp], kj_ref[pp]

        @pl.when(pp == 0)
        def _():
            v1_sc[:, LANES:] = jnp.ones((tk, LANES), BF16)

        @pl.when(j == 0)
        def _():
            m_sc[...] = jnp.full_like(m_sc, -jnp.inf)
            acc_sc[...] = jnp.zeros_like(acc_sc)

        v1_sc[:, :LANES] = v_ref[...]

        def update(diag):
            for r in range(ATTN_ROW_SPLIT):
                rows = slice(r * rs, (r + 1) * rs)
                keys = slice(0, (r + 1) * rs if diag else tk)
                s = _attn_scores(q_ref[rows, :], k_ref[keys, :])
                if diag:
                    s = jnp.where(_diag_mask(rs, keys.stop, r * rs), s, -jnp.inf)
                m_prev = m_sc[rows, :]
                m_new = jnp.maximum(m_prev, jnp.max(s, axis=1, keepdims=True))
                p = jnp.exp2(s - m_new).astype(BF16)
                alpha = jnp.exp2(m_prev - m_new)
                acc = alpha * acc_sc[rows, :] + jnp.dot(p, v1_sc[keys, :], preferred_element_type=F32)
                if diag:
                    o_ref[rows, :] = (acc[:, :LANES] / acc[:, LANES:]).astype(o_ref.dtype)
                    lse_ref[rows, :] = m_new + jnp.log2(acc[:, LANES:LANES + 1])
                else:
                    acc_sc[rows, :] = acc
                    m_sc[rows, :] = m_new

        @pl.when(j < i)
        def _():
            update(False)

        @pl.when(j == i)
        def _():
            update(True)

    return pl.pallas_call(
        body, name=name,
        grid_spec=pltpu.PrefetchScalarGridSpec(
            num_scalar_prefetch=2, grid=(MLA_HEADS, int(qi.shape[0])),
            in_specs=[pl.BlockSpec((tq, 2 * LANES), lambda h, pp, qi_, kj_: (qi_[pp], h)),
                      pl.BlockSpec((tk, 2 * LANES), lambda h, pp, qi_, kj_: (kj_[pp], h)),
                      pl.BlockSpec((tk, LANES), lambda h, pp, qi_, kj_: (kj_[pp], 2 * h + 1))],
            out_specs=[pl.BlockSpec((tq, LANES), lambda h, pp, qi_, kj_: (qi_[pp], h)),
                       pl.BlockSpec((None, tq, 1), lambda h, pp, qi_, kj_: (h, qi_[pp], 0))],
            scratch_shapes=[pltpu.VMEM((tq, 1), F32), pltpu.VMEM((tq, 2 * LANES), F32), pltpu.VMEM((tk, 2 * LANES), BF16)]),
        out_shape=[jax.ShapeDtypeStruct((t, MLA_HEADS * LANES), BF16), jax.ShapeDtypeStruct((MLA_HEADS, t, 1), F32)],
        compiler_params=_params("arbitrary", "arbitrary"),
    )(qi, kj, qf, kf, kvf)


def _attn_bwd(qf, kf, kvf, o, do, lse, *, tq, name):
    t = qf.shape[0]
    nq = t // tq
    tk = tq
    qi, kj = _causal_pairs(nq, by_key=True)
    rs = tq // ATTN_ROW_SPLIT_DKV

    def body(qi_ref, kj_ref, q_ref, k_ref, v_ref, o_ref, do_ref, lse_ref, dkv_ref, dkr_ref, dq_ref, dk_sc, dv_sc):
        pp = pl.program_id(1)
        i, j = qi_ref[pp], kj_ref[pp]
        tn = (((0,), (0,)), ((), ()))

        @pl.when(pp == 0)
        def _():
            dq_ref[...] = jnp.zeros_like(dq_ref)

        def update(diag):
            if diag:
                dv_sc[...] = jnp.zeros_like(dv_sc)
                dk_sc[...] = jnp.zeros_like(dk_sc)
            for r in range(ATTN_ROW_SPLIT_DKV):
                rows = slice(r * rs, (r + 1) * rs)
                keys = slice(0, (r + 1) * rs if diag else tk)
                do_ = do_ref[rows, :]
                delta = jnp.sum(do_.astype(F32) * o_ref[rows, :].astype(F32), axis=1, keepdims=True)
                s = _attn_scores(q_ref[rows, :], k_ref[keys, :])
                p = jnp.exp2(s - lse_ref[rows, :])
                if diag:
                    p = jnp.where(_diag_mask(rs, keys.stop, r * rs), p, 0.0)
                dp = lax.dot_general(do_, v_ref[keys, :], (((1,), (1,)), ((), ())), preferred_element_type=F32)
                ds = (p * (dp - delta)).astype(BF16)
                dv_sc[keys, :] += lax.dot_general(p.astype(BF16), do_, tn, preferred_element_type=F32)
                dk_sc[keys, :] += lax.dot_general(ds, q_ref[rows, :], tn, preferred_element_type=F32)
                q_rows = pl.ds(pl.multiple_of(i * tq + r * rs, rs), rs)
                dq_ref[q_rows, :] += jnp.dot(ds, k_ref[keys, :], preferred_element_type=F32)

        @pl.when(i > j)
        def _():
            update(False)

        @pl.when(i == j)
        def _():
            update(True)

        @pl.when(i == nq - 1)
        def _():
            dkv_ref[:, :LANES] = (dk_sc[:, :LANES] * LN_2).astype(dkv_ref.dtype)
            dkv_ref[:, LANES:] = dv_sc[...].astype(dkv_ref.dtype)
            dkr_ref[...] = dk_sc[:, LANES:] * LN_2

    qblk = lambda c: (lambda h, pp, qi_, kj_: (qi_[pp], c(h)))
    kblk = lambda c: (lambda h, pp, qi_, kj_: (kj_[pp], c(h)))
    return pl.pallas_call(
        body, name=name,
        grid_spec=pltpu.PrefetchScalarGridSpec(
            num_scalar_prefetch=2, grid=(MLA_HEADS, int(qi.shape[0])),
            in_specs=[pl.BlockSpec((tq, 2 * LANES), qblk(lambda h: h)), pl.BlockSpec((tk, 2 * LANES), kblk(lambda h: h)),
                      pl.BlockSpec((tk, LANES), kblk(lambda h: 2 * h + 1)),
                      pl.BlockSpec((tq, LANES), qblk(lambda h: h)), pl.BlockSpec((tq, LANES), qblk(lambda h: h)),
                      pl.BlockSpec((None, tq, 1), lambda h, pp, qi_, kj_: (h, qi_[pp], 0))],
            out_specs=[pl.BlockSpec((tk, 2 * LANES), kblk(lambda h: h)), pl.BlockSpec((tk, LANES), kblk(lambda h: h)),
                       pl.BlockSpec((t, 2 * LANES), lambda h, pp, qi_, kj_: (0, h))],
            scratch_shapes=[pltpu.VMEM((tk, 2 * LANES), F32), pltpu.VMEM((tk, LANES), F32)]),
        out_shape=[jax.ShapeDtypeStruct((t, MLA_HEADS * 2 * LANES), BF16), jax.ShapeDtypeStruct((t, MLA_HEADS * LANES), F32),
                   jax.ShapeDtypeStruct((t, MLA_HEADS * 2 * LANES), F32)],
        compiler_params=_params("parallel", "arbitrary"),
    )(qi, kj, qf, kf, kvf, o, do, lse)


def _rope_tables(positions):
    t = positions.shape[0]
    inv = 1.0 / (ROPE_BASE ** (jnp.arange(0, MLA_ROPE, 2, dtype=F32) / MLA_ROPE))
    ang = positions.astype(F32)[:, None] * inv
    cos, sin = jnp.cos(ang), jnp.sin(ang)
    z32, z64 = jnp.zeros((t, 32), F32), jnp.zeros((t, 64), F32)
    cpad = jnp.concatenate([cos, cos, z64], axis=1)
    s_lo = jnp.concatenate([-sin, z32, z64], axis=1)
    s_hi = jnp.concatenate([z32, sin, z64], axis=1)
    return cpad, s_lo, s_hi


def _mla_fwd(h, hn, w, post_g, next_g, rope, tq):
    cpad, s_lo, s_hi = rope
    cin = _mm(hn, w["in"], tm=1024, tn=512, tk=1024, name="mla_in")

    def lat(c, cp, sl, sh, qg, kvg):
        cq, ckv, kr = c[:, :MLA_Q_LORA], c[:, MLA_Q_LORA:MLA_Q_LORA + MLA_KV_LORA], c[:, MLA_Q_LORA + MLA_KV_LORA:]
        return _rms(cq, qg), _rms(ckv, kvg), _rope128(kr, cp, sl, sh)
    cqn, ckvn, kr = _rowwise(lat, [cin, cpad, s_lo, s_hi], [w["q_norm_g"], w["kv_norm_g"]],
                             [(MLA_Q_LORA, BF16), (MLA_KV_LORA, BF16), (LANES, BF16)], name="mla_latent")
    q_raw = _mm(cqn, w["uq"], tm=1024, tn=1024, tk=MLA_Q_LORA, name="mla_uq")

    def rope_q(q, cp, sl, sh):
        pieces = []
        for hd in range(MLA_HEADS):
            pieces.append(q[:, 256 * hd:256 * hd + LANES] * ATTN_QSCALE)
            pieces.append(_rope128(q[:, 256 * hd + LANES:256 * hd + 256], cp, sl, sh) * ATTN_QSCALE)
        return (tuple(pieces),)
    qf = _rowwise(rope_q, [q_raw, cpad, s_lo, s_hi], [], [(4096, BF16)], name="mla_rope_q")
    kvf = _mm(ckvn, w["ukv"], out_dtype=BF16, tm=1024, tn=1024, tk=MLA_KV_LORA, name="mla_ukv")
    t = h.shape[0]
    k_nope = kvf.reshape(t, MLA_HEADS, 2 * LANES)[:, :, :LANES]
    kf = jnp.concatenate([k_nope, jnp.broadcast_to(kr[:, None, :], k_nope.shape)], axis=2).reshape(t, MLA_HEADS * 2 * LANES)
    o, lse = _attn_fwd(qf, kf, kvf, tq=tq, name="mla_attn")
    mixed = _mm(o, w["out"], tm=1024, tn=1024, tk=2048, name="mla_out")
    out, a_next = _postnorm_residual(h, mixed, post_g, 1.0, next_g, "mla_postnorm")
    return out, a_next, (h, hn, cin, cqn, ckvn, qf, kf, kvf, o, lse, mixed)


def _mla_bwd(dh, saved, w, pre_g, post_g, rope, tq):
    cpad, s_lo, s_hi = rope
    h, hn, cin, cqn, ckvn, qf, kf, kvf, o, lse, mixed = saved
    dmixed, d_post = _postnorm_bwd(mixed, dh, post_g, 1.0, "mla_postnorm_bwd")
    do = _mm(dmixed, w["out"], tb=True, out_dtype=BF16, tm=1024, tn=1024, tk=1024, name="mla_out_dx")
    d_out = _mm(o, dmixed, ta=True, tm=1024, tn=1024, tk=2048, name="mla_out_dw")
    dkvf, dkr_heads, dq = _attn_bwd(qf, kf, kvf, o, do, lse, tq=tq, name="mla_attn_bwd")

    def unrope_q(d, cp, sl, sh):
        pieces = []
        for hd in range(MLA_HEADS):
            pieces.append(d[:, 256 * hd:256 * hd + LANES] * ATTN_SCALE)
            pieces.append(_rope128(d[:, 256 * hd + LANES:256 * hd + 256], cp, -sl, -sh) * ATTN_SCALE)
        return (tuple(pieces),)
    dq_raw = _rowwise(unrope_q, [dq, cpad, s_lo, s_hi], [], [(4096, BF16)], name="mla_rope_q_bwd")
    dcqn = _mm(dq_raw, w["uq"], tb=True, tm=1024, tn=256, tk=1024, name="mla_uq_dx")
    d_uq = _mm(cqn, dq_raw, ta=True, tm=256, tn=1024, tk=2048, name="mla_uq_dw")
    dckvn = _mm(dkvf, w["ukv"], tb=True, tm=1024, tn=128, tk=1024, name="mla_ukv_dx")
    d_ukv = _mm(ckvn, dkvf, ta=True, tm=128, tn=1024, tk=2048, name="mla_ukv_dw")

    def lat_bwd(c, dq_, dkv_, dkrh, cp, sl, sh, qg, kvg):
        cq, ckv = c[:, :MLA_Q_LORA], c[:, MLA_Q_LORA:MLA_Q_LORA + MLA_KV_LORA]
        dcq, dqg = _rms_bwd(cq, qg, dq_)
        dckv, dkvg = _rms_bwd(ckv, kvg, dkv_)
        dkr = dkrh[:, :LANES]
        for hd in range(1, MLA_HEADS):
            dkr = dkr + dkrh[:, hd * LANES:(hd + 1) * LANES]
        return (dcq, dckv, _rope128(dkr, cp, -sl, -sh)), _colsum(dqg), _colsum(dkvg)
    dcin, d_qg, d_kvg = _rowwise(lat_bwd, [cin, dcqn, dckvn, dkr_heads, cpad, s_lo, s_hi], [w["q_norm_g"], w["kv_norm_g"]],
                                 [(MLA_IN_PAD, BF16)], [(1, MLA_Q_LORA), (1, MLA_KV_LORA)], name="mla_latent_bwd")
    d_in = _mm(hn, dcin, ta=True, tm=1024, tn=512, tk=2048, name="mla_in_dw")
    dh_in, d_pre = _dx_prenorm_bwd([(dcin, w["in"], 0, MLA_IN_PAD)], h, dh, pre_g, "mla_in_dx_prenorm_bwd")
    return dh_in, dict(w_in=d_in, q_norm_g=d_qg, kv_norm_g=d_kvg, w_uq=d_uq, w_ukv=d_ukv, w_out=d_out,
                       pre_g=d_pre, post_g=d_post)


def _hyb_fwd(h, hn, w, post_g, next_g):
    proj = _mm(hn, w["main"], tm=1024, tn=512, tk=1024, name="hyb_in")
    dtr = _mm(hn, w["dt"], tm=1024, tn=LANES, tk=1024, name="hyb_in_dt")
    ya = _gmlp_fwd(proj, w["gm_w"], w["gm_bt"], w["gm_ln_g"], w["gm_ln_b"], "gmlp")
    xbc = proj[:, 3072:]
    xsh = [_shift_down(xbc, 3 - k) for k in range(3)] + [(proj, SSD_CONV_CH, 2)]
    act = _rowwise(lambda x0, x1, x2, x3, cw, cb: _silu(_conv_pre(x0, x1, x2, x3, cw, cb)), xsh,
                   [w["conv_w"], w["conv_b"]], [(SSD_CONV_CH, F32)], name="ssd_conv")
    dt_pad = _rowwise(lambda d, b: jax.nn.softplus(d + b), [dtr], [w["dt_bias"]], [(LANES, F32)], name="ssd_dt")
    y, states = _ssd_fwd(act, dt_pad, w["a"], w["d"], "ssd_scan")

    def gate_norm(y_, z, ng):
        yg = y_ * _silu(z)
        return ((_rms(yg[:, :512], ng[:, :512]), _rms(yg[:, 512:], ng[:, 512:])),)
    yb = _rowwise(gate_norm, [y, (proj, SSD_INNER, 2)], [w["norm_g"]], [(SSD_INNER, BF16)], name="ssd_gate_norm")
    yab = jnp.concatenate([ya, yb], axis=1)
    mixed = _mm(yab, w["out"], tm=1024, tn=1024, tk=2048, name="hyb_out")
    out, a_next = _postnorm_residual(h, mixed, post_g, 1.0, next_g, "hyb_postnorm")
    return out, a_next, (h, hn, proj, dtr, xsh, act, dt_pad, y, states, yab, mixed)


def _hyb_bwd(dh, saved, w, pre_g, post_g):
    h, hn, proj, dtr, xsh, act, dt_pad, y, states, yab, mixed = saved
    dmixed, d_post = _postnorm_bwd(mixed, dh, post_g, 1.0, "hyb_postnorm_bwd")
    dyab = _mm(dmixed, w["out"], tb=True, tm=1024, tn=1024, tk=1024, name="hyb_out_dx")
    d_out = _mm(yab, dmixed, ta=True, tm=1024, tn=1024, tk=2048, name="hyb_out_dw")

    def gate_norm_bwd(y_, z, d, ng):
        sz = _silu(z)
        yg = y_ * sz
        d_lo, g_lo = _rms_bwd(yg[:, :512], ng[:, :512], d[:, 1024:1536])
        d_hi, g_hi = _rms_bwd(yg[:, 512:], ng[:, 512:], d[:, 1536:])
        dyg = jnp.concatenate([d_lo, d_hi], axis=1)
        return dyg * sz, dyg * y_ * _silu_grad(z), _colsum(jnp.concatenate([g_lo, g_hi], axis=1))
    dy, dz, d_norm = _rowwise(gate_norm_bwd, [y, (proj, SSD_INNER, 2), dyab], [w["norm_g"]], [(SSD_INNER, F32), (SSD_INNER, BF16)],
                              [(1, SSD_INNER)], name="ssd_gate_norm_bwd")
    dact, ddt, da_sum, dd_sum = _ssd_bwd(act, dt_pad, w["a"], w["d"], states, dy, "ssd_scan_bwd")

    def conv_bwd(x0, x1, x2, x3, da_, cw, cb):
        dpre = da_ * _silu_grad(_conv_pre(x0, x1, x2, x3, cw, cb))
        dw = jnp.concatenate([_colsum(dpre * x0), _colsum(dpre * x1), _colsum(dpre * x2), _colsum(dpre * x3)], axis=0)
        return dpre, dw, _colsum(dpre)
    dconv, d_conv_w, d_conv_b = _rowwise(conv_bwd, [*xsh, dact], [w["conv_w"], w["conv_b"]], [(SSD_CONV_CH, F32)],
                                         [(4, SSD_CONV_CH), (1, SSD_CONV_CH)], name="ssd_conv_bwd")
    dsh = [_shift_up(dconv, 3 - k) for k in range(4)]
    dxbc = _rowwise(lambda d0, d1, d2, d3, cw: d0 * cw[0:1] + d1 * cw[1:2] + d2 * cw[2:3] + d3 * cw[3:4], dsh,
                    [w["conv_w"]], [(SSD_CONV_CH, BF16)], name="ssd_conv_dx")

    def dt_bwd(dd, d, b):
        g = dd * jax.nn.sigmoid(d + b)
        g = jnp.where(lax.broadcasted_iota(jnp.int32, g.shape, 1) < SSD_HEADS, g, 0.0)
        return g, _colsum(g)
    ddtr, d_dt_bias = _rowwise(dt_bwd, [ddt, dtr], [w["dt_bias"]], [(LANES, BF16)], [(1, LANES)], name="ssd_dt_bwd")
    duv, d_gm_w, d_gm_b, d_ln_g, d_ln_b = _gmlp_bwd(proj, dyab, w["gm_w"], w["gm_bt"], w["gm_ln_g"], w["gm_ln_b"],
                                                    "gmlp_bwd")
    dproj = jnp.concatenate([duv, dz, dxbc], axis=1)
    d_main = _mm(hn, dproj, ta=True, tm=1024, tn=512, tk=2048, name="hyb_in_dw")
    d_dt = _mm(hn, ddtr, ta=True, tm=1024, tn=LANES, tk=2048, name="hyb_in_dt_dw")
    dh_in, d_pre = _dx_prenorm_bwd([(dproj, w["main"], 0, 1536), (ddtr, w["dt"], 0, LANES)], h, dh, pre_g,
                                   "hyb_in_dx_prenorm_bwd")
    grads = dict(w_in=jnp.concatenate([d_main, d_dt[:, :SSD_HEADS]], axis=1), gm_ln_g=d_ln_g, gm_ln_b=d_ln_b,
                 gm_w_s=d_gm_w, gm_b_s=d_gm_b[:, :, 0], conv_w=d_conv_w, conv_b=d_conv_b,
                 dt_bias=d_dt_bias[:, :SSD_HEADS], a_log=(da_sum * w["a"])[:, :SSD_HEADS], d=dd_sum[:, :SSD_HEADS],
                 norm_g=d_norm, w_out=d_out, pre_g=d_pre, post_g=d_post)
    return dh_in, grads


def _row(v):
    return v.reshape(1, -1).astype(F32)


def _pad_lanes(v, n=LANES):
    v = _row(v)
    return jnp.pad(v, ((0, 0), (0, n - v.shape[1])))


HYB_IN = 4624
HYB_SHARD = HYB_IN // N_DEV
HYB_SHARD_PAD = 640


def _hyb_unblock_matrix():
    n = N_DEV * HYB_SHARD_PAD
    r = lax.broadcasted_iota(jnp.int32, (n, n), 0)
    c = lax.broadcasted_iota(jnp.int32, (n, n), 1)
    j = r % HYB_SHARD_PAD
    return jnp.logical_and(j < HYB_SHARD, c == HYB_SHARD * (r // HYB_SHARD_PAD) + j).astype(BF16)


def _layer_weights(fw, sm, i):
    j = i // 2
    lw = dict(
        ffn1=dict({"in": fw["ffn1_w_in"][i], "down": fw["ffn1_w_down"][i]}),
        ffn2=dict({"in": fw["ffn2_w_in"][i], "down": fw["ffn2_w_down"][i]}),
        ple=dict(gate=fw["ple_w_gate"][i], proj=fw["ple_w_proj"][i]),
    )
    if i % 2 == 0:
        w_in = _mm(fw["hyb_w_in"][j], _hyb_unblock_matrix(), out_dtype=BF16, tm=1024, tn=512, tk=1024, name="hyb_w_unblock")
        causal = jnp.tril(jnp.ones((CHUNK, CHUNK), dtype=bool))
        lw["mix"] = {
            "main": w_in[:, :HYB_MAIN], "dt": w_in[:, HYB_MAIN:HYB_MAIN + LANES],
            "gm_w": jnp.where(causal[None], sm["gm_w_s"][j], 0.0).astype(BF16),
            "gm_bt": jnp.pad(sm["gm_b_s"][j].T, ((0, 0), (0, LANES - GM_HEADS))),
            "gm_ln_g": _row(sm["gm_ln_g"][j]), "gm_ln_b": _row(sm["gm_ln_b"][j]),
            "conv_w": fw["ssd_conv_w"][j], "conv_b": _row(sm["ssd_conv_b"][j]),
            "dt_bias": _pad_lanes(sm["ssd_dt_bias"][j]), "a": _pad_lanes(-jnp.exp(sm["ssd_a_log"][j])),
            "d": _pad_lanes(sm["ssd_d"][j]), "norm_g": _row(sm["ssd_norm_g"][j]), "out": fw["hyb_w_out"][j],
        }
    else:
        uq = fw["mla_w_uq"][j].reshape(MLA_Q_LORA, MLA_HEADS, 192)
        uq = jnp.pad(uq, ((0, 0), (0, 0), (0, 64))).reshape(MLA_Q_LORA, MLA_HEADS * 256)
        lw["mix"] = {
            "in": jnp.pad(fw["mla_w_in"][j], ((0, 0), (0, MLA_IN_PAD - MLA_IN))), "uq": uq, "ukv": fw["mla_w_ukv"][j],
            "out": fw["mla_w_out"][j], "q_norm_g": _row(fw["mla_q_norm_g"][j]), "kv_norm_g": _row(sm["mla_kv_norm_g"][j]),
        }
    return lw


def _device_step(x, p, positions, target, fw, sm):
    t = x.shape[0]
    tq = _pick(t, (1024, 512, 256, 128))
    rope = _rope_tables(positions)
    h = x
    saved, lws = [], []
    a = None
    for i in range(DEPTH):
        lw = _layer_weights(fw, sm, i)
        lws.append(lw)
        after = _row(sm["ffn1_pre_g"][i + 1]) if i + 1 < DEPTH else None
        h, a, s1 = _ffn_fwd(h, a, lw["ffn1"], _row(sm["ffn1_pre_g"][i]), _row(sm["ffn1_post_g"][i]),
                            _row(sm["mix_pre_g"][i]), "ffn")
        if i % 2 == 0:
            h, a, s2 = _hyb_fwd(h, a, lw["mix"], _row(sm["mix_post_g"][i]), _row(sm["ffn2_pre_g"][i]))
        else:
            h, a, s2 = _mla_fwd(h, a, lw["mix"], _row(sm["mix_post_g"][i]), _row(sm["ffn2_pre_g"][i]), rope, tq)
        h, a, s3 = _ffn_fwd(h, a, lw["ffn2"], _row(sm["ffn2_pre_g"][i]), _row(sm["ffn2_post_g"][i]),
                            _row(sm["ple_pre_g"][i]), "ffn")
        h, a, s4 = _ple_fwd(h, a, p[i], lw["ple"], _row(sm["ple_post_g"][i]), after)
        saved.append((s1, s2, s3, s4))

    def loss_fn(y, tg):
        err = y - tg
        return err * (1.0 / D_MODEL), jnp.sum(_colsum(err * err), axis=1, keepdims=True)
    dh, loss_sum = _rowwise(loss_fn, [h, target], [], [(D_MODEL, F32)], [(1, 1)], name="loss")
    loss = loss_sum[0, 0] * (0.5 / D_MODEL)

    per_layer = {n: [None] * DEPTH for n in WEIGHTS if n.startswith(("ffn", "mix", "ple"))}
    per_mixer = {n: [None] * (DEPTH // 2) for n in WEIGHTS if n.startswith(("hyb", "gm", "ssd", "mla"))}
    for i in reversed(range(DEPTH)):
        lw = lws[i]
        s1, s2, s3, s4 = saved[i]
        j = i // 2
        dh, g = _ple_bwd(dh, s4, p[i], lw["ple"], _row(sm["ple_pre_g"][i]), _row(sm["ple_post_g"][i]))
        for k, v in g.items():
            per_layer["ple_" + k][i] = v
        dh, g = _ffn_bwd(dh, s3, lw["ffn2"], _row(sm["ffn2_pre_g"][i]), _row(sm["ffn2_post_g"][i]), "ffn")
        for k, v in g.items():
            per_layer["ffn2_" + k][i] = v
        if i % 2 == 0:
            dh, g = _hyb_bwd(dh, s2, lw["mix"], _row(sm["mix_pre_g"][i]), _row(sm["mix_post_g"][i]))
            names = dict(w_in="hyb_w_in", gm_ln_g="gm_ln_g", gm_ln_b="gm_ln_b", gm_w_s="gm_w_s", gm_b_s="gm_b_s",
                         conv_w="ssd_conv_w", conv_b="ssd_conv_b", dt_bias="ssd_dt_bias", a_log="ssd_a_log", d="ssd_d",
                         norm_g="ssd_norm_g", w_out="hyb_w_out")
        else:
            dh, g = _mla_bwd(dh, s2, lw["mix"], _row(sm["mix_pre_g"][i]), _row(sm["mix_post_g"][i]), rope, tq)
            g["w_in"] = g["w_in"][:, :MLA_IN]
            g["w_uq"] = g["w_uq"].reshape(MLA_Q_LORA, MLA_HEADS, 256)[:, :, :192].reshape(MLA_Q_LORA, MLA_HEADS * 192)
            names = dict(w_in="mla_w_in", q_norm_g="mla_q_norm_g", kv_norm_g="mla_kv_norm_g", w_uq="mla_w_uq",
                         w_ukv="mla_w_ukv", w_out="mla_w_out")
        per_layer["mix_pre_g"][i] = g.pop("pre_g")
        per_layer["mix_post_g"][i] = g.pop("post_g")
        for k, v in g.items():
            per_mixer[names[k]][j] = v
        dh, g = _ffn_bwd(dh, s1, lw["ffn1"], _row(sm["ffn1_pre_g"][i]), _row(sm["ffn1_post_g"][i]), "ffn")
        for k, v in g.items():
            per_layer["ffn1_" + k][i] = v

    return loss, dh, {**per_layer, **per_mixer}


def _stack_layers(parts, shape):
    return jnp.stack(parts, axis=0).reshape(shape)


MESH_AXES = ("x", "y", "c")
EXCHANGE_MAX_COPIES = 56


def _exchange(src, axes, mode, name):
    n = 2 ** len(axes)
    blk = src.shape[-2:]
    flips = [tuple(a for a, bit in zip(axes, np.binary_repr(f, len(axes))) if bit == "1") for f in range(1, n)]
    prefs = tuple(c for c in (16, 8, 4, 2, 1) if c * (n - 1) <= EXCHANGE_MAX_COPIES)
    pieces = _pick(blk[0] // 16, prefs) if blk[0] % 16 == 0 else 1
    rows = blk[0] // pieces

    def index(where):
        idx = 0
        for a in axes:
            idx = idx * 2 + where[a]
        return idx

    me_out = index({a: lax.axis_index(a) for a in MESH_AXES})
    own = lax.dynamic_index_in_dim(src, me_out, 0, keepdims=False) if mode == "a2a" else src
    landing = lax.dynamic_update_index_in_dim(lax.empty((n, *blk), src.dtype), own, me_out, 0)

    def body(src_ref, landing_ref, out_ref, send_sems, recv_sems):
        del landing_ref
        pos = {a: lax.axis_index(a) for a in MESH_AXES}
        me = index(pos)
        copies = []
        for k, flip in enumerate(flips):
            peer = {a: (1 - pos[a]) if a in flip else pos[a] for a in MESH_AXES}
            payload = src_ref.at[index(peer)] if mode == "a2a" else src_ref
            for q in range(pieces):
                part = pl.ds(q * rows, rows)
                cp = pltpu.make_async_remote_copy(
                    src_ref=payload.at[part], dst_ref=out_ref.at[me, part], send_sem=send_sems.at[k * pieces + q],
                    recv_sem=recv_sems.at[k * pieces + q], device_id=(peer["x"], peer["y"], peer["c"]),
                    device_id_type=pl.DeviceIdType.MESH)
                cp.start()
                copies.append(cp)
        for cp in copies:
            cp.wait()

    n_sems = (n - 1) * pieces
    return pl.pallas_call(
        body, name=name, in_specs=[pl.BlockSpec(memory_space=pl.ANY), pl.BlockSpec(memory_space=pl.ANY)],
        out_specs=pl.BlockSpec(memory_space=pl.ANY), out_shape=jax.ShapeDtypeStruct((n, *blk), src.dtype),
        input_output_aliases={1: 0},
        scratch_shapes=[pltpu.SemaphoreType.DMA((n_sems,)), pltpu.SemaphoreType.DMA((n_sems,))],
    )(src, landing)


def _pack_rows(n_elems):
    return -(-n_elems // (16 * PACK_W)) * 16


def _pack(parts, lead=()):
    nl = len(lead)
    rows = []
    for a in parts:
        flat = a.reshape(*lead, -1)
        r = _pack_rows(flat.shape[-1])
        flat = jnp.pad(flat, [(0, 0)] * nl + [(0, r * PACK_W - flat.shape[-1])])
        rows.append(flat.reshape(*lead, r, PACK_W))
    total = sum(r.shape[nl] for r in rows)
    pad = -total % PACK_TM
    if pad:
        rows.append(jnp.zeros((*lead, pad, PACK_W), rows[0].dtype))
    return jnp.concatenate(rows, axis=nl)


def _unpack(buf, shapes, lead=()):
    nl = len(lead)
    out, r0 = [], 0
    for shp in shapes:
        n = int(np.prod(shp))
        r = _pack_rows(n)
        piece = lax.slice_in_dim(buf, r0, r0 + r, axis=nl).reshape(*lead, r * PACK_W)
        out.append(lax.slice_in_dim(piece, 0, n, axis=nl).reshape(*lead, *shp))
        r0 += r
    return out


def _split_for_devices(g, axis):
    if isinstance(g, tuple):
        return jnp.concatenate([_split_for_devices_n(h, axis, N_DEV // len(g)) for h in g], axis=0)
    return _split_for_devices_n(g, axis, N_DEV)


def _split_for_devices_n(g, axis, n):
    shp = g.shape
    g = g.reshape(*shp[:axis], n, shp[axis] // n, *shp[axis + 1:])
    return jnp.moveaxis(g, axis, 0)


def _join_from_devices(parts, axis):
    parts = jnp.moveaxis(parts, 0, axis)
    shp = parts.shape
    return parts.reshape(*shp[:axis], shp[axis] * shp[axis + 1], *shp[axis + 2:])


def _adamw_terms(w, g, m, v):
    m = ADAM_B1 * m + (1.0 - ADAM_B1) * g
    v = ADAM_B2 * v + (1.0 - ADAM_B2) * (g * g)
    m_hat = m / (1.0 - ADAM_B1 ** ADAM_STEP)
    v_hat = v / (1.0 - ADAM_B2 ** ADAM_STEP)
    delta = -ADAM_LR * (m_hat / (jnp.sqrt(v_hat) + ADAM_EPS) + ADAM_WD * w)
    return delta, m, v


def _adamw_packed(w, m, v, partials, n_partials, name):
    def fn(w_, m_, v_, *parts):
        g = parts[0].astype(F32)
        for part in parts[1:]:
            g = g + part.astype(F32)
        return (g,) + _adamw_terms(w_, g, m_, v_)
    return _rowwise(fn, [w, m, v] + [(partials, s) for s in range(n_partials)], [], [(PACK_W, F32)] * 4,
                    tm=PACK_TM, name=name)


def kernel(x, p, positions, ffn1_pre_g, ffn1_w_in, ffn1_w_down, ffn1_post_g, mix_pre_g, mix_post_g, ffn2_pre_g, ffn2_w_in, ffn2_w_down, ffn2_post_g, ple_pre_g, ple_w_gate, ple_w_proj, ple_post_g, hyb_w_in, gm_ln_g, gm_ln_b, gm_w_s, gm_b_s, ssd_conv_w, ssd_conv_b, ssd_dt_bias, ssd_a_log, ssd_d, ssd_norm_g, hyb_w_out, mla_w_in, mla_q_norm_g, mla_kv_norm_g, mla_w_uq, mla_w_ukv, mla_w_out, loss_target, m_ffn1_pre_g, m_ffn1_w_in, m_ffn1_w_down, m_ffn1_post_g, m_mix_pre_g, m_mix_post_g, m_ffn2_pre_g, m_ffn2_w_in, m_ffn2_w_down, m_ffn2_post_g, m_ple_pre_g, m_ple_w_gate, m_ple_w_proj, m_ple_post_g, m_hyb_w_in, m_gm_ln_g, m_gm_ln_b, m_gm_w_s, m_gm_b_s, m_ssd_conv_w, m_ssd_conv_b, m_ssd_dt_bias, m_ssd_a_log, m_ssd_d, m_ssd_norm_g, m_hyb_w_out, m_mla_w_in, m_mla_q_norm_g, m_mla_kv_norm_g, m_mla_w_uq, m_mla_w_ukv, m_mla_w_out, v_ffn1_pre_g, v_ffn1_w_in, v_ffn1_w_down, v_ffn1_post_g, v_mix_pre_g, v_mix_post_g, v_ffn2_pre_g, v_ffn2_w_in, v_ffn2_w_down, v_ffn2_post_g, v_ple_pre_g, v_ple_w_gate, v_ple_w_proj, v_ple_post_g, v_hyb_w_in, v_gm_ln_g, v_gm_ln_b, v_gm_w_s, v_gm_b_s, v_ssd_conv_w, v_ssd_conv_b, v_ssd_dt_bias, v_ssd_a_log, v_ssd_d, v_ssd_norm_g, v_hyb_w_out, v_mla_w_in, v_mla_q_norm_g, v_mla_kv_norm_g, v_mla_w_uq, v_mla_w_ukv, v_mla_w_out):
    given = dict(locals())
    w = {n: given[n] for n in WEIGHTS}
    mom = {n: given["m_" + n] for n in WEIGHTS}
    var = {n: given["v_" + n] for n in WEIGHTS}
    shard_shapes = [w[n].shape for n in SHARDED]
    repl_shapes = [w[n].shape for n in REPLICATED]

    send16 = {n: w[n].astype(BF16) for n in SHARDED_BF16}
    send16["hyb_w_in"] = jnp.pad(send16["hyb_w_in"], ((0, 0), (0, 0), (0, HYB_SHARD_PAD - HYB_SHARD)))
    pack16 = _pack([send16[n] for n in SHARDED_BF16])
    by_chip = _exchange(pack16, ("x", "y"), "gather", "gather_weights_ici")
    by_core = _exchange(by_chip.reshape(-1, PACK_W), ("c",), "gather", "gather_weights_d2d")
    gathered = by_core.reshape(2, 4, -1, PACK_W).transpose(1, 0, 2, 3).reshape(N_DEV, -1, PACK_W)
    fw = {n: _join_from_devices(a, SHARD_AXIS[n])
          for n, a in zip(SHARDED_BF16, _unpack(gathered, [send16[n].shape for n in SHARDED_BF16], (N_DEV,)))}
    small = _exchange(_pack([w[n] for n in SHARDED_F32]), MESH_AXES, "gather", "gather_weights_f32")
    fw.update({n: _join_from_devices(a, SHARD_AXIS[n])
               for n, a in zip(SHARDED_F32, _unpack(small, [w[n].shape for n in SHARDED_F32], (N_DEV,)))})

    loss_local, grad_x, grads = _device_step(x[0], p[:, 0], positions[0], loss_target[0], fw, w)
    loss = lax.psum(loss_local, MESH_AXES)

    per_dev = []
    for n in SHARDED:
        layers = w[n].shape[0]
        whole = (1, *w[n].shape[1:SHARD_AXIS[n]], N_DEV * w[n].shape[SHARD_AXIS[n]], *w[n].shape[SHARD_AXIS[n] + 1:])
        if int(np.prod(w[n].shape[1:])) % (16 * PACK_W) == 0:
            parts = [tuple(h[None] for h in g) if isinstance(g, tuple) else g.reshape(whole) for g in grads[n]]
        else:
            parts = [_stack_layers(grads[n], (layers, *whole[1:]))]
        per_dev.extend(_split_for_devices(g, SHARD_AXIS[n]) for g in parts)
    per_dev = [a.reshape(4, 2, *a.shape[1:]).swapaxes(0, 1) for a in per_dev]
    gpack = _pack(per_dev, (2, 4))
    rows = gpack.shape[2]
    pair = _exchange(gpack.reshape(2, 4 * rows, PACK_W), ("c",), "a2a", "reduce_grads_d2d")
    chip_sum = _rowwise(lambda a, b: a + b, [(pair, 0), (pair, 1)], [], [(PACK_W, BF16)], tm=PACK_TM, name="reduce_grads_pair")
    quads = _exchange(chip_sum.reshape(4, rows, PACK_W), ("x", "y"), "a2a", "reduce_grads_ici")
    g_s, d_s, m_s, v_s = _adamw_packed(_pack([w[n] for n in SHARDED]), _pack([mom[n] for n in SHARDED]),
                                       _pack([var[n] for n in SHARDED]), quads, 4, "adamw_sharded")

    rpack = _pack([_stack_layers(grads[n], w[n].shape) for n in REPLICATED])
    everyone = _exchange(rpack, MESH_AXES, "gather", "gather_small_grads")
    g_r, d_r, m_r, v_r = _adamw_packed(_pack([w[n] for n in REPLICATED]), _pack([mom[n] for n in REPLICATED]),
                                       _pack([var[n] for n in REPLICATED]), everyone, N_DEV, "adamw_replicated")

    outs = []
    for sharded_buf, repl_buf in ((g_s, g_r), (d_s, d_r), (m_s, m_r), (v_s, v_r)):
        vals = dict(zip(SHARDED, _unpack(sharded_buf, shard_shapes)))
        vals.update(zip(REPLICATED, _unpack(repl_buf, repl_shapes)))
        outs.extend(vals[n] for n in WEIGHTS)
    return (loss, grad_x[None], *outs)
```
